```python
import math
import jax
import jax.numpy as jnp
from jax import lax
import numpy as np


D_MODEL = 1024
BATCH = 8
SEQ = 4096
DEPTH = 1

D_MIX = D_MODEL
HEAD_DIM = 64
D_HYENA = D_MIX // 2
D_FOURIER = D_MIX - D_HYENA
N_HYENA_HEADS = D_HYENA // HEAD_DIM
N_FOURIER_GROUPS = D_FOURIER // HEAD_DIM
N_MIX_HEADS = N_HYENA_HEADS + N_FOURIER_GROUPS
D_IN_PROJ = 3 * D_HYENA + D_FOURIER
SHORT_CONV = 3
FILTER_EMB = 33
FILTER_BANDS = (FILTER_EMB - 1) // 2
FILTER_ORDER = 64
N_INNER_MLPS = 2
DECAY_FAST_PCT = 0.3
DECAY_SLOW_PCT = 1.5
DECAY_TARGET = 1e-2
N_GROUPS = 4
EXPERTS_PER_GROUP = 8
N_EXPERTS = N_GROUPS * EXPERTS_PER_GROUP
TOP_K = 2
D_EXPERT = D_MODEL // 2
MOE_BLOCK = 128
EPS = 1e-6

kernel_name = "hyena_fnet_hier_moe_block"


def rmsnorm(x, g):
    xf = x.astype(jnp.float32)
    y = xf * lax.rsqrt(jnp.mean(xf * xf, axis=-1, keepdims=True) + EPS)
    return (y * g.astype(jnp.float32)).astype(x.dtype)


def short_conv_centred(u, w, b):
    L = u.shape[1]
    half = SHORT_CONV // 2
    up = jnp.pad(u, ((0, 0), (half, half), (0, 0)))
    y = b
    for k in range(SHORT_CONV):
        y = y + up[:, k:k + L] * w[k]
    return y


def hyena_filters(L, f_w_in, f_b_in, f_w_mid, f_b_mid, f_freq, f_w_out):
    t = jnp.linspace(0.0, 1.0, L, dtype=jnp.float32)[:, None]
    w = (2.0 * math.pi / L) * jnp.arange(L, dtype=jnp.float32)[:, None]
    f = jnp.linspace(1e-4, FILTER_BANDS - 1, FILTER_BANDS, dtype=jnp.float32)[None, :]
    z = jnp.concatenate([t, jnp.cos(f * w), -jnp.sin(f * w)], axis=-1)
    h = jnp.sin(f_freq[0] * (z @ f_w_in + f_b_in))
    for i in range(N_INNER_MLPS):
        h = jnp.sin(f_freq[i + 1] * (h @ f_w_mid[i] + f_b_mid[i]))
    h = (h @ f_w_out).astype(jnp.float32).reshape(L, 2, D_HYENA)
    max_decay = math.log(DECAY_TARGET) / DECAY_FAST_PCT
    min_decay = math.log(DECAY_TARGET) / DECAY_SLOW_PCT
    deltas = jnp.abs(jnp.linspace(min_decay, max_decay, D_HYENA, dtype=jnp.float32))
    decay = jnp.exp(-t * deltas)
    h = h * decay[:, None, :]
    return h[:, 0], h[:, 1]


def bidirectional_long_conv(u, h_fwd, h_bwd):
    L = u.shape[1]
    k = jnp.concatenate([h_fwd, jnp.zeros_like(h_fwd[:1]), h_bwd[:0:-1]], axis=0)
    k = k / jnp.sum(jnp.abs(k), axis=0, keepdims=True)
    k_f = jnp.fft.rfft(k, n=2 * L, axis=0)
    u_f = jnp.fft.rfft(u, n=2 * L, axis=1)
    return jnp.fft.irfft(u_f * k_f[None], n=2 * L, axis=1)[:, :L]


def hyena_mixer(u, conv_w, conv_b, f_w_in, f_b_in, f_w_mid, f_b_mid, f_freq, f_w_out, f_bias):
    uc = short_conv_centred(u, conv_w, conv_b).astype(jnp.float32)
    x0, x1, v = jnp.split(uc, 3, axis=-1)
    h_fwd, h_bwd = hyena_filters(u.shape[1], f_w_in, f_b_in, f_w_mid, f_b_mid, f_freq, f_w_out)
    z = x1 * v
    z = bidirectional_long_conv(z, h_fwd, h_bwd) + z * f_bias.astype(jnp.float32)
    return x0 * z


def fourier_mixer(u):
    B, L, _ = u.shape
    ug = u.astype(jnp.float32).reshape(B, L, N_FOURIER_GROUPS, HEAD_DIM)
    y = jnp.fft.fftn(ug, axes=(1, 3), norm='ortho').real
    return y.reshape(B, L, D_FOURIER)


def head_rmsnorm(y, g):
    B, L, _ = y.shape
    yh = y.reshape(B, L, N_MIX_HEADS, HEAD_DIM)
    yh = yh * lax.rsqrt(jnp.mean(yh * yh, axis=-1, keepdims=True) + EPS)
    return yh.reshape(B, L, D_MIX) * g.astype(jnp.float32)


def mixer_sublayer(x, norm_g, w_in, conv_w, conv_b, f_w_in, f_b_in, f_w_mid, f_b_mid, f_freq, f_w_out,
                   f_bias, mix_g, w_out):
    h = rmsnorm(x, norm_g)
    proj = h @ w_in
    y_hyena = hyena_mixer(proj[..., :3 * D_HYENA], conv_w, conv_b, f_w_in, f_b_in, f_w_mid, f_b_mid,
                          f_freq, f_w_out, f_bias)
    y_fourier = fourier_mixer(proj[..., 3 * D_HYENA:])
    y = head_rmsnorm(jnp.concatenate([y_hyena, y_fourier], axis=-1), mix_g).astype(x.dtype)
    return x + y @ w_out


def hierarchical_moe(h, w_group, b_group, w_router, b_router, w_gate, w_up, w_down):
    B, L, D = h.shape
    T = B * L
    tok = h.reshape(T, D)
    g_logits = (tok @ w_group).astype(jnp.float32) + b_group.astype(jnp.float32)
    g_sel = jnp.argmax(g_logits, axis=-1)
    p_group = jnp.take_along_axis(jax.nn.softmax(g_logits, axis=-1), g_sel[:, None], axis=-1)[:, 0]
    e_logits = jnp.einsum('td,gde->tge', tok, w_router).astype(jnp.float32) + b_router.astype(jnp.float32)
    e_logits = jnp.take_along_axis(e_logits, g_sel[:, None, None], axis=1)[:, 0]
    top_l, top_i = lax.top_k(e_logits, TOP_K)
    weight = p_group[:, None] * jax.nn.softmax(top_l, axis=-1)
    expert = (g_sel[:, None] * EXPERTS_PER_GROUP + top_i).astype(jnp.int32)

    P = T * TOP_K
    flat_e = expert.reshape(P)
    flat_t = jnp.repeat(jnp.arange(T, dtype=jnp.int32), TOP_K)
    flat_w = weight.reshape(P)
    order = jnp.argsort(flat_e)
    se = flat_e[order]
    counts = jnp.bincount(flat_e, length=N_EXPERTS).astype(jnp.int32)
    starts = jnp.cumsum(counts) - counts
    padded = ((counts + MOE_BLOCK - 1) // MOE_BLOCK) * MOE_BLOCK
    pad_ends = jnp.cumsum(padded)
    pad_starts = pad_ends - padded
    dest = pad_starts[se] + (jnp.arange(P, dtype=jnp.int32) - starts[se])
    n_blocks = -(-P // MOE_BLOCK) + N_EXPERTS
    n_rows = n_blocks * MOE_BLOCK
    row_tok = jnp.zeros((n_rows,), jnp.int32).at[dest].set(flat_t[order])
    row_w = jnp.zeros((n_rows,), jnp.float32).at[dest].set(flat_w[order])
    block_start = jnp.arange(n_blocks, dtype=jnp.int32) * MOE_BLOCK
    block_e = jnp.minimum(jnp.searchsorted(pad_ends, block_start, side='right'), N_EXPERTS - 1)

    def expert_block(args):
        e, rows_tok, rows_w = args
        xb = tok[rows_tok]
        a = xb @ w_gate[e]
        u = xb @ w_up[e]
        y = (jax.nn.silu(a) * u) @ w_down[e]
        return (y * rows_w[:, None].astype(y.dtype)).astype(tok.dtype)

    ys = lax.map(expert_block, (block_e, row_tok.reshape(n_blocks, MOE_BLOCK), row_w.reshape(n_blocks, MOE_BLOCK)))
    out = jnp.zeros((T, D), tok.dtype).at[row_tok].add(ys.reshape(n_rows, D))
    return out.reshape(B, L, D)


def setup_inputs(seed: int = 0) -> dict:
    key = jax.random.key(seed)
    ks = jax.random.split(key, 24)
    f32 = jnp.float32
    n = lambda k, shape, s: (jax.random.normal(k, shape, f32) * s)
    return {
        'x': n(ks[0], (BATCH, SEQ, D_MODEL), 1.0),
        'norm1_g': 1.0 + n(ks[1], (DEPTH, D_MODEL), 0.02),
        'w_in': n(ks[2], (DEPTH, D_MODEL, D_IN_PROJ), D_MODEL ** -0.5),
        'conv_w': n(ks[3], (DEPTH, SHORT_CONV, 3 * D_HYENA), SHORT_CONV ** -0.5),
        'conv_b': n(ks[4], (DEPTH, 3 * D_HYENA), 0.02),
        'f_w_in': n(ks[5], (DEPTH, FILTER_EMB, FILTER_ORDER), FILTER_EMB ** -0.5),
        'f_b_in': n(ks[6], (DEPTH, FILTER_ORDER), 0.1),
        'f_w_mid': n(ks[7], (DEPTH, N_INNER_MLPS, FILTER_ORDER, FILTER_ORDER), FILTER_ORDER ** -0.5),
        'f_b_mid': n(ks[8], (DEPTH, N_INNER_MLPS, FILTER_ORDER), 0.1),
        'f_freq': 1.0 + n(ks[9], (DEPTH, N_INNER_MLPS + 1, FILTER_ORDER), 0.1),
        'f_w_out': n(ks[10], (DEPTH, FILTER_ORDER, 2 * D_HYENA), FILTER_ORDER ** -0.5),
        'f_bias': n(ks[11], (DEPTH, D_HYENA), 1.0),
        'mix_g': 1.0 + n(ks[12], (DEPTH, D_MIX), 0.02),
        'w_out': n(ks[13], (DEPTH, D_MIX, D_MODEL), D_MIX ** -0.5),
        'norm2_g': 1.0 + n(ks[14], (DEPTH, D_MODEL), 0.02),
        'w_group': n(ks[15], (DEPTH, D_MODEL, N_GROUPS), D_MODEL ** -0.5),
        'b_group': n(ks[16], (DEPTH, N_GROUPS), 0.01),
        'w_router': n(ks[17], (DEPTH, N_GROUPS, D_MODEL, EXPERTS_PER_GROUP), D_MODEL ** -0.5),
        'b_router': n(ks[18], (DEPTH, N_GROUPS, EXPERTS_PER_GROUP), 0.01),
        'w_gate': n(ks[19], (DEPTH, N_EXPERTS, D_MODEL, D_EXPERT), D_MODEL ** -0.5),
        'w_up': n(ks[20], (DEPTH, N_EXPERTS, D_MODEL, D_EXPERT), D_MODEL ** -0.5),
        'w_down': n(ks[21], (DEPTH, N_EXPERTS, D_EXPERT, D_MODEL), D_EXPERT ** -0.5),
        'final_g': 1.0 + n(ks[22], (D_MODEL,), 0.02),
    }


def reference(x, norm1_g, w_in, conv_w, conv_b, f_w_in, f_b_in, f_w_mid, f_b_mid, f_freq, f_w_out, f_bias,
              mix_g, w_out, norm2_g, w_group, b_group, w_router, b_router, w_gate, w_up, w_down, final_g):
    for i in range(DEPTH):
        x = mixer_sublayer(x, norm1_g[i], w_in[i], conv_w[i], conv_b[i], f_w_in[i], f_b_in[i], f_w_mid[i],
                           f_b_mid[i], f_freq[i], f_w_out[i], f_bias[i], mix_g[i], w_out[i])
        x = x + hierarchical_moe(rmsnorm(x, norm2_g[i]), w_group[i], b_group[i], w_router[i], b_router[i],
                                 w_gate[i], w_up[i], w_down[i])
    return rmsnorm(x, final_g)
```

```python
import functools
import math

import jax
import jax.numpy as jnp
from jax import lax
from jax.experimental import pallas as pl
from jax.experimental.pallas import tpu as pltpu

HEAD_DIM = 64
N_GROUPS = 4
EXPERTS_PER_GROUP = 8
N_EXPERTS = N_GROUPS * EXPERTS_PER_GROUP
FILTER_BANDS = 16
DECAY_FAST_PCT = 0.3
DECAY_SLOW_PCT = 1.5
DECAY_TARGET = 1e-2
EPS = 1e-6

LANES = 128
SUBLANES = 8
MOE_ROWS = 256
VMEM_LIMIT = 56 * 1024 * 1024

f32 = jnp.float32
bf16 = jnp.bfloat16
HI = lax.Precision.HIGHEST


def _cparams(sem):
    return pltpu.CompilerParams(dimension_semantics=sem, vmem_limit_bytes=VMEM_LIMIT)


def _tile(pref, n):
    return min(pref, n)


def _angle(prod, m):
    return (prod & (m - 1)).astype(f32) * (2.0 * math.pi / m)


def _dft_consts_kernel(fwd_ref, inv_ref, fou_ref, *, L):
    Lh = L // 2
    Lq = Lh // 2
    tr = fwd_ref.shape[1]
    r = lax.broadcasted_iota(jnp.int32, (tr, LANES), 0) + pl.program_id(0) * tr
    d = lax.broadcasted_iota(jnp.int32, (tr, LANES), 1)
    alt_r = (1 - 2 * (r & 1)).astype(f32)
    alt_d = (1 - 2 * (d & 1)).astype(f32)
    ncb = Lh // LANES
    two_r1 = 2 * r + 1

    def col_l(cb):
        c0 = cb * LANES
        q, j0 = c0 // Lq, c0 % Lq
        return 2 * j0 + q

    fams = [
        (r, 1, lambda cb: cb * LANES, L),
        (r, 2, lambda cb: 2 * cb * LANES + 1, 2 * L),
        (two_r1, 1, lambda cb: cb * LANES, 2 * L),
        (r, 2, col_l, Lh),
        (two_r1, 2, col_l, L),
    ]
    for fi, (a, s, b0, m) in enumerate(fams):
        beta = _angle(a * (s * d), m)
        tc, ts = jnp.cos(beta), jnp.sin(beta)
        b0_lane = jnp.zeros((tr, LANES), jnp.int32)
        for cb in range(ncb):
            b0_lane = jnp.where(d == cb, b0(cb), b0_lane)
        alpha = _angle(a * b0_lane, m)
        ac, asn = jnp.cos(alpha), jnp.sin(alpha)
        for cb in range(ncb):
            ca = ac[:, cb:cb + 1]
            sa = asn[:, cb:cb + 1]
            c = ca * tc - sa * ts
            sn = sa * tc + ca * ts
            sl = slice(cb * LANES, (cb + 1) * LANES)
            if fi == 0:
                fwd_ref[0, :, sl] = c.astype(bf16)
                inv_ref[0, :, sl] = c.astype(bf16)
                fwd_ref[1, :, sl] = jnp.where(r == 0, alt_d, sn).astype(bf16)
                nst = -sn
                if cb == 0:
                    nst = jnp.where(d == 0, -alt_r, nst)
                inv_ref[1, :, sl] = nst.astype(bf16)
            elif fi == 1:
                fwd_ref[2, :, sl] = c.astype(bf16)
                fwd_ref[3, :, sl] = jnp.where(r == 0, alt_d, sn).astype(bf16)
            elif fi == 2:
                inv_ref[2, :, sl] = c.astype(bf16)
                nst = -sn
                if cb == 0:
                    nst = jnp.where(d == 0, -alt_r, nst)
                inv_ref[3, :, sl] = nst.astype(bf16)
            elif fi == 3:
                fou_ref[0, :, sl] = c.astype(bf16)
                fou_ref[1, :, sl] = (-sn).astype(bf16)
            else:
                fou_ref[2, :, sl] = c.astype(bf16)
                fou_ref[3, :, sl] = (-sn).astype(bf16)


def _dft_consts(L):
    Lh = L // 2
    tr = _tile(256, Lh)
    shp = jax.ShapeDtypeStruct((4, Lh, Lh), bf16)
    spec = pl.BlockSpec((4, tr, Lh), lambda i: (0, i, 0))
    return pl.pallas_call(
        functools.partial(_dft_consts_kernel, L=L),
        grid=(Lh // tr,),
        out_specs=[spec, spec, spec],
        out_shape=[shp, shp, shp],
        compiler_params=_cparams(("parallel",)),
        name="dft_consts",
    )()


def _prep_win_kernel(w_ref, o_ref, *, C):
    w = w_ref[...]
    wf = w[:, 3 * C:]
    i = lax.broadcasted_iota(jnp.int32, (C, C), 0)
    j = lax.broadcasted_iota(jnp.int32, (C, C), 1)
    same = (i // HEAD_DIM) == (j // HEAD_DIM)
    ang = _angle((i % HEAD_DIM) * (j % HEAD_DIM), HEAD_DIM)
    bc = jnp.where(same, jnp.cos(ang), 0.0)
    bs = jnp.where(same, jnp.sin(ang), 0.0)
    o_ref[:, :3 * C] = w[:, :3 * C].astype(bf16)
    o_ref[:, 3 * C:4 * C] = jnp.dot(wf, bc, precision=HI, preferred_element_type=f32).astype(bf16)
    o_ref[:, 4 * C:] = jnp.dot(wf, bs, precision=HI, preferred_element_type=f32).astype(bf16)


def _prep_win(w_in, C):
    D = w_in.shape[0]
    tr = _tile(256, D)
    return pl.pallas_call(
        functools.partial(_prep_win_kernel, C=C),
        grid=(D // tr,),
        in_specs=[pl.BlockSpec((tr, 4 * C), lambda i: (i, 0))],
        out_specs=pl.BlockSpec((tr, 5 * C), lambda i: (i, 0)),
        out_shape=jax.ShapeDtypeStruct((D, 5 * C), bf16),
        compiler_params=_cparams(("parallel",)),
        name="prep_win",
    )(w_in)


def _filter_kernel(win_ref, bin_ref, wmid_ref, bmid_ref, freq_ref, wout_ref, h_ref, asum_ref, *, L, C):
    tr = h_ref.shape[1]
    step = pl.program_id(0)
    lane = lax.broadcasted_iota(jnp.int32, (tr, LANES), 1)
    row = lax.broadcasted_iota(jnp.int32, (tr, LANES), 0) + step * tr
    band_i = jnp.where(lane <= FILTER_BANDS, lane - 1, lane - 1 - FILTER_BANDS)
    band = 1e-4 + band_i.astype(f32) * ((FILTER_BANDS - 1 - 1e-4) / (FILTER_BANDS - 1))
    ch = lax.broadcasted_iota(jnp.int32, (tr, 2 * C), 1) % C
    max_decay = math.log(DECAY_TARGET) / DECAY_FAST_PCT
    min_decay = math.log(DECAY_TARGET) / DECAY_SLOW_PCT
    delta = jnp.abs(min_decay + ch.astype(f32) * ((max_decay - min_decay) / (C - 1)))
    is_bwd = lax.broadcasted_iota(jnp.int32, (tr, 2 * C), 1) >= C
    row_c = lax.broadcasted_iota(jnp.int32, (tr, 2 * C), 0) + step * tr

    @pl.when(step == 0)
    def _():
        asum_ref[...] = jnp.zeros_like(asum_ref)

    total = jnp.zeros((1, 2 * C), f32)
    for p in range(2):
        pos = (2 * row + p).astype(f32)
        t = pos * (1.0 / (L - 1))
        w = pos * (2.0 * math.pi / L)
        fw = band * w
        z = jnp.where(lane == 0, t,
                      jnp.where(lane <= FILTER_BANDS, jnp.cos(fw),
                                jnp.where(lane <= 2 * FILTER_BANDS, -jnp.sin(fw), 0.0)))
        h = jnp.sin(freq_ref[0:1, :] * (jnp.dot(z, win_ref[...], precision=HI, preferred_element_type=f32)
                                       + bin_ref[...]))
        for i in range(wmid_ref.shape[0]):
            h = jnp.sin(freq_ref[i + 1:i + 2, :]
                        * (jnp.dot(h, wmid_ref[i], precision=HI, preferred_element_type=f32)
                           + bmid_ref[i:i + 1, :]))
        out = jnp.dot(h, wout_ref[...], precision=HI, preferred_element_type=f32)
        tc = (2 * row_c + p).astype(f32) * (1.0 / (L - 1))
        out = out * jnp.exp(-tc * delta)
        if p == 0:
            out = jnp.where(is_bwd & (row_c == 0), 0.0, out)
        h_ref[p] = out.astype(bf16)
        total = total + jnp.sum(jnp.abs(out), axis=0, keepdims=True)
    asum_ref[0:1, :] += total


def _hyena_filters(L, C, f_w_in, f_b_in, f_w_mid, f_b_mid, f_freq, f_w_out):
    Lh = L // 2
    tr = _tile(256, Lh)
    order = f_w_in.shape[1]
    win = jnp.pad(f_w_in, ((0, LANES - f_w_in.shape[0]), (0, 0)))
    full = lambda a: pl.BlockSpec(a.shape, lambda i: (0,) * a.ndim)
    args = (win, f_b_in.reshape(1, order), f_w_mid, f_b_mid, f_freq, f_w_out)
    return pl.pallas_call(
        functools.partial(_filter_kernel, L=L, C=C),
        grid=(Lh // tr,),
        in_specs=[full(a) for a in args],
        out_specs=[pl.BlockSpec((2, tr, 2 * C), lambda i: (0, i, 0)),
                   pl.BlockSpec((SUBLANES, 2 * C), lambda i: (0, 0))],
        out_shape=[jax.ShapeDtypeStruct((2, Lh, 2 * C), bf16),
                   jax.ShapeDtypeStruct((SUBLANES, 2 * C), f32)],
        compiler_params=_cparams(("arbitrary",)),
        name="hyena_filters",
    )(*args)


def _fwd_products(c_ref, x_ref):
    xe, xo = x_ref[0], x_ref[1]
    ce = jnp.dot(c_ref[0], xe, preferred_element_type=f32)
    se = jnp.dot(c_ref[1], xe, preferred_element_type=f32)
    co = jnp.dot(c_ref[2], xo, preferred_element_type=f32)
    so = jnp.dot(c_ref[3], xo, preferred_element_type=f32)
    return ce, se, co, so


def _filter_spectrum_kernel(c_ref, h_ref, asum_ref, k_ref, k0_ref, *, L, C):
    ce, se, co, so = _fwd_products(c_ref, h_ref)
    a = asum_ref[0:1, :]
    scale = 1.0 / ((a[:, :C] + a[:, C:]) * L)
    F = lambda v: v[:, :C]
    G = lambda v: v[:, C:]
    k_ref[0] = (F(ce) + F(co) + G(ce) + G(co)) * scale
    k_ref[1] = (-(F(se) + F(so)) + (G(se) + G(so))) * scale
    k_ref[2] = (F(ce) - F(co) + G(ce) - G(co)) * scale
    k_ref[3] = ((F(se) - F(so)) - (G(se) - G(so))) * scale

    @pl.when(pl.program_id(0) == 0)
    def _():
        k0_ref[...] = jnp.zeros_like(k0_ref)
        k0_ref[0:1, :] = ((F(se) + G(se)) * scale)[0:1, :]
        k0_ref[1:2, :] = ((F(so) - G(so)) * scale)[0:1, :]


def _filter_spectrum(cfwd, hcat, asum, L, C):
    Lh = L // 2
    tm = _tile(256, Lh)
    return pl.pallas_call(
        functools.partial(_filter_spectrum_kernel, L=L, C=C),
        grid=(Lh // tm,),
        in_specs=[pl.BlockSpec((4, tm, Lh), lambda i: (0, i, 0)),
                  pl.BlockSpec((2, Lh, 2 * C), lambda i: (0, 0, 0)),
                  pl.BlockSpec((SUBLANES, 2 * C), lambda i: (0, 0))],
        out_specs=[pl.BlockSpec((4, tm, C), lambda i: (0, i, 0)),
                   pl.BlockSpec((SUBLANES, C), lambda i: (0, 0))],
        out_shape=[jax.ShapeDtypeStruct((4, Lh, C), f32),
                   jax.ShapeDtypeStruct((SUBLANES, C), f32)],
        compiler_params=_cparams(("arbitrary",)),
        name="filter_spectrum",
    )(cfwd, hcat, asum)


def _conv_fwd_kernel(c_ref, z_ref, k_ref, k0_ref, s_ref):
    ce, se, co, so = _fwd_products(c_ref, z_ref)
    zsr, zsi, zdr, zdi = ce + co, -(se + so), ce - co, se - so
    ksr, ksi, kdr, kdi = k_ref[0], k_ref[1], k_ref[2], k_ref[3]
    psr = zsr * ksr - zsi * ksi
    psi = zsr * ksi + zsi * ksr
    pdr = zdr * kdr - zdi * kdi
    pdi = zdr * kdi + zdi * kdr
    gen = (psr + pdr, psi - pdi, psr - pdr, psi + pdi)
    first = pl.program_id(1) == 0

    @pl.when(first)
    def _():
        row0 = lax.broadcasted_iota(jnp.int32, ce.shape, 0) == 0
        p0 = zsr * ksr
        pl_ = zdr * kdr
        ak, bk = k0_ref[0:1, :], k0_ref[1:2, :]
        phr = se * ak - so * bk
        phi = -(se * bk + so * ak)
        spec = (0.5 * (p0 + pl_), -phr, 0.5 * (p0 - pl_), phi)
        for i in range(4):
            s_ref[i] = jnp.where(row0, spec[i], gen[i]).astype(bf16)

    @pl.when(jnp.logical_not(first))
    def _():
        for i in range(4):
            s_ref[i] = gen[i].astype(bf16)


def _conv_fwd(cfwd, z, kspec, k0, C):
    _, Lh, N = z.shape
    tm = _tile(256, Lh)
    return pl.pallas_call(
        _conv_fwd_kernel,
        grid=(N // C, Lh // tm),
        in_specs=[pl.BlockSpec((4, tm, Lh), lambda n, i: (0, i, 0)),
                  pl.BlockSpec((2, Lh, C), lambda n, i: (0, 0, n)),
                  pl.BlockSpec((4, tm, C), lambda n, i: (0, i, 0)),
                  pl.BlockSpec((SUBLANES, C), lambda n, i: (0, 0))],
        out_specs=pl.BlockSpec((4, tm, C), lambda n, i: (0, i, n)),
        out_shape=jax.ShapeDtypeStruct((4, Lh, N), bf16),
        compiler_params=_cparams(("parallel", "arbitrary")),
        name="conv_fwd",
    )(cfwd, z, kspec, k0)


def _pair_products(c_ref, r_ref):
    ev = (jnp.dot(c_ref[0], r_ref[0], preferred_element_type=f32)
          + jnp.dot(c_ref[1], r_ref[1], preferred_element_type=f32))
    od = (jnp.dot(c_ref[2], r_ref[2], preferred_element_type=f32)
          + jnp.dot(c_ref[3], r_ref[3], preferred_element_type=f32))
    return ev, od


def _conv_inv_kernel(c_ref, s_ref, x0_ref, z_ref, fb_ref, y_ref):
    conv = _pair_products(c_ref, s_ref)
    fb = fb_ref[...]
    for p in range(2):
        zp = z_ref[p].astype(f32)
        y_ref[p] = (x0_ref[p].astype(f32) * (conv[p] + zp * fb)).astype(bf16)


def _conv_inv(cinv, s, x0, z, f_bias, C):
    _, Lh, N = s.shape
    tm = _tile(256, Lh)
    return pl.pallas_call(
        _conv_inv_kernel,
        grid=(N // C, Lh // tm),
        in_specs=[pl.BlockSpec((4, tm, Lh), lambda n, i: (0, i, 0)),
                  pl.BlockSpec((4, Lh, C), lambda n, i: (0, 0, n)),
                  pl.BlockSpec((2, tm, C), lambda n, i: (0, i, n)),
                  pl.BlockSpec((2, tm, C), lambda n, i: (0, i, n)),
                  pl.BlockSpec((1, C), lambda n, i: (0, 0))],
        out_specs=pl.BlockSpec((2, tm, C), lambda n, i: (0, i, n)),
        out_shape=jax.ShapeDtypeStruct((2, Lh, N), bf16),
        compiler_params=_cparams(("parallel", "arbitrary")),
        name="conv_inv",
    )(cinv, s, x0, z, f_bias.reshape(1, C))


def _fourier_kernel(c_ref, u_ref, y_ref, *, scale):
    ev, od = _pair_products(c_ref, u_ref)
    y_ref[0] = (ev * scale).astype(bf16)
    y_ref[1] = (od * scale).astype(bf16)


def _fourier_dft(cfou, u4, L, C):
    _, Lh, N = u4.shape
    tm = _tile(256, Lh)
    return pl.pallas_call(
        functools.partial(_fourier_kernel, scale=1.0 / math.sqrt(L * HEAD_DIM)),
        grid=(N // C, Lh // tm),
        in_specs=[pl.BlockSpec((4, tm, Lh), lambda n, i: (0, i, 0)),
                  pl.BlockSpec((4, Lh, C), lambda n, i: (0, 0, n))],
        out_specs=pl.BlockSpec((2, tm, C), lambda n, i: (0, i, n)),
        out_shape=jax.ShapeDtypeStruct((2, Lh, N), bf16),
        compiler_params=_cparams(("parallel", "arbitrary")),
        name="fourier_dft",
    )(cfou, u4)


def _rms(x, g):
    return x * lax.rsqrt(jnp.mean(x * x, axis=-1, keepdims=True) + EPS) * g


def _inproj_kernel(xa_ref, xb_ref, pa_ref, na_ref, pb_ref, nb_ref, g_ref, w_ref, cw_ref, cb_ref,
                   z_ref, x0_ref, u_ref, *, C, D):
    tj = xa_ref.shape[1]
    jt = pl.program_id(1)
    n_half = pl.num_programs(1)
    g = g_ref[...]
    w = w_ref[...]
    cw0, cw1, cw2 = cw_ref[0:1, :], cw_ref[1:2, :], cw_ref[2:3, :]
    cb = cb_ref[...]
    row = lax.broadcasted_iota(jnp.int32, (tj, 3 * C), 0)
    four = []
    halves = ((xa_ref, pa_ref, na_ref, jt == 0, False),
              (xb_ref, pb_ref, nb_ref, False, jt == n_half - 1))
    for h, (x_ref, p_ref, n_ref, at_start, at_end) in enumerate(halves):
        x = x_ref[0]
        lhs = jnp.concatenate([
            _rms(x[:, :D], g).astype(bf16),
            _rms(x[:, D:], g).astype(bf16),
            _rms(p_ref[0][:, D:], g).astype(bf16),
            _rms(n_ref[0][:, :D], g).astype(bf16)], axis=0)
        res = jnp.dot(lhs, w, preferred_element_type=f32)
        pe, po = res[:tj, :3 * C], res[tj:2 * tj, :3 * C]
        prev = res[2 * tj + SUBLANES - 1:2 * tj + SUBLANES, :3 * C]
        nxt = res[2 * tj + SUBLANES:2 * tj + SUBLANES + 1, :3 * C]
        if at_start is not False:
            prev = jnp.where(at_start, 0.0, prev)
        if at_end is not False:
            nxt = jnp.where(at_end, 0.0, nxt)
        po_dn = jnp.where(row == 0, prev, pltpu.roll(po, 1, 0))
        pe_up = jnp.where(row == tj - 1, nxt, pltpu.roll(pe, tj - 1, 0))
        uc = (cb + cw0 * po_dn + cw1 * pe + cw2 * po,
              cb + cw0 * pe + cw1 * po + cw2 * pe_up)
        for p in range(2):
            x0_ref[p, h] = uc[p][:, :C].astype(bf16)
            z_ref[p, h] = (uc[p][:, C:2 * C] * uc[p][:, 2 * C:]).astype(bf16)
        four.append((res[:tj, 3 * C:], res[tj:2 * tj, 3 * C:]))
    for p in range(2):
        fa, fb = four[0][p], four[1][p]
        u_ref[0, p] = (fa[:, :C] + fb[:, :C]).astype(bf16)
        u_ref[1, p] = (fa[:, C:] + fb[:, C:]).astype(bf16)
        u_ref[2, p] = (fa[:, :C] - fb[:, :C]).astype(bf16)
        u_ref[3, p] = (fa[:, C:] - fb[:, C:]).astype(bf16)


def _inproj(x2, norm_g, wcat, conv_w, conv_b, C):
    B, Lh, D2 = x2.shape
    D = D2 // 2
    Lq = Lh // 2
    tj = _tile(256, Lq)
    nq = Lq // tj
    N = B * C
    hb = tj // SUBLANES
    last = Lh // SUBLANES - 1
    main = lambda off: pl.BlockSpec((1, tj, D2), lambda b, j: (b, j + off, 0))
    prev = lambda off: pl.BlockSpec((1, SUBLANES, D2), lambda b, j: (b, jnp.maximum((j + off) * hb - 1, 0), 0))
    nxt = lambda off: pl.BlockSpec((1, SUBLANES, D2), lambda b, j: (b, jnp.minimum((j + off + 1) * hb, last), 0))
    full = lambda a: pl.BlockSpec(a.shape, lambda b, j: (0,) * a.ndim)
    g = norm_g.reshape(1, D)
    cb = conv_b.reshape(1, 3 * C)
    return pl.pallas_call(
        functools.partial(_inproj_kernel, C=C, D=D),
        grid=(B, nq),
        in_specs=[main(0), main(nq), prev(0), nxt(0), prev(nq), nxt(nq),
                  full(g), full(wcat), full(conv_w), full(cb)],
        out_specs=[pl.BlockSpec((2, 2, tj, C), lambda b, j: (0, 0, j, b)),
                   pl.BlockSpec((2, 2, tj, C), lambda b, j: (0, 0, j, b)),
                   pl.BlockSpec((4, 2, tj, C), lambda b, j: (0, 0, j, b))],
        out_shape=[jax.ShapeDtypeStruct((2, 2, Lq, N), bf16),
                   jax.ShapeDtypeStruct((2, 2, Lq, N), bf16),
                   jax.ShapeDtypeStruct((4, 2, Lq, N), bf16)],
        compiler_params=_cparams(("parallel", "arbitrary")),
        name="inproj",
    )(x2, x2, x2, x2, x2, x2, g, wcat, conv_w, cb)


def _pack_bf16_pairs(v):
    k = v.shape[1] // 2
    hi = pltpu.bitcast(v[:, :k].astype(bf16).astype(f32), jnp.uint32)
    lo = pltpu.bitcast(v[:, k:].astype(bf16).astype(f32), jnp.uint32)
    return hi | (lo >> 16)


def _unpack_bf16_pairs(u):
    hi = pltpu.bitcast(u & jnp.uint32(0xFFFF0000), f32)
    lo = pltpu.bitcast(u << 16, f32)
    return jnp.concatenate([hi, lo], axis=1)


def _outproj_router_kernel(yh_ref, yf_ref, x_ref, mg_ref, wo_ref, g2_ref, wr_ref, br_ref,
                           h2_ref, tok_ref, meta_ref, wts_ref, cnt_ref, carry_ref, *, C):
    tj = yh_ref.shape[0]
    step = (pl.program_id(0) * pl.num_programs(1) + pl.program_id(1)) * 2 + pl.program_id(2)

    @pl.when(step == 0)
    def _():
        carry_ref[...] = jnp.zeros_like(carry_ref)

    i = lax.broadcasted_iota(jnp.int32, (LANES, LANES), 0) // HEAD_DIM
    j = lax.broadcasted_iota(jnp.int32, (LANES, LANES), 1) // HEAD_DIM
    avg = jnp.where(i == j, 1.0 / HEAD_DIM, 0.0).astype(bf16)
    parts = []
    for ref in (yh_ref, yf_ref):
        y = ref[...].astype(f32)
        ysq = (y * y).astype(bf16)
        ms = jnp.concatenate([jnp.dot(ysq[:, k:k + LANES], avg, preferred_element_type=f32)
                              for k in range(0, C, LANES)], axis=1)
        parts.append(y * lax.rsqrt(ms + EPS))
    yn = (jnp.concatenate(parts, axis=1) * mg_ref[...]).astype(bf16)
    h2 = x_ref[0] + jnp.dot(yn, wo_ref[...], preferred_element_type=f32)
    h2_ref[...] = h2
    tok = _rms(h2, g2_ref[...])
    tok_ref[...] = _pack_bf16_pairs(tok)

    wr = wr_ref[...]
    wr_hi = wr.astype(bf16)
    wr_lo = (wr - wr_hi.astype(f32)).astype(bf16)
    t_hi = tok.astype(bf16)
    t_lo = (tok - t_hi.astype(f32)).astype(bf16)
    logits = (jnp.dot(t_hi, wr_hi, preferred_element_type=f32)
              + jnp.dot(t_lo, wr_hi, preferred_element_type=f32)
              + jnp.dot(t_hi, wr_lo, preferred_element_type=f32)) + br_ref[...]
    lane = lax.broadcasted_iota(jnp.int32, (tj, LANES), 1)
    neg = jnp.float32(-jnp.inf)
    big = jnp.int32(LANES)

    def first_argmax(vals, mask):
        v = jnp.where(mask, vals, neg)
        m = jnp.max(v, axis=1, keepdims=True)
        idx = jnp.min(jnp.where(mask & (v == m), lane, big), axis=1, keepdims=True)
        return m, idx

    gmask = (lane >= N_EXPERTS) & (lane < N_EXPERTS + N_GROUPS)
    gmax, gidx = first_argmax(logits, gmask)
    gsel = gidx - N_EXPERTS
    p_group = 1.0 / jnp.sum(jnp.where(gmask, jnp.exp(logits - gmax), 0.0), axis=1, keepdims=True)
    emask = (lane >= gsel * EXPERTS_PER_GROUP) & (lane < (gsel + 1) * EXPERTS_PER_GROUP)
    l0, e0 = first_argmax(logits, emask)
    l1, e1 = first_argmax(logits, emask & (lane != e0))
    ex = jnp.exp(l1 - l0)
    w0 = p_group / (1.0 + ex)
    w1 = p_group * ex / (1.0 + ex)

    oh0 = (lane == e0).astype(f32)
    oh1 = (lane == e1).astype(f32)
    both = oh0 + oh1
    ri = lax.broadcasted_iota(jnp.int32, (tj, tj), 0)
    ci = lax.broadcasted_iota(jnp.int32, (tj, tj), 1)
    tri = (ci < ri).astype(bf16)
    before = jnp.dot(tri, both.astype(bf16), preferred_element_type=f32) + carry_ref[0:1, :]
    r0 = jnp.sum(oh0 * before, axis=1, keepdims=True)
    r1 = jnp.sum(oh1 * before, axis=1, keepdims=True)
    carry_ref[0:1, :] = carry_ref[0:1, :] + jnp.sum(both, axis=0, keepdims=True)
    cnt_ref[...] = jnp.broadcast_to(carry_ref[0:1, :], cnt_ref.shape).astype(jnp.int32)

    wts_ref[...] = jnp.where(lane == 0, w0, jnp.where(lane == 1, w1, 0.0))
    cols = jnp.where(lane == 0, e0.astype(f32),
                     jnp.where(lane == 1, e1.astype(f32),
                               jnp.where(lane == 2, r0, jnp.where(lane == 3, r1, 0.0))))
    meta_ref[0] = cols.T[:SUBLANES, :].astype(jnp.int32)


def _outproj_router(yh, yf, x2, mix_g, wo, norm2_g, wr, br, C):
    B, Lh, D2 = x2.shape
    D = D2 // 2
    tj = _tile(256, Lh)
    nj = Lh // tj
    T = B * Lh * 2
    nsteps = T // tj
    lin = lambda b, j, p: (b * nj + j) * 2 + p
    full = lambda a: pl.BlockSpec(a.shape, lambda b, j, p: (0,) * a.ndim)
    mg = mix_g.reshape(1, D)
    g2 = norm2_g.reshape(1, D)
    return pl.pallas_call(
        functools.partial(_outproj_router_kernel, C=C),
        grid=(B, nj, 2),
        in_specs=[pl.BlockSpec((None, tj, C), lambda b, j, p: (p, j, b)),
                  pl.BlockSpec((None, tj, C), lambda b, j, p: (p, j, b)),
                  pl.BlockSpec((1, tj, D), lambda b, j, p: (b, j, p)),
                  full(mg), full(wo), full(g2), full(wr), full(br)],
        out_specs=[pl.BlockSpec((tj, D), lambda b, j, p: (lin(b, j, p), 0)),
                   pl.BlockSpec((tj, D // 2), lambda b, j, p: (lin(b, j, p), 0)),
                   pl.BlockSpec((1, SUBLANES, tj), lambda b, j, p: (lin(b, j, p), 0, 0)),
                   pl.BlockSpec((tj, LANES), lambda b, j, p: (lin(b, j, p), 0)),
                   pl.BlockSpec((SUBLANES, LANES), lambda b, j, p: (0, 0))],
        out_shape=[jax.ShapeDtypeStruct((T, D), f32),
                   jax.ShapeDtypeStruct((T, D // 2), jnp.uint32),
                   jax.ShapeDtypeStruct((nsteps, SUBLANES, tj), jnp.int32),
                   jax.ShapeDtypeStruct((T, LANES), f32),
                   jax.ShapeDtypeStruct((SUBLANES, LANES), jnp.int32)],
        scratch_shapes=[pltpu.VMEM((SUBLANES, LANES), f32)],
        compiler_params=_cparams(("arbitrary", "arbitrary", "arbitrary")),
        name="outproj_router",
    )(yh, yf, x2, mg, wo, g2, wr, br)


def _pad_starts(cnt_ref, start_ref):
    def body(e, acc):
        start_ref[e] = acc
        c = cnt_ref[0, e]
        return acc + ((c + MOE_ROWS - 1) // MOE_ROWS) * MOE_ROWS
    return lax.fori_loop(0, N_EXPERTS, body, jnp.int32(0))


def _dispatch_kernel(meta_ref, cnt_ref, tok_ref, xs_in_ref, xs_ref, be_ref, start_ref, sem):
    del xs_in_ref
    tj = tok_ref.shape[0]
    n_blocks = be_ref.shape[0]

    @pl.when(pl.program_id(0) == 0)
    def _():
        total = _pad_starts(cnt_ref, start_ref)
        used = total // MOE_ROWS

        def fill(i, e):
            def cond(e):
                nxt = jnp.where(e + 1 < N_EXPERTS, start_ref[jnp.minimum(e + 1, N_EXPERTS - 1)], total)
                return (e + 1 < N_EXPERTS) & (i * MOE_ROWS >= nxt)
            e = lax.while_loop(cond, lambda e: e + 1, e)
            be_ref[i] = jnp.where(i < used, e, e + N_EXPERTS)
            return e
        lax.fori_loop(0, n_blocks, fill, jnp.int32(0))

    def copy(r, k):
        dst = start_ref[meta_ref[0, k, r]] + meta_ref[0, 2 + k, r]
        return pltpu.make_async_copy(tok_ref.at[pl.ds(r, 1)], xs_ref.at[pl.ds(dst, 1)], sem)

    def issue(r, c):
        copy(r, 0).start()
        copy(r, 1).start()
        return c

    def drain(r, c):
        copy(r, 0).wait()
        copy(r, 1).wait()
        return c

    lax.fori_loop(0, tj, issue, 0)
    lax.fori_loop(0, tj, drain, 0)


def _dispatch(meta, cnt, tokp, n_rows):
    nsteps, _, tj = meta.shape
    T, Dp = tokp.shape
    n_blocks = n_rows // MOE_ROWS
    xs0 = jnp.zeros((n_rows, Dp), jnp.uint32)
    return pl.pallas_call(
        _dispatch_kernel,
        grid=(nsteps,),
        in_specs=[pl.BlockSpec((1, SUBLANES, tj), lambda i: (i, 0, 0), memory_space=pltpu.SMEM),
                  pl.BlockSpec(memory_space=pltpu.SMEM),
                  pl.BlockSpec((tj, Dp), lambda i: (i, 0)),
                  pl.BlockSpec(memory_space=pl.ANY)],
        out_specs=[pl.BlockSpec(memory_space=pl.ANY),
                   pl.BlockSpec(memory_space=pltpu.SMEM)],
        out_shape=[jax.ShapeDtypeStruct((n_rows, Dp), jnp.uint32),
                   jax.ShapeDtypeStruct((n_blocks,), jnp.int32)],
        scratch_shapes=[pltpu.SMEM((N_EXPERTS,), jnp.int32), pltpu.SemaphoreType.DMA],
        input_output_aliases={3: 0},
        compiler_params=_cparams(("arbitrary",)),
        name="moe_dispatch",
    )(meta, cnt, tokp, xs0)


def _experts_kernel(be_ref, xs_ref, wg_ref, wu_ref, wd_ref, ys_ref, wg_s, wu_s, wd_s):
    i = pl.program_id(0)
    code = be_ref[i]
    prev = be_ref[jnp.maximum(i - 1, 0)]
    valid = code < N_EXPERTS

    @pl.when(valid & ((i == 0) | (code != prev)))
    def _():
        wg_s[...] = wg_ref[...].astype(bf16)
        wu_s[...] = wu_ref[...].astype(bf16)
        wd_s[...] = wd_ref[...].astype(bf16)

    @pl.when(valid)
    def _():
        xb = _unpack_bf16_pairs(xs_ref[...]).astype(bf16)
        a = jnp.dot(xb, wg_s[...], preferred_element_type=f32)
        u = jnp.dot(xb, wu_s[...], preferred_element_type=f32)
        h = (a * jax.nn.sigmoid(a) * u).astype(bf16)
        ys_ref[...] = _pack_bf16_pairs(jnp.dot(h, wd_s[...], preferred_element_type=f32))

    @pl.when(jnp.logical_not(valid))
    def _():
        ys_ref[...] = jnp.zeros_like(ys_ref)


def _experts(block_e, xs, w_gate, w_up, w_down):
    n_rows, Dp = xs.shape
    _, D, De = w_gate.shape
    n_blocks = n_rows // MOE_ROWS
    widx = lambda i, be: (be[i] % N_EXPERTS, 0, 0)
    return pl.pallas_call(
        _experts_kernel,
        grid_spec=pltpu.PrefetchScalarGridSpec(
            num_scalar_prefetch=1,
            grid=(n_blocks,),
            in_specs=[pl.BlockSpec((MOE_ROWS, Dp), lambda i, be: (i, 0)),
                      pl.BlockSpec((None, D, De), widx),
                      pl.BlockSpec((None, D, De), widx),
                      pl.BlockSpec((None, De, D), widx)],
            out_specs=pl.BlockSpec((MOE_ROWS, Dp), lambda i, be: (i, 0)),
            scratch_shapes=[pltpu.VMEM((D, De), bf16), pltpu.VMEM((D, De), bf16), pltpu.VMEM((De, D), bf16)],
        ),
        out_shape=jax.ShapeDtypeStruct((n_rows, Dp), jnp.uint32),
        compiler_params=_cparams(("arbitrary",)),
        name="moe_experts",
    )(block_e, xs, w_gate, w_up, w_down)


def _combine_kernel(meta_ref, cnt_ref, h2_ref, wts_ref, g_ref, ys_ref, o_ref, buf0, buf1, start_ref, sem):
    tj = h2_ref.shape[0]

    @pl.when(pl.program_id(0) == 0)
    def _():
        _pad_starts(cnt_ref, start_ref)

    def copy(r, k):
        src = start_ref[meta_ref[0, k, r]] + meta_ref[0, 2 + k, r]
        buf = buf0 if k == 0 else buf1
        return pltpu.make_async_copy(ys_ref.at[pl.ds(src, 1)], buf.at[pl.ds(r, 1)], sem)

    def issue(r, c):
        copy(r, 0).start()
        copy(r, 1).start()
        return c

    def drain(r, c):
        copy(r, 0).wait()
        copy(r, 1).wait()
        return c

    lax.fori_loop(0, tj, issue, 0)
    lax.fori_loop(0, tj, drain, 0)
    w = wts_ref[...]
    y = (h2_ref[...] + w[:, 0:1] * _unpack_bf16_pairs(buf0[...])
         + w[:, 1:2] * _unpack_bf16_pairs(buf1[...]))
    o_ref[0] = _rms(y, g_ref[...])


def _combine(meta, cnt, h2, wts, final_g, ys, B, Lh):
    nsteps, _, tj = meta.shape
    T, D = h2.shape
    nj = Lh // tj
    g = final_g.reshape(1, D)
    return pl.pallas_call(
        _combine_kernel,
        grid=(nsteps,),
        in_specs=[pl.BlockSpec((1, SUBLANES, tj), lambda i: (i, 0, 0), memory_space=pltpu.SMEM),
                  pl.BlockSpec(memory_space=pltpu.SMEM),
                  pl.BlockSpec((tj, D), lambda i: (i, 0)),
                  pl.BlockSpec((tj, LANES), lambda i: (i, 0)),
                  pl.BlockSpec((1, D), lambda i: (0, 0)),
                  pl.BlockSpec(memory_space=pl.ANY)],
        out_specs=pl.BlockSpec((1, tj, D), lambda i: (i // (2 * nj), (i // 2) % nj, i % 2)),
        out_shape=jax.ShapeDtypeStruct((B, Lh, 2 * D), f32),
        scratch_shapes=[pltpu.VMEM((tj, D // 2), jnp.uint32), pltpu.VMEM((tj, D // 2), jnp.uint32),
                        pltpu.SMEM((N_EXPERTS,), jnp.int32), pltpu.SemaphoreType.DMA],
        compiler_params=_cparams(("arbitrary",)),
        name="moe_combine",
    )(meta, cnt, h2, wts, g, ys)


def kernel(x, norm1_g, w_in, conv_w, conv_b, f_w_in, f_b_in, f_w_mid, f_b_mid, f_freq, f_w_out, f_bias,
           mix_g, w_out, norm2_g, w_group, b_group, w_router, b_router, w_gate, w_up, w_down, final_g):
    B, L, D = x.shape
    depth = norm1_g.shape[0]
    assert depth == 1, "the final RMSNorm is fused into the single layer's MoE combine"
    C = D // 2
    Lh = L // 2
    N = B * C
    T = B * L
    n_rows = T * 2 + N_EXPERTS * MOE_ROWS
    cfwd, cinv, cfou = _dft_consts(L)
    for i in range(depth):
        x2 = x.reshape(B, Lh, 2 * D)
        wcat = _prep_win(w_in[i], C)
        hcat, asum = _hyena_filters(L, C, f_w_in[i], f_b_in[i], f_w_mid[i], f_b_mid[i], f_freq[i], f_w_out[i])
        kspec, k0 = _filter_spectrum(cfwd, hcat, asum, L, C)
        z, x0, u4 = _inproj(x2, norm1_g[i], wcat, conv_w[i], conv_b[i], C)
        z = z.reshape(2, Lh, N)
        x0 = x0.reshape(2, Lh, N)
        u4 = u4.reshape(4, Lh, N)
        s = _conv_fwd(cfwd, z, kspec, k0, C)
        yh = _conv_inv(cinv, s, x0, z, f_bias[i], C)
        yf = _fourier_dft(cfou, u4, L, C)
        wr = jnp.concatenate([jnp.transpose(w_router[i], (1, 0, 2)).reshape(D, N_EXPERTS), w_group[i]], axis=1)
        wr = jnp.pad(wr, ((0, 0), (0, LANES - wr.shape[1])))
        br = jnp.pad(jnp.concatenate([b_router[i].reshape(-1), b_group[i]]), (0, LANES - N_EXPERTS - N_GROUPS))
        h2, tokp, meta, wts, cnt = _outproj_router(yh, yf, x2, mix_g[i], w_out[i].astype(bf16), norm2_g[i],
                                                  wr, br.reshape(1, LANES), C)
        xs, block_e = _dispatch(meta, cnt, tokp, n_rows)
        ys = _experts(block_e, xs, w_gate[i], w_up[i], w_down[i])
        out2 = _combine(meta, cnt, h2, wts, final_g, ys, B, Lh)
        x = out2.reshape(B, L, D)
    return x
```

```python
import functools
import math

import jax
import jax.numpy as jnp
from jax import lax
from jax.experimental import pallas as pl
from jax.experimental.pallas import tpu as pltpu

HEAD_DIM = 64
N_GROUPS = 4
EXPERTS_PER_GROUP = 8
N_EXPERTS = N_GROUPS * EXPERTS_PER_GROUP
FILTER_BANDS = 16
DECAY_FAST_PCT = 0.3
DECAY_SLOW_PCT = 1.5
DECAY_TARGET = 1e-2
EPS = 1e-6

LANES = 128
SUBLANES = 8
MOE_ROWS = 256
VMEM_LIMIT = 56 * 1024 * 1024

f32 = jnp.float32
bf16 = jnp.bfloat16
HI = lax.Precision.HIGHEST


def _cparams(sem):
    return pltpu.CompilerParams(dimension_semantics=sem, vmem_limit_bytes=VMEM_LIMIT)


def _tile(pref, n):
    return min(pref, n)


def _angle(prod, m):
    return (prod & (m - 1)).astype(f32) * (2.0 * math.pi / m)


def _dft_consts_kernel(fwd_ref, inv_ref, fou_ref, *, L):
    Lh = L // 2
    Lq = Lh // 2
    tr = fwd_ref.shape[1]
    r = lax.broadcasted_iota(jnp.int32, (tr, LANES), 0) + pl.program_id(0) * tr
    d = lax.broadcasted_iota(jnp.int32, (tr, LANES), 1)
    alt_r = (1 - 2 * (r & 1)).astype(f32)
    alt_d = (1 - 2 * (d & 1)).astype(f32)
    ncb = Lh // LANES
    two_r1 = 2 * r + 1

    def col_l(cb):
        c0 = cb * LANES
        q, j0 = c0 // Lq, c0 % Lq
        return 2 * j0 + q

    fams = [
        (r, 1, lambda cb: cb * LANES, L),
        (r, 2, lambda cb: 2 * cb * LANES + 1, 2 * L),
        (two_r1, 1, lambda cb: cb * LANES, 2 * L),
        (r, 2, col_l, Lh),
        (two_r1, 2, col_l, L),
    ]
    for fi, (a, s, b0, m) in enumerate(fams):
        beta = _angle(a * (s * d), m)
        tc, ts = jnp.cos(beta), jnp.sin(beta)
        b0_lane = jnp.zeros((tr, LANES), jnp.int32)
        for cb in range(ncb):
            b0_lane = jnp.where(d == cb, b0(cb), b0_lane)
        alpha = _angle(a * b0_lane, m)
        ac, asn = jnp.cos(alpha), jnp.sin(alpha)
        for cb in range(ncb):
            ca = ac[:, cb:cb + 1]
            sa = asn[:, cb:cb + 1]
            c = ca * tc - sa * ts
            sn = sa * tc + ca * ts
            sl = slice(cb * LANES, (cb + 1) * LANES)
            if fi == 0:
                fwd_ref[0, :, sl] = c.astype(bf16)
                inv_ref[0, :, sl] = c.astype(bf16)
                fwd_ref[1, :, sl] = jnp.where(r == 0, alt_d, sn).astype(bf16)
                nst = -sn
                if cb == 0:
                    nst = jnp.where(d == 0, -alt_r, nst)
                inv_ref[1, :, sl] = nst.astype(bf16)
            elif fi == 1:
                fwd_ref[2, :, sl] = c.astype(bf16)
                fwd_ref[3, :, sl] = jnp.where(r == 0, alt_d, sn).astype(bf16)
            elif fi == 2:
                inv_ref[2, :, sl] = c.astype(bf16)
                nst = -sn
                if cb == 0:
                    nst = jnp.where(d == 0, -alt_r, nst)
                inv_ref[3, :, sl] = nst.astype(bf16)
            elif fi == 3:
                fou_ref[0, :, sl] = c.astype(bf16)
                fou_ref[1, :, sl] = (-sn).astype(bf16)
            else:
                fou_ref[2, :, sl] = c.astype(bf16)
                fou_ref[3, :, sl] = (-sn).astype(bf16)


def _dft_consts(L):
    Lh = L // 2
    tr = _tile(256, Lh)
    shp = jax.ShapeDtypeStruct((4, Lh, Lh), bf16)
    spec = pl.BlockSpec((4, tr, Lh), lambda i: (0, i, 0))
    return pl.pallas_call(
        functools.partial(_dft_consts_kernel, L=L),
        grid=(Lh // tr,),
        out_specs=[spec, spec, spec],
        out_shape=[shp, shp, shp],
        compiler_params=_cparams(("parallel",)),
        name="dft_consts",
    )()


def _prep_win_kernel(w_ref, o_ref, *, C):
    w = w_ref[...]
    wf = w[:, 3 * C:]
    i = lax.broadcasted_iota(jnp.int32, (C, C), 0)
    j = lax.broadcasted_iota(jnp.int32, (C, C), 1)
    same = (i // HEAD_DIM) == (j // HEAD_DIM)
    ang = _angle((i % HEAD_DIM) * (j % HEAD_DIM), HEAD_DIM)
    bc = jnp.where(same, jnp.cos(ang), 0.0)
    bs = jnp.where(same, jnp.sin(ang), 0.0)
    o_ref[:, :3 * C] = w[:, :3 * C].astype(bf16)
    o_ref[:, 3 * C:4 * C] = jnp.dot(wf, bc, precision=HI, preferred_element_type=f32).astype(bf16)
    o_ref[:, 4 * C:] = jnp.dot(wf, bs, precision=HI, preferred_element_type=f32).astype(bf16)


def _prep_win(w_in, C):
    D = w_in.shape[0]
    tr = _tile(256, D)
    return pl.pallas_call(
        functools.partial(_prep_win_kernel, C=C),
        grid=(D // tr,),
        in_specs=[pl.BlockSpec((tr, 4 * C), lambda i: (i, 0))],
        out_specs=pl.BlockSpec((tr, 5 * C), lambda i: (i, 0)),
        out_shape=jax.ShapeDtypeStruct((D, 5 * C), bf16),
        compiler_params=_cparams(("parallel",)),
        name="prep_win",
    )(w_in)


def _filter_kernel(win_ref, bin_ref, wmid_ref, bmid_ref, freq_ref, wout_ref, h_ref, asum_ref, *, L, C):
    tr = h_ref.shape[1]
    step = pl.program_id(0)
    lane = lax.broadcasted_iota(jnp.int32, (tr, LANES), 1)
    row = lax.broadcasted_iota(jnp.int32, (tr, LANES), 0) + step * tr
    band_i = jnp.where(lane <= FILTER_BANDS, lane - 1, lane - 1 - FILTER_BANDS)
    band = 1e-4 + band_i.astype(f32) * ((FILTER_BANDS - 1 - 1e-4) / (FILTER_BANDS - 1))
    ch = lax.broadcasted_iota(jnp.int32, (tr, 2 * C), 1) % C
    max_decay = math.log(DECAY_TARGET) / DECAY_FAST_PCT
    min_decay = math.log(DECAY_TARGET) / DECAY_SLOW_PCT
    delta = jnp.abs(min_decay + ch.astype(f32) * ((max_decay - min_decay) / (C - 1)))
    is_bwd = lax.broadcasted_iota(jnp.int32, (tr, 2 * C), 1) >= C
    row_c = lax.broadcasted_iota(jnp.int32, (tr, 2 * C), 0) + step * tr

    @pl.when(step == 0)
    def _():
        asum_ref[...] = jnp.zeros_like(asum_ref)

    total = jnp.zeros((1, 2 * C), f32)
    for p in range(2):
        pos = (2 * row + p).astype(f32)
        t = pos * (1.0 / (L - 1))
        w = pos * (2.0 * math.pi / L)
        fw = band * w
        z = jnp.where(lane == 0, t,
                      jnp.where(lane <= FILTER_BANDS, jnp.cos(fw),
                                jnp.where(lane <= 2 * FILTER_BANDS, -jnp.sin(fw), 0.0)))
        h = jnp.sin(freq_ref[0:1, :] * (jnp.dot(z, win_ref[...], precision=HI, preferred_element_type=f32)
                                       + bin_ref[...]))
        for i in range(wmid_ref.shape[0]):
            h = jnp.sin(freq_ref[i + 1:i + 2, :]
                        * (jnp.dot(h, wmid_ref[i], precision=HI, preferred_element_type=f32)
                           + bmid_ref[i:i + 1, :]))
        out = jnp.dot(h, wout_ref[...], precision=HI, preferred_element_type=f32)
        tc = (2 * row_c + p).astype(f32) * (1.0 / (L - 1))
        out = out * jnp.exp(-tc * delta)
        if p == 0:
            out = jnp.where(is_bwd & (row_c == 0), 0.0, out)
        h_ref[p] = out.astype(bf16)
        total = total + jnp.sum(jnp.abs(out), axis=0, keepdims=True)
    asum_ref[0:1, :] += total


def _hyena_filters(L, C, f_w_in, f_b_in, f_w_mid, f_b_mid, f_freq, f_w_out):
    Lh = L // 2
    tr = _tile(256, Lh)
    order = f_w_in.shape[1]
    win = jnp.pad(f_w_in, ((0, LANES - f_w_in.shape[0]), (0, 0)))
    full = lambda a: pl.BlockSpec(a.shape, lambda i: (0,) * a.ndim)
    args = (win, f_b_in.reshape(1, order), f_w_mid, f_b_mid, f_freq, f_w_out)
    return pl.pallas_call(
        functools.partial(_filter_kernel, L=L, C=C),
        grid=(Lh // tr,),
        in_specs=[full(a) for a in args],
        out_specs=[pl.BlockSpec((2, tr, 2 * C), lambda i: (0, i, 0)),
                   pl.BlockSpec((SUBLANES, 2 * C), lambda i: (0, 0))],
        out_shape=[jax.ShapeDtypeStruct((2, Lh, 2 * C), bf16),
                   jax.ShapeDtypeStruct((SUBLANES, 2 * C), f32)],
        compiler_params=_cparams(("arbitrary",)),
        name="hyena_filters",
    )(*args)


def _fwd_products(c_ref, x_ref):
    xe, xo = x_ref[0], x_ref[1]
    ce = jnp.dot(c_ref[0], xe, preferred_element_type=f32)
    se = jnp.dot(c_ref[1], xe, preferred_element_type=f32)
    co = jnp.dot(c_ref[2], xo, preferred_element_type=f32)
    so = jnp.dot(c_ref[3], xo, preferred_element_type=f32)
    return ce, se, co, so


def _filter_spectrum_kernel(c_ref, h_ref, asum_ref, k_ref, k0_ref, *, L, C):
    ce, se, co, so = _fwd_products(c_ref, h_ref)
    a = asum_ref[0:1, :]
    scale = 1.0 / ((a[:, :C] + a[:, C:]) * L)
    F = lambda v: v[:, :C]
    G = lambda v: v[:, C:]
    k_ref[0] = (F(ce) + F(co) + G(ce) + G(co)) * scale
    k_ref[1] = (-(F(se) + F(so)) + (G(se) + G(so))) * scale
    k_ref[2] = (F(ce) - F(co) + G(ce) - G(co)) * scale
    k_ref[3] = ((F(se) - F(so)) - (G(se) - G(so))) * scale

    @pl.when(pl.program_id(0) == 0)
    def _():
        k0_ref[...] = jnp.zeros_like(k0_ref)
        k0_ref[0:1, :] = ((F(se) + G(se)) * scale)[0:1, :]
        k0_ref[1:2, :] = ((F(so) - G(so)) * scale)[0:1, :]


def _filter_spectrum(cfwd, hcat, asum, L, C):
    Lh = L // 2
    tm = _tile(256, Lh)
    return pl.pallas_call(
        functools.partial(_filter_spectrum_kernel, L=L, C=C),
        grid=(Lh // tm,),
        in_specs=[pl.BlockSpec((4, tm, Lh), lambda i: (0, i, 0)),
                  pl.BlockSpec((2, Lh, 2 * C), lambda i: (0, 0, 0)),
                  pl.BlockSpec((SUBLANES, 2 * C), lambda i: (0, 0))],
        out_specs=[pl.BlockSpec((4, tm, C), lambda i: (0, i, 0)),
                   pl.BlockSpec((SUBLANES, C), lambda i: (0, 0))],
        out_shape=[jax.ShapeDtypeStruct((4, Lh, C), f32),
                   jax.ShapeDtypeStruct((SUBLANES, C), f32)],
        compiler_params=_cparams(("arbitrary",)),
        name="filter_spectrum",
    )(cfwd, hcat, asum)


def _conv_fwd_kernel(c_ref, z_ref, k_ref, k0_ref, s_ref):
    ce, se, co, so = _fwd_products(c_ref, z_ref)
    zsr, zsi, zdr, zdi = ce + co, -(se + so), ce - co, se - so
    ksr, ksi, kdr, kdi = k_ref[0], k_ref[1], k_ref[2], k_ref[3]
    psr = zsr * ksr - zsi * ksi
    psi = zsr * ksi + zsi * ksr
    pdr = zdr * kdr - zdi * kdi
    pdi = zdr * kdi + zdi * kdr
    gen = (psr + pdr, psi - pdi, psr - pdr, psi + pdi)
    first = pl.program_id(1) == 0

    @pl.when(first)
    def _():
        row0 = lax.broadcasted_iota(jnp.int32, ce.shape, 0) == 0
        p0 = zsr * ksr
        pl_ = zdr * kdr
        ak, bk = k0_ref[0:1, :], k0_ref[1:2, :]
        phr = se * ak - so * bk
        phi = -(se * bk + so * ak)
        spec = (0.5 * (p0 + pl_), -phr, 0.5 * (p0 - pl_), phi)
        for i in range(4):
            s_ref[i] = jnp.where(row0, spec[i], gen[i]).astype(bf16)

    @pl.when(jnp.logical_not(first))
    def _():
        for i in range(4):
            s_ref[i] = gen[i].astype(bf16)


def _conv_fwd(cfwd, z, kspec, k0, C):
    _, Lh, N = z.shape
    tm = _tile(256, Lh)
    return pl.pallas_call(
        _conv_fwd_kernel,
        grid=(N // C, Lh // tm),
        in_specs=[pl.BlockSpec((4, tm, Lh), lambda n, i: (0, i, 0)),
                  pl.BlockSpec((2, Lh, C), lambda n, i: (0, 0, n)),
                  pl.BlockSpec((4, tm, C), lambda n, i: (0, i, 0)),
                  pl.BlockSpec((SUBLANES, C), lambda n, i: (0, 0))],
        out_specs=pl.BlockSpec((4, tm, C), lambda n, i: (0, i, n)),
        out_shape=jax.ShapeDtypeStruct((4, Lh, N), bf16),
        compiler_params=_cparams(("parallel", "arbitrary")),
        name="conv_fwd",
    )(cfwd, z, kspec, k0)


def _pair_products(c_ref, r_ref):
    ev = (jnp.dot(c_ref[0], r_ref[0], preferred_element_type=f32)
          + jnp.dot(c_ref[1], r_ref[1], preferred_element_type=f32))
    od = (jnp.dot(c_ref[2], r_ref[2], preferred_element_type=f32)
          + jnp.dot(c_ref[3], r_ref[3], preferred_element_type=f32))
    return ev, od


def _conv_inv_kernel(c_ref, s_ref, x0_ref, z_ref, fb_ref, y_ref):
    conv = _pair_products(c_ref, s_ref)
    fb = fb_ref[...]
    for p in range(2):
        zp = z_ref[p].astype(f32)
        y_ref[p] = (x0_ref[p].astype(f32) * (conv[p] + zp * fb)).astype(bf16)


def _conv_inv(cinv, s, x0, z, f_bias, C):
    _, Lh, N = s.shape
    tm = _tile(256, Lh)
    return pl.pallas_call(
        _conv_inv_kernel,
        grid=(N // C, Lh // tm),
        in_specs=[pl.BlockSpec((4, tm, Lh), lambda n, i: (0, i, 0)),
                  pl.BlockSpec((4, Lh, C), lambda n, i: (0, 0, n)),
                  pl.BlockSpec((2, tm, C), lambda n, i: (0, i, n)),
                  pl.BlockSpec((2, tm, C), lambda n, i: (0, i, n)),
                  pl.BlockSpec((1, C), lambda n, i: (0, 0))],
        out_specs=pl.BlockSpec((2, tm, C), lambda n, i: (0, i, n)),
        out_shape=jax.ShapeDtypeStruct((2, Lh, N), bf16),
        compiler_params=_cparams(("parallel", "arbitrary")),
        name="conv_inv",
    )(cinv, s, x0, z, f_bias.reshape(1, C))


def _fourier_kernel(c_ref, u_ref, y_ref, *, scale):
    ev, od = _pair_products(c_ref, u_ref)
    y_ref[0] = (ev * scale).astype(bf16)
    y_ref[1] = (od * scale).astype(bf16)


def _fourier_dft(cfou, u4, L, C):
    _, Lh, N = u4.shape
    tm = _tile(256, Lh)
    return pl.pallas_call(
        functools.partial(_fourier_kernel, scale=1.0 / math.sqrt(L * HEAD_DIM)),
        grid=(N // C, Lh // tm),
        in_specs=[pl.BlockSpec((4, tm, Lh), lambda n, i: (0, i, 0)),
                  pl.BlockSpec((4, Lh, C), lambda n, i: (0, 0, n))],
        out_specs=pl.BlockSpec((2, tm, C), lambda n, i: (0, i, n)),
        out_shape=jax.ShapeDtypeStruct((2, Lh, N), bf16),
        compiler_params=_cparams(("parallel", "arbitrary")),
        name="fourier_dft",
    )(cfou, u4)


def _rms(x, g):
    return x * lax.rsqrt(jnp.mean(x * x, axis=-1, keepdims=True) + EPS) * g


def _stage_rows(chunks, x):
    for k in range(chunks.shape[0]):
        chunks[k] = x[:, k * LANES:(k + 1) * LANES]


def _parity_rows(chunks, p, n):
    return jnp.concatenate([chunks[k, pl.ds(p, n, stride=2), :] for k in range(chunks.shape[0])], axis=1)


def _inproj_kernel(xa_ref, xb_ref, pa_ref, na_ref, pb_ref, nb_ref, g_ref, w_ref, cw_ref, cb_ref,
                   z_ref, x0_ref, u_ref, stage, *, C, D):
    tj = xa_ref.shape[1] // 2
    jt = pl.program_id(1)
    n_half = pl.num_programs(1)
    g = g_ref[...]
    w = w_ref[...]
    cw0, cw1, cw2 = cw_ref[0:1, :], cw_ref[1:2, :], cw_ref[2:3, :]
    cb = cb_ref[...]
    row = lax.broadcasted_iota(jnp.int32, (tj, 3 * C), 0)
    four = []
    halves = ((xa_ref, pa_ref, na_ref, jt == 0, False),
              (xb_ref, pb_ref, nb_ref, False, jt == n_half - 1))
    for h, (x_ref, p_ref, n_ref, at_start, at_end) in enumerate(halves):
        _stage_rows(stage, x_ref[0])
        lhs = jnp.concatenate([
            _rms(_parity_rows(stage, 0, tj), g).astype(bf16),
            _rms(_parity_rows(stage, 1, tj), g).astype(bf16),
            _rms(p_ref[0], g).astype(bf16),
            _rms(n_ref[0], g).astype(bf16)], axis=0)
        res = jnp.dot(lhs, w, preferred_element_type=f32)
        pe, po = res[:tj, :3 * C], res[tj:2 * tj, :3 * C]
        prev = res[2 * tj + SUBLANES - 1:2 * tj + SUBLANES, :3 * C]
        nxt = res[2 * tj + SUBLANES:2 * tj + SUBLANES + 1, :3 * C]
        if at_start is not False:
            prev = jnp.where(at_start, 0.0, prev)
        if at_end is not False:
            nxt = jnp.where(at_end, 0.0, nxt)
        po_dn = jnp.where(row == 0, prev, pltpu.roll(po, 1, 0))
        pe_up = jnp.where(row == tj - 1, nxt, pltpu.roll(pe, tj - 1, 0))
        uc = (cb + cw0 * po_dn + cw1 * pe + cw2 * po,
              cb + cw0 * pe + cw1 * po + cw2 * pe_up)
        for p in range(2):
            x0_ref[p, h] = uc[p][:, :C].astype(bf16)
            z_ref[p, h] = (uc[p][:, C:2 * C] * uc[p][:, 2 * C:]).astype(bf16)
        four.append((res[:tj, 3 * C:], res[tj:2 * tj, 3 * C:]))
    for p in range(2):
        fa, fb = four[0][p], four[1][p]
        u_ref[0, p] = (fa[:, :C] + fb[:, :C]).astype(bf16)
        u_ref[1, p] = (fa[:, C:] + fb[:, C:]).astype(bf16)
        u_ref[2, p] = (fa[:, :C] - fb[:, :C]).astype(bf16)
        u_ref[3, p] = (fa[:, C:] - fb[:, C:]).astype(bf16)


def _inproj(x, norm_g, wcat, conv_w, conv_b, C):
    B, L, D = x.shape
    Lq = L // 4
    tj = _tile(256, Lq)
    nq = Lq // tj
    N = B * C
    hb = 2 * tj // SUBLANES
    last = L // SUBLANES - 1
    main = lambda off: pl.BlockSpec((1, 2 * tj, D), lambda b, j: (b, j + off, 0))
    prev = lambda off: pl.BlockSpec((1, SUBLANES, D), lambda b, j: (b, jnp.maximum((j + off) * hb - 1, 0), 0))
    nxt = lambda off: pl.BlockSpec((1, SUBLANES, D), lambda b, j: (b, jnp.minimum((j + off + 1) * hb, last), 0))
    full = lambda a: pl.BlockSpec(a.shape, lambda b, j: (0,) * a.ndim)
    g = norm_g.reshape(1, D)
    cb = conv_b.reshape(1, 3 * C)
    return pl.pallas_call(
        functools.partial(_inproj_kernel, C=C, D=D),
        grid=(B, nq),
        in_specs=[main(0), main(nq), prev(0), nxt(0), prev(nq), nxt(nq),
                  full(g), full(wcat), full(conv_w), full(cb)],
        out_specs=[pl.BlockSpec((2, 2, tj, C), lambda b, j: (0, 0, j, b)),
                   pl.BlockSpec((2, 2, tj, C), lambda b, j: (0, 0, j, b)),
                   pl.BlockSpec((4, 2, tj, C), lambda b, j: (0, 0, j, b))],
        out_shape=[jax.ShapeDtypeStruct((2, 2, Lq, N), bf16),
                   jax.ShapeDtypeStruct((2, 2, Lq, N), bf16),
                   jax.ShapeDtypeStruct((4, 2, Lq, N), bf16)],
        scratch_shapes=[pltpu.VMEM((D // LANES, 2 * tj, LANES), f32)],
        compiler_params=_cparams(("parallel", "arbitrary")),
        name="inproj",
    )(x, x, x, x, x, x, g, wcat, conv_w, cb)


def _pack_bf16_pairs(v):
    k = v.shape[1] // 2
    hi = pltpu.bitcast(v[:, :k].astype(bf16).astype(f32), jnp.uint32)
    lo = pltpu.bitcast(v[:, k:].astype(bf16).astype(f32), jnp.uint32)
    return hi | (lo >> 16)


def _unpack_bf16_pairs(u):
    hi = pltpu.bitcast(u & jnp.uint32(0xFFFF0000), f32)
    lo = pltpu.bitcast(u << 16, f32)
    return jnp.concatenate([hi, lo], axis=1)


def _row_tiles_shape(n, D):
    return (n, D // (2 * LANES), LANES)


def _store_row_tiles(ref, packed):
    for s in range(ref.shape[1]):
        ref[:, s, :] = packed[:, s * LANES:(s + 1) * LANES]


def _load_row_tiles(ref):
    return jnp.concatenate([ref[:, s, :] for s in range(ref.shape[1])], axis=1)


def _outproj_router_kernel(yh_ref, yf_ref, x_ref, mg_ref, wo_ref, g2_ref, wr_ref, br_ref,
                           h2_ref, tok_ref, pos_ref, vec_ref, tinfo_ref, cnt_ref, carry_ref, stage, *, C):
    tj = yh_ref.shape[1]
    _stage_rows(stage, x_ref[0])

    @pl.when((pl.program_id(0) == 0) & (pl.program_id(1) == 0))
    def _():
        carry_ref[...] = jnp.zeros_like(carry_ref)

    li = lax.broadcasted_iota(jnp.int32, (LANES, LANES), 0)
    lj = lax.broadcasted_iota(jnp.int32, (LANES, LANES), 1)
    avg = jnp.where(li // HEAD_DIM == lj // HEAD_DIM, 1.0 / HEAD_DIM, 0.0).astype(bf16)
    upper = (li < lj).astype(bf16)
    ri = lax.broadcasted_iota(jnp.int32, (tj, tj), 0)
    ci = lax.broadcasted_iota(jnp.int32, (tj, tj), 1)
    tri = (ci < ri).astype(bf16)
    wr = wr_ref[...]
    wr_hi = wr.astype(bf16)
    wr_lo = (wr - wr_hi.astype(f32)).astype(bf16)
    lane = lax.broadcasted_iota(jnp.int32, (tj, LANES), 1)
    sub = lax.broadcasted_iota(jnp.int32, (SUBLANES, LANES), 0)
    neg = jnp.float32(-jnp.inf)
    big = jnp.int32(LANES)

    def first_argmax(vals, mask):
        v = jnp.where(mask, vals, neg)
        m = jnp.max(v, axis=1, keepdims=True)
        idx = jnp.min(jnp.where(mask & (v == m), lane, big), axis=1, keepdims=True)
        return m, idx

    for p in range(2):
        rows = slice(p * tj, (p + 1) * tj)
        parts = []
        for ref in (yh_ref, yf_ref):
            y = ref[p].astype(f32)
            ysq = (y * y).astype(bf16)
            ms = jnp.concatenate([jnp.dot(ysq[:, k:k + LANES], avg, preferred_element_type=f32)
                                  for k in range(0, C, LANES)], axis=1)
            parts.append(y * lax.rsqrt(ms + EPS))
        yn = (jnp.concatenate(parts, axis=1) * mg_ref[...]).astype(bf16)
        h2 = _parity_rows(stage, p, tj) + jnp.dot(yn, wo_ref[...], preferred_element_type=f32)
        h2_ref[rows, :] = h2
        tok = _rms(h2, g2_ref[...])
        tok_ref[rows, :] = _pack_bf16_pairs(tok)

        t_hi = tok.astype(bf16)
        t_lo = (tok - t_hi.astype(f32)).astype(bf16)
        logits = (jnp.dot(t_hi, wr_hi, preferred_element_type=f32)
                  + jnp.dot(t_lo, wr_hi, preferred_element_type=f32)
                  + jnp.dot(t_hi, wr_lo, preferred_element_type=f32)) + br_ref[...]

        gmask = (lane >= N_EXPERTS) & (lane < N_EXPERTS + N_GROUPS)
        gmax, gidx = first_argmax(logits, gmask)
        gsel = gidx - N_EXPERTS
        p_group = 1.0 / jnp.sum(jnp.where(gmask, jnp.exp(logits - gmax), 0.0), axis=1, keepdims=True)
        emask = (lane >= gsel * EXPERTS_PER_GROUP) & (lane < (gsel + 1) * EXPERTS_PER_GROUP)
        l0, e0 = first_argmax(logits, emask)
        l1, e1 = first_argmax(logits, emask & (lane != e0))
        ex = jnp.exp(l1 - l0)
        w0 = p_group / (1.0 + ex)
        w1 = p_group * ex / (1.0 + ex)

        oh0 = (lane == e0).astype(f32)
        oh1 = (lane == e1).astype(f32)
        both = oh0 + oh1
        n_e = jnp.sum(both, axis=0, keepdims=True)
        n_e8 = jnp.broadcast_to(n_e, (SUBLANES, LANES))
        tile_off = jnp.dot(n_e8.astype(bf16), upper, preferred_element_type=f32)[0:1, :]
        where = jnp.dot(tri, both.astype(bf16), preferred_element_type=f32) + tile_off
        pos0 = jnp.sum(oh0 * where, axis=1, keepdims=True)
        pos1 = jnp.sum(oh1 * where, axis=1, keepdims=True)
        base = carry_ref[0:1, :]
        tinfo_ref[p] = jnp.where(sub == 0, n_e8, jnp.where(sub == 1, base, 0.0)).astype(jnp.int32)
        carry_ref[0:1, :] = base + n_e

        cols = jnp.where(lane == 0, w0, jnp.where(lane == 1, w1,
                         jnp.where(lane == 2, pos0, jnp.where(lane == 3, pos1, 0.0))))
        vec_ref[rows, :] = cols
        pos_ref[p] = cols.T[:SUBLANES, :].astype(jnp.int32)
    cnt_ref[...] = jnp.broadcast_to(carry_ref[0:1, :], cnt_ref.shape).astype(jnp.int32)


def _outproj_router(yh, yf, x, mix_g, wo, norm2_g, wr, br, C):
    B, L, D = x.shape
    Lh = L // 2
    tj = _tile(256, Lh)
    nj = Lh // tj
    T = B * L
    ntiles = T // tj
    lin = lambda b, j: b * nj + j
    full = lambda a: pl.BlockSpec(a.shape, lambda b, j: (0,) * a.ndim)
    mg = mix_g.reshape(1, D)
    g2 = norm2_g.reshape(1, D)
    return pl.pallas_call(
        functools.partial(_outproj_router_kernel, C=C),
        grid=(B, nj),
        in_specs=[pl.BlockSpec((2, tj, C), lambda b, j: (0, j, b)),
                  pl.BlockSpec((2, tj, C), lambda b, j: (0, j, b)),
                  pl.BlockSpec((1, 2 * tj, D), lambda b, j: (b, j, 0)),
                  full(mg), full(wo), full(g2), full(wr), full(br)],
        out_specs=[pl.BlockSpec((2 * tj, D), lambda b, j: (lin(b, j), 0)),
                   pl.BlockSpec((2 * tj, D // 2), lambda b, j: (lin(b, j), 0)),
                   pl.BlockSpec((2, SUBLANES, tj), lambda b, j: (lin(b, j), 0, 0)),
                   pl.BlockSpec((2 * tj, LANES), lambda b, j: (lin(b, j), 0)),
                   pl.BlockSpec((2, SUBLANES, LANES), lambda b, j: (lin(b, j), 0, 0)),
                   pl.BlockSpec((SUBLANES, LANES), lambda b, j: (0, 0))],
        out_shape=[jax.ShapeDtypeStruct((T, D), f32),
                   jax.ShapeDtypeStruct((T, D // 2), jnp.uint32),
                   jax.ShapeDtypeStruct((ntiles, SUBLANES, tj), jnp.int32),
                   jax.ShapeDtypeStruct((T, LANES), f32),
                   jax.ShapeDtypeStruct((ntiles, SUBLANES, LANES), jnp.int32),
                   jax.ShapeDtypeStruct((SUBLANES, LANES), jnp.int32)],
        scratch_shapes=[pltpu.VMEM((SUBLANES, LANES), f32), pltpu.VMEM((D // LANES, 2 * tj, LANES), f32)],
        compiler_params=_cparams(("arbitrary", "arbitrary")),
        name="outproj_router",
    )(yh, yf, x, mg, wo, g2, wr, br)


def _padded(c):
    return ((c + MOE_ROWS - 1) // MOE_ROWS) * MOE_ROWS


def _pad_starts(cnt_ref, start_ref):
    def body(e, acc):
        start_ref[e] = acc
        return acc + _padded(cnt_ref[0, e])
    return lax.fori_loop(0, N_EXPERTS, body, jnp.int32(0))


def _segment_copies(n, src, src_off, dst, dst_off, sem, max_rows, wait, src_advances=True):
    bit = max_rows
    while bit >= 1:
        take = n & bit

        @pl.when(take != 0)
        def _(src_off=src_off, dst_off=dst_off, bit=bit):
            cp = pltpu.make_async_copy(src.at[pl.ds(src_off, bit)], dst.at[pl.ds(dst_off, bit)], sem)
            if wait:
                cp.wait()
            else:
                cp.start()
        if src_advances:
            src_off = src_off + take
        dst_off = dst_off + take
        bit //= 2


def _tile_segments(tinfo_ref, start_ref, local, local_is_src, remote, sem, tj, wait):
    def body(e, off):
        n = tinfo_ref[0, 0, e]
        far = start_ref[e] + tinfo_ref[0, 1, e]
        if local_is_src:
            _segment_copies(n, local, off, remote, far, sem, tj, wait)
        else:
            _segment_copies(n, remote, far, local, off, sem, tj, wait)
        return off + n
    lax.fori_loop(0, N_EXPERTS, body, jnp.int32(0))


def _dispatch_kernel(tinfo_ref, cnt_ref, tok_ref, pos_ref, xs_ref, be_ref, sbuf, zbuf, start_ref, sem, zsem):
    tj = tok_ref.shape[0]
    n_blocks = be_ref.shape[0]

    @pl.when(pl.program_id(0) == 0)
    def _():
        total = _pad_starts(cnt_ref, start_ref)
        used = total // MOE_ROWS

        def fill(i, e):
            def cond(e):
                nxt = jnp.where(e + 1 < N_EXPERTS, start_ref[jnp.minimum(e + 1, N_EXPERTS - 1)], total)
                return (e + 1 < N_EXPERTS) & (i * MOE_ROWS >= nxt)
            e = lax.while_loop(cond, lambda e: e + 1, e)
            be_ref[i] = jnp.where(i < used, e, e + N_EXPERTS)
            return e
        lax.fori_loop(0, n_blocks, fill, jnp.int32(0))

        zbuf[...] = jnp.zeros_like(zbuf)
        zrows = zbuf.shape[0]
        for wait in (False, True):
            def zero_pad(e, c, wait=wait):
                cnt = cnt_ref[0, e]
                _segment_copies(_padded(cnt) - cnt, zbuf, 0, xs_ref, start_ref[e] + cnt, zsem,
                                zrows, wait, src_advances=False)
                return c
            lax.fori_loop(0, N_EXPERTS, zero_pad, 0)

            def zero_tail(i, c, wait=wait):
                cp = pltpu.make_async_copy(zbuf, xs_ref.at[pl.ds(total + i * zrows, zrows)], zsem)
                if wait:
                    cp.wait()
                else:
                    cp.start()
                return c
            lax.fori_loop(0, (n_blocks - used) * (MOE_ROWS // zrows), zero_tail, 0)

    tok = _unpack_bf16_pairs(tok_ref[...]).astype(bf16)
    slot = lax.broadcasted_iota(jnp.int32, (2 * tj, tj), 0)
    perm = ((slot == pos_ref[0, 2:3, :]) | (slot == pos_ref[0, 3:4, :])).astype(bf16)
    _store_row_tiles(sbuf, _pack_bf16_pairs(jnp.dot(perm, tok, preferred_element_type=f32)))
    for wait in (False, True):
        _tile_segments(tinfo_ref, start_ref, sbuf, True, xs_ref, sem, tj, wait)


def _dispatch(tinfo, cnt, tokp, pos, n_rows):
    ntiles, _, tj = pos.shape
    T, Dp = tokp.shape
    D = 2 * Dp
    n_blocks = n_rows // MOE_ROWS
    return pl.pallas_call(
        _dispatch_kernel,
        grid=(ntiles,),
        in_specs=[pl.BlockSpec((1, SUBLANES, LANES), lambda i: (i, 0, 0), memory_space=pltpu.SMEM),
                  pl.BlockSpec(memory_space=pltpu.SMEM),
                  pl.BlockSpec((tj, Dp), lambda i: (i, 0)),
                  pl.BlockSpec((1, SUBLANES, tj), lambda i: (i, 0, 0))],
        out_specs=[pl.BlockSpec(memory_space=pl.ANY),
                   pl.BlockSpec(memory_space=pltpu.SMEM)],
        out_shape=[jax.ShapeDtypeStruct(_row_tiles_shape(n_rows, D), jnp.uint32),
                   jax.ShapeDtypeStruct((n_blocks,), jnp.int32)],
        scratch_shapes=[pltpu.VMEM(_row_tiles_shape(2 * tj, D), jnp.uint32),
                        pltpu.VMEM(_row_tiles_shape(MOE_ROWS // 2, D), jnp.uint32),
                        pltpu.SMEM((N_EXPERTS,), jnp.int32), pltpu.SemaphoreType.DMA,
                        pltpu.SemaphoreType.DMA],
        compiler_params=_cparams(("arbitrary",)),
        name="moe_dispatch",
    )(tinfo, cnt, tokp, pos)


def _experts_kernel(be_ref, xs_ref, wg_ref, wu_ref, wd_ref, ys_ref, wg_s, wu_s, wd_s):
    i = pl.program_id(0)
    code = be_ref[i]
    prev = be_ref[jnp.maximum(i - 1, 0)]
    valid = code < N_EXPERTS

    @pl.when(valid & ((i == 0) | (code != prev)))
    def _():
        wg_s[...] = wg_ref[...].astype(bf16)
        wu_s[...] = wu_ref[...].astype(bf16)
        wd_s[...] = wd_ref[...].astype(bf16)

    @pl.when(valid)
    def _():
        xb = _unpack_bf16_pairs(_load_row_tiles(xs_ref)).astype(bf16)
        a = jnp.dot(xb, wg_s[...], preferred_element_type=f32)
        u = jnp.dot(xb, wu_s[...], preferred_element_type=f32)
        h = (a * jax.nn.sigmoid(a) * u).astype(bf16)
        _store_row_tiles(ys_ref, _pack_bf16_pairs(jnp.dot(h, wd_s[...], preferred_element_type=f32)))

    @pl.when(jnp.logical_not(valid))
    def _():
        ys_ref[...] = jnp.zeros_like(ys_ref)


def _experts(block_e, xs, w_gate, w_up, w_down):
    n_rows = xs.shape[0]
    _, D, De = w_gate.shape
    n_blocks = n_rows // MOE_ROWS
    blk = _row_tiles_shape(MOE_ROWS, D)
    widx = lambda i, be: (be[i] % N_EXPERTS, 0, 0)
    xidx = lambda i, be: (jnp.where(be[i] < N_EXPERTS, i, 0), 0, 0)
    return pl.pallas_call(
        _experts_kernel,
        grid_spec=pltpu.PrefetchScalarGridSpec(
            num_scalar_prefetch=1,
            grid=(n_blocks,),
            in_specs=[pl.BlockSpec(blk, xidx),
                      pl.BlockSpec((None, D, De), widx),
                      pl.BlockSpec((None, D, De), widx),
                      pl.BlockSpec((None, De, D), widx)],
            out_specs=pl.BlockSpec(blk, lambda i, be: (i, 0, 0)),
            scratch_shapes=[pltpu.VMEM((D, De), bf16), pltpu.VMEM((D, De), bf16), pltpu.VMEM((De, D), bf16)],
        ),
        out_shape=jax.ShapeDtypeStruct(xs.shape, jnp.uint32),
        compiler_params=_cparams(("arbitrary",)),
        name="moe_experts",
    )(block_e, xs, w_gate, w_up, w_down)


def _combine_kernel(tinfo_ref, cnt_ref, h2_ref, vec_ref, g_ref, ys_ref, o_ref, gbuf, stage, start_ref, sem):
    tj = h2_ref.shape[0]

    @pl.when(pl.program_id(0) == 0)
    def _():
        _pad_starts(cnt_ref, start_ref)

    for wait in (False, True):
        _tile_segments(tinfo_ref, start_ref, gbuf, False, ys_ref, sem, tj, wait)
    rows = _unpack_bf16_pairs(_load_row_tiles(gbuf)).astype(bf16)
    v = vec_ref[...]
    slot = lax.broadcasted_iota(jnp.int32, (tj, 2 * tj), 1)
    y = h2_ref[...]
    for k in range(2):
        pick = (slot == v[:, 2 + k:3 + k].astype(jnp.int32)).astype(bf16)
        y = y + v[:, k:k + 1] * jnp.dot(pick, rows, preferred_element_type=f32)
    out = _rms(y, g_ref[...])
    parity = pl.program_id(0) % 2
    for p in range(2):
        @pl.when(parity == p)
        def _(p=p):
            for k in range(stage.shape[0]):
                stage[k, pl.ds(p, tj, stride=2), :] = out[:, k * LANES:(k + 1) * LANES]

    @pl.when(parity == 1)
    def _():
        for k in range(stage.shape[0]):
            o_ref[0, :, k * LANES:(k + 1) * LANES] = stage[k]


def _combine(tinfo, cnt, h2, vec, final_g, ys, B, L):
    T, D = h2.shape
    ntiles = tinfo.shape[0]
    tj = T // ntiles
    nj = L // (2 * tj)
    g = final_g.reshape(1, D)
    return pl.pallas_call(
        _combine_kernel,
        grid=(ntiles,),
        in_specs=[pl.BlockSpec((1, SUBLANES, LANES), lambda i: (i, 0, 0), memory_space=pltpu.SMEM),
                  pl.BlockSpec(memory_space=pltpu.SMEM),
                  pl.BlockSpec((tj, D), lambda i: (i, 0)),
                  pl.BlockSpec((tj, LANES), lambda i: (i, 0)),
                  pl.BlockSpec((1, D), lambda i: (0, 0)),
                  pl.BlockSpec(memory_space=pl.ANY)],
        out_specs=pl.BlockSpec((1, 2 * tj, D), lambda i: (i // (2 * nj), (i // 2) % nj, 0)),
        out_shape=jax.ShapeDtypeStruct((B, L, D), f32),
        scratch_shapes=[pltpu.VMEM(_row_tiles_shape(2 * tj, D), jnp.uint32),
                        pltpu.VMEM((D // LANES, 2 * tj, LANES), f32),
                        pltpu.SMEM((N_EXPERTS,), jnp.int32), pltpu.SemaphoreType.DMA],
        compiler_params=_cparams(("arbitrary",)),
        name="moe_combine",
    )(tinfo, cnt, h2, vec, g, ys)


def kernel(x, norm1_g, w_in, conv_w, conv_b, f_w_in, f_b_in, f_w_mid, f_b_mid, f_freq, f_w_out, f_bias,
           mix_g, w_out, norm2_g, w_group, b_group, w_router, b_router, w_gate, w_up, w_down, final_g):
    B, L, D = x.shape
    depth = norm1_g.shape[0]
    assert depth == 1, "the final RMSNorm is fused into the single layer's MoE combine"
    C = D // 2
    Lh = L // 2
    N = B * C
    T = B * L
    n_rows = T * 2 + N_EXPERTS * MOE_ROWS
    cfwd, cinv, cfou = _dft_consts(L)
    i = 0
    wcat = _prep_win(w_in[i], C)
    hcat, asum = _hyena_filters(L, C, f_w_in[i], f_b_in[i], f_w_mid[i], f_b_mid[i], f_freq[i], f_w_out[i])
    kspec, k0 = _filter_spectrum(cfwd, hcat, asum, L, C)
    z, x0, u4 = _inproj(x, norm1_g[i], wcat, conv_w[i], conv_b[i], C)
    z = z.reshape(2, Lh, N)
    x0 = x0.reshape(2, Lh, N)
    u4 = u4.reshape(4, Lh, N)
    s = _conv_fwd(cfwd, z, kspec, k0, C)
    yh = _conv_inv(cinv, s, x0, z, f_bias[i], C)
    yf = _fourier_dft(cfou, u4, L, C)
    wr = jnp.concatenate([jnp.transpose(w_router[i], (1, 0, 2)).reshape(D, N_EXPERTS), w_group[i]], axis=1)
    wr = jnp.pad(wr, ((0, 0), (0, LANES - wr.shape[1])))
    br = jnp.pad(jnp.concatenate([b_router[i].reshape(-1), b_group[i]]), (0, LANES - N_EXPERTS - N_GROUPS))
    h2, tokp, pos, vec, tinfo, cnt = _outproj_router(yh, yf, x, mix_g[i], w_out[i].astype(bf16), norm2_g[i],
                                                     wr, br.reshape(1, LANES), C)
    xs, block_e = _dispatch(tinfo, cnt, tokp, pos, n_rows)
    ys = _experts(block_e, xs, w_gate[i], w_up[i], w_down[i])
    return _combine(tinfo, cnt, h2, vec, final_g, ys, B, L)
```

```python
import functools
import math

import jax
import jax.numpy as jnp
from jax import lax
from jax.experimental import pallas as pl
from jax.experimental.pallas import tpu as pltpu

HEAD_DIM = 64
N_GROUPS = 4
EXPERTS_PER_GROUP = 8
N_EXPERTS = N_GROUPS * EXPERTS_PER_GROUP
FILTER_BANDS = 16
DECAY_FAST_PCT = 0.3
DECAY_SLOW_PCT = 1.5
DECAY_TARGET = 1e-2
EPS = 1e-6

LANES = 128
SUBLANES = 8
MOE_ROWS = 256
VMEM_LIMIT = 56 * 1024 * 1024

f32 = jnp.float32
bf16 = jnp.bfloat16
HI = lax.Precision.HIGHEST


def _cparams(sem):
    return pltpu.CompilerParams(dimension_semantics=sem, vmem_limit_bytes=VMEM_LIMIT)


def _tile(pref, n):
    return min(pref, n)


def _angle(prod, m):
    return (prod & (m - 1)).astype(f32) * (2.0 * math.pi / m)


def _dft_consts_kernel(fwd_ref, inv_ref, fou_ref, *, L):
    Lh = L // 2
    Lq = Lh // 2
    tr = fwd_ref.shape[1]
    r = lax.broadcasted_iota(jnp.int32, (tr, LANES), 0) + pl.program_id(0) * tr
    d = lax.broadcasted_iota(jnp.int32, (tr, LANES), 1)
    alt_r = (1 - 2 * (r & 1)).astype(f32)
    alt_d = (1 - 2 * (d & 1)).astype(f32)
    ncb = Lh // LANES
    two_r1 = 2 * r + 1

    def col_l(cb):
        c0 = cb * LANES
        q, j0 = c0 // Lq, c0 % Lq
        return 2 * j0 + q

    fams = [
        (r, 1, lambda cb: cb * LANES, L),
        (r, 2, lambda cb: 2 * cb * LANES + 1, 2 * L),
        (two_r1, 1, lambda cb: cb * LANES, 2 * L),
        (r, 2, col_l, Lh),
        (two_r1, 2, col_l, L),
    ]
    for fi, (a, s, b0, m) in enumerate(fams):
        beta = _angle(a * (s * d), m)
        tc, ts = jnp.cos(beta), jnp.sin(beta)
        b0_lane = jnp.zeros((tr, LANES), jnp.int32)
        for cb in range(ncb):
            b0_lane = jnp.where(d == cb, b0(cb), b0_lane)
        alpha = _angle(a * b0_lane, m)
        ac, asn = jnp.cos(alpha), jnp.sin(alpha)
        for cb in range(ncb):
            ca = ac[:, cb:cb + 1]
            sa = asn[:, cb:cb + 1]
            c = ca * tc - sa * ts
            sn = sa * tc + ca * ts
            sl = slice(cb * LANES, (cb + 1) * LANES)
            if fi == 0:
                fwd_ref[0, :, sl] = c.astype(bf16)
                inv_ref[0, :, sl] = c.astype(bf16)
                fwd_ref[1, :, sl] = jnp.where(r == 0, alt_d, sn).astype(bf16)
                nst = -sn
                if cb == 0:
                    nst = jnp.where(d == 0, -alt_r, nst)
                inv_ref[1, :, sl] = nst.astype(bf16)
            elif fi == 1:
                fwd_ref[2, :, sl] = c.astype(bf16)
                fwd_ref[3, :, sl] = jnp.where(r == 0, alt_d, sn).astype(bf16)
            elif fi == 2:
                inv_ref[2, :, sl] = c.astype(bf16)
                nst = -sn
                if cb == 0:
                    nst = jnp.where(d == 0, -alt_r, nst)
                inv_ref[3, :, sl] = nst.astype(bf16)
            elif fi == 3:
                fou_ref[0, :, sl] = c.astype(bf16)
                fou_ref[1, :, sl] = (-sn).astype(bf16)
            else:
                fou_ref[2, :, sl] = c.astype(bf16)
                fou_ref[3, :, sl] = (-sn).astype(bf16)


def _dft_consts(L):
    Lh = L // 2
    tr = _tile(256, Lh)
    shp = jax.ShapeDtypeStruct((4, Lh, Lh), bf16)
    spec = pl.BlockSpec((4, tr, Lh), lambda i: (0, i, 0))
    return pl.pallas_call(
        functools.partial(_dft_consts_kernel, L=L),
        grid=(Lh // tr,),
        out_specs=[spec, spec, spec],
        out_shape=[shp, shp, shp],
        compiler_params=_cparams(("parallel",)),
        name="dft_consts",
    )()


def _prep_win_kernel(w_ref, o_ref, *, C):
    w = w_ref[...]
    wf = w[:, 3 * C:]
    i = lax.broadcasted_iota(jnp.int32, (C, C), 0)
    j = lax.broadcasted_iota(jnp.int32, (C, C), 1)
    same = (i // HEAD_DIM) == (j // HEAD_DIM)
    ang = _angle((i % HEAD_DIM) * (j % HEAD_DIM), HEAD_DIM)
    bc = jnp.where(same, jnp.cos(ang), 0.0)
    bs = jnp.where(same, jnp.sin(ang), 0.0)
    o_ref[:, :3 * C] = w[:, :3 * C].astype(bf16)
    o_ref[:, 3 * C:4 * C] = jnp.dot(wf, bc, precision=HI, preferred_element_type=f32).astype(bf16)
    o_ref[:, 4 * C:] = jnp.dot(wf, bs, precision=HI, preferred_element_type=f32).astype(bf16)


def _prep_win(w_in, C):
    D = w_in.shape[0]
    tr = _tile(256, D)
    return pl.pallas_call(
        functools.partial(_prep_win_kernel, C=C),
        grid=(D // tr,),
        in_specs=[pl.BlockSpec((tr, 4 * C), lambda i: (i, 0))],
        out_specs=pl.BlockSpec((tr, 5 * C), lambda i: (i, 0)),
        out_shape=jax.ShapeDtypeStruct((D, 5 * C), bf16),
        compiler_params=_cparams(("parallel",)),
        name="prep_win",
    )(w_in)


def _filter_kernel(win_ref, bin_ref, wmid_ref, bmid_ref, freq_ref, wout_ref, h_ref, asum_ref, *, L, C):
    tr = h_ref.shape[1]
    step = pl.program_id(0)
    lane = lax.broadcasted_iota(jnp.int32, (tr, LANES), 1)
    row = lax.broadcasted_iota(jnp.int32, (tr, LANES), 0) + step * tr
    band_i = jnp.where(lane <= FILTER_BANDS, lane - 1, lane - 1 - FILTER_BANDS)
    band = 1e-4 + band_i.astype(f32) * ((FILTER_BANDS - 1 - 1e-4) / (FILTER_BANDS - 1))
    ch = lax.broadcasted_iota(jnp.int32, (tr, 2 * C), 1) % C
    max_decay = math.log(DECAY_TARGET) / DECAY_FAST_PCT
    min_decay = math.log(DECAY_TARGET) / DECAY_SLOW_PCT
    delta = jnp.abs(min_decay + ch.astype(f32) * ((max_decay - min_decay) / (C - 1)))
    is_bwd = lax.broadcasted_iota(jnp.int32, (tr, 2 * C), 1) >= C
    row_c = lax.broadcasted_iota(jnp.int32, (tr, 2 * C), 0) + step * tr

    @pl.when(step == 0)
    def _():
        asum_ref[...] = jnp.zeros_like(asum_ref)

    total = jnp.zeros((1, 2 * C), f32)
    for p in range(2):
        pos = (2 * row + p).astype(f32)
        t = pos * (1.0 / (L - 1))
        w = pos * (2.0 * math.pi / L)
        fw = band * w
        z = jnp.where(lane == 0, t,
                      jnp.where(lane <= FILTER_BANDS, jnp.cos(fw),
                                jnp.where(lane <= 2 * FILTER_BANDS, -jnp.sin(fw), 0.0)))
        h = jnp.sin(freq_ref[0:1, :] * (jnp.dot(z, win_ref[...], precision=HI, preferred_element_type=f32)
                                       + bin_ref[...]))
        for i in range(wmid_ref.shape[0]):
            h = jnp.sin(freq_ref[i + 1:i + 2, :]
                        * (jnp.dot(h, wmid_ref[i], precision=HI, preferred_element_type=f32)
                           + bmid_ref[i:i + 1, :]))
        out = jnp.dot(h, wout_ref[...], precision=HI, preferred_element_type=f32)
        tc = (2 * row_c + p).astype(f32) * (1.0 / (L - 1))
        out = out * jnp.exp(-tc * delta)
        if p == 0:
            out = jnp.where(is_bwd & (row_c == 0), 0.0, out)
        h_ref[p] = out.astype(bf16)
        total = total + jnp.sum(jnp.abs(out), axis=0, keepdims=True)
    asum_ref[0:1, :] += total


def _hyena_filters(L, C, f_w_in, f_b_in, f_w_mid, f_b_mid, f_freq, f_w_out):
    Lh = L // 2
    tr = _tile(256, Lh)
    order = f_w_in.shape[1]
    win = jnp.pad(f_w_in, ((0, LANES - f_w_in.shape[0]), (0, 0)))
    full = lambda a: pl.BlockSpec(a.shape, lambda i: (0,) * a.ndim)
    args = (win, f_b_in.reshape(1, order), f_w_mid, f_b_mid, f_freq, f_w_out)
    return pl.pallas_call(
        functools.partial(_filter_kernel, L=L, C=C),
        grid=(Lh // tr,),
        in_specs=[full(a) for a in args],
        out_specs=[pl.BlockSpec((2, tr, 2 * C), lambda i: (0, i, 0)),
                   pl.BlockSpec((SUBLANES, 2 * C), lambda i: (0, 0))],
        out_shape=[jax.ShapeDtypeStruct((2, Lh, 2 * C), bf16),
                   jax.ShapeDtypeStruct((SUBLANES, 2 * C), f32)],
        compiler_params=_cparams(("arbitrary",)),
        name="hyena_filters",
    )(*args)


def _fwd_products(c_ref, x_ref):
    xe, xo = x_ref[0], x_ref[1]
    ce = jnp.dot(c_ref[0], xe, preferred_element_type=f32)
    se = jnp.dot(c_ref[1], xe, preferred_element_type=f32)
    co = jnp.dot(c_ref[2], xo, preferred_element_type=f32)
    so = jnp.dot(c_ref[3], xo, preferred_element_type=f32)
    return ce, se, co, so


def _filter_spectrum_kernel(c_ref, h_ref, asum_ref, k_ref, k0_ref, *, L, C):
    ce, se, co, so = _fwd_products(c_ref, h_ref)
    a = asum_ref[0:1, :]
    scale = 1.0 / ((a[:, :C] + a[:, C:]) * L)
    F = lambda v: v[:, :C]
    G = lambda v: v[:, C:]
    k_ref[0] = (F(ce) + F(co) + G(ce) + G(co)) * scale
    k_ref[1] = (-(F(se) + F(so)) + (G(se) + G(so))) * scale
    k_ref[2] = (F(ce) - F(co) + G(ce) - G(co)) * scale
    k_ref[3] = ((F(se) - F(so)) - (G(se) - G(so))) * scale

    @pl.when(pl.program_id(0) == 0)
    def _():
        k0_ref[...] = jnp.zeros_like(k0_ref)
        k0_ref[0:1, :] = ((F(se) + G(se)) * scale)[0:1, :]
        k0_ref[1:2, :] = ((F(so) - G(so)) * scale)[0:1, :]


def _filter_spectrum(cfwd, hcat, asum, L, C):
    Lh = L // 2
    tm = _tile(256, Lh)
    return pl.pallas_call(
        functools.partial(_filter_spectrum_kernel, L=L, C=C),
        grid=(Lh // tm,),
        in_specs=[pl.BlockSpec((4, tm, Lh), lambda i: (0, i, 0)),
                  pl.BlockSpec((2, Lh, 2 * C), lambda i: (0, 0, 0)),
                  pl.BlockSpec((SUBLANES, 2 * C), lambda i: (0, 0))],
        out_specs=[pl.BlockSpec((4, tm, C), lambda i: (0, i, 0)),
                   pl.BlockSpec((SUBLANES, C), lambda i: (0, 0))],
        out_shape=[jax.ShapeDtypeStruct((4, Lh, C), f32),
                   jax.ShapeDtypeStruct((SUBLANES, C), f32)],
        compiler_params=_cparams(("arbitrary",)),
        name="filter_spectrum",
    )(cfwd, hcat, asum)


def _conv_fwd_kernel(c_ref, z_ref, k_ref, k0_ref, s_ref):
    ce, se, co, so = _fwd_products(c_ref, z_ref)
    zsr, zsi, zdr, zdi = ce + co, -(se + so), ce - co, se - so
    ksr, ksi, kdr, kdi = k_ref[0], k_ref[1], k_ref[2], k_ref[3]
    psr = zsr * ksr - zsi * ksi
    psi = zsr * ksi + zsi * ksr
    pdr = zdr * kdr - zdi * kdi
    pdi = zdr * kdi + zdi * kdr
    gen = (psr + pdr, psi - pdi, psr - pdr, psi + pdi)
    first = pl.program_id(1) == 0

    @pl.when(first)
    def _():
        row0 = lax.broadcasted_iota(jnp.int32, ce.shape, 0) == 0
        p0 = zsr * ksr
        pl_ = zdr * kdr
        ak, bk = k0_ref[0:1, :], k0_ref[1:2, :]
        phr = se * ak - so * bk
        phi = -(se * bk + so * ak)
        spec = (0.5 * (p0 + pl_), -phr, 0.5 * (p0 - pl_), phi)
        for i in range(4):
            s_ref[i] = jnp.where(row0, spec[i], gen[i]).astype(bf16)

    @pl.when(jnp.logical_not(first))
    def _():
        for i in range(4):
            s_ref[i] = gen[i].astype(bf16)


def _conv_fwd(cfwd, z, kspec, k0, C):
    _, Lh, N = z.shape
    tm = _tile(256, Lh)
    return pl.pallas_call(
        _conv_fwd_kernel,
        grid=(N // C, Lh // tm),
        in_specs=[pl.BlockSpec((4, tm, Lh), lambda n, i: (0, i, 0)),
                  pl.BlockSpec((2, Lh, C), lambda n, i: (0, 0, n)),
                  pl.BlockSpec((4, tm, C), lambda n, i: (0, i, 0)),
                  pl.BlockSpec((SUBLANES, C), lambda n, i: (0, 0))],
        out_specs=pl.BlockSpec((4, tm, C), lambda n, i: (0, i, n)),
        out_shape=jax.ShapeDtypeStruct((4, Lh, N), bf16),
        compiler_params=_cparams(("parallel", "arbitrary")),
        name="conv_fwd",
    )(cfwd, z, kspec, k0)


def _pair_products(c_ref, r_ref):
    ev = (jnp.dot(c_ref[0], r_ref[0], preferred_element_type=f32)
          + jnp.dot(c_ref[1], r_ref[1], preferred_element_type=f32))
    od = (jnp.dot(c_ref[2], r_ref[2], preferred_element_type=f32)
          + jnp.dot(c_ref[3], r_ref[3], preferred_element_type=f32))
    return ev, od


def _conv_inv_kernel(c_ref, s_ref, x0_ref, z_ref, fb_ref, y_ref):
    conv = _pair_products(c_ref, s_ref)
    fb = fb_ref[...]
    for p in range(2):
        zp = z_ref[p].astype(f32)
        y_ref[p] = (x0_ref[p].astype(f32) * (conv[p] + zp * fb)).astype(bf16)


def _conv_inv(cinv, s, x0, z, f_bias, C):
    _, Lh, N = s.shape
    tm = _tile(256, Lh)
    return pl.pallas_call(
        _conv_inv_kernel,
        grid=(N // C, Lh // tm),
        in_specs=[pl.BlockSpec((4, tm, Lh), lambda n, i: (0, i, 0)),
                  pl.BlockSpec((4, Lh, C), lambda n, i: (0, 0, n)),
                  pl.BlockSpec((2, tm, C), lambda n, i: (0, i, n)),
                  pl.BlockSpec((2, tm, C), lambda n, i: (0, i, n)),
                  pl.BlockSpec((1, C), lambda n, i: (0, 0))],
        out_specs=pl.BlockSpec((2, tm, C), lambda n, i: (0, i, n)),
        out_shape=jax.ShapeDtypeStruct((2, Lh, N), bf16),
        compiler_params=_cparams(("parallel", "arbitrary")),
        name="conv_inv",
    )(cinv, s, x0, z, f_bias.reshape(1, C))


def _fourier_kernel(c_ref, u_ref, y_ref, *, scale):
    ev, od = _pair_products(c_ref, u_ref)
    y_ref[0] = (ev * scale).astype(bf16)
    y_ref[1] = (od * scale).astype(bf16)


def _fourier_dft(cfou, u4, L, C):
    _, Lh, N = u4.shape
    tm = _tile(256, Lh)
    return pl.pallas_call(
        functools.partial(_fourier_kernel, scale=1.0 / math.sqrt(L * HEAD_DIM)),
        grid=(N // C, Lh // tm),
        in_specs=[pl.BlockSpec((4, tm, Lh), lambda n, i: (0, i, 0)),
                  pl.BlockSpec((4, Lh, C), lambda n, i: (0, 0, n))],
        out_specs=pl.BlockSpec((2, tm, C), lambda n, i: (0, i, n)),
        out_shape=jax.ShapeDtypeStruct((2, Lh, N), bf16),
        compiler_params=_cparams(("parallel", "arbitrary")),
        name="fourier_dft",
    )(cfou, u4)


def _rms(x, g):
    return x * lax.rsqrt(jnp.mean(x * x, axis=-1, keepdims=True) + EPS) * g


def _stage_rows(chunks, x):
    for k in range(chunks.shape[0]):
        chunks[k] = x[:, k * LANES:(k + 1) * LANES]


def _parity_rows(chunks, p, n):
    return jnp.concatenate([chunks[k, pl.ds(p, n, stride=2), :] for k in range(chunks.shape[0])], axis=1)


def _inproj_kernel(xa_ref, xb_ref, pa_ref, na_ref, pb_ref, nb_ref, g_ref, w_ref, cw_ref, cb_ref,
                   z_ref, x0_ref, u_ref, stage, *, C, D):
    tj = xa_ref.shape[1] // 2
    jt = pl.program_id(1)
    n_half = pl.num_programs(1)
    g = g_ref[...]
    w = w_ref[...]
    cw0, cw1, cw2 = cw_ref[0:1, :], cw_ref[1:2, :], cw_ref[2:3, :]
    cb = cb_ref[...]
    row = lax.broadcasted_iota(jnp.int32, (tj, 3 * C), 0)
    four = []
    halves = ((xa_ref, pa_ref, na_ref, jt == 0, False),
              (xb_ref, pb_ref, nb_ref, False, jt == n_half - 1))
    for h, (x_ref, p_ref, n_ref, at_start, at_end) in enumerate(halves):
        _stage_rows(stage, x_ref[0])
        lhs = jnp.concatenate([
            _rms(_parity_rows(stage, 0, tj), g).astype(bf16),
            _rms(_parity_rows(stage, 1, tj), g).astype(bf16),
            _rms(p_ref[0], g).astype(bf16),
            _rms(n_ref[0], g).astype(bf16)], axis=0)
        res = jnp.dot(lhs, w, preferred_element_type=f32)
        pe, po = res[:tj, :3 * C], res[tj:2 * tj, :3 * C]
        prev = res[2 * tj + SUBLANES - 1:2 * tj + SUBLANES, :3 * C]
        nxt = res[2 * tj + SUBLANES:2 * tj + SUBLANES + 1, :3 * C]
        if at_start is not False:
            prev = jnp.where(at_start, 0.0, prev)
        if at_end is not False:
            nxt = jnp.where(at_end, 0.0, nxt)
        po_dn = jnp.where(row == 0, prev, pltpu.roll(po, 1, 0))
        pe_up = jnp.where(row == tj - 1, nxt, pltpu.roll(pe, tj - 1, 0))
        uc = (cb + cw0 * po_dn + cw1 * pe + cw2 * po,
              cb + cw0 * pe + cw1 * po + cw2 * pe_up)
        for p in range(2):
            x0_ref[p, h] = uc[p][:, :C].astype(bf16)
            z_ref[p, h] = (uc[p][:, C:2 * C] * uc[p][:, 2 * C:]).astype(bf16)
        four.append((res[:tj, 3 * C:], res[tj:2 * tj, 3 * C:]))
    for p in range(2):
        fa, fb = four[0][p], four[1][p]
        u_ref[0, p] = (fa[:, :C] + fb[:, :C]).astype(bf16)
        u_ref[1, p] = (fa[:, C:] + fb[:, C:]).astype(bf16)
        u_ref[2, p] = (fa[:, :C] - fb[:, :C]).astype(bf16)
        u_ref[3, p] = (fa[:, C:] - fb[:, C:]).astype(bf16)


def _inproj(x, norm_g, wcat, conv_w, conv_b, C):
    B, L, D = x.shape
    Lq = L // 4
    tj = _tile(256, Lq)
    nq = Lq // tj
    N = B * C
    hb = 2 * tj // SUBLANES
    last = L // SUBLANES - 1
    main = lambda off: pl.BlockSpec((1, 2 * tj, D), lambda b, j: (b, j + off, 0))
    prev = lambda off: pl.BlockSpec((1, SUBLANES, D), lambda b, j: (b, jnp.maximum((j + off) * hb - 1, 0), 0))
    nxt = lambda off: pl.BlockSpec((1, SUBLANES, D), lambda b, j: (b, jnp.minimum((j + off + 1) * hb, last), 0))
    full = lambda a: pl.BlockSpec(a.shape, lambda b, j: (0,) * a.ndim)
    g = norm_g.reshape(1, D)
    cb = conv_b.reshape(1, 3 * C)
    return pl.pallas_call(
        functools.partial(_inproj_kernel, C=C, D=D),
        grid=(B, nq),
        in_specs=[main(0), main(nq), prev(0), nxt(0), prev(nq), nxt(nq),
                  full(g), full(wcat), full(conv_w), full(cb)],
        out_specs=[pl.BlockSpec((2, 2, tj, C), lambda b, j: (0, 0, j, b)),
                   pl.BlockSpec((2, 2, tj, C), lambda b, j: (0, 0, j, b)),
                   pl.BlockSpec((4, 2, tj, C), lambda b, j: (0, 0, j, b))],
        out_shape=[jax.ShapeDtypeStruct((2, 2, Lq, N), bf16),
                   jax.ShapeDtypeStruct((2, 2, Lq, N), bf16),
                   jax.ShapeDtypeStruct((4, 2, Lq, N), bf16)],
        scratch_shapes=[pltpu.VMEM((D // LANES, 2 * tj, LANES), f32)],
        compiler_params=_cparams(("parallel", "arbitrary")),
        name="inproj",
    )(x, x, x, x, x, x, g, wcat, conv_w, cb)


def _pack_bf16_pairs(v, is_bf16_exact=False):
    k = v.shape[1] // 2
    rounded = v if is_bf16_exact else v.astype(bf16).astype(f32)
    hi = pltpu.bitcast(rounded[:, :k], jnp.uint32)
    lo = pltpu.bitcast(rounded[:, k:], jnp.uint32)
    return hi | (lo >> 16)


def _unpack_bf16_pairs(u):
    hi = pltpu.bitcast(u & jnp.uint32(0xFFFF0000), f32)
    lo = pltpu.bitcast(u << 16, f32)
    return jnp.concatenate([hi, lo], axis=1)


def _row_tiles_shape(n, D):
    return (n, D // (2 * LANES), LANES)


def _store_row_tiles(ref, packed):
    n, per_row, _ = ref.shape
    flat = ref.reshape(n * per_row, LANES)
    for s in range(per_row):
        flat[pl.ds(s, n, stride=per_row), :] = packed[:, s * LANES:(s + 1) * LANES]


def _load_row_tiles(ref):
    n, per_row, _ = ref.shape
    flat = ref.reshape(n * per_row, LANES)
    return jnp.concatenate([flat[pl.ds(s, n, stride=per_row), :] for s in range(per_row)], axis=1)


def _outproj_router_kernel(yh_ref, yf_ref, x_ref, mg_ref, wo_ref, g2_ref, wr_ref, br_ref,
                           h2_ref, tok_ref, pos_ref, vec_ref, tinfo_ref, cnt_ref, carry_ref, stage, *, C):
    tj = yh_ref.shape[1]
    _stage_rows(stage, x_ref[0])

    @pl.when((pl.program_id(0) == 0) & (pl.program_id(1) == 0))
    def _():
        carry_ref[...] = jnp.zeros_like(carry_ref)

    li = lax.broadcasted_iota(jnp.int32, (LANES, LANES), 0)
    lj = lax.broadcasted_iota(jnp.int32, (LANES, LANES), 1)
    avg = jnp.where(li // HEAD_DIM == lj // HEAD_DIM, 1.0 / HEAD_DIM, 0.0).astype(bf16)
    upper = (li < lj).astype(bf16)
    ri = lax.broadcasted_iota(jnp.int32, (tj, tj), 0)
    ci = lax.broadcasted_iota(jnp.int32, (tj, tj), 1)
    tri = (ci < ri).astype(bf16)
    wr = wr_ref[...]
    wr_hi = wr.astype(bf16)
    wr_lo = (wr - wr_hi.astype(f32)).astype(bf16)
    lane = lax.broadcasted_iota(jnp.int32, (tj, LANES), 1)
    sub = lax.broadcasted_iota(jnp.int32, (SUBLANES, LANES), 0)
    neg = jnp.float32(-jnp.inf)
    big = jnp.int32(LANES)

    def first_argmax(vals, mask):
        v = jnp.where(mask, vals, neg)
        m = jnp.max(v, axis=1, keepdims=True)
        idx = jnp.min(jnp.where(mask & (v == m), lane, big), axis=1, keepdims=True)
        return m, idx

    for p in range(2):
        rows = slice(p * tj, (p + 1) * tj)
        parts = []
        for ref in (yh_ref, yf_ref):
            y = ref[p].astype(f32)
            ysq = (y * y).astype(bf16)
            ms = jnp.concatenate([jnp.dot(ysq[:, k:k + LANES], avg, preferred_element_type=f32)
                                  for k in range(0, C, LANES)], axis=1)
            parts.append(y * lax.rsqrt(ms + EPS))
        yn = (jnp.concatenate(parts, axis=1) * mg_ref[...]).astype(bf16)
        h2 = _parity_rows(stage, p, tj) + jnp.dot(yn, wo_ref[...], preferred_element_type=f32)
        h2_ref[rows, :] = h2
        tok = _rms(h2, g2_ref[...])
        tok_ref[rows, :] = tok.astype(bf16)

        t_hi = tok.astype(bf16)
        t_lo = (tok - t_hi.astype(f32)).astype(bf16)
        logits = (jnp.dot(t_hi, wr_hi, preferred_element_type=f32)
                  + jnp.dot(t_lo, wr_hi, preferred_element_type=f32)
                  + jnp.dot(t_hi, wr_lo, preferred_element_type=f32)) + br_ref[...]

        gmask = (lane >= N_EXPERTS) & (lane < N_EXPERTS + N_GROUPS)
        gmax, gidx = first_argmax(logits, gmask)
        gsel = gidx - N_EXPERTS
        p_group = 1.0 / jnp.sum(jnp.where(gmask, jnp.exp(logits - gmax), 0.0), axis=1, keepdims=True)
        emask = (lane >= gsel * EXPERTS_PER_GROUP) & (lane < (gsel + 1) * EXPERTS_PER_GROUP)
        l0, e0 = first_argmax(logits, emask)
        l1, e1 = first_argmax(logits, emask & (lane != e0))
        ex = jnp.exp(l1 - l0)
        w0 = p_group / (1.0 + ex)
        w1 = p_group * ex / (1.0 + ex)

        oh0 = (lane == e0).astype(f32)
        oh1 = (lane == e1).astype(f32)
        both = oh0 + oh1
        n_e = jnp.sum(both, axis=0, keepdims=True)
        n_e8 = jnp.broadcast_to(n_e, (SUBLANES, LANES))
        tile_off = jnp.dot(n_e8.astype(bf16), upper, preferred_element_type=f32)[0:1, :]
        where = jnp.dot(tri, both.astype(bf16), preferred_element_type=f32) + tile_off
        pos0 = jnp.sum(oh0 * where, axis=1, keepdims=True)
        pos1 = jnp.sum(oh1 * where, axis=1, keepdims=True)
        base = carry_ref[0:1, :]
        tinfo_ref[p] = jnp.where(sub == 0, n_e8, jnp.where(sub == 1, base, 0.0)).astype(jnp.int32)
        carry_ref[0:1, :] = base + n_e

        cols = jnp.where(lane == 0, w0, jnp.where(lane == 1, w1,
                         jnp.where(lane == 2, pos0, jnp.where(lane == 3, pos1, 0.0))))
        vec_ref[rows, :] = cols
        pos_ref[p] = cols.T[:SUBLANES, :].astype(jnp.int32)
    cnt_ref[...] = jnp.broadcast_to(carry_ref[0:1, :], cnt_ref.shape).astype(jnp.int32)


def _outproj_router(yh, yf, x, mix_g, wo, norm2_g, wr, br, C):
    B, L, D = x.shape
    Lh = L // 2
    tj = _tile(256, Lh)
    nj = Lh // tj
    T = B * L
    ntiles = T // tj
    lin = lambda b, j: b * nj + j
    full = lambda a: pl.BlockSpec(a.shape, lambda b, j: (0,) * a.ndim)
    mg = mix_g.reshape(1, D)
    g2 = norm2_g.reshape(1, D)
    return pl.pallas_call(
        functools.partial(_outproj_router_kernel, C=C),
        grid=(B, nj),
        in_specs=[pl.BlockSpec((2, tj, C), lambda b, j: (0, j, b)),
                  pl.BlockSpec((2, tj, C), lambda b, j: (0, j, b)),
                  pl.BlockSpec((1, 2 * tj, D), lambda b, j: (b, j, 0)),
                  full(mg), full(wo), full(g2), full(wr), full(br)],
        out_specs=[pl.BlockSpec((2 * tj, D), lambda b, j: (lin(b, j), 0)),
                   pl.BlockSpec((2 * tj, D), lambda b, j: (lin(b, j), 0)),
                   pl.BlockSpec((2, SUBLANES, tj), lambda b, j: (lin(b, j), 0, 0)),
                   pl.BlockSpec((2 * tj, LANES), lambda b, j: (lin(b, j), 0)),
                   pl.BlockSpec((2, SUBLANES, LANES), lambda b, j: (lin(b, j), 0, 0)),
                   pl.BlockSpec((SUBLANES, LANES), lambda b, j: (0, 0))],
        out_shape=[jax.ShapeDtypeStruct((T, D), f32),
                   jax.ShapeDtypeStruct((T, D), bf16),
                   jax.ShapeDtypeStruct((ntiles, SUBLANES, tj), jnp.int32),
                   jax.ShapeDtypeStruct((T, LANES), f32),
                   jax.ShapeDtypeStruct((ntiles, SUBLANES, LANES), jnp.int32),
                   jax.ShapeDtypeStruct((SUBLANES, LANES), jnp.int32)],
        scratch_shapes=[pltpu.VMEM((SUBLANES, LANES), f32), pltpu.VMEM((D // LANES, 2 * tj, LANES), f32)],
        compiler_params=_cparams(("arbitrary", "arbitrary")),
        name="outproj_router",
    )(yh, yf, x, mg, wo, g2, wr, br)


def _padded(c):
    return ((c + MOE_ROWS - 1) // MOE_ROWS) * MOE_ROWS


def _pad_starts(cnt_ref, start_ref):
    def body(e, acc):
        start_ref[e] = acc
        return acc + _padded(cnt_ref[0, e])
    return lax.fori_loop(0, N_EXPERTS, body, jnp.int32(0))


def _segment_copies(n, src, src_off, dst, dst_off, sem, max_rows, wait, src_advances=True):
    bit = max_rows
    while bit >= 1:
        take = n & bit

        @pl.when(take != 0)
        def _(src_off=src_off, dst_off=dst_off, bit=bit):
            cp = pltpu.make_async_copy(src.at[pl.ds(src_off, bit)], dst.at[pl.ds(dst_off, bit)], sem)
            if wait:
                cp.wait()
            else:
                cp.start()
        if src_advances:
            src_off = src_off + take
        dst_off = dst_off + take
        bit //= 2


def _tile_segments(tinfo_ref, start_ref, local, local_is_src, remote, sem, tj, wait):
    def body(e, off):
        n = tinfo_ref[0, 0, e]
        far = start_ref[e] + tinfo_ref[0, 1, e]
        if local_is_src:
            _segment_copies(n, local, off, remote, far, sem, tj, wait)
        else:
            _segment_copies(n, remote, far, local, off, sem, tj, wait)
        return off + n
    lax.fori_loop(0, N_EXPERTS, body, jnp.int32(0))


def _dispatch_kernel(tinfo_ref, cnt_ref, tok_ref, pos_ref, xs_ref, be_ref, sbuf, zbuf, start_ref, sem, zsem):
    tj = tok_ref.shape[0]
    n_blocks = be_ref.shape[0]

    @pl.when(pl.program_id(0) == 0)
    def _():
        total = _pad_starts(cnt_ref, start_ref)
        used = total // MOE_ROWS

        def fill(i, e):
            def cond(e):
                nxt = jnp.where(e + 1 < N_EXPERTS, start_ref[jnp.minimum(e + 1, N_EXPERTS - 1)], total)
                return (e + 1 < N_EXPERTS) & (i * MOE_ROWS >= nxt)
            e = lax.while_loop(cond, lambda e: e + 1, e)
            be_ref[i] = jnp.where(i < used, e, e + N_EXPERTS)
            return e
        lax.fori_loop(0, n_blocks, fill, jnp.int32(0))

        zbuf[...] = jnp.zeros_like(zbuf)
        zrows = zbuf.shape[0]
        for wait in (False, True):
            def zero_pad(e, c, wait=wait):
                cnt = cnt_ref[0, e]
                _segment_copies(_padded(cnt) - cnt, zbuf, 0, xs_ref, start_ref[e] + cnt, zsem,
                                zrows, wait, src_advances=False)
                return c
            lax.fori_loop(0, N_EXPERTS, zero_pad, 0)

            def zero_tail(i, c, wait=wait):
                cp = pltpu.make_async_copy(zbuf, xs_ref.at[pl.ds(total + i * zrows, zrows)], zsem)
                if wait:
                    cp.wait()
                else:
                    cp.start()
                return c
            lax.fori_loop(0, (n_blocks - used) * (MOE_ROWS // zrows), zero_tail, 0)

    slot = lax.broadcasted_iota(jnp.int32, (2 * tj, tj), 0)
    perm = ((slot == pos_ref[0, 2:3, :]) | (slot == pos_ref[0, 3:4, :])).astype(bf16)
    sorted_rows = jnp.dot(perm, tok_ref[...], preferred_element_type=f32)
    _store_row_tiles(sbuf, _pack_bf16_pairs(sorted_rows, is_bf16_exact=True))
    _tile_segments(tinfo_ref, start_ref, sbuf, True, xs_ref, sem, tj, wait=False)
    pltpu.make_async_copy(sbuf, xs_ref.at[pl.ds(0, 2 * tj)], sem).wait()


def _dispatch(tinfo, cnt, tok, pos, n_rows):
    ntiles, _, tj = pos.shape
    T, D = tok.shape
    n_blocks = n_rows // MOE_ROWS
    return pl.pallas_call(
        _dispatch_kernel,
        grid=(ntiles,),
        in_specs=[pl.BlockSpec((1, SUBLANES, LANES), lambda i: (i, 0, 0), memory_space=pltpu.SMEM),
                  pl.BlockSpec(memory_space=pltpu.SMEM),
                  pl.BlockSpec((tj, D), lambda i: (i, 0)),
                  pl.BlockSpec((1, SUBLANES, tj), lambda i: (i, 0, 0))],
        out_specs=[pl.BlockSpec(memory_space=pl.ANY),
                   pl.BlockSpec(memory_space=pltpu.SMEM)],
        out_shape=[jax.ShapeDtypeStruct(_row_tiles_shape(n_rows, D), jnp.uint32),
                   jax.ShapeDtypeStruct((n_blocks,), jnp.int32)],
        scratch_shapes=[pltpu.VMEM(_row_tiles_shape(2 * tj, D), jnp.uint32),
                        pltpu.VMEM(_row_tiles_shape(MOE_ROWS // 2, D), jnp.uint32),
                        pltpu.SMEM((N_EXPERTS,), jnp.int32), pltpu.SemaphoreType.DMA,
                        pltpu.SemaphoreType.DMA],
        compiler_params=_cparams(("arbitrary",)),
        name="moe_dispatch",
    )(tinfo, cnt, tok, pos)


def _experts_kernel(be_ref, xs_ref, wg_ref, wu_ref, wd_ref, ys_ref, wg_s, wu_s, wd_s):
    i = pl.program_id(0)
    code = be_ref[i]
    prev = be_ref[jnp.maximum(i - 1, 0)]
    valid = code < N_EXPERTS

    @pl.when(valid & ((i == 0) | (code != prev)))
    def _():
        wg_s[...] = wg_ref[...].astype(bf16)
        wu_s[...] = wu_ref[...].astype(bf16)
        wd_s[...] = wd_ref[...].astype(bf16)

    @pl.when(valid)
    def _():
        xb = _unpack_bf16_pairs(_load_row_tiles(xs_ref)).astype(bf16)
        a = jnp.dot(xb, wg_s[...], preferred_element_type=f32)
        u = jnp.dot(xb, wu_s[...], preferred_element_type=f32)
        h = (a * jax.nn.sigmoid(a) * u).astype(bf16)
        _store_row_tiles(ys_ref, _pack_bf16_pairs(jnp.dot(h, wd_s[...], preferred_element_type=f32)))

    @pl.when(jnp.logical_not(valid))
    def _():
        ys_ref[...] = jnp.zeros_like(ys_ref)


def _experts(block_e, xs, w_gate, w_up, w_down):
    n_rows = xs.shape[0]
    _, D, De = w_gate.shape
    n_blocks = n_rows // MOE_ROWS
    blk = _row_tiles_shape(MOE_ROWS, D)
    widx = lambda i, be: (be[i] % N_EXPERTS, 0, 0)
    xidx = lambda i, be: (jnp.where(be[i] < N_EXPERTS, i, 0), 0, 0)
    return pl.pallas_call(
        _experts_kernel,
        grid_spec=pltpu.PrefetchScalarGridSpec(
            num_scalar_prefetch=1,
            grid=(n_blocks,),
            in_specs=[pl.BlockSpec(blk, xidx),
                      pl.BlockSpec((None, D, De), widx),
                      pl.BlockSpec((None, D, De), widx),
                      pl.BlockSpec((None, De, D), widx)],
            out_specs=pl.BlockSpec(blk, lambda i, be: (i, 0, 0)),
            scratch_shapes=[pltpu.VMEM((D, De), bf16), pltpu.VMEM((D, De), bf16), pltpu.VMEM((De, D), bf16)],
        ),
        out_shape=jax.ShapeDtypeStruct(xs.shape, jnp.uint32),
        compiler_params=_cparams(("arbitrary",)),
        name="moe_experts",
    )(block_e, xs, w_gate, w_up, w_down)


def _combine_kernel(tinfo_ref, cnt_ref, h2_ref, vec_ref, g_ref, ys_ref, o_ref, gbuf, stage, start_ref, sem):
    tj = h2_ref.shape[0]

    @pl.when(pl.program_id(0) == 0)
    def _():
        _pad_starts(cnt_ref, start_ref)

    _tile_segments(tinfo_ref, start_ref, gbuf, False, ys_ref, sem, tj, wait=False)
    pltpu.make_async_copy(ys_ref.at[pl.ds(0, 2 * tj)], gbuf, sem).wait()
    rows =_unpack_bf16_pairs(_load_row_tiles(gbuf)).astype(bf16)
    v = vec_ref[...]
    slot = lax.broadcasted_iota(jnp.int32, (tj, 2 * tj), 1)
    y = h2_ref[...]
    for k in range(2):
        pick = (slot == v[:, 2 + k:3 + k].astype(jnp.int32)).astype(bf16)
        y = y + v[:, k:k + 1] * jnp.dot(pick, rows, preferred_element_type=f32)
    out = _rms(y, g_ref[...])
    parity = pl.program_id(0) % 2
    for p in range(2):
        @pl.when(parity == p)
        def _(p=p):
            for k in range(stage.shape[0]):
                stage[k, pl.ds(p, tj, stride=2), :] = out[:, k * LANES:(k + 1) * LANES]

    @pl.when(parity == 1)
    def _():
        for k in range(stage.shape[0]):
            o_ref[0, :, k * LANES:(k + 1) * LANES] = stage[k]


def _combine(tinfo, cnt, h2, vec, final_g, ys, B, L):
    T, D = h2.shape
    ntiles = tinfo.shape[0]
    tj = T // ntiles
    nj = L // (2 * tj)
    g = final_g.reshape(1, D)
    return pl.pallas_call(
        _combine_kernel,
        grid=(ntiles,),
        in_specs=[pl.BlockSpec((1, SUBLANES, LANES), lambda i: (i, 0, 0), memory_space=pltpu.SMEM),
                  pl.BlockSpec(memory_space=pltpu.SMEM),
                  pl.BlockSpec((tj, D), lambda i: (i, 0)),
                  pl.BlockSpec((tj, LANES), lambda i: (i, 0)),
                  pl.BlockSpec((1, D), lambda i: (0, 0)),
                  pl.BlockSpec(memory_space=pl.ANY)],
        out_specs=pl.BlockSpec((1, 2 * tj, D), lambda i: (i // (2 * nj), (i // 2) % nj, 0)),
        out_shape=jax.ShapeDtypeStruct((B, L, D), f32),
        scratch_shapes=[pltpu.VMEM(_row_tiles_shape(2 * tj, D), jnp.uint32),
                        pltpu.VMEM((D // LANES, 2 * tj, LANES), f32),
                        pltpu.SMEM((N_EXPERTS,), jnp.int32), pltpu.SemaphoreType.DMA],
        compiler_params=_cparams(("arbitrary",)),
        name="moe_combine",
    )(tinfo, cnt, h2, vec, g, ys)


def kernel(x, norm1_g, w_in, conv_w, conv_b, f_w_in, f_b_in, f_w_mid, f_b_mid, f_freq, f_w_out, f_bias,
           mix_g, w_out, norm2_g, w_group, b_group, w_router, b_router, w_gate, w_up, w_down, final_g):
    B, L, D = x.shape
    depth = norm1_g.shape[0]
    assert depth == 1, "the final RMSNorm is fused into the single layer's MoE combine"
    C = D // 2
    Lh = L // 2
    N = B * C
    T = B * L
    n_rows = T * 2 + N_EXPERTS * MOE_ROWS
    cfwd, cinv, cfou = _dft_consts(L)
    i = 0
    wcat = _prep_win(w_in[i], C)
    hcat, asum = _hyena_filters(L, C, f_w_in[i], f_b_in[i], f_w_mid[i], f_b_mid[i], f_freq[i], f_w_out[i])
    kspec, k0 = _filter_spectrum(cfwd, hcat, asum, L, C)
    z, x0, u4 = _inproj(x, norm1_g[i], wcat, conv_w[i], conv_b[i], C)
    z = z.reshape(2, Lh, N)
    x0 = x0.reshape(2, Lh, N)
    u4 = u4.reshape(4, Lh, N)
    s = _conv_fwd(cfwd, z, kspec, k0, C)
    yh = _conv_inv(cinv, s, x0, z, f_bias[i], C)
    yf = _fourier_dft(cfou, u4, L, C)
    wr = jnp.concatenate([jnp.transpose(w_router[i], (1, 0, 2)).reshape(D, N_EXPERTS), w_group[i]], axis=1)
    wr = jnp.pad(wr, ((0, 0), (0, LANES - wr.shape[1])))
    br = jnp.pad(jnp.concatenate([b_router[i].reshape(-1), b_group[i]]), (0, LANES - N_EXPERTS - N_GROUPS))
    h2, tok, pos, vec, tinfo, cnt = _outproj_router(yh, yf, x, mix_g[i], w_out[i].astype(bf16), norm2_g[i],
                                                     wr, br.reshape(1, LANES), C)
    xs, block_e = _dispatch(tinfo, cnt, tok, pos, n_rows)
    ys = _experts(block_e, xs, w_gate[i], w_up[i], w_down[i])
    return _combine(tinfo, cnt, h2, vec, final_g, ys, B, L)
```

```python
import functools
import math

import jax
import jax.numpy as jnp
from jax import lax
from jax.experimental import pallas as pl
from jax.experimental.pallas import tpu as pltpu

HEAD_DIM = 64
N_GROUPS = 4
EXPERTS_PER_GROUP = 8
N_EXPERTS = N_GROUPS * EXPERTS_PER_GROUP
FILTER_BANDS = 16
DECAY_FAST_PCT = 0.3
DECAY_SLOW_PCT = 1.5
DECAY_TARGET = 1e-2
EPS = 1e-6

LANES = 128
SUBLANES = 8
MOE_ROWS = 256
TILES_PER_STEP = 4
GATHER_AHEAD = 2
SEG_SPLIT = 32
VMEM_LIMIT = 56 * 1024 * 1024

f32 = jnp.float32
bf16 = jnp.bfloat16
HI = lax.Precision.HIGHEST


def _cparams(sem):
    return pltpu.CompilerParams(dimension_semantics=sem, vmem_limit_bytes=VMEM_LIMIT)


def _tile(pref, n):
    return min(pref, n)


def _angle(prod, m):
    return (prod & (m - 1)).astype(f32) * (2.0 * math.pi / m)


def _dft_consts_kernel(fwd_ref, inv_ref, fou_ref, *, L):
    Lh = L // 2
    Lq = Lh // 2
    tr = fwd_ref.shape[1]
    r = lax.broadcasted_iota(jnp.int32, (tr, LANES), 0) + pl.program_id(0) * tr
    d = lax.broadcasted_iota(jnp.int32, (tr, LANES), 1)
    alt_r = (1 - 2 * (r & 1)).astype(f32)
    alt_d = (1 - 2 * (d & 1)).astype(f32)
    ncb = Lh // LANES
    two_r1 = 2 * r + 1

    def col_l(cb):
        c0 = cb * LANES
        q, j0 = c0 // Lq, c0 % Lq
        return 2 * j0 + q

    fams = [
        (r, 1, lambda cb: cb * LANES, L),
        (r, 2, lambda cb: 2 * cb * LANES + 1, 2 * L),
        (two_r1, 1, lambda cb: cb * LANES, 2 * L),
        (r, 2, col_l, Lh),
        (two_r1, 2, col_l, L),
    ]
    for fi, (a, s, b0, m) in enumerate(fams):
        beta = _angle(a * (s * d), m)
        tc, ts = jnp.cos(beta), jnp.sin(beta)
        b0_lane = jnp.zeros((tr, LANES), jnp.int32)
        for cb in range(ncb):
            b0_lane = jnp.where(d == cb, b0(cb), b0_lane)
        alpha = _angle(a * b0_lane, m)
        ac, asn = jnp.cos(alpha), jnp.sin(alpha)
        for cb in range(ncb):
            ca = ac[:, cb:cb + 1]
            sa = asn[:, cb:cb + 1]
            c = ca * tc - sa * ts
            sn = sa * tc + ca * ts
            sl = slice(cb * LANES, (cb + 1) * LANES)
            if fi == 0:
                fwd_ref[0, :, sl] = c.astype(bf16)
                inv_ref[0, :, sl] = c.astype(bf16)
                fwd_ref[1, :, sl] = jnp.where(r == 0, alt_d, sn).astype(bf16)
                nst = -sn
                if cb == 0:
                    nst = jnp.where(d == 0, -alt_r, nst)
                inv_ref[1, :, sl] = nst.astype(bf16)
            elif fi == 1:
                fwd_ref[2, :, sl] = c.astype(bf16)
                fwd_ref[3, :, sl] = jnp.where(r == 0, alt_d, sn).astype(bf16)
            elif fi == 2:
                inv_ref[2, :, sl] = c.astype(bf16)
                nst = -sn
                if cb == 0:
                    nst = jnp.where(d == 0, -alt_r, nst)
                inv_ref[3, :, sl] = nst.astype(bf16)
            elif fi == 3:
                fou_ref[0, :, sl] = c.astype(bf16)
                fou_ref[1, :, sl] = (-sn).astype(bf16)
            else:
                fou_ref[2, :, sl] = c.astype(bf16)
                fou_ref[3, :, sl] = (-sn).astype(bf16)


def _dft_consts(L):
    Lh = L // 2
    tr = _tile(256, Lh)
    shp = jax.ShapeDtypeStruct((4, Lh, Lh), bf16)
    spec = pl.BlockSpec((4, tr, Lh), lambda i: (0, i, 0))
    return pl.pallas_call(
        functools.partial(_dft_consts_kernel, L=L),
        grid=(Lh // tr,),
        out_specs=[spec, spec, spec],
        out_shape=[shp, shp, shp],
        compiler_params=_cparams(("parallel",)),
        name="dft_consts",
    )()


def _prep_win_kernel(w_ref, o_ref, *, C):
    w = w_ref[...]
    wf = w[:, 3 * C:]
    i = lax.broadcasted_iota(jnp.int32, (C, C), 0)
    j = lax.broadcasted_iota(jnp.int32, (C, C), 1)
    same = (i // HEAD_DIM) == (j // HEAD_DIM)
    ang = _angle((i % HEAD_DIM) * (j % HEAD_DIM), HEAD_DIM)
    bc = jnp.where(same, jnp.cos(ang), 0.0)
    bs = jnp.where(same, jnp.sin(ang), 0.0)
    o_ref[:, :3 * C] = w[:, :3 * C].astype(bf16)
    o_ref[:, 3 * C:4 * C] = jnp.dot(wf, bc, precision=HI, preferred_element_type=f32).astype(bf16)
    o_ref[:, 4 * C:] = jnp.dot(wf, bs, precision=HI, preferred_element_type=f32).astype(bf16)


def _prep_win(w_in, C):
    D = w_in.shape[0]
    tr = _tile(256, D)
    return pl.pallas_call(
        functools.partial(_prep_win_kernel, C=C),
        grid=(D // tr,),
        in_specs=[pl.BlockSpec((tr, 4 * C), lambda i: (i, 0))],
        out_specs=pl.BlockSpec((tr, 5 * C), lambda i: (i, 0)),
        out_shape=jax.ShapeDtypeStruct((D, 5 * C), bf16),
        compiler_params=_cparams(("parallel",)),
        name="prep_win",
    )(w_in)


def _filter_kernel(win_ref, bin_ref, wmid_ref, bmid_ref, freq_ref, wout_ref, h_ref, asum_ref, *, L, C):
    tr = h_ref.shape[1]
    step = pl.program_id(0)
    lane = lax.broadcasted_iota(jnp.int32, (tr, LANES), 1)
    row = lax.broadcasted_iota(jnp.int32, (tr, LANES), 0) + step * tr
    band_i = jnp.where(lane <= FILTER_BANDS, lane - 1, lane - 1 - FILTER_BANDS)
    band = 1e-4 + band_i.astype(f32) * ((FILTER_BANDS - 1 - 1e-4) / (FILTER_BANDS - 1))
    ch = lax.broadcasted_iota(jnp.int32, (tr, 2 * C), 1) % C
    max_decay = math.log(DECAY_TARGET) / DECAY_FAST_PCT
    min_decay = math.log(DECAY_TARGET) / DECAY_SLOW_PCT
    delta = jnp.abs(min_decay + ch.astype(f32) * ((max_decay - min_decay) / (C - 1)))
    is_bwd = lax.broadcasted_iota(jnp.int32, (tr, 2 * C), 1) >= C
    row_c = lax.broadcasted_iota(jnp.int32, (tr, 2 * C), 0) + step * tr

    @pl.when(step == 0)
    def _():
        asum_ref[...] = jnp.zeros_like(asum_ref)

    total = jnp.zeros((1, 2 * C), f32)
    for p in range(2):
        pos = (2 * row + p).astype(f32)
        t = pos * (1.0 / (L - 1))
        w = pos * (2.0 * math.pi / L)
        fw = band * w
        z = jnp.where(lane == 0, t,
                      jnp.where(lane <= FILTER_BANDS, jnp.cos(fw),
                                jnp.where(lane <= 2 * FILTER_BANDS, -jnp.sin(fw), 0.0)))
        h = jnp.sin(freq_ref[0:1, :] * (jnp.dot(z, win_ref[...], precision=HI, preferred_element_type=f32)
                                       + bin_ref[...]))
        for i in range(wmid_ref.shape[0]):
            h = jnp.sin(freq_ref[i + 1:i + 2, :]
                        * (jnp.dot(h, wmid_ref[i], precision=HI, preferred_element_type=f32)
                           + bmid_ref[i:i + 1, :]))
        out = jnp.dot(h, wout_ref[...], precision=HI, preferred_element_type=f32)
        tc = (2 * row_c + p).astype(f32) * (1.0 / (L - 1))
        out = out * jnp.exp(-tc * delta)
        if p == 0:
            out = jnp.where(is_bwd & (row_c == 0), 0.0, out)
        h_ref[p] = out.astype(bf16)
        total = total + jnp.sum(jnp.abs(out), axis=0, keepdims=True)
    asum_ref[0:1, :] += total


def _hyena_filters(L, C, f_w_in, f_b_in, f_w_mid, f_b_mid, f_freq, f_w_out):
    Lh = L // 2
    tr = _tile(256, Lh)
    order = f_w_in.shape[1]
    win = jnp.pad(f_w_in, ((0, LANES - f_w_in.shape[0]), (0, 0)))
    full = lambda a: pl.BlockSpec(a.shape, lambda i: (0,) * a.ndim)
    args = (win, f_b_in.reshape(1, order), f_w_mid, f_b_mid, f_freq, f_w_out)
    return pl.pallas_call(
        functools.partial(_filter_kernel, L=L, C=C),
        grid=(Lh // tr,),
        in_specs=[full(a) for a in args],
        out_specs=[pl.BlockSpec((2, tr, 2 * C), lambda i: (0, i, 0)),
                   pl.BlockSpec((SUBLANES, 2 * C), lambda i: (0, 0))],
        out_shape=[jax.ShapeDtypeStruct((2, Lh, 2 * C), bf16),
                   jax.ShapeDtypeStruct((SUBLANES, 2 * C), f32)],
        compiler_params=_cparams(("arbitrary",)),
        name="hyena_filters",
    )(*args)


def _fwd_products(c_ref, x_ref):
    xe, xo = x_ref[0], x_ref[1]
    ce = jnp.dot(c_ref[0], xe, preferred_element_type=f32)
    se = jnp.dot(c_ref[1], xe, preferred_element_type=f32)
    co = jnp.dot(c_ref[2], xo, preferred_element_type=f32)
    so = jnp.dot(c_ref[3], xo, preferred_element_type=f32)
    return ce, se, co, so


def _filter_spectrum_kernel(c_ref, h_ref, asum_ref, k_ref, k0_ref, *, L, C):
    ce, se, co, so = _fwd_products(c_ref, h_ref)
    a = asum_ref[0:1, :]
    scale = 1.0 / ((a[:, :C] + a[:, C:]) * L)
    F = lambda v: v[:, :C]
    G = lambda v: v[:, C:]
    k_ref[0] = (F(ce) + F(co) + G(ce) + G(co)) * scale
    k_ref[1] = (-(F(se) + F(so)) + (G(se) + G(so))) * scale
    k_ref[2] = (F(ce) - F(co) + G(ce) - G(co)) * scale
    k_ref[3] = ((F(se) - F(so)) - (G(se) - G(so))) * scale

    @pl.when(pl.program_id(0) == 0)
    def _():
        k0_ref[...] = jnp.zeros_like(k0_ref)
        k0_ref[0:1, :] = ((F(se) + G(se)) * scale)[0:1, :]
        k0_ref[1:2, :] = ((F(so) - G(so)) * scale)[0:1, :]


def _filter_spectrum(cfwd, hcat, asum, L, C):
    Lh = L // 2
    tm = _tile(256, Lh)
    return pl.pallas_call(
        functools.partial(_filter_spectrum_kernel, L=L, C=C),
        grid=(Lh // tm,),
        in_specs=[pl.BlockSpec((4, tm, Lh), lambda i: (0, i, 0)),
                  pl.BlockSpec((2, Lh, 2 * C), lambda i: (0, 0, 0)),
                  pl.BlockSpec((SUBLANES, 2 * C), lambda i: (0, 0))],
        out_specs=[pl.BlockSpec((4, tm, C), lambda i: (0, i, 0)),
                   pl.BlockSpec((SUBLANES, C), lambda i: (0, 0))],
        out_shape=[jax.ShapeDtypeStruct((4, Lh, C), f32),
                   jax.ShapeDtypeStruct((SUBLANES, C), f32)],
        compiler_params=_cparams(("arbitrary",)),
        name="filter_spectrum",
    )(cfwd, hcat, asum)


def _conv_fwd_kernel(c_ref, z_ref, k_ref, k0_ref, s_ref):
    ce, se, co, so = _fwd_products(c_ref, z_ref)
    zsr, zsi, zdr, zdi = ce + co, -(se + so), ce - co, se - so
    ksr, ksi, kdr, kdi = k_ref[0], k_ref[1], k_ref[2], k_ref[3]
    psr = zsr * ksr - zsi * ksi
    psi = zsr * ksi + zsi * ksr
    pdr = zdr * kdr - zdi * kdi
    pdi = zdr * kdi + zdi * kdr
    gen = (psr + pdr, psi - pdi, psr - pdr, psi + pdi)
    first = pl.program_id(1) == 0

    @pl.when(first)
    def _():
        row0 = lax.broadcasted_iota(jnp.int32, ce.shape, 0) == 0
        p0 = zsr * ksr
        pl_ = zdr * kdr
        ak, bk = k0_ref[0:1, :], k0_ref[1:2, :]
        phr = se * ak - so * bk
        phi = -(se * bk + so * ak)
        spec = (0.5 * (p0 + pl_), -phr, 0.5 * (p0 - pl_), phi)
        for i in range(4):
            s_ref[i] = jnp.where(row0, spec[i], gen[i]).astype(bf16)

    @pl.when(jnp.logical_not(first))
    def _():
        for i in range(4):
            s_ref[i] = gen[i].astype(bf16)


def _conv_fwd(cfwd, z, kspec, k0, C):
    _, Lh, N = z.shape
    tm = _tile(256, Lh)
    return pl.pallas_call(
        _conv_fwd_kernel,
        grid=(N // C, Lh // tm),
        in_specs=[pl.BlockSpec((4, tm, Lh), lambda n, i: (0, i, 0)),
                  pl.BlockSpec((2, Lh, C), lambda n, i: (0, 0, n)),
                  pl.BlockSpec((4, tm, C), lambda n, i: (0, i, 0)),
                  pl.BlockSpec((SUBLANES, C), lambda n, i: (0, 0))],
        out_specs=pl.BlockSpec((4, tm, C), lambda n, i: (0, i, n)),
        out_shape=jax.ShapeDtypeStruct((4, Lh, N), bf16),
        compiler_params=_cparams(("parallel", "arbitrary")),
        name="conv_fwd",
    )(cfwd, z, kspec, k0)


def _pair_products(c_ref, r_ref):
    ev = (jnp.dot(c_ref[0], r_ref[0], preferred_element_type=f32)
          + jnp.dot(c_ref[1], r_ref[1], preferred_element_type=f32))
    od = (jnp.dot(c_ref[2], r_ref[2], preferred_element_type=f32)
          + jnp.dot(c_ref[3], r_ref[3], preferred_element_type=f32))
    return ev, od


def _conv_inv_kernel(c_ref, s_ref, x0_ref, z_ref, fb_ref, y_ref):
    conv = _pair_products(c_ref, s_ref)
    fb = fb_ref[...]
    for p in range(2):
        zp = z_ref[p].astype(f32)
        y_ref[p] = (x0_ref[p].astype(f32) * (conv[p] + zp * fb)).astype(bf16)


def _conv_inv(cinv, s, x0, z, f_bias, C):
    _, Lh, N = s.shape
    tm = _tile(256, Lh)
    return pl.pallas_call(
        _conv_inv_kernel,
        grid=(N // C, Lh // tm),
        in_specs=[pl.BlockSpec((4, tm, Lh), lambda n, i: (0, i, 0)),
                  pl.BlockSpec((4, Lh, C), lambda n, i: (0, 0, n)),
                  pl.BlockSpec((2, tm, C), lambda n, i: (0, i, n)),
                  pl.BlockSpec((2, tm, C), lambda n, i: (0, i, n)),
                  pl.BlockSpec((1, C), lambda n, i: (0, 0))],
        out_specs=pl.BlockSpec((2, tm, C), lambda n, i: (0, i, n)),
        out_shape=jax.ShapeDtypeStruct((2, Lh, N), bf16),
        compiler_params=_cparams(("parallel", "arbitrary")),
        name="conv_inv",
    )(cinv, s, x0, z, f_bias.reshape(1, C))


def _fourier_kernel(c_ref, u_ref, y_ref, *, scale):
    ev, od = _pair_products(c_ref, u_ref)
    y_ref[0] = (ev * scale).astype(bf16)
    y_ref[1] = (od * scale).astype(bf16)


def _fourier_dft(cfou, u4, L, C):
    _, Lh, N = u4.shape
    tm = _tile(256, Lh)
    return pl.pallas_call(
        functools.partial(_fourier_kernel, scale=1.0 / math.sqrt(L * HEAD_DIM)),
        grid=(N // C, Lh // tm),
        in_specs=[pl.BlockSpec((4, tm, Lh), lambda n, i: (0, i, 0)),
                  pl.BlockSpec((4, Lh, C), lambda n, i: (0, 0, n))],
        out_specs=pl.BlockSpec((2, tm, C), lambda n, i: (0, i, n)),
        out_shape=jax.ShapeDtypeStruct((2, Lh, N), bf16),
        compiler_params=_cparams(("parallel", "arbitrary")),
        name="fourier_dft",
    )(cfou, u4)


def _rms(x, g):
    return x * lax.rsqrt(jnp.mean(x * x, axis=-1, keepdims=True) + EPS) * g


def _stage_rows(chunks, x):
    for k in range(chunks.shape[0]):
        chunks[k] = x[:, k * LANES:(k + 1) * LANES]


def _parity_rows(chunks, p, n):
    return jnp.concatenate([chunks[k, pl.ds(p, n, stride=2), :] for k in range(chunks.shape[0])], axis=1)


def _inproj_kernel(xa_ref, xb_ref, pa_ref, na_ref, pb_ref, nb_ref, g_ref, w_ref, cw_ref, cb_ref,
                   z_ref, x0_ref, u_ref, stage, *, C, D):
    tj = xa_ref.shape[1] // 2
    jt = pl.program_id(1)
    n_half = pl.num_programs(1)
    g = g_ref[...]
    w = w_ref[...]
    cw0, cw1, cw2 = cw_ref[0:1, :], cw_ref[1:2, :], cw_ref[2:3, :]
    cb = cb_ref[...]
    row = lax.broadcasted_iota(jnp.int32, (tj, 3 * C), 0)
    four = []
    halves = ((xa_ref, pa_ref, na_ref, jt == 0, False),
              (xb_ref, pb_ref, nb_ref, False, jt == n_half - 1))
    for h, (x_ref, p_ref, n_ref, at_start, at_end) in enumerate(halves):
        _stage_rows(stage, x_ref[0])
        lhs = jnp.concatenate([
            _rms(_parity_rows(stage, 0, tj), g).astype(bf16),
            _rms(_parity_rows(stage, 1, tj), g).astype(bf16),
            _rms(p_ref[0], g).astype(bf16),
            _rms(n_ref[0], g).astype(bf16)], axis=0)
        res = jnp.dot(lhs, w, preferred_element_type=f32)
        pe, po = res[:tj, :3 * C], res[tj:2 * tj, :3 * C]
        prev = res[2 * tj + SUBLANES - 1:2 * tj + SUBLANES, :3 * C]
        nxt = res[2 * tj + SUBLANES:2 * tj + SUBLANES + 1, :3 * C]
        if at_start is not False:
            prev = jnp.where(at_start, 0.0, prev)
        if at_end is not False:
            nxt = jnp.where(at_end, 0.0, nxt)
        po_dn = jnp.where(row == 0, prev, pltpu.roll(po, 1, 0))
        pe_up = jnp.where(row == tj - 1, nxt, pltpu.roll(pe, tj - 1, 0))
        uc = (cb + cw0 * po_dn + cw1 * pe + cw2 * po,
              cb + cw0 * pe + cw1 * po + cw2 * pe_up)
        for p in range(2):
            x0_ref[p, h] = uc[p][:, :C].astype(bf16)
            z_ref[p, h] = (uc[p][:, C:2 * C] * uc[p][:, 2 * C:]).astype(bf16)
        four.append((res[:tj, 3 * C:], res[tj:2 * tj, 3 * C:]))
    for p in range(2):
        fa, fb = four[0][p], four[1][p]
        u_ref[0, p] = (fa[:, :C] + fb[:, :C]).astype(bf16)
        u_ref[1, p] = (fa[:, C:] + fb[:, C:]).astype(bf16)
        u_ref[2, p] = (fa[:, :C] - fb[:, :C]).astype(bf16)
        u_ref[3, p] = (fa[:, C:] - fb[:, C:]).astype(bf16)


def _inproj(x, norm_g, wcat, conv_w, conv_b, C):
    B, L, D = x.shape
    Lq = L // 4
    tj = _tile(256, Lq)
    nq = Lq // tj
    N = B * C
    hb = 2 * tj // SUBLANES
    last = L // SUBLANES - 1
    main = lambda off: pl.BlockSpec((1, 2 * tj, D), lambda b, j: (b, j + off, 0))
    prev = lambda off: pl.BlockSpec((1, SUBLANES, D), lambda b, j: (b, jnp.maximum((j + off) * hb - 1, 0), 0))
    nxt = lambda off: pl.BlockSpec((1, SUBLANES, D), lambda b, j: (b, jnp.minimum((j + off + 1) * hb, last), 0))
    full = lambda a: pl.BlockSpec(a.shape, lambda b, j: (0,) * a.ndim)
    g = norm_g.reshape(1, D)
    cb = conv_b.reshape(1, 3 * C)
    return pl.pallas_call(
        functools.partial(_inproj_kernel, C=C, D=D),
        grid=(B, nq),
        in_specs=[main(0), main(nq), prev(0), nxt(0), prev(nq), nxt(nq),
                  full(g), full(wcat), full(conv_w), full(cb)],
        out_specs=[pl.BlockSpec((2, 2, tj, C), lambda b, j: (0, 0, j, b)),
                   pl.BlockSpec((2, 2, tj, C), lambda b, j: (0, 0, j, b)),
                   pl.BlockSpec((4, 2, tj, C), lambda b, j: (0, 0, j, b))],
        out_shape=[jax.ShapeDtypeStruct((2, 2, Lq, N), bf16),
                   jax.ShapeDtypeStruct((2, 2, Lq, N), bf16),
                   jax.ShapeDtypeStruct((4, 2, Lq, N), bf16)],
        scratch_shapes=[pltpu.VMEM((D // LANES, 2 * tj, LANES), f32)],
        compiler_params=_cparams(("parallel", "arbitrary")),
        name="inproj",
    )(x, x, x, x, x, x, g, wcat, conv_w, cb)


def _pack_bf16_pairs(v, is_bf16_exact=False):
    k = v.shape[1] // 2
    rounded = v if is_bf16_exact else v.astype(bf16).astype(f32)
    hi = pltpu.bitcast(rounded[:, :k], jnp.uint32)
    lo = pltpu.bitcast(rounded[:, k:], jnp.uint32)
    return hi | (lo >> 16)


def _unpack_bf16_pairs(u):
    hi = pltpu.bitcast(u & jnp.uint32(0xFFFF0000), f32)
    lo = pltpu.bitcast(u << 16, f32)
    return jnp.concatenate([hi, lo], axis=1)


def _row_tiles_shape(n, D):
    return (n, D // (2 * LANES), LANES)


def _store_row_tiles(ref, packed):
    n, per_row, _ = ref.shape
    flat = ref.reshape(n * per_row, LANES)
    for s in range(per_row):
        flat[pl.ds(s, n, stride=per_row), :] = packed[:, s * LANES:(s + 1) * LANES]


def _load_row_tiles(ref):
    n, per_row, _ = ref.shape
    flat = ref.reshape(n * per_row, LANES)
    return jnp.concatenate([flat[pl.ds(s, n, stride=per_row), :] for s in range(per_row)], axis=1)


def _outproj_router_kernel(yh_ref, yf_ref, x_ref, mg_ref, wo_ref, g2_ref, wr_ref, br_ref,
                           h2_ref, tok_ref, pos_ref, vec_ref, tinfo_ref, cnt_ref, carry_ref, stage, *, C):
    tj = yh_ref.shape[1]
    _stage_rows(stage, x_ref[0])

    @pl.when((pl.program_id(0) == 0) & (pl.program_id(1) == 0))
    def _():
        carry_ref[...] = jnp.zeros_like(carry_ref)

    li = lax.broadcasted_iota(jnp.int32, (LANES, LANES), 0)
    lj = lax.broadcasted_iota(jnp.int32, (LANES, LANES), 1)
    avg = jnp.where(li // HEAD_DIM == lj // HEAD_DIM, 1.0 / HEAD_DIM, 0.0).astype(bf16)
    upper = (li < lj).astype(bf16)
    ri = lax.broadcasted_iota(jnp.int32, (tj, tj), 0)
    ci = lax.broadcasted_iota(jnp.int32, (tj, tj), 1)
    tri = (ci < ri).astype(bf16)
    wr = wr_ref[...]
    wr_hi = wr.astype(bf16)
    wr_lo = (wr - wr_hi.astype(f32)).astype(bf16)
    lane = lax.broadcasted_iota(jnp.int32, (tj, LANES), 1)
    sub = lax.broadcasted_iota(jnp.int32, (SUBLANES, LANES), 0)
    neg = jnp.float32(-jnp.inf)
    big = jnp.int32(LANES)

    def first_argmax(vals, mask):
        v = jnp.where(mask, vals, neg)
        m = jnp.max(v, axis=1, keepdims=True)
        idx = jnp.min(jnp.where(mask & (v == m), lane, big), axis=1, keepdims=True)
        return m, idx

    for p in range(2):
        rows = slice(p * tj, (p + 1) * tj)
        parts = []
        for ref in (yh_ref, yf_ref):
            y = ref[p].astype(f32)
            ysq = (y * y).astype(bf16)
            ms = jnp.concatenate([jnp.dot(ysq[:, k:k + LANES], avg, preferred_element_type=f32)
                                  for k in range(0, C, LANES)], axis=1)
            parts.append(y * lax.rsqrt(ms + EPS))
        yn = (jnp.concatenate(parts, axis=1) * mg_ref[...]).astype(bf16)
        h2 = _parity_rows(stage, p, tj) + jnp.dot(yn, wo_ref[...], preferred_element_type=f32)
        h2_ref[rows, :] = h2
        tok = _rms(h2, g2_ref[...])
        tok_ref[rows, :] = tok.astype(bf16)

        t_hi = tok.astype(bf16)
        t_lo = (tok - t_hi.astype(f32)).astype(bf16)
        logits = (jnp.dot(t_hi, wr_hi, preferred_element_type=f32)
                  + jnp.dot(t_lo, wr_hi, preferred_element_type=f32)
                  + jnp.dot(t_hi, wr_lo, preferred_element_type=f32)) + br_ref[...]

        gmask = (lane >= N_EXPERTS) & (lane < N_EXPERTS + N_GROUPS)
        gmax, gidx = first_argmax(logits, gmask)
        gsel = gidx - N_EXPERTS
        p_group = 1.0 / jnp.sum(jnp.where(gmask, jnp.exp(logits - gmax), 0.0), axis=1, keepdims=True)
        emask = (lane >= gsel * EXPERTS_PER_GROUP) & (lane < (gsel + 1) * EXPERTS_PER_GROUP)
        l0, e0 = first_argmax(logits, emask)
        l1, e1 = first_argmax(logits, emask & (lane != e0))
        ex = jnp.exp(l1 - l0)
        w0 = p_group / (1.0 + ex)
        w1 = p_group * ex / (1.0 + ex)

        oh0 = (lane == e0).astype(f32)
        oh1 = (lane == e1).astype(f32)
        both = oh0 + oh1
        n_e = jnp.sum(both, axis=0, keepdims=True)
        n_e8 = jnp.broadcast_to(n_e, (SUBLANES, LANES))
        tile_off = jnp.dot(n_e8.astype(bf16), upper, preferred_element_type=f32)[0:1, :]
        where = jnp.dot(tri, both.astype(bf16), preferred_element_type=f32) + tile_off
        pos0 = jnp.sum(oh0 * where, axis=1, keepdims=True)
        pos1 = jnp.sum(oh1 * where, axis=1, keepdims=True)
        base = carry_ref[0:1, :]
        long_seg = (jnp.max(n_e, axis=1, keepdims=True) >= SEG_SPLIT).astype(f32)
        tinfo_ref[p] = jnp.where(sub == 0, n_e8, jnp.where(sub == 1, base, jnp.where(sub == 2, long_seg, 0.0))
                                 ).astype(jnp.int32)
        carry_ref[0:1, :] = base + n_e

        cols = jnp.where(lane == 0, w0, jnp.where(lane == 1, w1,
                         jnp.where(lane == 2, pos0, jnp.where(lane == 3, pos1, 0.0))))
        vec_ref[rows, :] = cols
        pos_ref[p] = cols.T[:SUBLANES, :].astype(jnp.int32)
    cnt_ref[...] = jnp.broadcast_to(carry_ref[0:1, :], cnt_ref.shape).astype(jnp.int32)


def _outproj_router(yh, yf, x, mix_g, wo, norm2_g, wr, br, C):
    B, L, D = x.shape
    Lh = L // 2
    tj = _tile(256, Lh)
    nj = Lh // tj
    T = B * L
    ntiles = T // tj
    lin = lambda b, j: b * nj + j
    full = lambda a: pl.BlockSpec(a.shape, lambda b, j: (0,) * a.ndim)
    mg = mix_g.reshape(1, D)
    g2 = norm2_g.reshape(1, D)
    return pl.pallas_call(
        functools.partial(_outproj_router_kernel, C=C),
        grid=(B, nj),
        in_specs=[pl.BlockSpec((2, tj, C), lambda b, j: (0, j, b)),
                  pl.BlockSpec((2, tj, C), lambda b, j: (0, j, b)),
                  pl.BlockSpec((1, 2 * tj, D), lambda b, j: (b, j, 0)),
                  full(mg), full(wo), full(g2), full(wr), full(br)],
        out_specs=[pl.BlockSpec((2 * tj, D), lambda b, j: (lin(b, j), 0)),
                   pl.BlockSpec((2 * tj, D), lambda b, j: (lin(b, j), 0)),
                   pl.BlockSpec((2, SUBLANES, tj), lambda b, j: (lin(b, j), 0, 0)),
                   pl.BlockSpec((2 * tj, LANES), lambda b, j: (lin(b, j), 0)),
                   pl.BlockSpec((2, SUBLANES, LANES), lambda b, j: (lin(b, j), 0, 0)),
                   pl.BlockSpec((SUBLANES, LANES), lambda b, j: (0, 0))],
        out_shape=[jax.ShapeDtypeStruct((T, D), f32),
                   jax.ShapeDtypeStruct((T, D), bf16),
                   jax.ShapeDtypeStruct((ntiles, SUBLANES, tj), jnp.int32),
                   jax.ShapeDtypeStruct((T, LANES), f32),
                   jax.ShapeDtypeStruct((ntiles, SUBLANES, LANES), jnp.int32),
                   jax.ShapeDtypeStruct((SUBLANES, LANES), jnp.int32)],
        scratch_shapes=[pltpu.VMEM((SUBLANES, LANES), f32), pltpu.VMEM((D // LANES, 2 * tj, LANES), f32)],
        compiler_params=_cparams(("arbitrary", "arbitrary")),
        name="outproj_router",
    )(yh, yf, x, mg, wo, g2, wr, br)


def _padded(c):
    return ((c + MOE_ROWS - 1) // MOE_ROWS) * MOE_ROWS


def _pad_starts(cnt_ref, start_ref):
    def body(e, acc):
        start_ref[e] = acc
        return acc + _padded(cnt_ref[0, e])
    return lax.fori_loop(0, N_EXPERTS, body, jnp.int32(0))


def _segment_copies(n, src, src_off, dst, dst_off, sem, max_rows, wait, src_advances=True, min_rows=1):
    bit = max_rows
    while bit >= min_rows:
        take = n & bit

        @pl.when(take != 0)
        def _(src_off=src_off, dst_off=dst_off, bit=bit):
            cp = pltpu.make_async_copy(src.at[pl.ds(src_off, bit)], dst.at[pl.ds(dst_off, bit)], sem)
            if wait:
                cp.wait()
            else:
                cp.start()
        if src_advances:
            src_off = src_off + take
        dst_off = dst_off + take
        bit //= 2


def _tile_segments(tinfo_ref, t, start_ref, local, local_is_src, remote, sem, tj):
    def copies(long_part):
        off = 0
        for e in range(N_EXPERTS):
            n = tinfo_ref[t, 0, e]
            far = start_ref[e] + tinfo_ref[t, 1, e]
            if long_part:
                args = (n, off, far, tj, SEG_SPLIT)
            else:
                head = n & ~(SEG_SPLIT - 1)
                args = (n, off + head, far + head, SEG_SPLIT // 2, 1)
            n_, near_, far_, hi, lo = args
            if local_is_src:
                _segment_copies(n_, local, near_, remote, far_, sem, hi, wait=False, min_rows=lo)
            else:
                _segment_copies(n_, remote, far_, local, near_, sem, hi, wait=False, min_rows=lo)
            off = off + n

    copies(long_part=False)

    @pl.when(tinfo_ref[t, 2, 0] != 0)
    def _():
        copies(long_part=True)


def _dispatch_kernel(tinfo_ref, cnt_ref, tok_ref, pos_ref, xs_ref, be_ref, sbufs, zbuf, start_ref, sems, zsem):
    tj = pos_ref.shape[2]
    n_blocks = be_ref.shape[0]

    @pl.when(pl.program_id(0) == 0)
    def _():
        total = _pad_starts(cnt_ref, start_ref)
        used = total // MOE_ROWS

        def fill(i, e):
            def cond(e):
                nxt = jnp.where(e + 1 < N_EXPERTS, start_ref[jnp.minimum(e + 1, N_EXPERTS - 1)], total)
                return (e + 1 < N_EXPERTS) & (i * MOE_ROWS >= nxt)
            e = lax.while_loop(cond, lambda e: e + 1, e)
            be_ref[i] = jnp.where(i < used, e, e + N_EXPERTS)
            return e
        lax.fori_loop(0, n_blocks, fill, jnp.int32(0))

        zbuf[...] = jnp.zeros_like(zbuf)
        zrows = zbuf.shape[0]
        for wait in (False, True):
            def zero_pad(e, c, wait=wait):
                cnt = cnt_ref[0, e]
                _segment_copies(_padded(cnt) - cnt, zbuf, 0, xs_ref, start_ref[e] + cnt, zsem,
                                zrows, wait, src_advances=False)
                return c
            lax.fori_loop(0, N_EXPERTS, zero_pad, 0)

            def zero_tail(i, c, wait=wait):
                cp = pltpu.make_async_copy(zbuf, xs_ref.at[pl.ds(total + i * zrows, zrows)], zsem)
                if wait:
                    cp.wait()
                else:
                    cp.start()
                return c
            lax.fori_loop(0, (n_blocks - used) * (MOE_ROWS // zrows), zero_tail, 0)

    def wait_tile(k):
        pltpu.make_async_copy(sbufs[k], xs_ref.at[pl.ds(0, 2 * tj)], sems.at[k]).wait()

    slot = lax.broadcasted_iota(jnp.int32, (2 * tj, tj), 0)
    for k in range(TILES_PER_STEP):
        @pl.when(pl.program_id(0) > 0)
        def _(k=k):
            wait_tile(k)
        perm = ((slot == pos_ref[k, 2:3, :]) | (slot == pos_ref[k, 3:4, :])).astype(bf16)
        sorted_rows = jnp.dot(perm, tok_ref[k * tj:(k + 1) * tj, :], preferred_element_type=f32)
        _store_row_tiles(sbufs[k], _pack_bf16_pairs(sorted_rows, is_bf16_exact=True))
        if k > 0:
            _tile_segments(tinfo_ref, k - 1, start_ref, sbufs[k - 1], True, xs_ref, sems.at[k - 1], tj)
    last = TILES_PER_STEP - 1
    _tile_segments(tinfo_ref, last, start_ref, sbufs[last], True, xs_ref, sems.at[last], tj)

    @pl.when(pl.program_id(0) == pl.num_programs(0) - 1)
    def _():
        for k in range(TILES_PER_STEP):
            wait_tile(k)


def _dispatch(tinfo, cnt, tok, pos, n_rows):
    ntiles, _, tj = pos.shape
    T, D = tok.shape
    n_blocks = n_rows // MOE_ROWS
    nt = TILES_PER_STEP
    assert ntiles % nt == 0
    sbuf = pltpu.VMEM(_row_tiles_shape(2 * tj, D), jnp.uint32)
    return pl.pallas_call(
        _dispatch_kernel,
        grid=(ntiles // nt,),
        in_specs=[pl.BlockSpec((nt, SUBLANES, LANES), lambda i: (i, 0, 0), memory_space=pltpu.SMEM),
                  pl.BlockSpec(memory_space=pltpu.SMEM),
                  pl.BlockSpec((nt * tj, D), lambda i: (i, 0)),
                  pl.BlockSpec((nt, SUBLANES, tj), lambda i: (i, 0, 0))],
        out_specs=[pl.BlockSpec(memory_space=pl.ANY),
                   pl.BlockSpec(memory_space=pltpu.SMEM)],
        out_shape=[jax.ShapeDtypeStruct(_row_tiles_shape(n_rows, D), jnp.uint32),
                   jax.ShapeDtypeStruct((n_blocks,), jnp.int32)],
        scratch_shapes=[[sbuf] * nt,
                        pltpu.VMEM(_row_tiles_shape(MOE_ROWS // 2, D), jnp.uint32),
                        pltpu.SMEM((N_EXPERTS,), jnp.int32), pltpu.SemaphoreType.DMA((nt,)),
                        pltpu.SemaphoreType.DMA],
        compiler_params=_cparams(("arbitrary",)),
        name="moe_dispatch",
    )(tinfo, cnt, tok, pos)


def _experts_kernel(be_ref, xs_ref, wg_ref, wu_ref, wd_ref, ys_ref, wg_s, wu_s, wd_s):
    i = pl.program_id(0)
    code = be_ref[i]
    prev = be_ref[jnp.maximum(i - 1, 0)]
    valid = code < N_EXPERTS

    @pl.when(valid & ((i == 0) | (code != prev)))
    def _():
        wg_s[...] = wg_ref[...].astype(bf16)
        wu_s[...] = wu_ref[...].astype(bf16)
        wd_s[...] = wd_ref[...].astype(bf16)

    @pl.when(valid)
    def _():
        xb = _unpack_bf16_pairs(_load_row_tiles(xs_ref)).astype(bf16)
        a = jnp.dot(xb, wg_s[...], preferred_element_type=f32)
        u = jnp.dot(xb, wu_s[...], preferred_element_type=f32)
        h = (a * jax.nn.sigmoid(a) * u).astype(bf16)
        _store_row_tiles(ys_ref, _pack_bf16_pairs(jnp.dot(h, wd_s[...], preferred_element_type=f32)))

    @pl.when(jnp.logical_not(valid))
    def _():
        ys_ref[...] = jnp.zeros_like(ys_ref)


def _experts(block_e, xs, w_gate, w_up, w_down):
    n_rows = xs.shape[0]
    _, D, De = w_gate.shape
    n_blocks = n_rows // MOE_ROWS
    blk = _row_tiles_shape(MOE_ROWS, D)
    widx = lambda i, be: (be[i] % N_EXPERTS, 0, 0)
    xidx = lambda i, be: (jnp.where(be[i] < N_EXPERTS, i, 0), 0, 0)
    return pl.pallas_call(
        _experts_kernel,
        grid_spec=pltpu.PrefetchScalarGridSpec(
            num_scalar_prefetch=1,
            grid=(n_blocks,),
            in_specs=[pl.BlockSpec(blk, xidx),
                      pl.BlockSpec((None, D, De), widx),
                      pl.BlockSpec((None, D, De), widx),
                      pl.BlockSpec((None, De, D), widx)],
            out_specs=pl.BlockSpec(blk, lambda i, be: (i, 0, 0)),
            scratch_shapes=[pltpu.VMEM((D, De), bf16), pltpu.VMEM((D, De), bf16), pltpu.VMEM((De, D), bf16)],
        ),
        out_shape=jax.ShapeDtypeStruct(xs.shape, jnp.uint32),
        compiler_params=_cparams(("arbitrary",)),
        name="moe_experts",
    )(block_e, xs, w_gate, w_up, w_down)


def _combine_kernel(tinfo_ref, tnext_ref, cnt_ref, h2_ref, vec_ref, g_ref, ys_ref, o_ref,
                    gbufs, stages, start_ref, sems):
    tj = vec_ref.shape[0] // TILES_PER_STEP
    step = pl.program_id(0)

    def fetch(info_ref, t, buf):
        _tile_segments(info_ref, t, start_ref, gbufs[buf], False, ys_ref, sems.at[buf], tj)

    @pl.when(step == 0)
    def _():
        _pad_starts(cnt_ref, start_ref)
        for t in range(GATHER_AHEAD):
            fetch(tinfo_ref, t, t)

    slot = lax.broadcasted_iota(jnp.int32, (tj, 2 * tj), 1)
    for k in range(TILES_PER_STEP):
        pltpu.make_async_copy(ys_ref.at[pl.ds(0, 2 * tj)], gbufs[k], sems.at[k]).wait()
        ahead = k + GATHER_AHEAD
        if ahead < TILES_PER_STEP:
            fetch(tinfo_ref, ahead, ahead)
        else:
            fetch(tnext_ref, ahead - TILES_PER_STEP, ahead - TILES_PER_STEP)
        rows = _unpack_bf16_pairs(_load_row_tiles(gbufs[k])).astype(bf16)
        v = vec_ref[k * tj:(k + 1) * tj, :]
        y = h2_ref[k * tj:(k + 1) * tj, :]
        for c in range(2):
            pick = (slot == v[:, 2 + c:3 + c].astype(jnp.int32)).astype(bf16)
            y = y + v[:, c:c + 1] * jnp.dot(pick, rows, preferred_element_type=f32)
        out = _rms(y, g_ref[...])
        stage = stages[k // 2]
        for q in range(stage.shape[0]):
            stage[q, pl.ds(k % 2, tj, stride=2), :] = out[:, q * LANES:(q + 1) * LANES]
        if k % 2 == 1:
            r0 = (k // 2) * 2 * tj
            for q in range(stage.shape[0]):
                o_ref[0, r0:r0 + 2 * tj, q * LANES:(q + 1) * LANES] = stage[q]


def _combine(tinfo, cnt, h2, vec, final_g, ys, B, L):
    T, D = h2.shape
    ntiles = tinfo.shape[0]
    tj = T // ntiles
    nt = TILES_PER_STEP
    assert GATHER_AHEAD < nt and nt % 2 == 0 and (L // tj) % nt == 0
    nsteps = ntiles // nt
    per_seq = L // (nt * tj)
    g = final_g.reshape(1, D)
    gbuf = pltpu.VMEM(_row_tiles_shape(2 * tj, D), jnp.uint32)
    return pl.pallas_call(
        _combine_kernel,
        grid=(nsteps,),
        in_specs=[pl.BlockSpec((nt, SUBLANES, LANES), lambda i: (i, 0, 0), memory_space=pltpu.SMEM),
                  pl.BlockSpec((nt, SUBLANES, LANES), lambda i: (i + 1, 0, 0), memory_space=pltpu.SMEM),
                  pl.BlockSpec(memory_space=pltpu.SMEM),
                  pl.BlockSpec((nt * tj, D), lambda i: (i, 0)),
                  pl.BlockSpec((nt * tj, LANES), lambda i: (i, 0)),
                  pl.BlockSpec((1, D), lambda i: (0, 0)),
                  pl.BlockSpec(memory_space=pl.ANY)],
        out_specs=pl.BlockSpec((1, nt * tj, D), lambda i: (i // per_seq, i % per_seq, 0)),
        out_shape=jax.ShapeDtypeStruct((B, L, D), f32),
        scratch_shapes=[[gbuf] * nt,
                        [pltpu.VMEM((D // LANES, 2 * tj, LANES), f32)] * (nt // 2),
                        pltpu.SMEM((N_EXPERTS,), jnp.int32), pltpu.SemaphoreType.DMA((nt,))],
        compiler_params=_cparams(("arbitrary",)),
        name="moe_combine",
    )(tinfo, jnp.pad(tinfo, ((0, nt), (0, 0), (0, 0))), cnt, h2, vec, g, ys)


def kernel(x, norm1_g, w_in, conv_w, conv_b, f_w_in, f_b_in, f_w_mid, f_b_mid, f_freq, f_w_out, f_bias,
           mix_g, w_out, norm2_g, w_group, b_group, w_router, b_router, w_gate, w_up, w_down, final_g):
    B, L, D = x.shape
    depth = norm1_g.shape[0]
    assert depth == 1, "the final RMSNorm is fused into the single layer's MoE combine"
    C = D // 2
    Lh = L // 2
    N = B * C
    T = B * L
    n_rows = T * 2 + N_EXPERTS * MOE_ROWS
    cfwd, cinv, cfou = _dft_consts(L)
    i = 0
    wcat = _prep_win(w_in[i], C)
    hcat, asum = _hyena_filters(L, C, f_w_in[i], f_b_in[i], f_w_mid[i], f_b_mid[i], f_freq[i], f_w_out[i])
    kspec, k0 = _filter_spectrum(cfwd, hcat, asum, L, C)
    z, x0, u4 = _inproj(x, norm1_g[i], wcat, conv_w[i], conv_b[i], C)
    z = z.reshape(2, Lh, N)
    x0 = x0.reshape(2, Lh, N)
    u4 = u4.reshape(4, Lh, N)
    s = _conv_fwd(cfwd, z, kspec, k0, C)
    yh = _conv_inv(cinv, s, x0, z, f_bias[i], C)
    yf = _fourier_dft(cfou, u4, L, C)
    wr = jnp.concatenate([jnp.transpose(w_router[i], (1, 0, 2)).reshape(D, N_EXPERTS), w_group[i]], axis=1)
    wr = jnp.pad(wr, ((0, 0), (0, LANES - wr.shape[1])))
    br = jnp.pad(jnp.concatenate([b_router[i].reshape(-1), b_group[i]]), (0, LANES - N_EXPERTS - N_GROUPS))
    h2, tok, pos, vec, tinfo, cnt = _outproj_router(yh, yf, x, mix_g[i], w_out[i].astype(bf16), norm2_g[i],
                                                     wr, br.reshape(1, LANES), C)
    xs, block_e = _dispatch(tinfo, cnt, tok, pos, n_rows)
    ys = _experts(block_e, xs, w_gate[i], w_up[i], w_down[i])
    return _combine(tinfo, cnt, h2, vec, final_g, ys, B, L)
```

```python
import functools
import math

import jax
import jax.numpy as jnp
from jax import lax
from jax.experimental import pallas as pl
from jax.experimental.pallas import tpu as pltpu

HEAD_DIM = 64
N_GROUPS = 4
EXPERTS_PER_GROUP = 8
N_EXPERTS = N_GROUPS * EXPERTS_PER_GROUP
FILTER_BANDS = 16
DECAY_FAST_PCT = 0.3
DECAY_SLOW_PCT = 1.5
DECAY_TARGET = 1e-2
EPS = 1e-6

LANES = 128
SUBLANES = 8
DFT_ROWS = 512
MOE_ROWS = 256
TILES_PER_STEP = 4
GATHER_AHEAD = 2
SEG_SPLIT = 32
VMEM_LIMIT = 56 * 1024 * 1024

f32 = jnp.float32
bf16 = jnp.bfloat16
HI = lax.Precision.HIGHEST


def _cparams(sem):
    return pltpu.CompilerParams(dimension_semantics=sem, vmem_limit_bytes=VMEM_LIMIT)


def _tile(pref, n):
    return min(pref, n)


def _angle(prod, m):
    return (prod & (m - 1)).astype(f32) * (2.0 * math.pi / m)


def _dft_consts_kernel(fwd_ref, inv_ref, fou_ref, *, L):
    Lh = L // 2
    Lq = Lh // 2
    tr = fwd_ref.shape[1]
    r = lax.broadcasted_iota(jnp.int32, (tr, LANES), 0) + pl.program_id(0) * tr
    d = lax.broadcasted_iota(jnp.int32, (tr, LANES), 1)
    alt_r = (1 - 2 * (r & 1)).astype(f32)
    alt_d = (1 - 2 * (d & 1)).astype(f32)
    ncb = Lh // LANES
    two_r1 = 2 * r + 1

    def col_l(cb):
        c0 = cb * LANES
        q, j0 = c0 // Lq, c0 % Lq
        return 2 * j0 + q

    fams = [
        (r, 1, lambda cb: cb * LANES, L),
        (r, 2, lambda cb: 2 * cb * LANES + 1, 2 * L),
        (two_r1, 1, lambda cb: cb * LANES, 2 * L),
        (r, 2, col_l, Lh),
        (two_r1, 2, col_l, L),
    ]
    for fi, (a, s, b0, m) in enumerate(fams):
        beta = _angle(a * (s * d), m)
        tc, ts = jnp.cos(beta), jnp.sin(beta)
        b0_lane = jnp.zeros((tr, LANES), jnp.int32)
        for cb in range(ncb):
            b0_lane = jnp.where(d == cb, b0(cb), b0_lane)
        alpha = _angle(a * b0_lane, m)
        ac, asn = jnp.cos(alpha), jnp.sin(alpha)
        for cb in range(ncb):
            ca = ac[:, cb:cb + 1]
            sa = asn[:, cb:cb + 1]
            c = ca * tc - sa * ts
            sn = sa * tc + ca * ts
            sl = slice(cb * LANES, (cb + 1) * LANES)
            if fi == 0:
                fwd_ref[0, :, sl] = c.astype(bf16)
                inv_ref[0, :, sl] = c.astype(bf16)
                fwd_ref[1, :, sl] = jnp.where(r == 0, alt_d, sn).astype(bf16)
                nst = -sn
                if cb == 0:
                    nst = jnp.where(d == 0, -alt_r, nst)
                inv_ref[1, :, sl] = nst.astype(bf16)
            elif fi == 1:
                fwd_ref[2, :, sl] = c.astype(bf16)
                fwd_ref[3, :, sl] = jnp.where(r == 0, alt_d, sn).astype(bf16)
            elif fi == 2:
                inv_ref[2, :, sl] = c.astype(bf16)
                nst = -sn
                if cb == 0:
                    nst = jnp.where(d == 0, -alt_r, nst)
                inv_ref[3, :, sl] = nst.astype(bf16)
            elif fi == 3:
                fou_ref[0, :, sl] = c.astype(bf16)
                fou_ref[1, :, sl] = (-sn).astype(bf16)
            else:
                fou_ref[2, :, sl] = c.astype(bf16)
                fou_ref[3, :, sl] = (-sn).astype(bf16)


def _dft_consts(L):
    Lh = L // 2
    tr = _tile(256, Lh)
    shp = jax.ShapeDtypeStruct((4, Lh, Lh), bf16)
    spec = pl.BlockSpec((4, tr, Lh), lambda i: (0, i, 0))
    return pl.pallas_call(
        functools.partial(_dft_consts_kernel, L=L),
        grid=(Lh // tr,),
        out_specs=[spec, spec, spec],
        out_shape=[shp, shp, shp],
        compiler_params=_cparams(("parallel",)),
        name="dft_consts",
    )()


def _prep_win_kernel(w_ref, o_ref, *, C):
    w = w_ref[...]
    wf = w[:, 3 * C:]
    i = lax.broadcasted_iota(jnp.int32, (C, C), 0)
    j = lax.broadcasted_iota(jnp.int32, (C, C), 1)
    same = (i // HEAD_DIM) == (j // HEAD_DIM)
    ang = _angle((i % HEAD_DIM) * (j % HEAD_DIM), HEAD_DIM)
    bc = jnp.where(same, jnp.cos(ang), 0.0)
    bs = jnp.where(same, jnp.sin(ang), 0.0)
    o_ref[:, :3 * C] = w[:, :3 * C].astype(bf16)
    o_ref[:, 3 * C:4 * C] = jnp.dot(wf, bc, precision=HI, preferred_element_type=f32).astype(bf16)
    o_ref[:, 4 * C:] = jnp.dot(wf, bs, precision=HI, preferred_element_type=f32).astype(bf16)


def _prep_win(w_in, C):
    D = w_in.shape[0]
    tr = _tile(256, D)
    return pl.pallas_call(
        functools.partial(_prep_win_kernel, C=C),
        grid=(D // tr,),
        in_specs=[pl.BlockSpec((tr, 4 * C), lambda i: (i, 0))],
        out_specs=pl.BlockSpec((tr, 5 * C), lambda i: (i, 0)),
        out_shape=jax.ShapeDtypeStruct((D, 5 * C), bf16),
        compiler_params=_cparams(("parallel",)),
        name="prep_win",
    )(w_in)


def _filter_kernel(win_ref, bin_ref, wmid_ref, bmid_ref, freq_ref, wout_ref, h_ref, asum_ref, *, L, C):
    tr = h_ref.shape[1]
    step = pl.program_id(0)
    lane = lax.broadcasted_iota(jnp.int32, (tr, LANES), 1)
    row = lax.broadcasted_iota(jnp.int32, (tr, LANES), 0) + step * tr
    band_i = jnp.where(lane <= FILTER_BANDS, lane - 1, lane - 1 - FILTER_BANDS)
    band = 1e-4 + band_i.astype(f32) * ((FILTER_BANDS - 1 - 1e-4) / (FILTER_BANDS - 1))
    ch = lax.broadcasted_iota(jnp.int32, (tr, 2 * C), 1) % C
    max_decay = math.log(DECAY_TARGET) / DECAY_FAST_PCT
    min_decay = math.log(DECAY_TARGET) / DECAY_SLOW_PCT
    delta = jnp.abs(min_decay + ch.astype(f32) * ((max_decay - min_decay) / (C - 1)))
    is_bwd = lax.broadcasted_iota(jnp.int32, (tr, 2 * C), 1) >= C
    row_c = lax.broadcasted_iota(jnp.int32, (tr, 2 * C), 0) + step * tr

    @pl.when(step == 0)
    def _():
        asum_ref[...] = jnp.zeros_like(asum_ref)

    total = jnp.zeros((1, 2 * C), f32)
    for p in range(2):
        pos = (2 * row + p).astype(f32)
        t = pos * (1.0 / (L - 1))
        w = pos * (2.0 * math.pi / L)
        fw = band * w
        z = jnp.where(lane == 0, t,
                      jnp.where(lane <= FILTER_BANDS, jnp.cos(fw),
                                jnp.where(lane <= 2 * FILTER_BANDS, -jnp.sin(fw), 0.0)))
        h = jnp.sin(freq_ref[0:1, :] * (jnp.dot(z, win_ref[...], precision=HI, preferred_element_type=f32)
                                       + bin_ref[...]))
        for i in range(wmid_ref.shape[0]):
            h = jnp.sin(freq_ref[i + 1:i + 2, :]
                        * (jnp.dot(h, wmid_ref[i], precision=HI, preferred_element_type=f32)
                           + bmid_ref[i:i + 1, :]))
        out = jnp.dot(h, wout_ref[...], precision=HI, preferred_element_type=f32)
        tc = (2 * row_c + p).astype(f32) * (1.0 / (L - 1))
        out = out * jnp.exp(-tc * delta)
        if p == 0:
            out = jnp.where(is_bwd & (row_c == 0), 0.0, out)
        h_ref[p] = out.astype(bf16)
        total = total + jnp.sum(jnp.abs(out), axis=0, keepdims=True)
    asum_ref[0:1, :] += total


def _hyena_filters(L, C, f_w_in, f_b_in, f_w_mid, f_b_mid, f_freq, f_w_out):
    Lh = L // 2
    tr = _tile(256, Lh)
    order = f_w_in.shape[1]
    win = jnp.pad(f_w_in, ((0, LANES - f_w_in.shape[0]), (0, 0)))
    full = lambda a: pl.BlockSpec(a.shape, lambda i: (0,) * a.ndim)
    args = (win, f_b_in.reshape(1, order), f_w_mid, f_b_mid, f_freq, f_w_out)
    return pl.pallas_call(
        functools.partial(_filter_kernel, L=L, C=C),
        grid=(Lh // tr,),
        in_specs=[full(a) for a in args],
        out_specs=[pl.BlockSpec((2, tr, 2 * C), lambda i: (0, i, 0)),
                   pl.BlockSpec((SUBLANES, 2 * C), lambda i: (0, 0))],
        out_shape=[jax.ShapeDtypeStruct((2, Lh, 2 * C), bf16),
                   jax.ShapeDtypeStruct((SUBLANES, 2 * C), f32)],
        compiler_params=_cparams(("arbitrary",)),
        name="hyena_filters",
    )(*args)


def _fwd_products(c_ref, x_ref):
    xe, xo = x_ref[0], x_ref[1]
    ce = jnp.dot(c_ref[0], xe, preferred_element_type=f32)
    se = jnp.dot(c_ref[1], xe, preferred_element_type=f32)
    co = jnp.dot(c_ref[2], xo, preferred_element_type=f32)
    so = jnp.dot(c_ref[3], xo, preferred_element_type=f32)
    return ce, se, co, so


def _filter_spectrum_kernel(c_ref, h_ref, asum_ref, k_ref, k0_ref, *, L, C):
    ce, se, co, so = _fwd_products(c_ref, h_ref)
    a = asum_ref[0:1, :]
    scale = 1.0 / ((a[:, :C] + a[:, C:]) * L)
    F = lambda v: v[:, :C]
    G = lambda v: v[:, C:]
    k_ref[0] = (F(ce) + F(co) + G(ce) + G(co)) * scale
    k_ref[1] = (-(F(se) + F(so)) + (G(se) + G(so))) * scale
    k_ref[2] = (F(ce) - F(co) + G(ce) - G(co)) * scale
    k_ref[3] = ((F(se) - F(so)) - (G(se) - G(so))) * scale

    @pl.when(pl.program_id(0) == 0)
    def _():
        k0_ref[...] = jnp.zeros_like(k0_ref)
        k0_ref[0:1, :] = ((F(se) + G(se)) * scale)[0:1, :]
        k0_ref[1:2, :] = ((F(so) - G(so)) * scale)[0:1, :]


def _filter_spectrum(cfwd, hcat, asum, L, C):
    Lh = L // 2
    tm = _tile(256, Lh)
    return pl.pallas_call(
        functools.partial(_filter_spectrum_kernel, L=L, C=C),
        grid=(Lh // tm,),
        in_specs=[pl.BlockSpec((4, tm, Lh), lambda i: (0, i, 0)),
                  pl.BlockSpec((2, Lh, 2 * C), lambda i: (0, 0, 0)),
                  pl.BlockSpec((SUBLANES, 2 * C), lambda i: (0, 0))],
        out_specs=[pl.BlockSpec((4, tm, C), lambda i: (0, i, 0)),
                   pl.BlockSpec((SUBLANES, C), lambda i: (0, 0))],
        out_shape=[jax.ShapeDtypeStruct((4, Lh, C), f32),
                   jax.ShapeDtypeStruct((SUBLANES, C), f32)],
        compiler_params=_cparams(("arbitrary",)),
        name="filter_spectrum",
    )(cfwd, hcat, asum)


def _conv_fwd_kernel(c_ref, z_ref, k_ref, k0_ref, s_ref):
    ce, se, co, so = _fwd_products(c_ref, z_ref)
    zsr, zsi, zdr, zdi = ce + co, -(se + so), ce - co, se - so
    ksr, ksi, kdr, kdi = k_ref[0], k_ref[1], k_ref[2], k_ref[3]
    psr = zsr * ksr - zsi * ksi
    psi = zsr * ksi + zsi * ksr
    pdr = zdr * kdr - zdi * kdi
    pdi = zdr * kdi + zdi * kdr
    gen = (psr + pdr, psi - pdi, psr - pdr, psi + pdi)
    first = pl.program_id(1) == 0

    @pl.when(first)
    def _():
        row0 = lax.broadcasted_iota(jnp.int32, ce.shape, 0) == 0
        p0 = zsr * ksr
        pl_ = zdr * kdr
        ak, bk = k0_ref[0:1, :], k0_ref[1:2, :]
        phr = se * ak - so * bk
        phi = -(se * bk + so * ak)
        spec = (0.5 * (p0 + pl_), -phr, 0.5 * (p0 - pl_), phi)
        for i in range(4):
            s_ref[i] = jnp.where(row0, spec[i], gen[i]).astype(bf16)

    @pl.when(jnp.logical_not(first))
    def _():
        for i in range(4):
            s_ref[i] = gen[i].astype(bf16)


def _conv_fwd(cfwd, z, kspec, k0, C):
    _, Lh, N = z.shape
    tm = _tile(DFT_ROWS, Lh)
    return pl.pallas_call(
        _conv_fwd_kernel,
        grid=(N // C, Lh // tm),
        in_specs=[pl.BlockSpec((4, tm, Lh), lambda n, i: (0, i, 0)),
                  pl.BlockSpec((2, Lh, C), lambda n, i: (0, 0, n)),
                  pl.BlockSpec((4, tm, C), lambda n, i: (0, i, 0)),
                  pl.BlockSpec((SUBLANES, C), lambda n, i: (0, 0))],
        out_specs=pl.BlockSpec((4, tm, C), lambda n, i: (0, i, n)),
        out_shape=jax.ShapeDtypeStruct((4, Lh, N), bf16),
        compiler_params=_cparams(("parallel", "arbitrary")),
        name="conv_fwd",
    )(cfwd, z, kspec, k0)


def _pair_products(c_ref, r_ref):
    ev = (jnp.dot(c_ref[0], r_ref[0], preferred_element_type=f32)
          + jnp.dot(c_ref[1], r_ref[1], preferred_element_type=f32))
    od = (jnp.dot(c_ref[2], r_ref[2], preferred_element_type=f32)
          + jnp.dot(c_ref[3], r_ref[3], preferred_element_type=f32))
    return ev, od


def _conv_inv_kernel(c_ref, s_ref, x0_ref, z_ref, fb_ref, y_ref):
    conv = _pair_products(c_ref, s_ref)
    fb = fb_ref[...]
    for p in range(2):
        zp = z_ref[p].astype(f32)
        y_ref[p] = (x0_ref[p].astype(f32) * (conv[p] + zp * fb)).astype(bf16)


def _conv_inv(cinv, s, x0, z, f_bias, C):
    _, Lh, N = s.shape
    tm = _tile(DFT_ROWS, Lh)
    return pl.pallas_call(
        _conv_inv_kernel,
        grid=(N // C, Lh // tm),
        in_specs=[pl.BlockSpec((4, tm, Lh), lambda n, i: (0, i, 0)),
                  pl.BlockSpec((4, Lh, C), lambda n, i: (0, 0, n)),
                  pl.BlockSpec((2, tm, C), lambda n, i: (0, i, n)),
                  pl.BlockSpec((2, tm, C), lambda n, i: (0, i, n)),
                  pl.BlockSpec((1, C), lambda n, i: (0, 0))],
        out_specs=pl.BlockSpec((2, tm, C), lambda n, i: (0, i, n)),
        out_shape=jax.ShapeDtypeStruct((2, Lh, N), bf16),
        compiler_params=_cparams(("parallel", "arbitrary")),
        name="conv_inv",
    )(cinv, s, x0, z, f_bias.reshape(1, C))


def _fourier_kernel(c_ref, u_ref, y_ref, *, scale):
    ev, od = _pair_products(c_ref, u_ref)
    y_ref[0] = (ev * scale).astype(bf16)
    y_ref[1] = (od * scale).astype(bf16)


def _fourier_dft(cfou, u4, L, C):
    _, Lh, N = u4.shape
    tm = _tile(DFT_ROWS, Lh)
    return pl.pallas_call(
        functools.partial(_fourier_kernel, scale=1.0 / math.sqrt(L * HEAD_DIM)),
        grid=(N // C, Lh // tm),
        in_specs=[pl.BlockSpec((4, tm, Lh), lambda n, i: (0, i, 0)),
                  pl.BlockSpec((4, Lh, C), lambda n, i: (0, 0, n))],
        out_specs=pl.BlockSpec((2, tm, C), lambda n, i: (0, i, n)),
        out_shape=jax.ShapeDtypeStruct((2, Lh, N), bf16),
        compiler_params=_cparams(("parallel", "arbitrary")),
        name="fourier_dft",
    )(cfou, u4)


def _rms(x, g):
    return x * lax.rsqrt(jnp.mean(x * x, axis=-1, keepdims=True) + EPS) * g


def _stage_rows(chunks, x):
    for k in range(chunks.shape[0]):
        chunks[k] = x[:, k * LANES:(k + 1) * LANES]


def _parity_rows(chunks, p, n):
    return jnp.concatenate([chunks[k, pl.ds(p, n, stride=2), :] for k in range(chunks.shape[0])], axis=1)


def _inproj_kernel(xa_ref, xb_ref, pa_ref, na_ref, pb_ref, nb_ref, g_ref, w_ref, cw_ref, cb_ref,
                   z_ref, x0_ref, u_ref, stage, *, C, D):
    tj = xa_ref.shape[1] // 2
    jt = pl.program_id(1)
    n_half = pl.num_programs(1)
    g = g_ref[...]
    w = w_ref[...]
    cw0, cw1, cw2 = cw_ref[0:1, :], cw_ref[1:2, :], cw_ref[2:3, :]
    cb = cb_ref[...]
    row = lax.broadcasted_iota(jnp.int32, (tj, 3 * C), 0)
    four = []
    halves = ((xa_ref, pa_ref, na_ref, jt == 0, False),
              (xb_ref, pb_ref, nb_ref, False, jt == n_half - 1))
    for h, (x_ref, p_ref, n_ref, at_start, at_end) in enumerate(halves):
        _stage_rows(stage, x_ref[0])
        lhs = jnp.concatenate([
            _rms(_parity_rows(stage, 0, tj), g).astype(bf16),
            _rms(_parity_rows(stage, 1, tj), g).astype(bf16),
            _rms(p_ref[0], g).astype(bf16),
            _rms(n_ref[0], g).astype(bf16)], axis=0)
        res = jnp.dot(lhs, w, preferred_element_type=f32)
        pe, po = res[:tj, :3 * C], res[tj:2 * tj, :3 * C]
        prev = res[2 * tj + SUBLANES - 1:2 * tj + SUBLANES, :3 * C]
        nxt = res[2 * tj + SUBLANES:2 * tj + SUBLANES + 1, :3 * C]
        if at_start is not False:
            prev = jnp.where(at_start, 0.0, prev)
        if at_end is not False:
            nxt = jnp.where(at_end, 0.0, nxt)
        po_dn = jnp.where(row == 0, prev, pltpu.roll(po, 1, 0))
        pe_up = jnp.where(row == tj - 1, nxt, pltpu.roll(pe, tj - 1, 0))
        uc = (cb + cw0 * po_dn + cw1 * pe + cw2 * po,
              cb + cw0 * pe + cw1 * po + cw2 * pe_up)
        for p in range(2):
            x0_ref[p, h] = uc[p][:, :C].astype(bf16)
            z_ref[p, h] = (uc[p][:, C:2 * C] * uc[p][:, 2 * C:]).astype(bf16)
        four.append((res[:tj, 3 * C:], res[tj:2 * tj, 3 * C:]))
    for p in range(2):
        fa, fb = four[0][p], four[1][p]
        u_ref[0, p] = (fa[:, :C] + fb[:, :C]).astype(bf16)
        u_ref[1, p] = (fa[:, C:] + fb[:, C:]).astype(bf16)
        u_ref[2, p] = (fa[:, :C] - fb[:, :C]).astype(bf16)
        u_ref[3, p] = (fa[:, C:] - fb[:, C:]).astype(bf16)


def _inproj(x, norm_g, wcat, conv_w, conv_b, C):
    B, L, D = x.shape
    Lq = L // 4
    tj = _tile(256, Lq)
    nq = Lq // tj
    N = B * C
    hb = 2 * tj // SUBLANES
    last = L // SUBLANES - 1
    main = lambda off: pl.BlockSpec((1, 2 * tj, D), lambda b, j: (b, j + off, 0))
    prev = lambda off: pl.BlockSpec((1, SUBLANES, D), lambda b, j: (b, jnp.maximum((j + off) * hb - 1, 0), 0))
    nxt = lambda off: pl.BlockSpec((1, SUBLANES, D), lambda b, j: (b, jnp.minimum((j + off + 1) * hb, last), 0))
    full = lambda a: pl.BlockSpec(a.shape, lambda b, j: (0,) * a.ndim)
    g = norm_g.reshape(1, D)
    cb = conv_b.reshape(1, 3 * C)
    return pl.pallas_call(
        functools.partial(_inproj_kernel, C=C, D=D),
        grid=(B, nq),
        in_specs=[main(0), main(nq), prev(0), nxt(0), prev(nq), nxt(nq),
                  full(g), full(wcat), full(conv_w), full(cb)],
        out_specs=[pl.BlockSpec((2, 2, tj, C), lambda b, j: (0, 0, j, b)),
                   pl.BlockSpec((2, 2, tj, C), lambda b, j: (0, 0, j, b)),
                   pl.BlockSpec((4, 2, tj, C), lambda b, j: (0, 0, j, b))],
        out_shape=[jax.ShapeDtypeStruct((2, 2, Lq, N), bf16),
                   jax.ShapeDtypeStruct((2, 2, Lq, N), bf16),
                   jax.ShapeDtypeStruct((4, 2, Lq, N), bf16)],
        scratch_shapes=[pltpu.VMEM((D // LANES, 2 * tj, LANES), f32)],
        compiler_params=_cparams(("parallel", "arbitrary")),
        name="inproj",
    )(x, x, x, x, x, x, g, wcat, conv_w, cb)


def _pack_bf16_pairs(v, is_bf16_exact=False):
    k = v.shape[1] // 2
    rounded = v if is_bf16_exact else v.astype(bf16).astype(f32)
    hi = pltpu.bitcast(rounded[:, :k], jnp.uint32)
    lo = pltpu.bitcast(rounded[:, k:], jnp.uint32)
    return hi | (lo >> 16)


def _unpack_bf16_pairs(u):
    hi = pltpu.bitcast(u & jnp.uint32(0xFFFF0000), f32)
    lo = pltpu.bitcast(u << 16, f32)
    return jnp.concatenate([hi, lo], axis=1)


def _row_tiles_shape(n, D):
    return (n, D // (2 * LANES), LANES)


def _store_row_tiles(ref, packed):
    n, per_row, _ = ref.shape
    flat = ref.reshape(n * per_row, LANES)
    for s in range(per_row):
        flat[pl.ds(s, n, stride=per_row), :] = packed[:, s * LANES:(s + 1) * LANES]


def _load_row_tiles(ref):
    n, per_row, _ = ref.shape
    flat = ref.reshape(n * per_row, LANES)
    return jnp.concatenate([flat[pl.ds(s, n, stride=per_row), :] for s in range(per_row)], axis=1)


def _outproj_router_kernel(yh_ref, yf_ref, x_ref, mg_ref, wo_ref, g2_ref, wr_ref, br_ref,
                           h2_ref, tok_ref, pos_ref, vec_ref, tinfo_ref, cnt_ref, carry_ref, stage, *, C):
    tj = yh_ref.shape[1]
    _stage_rows(stage, x_ref[0])

    @pl.when((pl.program_id(0) == 0) & (pl.program_id(1) == 0))
    def _():
        carry_ref[...] = jnp.zeros_like(carry_ref)

    li = lax.broadcasted_iota(jnp.int32, (LANES, LANES), 0)
    lj = lax.broadcasted_iota(jnp.int32, (LANES, LANES), 1)
    avg = jnp.where(li // HEAD_DIM == lj // HEAD_DIM, 1.0 / HEAD_DIM, 0.0).astype(bf16)
    upper = (li < lj).astype(bf16)
    ri = lax.broadcasted_iota(jnp.int32, (tj, tj), 0)
    ci = lax.broadcasted_iota(jnp.int32, (tj, tj), 1)
    tri = (ci < ri).astype(bf16)
    wr = wr_ref[...]
    wr_hi = wr.astype(bf16)
    wr_lo = (wr - wr_hi.astype(f32)).astype(bf16)
    lane = lax.broadcasted_iota(jnp.int32, (tj, LANES), 1)
    sub = lax.broadcasted_iota(jnp.int32, (SUBLANES, LANES), 0)
    neg = jnp.float32(-jnp.inf)
    big = jnp.int32(LANES)

    def first_argmax(vals, mask):
        v = jnp.where(mask, vals, neg)
        m = jnp.max(v, axis=1, keepdims=True)
        idx = jnp.min(jnp.where(mask & (v == m), lane, big), axis=1, keepdims=True)
        return m, idx

    for p in range(2):
        rows = slice(p * tj, (p + 1) * tj)
        parts = []
        for ref in (yh_ref, yf_ref):
            y = ref[p].astype(f32)
            ysq = (y * y).astype(bf16)
            ms = jnp.concatenate([jnp.dot(ysq[:, k:k + LANES], avg, preferred_element_type=f32)
                                  for k in range(0, C, LANES)], axis=1)
            parts.append(y * lax.rsqrt(ms + EPS))
        yn = (jnp.concatenate(parts, axis=1) * mg_ref[...]).astype(bf16)
        h2 = _parity_rows(stage, p, tj) + jnp.dot(yn, wo_ref[...], preferred_element_type=f32)
        h2_ref[rows, :] = h2
        tok = _rms(h2, g2_ref[...])
        tok_ref[rows, :] = tok.astype(bf16)

        t_hi = tok.astype(bf16)
        t_lo = (tok - t_hi.astype(f32)).astype(bf16)
        logits = (jnp.dot(t_hi, wr_hi, preferred_element_type=f32)
                  + jnp.dot(t_lo, wr_hi, preferred_element_type=f32)
                  + jnp.dot(t_hi, wr_lo, preferred_element_type=f32)) + br_ref[...]

        gmask = (lane >= N_EXPERTS) & (lane < N_EXPERTS + N_GROUPS)
        gmax, gidx = first_argmax(logits, gmask)
        gsel = gidx - N_EXPERTS
        p_group = 1.0 / jnp.sum(jnp.where(gmask, jnp.exp(logits - gmax), 0.0), axis=1, keepdims=True)
        emask = (lane >= gsel * EXPERTS_PER_GROUP) & (lane < (gsel + 1) * EXPERTS_PER_GROUP)
        l0, e0 = first_argmax(logits, emask)
        l1, e1 = first_argmax(logits, emask & (lane != e0))
        ex = jnp.exp(l1 - l0)
        w0 = p_group / (1.0 + ex)
        w1 = p_group * ex / (1.0 + ex)

        oh0 = (lane == e0).astype(f32)
        oh1 = (lane == e1).astype(f32)
        both = oh0 + oh1
        n_e = jnp.sum(both, axis=0, keepdims=True)
        n_e8 = jnp.broadcast_to(n_e, (SUBLANES, LANES))
        tile_off = jnp.dot(n_e8.astype(bf16), upper, preferred_element_type=f32)[0:1, :]
        where = jnp.dot(tri, both.astype(bf16), preferred_element_type=f32) + tile_off
        pos0 = jnp.sum(oh0 * where, axis=1, keepdims=True)
        pos1 = jnp.sum(oh1 * where, axis=1, keepdims=True)
        base = carry_ref[0:1, :]
        long_seg = (jnp.max(n_e, axis=1, keepdims=True) >= SEG_SPLIT).astype(f32)
        tinfo_ref[p] = jnp.where(sub == 0, n_e8, jnp.where(sub == 1, base, jnp.where(sub == 2, long_seg, 0.0))
                                 ).astype(jnp.int32)
        carry_ref[0:1, :] = base + n_e

        cols = jnp.where(lane == 0, w0, jnp.where(lane == 1, w1,
                         jnp.where(lane == 2, pos0, jnp.where(lane == 3, pos1, 0.0))))
        vec_ref[rows, :] = cols
        pos_ref[p] = cols.T[:SUBLANES, :].astype(jnp.int32)
    cnt_ref[...] = jnp.broadcast_to(carry_ref[0:1, :], cnt_ref.shape).astype(jnp.int32)


def _outproj_router(yh, yf, x, mix_g, wo, norm2_g, wr, br, C):
    B, L, D = x.shape
    Lh = L // 2
    tj = _tile(256, Lh)
    nj = Lh // tj
    T = B * L
    ntiles = T // tj
    lin = lambda b, j: b * nj + j
    full = lambda a: pl.BlockSpec(a.shape, lambda b, j: (0,) * a.ndim)
    mg = mix_g.reshape(1, D)
    g2 = norm2_g.reshape(1, D)
    return pl.pallas_call(
        functools.partial(_outproj_router_kernel, C=C),
        grid=(B, nj),
        in_specs=[pl.BlockSpec((2, tj, C), lambda b, j: (0, j, b)),
                  pl.BlockSpec((2, tj, C), lambda b, j: (0, j, b)),
                  pl.BlockSpec((1, 2 * tj, D), lambda b, j: (b, j, 0)),
                  full(mg), full(wo), full(g2), full(wr), full(br)],
        out_specs=[pl.BlockSpec((2 * tj, D), lambda b, j: (lin(b, j), 0)),
                   pl.BlockSpec((2 * tj, D), lambda b, j: (lin(b, j), 0)),
                   pl.BlockSpec((2, SUBLANES, tj), lambda b, j: (lin(b, j), 0, 0)),
                   pl.BlockSpec((2 * tj, LANES), lambda b, j: (lin(b, j), 0)),
                   pl.BlockSpec((2, SUBLANES, LANES), lambda b, j: (lin(b, j), 0, 0)),
                   pl.BlockSpec((SUBLANES, LANES), lambda b, j: (0, 0))],
        out_shape=[jax.ShapeDtypeStruct((T, D), f32),
                   jax.ShapeDtypeStruct((T, D), bf16),
                   jax.ShapeDtypeStruct((ntiles, SUBLANES, tj), jnp.int32),
                   jax.ShapeDtypeStruct((T, LANES), f32),
                   jax.ShapeDtypeStruct((ntiles, SUBLANES, LANES), jnp.int32),
                   jax.ShapeDtypeStruct((SUBLANES, LANES), jnp.int32)],
        scratch_shapes=[pltpu.VMEM((SUBLANES, LANES), f32), pltpu.VMEM((D // LANES, 2 * tj, LANES), f32)],
        compiler_params=_cparams(("arbitrary", "arbitrary")),
        name="outproj_router",
    )(yh, yf, x, mg, wo, g2, wr, br)


def _padded(c):
    return ((c + MOE_ROWS - 1) // MOE_ROWS) * MOE_ROWS


def _pad_starts(cnt_ref, start_ref):
    def body(e, acc):
        start_ref[e] = acc
        return acc + _padded(cnt_ref[0, e])
    return lax.fori_loop(0, N_EXPERTS, body, jnp.int32(0))


def _segment_copies(n, src, src_off, dst, dst_off, sem, max_rows, wait, src_advances=True, min_rows=1):
    bit = max_rows
    while bit >= min_rows:
        take = n & bit

        @pl.when(take != 0)
        def _(src_off=src_off, dst_off=dst_off, bit=bit):
            cp = pltpu.make_async_copy(src.at[pl.ds(src_off, bit)], dst.at[pl.ds(dst_off, bit)], sem)
            if wait:
                cp.wait()
            else:
                cp.start()
        if src_advances:
            src_off = src_off + take
        dst_off = dst_off + take
        bit //= 2


def _tile_segments(tinfo_ref, t, start_ref, local, local_is_src, remote, sem, tj):
    def copies(long_part):
        off = 0
        for e in range(N_EXPERTS):
            n = tinfo_ref[t, 0, e]
            far = start_ref[e] + tinfo_ref[t, 1, e]
            if long_part:
                args = (n, off, far, tj, SEG_SPLIT)
            else:
                head = n & ~(SEG_SPLIT - 1)
                args = (n, off + head, far + head, SEG_SPLIT // 2, 1)
            n_, near_, far_, hi, lo = args
            if local_is_src:
                _segment_copies(n_, local, near_, remote, far_, sem, hi, wait=False, min_rows=lo)
            else:
                _segment_copies(n_, remote, far_, local, near_, sem, hi, wait=False, min_rows=lo)
            off = off + n

    copies(long_part=False)

    @pl.when(tinfo_ref[t, 2, 0] != 0)
    def _():
        copies(long_part=True)


def _dispatch_kernel(tinfo_ref, cnt_ref, tok_ref, pos_ref, xs_ref, be_ref, sbufs, zbuf, start_ref, sems, zsem):
    tj = pos_ref.shape[2]
    n_blocks = be_ref.shape[0]

    @pl.when(pl.program_id(0) == 0)
    def _():
        total = _pad_starts(cnt_ref, start_ref)
        used = total // MOE_ROWS

        def fill(i, e):
            def cond(e):
                nxt = jnp.where(e + 1 < N_EXPERTS, start_ref[jnp.minimum(e + 1, N_EXPERTS - 1)], total)
                return (e + 1 < N_EXPERTS) & (i * MOE_ROWS >= nxt)
            e = lax.while_loop(cond, lambda e: e + 1, e)
            be_ref[i] = jnp.where(i < used, e, e + N_EXPERTS)
            return e
        lax.fori_loop(0, n_blocks, fill, jnp.int32(0))

        zbuf[...] = jnp.zeros_like(zbuf)
        zrows = zbuf.shape[0]
        for wait in (False, True):
            def zero_pad(e, c, wait=wait):
                cnt = cnt_ref[0, e]
                _segment_copies(_padded(cnt) - cnt, zbuf, 0, xs_ref, start_ref[e] + cnt, zsem,
                                zrows, wait, src_advances=False)
                return c
            lax.fori_loop(0, N_EXPERTS, zero_pad, 0)

            def zero_tail(i, c, wait=wait):
                cp = pltpu.make_async_copy(zbuf, xs_ref.at[pl.ds(total + i * zrows, zrows)], zsem)
                if wait:
                    cp.wait()
                else:
                    cp.start()
                return c
            lax.fori_loop(0, (n_blocks - used) * (MOE_ROWS // zrows), zero_tail, 0)

    def wait_tile(k):
        pltpu.make_async_copy(sbufs[k], xs_ref.at[pl.ds(0, 2 * tj)], sems.at[k]).wait()

    slot = lax.broadcasted_iota(jnp.int32, (2 * tj, tj), 0)
    for k in range(TILES_PER_STEP):
        @pl.when(pl.program_id(0) > 0)
        def _(k=k):
            wait_tile(k)
        perm = ((slot == pos_ref[k, 2:3, :]) | (slot == pos_ref[k, 3:4, :])).astype(bf16)
        sorted_rows = jnp.dot(perm, tok_ref[k * tj:(k + 1) * tj, :], preferred_element_type=f32)
        _store_row_tiles(sbufs[k], _pack_bf16_pairs(sorted_rows, is_bf16_exact=True))
        if k > 0:
            _tile_segments(tinfo_ref, k - 1, start_ref, sbufs[k - 1], True, xs_ref, sems.at[k - 1], tj)
    last = TILES_PER_STEP - 1
    _tile_segments(tinfo_ref, last, start_ref, sbufs[last], True, xs_ref, sems.at[last], tj)

    @pl.when(pl.program_id(0) == pl.num_programs(0) - 1)
    def _():
        for k in range(TILES_PER_STEP):
            wait_tile(k)


def _dispatch(tinfo, cnt, tok, pos, n_rows):
    ntiles, _, tj = pos.shape
    T, D = tok.shape
    n_blocks = n_rows // MOE_ROWS
    nt = TILES_PER_STEP
    assert ntiles % nt == 0
    sbuf = pltpu.VMEM(_row_tiles_shape(2 * tj, D), jnp.uint32)
    return pl.pallas_call(
        _dispatch_kernel,
        grid=(ntiles // nt,),
        in_specs=[pl.BlockSpec((nt, SUBLANES, LANES), lambda i: (i, 0, 0), memory_space=pltpu.SMEM),
                  pl.BlockSpec(memory_space=pltpu.SMEM),
                  pl.BlockSpec((nt * tj, D), lambda i: (i, 0)),
                  pl.BlockSpec((nt, SUBLANES, tj), lambda i: (i, 0, 0))],
        out_specs=[pl.BlockSpec(memory_space=pl.ANY),
                   pl.BlockSpec(memory_space=pltpu.SMEM)],
        out_shape=[jax.ShapeDtypeStruct(_row_tiles_shape(n_rows, D), jnp.uint32),
                   jax.ShapeDtypeStruct((n_blocks,), jnp.int32)],
        scratch_shapes=[[sbuf] * nt,
                        pltpu.VMEM(_row_tiles_shape(MOE_ROWS // 2, D), jnp.uint32),
                        pltpu.SMEM((N_EXPERTS,), jnp.int32), pltpu.SemaphoreType.DMA((nt,)),
                        pltpu.SemaphoreType.DMA],
        compiler_params=_cparams(("arbitrary",)),
        name="moe_dispatch",
    )(tinfo, cnt, tok, pos)


def _experts_kernel(be_ref, xs_ref, wg_ref, wu_ref, wd_ref, ys_ref, wg_s, wu_s, wd_s):
    i = pl.program_id(0)
    code = be_ref[i]
    prev = be_ref[jnp.maximum(i - 1, 0)]
    valid = code < N_EXPERTS

    @pl.when(valid & ((i == 0) | (code != prev)))
    def _():
        wg_s[...] = wg_ref[...].astype(bf16)
        wu_s[...] = wu_ref[...].astype(bf16)
        wd_s[...] = wd_ref[...].astype(bf16)

    @pl.when(valid)
    def _():
        half = xs_ref.shape[0] // 2
        halves = [pl.ds(r * half, half) for r in range(2)]
        gated = []
        for rows in halves:
            xb = _unpack_bf16_pairs(_load_row_tiles(xs_ref.at[rows])).astype(bf16)
            gated.append((jnp.dot(xb, wg_s[...], preferred_element_type=f32),
                          jnp.dot(xb, wu_s[...], preferred_element_type=f32)))
        for rows, (a, u) in zip(halves, gated):
            h = (a * jax.nn.sigmoid(a) * u).astype(bf16)
            _store_row_tiles(ys_ref.at[rows], _pack_bf16_pairs(jnp.dot(h, wd_s[...], preferred_element_type=f32)))

    @pl.when(jnp.logical_not(valid))
    def _():
        ys_ref[...] = jnp.zeros_like(ys_ref)


def _experts(block_e, xs, w_gate, w_up, w_down):
    n_rows = xs.shape[0]
    _, D, De = w_gate.shape
    n_blocks = n_rows // MOE_ROWS
    blk = _row_tiles_shape(MOE_ROWS, D)
    widx = lambda i, be: (be[i] % N_EXPERTS, 0, 0)
    xidx = lambda i, be: (jnp.where(be[i] < N_EXPERTS, i, 0), 0, 0)
    return pl.pallas_call(
        _experts_kernel,
        grid_spec=pltpu.PrefetchScalarGridSpec(
            num_scalar_prefetch=1,
            grid=(n_blocks,),
            in_specs=[pl.BlockSpec(blk, xidx),
                      pl.BlockSpec((None, D, De), widx),
                      pl.BlockSpec((None, D, De), widx),
                      pl.BlockSpec((None, De, D), widx)],
            out_specs=pl.BlockSpec(blk, lambda i, be: (i, 0, 0)),
            scratch_shapes=[pltpu.VMEM((D, De), bf16), pltpu.VMEM((D, De), bf16), pltpu.VMEM((De, D), bf16)],
        ),
        out_shape=jax.ShapeDtypeStruct(xs.shape, jnp.uint32),
        compiler_params=_cparams(("arbitrary",)),
        name="moe_experts",
    )(block_e, xs, w_gate, w_up, w_down)


def _combine_kernel(tinfo_ref, tnext_ref, cnt_ref, h2_ref, vec_ref, g_ref, ys_ref, o_ref,
                    gbufs, stages, start_ref, sems):
    tj = vec_ref.shape[0] // TILES_PER_STEP
    step = pl.program_id(0)

    def fetch(info_ref, t, buf):
        _tile_segments(info_ref, t, start_ref, gbufs[buf], False, ys_ref, sems.at[buf], tj)

    @pl.when(step == 0)
    def _():
        _pad_starts(cnt_ref, start_ref)
        for t in range(GATHER_AHEAD):
            fetch(tinfo_ref, t, t)

    slot = lax.broadcasted_iota(jnp.int32, (tj, 2 * tj), 1)
    for k in range(TILES_PER_STEP):
        pltpu.make_async_copy(ys_ref.at[pl.ds(0, 2 * tj)], gbufs[k], sems.at[k]).wait()
        ahead = k + GATHER_AHEAD
        if ahead < TILES_PER_STEP:
            fetch(tinfo_ref, ahead, ahead)
        else:
            fetch(tnext_ref, ahead - TILES_PER_STEP, ahead - TILES_PER_STEP)
        rows = _unpack_bf16_pairs(_load_row_tiles(gbufs[k])).astype(bf16)
        v = vec_ref[k * tj:(k + 1) * tj, :]
        y = h2_ref[k * tj:(k + 1) * tj, :]
        for c in range(2):
            pick = (slot == v[:, 2 + c:3 + c].astype(jnp.int32)).astype(bf16)
            y = y + v[:, c:c + 1] * jnp.dot(pick, rows, preferred_element_type=f32)
        out = _rms(y, g_ref[...])
        stage = stages[k // 2]
        for q in range(stage.shape[0]):
            stage[q, pl.ds(k % 2, tj, stride=2), :] = out[:, q * LANES:(q + 1) * LANES]
        if k % 2 == 1:
            r0 = (k // 2) * 2 * tj
            for q in range(stage.shape[0]):
                o_ref[0, r0:r0 + 2 * tj, q * LANES:(q + 1) * LANES] = stage[q]


def _combine(tinfo, cnt, h2, vec, final_g, ys, B, L):
    T, D = h2.shape
    ntiles = tinfo.shape[0]
    tj = T // ntiles
    nt = TILES_PER_STEP
    assert GATHER_AHEAD < nt and nt % 2 == 0 and (L // tj) % nt == 0
    nsteps = ntiles // nt
    per_seq = L // (nt * tj)
    g = final_g.reshape(1, D)
    gbuf = pltpu.VMEM(_row_tiles_shape(2 * tj, D), jnp.uint32)
    return pl.pallas_call(
        _combine_kernel,
        grid=(nsteps,),
        in_specs=[pl.BlockSpec((nt, SUBLANES, LANES), lambda i: (i, 0, 0), memory_space=pltpu.SMEM),
                  pl.BlockSpec((nt, SUBLANES, LANES), lambda i: (i + 1, 0, 0), memory_space=pltpu.SMEM),
                  pl.BlockSpec(memory_space=pltpu.SMEM),
                  pl.BlockSpec((nt * tj, D), lambda i: (i, 0)),
                  pl.BlockSpec((nt * tj, LANES), lambda i: (i, 0)),
                  pl.BlockSpec((1, D), lambda i: (0, 0)),
                  pl.BlockSpec(memory_space=pl.ANY)],
        out_specs=pl.BlockSpec((1, nt * tj, D), lambda i: (i // per_seq, i % per_seq, 0)),
        out_shape=jax.ShapeDtypeStruct((B, L, D), f32),
        scratch_shapes=[[gbuf] * nt,
                        [pltpu.VMEM((D // LANES, 2 * tj, LANES), f32)] * (nt // 2),
                        pltpu.SMEM((N_EXPERTS,), jnp.int32), pltpu.SemaphoreType.DMA((nt,))],
        compiler_params=_cparams(("arbitrary",)),
        name="moe_combine",
    )(tinfo, jnp.pad(tinfo, ((0, nt), (0, 0), (0, 0))), cnt, h2, vec, g, ys)


def kernel(x, norm1_g, w_in, conv_w, conv_b, f_w_in, f_b_in, f_w_mid, f_b_mid, f_freq, f_w_out, f_bias,
           mix_g, w_out, norm2_g, w_group, b_group, w_router, b_router, w_gate, w_up, w_down, final_g):
    B, L, D = x.shape
    depth = norm1_g.shape[0]
    assert depth == 1, "the final RMSNorm is fused into the single layer's MoE combine"
    C = D // 2
    Lh = L // 2
    N = B * C
    T = B * L
    n_rows = T * 2 + N_EXPERTS * MOE_ROWS
    cfwd, cinv, cfou = _dft_consts(L)
    i = 0
    wcat = _prep_win(w_in[i], C)
    hcat, asum = _hyena_filters(L, C, f_w_in[i], f_b_in[i], f_w_mid[i], f_b_mid[i], f_freq[i], f_w_out[i])
    kspec, k0 = _filter_spectrum(cfwd, hcat, asum, L, C)
    z, x0, u4 = _inproj(x, norm1_g[i], wcat, conv_w[i], conv_b[i], C)
    z = z.reshape(2, Lh, N)
    x0 = x0.reshape(2, Lh, N)
    u4 = u4.reshape(4, Lh, N)
    s = _conv_fwd(cfwd, z, kspec, k0, C)
    yh = _conv_inv(cinv, s, x0, z, f_bias[i], C)
    yf = _fourier_dft(cfou, u4, L, C)
    wr = jnp.concatenate([jnp.transpose(w_router[i], (1, 0, 2)).reshape(D, N_EXPERTS), w_group[i]], axis=1)
    wr = jnp.pad(wr, ((0, 0), (0, LANES - wr.shape[1])))
    br = jnp.pad(jnp.concatenate([b_router[i].reshape(-1), b_group[i]]), (0, LANES - N_EXPERTS - N_GROUPS))
    h2, tok, pos, vec, tinfo, cnt = _outproj_router(yh, yf, x, mix_g[i], w_out[i].astype(bf16), norm2_g[i],
                                                     wr, br.reshape(1, LANES), C)
    xs, block_e = _dispatch(tinfo, cnt, tok, pos, n_rows)
    ys = _experts(block_e, xs, w_gate[i], w_up[i], w_down[i])
    return _combine(tinfo, cnt, h2, vec, final_g, ys, B, L)
```

```python
import functools
import math

import jax
import jax.numpy as jnp
from jax import lax
from jax.experimental import pallas as pl
from jax.experimental.pallas import tpu as pltpu

HEAD_DIM = 64
N_GROUPS = 4
EXPERTS_PER_GROUP = 8
N_EXPERTS = N_GROUPS * EXPERTS_PER_GROUP
FILTER_BANDS = 16
DECAY_FAST_PCT = 0.3
DECAY_SLOW_PCT = 1.5
DECAY_TARGET = 1e-2
EPS = 1e-6

LANES = 128
SUBLANES = 8
MXU_DIM = 256
DFT_ROWS = 512
MOE_ROWS = 256
ROW_TILES_PER_STEP = 2
TILES_PER_STEP = 4
GATHER_AHEAD = 2
SEG_SPLIT = 32
VMEM_LIMIT = 56 * 1024 * 1024

f32 = jnp.float32
bf16 = jnp.bfloat16
HI = lax.Precision.HIGHEST


def _cparams(sem):
    return pltpu.CompilerParams(dimension_semantics=sem, vmem_limit_bytes=VMEM_LIMIT)


def _tile(pref, n):
    return min(pref, n)


def _angle(prod, m):
    return (prod & (m - 1)).astype(f32) * (2.0 * math.pi / m)


def _dft_consts_kernel(fwd_ref, inv_ref, fou_ref, *, L):
    Lh = L // 2
    Lq = Lh // 2
    tr = fwd_ref.shape[1]
    r = lax.broadcasted_iota(jnp.int32, (tr, LANES), 0) + pl.program_id(0) * tr
    d = lax.broadcasted_iota(jnp.int32, (tr, LANES), 1)
    alt_r = (1 - 2 * (r & 1)).astype(f32)
    alt_d = (1 - 2 * (d & 1)).astype(f32)
    ncb = Lh // LANES
    two_r1 = 2 * r + 1

    def col_l(cb):
        c0 = cb * LANES
        q, j0 = c0 // Lq, c0 % Lq
        return 2 * j0 + q

    fams = [
        (r, 1, lambda cb: cb * LANES, L),
        (r, 2, lambda cb: 2 * cb * LANES + 1, 2 * L),
        (two_r1, 1, lambda cb: cb * LANES, 2 * L),
        (r, 2, col_l, Lh),
        (two_r1, 2, col_l, L),
    ]
    for fi, (a, s, b0, m) in enumerate(fams):
        beta = _angle(a * (s * d), m)
        tc, ts = jnp.cos(beta), jnp.sin(beta)
        b0_lane = jnp.zeros((tr, LANES), jnp.int32)
        for cb in range(ncb):
            b0_lane = jnp.where(d == cb, b0(cb), b0_lane)
        alpha = _angle(a * b0_lane, m)
        ac, asn = jnp.cos(alpha), jnp.sin(alpha)
        for cb in range(ncb):
            ca = ac[:, cb:cb + 1]
            sa = asn[:, cb:cb + 1]
            c = ca * tc - sa * ts
            sn = sa * tc + ca * ts
            sl = slice(cb * LANES, (cb + 1) * LANES)
            if fi == 0:
                fwd_ref[0, :, sl] = c.astype(bf16)
                inv_ref[0, :, sl] = c.astype(bf16)
                fwd_ref[1, :, sl] = jnp.where(r == 0, alt_d, sn).astype(bf16)
                nst = -sn
                if cb == 0:
                    nst = jnp.where(d == 0, -alt_r, nst)
                inv_ref[1, :, sl] = nst.astype(bf16)
            elif fi == 1:
                fwd_ref[2, :, sl] = c.astype(bf16)
                fwd_ref[3, :, sl] = jnp.where(r == 0, alt_d, sn).astype(bf16)
            elif fi == 2:
                inv_ref[2, :, sl] = c.astype(bf16)
                nst = -sn
                if cb == 0:
                    nst = jnp.where(d == 0, -alt_r, nst)
                inv_ref[3, :, sl] = nst.astype(bf16)
            elif fi == 3:
                fou_ref[0, :, sl] = c.astype(bf16)
                fou_ref[1, :, sl] = (-sn).astype(bf16)
            else:
                fou_ref[2, :, sl] = c.astype(bf16)
                fou_ref[3, :, sl] = (-sn).astype(bf16)


def _dft_consts(L):
    Lh = L // 2
    tr = _tile(256, Lh)
    shp = jax.ShapeDtypeStruct((4, Lh, Lh), bf16)
    spec = pl.BlockSpec((4, tr, Lh), lambda i: (0, i, 0))
    return pl.pallas_call(
        functools.partial(_dft_consts_kernel, L=L),
        grid=(Lh // tr,),
        out_specs=[spec, spec, spec],
        out_shape=[shp, shp, shp],
        compiler_params=_cparams(("parallel",)),
        name="dft_consts",
    )()


def _prep_win_kernel(w_ref, o_ref, *, C):
    w = w_ref[...]
    wf = w[:, 3 * C:]
    i = lax.broadcasted_iota(jnp.int32, (C, C), 0)
    j = lax.broadcasted_iota(jnp.int32, (C, C), 1)
    same = (i // HEAD_DIM) == (j // HEAD_DIM)
    ang = _angle((i % HEAD_DIM) * (j % HEAD_DIM), HEAD_DIM)
    bc = jnp.where(same, jnp.cos(ang), 0.0)
    bs = jnp.where(same, jnp.sin(ang), 0.0)
    o_ref[:, :3 * C] = w[:, :3 * C].astype(bf16)
    o_ref[:, 3 * C:4 * C] = jnp.dot(wf, bc, precision=HI, preferred_element_type=f32).astype(bf16)
    o_ref[:, 4 * C:] = jnp.dot(wf, bs, precision=HI, preferred_element_type=f32).astype(bf16)


def _prep_win(w_in, C):
    D = w_in.shape[0]
    tr = _tile(256, D)
    return pl.pallas_call(
        functools.partial(_prep_win_kernel, C=C),
        grid=(D // tr,),
        in_specs=[pl.BlockSpec((tr, 4 * C), lambda i: (i, 0))],
        out_specs=pl.BlockSpec((tr, 5 * C), lambda i: (i, 0)),
        out_shape=jax.ShapeDtypeStruct((D, 5 * C), bf16),
        compiler_params=_cparams(("parallel",)),
        name="prep_win",
    )(w_in)


def _filter_kernel(win_ref, bin_ref, wmid_ref, bmid_ref, freq_ref, wout_ref, h_ref, asum_ref, *, L, C):
    tr = h_ref.shape[1]
    step = pl.program_id(0)
    lane = lax.broadcasted_iota(jnp.int32, (tr, LANES), 1)
    row = lax.broadcasted_iota(jnp.int32, (tr, LANES), 0) + step * tr
    band_i = jnp.where(lane <= FILTER_BANDS, lane - 1, lane - 1 - FILTER_BANDS)
    band = 1e-4 + band_i.astype(f32) * ((FILTER_BANDS - 1 - 1e-4) / (FILTER_BANDS - 1))
    ch = lax.broadcasted_iota(jnp.int32, (tr, 2 * C), 1) % C
    max_decay = math.log(DECAY_TARGET) / DECAY_FAST_PCT
    min_decay = math.log(DECAY_TARGET) / DECAY_SLOW_PCT
    delta = jnp.abs(min_decay + ch.astype(f32) * ((max_decay - min_decay) / (C - 1)))
    is_bwd = lax.broadcasted_iota(jnp.int32, (tr, 2 * C), 1) >= C
    row_c = lax.broadcasted_iota(jnp.int32, (tr, 2 * C), 0) + step * tr

    @pl.when(step == 0)
    def _():
        asum_ref[...] = jnp.zeros_like(asum_ref)

    total = jnp.zeros((1, 2 * C), f32)
    for p in range(2):
        pos = (2 * row + p).astype(f32)
        t = pos * (1.0 / (L - 1))
        w = pos * (2.0 * math.pi / L)
        fw = band * w
        z = jnp.where(lane == 0, t,
                      jnp.where(lane <= FILTER_BANDS, jnp.cos(fw),
                                jnp.where(lane <= 2 * FILTER_BANDS, -jnp.sin(fw), 0.0)))
        h = jnp.sin(freq_ref[0:1, :] * (jnp.dot(z, win_ref[...], precision=HI, preferred_element_type=f32)
                                       + bin_ref[...]))
        for i in range(wmid_ref.shape[0]):
            h = jnp.sin(freq_ref[i + 1:i + 2, :]
                        * (jnp.dot(h, wmid_ref[i], precision=HI, preferred_element_type=f32)
                           + bmid_ref[i:i + 1, :]))
        out = jnp.dot(h, wout_ref[...], precision=HI, preferred_element_type=f32)
        tc = (2 * row_c + p).astype(f32) * (1.0 / (L - 1))
        out = out * jnp.exp(-tc * delta)
        if p == 0:
            out = jnp.where(is_bwd & (row_c == 0), 0.0, out)
        h_ref[p] = out.astype(bf16)
        total = total + jnp.sum(jnp.abs(out), axis=0, keepdims=True)
    asum_ref[0:1, :] += total


def _hyena_filters(L, C, f_w_in, f_b_in, f_w_mid, f_b_mid, f_freq, f_w_out):
    Lh = L // 2
    tr = _tile(256, Lh)
    order = f_w_in.shape[1]
    win = jnp.pad(f_w_in, ((0, LANES - f_w_in.shape[0]), (0, 0)))
    full = lambda a: pl.BlockSpec(a.shape, lambda i: (0,) * a.ndim)
    args = (win, f_b_in.reshape(1, order), f_w_mid, f_b_mid, f_freq, f_w_out)
    return pl.pallas_call(
        functools.partial(_filter_kernel, L=L, C=C),
        grid=(Lh // tr,),
        in_specs=[full(a) for a in args],
        out_specs=[pl.BlockSpec((2, tr, 2 * C), lambda i: (0, i, 0)),
                   pl.BlockSpec((SUBLANES, 2 * C), lambda i: (0, 0))],
        out_shape=[jax.ShapeDtypeStruct((2, Lh, 2 * C), bf16),
                   jax.ShapeDtypeStruct((SUBLANES, 2 * C), f32)],
        compiler_params=_cparams(("arbitrary",)),
        name="hyena_filters",
    )(*args)


def _fwd_products(c_ref, x_ref):
    xe, xo = x_ref[0], x_ref[1]
    ce = jnp.dot(c_ref[0], xe, preferred_element_type=f32)
    se = jnp.dot(c_ref[1], xe, preferred_element_type=f32)
    co = jnp.dot(c_ref[2], xo, preferred_element_type=f32)
    so = jnp.dot(c_ref[3], xo, preferred_element_type=f32)
    return ce, se, co, so


def _filter_spectrum_kernel(c_ref, h_ref, asum_ref, k_ref, k0_ref, *, L, C):
    ce, se, co, so = _fwd_products(c_ref, h_ref)
    a = asum_ref[0:1, :]
    scale = 1.0 / ((a[:, :C] + a[:, C:]) * L)
    F = lambda v: v[:, :C]
    G = lambda v: v[:, C:]
    k_ref[0] = (F(ce) + F(co) + G(ce) + G(co)) * scale
    k_ref[1] = (-(F(se) + F(so)) + (G(se) + G(so))) * scale
    k_ref[2] = (F(ce) - F(co) + G(ce) - G(co)) * scale
    k_ref[3] = ((F(se) - F(so)) - (G(se) - G(so))) * scale

    @pl.when(pl.program_id(0) == 0)
    def _():
        k0_ref[...] = jnp.zeros_like(k0_ref)
        k0_ref[0:1, :] = ((F(se) + G(se)) * scale)[0:1, :]
        k0_ref[1:2, :] = ((F(so) - G(so)) * scale)[0:1, :]


def _filter_spectrum(cfwd, hcat, asum, L, C):
    Lh = L // 2
    tm = _tile(256, Lh)
    return pl.pallas_call(
        functools.partial(_filter_spectrum_kernel, L=L, C=C),
        grid=(Lh // tm,),
        in_specs=[pl.BlockSpec((4, tm, Lh), lambda i: (0, i, 0)),
                  pl.BlockSpec((2, Lh, 2 * C), lambda i: (0, 0, 0)),
                  pl.BlockSpec((SUBLANES, 2 * C), lambda i: (0, 0))],
        out_specs=[pl.BlockSpec((4, tm, C), lambda i: (0, i, 0)),
                   pl.BlockSpec((SUBLANES, C), lambda i: (0, 0))],
        out_shape=[jax.ShapeDtypeStruct((4, Lh, C), f32),
                   jax.ShapeDtypeStruct((SUBLANES, C), f32)],
        compiler_params=_cparams(("arbitrary",)),
        name="filter_spectrum",
    )(cfwd, hcat, asum)


def _conv_fwd_kernel(c_ref, z_ref, k_ref, k0_ref, s_ref):
    ce, se, co, so = _fwd_products(c_ref, z_ref)
    zsr, zsi, zdr, zdi = ce + co, -(se + so), ce - co, se - so
    ksr, ksi, kdr, kdi = k_ref[0], k_ref[1], k_ref[2], k_ref[3]
    psr = zsr * ksr - zsi * ksi
    psi = zsr * ksi + zsi * ksr
    pdr = zdr * kdr - zdi * kdi
    pdi = zdr * kdi + zdi * kdr
    gen = (psr + pdr, psi - pdi, psr - pdr, psi + pdi)
    first = pl.program_id(1) == 0

    @pl.when(first)
    def _():
        row0 = lax.broadcasted_iota(jnp.int32, ce.shape, 0) == 0
        p0 = zsr * ksr
        pl_ = zdr * kdr
        ak, bk = k0_ref[0:1, :], k0_ref[1:2, :]
        phr = se * ak - so * bk
        phi = -(se * bk + so * ak)
        spec = (0.5 * (p0 + pl_), -phr, 0.5 * (p0 - pl_), phi)
        for i in range(4):
            s_ref[i] = jnp.where(row0, spec[i], gen[i]).astype(bf16)

    @pl.when(jnp.logical_not(first))
    def _():
        for i in range(4):
            s_ref[i] = gen[i].astype(bf16)


def _conv_fwd(cfwd, z, kspec, k0, C):
    _, Lh, N = z.shape
    tm = _tile(DFT_ROWS, Lh)
    return pl.pallas_call(
        _conv_fwd_kernel,
        grid=(N // C, Lh // tm),
        in_specs=[pl.BlockSpec((4, tm, Lh), lambda n, i: (0, i, 0)),
                  pl.BlockSpec((2, Lh, C), lambda n, i: (0, 0, n)),
                  pl.BlockSpec((4, tm, C), lambda n, i: (0, i, 0)),
                  pl.BlockSpec((SUBLANES, C), lambda n, i: (0, 0))],
        out_specs=pl.BlockSpec((4, tm, C), lambda n, i: (0, i, n)),
        out_shape=jax.ShapeDtypeStruct((4, Lh, N), bf16),
        compiler_params=_cparams(("parallel", "arbitrary")),
        name="conv_fwd",
    )(cfwd, z, kspec, k0)


def _pair_products(c_ref, r_ref):
    ev = (jnp.dot(c_ref[0], r_ref[0], preferred_element_type=f32)
          + jnp.dot(c_ref[1], r_ref[1], preferred_element_type=f32))
    od = (jnp.dot(c_ref[2], r_ref[2], preferred_element_type=f32)
          + jnp.dot(c_ref[3], r_ref[3], preferred_element_type=f32))
    return ev, od


def _conv_inv_kernel(c_ref, s_ref, x0_ref, z_ref, fb_ref, y_ref):
    conv = _pair_products(c_ref, s_ref)
    fb = fb_ref[...]
    for p in range(2):
        zp = z_ref[p].astype(f32)
        y_ref[p] = (x0_ref[p].astype(f32) * (conv[p] + zp * fb)).astype(bf16)


def _conv_inv(cinv, s, x0, z, f_bias, C):
    _, Lh, N = s.shape
    tm = _tile(DFT_ROWS, Lh)
    return pl.pallas_call(
        _conv_inv_kernel,
        grid=(N // C, Lh // tm),
        in_specs=[pl.BlockSpec((4, tm, Lh), lambda n, i: (0, i, 0)),
                  pl.BlockSpec((4, Lh, C), lambda n, i: (0, 0, n)),
                  pl.BlockSpec((2, tm, C), lambda n, i: (0, i, n)),
                  pl.BlockSpec((2, tm, C), lambda n, i: (0, i, n)),
                  pl.BlockSpec((1, C), lambda n, i: (0, 0))],
        out_specs=pl.BlockSpec((2, tm, C), lambda n, i: (0, i, n)),
        out_shape=jax.ShapeDtypeStruct((2, Lh, N), bf16),
        compiler_params=_cparams(("parallel", "arbitrary")),
        name="conv_inv",
    )(cinv, s, x0, z, f_bias.reshape(1, C))


def _fourier_kernel(c_ref, u_ref, y_ref, *, scale):
    ev, od = _pair_products(c_ref, u_ref)
    y_ref[0] = (ev * scale).astype(bf16)
    y_ref[1] = (od * scale).astype(bf16)


def _fourier_dft(cfou, u4, L, C):
    _, Lh, N = u4.shape
    tm = _tile(DFT_ROWS, Lh)
    return pl.pallas_call(
        functools.partial(_fourier_kernel, scale=1.0 / math.sqrt(L * HEAD_DIM)),
        grid=(N // C, Lh // tm),
        in_specs=[pl.BlockSpec((4, tm, Lh), lambda n, i: (0, i, 0)),
                  pl.BlockSpec((4, Lh, C), lambda n, i: (0, 0, n))],
        out_specs=pl.BlockSpec((2, tm, C), lambda n, i: (0, i, n)),
        out_shape=jax.ShapeDtypeStruct((2, Lh, N), bf16),
        compiler_params=_cparams(("parallel", "arbitrary")),
        name="fourier_dft",
    )(cfou, u4)


def _rms(x, g):
    return x * lax.rsqrt(jnp.mean(x * x, axis=-1, keepdims=True) + EPS) * g


def _stage_rows(chunks, x):
    for k in range(chunks.shape[0]):
        chunks[k] = x[:, k * LANES:(k + 1) * LANES]


def _parity_rows(chunks, p, n):
    return jnp.concatenate([chunks[k, pl.ds(p, n, stride=2), :] for k in range(chunks.shape[0])], axis=1)


def _inproj_kernel(xa_ref, xb_ref, pa_ref, na_ref, pb_ref, nb_ref, g_ref, w_ref, cw_ref, cb_ref,
                   z_ref, x0_ref, u_ref, stage, *, C, D):
    tj = xa_ref.shape[1] // 2
    jt = pl.program_id(1)
    n_half = pl.num_programs(1)
    g = g_ref[...]
    w = w_ref[...]
    cw0, cw1, cw2 = cw_ref[0:1, :], cw_ref[1:2, :], cw_ref[2:3, :]
    cb = cb_ref[...]
    row = lax.broadcasted_iota(jnp.int32, (tj, 3 * C), 0)
    four = []
    halves = ((xa_ref, pa_ref, na_ref, jt == 0, False),
              (xb_ref, pb_ref, nb_ref, False, jt == n_half - 1))
    for h, (x_ref, p_ref, n_ref, at_start, at_end) in enumerate(halves):
        _stage_rows(stage, x_ref[0])
        lhs = jnp.concatenate([
            _rms(_parity_rows(stage, 0, tj), g).astype(bf16),
            _rms(_parity_rows(stage, 1, tj), g).astype(bf16),
            _rms(p_ref[0], g).astype(bf16),
            _rms(n_ref[0], g).astype(bf16)], axis=0)
        res = jnp.dot(lhs, w, preferred_element_type=f32)
        pe, po = res[:tj, :3 * C], res[tj:2 * tj, :3 * C]
        prev = res[2 * tj + SUBLANES - 1:2 * tj + SUBLANES, :3 * C]
        nxt = res[2 * tj + SUBLANES:2 * tj + SUBLANES + 1, :3 * C]
        if at_start is not False:
            prev = jnp.where(at_start, 0.0, prev)
        if at_end is not False:
            nxt = jnp.where(at_end, 0.0, nxt)
        po_dn = jnp.where(row == 0, prev, pltpu.roll(po, 1, 0))
        pe_up = jnp.where(row == tj - 1, nxt, pltpu.roll(pe, tj - 1, 0))
        uc = (cb + cw0 * po_dn + cw1 * pe + cw2 * po,
              cb + cw0 * pe + cw1 * po + cw2 * pe_up)
        for p in range(2):
            x0_ref[p, h] = uc[p][:, :C].astype(bf16)
            z_ref[p, h] = (uc[p][:, C:2 * C] * uc[p][:, 2 * C:]).astype(bf16)
        four.append((res[:tj, 3 * C:], res[tj:2 * tj, 3 * C:]))
    for p in range(2):
        fa, fb = four[0][p], four[1][p]
        u_ref[0, p] = (fa[:, :C] + fb[:, :C]).astype(bf16)
        u_ref[1, p] = (fa[:, C:] + fb[:, C:]).astype(bf16)
        u_ref[2, p] = (fa[:, :C] - fb[:, :C]).astype(bf16)
        u_ref[3, p] = (fa[:, C:] - fb[:, C:]).astype(bf16)


def _inproj(x, norm_g, wcat, conv_w, conv_b, C):
    B, L, D = x.shape
    Lq = L // 4
    tj = _tile(256, Lq)
    nq = Lq // tj
    N = B * C
    hb = 2 * tj // SUBLANES
    last = L // SUBLANES - 1
    main = lambda off: pl.BlockSpec((1, 2 * tj, D), lambda b, j: (b, j + off, 0))
    prev = lambda off: pl.BlockSpec((1, SUBLANES, D), lambda b, j: (b, jnp.maximum((j + off) * hb - 1, 0), 0))
    nxt = lambda off: pl.BlockSpec((1, SUBLANES, D), lambda b, j: (b, jnp.minimum((j + off + 1) * hb, last), 0))
    full = lambda a: pl.BlockSpec(a.shape, lambda b, j: (0,) * a.ndim)
    g = norm_g.reshape(1, D)
    cb = conv_b.reshape(1, 3 * C)
    return pl.pallas_call(
        functools.partial(_inproj_kernel, C=C, D=D),
        grid=(B, nq),
        in_specs=[main(0), main(nq), prev(0), nxt(0), prev(nq), nxt(nq),
                  full(g), full(wcat), full(conv_w), full(cb)],
        out_specs=[pl.BlockSpec((2, 2, tj, C), lambda b, j: (0, 0, j, b)),
                   pl.BlockSpec((2, 2, tj, C), lambda b, j: (0, 0, j, b)),
                   pl.BlockSpec((4, 2, tj, C), lambda b, j: (0, 0, j, b))],
        out_shape=[jax.ShapeDtypeStruct((2, 2, Lq, N), bf16),
                   jax.ShapeDtypeStruct((2, 2, Lq, N), bf16),
                   jax.ShapeDtypeStruct((4, 2, Lq, N), bf16)],
        scratch_shapes=[pltpu.VMEM((D // LANES, 2 * tj, LANES), f32)],
        compiler_params=_cparams(("parallel", "arbitrary")),
        name="inproj",
    )(x, x, x, x, x, x, g, wcat, conv_w, cb)


def _pack_bf16_pairs(v, is_bf16_exact=False):
    k = v.shape[1] // 2
    rounded = v if is_bf16_exact else v.astype(bf16).astype(f32)
    hi = pltpu.bitcast(rounded[:, :k], jnp.uint32)
    lo = pltpu.bitcast(rounded[:, k:], jnp.uint32)
    return hi | (lo >> 16)


def _unpack_bf16_pairs(u):
    hi = pltpu.bitcast(u & jnp.uint32(0xFFFF0000), f32)
    lo = pltpu.bitcast(u << 16, f32)
    return jnp.concatenate([hi, lo], axis=1)


def _row_tiles_shape(n, D):
    return (n, D // (2 * LANES), LANES)


def _store_row_tiles(ref, packed):
    n, per_row, _ = ref.shape
    flat = ref.reshape(n * per_row, LANES)
    for s in range(per_row):
        flat[pl.ds(s, n, stride=per_row), :] = packed[:, s * LANES:(s + 1) * LANES]


def _load_row_tiles(ref):
    n, per_row, _ = ref.shape
    flat = ref.reshape(n * per_row, LANES)
    return jnp.concatenate([flat[pl.ds(s, n, stride=per_row), :] for s in range(per_row)], axis=1)


def _outproj_router_kernel(yh_ref, yf_ref, x_ref, mg_ref, wo_ref, g2_ref, wr_ref, br_ref,
                           h2_ref, tok_ref, pos_ref, vec_ref, tinfo_ref, cnt_ref, carry_ref, stage, *, C):
    n_sub = pos_ref.shape[0]
    tj = pos_ref.shape[2]
    _stage_rows(stage, x_ref[0])

    @pl.when((pl.program_id(0) == 0) & (pl.program_id(1) == 0))
    def _():
        carry_ref[...] = jnp.zeros_like(carry_ref)

    li = lax.broadcasted_iota(jnp.int32, (LANES, LANES), 0)
    lj = lax.broadcasted_iota(jnp.int32, (LANES, LANES), 1)
    hw = _tile(MXU_DIM, C)
    ai = lax.broadcasted_iota(jnp.int32, (hw, hw), 0)
    aj = lax.broadcasted_iota(jnp.int32, (hw, hw), 1)
    avg = jnp.where(ai // HEAD_DIM == aj // HEAD_DIM, 1.0 / HEAD_DIM, 0.0).astype(bf16)
    upper = (li < lj).astype(bf16)
    ri = lax.broadcasted_iota(jnp.int32, (tj, tj), 0)
    ci = lax.broadcasted_iota(jnp.int32, (tj, tj), 1)
    tri = (ci < ri).astype(bf16)
    wr = wr_ref[...]
    wr_hi = wr.astype(bf16)
    wr_lo = (wr - wr_hi.astype(f32)).astype(bf16)
    wr_both = jnp.concatenate([wr_hi, wr_lo], axis=1)
    lane = lax.broadcasted_iota(jnp.int32, (n_sub * tj, LANES), 1).astype(f32)
    sub = lax.broadcasted_iota(jnp.int32, (SUBLANES, LANES), 0)
    neg = jnp.float32(-jnp.inf)
    big = jnp.float32(LANES)

    def first_argmax(vals, mask):
        v = jnp.where(mask, vals, neg)
        m = jnp.max(v, axis=1, keepdims=True)
        idx = jnp.min(jnp.where(mask & (v == m), lane, big), axis=1, keepdims=True)
        return m, idx

    tile_logits = []
    for s in range(n_sub):
        r, p = divmod(s, 2)
        rows = slice(s * tj, (s + 1) * tj)
        parts = []
        for ref in (yh_ref, yf_ref):
            y = ref[p, r * tj:(r + 1) * tj, :].astype(f32)
            ysq = (y * y).astype(bf16)
            ms = jnp.concatenate([jnp.dot(ysq[:, k:k + hw], avg, preferred_element_type=f32)
                                  for k in range(0, C, hw)], axis=1)
            parts.append(y * lax.rsqrt(ms + EPS))
        yn = (jnp.concatenate(parts, axis=1) * mg_ref[...]).astype(bf16)
        h2 = _parity_rows(stage, r * 2 * tj + p, tj) + jnp.dot(yn, wo_ref[...], preferred_element_type=f32)
        h2_ref[rows, :] = h2
        tok = _rms(h2, g2_ref[...])
        tok_ref[rows, :] = tok.astype(bf16)

        t_hi = tok.astype(bf16)
        t_lo = (tok - t_hi.astype(f32)).astype(bf16)
        hi_terms = jnp.dot(t_hi, wr_both, preferred_element_type=f32)
        tile_logits.append(hi_terms[:, :LANES] + hi_terms[:, LANES:]
                           + jnp.dot(t_lo, wr_hi, preferred_element_type=f32))
    logits = jnp.concatenate(tile_logits, axis=0) + br_ref[...]

    gmask = (lane >= N_EXPERTS) & (lane < N_EXPERTS + N_GROUPS)
    gmax, gidx = first_argmax(logits, gmask)
    gsel = gidx - N_EXPERTS
    p_group = 1.0 / jnp.sum(jnp.where(gmask, jnp.exp(logits - gmax), 0.0), axis=1, keepdims=True)
    emask = (lane >= gsel * EXPERTS_PER_GROUP) & (lane < (gsel + 1) * EXPERTS_PER_GROUP)
    l0, e0 = first_argmax(logits, emask)
    l1, e1 = first_argmax(logits, emask & (lane != e0))
    ex = jnp.exp(l1 - l0)
    w0 = p_group / (1.0 + ex)
    w1 = p_group * ex / (1.0 + ex)

    oh0 = (lane == e0).astype(f32)
    oh1 = (lane == e1).astype(f32)
    both = (oh0 + oh1).astype(bf16)
    where, counts = [], []
    for s in range(n_sub):
        tile_both = both[s * tj:(s + 1) * tj, :]
        ranks = jnp.dot(tri, tile_both, preferred_element_type=f32)
        n_e = ranks[tj - 1:tj, :] + tile_both[tj - 1:tj, :].astype(f32)
        n_e8 = jnp.broadcast_to(n_e, (SUBLANES, LANES))
        tile_off = jnp.dot(n_e8.astype(bf16), upper, preferred_element_type=f32)[0:1, :]
        where.append(ranks + tile_off)
        counts.append(n_e8)
    where = jnp.concatenate(where, axis=0)
    pos0 = jnp.sum(oh0 * where, axis=1, keepdims=True)
    pos1 = jnp.sum(oh1 * where, axis=1, keepdims=True)
    cols = jnp.where(lane == 0, w0, jnp.where(lane == 1, w1,
                     jnp.where(lane == 2, pos0, jnp.where(lane == 3, pos1, 0.0))))
    vec_ref[...] = cols
    base = carry_ref[...]
    for s in range(n_sub):
        pos_ref[s] = cols[s * tj:(s + 1) * tj, :].T[:SUBLANES, :].astype(jnp.int32)
        long_seg = (jnp.max(counts[s], axis=1, keepdims=True) >= SEG_SPLIT).astype(f32)
        tinfo_ref[s] = jnp.where(sub == 0, counts[s], jnp.where(sub == 1, base, jnp.where(sub == 2, long_seg, 0.0))
                                 ).astype(jnp.int32)
        base = base + counts[s]
    carry_ref[...] = base
    cnt_ref[...] = base.astype(jnp.int32)


def _outproj_router(yh, yf, x, mix_g, wo, norm2_g, wr, br, C):
    B, L, D = x.shape
    Lh = L // 2
    tj = _tile(256, Lh)
    rt = ROW_TILES_PER_STEP
    ns = 2 * rt
    nj = Lh // (rt * tj)
    T = B * L
    ntiles = T // tj
    lin = lambda b, j: b * nj + j
    full = lambda a: pl.BlockSpec(a.shape, lambda b, j: (0,) * a.ndim)
    mg = mix_g.reshape(1, D)
    g2 = norm2_g.reshape(1, D)
    return pl.pallas_call(
        functools.partial(_outproj_router_kernel, C=C),
        grid=(B, nj),
        in_specs=[pl.BlockSpec((2, rt * tj, C), lambda b, j: (0, j, b)),
                  pl.BlockSpec((2, rt * tj, C), lambda b, j: (0, j, b)),
                  pl.BlockSpec((1, ns * tj, D), lambda b, j: (b, j, 0)),
                  full(mg), full(wo), full(g2), full(wr), full(br)],
        out_specs=[pl.BlockSpec((ns * tj, D), lambda b, j: (lin(b, j), 0)),
                   pl.BlockSpec((ns * tj, D), lambda b, j: (lin(b, j), 0)),
                   pl.BlockSpec((ns, SUBLANES, tj), lambda b, j: (lin(b, j), 0, 0)),
                   pl.BlockSpec((ns * tj, LANES), lambda b, j: (lin(b, j), 0)),
                   pl.BlockSpec((ns, SUBLANES, LANES), lambda b, j: (lin(b, j), 0, 0)),
                   pl.BlockSpec((SUBLANES, LANES), lambda b, j: (0, 0))],
        out_shape=[jax.ShapeDtypeStruct((T, D), f32),
                   jax.ShapeDtypeStruct((T, D), bf16),
                   jax.ShapeDtypeStruct((ntiles, SUBLANES, tj), jnp.int32),
                   jax.ShapeDtypeStruct((T, LANES), f32),
                   jax.ShapeDtypeStruct((ntiles, SUBLANES, LANES), jnp.int32),
                   jax.ShapeDtypeStruct((SUBLANES, LANES), jnp.int32)],
        scratch_shapes=[pltpu.VMEM((SUBLANES, LANES), f32), pltpu.VMEM((D // LANES, ns * tj, LANES), f32)],
        compiler_params=_cparams(("arbitrary", "arbitrary")),
        name="outproj_router",
    )(yh, yf, x, mg, wo, g2, wr, br)


def _padded(c):
    return ((c + MOE_ROWS - 1) // MOE_ROWS) * MOE_ROWS


def _pad_starts(cnt_ref, start_ref):
    def body(e, acc):
        start_ref[e] = acc
        return acc + _padded(cnt_ref[0, e])
    return lax.fori_loop(0, N_EXPERTS, body, jnp.int32(0))


def _segment_copies(n, src, src_off, dst, dst_off, sem, max_rows, wait, src_advances=True, min_rows=1):
    bit = max_rows
    while bit >= min_rows:
        take = n & bit

        @pl.when(take != 0)
        def _(src_off=src_off, dst_off=dst_off, bit=bit):
            cp = pltpu.make_async_copy(src.at[pl.ds(src_off, bit)], dst.at[pl.ds(dst_off, bit)], sem)
            if wait:
                cp.wait()
            else:
                cp.start()
        if src_advances:
            src_off = src_off + take
        dst_off = dst_off + take
        bit //= 2


def _tile_segments(tinfo_ref, t, start_ref, local, local_is_src, remote, sem, tj):
    def copies(long_part):
        off = 0
        for e in range(N_EXPERTS):
            n = tinfo_ref[t, 0, e]
            far = start_ref[e] + tinfo_ref[t, 1, e]
            if long_part:
                args = (n, off, far, tj, SEG_SPLIT)
            else:
                head = n & ~(SEG_SPLIT - 1)
                args = (n, off + head, far + head, SEG_SPLIT // 2, 1)
            n_, near_, far_, hi, lo = args
            if local_is_src:
                _segment_copies(n_, local, near_, remote, far_, sem, hi, wait=False, min_rows=lo)
            else:
                _segment_copies(n_, remote, far_, local, near_, sem, hi, wait=False, min_rows=lo)
            off = off + n

    copies(long_part=False)

    @pl.when(tinfo_ref[t, 2, 0] != 0)
    def _():
        copies(long_part=True)


def _dispatch_kernel(tinfo_ref, cnt_ref, tok_ref, pos_ref, xs_ref, be_ref, sbufs, zbuf, start_ref, sems, zsem):
    tj = pos_ref.shape[2]
    n_blocks = be_ref.shape[0]

    @pl.when(pl.program_id(0) == 0)
    def _():
        total = _pad_starts(cnt_ref, start_ref)
        used = total // MOE_ROWS

        def fill(i, e):
            def cond(e):
                nxt = jnp.where(e + 1 < N_EXPERTS, start_ref[jnp.minimum(e + 1, N_EXPERTS - 1)], total)
                return (e + 1 < N_EXPERTS) & (i * MOE_ROWS >= nxt)
            e = lax.while_loop(cond, lambda e: e + 1, e)
            be_ref[i] = jnp.where(i < used, e, e + N_EXPERTS)
            return e
        lax.fori_loop(0, n_blocks, fill, jnp.int32(0))

        zbuf[...] = jnp.zeros_like(zbuf)
        zrows = zbuf.shape[0]
        for wait in (False, True):
            def zero_pad(e, c, wait=wait):
                cnt = cnt_ref[0, e]
                _segment_copies(_padded(cnt) - cnt, zbuf, 0, xs_ref, start_ref[e] + cnt, zsem,
                                zrows, wait, src_advances=False)
                return c
            lax.fori_loop(0, N_EXPERTS, zero_pad, 0)

            def zero_tail(i, c, wait=wait):
                cp = pltpu.make_async_copy(zbuf, xs_ref.at[pl.ds(total + i * zrows, zrows)], zsem)
                if wait:
                    cp.wait()
                else:
                    cp.start()
                return c
            lax.fori_loop(0, (n_blocks - used) * (MOE_ROWS // zrows), zero_tail, 0)

    def wait_tile(k):
        pltpu.make_async_copy(sbufs[k], xs_ref.at[pl.ds(0, 2 * tj)], sems.at[k]).wait()

    slot = lax.broadcasted_iota(jnp.int32, (2 * tj, tj), 0)
    for k in range(TILES_PER_STEP):
        @pl.when(pl.program_id(0) > 0)
        def _(k=k):
            wait_tile(k)
        perm = ((slot == pos_ref[k, 2:3, :]) | (slot == pos_ref[k, 3:4, :])).astype(bf16)
        sorted_rows = jnp.dot(perm, tok_ref[k * tj:(k + 1) * tj, :], preferred_element_type=f32)
        _store_row_tiles(sbufs[k], _pack_bf16_pairs(sorted_rows, is_bf16_exact=True))
        if k > 0:
            _tile_segments(tinfo_ref, k - 1, start_ref, sbufs[k - 1], True, xs_ref, sems.at[k - 1], tj)
    last = TILES_PER_STEP - 1
    _tile_segments(tinfo_ref, last, start_ref, sbufs[last], True, xs_ref, sems.at[last], tj)

    @pl.when(pl.program_id(0) == pl.num_programs(0) - 1)
    def _():
        for k in range(TILES_PER_STEP):
            wait_tile(k)


def _dispatch(tinfo, cnt, tok, pos, n_rows):
    ntiles, _, tj = pos.shape
    T, D = tok.shape
    n_blocks = n_rows // MOE_ROWS
    nt = TILES_PER_STEP
    assert ntiles % nt == 0
    sbuf = pltpu.VMEM(_row_tiles_shape(2 * tj, D), jnp.uint32)
    return pl.pallas_call(
        _dispatch_kernel,
        grid=(ntiles // nt,),
        in_specs=[pl.BlockSpec((nt, SUBLANES, LANES), lambda i: (i, 0, 0), memory_space=pltpu.SMEM),
                  pl.BlockSpec(memory_space=pltpu.SMEM),
                  pl.BlockSpec((nt * tj, D), lambda i: (i, 0)),
                  pl.BlockSpec((nt, SUBLANES, tj), lambda i: (i, 0, 0))],
        out_specs=[pl.BlockSpec(memory_space=pl.ANY),
                   pl.BlockSpec(memory_space=pltpu.SMEM)],
        out_shape=[jax.ShapeDtypeStruct(_row_tiles_shape(n_rows, D), jnp.uint32),
                   jax.ShapeDtypeStruct((n_blocks,), jnp.int32)],
        scratch_shapes=[[sbuf] * nt,
                        pltpu.VMEM(_row_tiles_shape(MOE_ROWS // 2, D), jnp.uint32),
                        pltpu.SMEM((N_EXPERTS,), jnp.int32), pltpu.SemaphoreType.DMA((nt,)),
                        pltpu.SemaphoreType.DMA],
        compiler_params=_cparams(("arbitrary",)),
        name="moe_dispatch",
    )(tinfo, cnt, tok, pos)


def _experts_kernel(be_ref, xs_ref, wg_ref, wu_ref, wd_ref, ys_ref, wg_s, wu_s, wd_s):
    i = pl.program_id(0)
    code = be_ref[i]
    prev = be_ref[jnp.maximum(i - 1, 0)]
    valid = code < N_EXPERTS

    @pl.when(valid & ((i == 0) | (code != prev)))
    def _():
        wg_s[...] = wg_ref[...].astype(bf16)
        wu_s[...] = wu_ref[...].astype(bf16)
        wd_s[...] = wd_ref[...].astype(bf16)

    @pl.when(valid)
    def _():
        half = xs_ref.shape[0] // 2
        halves = [pl.ds(r * half, half) for r in range(2)]
        gated = []
        for rows in halves:
            xb = _unpack_bf16_pairs(_load_row_tiles(xs_ref.at[rows])).astype(bf16)
            gated.append((jnp.dot(xb, wg_s[...], preferred_element_type=f32),
                          jnp.dot(xb, wu_s[...], preferred_element_type=f32)))
        for rows, (a, u) in zip(halves, gated):
            h = (a * jax.nn.sigmoid(a) * u).astype(bf16)
            _store_row_tiles(ys_ref.at[rows], _pack_bf16_pairs(jnp.dot(h, wd_s[...], preferred_element_type=f32)))

    @pl.when(jnp.logical_not(valid))
    def _():
        ys_ref[...] = jnp.zeros_like(ys_ref)


def _experts(block_e, xs, w_gate, w_up, w_down):
    n_rows = xs.shape[0]
    _, D, De = w_gate.shape
    n_blocks = n_rows // MOE_ROWS
    blk = _row_tiles_shape(MOE_ROWS, D)
    widx = lambda i, be: (be[i] % N_EXPERTS, 0, 0)
    xidx = lambda i, be: (jnp.where(be[i] < N_EXPERTS, i, 0), 0, 0)
    return pl.pallas_call(
        _experts_kernel,
        grid_spec=pltpu.PrefetchScalarGridSpec(
            num_scalar_prefetch=1,
            grid=(n_blocks,),
            in_specs=[pl.BlockSpec(blk, xidx),
                      pl.BlockSpec((None, D, De), widx),
                      pl.BlockSpec((None, D, De), widx),
                      pl.BlockSpec((None, De, D), widx)],
            out_specs=pl.BlockSpec(blk, lambda i, be: (i, 0, 0)),
            scratch_shapes=[pltpu.VMEM((D, De), bf16), pltpu.VMEM((D, De), bf16), pltpu.VMEM((De, D), bf16)],
        ),
        out_shape=jax.ShapeDtypeStruct(xs.shape, jnp.uint32),
        compiler_params=_cparams(("arbitrary",)),
        name="moe_experts",
    )(block_e, xs, w_gate, w_up, w_down)


def _combine_kernel(tinfo_ref, tnext_ref, cnt_ref, h2_ref, vec_ref, g_ref, ys_ref, o_ref,
                    gbufs, stages, start_ref, sems):
    tj = vec_ref.shape[0] // TILES_PER_STEP
    step = pl.program_id(0)

    def fetch(info_ref, t, buf):
        _tile_segments(info_ref, t, start_ref, gbufs[buf], False, ys_ref, sems.at[buf], tj)

    @pl.when(step == 0)
    def _():
        _pad_starts(cnt_ref, start_ref)
        for t in range(GATHER_AHEAD):
            fetch(tinfo_ref, t, t)

    slot = lax.broadcasted_iota(jnp.int32, (tj, 2 * tj), 1)
    for k in range(TILES_PER_STEP):
        pltpu.make_async_copy(ys_ref.at[pl.ds(0, 2 * tj)], gbufs[k], sems.at[k]).wait()
        ahead = k + GATHER_AHEAD
        if ahead < TILES_PER_STEP:
            fetch(tinfo_ref, ahead, ahead)
        else:
            fetch(tnext_ref, ahead - TILES_PER_STEP, ahead - TILES_PER_STEP)
        rows = _unpack_bf16_pairs(_load_row_tiles(gbufs[k])).astype(bf16)
        v = vec_ref[k * tj:(k + 1) * tj, :]
        y = h2_ref[k * tj:(k + 1) * tj, :]
        for c in range(2):
            pick = (slot == v[:, 2 + c:3 + c].astype(jnp.int32)).astype(bf16)
            y = y + v[:, c:c + 1] * jnp.dot(pick, rows, preferred_element_type=f32)
        out = _rms(y, g_ref[...])
        stage = stages[k // 2]
        for q in range(stage.shape[0]):
            stage[q, pl.ds(k % 2, tj, stride=2), :] = out[:, q * LANES:(q + 1) * LANES]
        if k % 2 == 1:
            r0 = (k // 2) * 2 * tj
            for q in range(stage.shape[0]):
                o_ref[0, r0:r0 + 2 * tj, q * LANES:(q + 1) * LANES] = stage[q]


def _combine(tinfo, cnt, h2, vec, final_g, ys, B, L):
    T, D = h2.shape
    ntiles = tinfo.shape[0]
    tj = T // ntiles
    nt = TILES_PER_STEP
    assert GATHER_AHEAD < nt and nt % 2 == 0 and (L // tj) % nt == 0
    nsteps = ntiles // nt
    per_seq = L // (nt * tj)
    g = final_g.reshape(1, D)
    gbuf = pltpu.VMEM(_row_tiles_shape(2 * tj, D), jnp.uint32)
    return pl.pallas_call(
        _combine_kernel,
        grid=(nsteps,),
        in_specs=[pl.BlockSpec((nt, SUBLANES, LANES), lambda i: (i, 0, 0), memory_space=pltpu.SMEM),
                  pl.BlockSpec((nt, SUBLANES, LANES), lambda i: (i + 1, 0, 0), memory_space=pltpu.SMEM),
                  pl.BlockSpec(memory_space=pltpu.SMEM),
                  pl.BlockSpec((nt * tj, D), lambda i: (i, 0)),
                  pl.BlockSpec((nt * tj, LANES), lambda i: (i, 0)),
                  pl.BlockSpec((1, D), lambda i: (0, 0)),
                  pl.BlockSpec(memory_space=pl.ANY)],
        out_specs=pl.BlockSpec((1, nt * tj, D), lambda i: (i // per_seq, i % per_seq, 0)),
        out_shape=jax.ShapeDtypeStruct((B, L, D), f32),
        scratch_shapes=[[gbuf] * nt,
                        [pltpu.VMEM((D // LANES, 2 * tj, LANES), f32)] * (nt // 2),
                        pltpu.SMEM((N_EXPERTS,), jnp.int32), pltpu.SemaphoreType.DMA((nt,))],
        compiler_params=_cparams(("arbitrary",)),
        name="moe_combine",
    )(tinfo, jnp.pad(tinfo, ((0, nt), (0, 0), (0, 0))), cnt, h2, vec, g, ys)


def kernel(x, norm1_g, w_in, conv_w, conv_b, f_w_in, f_b_in, f_w_mid, f_b_mid, f_freq, f_w_out, f_bias,
           mix_g, w_out, norm2_g, w_group, b_group, w_router, b_router, w_gate, w_up, w_down, final_g):
    B, L, D = x.shape
    depth = norm1_g.shape[0]
    assert depth == 1, "the final RMSNorm is fused into the single layer's MoE combine"
    C = D // 2
    Lh = L // 2
    N = B * C
    T = B * L
    n_rows = T * 2 + N_EXPERTS * MOE_ROWS
    cfwd, cinv, cfou = _dft_consts(L)
    i = 0
    wcat = _prep_win(w_in[i], C)
    hcat, asum = _hyena_filters(L, C, f_w_in[i], f_b_in[i], f_w_mid[i], f_b_mid[i], f_freq[i], f_w_out[i])
    kspec, k0 = _filter_spectrum(cfwd, hcat, asum, L, C)
    z, x0, u4 = _inproj(x, norm1_g[i], wcat, conv_w[i], conv_b[i], C)
    z = z.reshape(2, Lh, N)
    x0 = x0.reshape(2, Lh, N)
    u4 = u4.reshape(4, Lh, N)
    s = _conv_fwd(cfwd, z, kspec, k0, C)
    yh = _conv_inv(cinv, s, x0, z, f_bias[i], C)
    yf = _fourier_dft(cfou, u4, L, C)
    wr = jnp.concatenate([jnp.transpose(w_router[i], (1, 0, 2)).reshape(D, N_EXPERTS), w_group[i]], axis=1)
    wr = jnp.pad(wr, ((0, 0), (0, LANES - wr.shape[1])))
    br = jnp.pad(jnp.concatenate([b_router[i].reshape(-1), b_group[i]]), (0, LANES - N_EXPERTS - N_GROUPS))
    h2, tok, pos, vec, tinfo, cnt = _outproj_router(yh, yf, x, mix_g[i], w_out[i].astype(bf16), norm2_g[i],
                                                     wr, br.reshape(1, LANES), C)
    xs, block_e = _dispatch(tinfo, cnt, tok, pos, n_rows)
    ys = _experts(block_e, xs, w_gate[i], w_up[i], w_down[i])
    return _combine(tinfo, cnt, h2, vec, final_g, ys, B, L)
```

```python
import functools
import math

import jax
import jax.numpy as jnp
from jax import lax
from jax.experimental import pallas as pl
from jax.experimental.pallas import tpu as pltpu

HEAD_DIM = 64
N_GROUPS = 4
EXPERTS_PER_GROUP = 8
N_EXPERTS = N_GROUPS * EXPERTS_PER_GROUP
FILTER_BANDS = 16
DECAY_FAST_PCT = 0.3
DECAY_SLOW_PCT = 1.5
DECAY_TARGET = 1e-2
EPS = 1e-6

LANES = 128
SUBLANES = 8
MXU_DIM = 256
DFT_ROWS = 512
MOE_ROWS = 256
ROW_TILES_PER_STEP = 2
TILES_PER_STEP = 4
GATHER_AHEAD = 2
SEG_SPLIT = 32
VMEM_LIMIT = 56 * 1024 * 1024

f32 = jnp.float32
bf16 = jnp.bfloat16
HI = lax.Precision.HIGHEST


def _cparams(sem):
    return pltpu.CompilerParams(dimension_semantics=sem, vmem_limit_bytes=VMEM_LIMIT)


def _tile(pref, n):
    return min(pref, n)


def _angle(prod, m):
    return (prod & (m - 1)).astype(f32) * (2.0 * math.pi / m)


def _dft_consts_kernel(fwd_ref, inv_ref, fou_ref, *, L):
    Lh = L // 2
    Lq = Lh // 2
    tr = fwd_ref.shape[1]
    r = lax.broadcasted_iota(jnp.int32, (tr, LANES), 0) + pl.program_id(0) * tr
    d = lax.broadcasted_iota(jnp.int32, (tr, LANES), 1)
    alt_r = (1 - 2 * (r & 1)).astype(f32)
    alt_d = (1 - 2 * (d & 1)).astype(f32)
    ncb = Lh // LANES
    two_r1 = 2 * r + 1

    def col_l(cb):
        c0 = cb * LANES
        q, j0 = c0 // Lq, c0 % Lq
        return 2 * j0 + q

    fams = [
        (r, 1, lambda cb: cb * LANES, L),
        (r, 2, lambda cb: 2 * cb * LANES + 1, 2 * L),
        (two_r1, 1, lambda cb: cb * LANES, 2 * L),
        (r, 2, col_l, Lh),
        (two_r1, 2, col_l, L),
    ]
    for fi, (a, s, b0, m) in enumerate(fams):
        beta = _angle(a * (s * d), m)
        tc, ts = jnp.cos(beta), jnp.sin(beta)
        b0_lane = jnp.zeros((tr, LANES), jnp.int32)
        for cb in range(ncb):
            b0_lane = jnp.where(d == cb, b0(cb), b0_lane)
        alpha = _angle(a * b0_lane, m)
        ac, asn = jnp.cos(alpha), jnp.sin(alpha)
        for cb in range(ncb):
            ca = ac[:, cb:cb + 1]
            sa = asn[:, cb:cb + 1]
            c = ca * tc - sa * ts
            sn = sa * tc + ca * ts
            sl = slice(cb * LANES, (cb + 1) * LANES)
            if fi == 0:
                fwd_ref[0, :, sl] = c.astype(bf16)
                inv_ref[0, :, sl] = c.astype(bf16)
                fwd_ref[1, :, sl] = jnp.where(r == 0, alt_d, sn).astype(bf16)
                nst = -sn
                if cb == 0:
                    nst = jnp.where(d == 0, -alt_r, nst)
                inv_ref[1, :, sl] = nst.astype(bf16)
            elif fi == 1:
                fwd_ref[2, :, sl] = c.astype(bf16)
                fwd_ref[3, :, sl] = jnp.where(r == 0, alt_d, sn).astype(bf16)
            elif fi == 2:
                inv_ref[2, :, sl] = c.astype(bf16)
                nst = -sn
                if cb == 0:
                    nst = jnp.where(d == 0, -alt_r, nst)
                inv_ref[3, :, sl] = nst.astype(bf16)
            elif fi == 3:
                fou_ref[0, :, sl] = c.astype(bf16)
                fou_ref[1, :, sl] = (-sn).astype(bf16)
            else:
                fou_ref[2, :, sl] = c.astype(bf16)
                fou_ref[3, :, sl] = (-sn).astype(bf16)


def _dft_consts(L):
    Lh = L // 2
    tr = _tile(256, Lh)
    shp = jax.ShapeDtypeStruct((4, Lh, Lh), bf16)
    spec = pl.BlockSpec((4, tr, Lh), lambda i: (0, i, 0))
    return pl.pallas_call(
        functools.partial(_dft_consts_kernel, L=L),
        grid=(Lh // tr,),
        out_specs=[spec, spec, spec],
        out_shape=[shp, shp, shp],
        compiler_params=_cparams(("parallel",)),
        name="dft_consts",
    )()


def _prep_win_kernel(w_ref, o_ref, *, C):
    w = w_ref[...]
    wf = w[:, 3 * C:]
    i = lax.broadcasted_iota(jnp.int32, (C, C), 0)
    j = lax.broadcasted_iota(jnp.int32, (C, C), 1)
    same = (i // HEAD_DIM) == (j // HEAD_DIM)
    ang = _angle((i % HEAD_DIM) * (j % HEAD_DIM), HEAD_DIM)
    bc = jnp.where(same, jnp.cos(ang), 0.0)
    bs = jnp.where(same, jnp.sin(ang), 0.0)
    o_ref[:, :3 * C] = w[:, :3 * C].astype(bf16)
    o_ref[:, 3 * C:4 * C] = jnp.dot(wf, bc, precision=HI, preferred_element_type=f32).astype(bf16)
    o_ref[:, 4 * C:] = jnp.dot(wf, bs, precision=HI, preferred_element_type=f32).astype(bf16)


def _prep_win(w_in, C):
    D = w_in.shape[0]
    tr = _tile(256, D)
    return pl.pallas_call(
        functools.partial(_prep_win_kernel, C=C),
        grid=(D // tr,),
        in_specs=[pl.BlockSpec((tr, 4 * C), lambda i: (i, 0))],
        out_specs=pl.BlockSpec((tr, 5 * C), lambda i: (i, 0)),
        out_shape=jax.ShapeDtypeStruct((D, 5 * C), bf16),
        compiler_params=_cparams(("parallel",)),
        name="prep_win",
    )(w_in)


def _filter_kernel(win_ref, bin_ref, wmid_ref, bmid_ref, freq_ref, wout_ref, h_ref, asum_ref, *, L, C):
    tr = h_ref.shape[1]
    step = pl.program_id(0)
    lane = lax.broadcasted_iota(jnp.int32, (tr, LANES), 1)
    row = lax.broadcasted_iota(jnp.int32, (tr, LANES), 0) + step * tr
    band_i = jnp.where(lane <= FILTER_BANDS, lane - 1, lane - 1 - FILTER_BANDS)
    band = 1e-4 + band_i.astype(f32) * ((FILTER_BANDS - 1 - 1e-4) / (FILTER_BANDS - 1))
    ch = lax.broadcasted_iota(jnp.int32, (tr, 2 * C), 1) % C
    max_decay = math.log(DECAY_TARGET) / DECAY_FAST_PCT
    min_decay = math.log(DECAY_TARGET) / DECAY_SLOW_PCT
    delta = jnp.abs(min_decay + ch.astype(f32) * ((max_decay - min_decay) / (C - 1)))
    is_bwd = lax.broadcasted_iota(jnp.int32, (tr, 2 * C), 1) >= C
    row_c = lax.broadcasted_iota(jnp.int32, (tr, 2 * C), 0) + step * tr

    @pl.when(step == 0)
    def _():
        asum_ref[...] = jnp.zeros_like(asum_ref)

    total = jnp.zeros((1, 2 * C), f32)
    for p in range(2):
        pos = (2 * row + p).astype(f32)
        t = pos * (1.0 / (L - 1))
        w = pos * (2.0 * math.pi / L)
        fw = band * w
        z = jnp.where(lane == 0, t,
                      jnp.where(lane <= FILTER_BANDS, jnp.cos(fw),
                                jnp.where(lane <= 2 * FILTER_BANDS, -jnp.sin(fw), 0.0)))
        h = jnp.sin(freq_ref[0:1, :] * (jnp.dot(z, win_ref[...], precision=HI, preferred_element_type=f32)
                                       + bin_ref[...]))
        for i in range(wmid_ref.shape[0]):
            h = jnp.sin(freq_ref[i + 1:i + 2, :]
                        * (jnp.dot(h, wmid_ref[i], precision=HI, preferred_element_type=f32)
                           + bmid_ref[i:i + 1, :]))
        out = jnp.dot(h, wout_ref[...], precision=HI, preferred_element_type=f32)
        tc = (2 * row_c + p).astype(f32) * (1.0 / (L - 1))
        out = out * jnp.exp(-tc * delta)
        if p == 0:
            out = jnp.where(is_bwd & (row_c == 0), 0.0, out)
        h_ref[p] = out.astype(bf16)
        total = total + jnp.sum(jnp.abs(out), axis=0, keepdims=True)
    asum_ref[0:1, :] += total


def _hyena_filters(L, C, f_w_in, f_b_in, f_w_mid, f_b_mid, f_freq, f_w_out):
    Lh = L // 2
    tr = _tile(256, Lh)
    order = f_w_in.shape[1]
    win = jnp.pad(f_w_in, ((0, LANES - f_w_in.shape[0]), (0, 0)))
    full = lambda a: pl.BlockSpec(a.shape, lambda i: (0,) * a.ndim)
    args = (win, f_b_in.reshape(1, order), f_w_mid, f_b_mid, f_freq, f_w_out)
    return pl.pallas_call(
        functools.partial(_filter_kernel, L=L, C=C),
        grid=(Lh // tr,),
        in_specs=[full(a) for a in args],
        out_specs=[pl.BlockSpec((2, tr, 2 * C), lambda i: (0, i, 0)),
                   pl.BlockSpec((SUBLANES, 2 * C), lambda i: (0, 0))],
        out_shape=[jax.ShapeDtypeStruct((2, Lh, 2 * C), bf16),
                   jax.ShapeDtypeStruct((SUBLANES, 2 * C), f32)],
        compiler_params=_cparams(("arbitrary",)),
        name="hyena_filters",
    )(*args)


def _fwd_products(c_ref, x_ref):
    xe, xo = x_ref[0], x_ref[1]
    ce = jnp.dot(c_ref[0], xe, preferred_element_type=f32)
    se = jnp.dot(c_ref[1], xe, preferred_element_type=f32)
    co = jnp.dot(c_ref[2], xo, preferred_element_type=f32)
    so = jnp.dot(c_ref[3], xo, preferred_element_type=f32)
    return ce, se, co, so


def _filter_spectrum_kernel(c_ref, h_ref, asum_ref, k_ref, k0_ref, *, L, C):
    ce, se, co, so = _fwd_products(c_ref, h_ref)
    a = asum_ref[0:1, :]
    scale = 1.0 / ((a[:, :C] + a[:, C:]) * L)
    F = lambda v: v[:, :C]
    G = lambda v: v[:, C:]
    k_ref[0] = (F(ce) + F(co) + G(ce) + G(co)) * scale
    k_ref[1] = (-(F(se) + F(so)) + (G(se) + G(so))) * scale
    k_ref[2] = (F(ce) - F(co) + G(ce) - G(co)) * scale
    k_ref[3] = ((F(se) - F(so)) - (G(se) - G(so))) * scale

    @pl.when(pl.program_id(0) == 0)
    def _():
        k0_ref[...] = jnp.zeros_like(k0_ref)
        k0_ref[0:1, :] = ((F(se) + G(se)) * scale)[0:1, :]
        k0_ref[1:2, :] = ((F(so) - G(so)) * scale)[0:1, :]


def _filter_spectrum(cfwd, hcat, asum, L, C):
    Lh = L // 2
    tm = _tile(256, Lh)
    return pl.pallas_call(
        functools.partial(_filter_spectrum_kernel, L=L, C=C),
        grid=(Lh // tm,),
        in_specs=[pl.BlockSpec((4, tm, Lh), lambda i: (0, i, 0)),
                  pl.BlockSpec((2, Lh, 2 * C), lambda i: (0, 0, 0)),
                  pl.BlockSpec((SUBLANES, 2 * C), lambda i: (0, 0))],
        out_specs=[pl.BlockSpec((4, tm, C), lambda i: (0, i, 0)),
                   pl.BlockSpec((SUBLANES, C), lambda i: (0, 0))],
        out_shape=[jax.ShapeDtypeStruct((4, Lh, C), f32),
                   jax.ShapeDtypeStruct((SUBLANES, C), f32)],
        compiler_params=_cparams(("arbitrary",)),
        name="filter_spectrum",
    )(cfwd, hcat, asum)


def _conv_fwd_kernel(c_ref, z_ref, k_ref, k0_ref, s_ref):
    ce, se, co, so = _fwd_products(c_ref, z_ref)
    zsr, zsi, zdr, zdi = ce + co, -(se + so), ce - co, se - so
    ksr, ksi, kdr, kdi = k_ref[0], k_ref[1], k_ref[2], k_ref[3]
    psr = zsr * ksr - zsi * ksi
    psi = zsr * ksi + zsi * ksr
    pdr = zdr * kdr - zdi * kdi
    pdi = zdr * kdi + zdi * kdr
    gen = (psr + pdr, psi - pdi, psr - pdr, psi + pdi)
    first = pl.program_id(1) == 0

    @pl.when(first)
    def _():
        row0 = lax.broadcasted_iota(jnp.int32, ce.shape, 0) == 0
        p0 = zsr * ksr
        pl_ = zdr * kdr
        ak, bk = k0_ref[0:1, :], k0_ref[1:2, :]
        phr = se * ak - so * bk
        phi = -(se * bk + so * ak)
        spec = (0.5 * (p0 + pl_), -phr, 0.5 * (p0 - pl_), phi)
        for i in range(4):
            s_ref[i] = jnp.where(row0, spec[i], gen[i]).astype(bf16)

    @pl.when(jnp.logical_not(first))
    def _():
        for i in range(4):
            s_ref[i] = gen[i].astype(bf16)


def _conv_fwd(cfwd, z, kspec, k0, C):
    _, Lh, N = z.shape
    tm = _tile(DFT_ROWS, Lh)
    return pl.pallas_call(
        _conv_fwd_kernel,
        grid=(N // C, Lh // tm),
        in_specs=[pl.BlockSpec((4, tm, Lh), lambda n, i: (0, i, 0)),
                  pl.BlockSpec((2, Lh, C), lambda n, i: (0, 0, n)),
                  pl.BlockSpec((4, tm, C), lambda n, i: (0, i, 0)),
                  pl.BlockSpec((SUBLANES, C), lambda n, i: (0, 0))],
        out_specs=pl.BlockSpec((4, tm, C), lambda n, i: (0, i, n)),
        out_shape=jax.ShapeDtypeStruct((4, Lh, N), bf16),
        compiler_params=_cparams(("parallel", "arbitrary")),
        name="conv_fwd",
    )(cfwd, z, kspec, k0)


def _pair_products(c_ref, r_ref):
    ev = (jnp.dot(c_ref[0], r_ref[0], preferred_element_type=f32)
          + jnp.dot(c_ref[1], r_ref[1], preferred_element_type=f32))
    od = (jnp.dot(c_ref[2], r_ref[2], preferred_element_type=f32)
          + jnp.dot(c_ref[3], r_ref[3], preferred_element_type=f32))
    return ev, od


def _conv_inv_kernel(c_ref, s_ref, x0_ref, z_ref, fb_ref, y_ref):
    conv = _pair_products(c_ref, s_ref)
    fb = fb_ref[...]
    for p in range(2):
        zp = z_ref[p].astype(f32)
        y_ref[p] = (x0_ref[p].astype(f32) * (conv[p] + zp * fb)).astype(bf16)


def _conv_inv(cinv, s, x0, z, f_bias, C):
    _, Lh, N = s.shape
    tm = _tile(DFT_ROWS, Lh)
    return pl.pallas_call(
        _conv_inv_kernel,
        grid=(N // C, Lh // tm),
        in_specs=[pl.BlockSpec((4, tm, Lh), lambda n, i: (0, i, 0)),
                  pl.BlockSpec((4, Lh, C), lambda n, i: (0, 0, n)),
                  pl.BlockSpec((2, tm, C), lambda n, i: (0, i, n)),
                  pl.BlockSpec((2, tm, C), lambda n, i: (0, i, n)),
                  pl.BlockSpec((1, C), lambda n, i: (0, 0))],
        out_specs=pl.BlockSpec((2, tm, C), lambda n, i: (0, i, n)),
        out_shape=jax.ShapeDtypeStruct((2, Lh, N), bf16),
        compiler_params=_cparams(("parallel", "arbitrary")),
        name="conv_inv",
    )(cinv, s, x0, z, f_bias.reshape(1, C))


def _fourier_kernel(c_ref, u_ref, y_ref, *, scale):
    ev, od = _pair_products(c_ref, u_ref)
    y_ref[0] = (ev * scale).astype(bf16)
    y_ref[1] = (od * scale).astype(bf16)


def _fourier_dft(cfou, u4, L, C):
    _, Lh, N = u4.shape
    tm = _tile(DFT_ROWS, Lh)
    return pl.pallas_call(
        functools.partial(_fourier_kernel, scale=1.0 / math.sqrt(L * HEAD_DIM)),
        grid=(N // C, Lh // tm),
        in_specs=[pl.BlockSpec((4, tm, Lh), lambda n, i: (0, i, 0)),
                  pl.BlockSpec((4, Lh, C), lambda n, i: (0, 0, n))],
        out_specs=pl.BlockSpec((2, tm, C), lambda n, i: (0, i, n)),
        out_shape=jax.ShapeDtypeStruct((2, Lh, N), bf16),
        compiler_params=_cparams(("parallel", "arbitrary")),
        name="fourier_dft",
    )(cfou, u4)


def _rms(x, g):
    return x * lax.rsqrt(jnp.mean(x * x, axis=-1, keepdims=True) + EPS) * g


def _stage_rows(chunks, x):
    for k in range(chunks.shape[0]):
        chunks[k] = x[:, k * LANES:(k + 1) * LANES]


def _parity_rows(chunks, p, n):
    return jnp.concatenate([chunks[k, pl.ds(p, n, stride=2), :] for k in range(chunks.shape[0])], axis=1)


def _inproj_kernel(xa_ref, xb_ref, pa_ref, na_ref, pb_ref, nb_ref, g_ref, w_ref, cw_ref, cb_ref,
                   z_ref, x0_ref, u_ref, stage, *, C, D):
    tj = xa_ref.shape[1] // 2
    jt = pl.program_id(1)
    n_half = pl.num_programs(1)
    g = g_ref[...]
    w = w_ref[...]
    cw0, cw1, cw2 = cw_ref[0:1, :], cw_ref[1:2, :], cw_ref[2:3, :]
    cb = cb_ref[...]
    row = lax.broadcasted_iota(jnp.int32, (tj, 3 * C), 0)
    four = []
    halves = ((xa_ref, pa_ref, na_ref, jt == 0, False),
              (xb_ref, pb_ref, nb_ref, False, jt == n_half - 1))
    for h, (x_ref, p_ref, n_ref, at_start, at_end) in enumerate(halves):
        _stage_rows(stage, x_ref[0])
        lhs = jnp.concatenate([
            _rms(_parity_rows(stage, 0, tj), g).astype(bf16),
            _rms(_parity_rows(stage, 1, tj), g).astype(bf16),
            _rms(p_ref[0], g).astype(bf16),
            _rms(n_ref[0], g).astype(bf16)], axis=0)
        res = jnp.dot(lhs, w, preferred_element_type=f32)
        pe, po = res[:tj, :3 * C], res[tj:2 * tj, :3 * C]
        prev = res[2 * tj + SUBLANES - 1:2 * tj + SUBLANES, :3 * C]
        nxt = res[2 * tj + SUBLANES:2 * tj + SUBLANES + 1, :3 * C]
        if at_start is not False:
            prev = jnp.where(at_start, 0.0, prev)
        if at_end is not False:
            nxt = jnp.where(at_end, 0.0, nxt)
        po_dn = jnp.where(row == 0, prev, pltpu.roll(po, 1, 0))
        pe_up = jnp.where(row == tj - 1, nxt, pltpu.roll(pe, tj - 1, 0))
        uc = (cb + cw0 * po_dn + cw1 * pe + cw2 * po,
              cb + cw0 * pe + cw1 * po + cw2 * pe_up)
        for p in range(2):
            x0_ref[p, h] = uc[p][:, :C].astype(bf16)
            z_ref[p, h] = (uc[p][:, C:2 * C] * uc[p][:, 2 * C:]).astype(bf16)
        four.append((res[:tj, 3 * C:], res[tj:2 * tj, 3 * C:]))
    for p in range(2):
        fa, fb = four[0][p], four[1][p]
        u_ref[0, p] = (fa[:, :C] + fb[:, :C]).astype(bf16)
        u_ref[1, p] = (fa[:, C:] + fb[:, C:]).astype(bf16)
        u_ref[2, p] = (fa[:, :C] - fb[:, :C]).astype(bf16)
        u_ref[3, p] = (fa[:, C:] - fb[:, C:]).astype(bf16)


def _inproj(x, norm_g, wcat, conv_w, conv_b, C):
    B, L, D = x.shape
    Lq = L // 4
    tj = _tile(256, Lq)
    nq = Lq // tj
    N = B * C
    hb = 2 * tj // SUBLANES
    last = L // SUBLANES - 1
    main = lambda off: pl.BlockSpec((1, 2 * tj, D), lambda b, j: (b, j + off, 0))
    prev = lambda off: pl.BlockSpec((1, SUBLANES, D), lambda b, j: (b, jnp.maximum((j + off) * hb - 1, 0), 0))
    nxt = lambda off: pl.BlockSpec((1, SUBLANES, D), lambda b, j: (b, jnp.minimum((j + off + 1) * hb, last), 0))
    full = lambda a: pl.BlockSpec(a.shape, lambda b, j: (0,) * a.ndim)
    g = norm_g.reshape(1, D)
    cb = conv_b.reshape(1, 3 * C)
    return pl.pallas_call(
        functools.partial(_inproj_kernel, C=C, D=D),
        grid=(B, nq),
        in_specs=[main(0), main(nq), prev(0), nxt(0), prev(nq), nxt(nq),
                  full(g), full(wcat), full(conv_w), full(cb)],
        out_specs=[pl.BlockSpec((2, 2, tj, C), lambda b, j: (0, 0, j, b)),
                   pl.BlockSpec((2, 2, tj, C), lambda b, j: (0, 0, j, b)),
                   pl.BlockSpec((4, 2, tj, C), lambda b, j: (0, 0, j, b))],
        out_shape=[jax.ShapeDtypeStruct((2, 2, Lq, N), bf16),
                   jax.ShapeDtypeStruct((2, 2, Lq, N), bf16),
                   jax.ShapeDtypeStruct((4, 2, Lq, N), bf16)],
        scratch_shapes=[pltpu.VMEM((D // LANES, 2 * tj, LANES), f32)],
        compiler_params=_cparams(("parallel", "arbitrary")),
        name="inproj",
    )(x, x, x, x, x, x, g, wcat, conv_w, cb)


def _pack_bf16_pairs(v, is_bf16_exact=False):
    k = v.shape[1] // 2
    rounded = v if is_bf16_exact else v.astype(bf16).astype(f32)
    hi = pltpu.bitcast(rounded[:, :k], jnp.uint32)
    lo = pltpu.bitcast(rounded[:, k:], jnp.uint32)
    return hi | (lo >> 16)


def _unpack_bf16_pairs(u):
    hi = pltpu.bitcast(u & jnp.uint32(0xFFFF0000), f32)
    lo = pltpu.bitcast(u << 16, f32)
    return jnp.concatenate([hi, lo], axis=1)


def _row_tiles_shape(n, D):
    return (n, D // (2 * LANES), LANES)


def _store_row_tiles(ref, packed, row0=0):
    total, per_row, _ = ref.shape
    n = packed.shape[0]
    flat = ref.reshape(total * per_row, LANES)
    for s in range(per_row):
        flat[pl.ds(row0 * per_row + s, n, stride=per_row), :] = packed[:, s * LANES:(s + 1) * LANES]


def _load_row_tiles(ref, row0=0, n=None):
    total, per_row, _ = ref.shape
    n = total if n is None else n
    flat = ref.reshape(total * per_row, LANES)
    return jnp.concatenate([flat[pl.ds(row0 * per_row + s, n, stride=per_row), :] for s in range(per_row)],
                           axis=1)


def _outproj_router_kernel(yh_ref, yf_ref, x_ref, mg_ref, wo_ref, g2_ref, wr_ref, br_ref,
                           h2_ref, tok_ref, pos_ref, vec_ref, tinfo_ref, cnt_ref, carry_ref, stage, *, C):
    n_sub = pos_ref.shape[0]
    tj = pos_ref.shape[2]
    _stage_rows(stage, x_ref[0])

    @pl.when((pl.program_id(0) == 0) & (pl.program_id(1) == 0))
    def _():
        carry_ref[...] = jnp.zeros_like(carry_ref)

    li = lax.broadcasted_iota(jnp.int32, (LANES, LANES), 0)
    lj = lax.broadcasted_iota(jnp.int32, (LANES, LANES), 1)
    hw = _tile(MXU_DIM, C)
    ai = lax.broadcasted_iota(jnp.int32, (hw, hw), 0)
    aj = lax.broadcasted_iota(jnp.int32, (hw, hw), 1)
    avg = jnp.where(ai // HEAD_DIM == aj // HEAD_DIM, 1.0 / HEAD_DIM, 0.0).astype(bf16)
    upper = (li < lj).astype(bf16)
    ri = lax.broadcasted_iota(jnp.int32, (tj, tj), 0)
    ci = lax.broadcasted_iota(jnp.int32, (tj, tj), 1)
    tri = (ci < ri).astype(bf16)
    wr = wr_ref[...]
    wr_hi = wr.astype(bf16)
    wr_lo = (wr - wr_hi.astype(f32)).astype(bf16)
    wr_both = jnp.concatenate([wr_hi, wr_lo], axis=1)
    lane = lax.broadcasted_iota(jnp.int32, (n_sub * tj, LANES), 1).astype(f32)
    sub = lax.broadcasted_iota(jnp.int32, (SUBLANES, LANES), 0)
    neg = jnp.float32(-jnp.inf)
    big = jnp.float32(LANES)

    def first_argmax(vals, mask):
        v = jnp.where(mask, vals, neg)
        m = jnp.max(v, axis=1, keepdims=True)
        idx = jnp.min(jnp.where(mask & (v == m), lane, big), axis=1, keepdims=True)
        return m, idx

    tile_logits = []
    for s in range(n_sub):
        r, p = divmod(s, 2)
        rows = slice(s * tj, (s + 1) * tj)
        parts = []
        for ref in (yh_ref, yf_ref):
            y = ref[p, r * tj:(r + 1) * tj, :].astype(f32)
            ysq = (y * y).astype(bf16)
            ms = jnp.concatenate([jnp.dot(ysq[:, k:k + hw], avg, preferred_element_type=f32)
                                  for k in range(0, C, hw)], axis=1)
            parts.append(y * lax.rsqrt(ms + EPS))
        yn = (jnp.concatenate(parts, axis=1) * mg_ref[...]).astype(bf16)
        h2 = _parity_rows(stage, r * 2 * tj + p, tj) + jnp.dot(yn, wo_ref[...], preferred_element_type=f32)
        h2_ref[rows, :] = h2
        tok = _rms(h2, g2_ref[...])
        tok_ref[rows, :] = tok.astype(bf16)

        t_hi = tok.astype(bf16)
        t_lo = (tok - t_hi.astype(f32)).astype(bf16)
        hi_terms = jnp.dot(t_hi, wr_both, preferred_element_type=f32)
        tile_logits.append(hi_terms[:, :LANES] + hi_terms[:, LANES:]
                           + jnp.dot(t_lo, wr_hi, preferred_element_type=f32))
    logits = jnp.concatenate(tile_logits, axis=0) + br_ref[...]

    gmask = (lane >= N_EXPERTS) & (lane < N_EXPERTS + N_GROUPS)
    gmax, gidx = first_argmax(logits, gmask)
    gsel = gidx - N_EXPERTS
    p_group = 1.0 / jnp.sum(jnp.where(gmask, jnp.exp(logits - gmax), 0.0), axis=1, keepdims=True)
    emask = (lane >= gsel * EXPERTS_PER_GROUP) & (lane < (gsel + 1) * EXPERTS_PER_GROUP)
    l0, e0 = first_argmax(logits, emask)
    l1, e1 = first_argmax(logits, emask & (lane != e0))
    ex = jnp.exp(l1 - l0)
    w0 = p_group / (1.0 + ex)
    w1 = p_group * ex / (1.0 + ex)

    oh0 = (lane == e0).astype(f32)
    oh1 = (lane == e1).astype(f32)
    both = (oh0 + oh1).astype(bf16)
    where, counts = [], []
    for s in range(n_sub):
        tile_both = both[s * tj:(s + 1) * tj, :]
        ranks = jnp.dot(tri, tile_both, preferred_element_type=f32)
        n_e = ranks[tj - 1:tj, :] + tile_both[tj - 1:tj, :].astype(f32)
        n_e8 = jnp.broadcast_to(n_e, (SUBLANES, LANES))
        tile_off = jnp.dot(n_e8.astype(bf16), upper, preferred_element_type=f32)[0:1, :]
        where.append(ranks + tile_off)
        counts.append(n_e8)
    where = jnp.concatenate(where, axis=0)
    pos0 = jnp.sum(oh0 * where, axis=1, keepdims=True)
    pos1 = jnp.sum(oh1 * where, axis=1, keepdims=True)
    cols = jnp.where(lane == 0, w0, jnp.where(lane == 1, w1,
                     jnp.where(lane == 2, pos0, jnp.where(lane == 3, pos1, 0.0))))
    vec_ref[...] = cols
    base = carry_ref[...]
    for s in range(n_sub):
        pos_ref[s] = cols[s * tj:(s + 1) * tj, :].T[:SUBLANES, :].astype(jnp.int32)
        long_seg = (jnp.max(counts[s], axis=1, keepdims=True) >= SEG_SPLIT).astype(f32)
        tinfo_ref[s] = jnp.where(sub == 0, counts[s], jnp.where(sub == 1, base, jnp.where(sub == 2, long_seg, 0.0))
                                 ).astype(jnp.int32)
        base = base + counts[s]
    carry_ref[...] = base
    cnt_ref[...] = base.astype(jnp.int32)


def _outproj_router(yh, yf, x, mix_g, wo, norm2_g, wr, br, C):
    B, L, D = x.shape
    Lh = L // 2
    tj = _tile(256, Lh)
    rt = ROW_TILES_PER_STEP
    ns = 2 * rt
    nj = Lh // (rt * tj)
    T = B * L
    ntiles = T // tj
    lin = lambda b, j: b * nj + j
    full = lambda a: pl.BlockSpec(a.shape, lambda b, j: (0,) * a.ndim)
    mg = mix_g.reshape(1, D)
    g2 = norm2_g.reshape(1, D)
    return pl.pallas_call(
        functools.partial(_outproj_router_kernel, C=C),
        grid=(B, nj),
        in_specs=[pl.BlockSpec((2, rt * tj, C), lambda b, j: (0, j, b)),
                  pl.BlockSpec((2, rt * tj, C), lambda b, j: (0, j, b)),
                  pl.BlockSpec((1, ns * tj, D), lambda b, j: (b, j, 0)),
                  full(mg), full(wo), full(g2), full(wr), full(br)],
        out_specs=[pl.BlockSpec((ns * tj, D), lambda b, j: (lin(b, j), 0)),
                   pl.BlockSpec((ns * tj, D), lambda b, j: (lin(b, j), 0)),
                   pl.BlockSpec((ns, SUBLANES, tj), lambda b, j: (lin(b, j), 0, 0)),
                   pl.BlockSpec((ns * tj, LANES), lambda b, j: (lin(b, j), 0)),
                   pl.BlockSpec((ns, SUBLANES, LANES), lambda b, j: (lin(b, j), 0, 0)),
                   pl.BlockSpec((SUBLANES, LANES), lambda b, j: (0, 0))],
        out_shape=[jax.ShapeDtypeStruct((T, D), f32),
                   jax.ShapeDtypeStruct((T, D), bf16),
                   jax.ShapeDtypeStruct((ntiles, SUBLANES, tj), jnp.int32),
                   jax.ShapeDtypeStruct((T, LANES), f32),
                   jax.ShapeDtypeStruct((ntiles, SUBLANES, LANES), jnp.int32),
                   jax.ShapeDtypeStruct((SUBLANES, LANES), jnp.int32)],
        scratch_shapes=[pltpu.VMEM((SUBLANES, LANES), f32), pltpu.VMEM((D // LANES, ns * tj, LANES), f32)],
        compiler_params=_cparams(("arbitrary", "arbitrary")),
        name="outproj_router",
    )(yh, yf, x, mg, wo, g2, wr, br)


def _padded(c):
    return ((c + MOE_ROWS - 1) // MOE_ROWS) * MOE_ROWS


def _pad_starts(cnt_ref, start_ref):
    def body(e, acc):
        start_ref[e] = acc
        return acc + _padded(cnt_ref[0, e])
    return lax.fori_loop(0, N_EXPERTS, body, jnp.int32(0))


def _segment_copies(n, src, src_off, dst, dst_off, sem, max_rows, wait, src_advances=True, min_rows=1):
    bit = max_rows
    while bit >= min_rows:
        take = n & bit

        @pl.when(take != 0)
        def _(src_off=src_off, dst_off=dst_off, bit=bit):
            cp = pltpu.make_async_copy(src.at[pl.ds(src_off, bit)], dst.at[pl.ds(dst_off, bit)], sem)
            if wait:
                cp.wait()
            else:
                cp.start()
        if src_advances:
            src_off = src_off + take
        dst_off = dst_off + take
        bit //= 2


def _tile_segments(tinfo_ref, t, start_ref, local, local_is_src, remote, sem, tj):
    def copies(long_part):
        off = 0
        for e in range(N_EXPERTS):
            n = tinfo_ref[t, 0, e]
            far = start_ref[e] + tinfo_ref[t, 1, e]
            if long_part:
                args = (n, off, far, tj, SEG_SPLIT)
            else:
                head = n & ~(SEG_SPLIT - 1)
                args = (n, off + head, far + head, SEG_SPLIT // 2, 1)
            n_, near_, far_, hi, lo = args
            if local_is_src:
                _segment_copies(n_, local, near_, remote, far_, sem, hi, wait=False, min_rows=lo)
            else:
                _segment_copies(n_, remote, far_, local, near_, sem, hi, wait=False, min_rows=lo)
            off = off + n

    copies(long_part=False)

    @pl.when(tinfo_ref[t, 2, 0] != 0)
    def _():
        copies(long_part=True)


def _dispatch_kernel(tinfo_ref, cnt_ref, tok_ref, pos_ref, xs_ref, first_ref, nblk_ref,
                     sbufs, zbuf, start_ref, sems, zsem):
    tj = pos_ref.shape[2]
    n_blocks = xs_ref.shape[0] // MOE_ROWS

    @pl.when(pl.program_id(0) == 0)
    def _():
        total = _pad_starts(cnt_ref, start_ref)
        used = total // MOE_ROWS

        def block_ranges(e, c):
            first_ref[e] = start_ref[e] // MOE_ROWS
            nblk_ref[e] = _padded(cnt_ref[0, e]) // MOE_ROWS
            return c
        lax.fori_loop(0, N_EXPERTS, block_ranges, 0)

        zbuf[...] = jnp.zeros_like(zbuf)
        zrows = zbuf.shape[0]
        for wait in (False, True):
            def zero_pad(e, c, wait=wait):
                cnt = cnt_ref[0, e]
                _segment_copies(_padded(cnt) - cnt, zbuf, 0, xs_ref, start_ref[e] + cnt, zsem,
                                zrows, wait, src_advances=False)
                return c
            lax.fori_loop(0, N_EXPERTS, zero_pad, 0)

            def zero_tail(i, c, wait=wait):
                cp = pltpu.make_async_copy(zbuf, xs_ref.at[pl.ds(total + i * zrows, zrows)], zsem)
                if wait:
                    cp.wait()
                else:
                    cp.start()
                return c
            lax.fori_loop(0, (n_blocks - used) * (MOE_ROWS // zrows), zero_tail, 0)

    def wait_tile(k):
        pltpu.make_async_copy(sbufs[k], xs_ref.at[pl.ds(0, 2 * tj)], sems.at[k]).wait()

    slot = lax.broadcasted_iota(jnp.int32, (2 * tj, tj), 0)
    for k in range(TILES_PER_STEP):
        @pl.when(pl.program_id(0) > 0)
        def _(k=k):
            wait_tile(k)
        perm = ((slot == pos_ref[k, 2:3, :]) | (slot == pos_ref[k, 3:4, :])).astype(bf16)
        sorted_rows = jnp.dot(perm, tok_ref[k * tj:(k + 1) * tj, :], preferred_element_type=f32)
        _store_row_tiles(sbufs[k], _pack_bf16_pairs(sorted_rows, is_bf16_exact=True))
        if k > 0:
            _tile_segments(tinfo_ref, k - 1, start_ref, sbufs[k - 1], True, xs_ref, sems.at[k - 1], tj)
    last = TILES_PER_STEP - 1
    _tile_segments(tinfo_ref, last, start_ref, sbufs[last], True, xs_ref, sems.at[last], tj)

    @pl.when(pl.program_id(0) == pl.num_programs(0) - 1)
    def _():
        for k in range(TILES_PER_STEP):
            wait_tile(k)


def _dispatch(tinfo, cnt, tok, pos, n_rows):
    ntiles, _, tj = pos.shape
    T, D = tok.shape
    n_blocks = n_rows // MOE_ROWS
    nt = TILES_PER_STEP
    assert ntiles % nt == 0
    sbuf = pltpu.VMEM(_row_tiles_shape(2 * tj, D), jnp.uint32)
    return pl.pallas_call(
        _dispatch_kernel,
        grid=(ntiles // nt,),
        in_specs=[pl.BlockSpec((nt, SUBLANES, LANES), lambda i: (i, 0, 0), memory_space=pltpu.SMEM),
                  pl.BlockSpec(memory_space=pltpu.SMEM),
                  pl.BlockSpec((nt * tj, D), lambda i: (i, 0)),
                  pl.BlockSpec((nt, SUBLANES, tj), lambda i: (i, 0, 0))],
        out_specs=[pl.BlockSpec(memory_space=pl.ANY),
                   pl.BlockSpec(memory_space=pltpu.SMEM),
                   pl.BlockSpec(memory_space=pltpu.SMEM)],
        out_shape=[jax.ShapeDtypeStruct(_row_tiles_shape(n_rows, D), jnp.uint32),
                   jax.ShapeDtypeStruct((N_EXPERTS,), jnp.int32),
                   jax.ShapeDtypeStruct((N_EXPERTS,), jnp.int32)],
        scratch_shapes=[[sbuf] * nt,
                        pltpu.VMEM(_row_tiles_shape(MOE_ROWS // 2, D), jnp.uint32),
                        pltpu.SMEM((N_EXPERTS,), jnp.int32), pltpu.SemaphoreType.DMA((nt,)),
                        pltpu.SemaphoreType.DMA],
        compiler_params=_cparams(("arbitrary",)),
        name="moe_dispatch",
    )(tinfo, cnt, tok, pos)


def _experts_kernel(first_ref, nblk_ref, xs_ref, wg_ref, wu_ref, wd_ref, ys_ref,
                    wg_s, wu_s, wd_s, xbuf, ybuf, in_sems, out_sems):
    e = pl.program_id(0)
    last_e = pl.num_programs(0) - 1
    rows = MOE_ROWS
    n_total = xs_ref.shape[0] // rows
    used = first_ref[last_e] + nblk_ref[last_e]

    def fetch(g, slot):
        return pltpu.make_async_copy(xs_ref.at[pl.ds(g * rows, rows)], xbuf.at[pl.ds(slot * rows, rows)],
                                     in_sems.at[slot])

    def flush(g, slot):
        return pltpu.make_async_copy(ybuf.at[pl.ds(slot * rows, rows)], ys_ref.at[pl.ds(g * rows, rows)],
                                     out_sems.at[slot])

    @pl.when((e == 0) & (used > 0))
    def _():
        fetch(0, 0).start()

    @pl.when(nblk_ref[e] > 0)
    def _():
        wg_s[...] = wg_ref[...].astype(bf16)
        wu_s[...] = wu_ref[...].astype(bf16)
        wd_s[...] = wd_ref[...].astype(bf16)

    def block(b, carry):
        g = first_ref[e] + b
        slot = g % 2
        fetch(g, slot).wait()

        @pl.when(g + 1 < used)
        def _():
            fetch(g + 1, 1 - slot).start()

        @pl.when(g >= 2)
        def _():
            flush(g - 2, slot).wait()
        half = rows // 2
        starts = [slot * rows + r * half for r in range(2)]
        gated = []
        for row0 in starts:
            xb = _unpack_bf16_pairs(_load_row_tiles(xbuf, row0, half)).astype(bf16)
            gated.append((jnp.dot(xb, wg_s[...], preferred_element_type=f32),
                          jnp.dot(xb, wu_s[...], preferred_element_type=f32)))
        for row0, (a, u) in zip(starts, gated):
            h = (a * jax.nn.sigmoid(a) * u).astype(bf16)
            _store_row_tiles(ybuf, _pack_bf16_pairs(jnp.dot(h, wd_s[...], preferred_element_type=f32)), row0)
        flush(g, slot).start()
        return carry
    lax.fori_loop(0, nblk_ref[e], block, 0)

    @pl.when(e == last_e)
    def _():
        for back in (1, 2):
            @pl.when(used >= back)
            def _(back=back):
                flush(used - back, (used - back) % 2).wait()
        ybuf[pl.ds(0, rows)] = jnp.zeros((rows,) + ybuf.shape[1:], ybuf.dtype)
        for wait in (False, True):
            def zero_tail(g, c, wait=wait):
                cp = flush(g, 0)
                if wait:
                    cp.wait()
                else:
                    cp.start()
                return c
            lax.fori_loop(used, n_total, zero_tail, 0)


def _experts(first_block, n_blocks, xs, w_gate, w_up, w_down):
    _, D, De = w_gate.shape
    widx = lambda e, first, nblk: (e, 0, 0)
    buf = pltpu.VMEM(_row_tiles_shape(2 * MOE_ROWS, D), jnp.uint32)
    return pl.pallas_call(
        _experts_kernel,
        grid_spec=pltpu.PrefetchScalarGridSpec(
            num_scalar_prefetch=2,
            grid=(N_EXPERTS,),
            in_specs=[pl.BlockSpec(memory_space=pl.ANY),
                      pl.BlockSpec((None, D, De), widx),
                      pl.BlockSpec((None, D, De), widx),
                      pl.BlockSpec((None, De, D), widx)],
            out_specs=pl.BlockSpec(memory_space=pl.ANY),
            scratch_shapes=[pltpu.VMEM((D, De), bf16), pltpu.VMEM((D, De), bf16), pltpu.VMEM((De, D), bf16),
                            buf, buf, pltpu.SemaphoreType.DMA((2,)), pltpu.SemaphoreType.DMA((2,))],
        ),
        out_shape=jax.ShapeDtypeStruct(xs.shape, jnp.uint32),
        compiler_params=_cparams(("arbitrary",)),
        name="moe_experts",
    )(first_block, n_blocks, xs, w_gate, w_up, w_down)


def _combine_kernel(tinfo_ref, tnext_ref, cnt_ref, h2_ref, vec_ref, g_ref, ys_ref, o_ref,
                    gbufs, stages, start_ref, sems):
    tj = vec_ref.shape[0] // TILES_PER_STEP
    step = pl.program_id(0)

    def fetch(info_ref, t, buf):
        _tile_segments(info_ref, t, start_ref, gbufs[buf], False, ys_ref, sems.at[buf], tj)

    @pl.when(step == 0)
    def _():
        _pad_starts(cnt_ref, start_ref)
        for t in range(GATHER_AHEAD):
            fetch(tinfo_ref, t, t)

    slot = lax.broadcasted_iota(jnp.int32, (tj, 2 * tj), 1)
    for k in range(TILES_PER_STEP):
        pltpu.make_async_copy(ys_ref.at[pl.ds(0, 2 * tj)], gbufs[k], sems.at[k]).wait()
        ahead = k + GATHER_AHEAD
        if ahead < TILES_PER_STEP:
            fetch(tinfo_ref, ahead, ahead)
        else:
            fetch(tnext_ref, ahead - TILES_PER_STEP, ahead - TILES_PER_STEP)
        rows = _unpack_bf16_pairs(_load_row_tiles(gbufs[k])).astype(bf16)
        v = vec_ref[k * tj:(k + 1) * tj, :]
        y = h2_ref[k * tj:(k + 1) * tj, :]
        for c in range(2):
            pick = (slot == v[:, 2 + c:3 + c].astype(jnp.int32)).astype(bf16)
            y = y + v[:, c:c + 1] * jnp.dot(pick, rows, preferred_element_type=f32)
        out = _rms(y, g_ref[...])
        stage = stages[k // 2]
        for q in range(stage.shape[0]):
            stage[q, pl.ds(k % 2, tj, stride=2), :] = out[:, q * LANES:(q + 1) * LANES]
        if k % 2 == 1:
            r0 = (k // 2) * 2 * tj
            for q in range(stage.shape[0]):
                o_ref[0, r0:r0 + 2 * tj, q * LANES:(q + 1) * LANES] = stage[q]


def _combine(tinfo, cnt, h2, vec, final_g, ys, B, L):
    T, D = h2.shape
    ntiles = tinfo.shape[0]
    tj = T // ntiles
    nt = TILES_PER_STEP
    assert GATHER_AHEAD < nt and nt % 2 == 0 and (L // tj) % nt == 0
    nsteps = ntiles // nt
    per_seq = L // (nt * tj)
    g = final_g.reshape(1, D)
    gbuf = pltpu.VMEM(_row_tiles_shape(2 * tj, D), jnp.uint32)
    return pl.pallas_call(
        _combine_kernel,
        grid=(nsteps,),
        in_specs=[pl.BlockSpec((nt, SUBLANES, LANES), lambda i: (i, 0, 0), memory_space=pltpu.SMEM),
                  pl.BlockSpec((nt, SUBLANES, LANES), lambda i: (i + 1, 0, 0), memory_space=pltpu.SMEM),
                  pl.BlockSpec(memory_space=pltpu.SMEM),
                  pl.BlockSpec((nt * tj, D), lambda i: (i, 0)),
                  pl.BlockSpec((nt * tj, LANES), lambda i: (i, 0)),
                  pl.BlockSpec((1, D), lambda i: (0, 0)),
                  pl.BlockSpec(memory_space=pl.ANY)],
        out_specs=pl.BlockSpec((1, nt * tj, D), lambda i: (i // per_seq, i % per_seq, 0)),
        out_shape=jax.ShapeDtypeStruct((B, L, D), f32),
        scratch_shapes=[[gbuf] * nt,
                        [pltpu.VMEM((D // LANES, 2 * tj, LANES), f32)] * (nt // 2),
                        pltpu.SMEM((N_EXPERTS,), jnp.int32), pltpu.SemaphoreType.DMA((nt,))],
        compiler_params=_cparams(("arbitrary",)),
        name="moe_combine",
    )(tinfo, jnp.pad(tinfo, ((0, nt), (0, 0), (0, 0))), cnt, h2, vec, g, ys)


def kernel(x, norm1_g, w_in, conv_w, conv_b, f_w_in, f_b_in, f_w_mid, f_b_mid, f_freq, f_w_out, f_bias,
           mix_g, w_out, norm2_g, w_group, b_group, w_router, b_router, w_gate, w_up, w_down, final_g):
    B, L, D = x.shape
    depth = norm1_g.shape[0]
    assert depth == 1, "the final RMSNorm is fused into the single layer's MoE combine"
    C = D // 2
    Lh = L // 2
    N = B * C
    T = B * L
    n_rows = T * 2 + N_EXPERTS * MOE_ROWS
    cfwd, cinv, cfou = _dft_consts(L)
    i = 0
    wcat = _prep_win(w_in[i], C)
    hcat, asum = _hyena_filters(L, C, f_w_in[i], f_b_in[i], f_w_mid[i], f_b_mid[i], f_freq[i], f_w_out[i])
    kspec, k0 = _filter_spectrum(cfwd, hcat, asum, L, C)
    z, x0, u4 = _inproj(x, norm1_g[i], wcat, conv_w[i], conv_b[i], C)
    z = z.reshape(2, Lh, N)
    x0 = x0.reshape(2, Lh, N)
    u4 = u4.reshape(4, Lh, N)
    s = _conv_fwd(cfwd, z, kspec, k0, C)
    yh = _conv_inv(cinv, s, x0, z, f_bias[i], C)
    yf = _fourier_dft(cfou, u4, L, C)
    wr = jnp.concatenate([jnp.transpose(w_router[i], (1, 0, 2)).reshape(D, N_EXPERTS), w_group[i]], axis=1)
    wr = jnp.pad(wr, ((0, 0), (0, LANES - wr.shape[1])))
    br = jnp.pad(jnp.concatenate([b_router[i].reshape(-1), b_group[i]]), (0, LANES - N_EXPERTS - N_GROUPS))
    h2, tok, pos, vec, tinfo, cnt = _outproj_router(yh, yf, x, mix_g[i], w_out[i].astype(bf16), norm2_g[i],
                                                     wr, br.reshape(1, LANES), C)
    xs, first_block, n_blocks = _dispatch(tinfo, cnt, tok, pos, n_rows)
    ys = _experts(first_block, n_blocks, xs, w_gate[i], w_up[i], w_down[i])
    return _combine(tinfo, cnt, h2, vec, final_g, ys, B, L)
```

```python
import functools
import math

import jax
import jax.numpy as jnp
from jax import lax
from jax.experimental import pallas as pl
from jax.experimental.pallas import tpu as pltpu

HEAD_DIM = 64
N_GROUPS = 4
EXPERTS_PER_GROUP = 8
N_EXPERTS = N_GROUPS * EXPERTS_PER_GROUP
FILTER_BANDS = 16
DECAY_FAST_PCT = 0.3
DECAY_SLOW_PCT = 1.5
DECAY_TARGET = 1e-2
EPS = 1e-6

LANES = 128
SUBLANES = 8
MXU_DIM = 256
DFT_ROWS = 512
MOE_ROWS = 512
ROW_TILES_PER_STEP = 2
TILES_PER_STEP = 4
GATHER_AHEAD = 2
SEG_SPLIT = 32
VMEM_LIMIT = 56 * 1024 * 1024

f32 = jnp.float32
bf16 = jnp.bfloat16
HI = lax.Precision.HIGHEST


def _cparams(sem):
    return pltpu.CompilerParams(dimension_semantics=sem, vmem_limit_bytes=VMEM_LIMIT)


def _tile(pref, n):
    return min(pref, n)


_SIN_TERMS = tuple((-1.0) ** k / math.factorial(2 * k + 1) for k in range(7))
_COS_TERMS = tuple((-1.0) ** k / math.factorial(2 * k) for k in range(8))


def _cos_sin_of_turn_fraction(prod, m):
    num = prod & (m - 1)
    quadrant = num // (m // 4)
    x = (num & (m // 4 - 1)).astype(f32) * (2.0 * math.pi / m)
    x2 = x * x
    s = jnp.full_like(x, _SIN_TERMS[-1])
    for c in _SIN_TERMS[-2::-1]:
        s = s * x2 + c
    s = s * x
    c_ = jnp.full_like(x, _COS_TERMS[-1])
    for c in _COS_TERMS[-2::-1]:
        c_ = c_ * x2 + c
    odd = (quadrant & 1) == 1
    cos = jnp.where(odd, s, c_)
    sin = jnp.where(odd, c_, s)
    cos = jnp.where((quadrant == 1) | (quadrant == 2), -cos, cos)
    sin = jnp.where(quadrant >= 2, -sin, sin)
    return cos, sin


def _dft_consts_kernel(fwd_ref, inv_ref, fou_ref, *, L):
    Lh = L // 2
    Lq = Lh // 2
    tr = fwd_ref.shape[1]
    r = lax.broadcasted_iota(jnp.int32, (tr, LANES), 0) + pl.program_id(0) * tr
    d = lax.broadcasted_iota(jnp.int32, (tr, LANES), 1)
    alt_r = (1 - 2 * (r & 1)).astype(f32)
    alt_d = (1 - 2 * (d & 1)).astype(f32)
    ncb = Lh // LANES
    two_r1 = 2 * r + 1

    def col_l(cb):
        c0 = cb * LANES
        q, j0 = c0 // Lq, c0 % Lq
        return 2 * j0 + q

    fams = [
        (r, 1, lambda cb: cb * LANES, L),
        (r, 2, lambda cb: 2 * cb * LANES + 1, 2 * L),
        (two_r1, 1, lambda cb: cb * LANES, 2 * L),
        (r, 2, col_l, Lh),
        (two_r1, 2, col_l, L),
    ]
    for fi, (a, s, b0, m) in enumerate(fams):
        tc, ts = _cos_sin_of_turn_fraction(a * (s * d), m)
        b0_lane = jnp.zeros((tr, LANES), jnp.int32)
        for cb in range(ncb):
            b0_lane = jnp.where(d == cb, b0(cb), b0_lane)
        ac, asn = _cos_sin_of_turn_fraction(a * b0_lane, m)
        for cb in range(ncb):
            ca = ac[:, cb:cb + 1]
            sa = asn[:, cb:cb + 1]
            c = ca * tc - sa * ts
            sn = sa * tc + ca * ts
            sl = slice(cb * LANES, (cb + 1) * LANES)
            if fi == 0:
                fwd_ref[0, :, sl] = c.astype(bf16)
                inv_ref[0, :, sl] = c.astype(bf16)
                fwd_ref[1, :, sl] = jnp.where(r == 0, alt_d, sn).astype(bf16)
                nst = -sn
                if cb == 0:
                    nst = jnp.where(d == 0, -alt_r, nst)
                inv_ref[1, :, sl] = nst.astype(bf16)
            elif fi == 1:
                fwd_ref[2, :, sl] = c.astype(bf16)
                fwd_ref[3, :, sl] = jnp.where(r == 0, alt_d, sn).astype(bf16)
            elif fi == 2:
                inv_ref[2, :, sl] = c.astype(bf16)
                nst = -sn
                if cb == 0:
                    nst = jnp.where(d == 0, -alt_r, nst)
                inv_ref[3, :, sl] = nst.astype(bf16)
            elif fi == 3:
                fou_ref[0, :, sl] = c.astype(bf16)
                fou_ref[1, :, sl] = (-sn).astype(bf16)
            else:
                fou_ref[2, :, sl] = c.astype(bf16)
                fou_ref[3, :, sl] = (-sn).astype(bf16)


def _dft_consts(L):
    Lh = L // 2
    tr = _tile(256, Lh)
    shp = jax.ShapeDtypeStruct((4, Lh, Lh), bf16)
    spec = pl.BlockSpec((4, tr, Lh), lambda i: (0, i, 0))
    return pl.pallas_call(
        functools.partial(_dft_consts_kernel, L=L),
        grid=(Lh // tr,),
        out_specs=[spec, spec, spec],
        out_shape=[shp, shp, shp],
        compiler_params=_cparams(("parallel",)),
        name="dft_consts",
    )()


def _prep_win_kernel(w_ref, o_ref, *, C):
    w = w_ref[...]
    i = lax.broadcasted_iota(jnp.int32, (LANES, LANES), 0)
    j = lax.broadcasted_iota(jnp.int32, (LANES, LANES), 1)
    same = (i // HEAD_DIM) == (j // HEAD_DIM)
    cos, sin = _cos_sin_of_turn_fraction((i % HEAD_DIM) * (j % HEAD_DIM), HEAD_DIM)
    bc = jnp.where(same, cos, 0.0)
    bs = jnp.where(same, sin, 0.0)
    o_ref[:, :3 * C] = w[:, :3 * C].astype(bf16)
    for k in range(0, C, LANES):
        wf = w[:, 3 * C + k:3 * C + k + LANES]
        o_ref[:, 3 * C + k:3 * C + k + LANES] = jnp.dot(wf, bc, precision=HI, preferred_element_type=f32).astype(bf16)
        o_ref[:, 4 * C + k:4 * C + k + LANES] = jnp.dot(wf, bs, precision=HI, preferred_element_type=f32).astype(bf16)


def _prep_win(w_in, C):
    D = w_in.shape[0]
    tr = _tile(256, D)
    return pl.pallas_call(
        functools.partial(_prep_win_kernel, C=C),
        grid=(D // tr,),
        in_specs=[pl.BlockSpec((tr, 4 * C), lambda i: (i, 0))],
        out_specs=pl.BlockSpec((tr, 5 * C), lambda i: (i, 0)),
        out_shape=jax.ShapeDtypeStruct((D, 5 * C), bf16),
        compiler_params=_cparams(("parallel",)),
        name="prep_win",
    )(w_in)


def _filter_kernel(win_ref, bin_ref, wmid_ref, bmid_ref, freq_ref, wout_ref, h_ref, asum_ref, *, L, C):
    tr = h_ref.shape[1]
    step = pl.program_id(0)
    lane = lax.broadcasted_iota(jnp.int32, (tr, LANES), 1)
    row = lax.broadcasted_iota(jnp.int32, (tr, LANES), 0) + step * tr
    band_i = jnp.where(lane <= FILTER_BANDS, lane - 1, lane - 1 - FILTER_BANDS)
    band = 1e-4 + band_i.astype(f32) * ((FILTER_BANDS - 1 - 1e-4) / (FILTER_BANDS - 1))
    ch = lax.broadcasted_iota(jnp.int32, (tr, 2 * C), 1) % C
    max_decay = math.log(DECAY_TARGET) / DECAY_FAST_PCT
    min_decay = math.log(DECAY_TARGET) / DECAY_SLOW_PCT
    delta = jnp.abs(min_decay + ch.astype(f32) * ((max_decay - min_decay) / (C - 1)))
    is_bwd = lax.broadcasted_iota(jnp.int32, (tr, 2 * C), 1) >= C
    row_c = lax.broadcasted_iota(jnp.int32, (tr, 2 * C), 0) + step * tr

    @pl.when(step == 0)
    def _():
        asum_ref[...] = jnp.zeros_like(asum_ref)

    total = jnp.zeros((1, 2 * C), f32)
    for p in range(2):
        pos = (2 * row + p).astype(f32)
        t = pos * (1.0 / (L - 1))
        w = pos * (2.0 * math.pi / L)
        fw = band * w
        z = jnp.where(lane == 0, t,
                      jnp.where(lane <= FILTER_BANDS, jnp.cos(fw),
                                jnp.where(lane <= 2 * FILTER_BANDS, -jnp.sin(fw), 0.0)))
        h = jnp.sin(freq_ref[0:1, :] * (jnp.dot(z, win_ref[...], precision=HI, preferred_element_type=f32)
                                       + bin_ref[...]))
        for i in range(wmid_ref.shape[0]):
            h = jnp.sin(freq_ref[i + 1:i + 2, :]
                        * (jnp.dot(h, wmid_ref[i], precision=HI, preferred_element_type=f32)
                           + bmid_ref[i:i + 1, :]))
        out = jnp.dot(h, wout_ref[...], precision=HI, preferred_element_type=f32)
        tc = (2 * row_c + p).astype(f32) * (1.0 / (L - 1))
        out = out * jnp.exp(-tc * delta)
        if p == 0:
            out = jnp.where(is_bwd & (row_c == 0), 0.0, out)
        h_ref[p] = out.astype(bf16)
        total = total + jnp.sum(jnp.abs(out), axis=0, keepdims=True)
    asum_ref[0:1, :] += total


def _hyena_filters(L, C, f_w_in, f_b_in, f_w_mid, f_b_mid, f_freq, f_w_out):
    Lh = L // 2
    tr = _tile(256, Lh)
    order = f_w_in.shape[1]
    win = jnp.pad(f_w_in, ((0, LANES - f_w_in.shape[0]), (0, 0)))
    full = lambda a: pl.BlockSpec(a.shape, lambda i: (0,) * a.ndim)
    args = (win, f_b_in.reshape(1, order), f_w_mid, f_b_mid, f_freq, f_w_out)
    return pl.pallas_call(
        functools.partial(_filter_kernel, L=L, C=C),
        grid=(Lh // tr,),
        in_specs=[full(a) for a in args],
        out_specs=[pl.BlockSpec((2, tr, 2 * C), lambda i: (0, i, 0)),
                   pl.BlockSpec((SUBLANES, 2 * C), lambda i: (0, 0))],
        out_shape=[jax.ShapeDtypeStruct((2, Lh, 2 * C), bf16),
                   jax.ShapeDtypeStruct((SUBLANES, 2 * C), f32)],
        compiler_params=_cparams(("arbitrary",)),
        name="hyena_filters",
    )(*args)


def _fwd_products(c_ref, x_ref):
    xe, xo = x_ref[0], x_ref[1]
    ce = jnp.dot(c_ref[0], xe, preferred_element_type=f32)
    se = jnp.dot(c_ref[1], xe, preferred_element_type=f32)
    co = jnp.dot(c_ref[2], xo, preferred_element_type=f32)
    so = jnp.dot(c_ref[3], xo, preferred_element_type=f32)
    return ce, se, co, so


def _filter_spectrum_kernel(c_ref, h_ref, asum_ref, k_ref, k0_ref, *, L, C):
    ce, se, co, so = _fwd_products(c_ref, h_ref)
    a = asum_ref[0:1, :]
    scale = 1.0 / ((a[:, :C] + a[:, C:]) * L)
    F = lambda v: v[:, :C]
    G = lambda v: v[:, C:]
    k_ref[0] = (F(ce) + F(co) + G(ce) + G(co)) * scale
    k_ref[1] = (-(F(se) + F(so)) + (G(se) + G(so))) * scale
    k_ref[2] = (F(ce) - F(co) + G(ce) - G(co)) * scale
    k_ref[3] = ((F(se) - F(so)) - (G(se) - G(so))) * scale

    @pl.when(pl.program_id(0) == 0)
    def _():
        k0_ref[...] = jnp.zeros_like(k0_ref)
        k0_ref[0:1, :] = ((F(se) + G(se)) * scale)[0:1, :]
        k0_ref[1:2, :] = ((F(so) - G(so)) * scale)[0:1, :]


def _filter_spectrum(cfwd, hcat, asum, L, C):
    Lh = L // 2
    tm = _tile(256, Lh)
    return pl.pallas_call(
        functools.partial(_filter_spectrum_kernel, L=L, C=C),
        grid=(Lh // tm,),
        in_specs=[pl.BlockSpec((4, tm, Lh), lambda i: (0, i, 0)),
                  pl.BlockSpec((2, Lh, 2 * C), lambda i: (0, 0, 0)),
                  pl.BlockSpec((SUBLANES, 2 * C), lambda i: (0, 0))],
        out_specs=[pl.BlockSpec((4, tm, C), lambda i: (0, i, 0)),
                   pl.BlockSpec((SUBLANES, C), lambda i: (0, 0))],
        out_shape=[jax.ShapeDtypeStruct((4, Lh, C), f32),
                   jax.ShapeDtypeStruct((SUBLANES, C), f32)],
        compiler_params=_cparams(("arbitrary",)),
        name="filter_spectrum",
    )(cfwd, hcat, asum)


def _conv_fwd_kernel(c_ref, z_ref, k_ref, k0_ref, s_ref):
    ce, se, co, so = _fwd_products(c_ref, z_ref)
    zsr, zsi, zdr, zdi = ce + co, -(se + so), ce - co, se - so
    ksr, ksi, kdr, kdi = k_ref[0], k_ref[1], k_ref[2], k_ref[3]
    psr = zsr * ksr - zsi * ksi
    psi = zsr * ksi + zsi * ksr
    pdr = zdr * kdr - zdi * kdi
    pdi = zdr * kdi + zdi * kdr
    gen = (psr + pdr, psi - pdi, psr - pdr, psi + pdi)
    first = pl.program_id(1) == 0

    @pl.when(first)
    def _():
        row0 = lax.broadcasted_iota(jnp.int32, ce.shape, 0) == 0
        p0 = zsr * ksr
        pl_ = zdr * kdr
        ak, bk = k0_ref[0:1, :], k0_ref[1:2, :]
        phr = se * ak - so * bk
        phi = -(se * bk + so * ak)
        spec = (0.5 * (p0 + pl_), -phr, 0.5 * (p0 - pl_), phi)
        for i in range(4):
            s_ref[i] = jnp.where(row0, spec[i], gen[i]).astype(bf16)

    @pl.when(jnp.logical_not(first))
    def _():
        for i in range(4):
            s_ref[i] = gen[i].astype(bf16)


def _conv_fwd(cfwd, z, kspec, k0, C):
    _, Lh, N = z.shape
    tm = _tile(DFT_ROWS, Lh)
    return pl.pallas_call(
        _conv_fwd_kernel,
        grid=(N // C, Lh // tm),
        in_specs=[pl.BlockSpec((4, tm, Lh), lambda n, i: (0, i, 0)),
                  pl.BlockSpec((2, Lh, C), lambda n, i: (0, 0, n)),
                  pl.BlockSpec((4, tm, C), lambda n, i: (0, i, 0)),
                  pl.BlockSpec((SUBLANES, C), lambda n, i: (0, 0))],
        out_specs=pl.BlockSpec((4, tm, C), lambda n, i: (0, i, n)),
        out_shape=jax.ShapeDtypeStruct((4, Lh, N), bf16),
        compiler_params=_cparams(("parallel", "arbitrary")),
        name="conv_fwd",
    )(cfwd, z, kspec, k0)


def _pair_products(c_ref, r_ref):
    ev = (jnp.dot(c_ref[0], r_ref[0], preferred_element_type=f32)
          + jnp.dot(c_ref[1], r_ref[1], preferred_element_type=f32))
    od = (jnp.dot(c_ref[2], r_ref[2], preferred_element_type=f32)
          + jnp.dot(c_ref[3], r_ref[3], preferred_element_type=f32))
    return ev, od


def _conv_inv_kernel(c_ref, s_ref, x0_ref, z_ref, fb_ref, y_ref):
    conv = _pair_products(c_ref, s_ref)
    fb = fb_ref[...]
    for p in range(2):
        zp = z_ref[p].astype(f32)
        y_ref[p] = (x0_ref[p].astype(f32) * (conv[p] + zp * fb)).astype(bf16)


def _conv_inv(cinv, s, x0, z, f_bias, C):
    _, Lh, N = s.shape
    tm = _tile(DFT_ROWS, Lh)
    return pl.pallas_call(
        _conv_inv_kernel,
        grid=(N // C, Lh // tm),
        in_specs=[pl.BlockSpec((4, tm, Lh), lambda n, i: (0, i, 0)),
                  pl.BlockSpec((4, Lh, C), lambda n, i: (0, 0, n)),
                  pl.BlockSpec((2, tm, C), lambda n, i: (0, i, n)),
                  pl.BlockSpec((2, tm, C), lambda n, i: (0, i, n)),
                  pl.BlockSpec((1, C), lambda n, i: (0, 0))],
        out_specs=pl.BlockSpec((2, tm, C), lambda n, i: (0, i, n)),
        out_shape=jax.ShapeDtypeStruct((2, Lh, N), bf16),
        compiler_params=_cparams(("parallel", "arbitrary")),
        name="conv_inv",
    )(cinv, s, x0, z, f_bias.reshape(1, C))


def _fourier_kernel(c_ref, u_ref, y_ref, *, scale):
    ev, od = _pair_products(c_ref, u_ref)
    y_ref[0] = (ev * scale).astype(bf16)
    y_ref[1] = (od * scale).astype(bf16)


def _fourier_dft(cfou, u4, L, C):
    _, Lh, N = u4.shape
    tm = _tile(DFT_ROWS, Lh)
    return pl.pallas_call(
        functools.partial(_fourier_kernel, scale=1.0 / math.sqrt(L * HEAD_DIM)),
        grid=(N // C, Lh // tm),
        in_specs=[pl.BlockSpec((4, tm, Lh), lambda n, i: (0, i, 0)),
                  pl.BlockSpec((4, Lh, C), lambda n, i: (0, 0, n))],
        out_specs=pl.BlockSpec((2, tm, C), lambda n, i: (0, i, n)),
        out_shape=jax.ShapeDtypeStruct((2, Lh, N), bf16),
        compiler_params=_cparams(("parallel", "arbitrary")),
        name="fourier_dft",
    )(cfou, u4)


def _rms(x, g):
    return x * lax.rsqrt(jnp.mean(x * x, axis=-1, keepdims=True) + EPS) * g


def _stage_rows(chunks, x):
    for k in range(chunks.shape[0]):
        chunks[k] = x[:, k * LANES:(k + 1) * LANES]


def _parity_rows(chunks, p, n):
    return jnp.concatenate([chunks[k, pl.ds(p, n, stride=2), :] for k in range(chunks.shape[0])], axis=1)


def _inproj_kernel(xa_ref, xb_ref, pa_ref, na_ref, pb_ref, nb_ref, g_ref, w_ref, cw_ref, cb_ref,
                   z_ref, x0_ref, u_ref, stage, *, C, D):
    tj = xa_ref.shape[1] // 2
    jt = pl.program_id(1)
    n_half = pl.num_programs(1)
    g = g_ref[...]
    w = w_ref[...]
    cw0, cw1, cw2 = cw_ref[0:1, :], cw_ref[1:2, :], cw_ref[2:3, :]
    cb = cb_ref[...]
    row = lax.broadcasted_iota(jnp.int32, (tj, 3 * C), 0)
    four = []
    halves = ((xa_ref, pa_ref, na_ref, jt == 0, False),
              (xb_ref, pb_ref, nb_ref, False, jt == n_half - 1))
    for h, (x_ref, p_ref, n_ref, at_start, at_end) in enumerate(halves):
        _stage_rows(stage, x_ref[0])
        lhs = jnp.concatenate([
            _rms(_parity_rows(stage, 0, tj), g).astype(bf16),
            _rms(_parity_rows(stage, 1, tj), g).astype(bf16),
            _rms(p_ref[0], g).astype(bf16),
            _rms(n_ref[0], g).astype(bf16)], axis=0)
        res = jnp.dot(lhs, w, preferred_element_type=f32)
        pe, po = res[:tj, :3 * C], res[tj:2 * tj, :3 * C]
        prev = res[2 * tj + SUBLANES - 1:2 * tj + SUBLANES, :3 * C]
        nxt = res[2 * tj + SUBLANES:2 * tj + SUBLANES + 1, :3 * C]
        if at_start is not False:
            prev = jnp.where(at_start, 0.0, prev)
        if at_end is not False:
            nxt = jnp.where(at_end, 0.0, nxt)
        po_dn = jnp.where(row == 0, prev, pltpu.roll(po, 1, 0))
        pe_up = jnp.where(row == tj - 1, nxt, pltpu.roll(pe, tj - 1, 0))
        uc = (cb + cw0 * po_dn + cw1 * pe + cw2 * po,
              cb + cw0 * pe + cw1 * po + cw2 * pe_up)
        for p in range(2):
            x0_ref[p, h] = uc[p][:, :C].astype(bf16)
            z_ref[p, h] = (uc[p][:, C:2 * C] * uc[p][:, 2 * C:]).astype(bf16)
        four.append((res[:tj, 3 * C:], res[tj:2 * tj, 3 * C:]))
    for p in range(2):
        fa, fb = four[0][p], four[1][p]
        u_ref[0, p] = (fa[:, :C] + fb[:, :C]).astype(bf16)
        u_ref[1, p] = (fa[:, C:] + fb[:, C:]).astype(bf16)
        u_ref[2, p] = (fa[:, :C] - fb[:, :C]).astype(bf16)
        u_ref[3, p] = (fa[:, C:] - fb[:, C:]).astype(bf16)


def _inproj(x, norm_g, wcat, conv_w, conv_b, C):
    B, L, D = x.shape
    Lq = L // 4
    tj = _tile(256, Lq)
    nq = Lq // tj
    N = B * C
    hb = 2 * tj // SUBLANES
    last = L // SUBLANES - 1
    main = lambda off: pl.BlockSpec((1, 2 * tj, D), lambda b, j: (b, j + off, 0))
    prev = lambda off: pl.BlockSpec((1, SUBLANES, D), lambda b, j: (b, jnp.maximum((j + off) * hb - 1, 0), 0))
    nxt = lambda off: pl.BlockSpec((1, SUBLANES, D), lambda b, j: (b, jnp.minimum((j + off + 1) * hb, last), 0))
    full = lambda a: pl.BlockSpec(a.shape, lambda b, j: (0,) * a.ndim)
    g = norm_g.reshape(1, D)
    cb = conv_b.reshape(1, 3 * C)
    return pl.pallas_call(
        functools.partial(_inproj_kernel, C=C, D=D),
        grid=(B, nq),
        in_specs=[main(0), main(nq), prev(0), nxt(0), prev(nq), nxt(nq),
                  full(g), full(wcat), full(conv_w), full(cb)],
        out_specs=[pl.BlockSpec((2, 2, tj, C), lambda b, j: (0, 0, j, b)),
                   pl.BlockSpec((2, 2, tj, C), lambda b, j: (0, 0, j, b)),
                   pl.BlockSpec((4, 2, tj, C), lambda b, j: (0, 0, j, b))],
        out_shape=[jax.ShapeDtypeStruct((2, 2, Lq, N), bf16),
                   jax.ShapeDtypeStruct((2, 2, Lq, N), bf16),
                   jax.ShapeDtypeStruct((4, 2, Lq, N), bf16)],
        scratch_shapes=[pltpu.VMEM((D // LANES, 2 * tj, LANES), f32)],
        compiler_params=_cparams(("parallel", "arbitrary")),
        name="inproj",
    )(x, x, x, x, x, x, g, wcat, conv_w, cb)


def _pack_bf16_pairs(v, is_bf16_exact=False):
    k = v.shape[1] // 2
    rounded = v if is_bf16_exact else v.astype(bf16).astype(f32)
    hi = pltpu.bitcast(rounded[:, :k], jnp.uint32)
    lo = pltpu.bitcast(rounded[:, k:], jnp.uint32)
    return hi | (lo >> 16)


def _unpack_bf16_pairs(u):
    hi = pltpu.bitcast(u & jnp.uint32(0xFFFF0000), f32)
    lo = pltpu.bitcast(u << 16, f32)
    return jnp.concatenate([hi, lo], axis=1)


def _row_tiles_shape(n, D):
    return (n, D // (2 * LANES), LANES)


def _store_row_tiles(ref, packed, row0=0):
    total, per_row, _ = ref.shape
    n = packed.shape[0]
    flat = ref.reshape(total * per_row, LANES)
    for s in range(per_row):
        flat[pl.ds(row0 * per_row + s, n, stride=per_row), :] = packed[:, s * LANES:(s + 1) * LANES]


def _load_row_tiles(ref, row0=0, n=None):
    total, per_row, _ = ref.shape
    n = total if n is None else n
    flat = ref.reshape(total * per_row, LANES)
    return jnp.concatenate([flat[pl.ds(row0 * per_row + s, n, stride=per_row), :] for s in range(per_row)],
                           axis=1)


def _outproj_router_kernel(yh_ref, yf_ref, x_ref, mg_ref, wo_ref, g2_ref, wr_ref, br_ref,
                           h2_ref, tok_ref, pos_ref, vec_ref, tinfo_ref, cnt_ref, carry_ref, stage, *, C):
    n_sub = pos_ref.shape[0]
    tj = pos_ref.shape[2]
    _stage_rows(stage, x_ref[0])

    @pl.when((pl.program_id(0) == 0) & (pl.program_id(1) == 0))
    def _():
        carry_ref[...] = jnp.zeros_like(carry_ref)

    li = lax.broadcasted_iota(jnp.int32, (LANES, LANES), 0)
    lj = lax.broadcasted_iota(jnp.int32, (LANES, LANES), 1)
    hw = _tile(MXU_DIM, C)
    ai = lax.broadcasted_iota(jnp.int32, (hw, hw), 0)
    aj = lax.broadcasted_iota(jnp.int32, (hw, hw), 1)
    avg = jnp.where(ai // HEAD_DIM == aj // HEAD_DIM, 1.0 / HEAD_DIM, 0.0).astype(bf16)
    upper = (li < lj).astype(bf16)
    ri = lax.broadcasted_iota(jnp.int32, (tj, tj), 0)
    ci = lax.broadcasted_iota(jnp.int32, (tj, tj), 1)
    tri = (ci < ri).astype(bf16)
    wr = wr_ref[...]
    wr_hi = wr.astype(bf16)
    wr_lo = (wr - wr_hi.astype(f32)).astype(bf16)
    wr_both = jnp.concatenate([wr_hi, wr_lo], axis=1)
    lane = lax.broadcasted_iota(jnp.int32, (n_sub * tj, LANES), 1).astype(f32)
    sub = lax.broadcasted_iota(jnp.int32, (SUBLANES, LANES), 0)
    neg = jnp.float32(-jnp.inf)
    big = jnp.float32(LANES)

    def first_argmax(vals, mask):
        v = jnp.where(mask, vals, neg)
        m = jnp.max(v, axis=1, keepdims=True)
        idx = jnp.min(jnp.where(mask & (v == m), lane, big), axis=1, keepdims=True)
        return m, idx

    tile_logits = []
    for s in range(n_sub):
        r, p = divmod(s, 2)
        rows = slice(s * tj, (s + 1) * tj)
        parts = []
        for ref in (yh_ref, yf_ref):
            y = ref[p, r * tj:(r + 1) * tj, :].astype(f32)
            ysq = (y * y).astype(bf16)
            ms = jnp.concatenate([jnp.dot(ysq[:, k:k + hw], avg, preferred_element_type=f32)
                                  for k in range(0, C, hw)], axis=1)
            parts.append(y * lax.rsqrt(ms + EPS))
        yn = (jnp.concatenate(parts, axis=1) * mg_ref[...]).astype(bf16)
        h2 = _parity_rows(stage, r * 2 * tj + p, tj) + jnp.dot(yn, wo_ref[...], preferred_element_type=f32)
        h2_ref[rows, :] = h2
        tok = _rms(h2, g2_ref[...])
        tok_ref[rows, :] = tok.astype(bf16)

        t_hi = tok.astype(bf16)
        t_lo = (tok - t_hi.astype(f32)).astype(bf16)
        hi_terms = jnp.dot(t_hi, wr_both, preferred_element_type=f32)
        tile_logits.append(hi_terms[:, :LANES] + hi_terms[:, LANES:]
                           + jnp.dot(t_lo, wr_hi, preferred_element_type=f32))
    logits = jnp.concatenate(tile_logits, axis=0) + br_ref[...]

    gmask = (lane >= N_EXPERTS) & (lane < N_EXPERTS + N_GROUPS)
    gmax, gidx = first_argmax(logits, gmask)
    gsel = gidx - N_EXPERTS
    p_group = 1.0 / jnp.sum(jnp.where(gmask, jnp.exp(logits - gmax), 0.0), axis=1, keepdims=True)
    emask = (lane >= gsel * EXPERTS_PER_GROUP) & (lane < (gsel + 1) * EXPERTS_PER_GROUP)
    l0, e0 = first_argmax(logits, emask)
    l1, e1 = first_argmax(logits, emask & (lane != e0))
    ex = jnp.exp(l1 - l0)
    w0 = p_group / (1.0 + ex)
    w1 = p_group * ex / (1.0 + ex)

    oh0 = (lane == e0).astype(f32)
    oh1 = (lane == e1).astype(f32)
    both = (oh0 + oh1).astype(bf16)
    where, counts = [], []
    for s in range(n_sub):
        tile_both = both[s * tj:(s + 1) * tj, :]
        ranks = jnp.dot(tri, tile_both, preferred_element_type=f32)
        n_e = ranks[tj - 1:tj, :] + tile_both[tj - 1:tj, :].astype(f32)
        n_e8 = jnp.broadcast_to(n_e, (SUBLANES, LANES))
        tile_off = jnp.dot(n_e8.astype(bf16), upper, preferred_element_type=f32)[0:1, :]
        where.append(ranks + tile_off)
        counts.append(n_e8)
    where = jnp.concatenate(where, axis=0)
    pos0 = jnp.sum(oh0 * where, axis=1, keepdims=True)
    pos1 = jnp.sum(oh1 * where, axis=1, keepdims=True)
    cols = jnp.where(lane == 0, w0, jnp.where(lane == 1, w1,
                     jnp.where(lane == 2, pos0, jnp.where(lane == 3, pos1, 0.0))))
    vec_ref[...] = cols
    base = carry_ref[...]
    for s in range(n_sub):
        pos_ref[s] = cols[s * tj:(s + 1) * tj, :].T[:SUBLANES, :].astype(jnp.int32)
        long_seg = (jnp.max(counts[s], axis=1, keepdims=True) >= SEG_SPLIT).astype(f32)
        tinfo_ref[s] = jnp.where(sub == 0, counts[s], jnp.where(sub == 1, base, jnp.where(sub == 2, long_seg, 0.0))
                                 ).astype(jnp.int32)
        base = base + counts[s]
    carry_ref[...] = base
    cnt_ref[...] = base.astype(jnp.int32)


def _outproj_router(yh, yf, x, mix_g, wo, norm2_g, wr, br, C):
    B, L, D = x.shape
    Lh = L // 2
    tj = _tile(256, Lh)
    rt = ROW_TILES_PER_STEP
    ns = 2 * rt
    nj = Lh // (rt * tj)
    T = B * L
    ntiles = T // tj
    lin = lambda b, j: b * nj + j
    full = lambda a: pl.BlockSpec(a.shape, lambda b, j: (0,) * a.ndim)
    mg = mix_g.reshape(1, D)
    g2 = norm2_g.reshape(1, D)
    return pl.pallas_call(
        functools.partial(_outproj_router_kernel, C=C),
        grid=(B, nj),
        in_specs=[pl.BlockSpec((2, rt * tj, C), lambda b, j: (0, j, b)),
                  pl.BlockSpec((2, rt * tj, C), lambda b, j: (0, j, b)),
                  pl.BlockSpec((1, ns * tj, D), lambda b, j: (b, j, 0)),
                  full(mg), full(wo), full(g2), full(wr), full(br)],
        out_specs=[pl.BlockSpec((ns * tj, D), lambda b, j: (lin(b, j), 0)),
                   pl.BlockSpec((ns * tj, D), lambda b, j: (lin(b, j), 0)),
                   pl.BlockSpec((ns, SUBLANES, tj), lambda b, j: (lin(b, j), 0, 0)),
                   pl.BlockSpec((ns * tj, LANES), lambda b, j: (lin(b, j), 0)),
                   pl.BlockSpec((ns, SUBLANES, LANES), lambda b, j: (lin(b, j), 0, 0)),
                   pl.BlockSpec((SUBLANES, LANES), lambda b, j: (0, 0))],
        out_shape=[jax.ShapeDtypeStruct((T, D), f32),
                   jax.ShapeDtypeStruct((T, D), bf16),
                   jax.ShapeDtypeStruct((ntiles, SUBLANES, tj), jnp.int32),
                   jax.ShapeDtypeStruct((T, LANES), f32),
                   jax.ShapeDtypeStruct((ntiles, SUBLANES, LANES), jnp.int32),
                   jax.ShapeDtypeStruct((SUBLANES, LANES), jnp.int32)],
        scratch_shapes=[pltpu.VMEM((SUBLANES, LANES), f32), pltpu.VMEM((D // LANES, ns * tj, LANES), f32)],
        compiler_params=_cparams(("arbitrary", "arbitrary")),
        name="outproj_router",
    )(yh, yf, x, mg, wo, g2, wr, br)


def _padded(c):
    return ((c + MOE_ROWS - 1) // MOE_ROWS) * MOE_ROWS


def _pad_starts(cnt_ref, start_ref):
    def body(e, acc):
        start_ref[e] = acc
        return acc + _padded(cnt_ref[0, e])
    return lax.fori_loop(0, N_EXPERTS, body, jnp.int32(0))


def _segment_copies(n, src, src_off, dst, dst_off, sem, max_rows, wait, src_advances=True, min_rows=1):
    bit = max_rows
    while bit >= min_rows:
        take = n & bit

        @pl.when(take != 0)
        def _(src_off=src_off, dst_off=dst_off, bit=bit):
            cp = pltpu.make_async_copy(src.at[pl.ds(src_off, bit)], dst.at[pl.ds(dst_off, bit)], sem)
            if wait:
                cp.wait()
            else:
                cp.start()
        if src_advances:
            src_off = src_off + take
        dst_off = dst_off + take
        bit //= 2


def _tile_segments(tinfo_ref, t, start_ref, local, local_is_src, remote, sem, tj):
    def copies(long_part):
        off = 0
        for e in range(N_EXPERTS):
            n = tinfo_ref[t, 0, e]
            far = start_ref[e] + tinfo_ref[t, 1, e]
            if long_part:
                args = (n, off, far, tj, SEG_SPLIT)
            else:
                head = n & ~(SEG_SPLIT - 1)
                args = (n, off + head, far + head, SEG_SPLIT // 2, 1)
            n_, near_, far_, hi, lo = args
            if local_is_src:
                _segment_copies(n_, local, near_, remote, far_, sem, hi, wait=False, min_rows=lo)
            else:
                _segment_copies(n_, remote, far_, local, near_, sem, hi, wait=False, min_rows=lo)
            off = off + n

    copies(long_part=False)

    @pl.when(tinfo_ref[t, 2, 0] != 0)
    def _():
        copies(long_part=True)


def _dispatch_kernel(tinfo_ref, cnt_ref, tok_ref, pos_ref, xs_ref, first_ref, nblk_ref,
                     sbufs, zbuf, start_ref, sems, zsem):
    tj = pos_ref.shape[2]
    n_blocks = xs_ref.shape[0] // MOE_ROWS

    @pl.when(pl.program_id(0) == 0)
    def _():
        total = _pad_starts(cnt_ref, start_ref)
        used = total // MOE_ROWS

        def block_ranges(e, c):
            first_ref[e] = start_ref[e] // MOE_ROWS
            nblk_ref[e] = _padded(cnt_ref[0, e]) // MOE_ROWS
            return c
        lax.fori_loop(0, N_EXPERTS, block_ranges, 0)

        zbuf[...] = jnp.zeros_like(zbuf)
        zrows = zbuf.shape[0]
        for wait in (False, True):
            def zero_pad(e, c, wait=wait):
                cnt = cnt_ref[0, e]
                _segment_copies(_padded(cnt) - cnt, zbuf, 0, xs_ref, start_ref[e] + cnt, zsem,
                                zrows, wait, src_advances=False)
                return c
            lax.fori_loop(0, N_EXPERTS, zero_pad, 0)

            def zero_tail(i, c, wait=wait):
                cp = pltpu.make_async_copy(zbuf, xs_ref.at[pl.ds(total + i * zrows, zrows)], zsem)
                if wait:
                    cp.wait()
                else:
                    cp.start()
                return c
            lax.fori_loop(0, (n_blocks - used) * (MOE_ROWS // zrows), zero_tail, 0)

    def wait_tile(k):
        pltpu.make_async_copy(sbufs[k], xs_ref.at[pl.ds(0, 2 * tj)], sems.at[k]).wait()

    slot = lax.broadcasted_iota(jnp.int32, (2 * tj, tj), 0)
    for k in range(TILES_PER_STEP):
        @pl.when(pl.program_id(0) > 0)
        def _(k=k):
            wait_tile(k)
        perm = ((slot == pos_ref[k, 2:3, :]) | (slot == pos_ref[k, 3:4, :])).astype(bf16)
        sorted_rows = jnp.dot(perm, tok_ref[k * tj:(k + 1) * tj, :], preferred_element_type=f32)
        _store_row_tiles(sbufs[k], _pack_bf16_pairs(sorted_rows, is_bf16_exact=True))
        if k > 0:
            _tile_segments(tinfo_ref, k - 1, start_ref, sbufs[k - 1], True, xs_ref, sems.at[k - 1], tj)
    last = TILES_PER_STEP - 1
    _tile_segments(tinfo_ref, last, start_ref, sbufs[last], True, xs_ref, sems.at[last], tj)

    @pl.when(pl.program_id(0) == pl.num_programs(0) - 1)
    def _():
        for k in range(TILES_PER_STEP):
            wait_tile(k)


def _dispatch(tinfo, cnt, tok, pos, n_rows):
    ntiles, _, tj = pos.shape
    T, D = tok.shape
    n_blocks = n_rows // MOE_ROWS
    nt = TILES_PER_STEP
    assert ntiles % nt == 0
    sbuf = pltpu.VMEM(_row_tiles_shape(2 * tj, D), jnp.uint32)
    return pl.pallas_call(
        _dispatch_kernel,
        grid=(ntiles // nt,),
        in_specs=[pl.BlockSpec((nt, SUBLANES, LANES), lambda i: (i, 0, 0), memory_space=pltpu.SMEM),
                  pl.BlockSpec(memory_space=pltpu.SMEM),
                  pl.BlockSpec((nt * tj, D), lambda i: (i, 0)),
                  pl.BlockSpec((nt, SUBLANES, tj), lambda i: (i, 0, 0))],
        out_specs=[pl.BlockSpec(memory_space=pl.ANY),
                   pl.BlockSpec(memory_space=pltpu.SMEM),
                   pl.BlockSpec(memory_space=pltpu.SMEM)],
        out_shape=[jax.ShapeDtypeStruct(_row_tiles_shape(n_rows, D), jnp.uint32),
                   jax.ShapeDtypeStruct((N_EXPERTS,), jnp.int32),
                   jax.ShapeDtypeStruct((N_EXPERTS,), jnp.int32)],
        scratch_shapes=[[sbuf] * nt,
                        pltpu.VMEM(_row_tiles_shape(MOE_ROWS // 2, D), jnp.uint32),
                        pltpu.SMEM((N_EXPERTS,), jnp.int32), pltpu.SemaphoreType.DMA((nt,)),
                        pltpu.SemaphoreType.DMA],
        compiler_params=_cparams(("arbitrary",)),
        name="moe_dispatch",
    )(tinfo, cnt, tok, pos)


def _experts_kernel(first_ref, nblk_ref, xs_ref, wg_ref, wu_ref, wd_ref, ys_ref,
                    wg_s, wu_s, wd_s, xbuf, ybuf, in_sems, out_sems):
    e = pl.program_id(0)
    last_e = pl.num_programs(0) - 1
    rows = MOE_ROWS
    n_total = xs_ref.shape[0] // rows
    used = first_ref[last_e] + nblk_ref[last_e]

    def fetch(g, slot):
        return pltpu.make_async_copy(xs_ref.at[pl.ds(g * rows, rows)], xbuf.at[pl.ds(slot * rows, rows)],
                                     in_sems.at[slot])

    def flush(g, slot):
        return pltpu.make_async_copy(ybuf.at[pl.ds(slot * rows, rows)], ys_ref.at[pl.ds(g * rows, rows)],
                                     out_sems.at[slot])

    @pl.when((e == 0) & (used > 0))
    def _():
        fetch(0, 0).start()

    @pl.when(nblk_ref[e] > 0)
    def _():
        wg_s[...] = wg_ref[...].astype(bf16)
        wu_s[...] = wu_ref[...].astype(bf16)
        wd_s[...] = wd_ref[...].astype(bf16)

    def block(b, carry):
        g = first_ref[e] + b
        slot = g % 2
        fetch(g, slot).wait()

        @pl.when(g + 1 < used)
        def _():
            fetch(g + 1, 1 - slot).start()

        @pl.when(g >= 2)
        def _():
            flush(g - 2, slot).wait()
        half = rows // 2
        starts = [slot * rows + r * half for r in range(2)]
        gated = []
        for row0 in starts:
            xb = _unpack_bf16_pairs(_load_row_tiles(xbuf, row0, half)).astype(bf16)
            gated.append((jnp.dot(xb, wg_s[...], preferred_element_type=f32),
                          jnp.dot(xb, wu_s[...], preferred_element_type=f32)))
        for row0, (a, u) in zip(starts, gated):
            h = (a * jax.nn.sigmoid(a) * u).astype(bf16)
            _store_row_tiles(ybuf, _pack_bf16_pairs(jnp.dot(h, wd_s[...], preferred_element_type=f32)), row0)
        flush(g, slot).start()
        return carry
    lax.fori_loop(0, nblk_ref[e], block, 0)

    @pl.when(e == last_e)
    def _():
        for back in (1, 2):
            @pl.when(used >= back)
            def _(back=back):
                flush(used - back, (used - back) % 2).wait()
        ybuf[pl.ds(0, rows)] = jnp.zeros((rows,) + ybuf.shape[1:], ybuf.dtype)
        for wait in (False, True):
            def zero_tail(g, c, wait=wait):
                cp = flush(g, 0)
                if wait:
                    cp.wait()
                else:
                    cp.start()
                return c
            lax.fori_loop(used, n_total, zero_tail, 0)


def _experts(first_block, n_blocks, xs, w_gate, w_up, w_down):
    _, D, De = w_gate.shape
    widx = lambda e, first, nblk: (e, 0, 0)
    buf = pltpu.VMEM(_row_tiles_shape(2 * MOE_ROWS, D), jnp.uint32)
    return pl.pallas_call(
        _experts_kernel,
        grid_spec=pltpu.PrefetchScalarGridSpec(
            num_scalar_prefetch=2,
            grid=(N_EXPERTS,),
            in_specs=[pl.BlockSpec(memory_space=pl.ANY),
                      pl.BlockSpec((None, D, De), widx),
                      pl.BlockSpec((None, D, De), widx),
                      pl.BlockSpec((None, De, D), widx)],
            out_specs=pl.BlockSpec(memory_space=pl.ANY),
            scratch_shapes=[pltpu.VMEM((D, De), bf16), pltpu.VMEM((D, De), bf16), pltpu.VMEM((De, D), bf16),
                            buf, buf, pltpu.SemaphoreType.DMA((2,)), pltpu.SemaphoreType.DMA((2,))],
        ),
        out_shape=jax.ShapeDtypeStruct(xs.shape, jnp.uint32),
        compiler_params=_cparams(("arbitrary",)),
        name="moe_experts",
    )(first_block, n_blocks, xs, w_gate, w_up, w_down)


def _combine_kernel(tinfo_ref, tnext_ref, cnt_ref, h2_ref, vec_ref, g_ref, ys_ref, o_ref,
                    gbufs, stages, start_ref, sems):
    tj = vec_ref.shape[0] // TILES_PER_STEP
    step = pl.program_id(0)

    def fetch(info_ref, t, buf):
        _tile_segments(info_ref, t, start_ref, gbufs[buf], False, ys_ref, sems.at[buf], tj)

    @pl.when(step == 0)
    def _():
        _pad_starts(cnt_ref, start_ref)
        for t in range(GATHER_AHEAD):
            fetch(tinfo_ref, t, t)

    slot = lax.broadcasted_iota(jnp.int32, (tj, 2 * tj), 1)
    for k in range(TILES_PER_STEP):
        pltpu.make_async_copy(ys_ref.at[pl.ds(0, 2 * tj)], gbufs[k], sems.at[k]).wait()
        ahead = k + GATHER_AHEAD
        if ahead < TILES_PER_STEP:
            fetch(tinfo_ref, ahead, ahead)
        else:
            fetch(tnext_ref, ahead - TILES_PER_STEP, ahead - TILES_PER_STEP)
        rows = _unpack_bf16_pairs(_load_row_tiles(gbufs[k])).astype(bf16)
        v = vec_ref[k * tj:(k + 1) * tj, :]
        y = h2_ref[k * tj:(k + 1) * tj, :]
        for c in range(2):
            pick = (slot == v[:, 2 + c:3 + c].astype(jnp.int32)).astype(bf16)
            y = y + v[:, c:c + 1] * jnp.dot(pick, rows, preferred_element_type=f32)
        out = _rms(y, g_ref[...])
        stage = stages[k // 2]
        for q in range(stage.shape[0]):
            stage[q, pl.ds(k % 2, tj, stride=2), :] = out[:, q * LANES:(q + 1) * LANES]
        if k % 2 == 1:
            r0 = (k // 2) * 2 * tj
            for q in range(stage.shape[0]):
                o_ref[0, r0:r0 + 2 * tj, q * LANES:(q + 1) * LANES] = stage[q]


def _combine(tinfo, cnt, h2, vec, final_g, ys, B, L):
    T, D = h2.shape
    ntiles = tinfo.shape[0]
    tj = T // ntiles
    nt = TILES_PER_STEP
    assert GATHER_AHEAD < nt and nt % 2 == 0 and (L // tj) % nt == 0
    nsteps = ntiles // nt
    per_seq = L // (nt * tj)
    g = final_g.reshape(1, D)
    gbuf = pltpu.VMEM(_row_tiles_shape(2 * tj, D), jnp.uint32)
    return pl.pallas_call(
        _combine_kernel,
        grid=(nsteps,),
        in_specs=[pl.BlockSpec((nt, SUBLANES, LANES), lambda i: (i, 0, 0), memory_space=pltpu.SMEM),
                  pl.BlockSpec((nt, SUBLANES, LANES), lambda i: (i + 1, 0, 0), memory_space=pltpu.SMEM),
                  pl.BlockSpec(memory_space=pltpu.SMEM),
                  pl.BlockSpec((nt * tj, D), lambda i: (i, 0)),
                  pl.BlockSpec((nt * tj, LANES), lambda i: (i, 0)),
                  pl.BlockSpec((1, D), lambda i: (0, 0)),
                  pl.BlockSpec(memory_space=pl.ANY)],
        out_specs=pl.BlockSpec((1, nt * tj, D), lambda i: (i // per_seq, i % per_seq, 0)),
        out_shape=jax.ShapeDtypeStruct((B, L, D), f32),
        scratch_shapes=[[gbuf] * nt,
                        [pltpu.VMEM((D // LANES, 2 * tj, LANES), f32)] * (nt // 2),
                        pltpu.SMEM((N_EXPERTS,), jnp.int32), pltpu.SemaphoreType.DMA((nt,))],
        compiler_params=_cparams(("arbitrary",)),
        name="moe_combine",
    )(tinfo, jnp.pad(tinfo, ((0, nt), (0, 0), (0, 0))), cnt, h2, vec, g, ys)


def kernel(x, norm1_g, w_in, conv_w, conv_b, f_w_in, f_b_in, f_w_mid, f_b_mid, f_freq, f_w_out, f_bias,
           mix_g, w_out, norm2_g, w_group, b_group, w_router, b_router, w_gate, w_up, w_down, final_g):
    B, L, D = x.shape
    depth = norm1_g.shape[0]
    assert depth == 1, "the final RMSNorm is fused into the single layer's MoE combine"
    C = D // 2
    Lh = L // 2
    N = B * C
    T = B * L
    n_rows = T * 2 + N_EXPERTS * MOE_ROWS
    cfwd, cinv, cfou = _dft_consts(L)
    i = 0
    wcat = _prep_win(w_in[i], C)
    hcat, asum = _hyena_filters(L, C, f_w_in[i], f_b_in[i], f_w_mid[i], f_b_mid[i], f_freq[i], f_w_out[i])
    kspec, k0 = _filter_spectrum(cfwd, hcat, asum, L, C)
    z, x0, u4 = _inproj(x, norm1_g[i], wcat, conv_w[i], conv_b[i], C)
    z = z.reshape(2, Lh, N)
    x0 = x0.reshape(2, Lh, N)
    u4 = u4.reshape(4, Lh, N)
    s = _conv_fwd(cfwd, z, kspec, k0, C)
    yh = _conv_inv(cinv, s, x0, z, f_bias[i], C)
    yf = _fourier_dft(cfou, u4, L, C)
    wr = jnp.concatenate([jnp.transpose(w_router[i], (1, 0, 2)).reshape(D, N_EXPERTS), w_group[i]], axis=1)
    wr = jnp.pad(wr, ((0, 0), (0, LANES - wr.shape[1])))
    br = jnp.pad(jnp.concatenate([b_router[i].reshape(-1), b_group[i]]), (0, LANES - N_EXPERTS - N_GROUPS))
    h2, tok, pos, vec, tinfo, cnt = _outproj_router(yh, yf, x, mix_g[i], w_out[i].astype(bf16), norm2_g[i],
                                                     wr, br.reshape(1, LANES), C)
    xs, first_block, n_blocks = _dispatch(tinfo, cnt, tok, pos, n_rows)
    ys = _experts(first_block, n_blocks, xs, w_gate[i], w_up[i], w_down[i])
    return _combine(tinfo, cnt, h2, vec, final_g, ys, B, L)
```

```python
import functools
import math

import jax
import jax.numpy as jnp
from jax import lax
from jax.experimental import pallas as pl
from jax.experimental.pallas import tpu as pltpu

HEAD_DIM = 64
N_GROUPS = 4
EXPERTS_PER_GROUP = 8
N_EXPERTS = N_GROUPS * EXPERTS_PER_GROUP
FILTER_BANDS = 16
DECAY_FAST_PCT = 0.3
DECAY_SLOW_PCT = 1.5
DECAY_TARGET = 1e-2
EPS = 1e-6

LANES = 128
SUBLANES = 8
MXU_DIM = 256
DFT_ROWS = 512
FOURIER_COLS = 256
MOE_ROWS = 512
ROW_TILES_PER_STEP = 2
TILES_PER_STEP = 4
GATHER_AHEAD = 2
SEG_SPLIT = 32
VMEM_LIMIT = 56 * 1024 * 1024

f32 = jnp.float32
bf16 = jnp.bfloat16
HI = lax.Precision.HIGHEST


def _cparams(sem):
    return pltpu.CompilerParams(dimension_semantics=sem, vmem_limit_bytes=VMEM_LIMIT)


def _tile(pref, n):
    return min(pref, n)


_SIN_TERMS = tuple((-1.0) ** k / math.factorial(2 * k + 1) for k in range(7))
_COS_TERMS = tuple((-1.0) ** k / math.factorial(2 * k) for k in range(8))


def _cos_sin_of_turn_fraction(prod, m):
    num = prod & (m - 1)
    quadrant = num // (m // 4)
    x = (num & (m // 4 - 1)).astype(f32) * (2.0 * math.pi / m)
    x2 = x * x
    s = jnp.full_like(x, _SIN_TERMS[-1])
    for c in _SIN_TERMS[-2::-1]:
        s = s * x2 + c
    s = s * x
    c_ = jnp.full_like(x, _COS_TERMS[-1])
    for c in _COS_TERMS[-2::-1]:
        c_ = c_ * x2 + c
    odd = (quadrant & 1) == 1
    cos = jnp.where(odd, s, c_)
    sin = jnp.where(odd, c_, s)
    cos = jnp.where((quadrant == 1) | (quadrant == 2), -cos, cos)
    sin = jnp.where(quadrant >= 2, -sin, sin)
    return cos, sin


def _cos_sin_blocks(a, s, b0, m, ncb, d):
    tc, ts = _cos_sin_of_turn_fraction(a * (s * d), m)
    b0_lane = jnp.zeros(d.shape, jnp.int32)
    for cb in range(ncb):
        b0_lane = jnp.where(d == cb, b0(cb), b0_lane)
    ac, asn = _cos_sin_of_turn_fraction(a * b0_lane, m)
    for cb in range(ncb):
        ca = ac[:, cb:cb + 1]
        sa = asn[:, cb:cb + 1]
        yield cb, ca * tc - sa * ts, sa * tc + ca * ts


def _dft_consts_kernel(fwd_ref, inv_ref, *, L):
    Lh = L // 2
    tr = fwd_ref.shape[1]
    r = lax.broadcasted_iota(jnp.int32, (tr, LANES), 0) + pl.program_id(0) * tr
    d = lax.broadcasted_iota(jnp.int32, (tr, LANES), 1)
    alt_r = (1 - 2 * (r & 1)).astype(f32)
    alt_d = (1 - 2 * (d & 1)).astype(f32)
    ncb = Lh // LANES
    fams = [
        (r, 1, lambda cb: cb * LANES, L),
        (r, 2, lambda cb: 2 * cb * LANES + 1, 2 * L),
        (2 * r + 1, 1, lambda cb: cb * LANES, 2 * L),
    ]
    for fi, (a, s, b0, m) in enumerate(fams):
        for cb, c, sn in _cos_sin_blocks(a, s, b0, m, ncb, d):
            sl = slice(cb * LANES, (cb + 1) * LANES)
            if fi == 0:
                fwd_ref[0, :, sl] = c.astype(bf16)
                inv_ref[0, :, sl] = c.astype(bf16)
                fwd_ref[1, :, sl] = jnp.where(r == 0, alt_d, sn).astype(bf16)
                nst = -sn
                if cb == 0:
                    nst = jnp.where(d == 0, -alt_r, nst)
                inv_ref[1, :, sl] = nst.astype(bf16)
            elif fi == 1:
                fwd_ref[2, :, sl] = c.astype(bf16)
                fwd_ref[3, :, sl] = jnp.where(r == 0, alt_d, sn).astype(bf16)
            else:
                inv_ref[2, :, sl] = c.astype(bf16)
                nst = -sn
                if cb == 0:
                    nst = jnp.where(d == 0, -alt_r, nst)
                inv_ref[3, :, sl] = nst.astype(bf16)


def _dft_consts(L):
    Lh = L // 2
    tr = _tile(256, Lh)
    shp = jax.ShapeDtypeStruct((4, Lh, Lh), bf16)
    spec = pl.BlockSpec((4, tr, Lh), lambda i: (0, i, 0))
    return pl.pallas_call(
        functools.partial(_dft_consts_kernel, L=L),
        grid=(Lh // tr,),
        out_specs=[spec, spec],
        out_shape=[shp, shp],
        compiler_params=_cparams(("parallel",)),
        name="dft_consts",
    )()


def _fourier_consts_kernel(o_ref, *, L):
    n = L // 4
    tr = o_ref.shape[0]
    r = lax.broadcasted_iota(jnp.int32, (tr, LANES), 0) + pl.program_id(0) * tr
    d = lax.broadcasted_iota(jnp.int32, (tr, LANES), 1)

    def col_l(cb):
        parity, j0 = divmod(cb * LANES, n // 2)
        return 2 * j0 + parity

    for cb, c, sn in _cos_sin_blocks(r, 2, col_l, n, n // LANES, d):
        o_ref[:, cb * LANES:(cb + 1) * LANES] = c.astype(bf16)
        o_ref[:, n + cb * LANES:n + (cb + 1) * LANES] = (-sn).astype(bf16)


def _fourier_consts(L):
    n = L // 4
    tr = _tile(256, n)
    return pl.pallas_call(
        functools.partial(_fourier_consts_kernel, L=L),
        grid=(n // tr,),
        out_specs=pl.BlockSpec((tr, 2 * n), lambda i: (i, 0)),
        out_shape=jax.ShapeDtypeStruct((n, 2 * n), bf16),
        compiler_params=_cparams(("parallel",)),
        name="fourier_consts",
    )()


def _prep_win_kernel(w_ref, o_ref, *, C):
    w = w_ref[...]
    i = lax.broadcasted_iota(jnp.int32, (LANES, LANES), 0)
    j = lax.broadcasted_iota(jnp.int32, (LANES, LANES), 1)
    same = (i // HEAD_DIM) == (j // HEAD_DIM)
    cos, sin = _cos_sin_of_turn_fraction((i % HEAD_DIM) * (j % HEAD_DIM), HEAD_DIM)
    bc = jnp.where(same, cos, 0.0)
    bs = jnp.where(same, sin, 0.0)
    o_ref[:, :3 * C] = w[:, :3 * C].astype(bf16)
    for k in range(0, C, LANES):
        wf = w[:, 3 * C + k:3 * C + k + LANES]
        o_ref[:, 3 * C + k:3 * C + k + LANES] = jnp.dot(wf, bc, precision=HI, preferred_element_type=f32).astype(bf16)
        o_ref[:, 4 * C + k:4 * C + k + LANES] = jnp.dot(wf, bs, precision=HI, preferred_element_type=f32).astype(bf16)


def _prep_win(w_in, C):
    D = w_in.shape[0]
    tr = _tile(256, D)
    return pl.pallas_call(
        functools.partial(_prep_win_kernel, C=C),
        grid=(D // tr,),
        in_specs=[pl.BlockSpec((tr, 4 * C), lambda i: (i, 0))],
        out_specs=pl.BlockSpec((tr, 5 * C), lambda i: (i, 0)),
        out_shape=jax.ShapeDtypeStruct((D, 5 * C), bf16),
        compiler_params=_cparams(("parallel",)),
        name="prep_win",
    )(w_in)


def _filter_kernel(win_ref, bin_ref, wmid_ref, bmid_ref, freq_ref, wout_ref, h_ref, asum_ref, *, L, C):
    tr = h_ref.shape[1]
    step = pl.program_id(0)
    lane = lax.broadcasted_iota(jnp.int32, (tr, LANES), 1)
    row = lax.broadcasted_iota(jnp.int32, (tr, LANES), 0) + step * tr
    band_i = jnp.where(lane <= FILTER_BANDS, lane - 1, lane - 1 - FILTER_BANDS)
    band = 1e-4 + band_i.astype(f32) * ((FILTER_BANDS - 1 - 1e-4) / (FILTER_BANDS - 1))
    ch = lax.broadcasted_iota(jnp.int32, (tr, 2 * C), 1) % C
    max_decay = math.log(DECAY_TARGET) / DECAY_FAST_PCT
    min_decay = math.log(DECAY_TARGET) / DECAY_SLOW_PCT
    delta = jnp.abs(min_decay + ch.astype(f32) * ((max_decay - min_decay) / (C - 1)))
    is_bwd = lax.broadcasted_iota(jnp.int32, (tr, 2 * C), 1) >= C
    row_c = lax.broadcasted_iota(jnp.int32, (tr, 2 * C), 0) + step * tr

    @pl.when(step == 0)
    def _():
        asum_ref[...] = jnp.zeros_like(asum_ref)

    total = jnp.zeros((1, 2 * C), f32)
    for p in range(2):
        pos = (2 * row + p).astype(f32)
        t = pos * (1.0 / (L - 1))
        w = pos * (2.0 * math.pi / L)
        fw = band * w
        z = jnp.where(lane == 0, t,
                      jnp.where(lane <= FILTER_BANDS, jnp.cos(fw),
                                jnp.where(lane <= 2 * FILTER_BANDS, -jnp.sin(fw), 0.0)))
        h = jnp.sin(freq_ref[0:1, :] * (jnp.dot(z, win_ref[...], precision=HI, preferred_element_type=f32)
                                       + bin_ref[...]))
        for i in range(wmid_ref.shape[0]):
            h = jnp.sin(freq_ref[i + 1:i + 2, :]
                        * (jnp.dot(h, wmid_ref[i], precision=HI, preferred_element_type=f32)
                           + bmid_ref[i:i + 1, :]))
        out = jnp.dot(h, wout_ref[...], precision=HI, preferred_element_type=f32)
        tc = (2 * row_c + p).astype(f32) * (1.0 / (L - 1))
        out = out * jnp.exp(-tc * delta)
        if p == 0:
            out = jnp.where(is_bwd & (row_c == 0), 0.0, out)
        h_ref[p] = out.astype(bf16)
        total = total + jnp.sum(jnp.abs(out), axis=0, keepdims=True)
    asum_ref[0:1, :] += total


def _hyena_filters(L, C, f_w_in, f_b_in, f_w_mid, f_b_mid, f_freq, f_w_out):
    Lh = L // 2
    tr = _tile(256, Lh)
    order = f_w_in.shape[1]
    win = jnp.pad(f_w_in, ((0, LANES - f_w_in.shape[0]), (0, 0)))
    full = lambda a: pl.BlockSpec(a.shape, lambda i: (0,) * a.ndim)
    args = (win, f_b_in.reshape(1, order), f_w_mid, f_b_mid, f_freq, f_w_out)
    return pl.pallas_call(
        functools.partial(_filter_kernel, L=L, C=C),
        grid=(Lh // tr,),
        in_specs=[full(a) for a in args],
        out_specs=[pl.BlockSpec((2, tr, 2 * C), lambda i: (0, i, 0)),
                   pl.BlockSpec((SUBLANES, 2 * C), lambda i: (0, 0))],
        out_shape=[jax.ShapeDtypeStruct((2, Lh, 2 * C), bf16),
                   jax.ShapeDtypeStruct((SUBLANES, 2 * C), f32)],
        compiler_params=_cparams(("arbitrary",)),
        name="hyena_filters",
    )(*args)


def _fwd_products(c_ref, x_ref):
    xe, xo = x_ref[0], x_ref[1]
    ce = jnp.dot(c_ref[0], xe, preferred_element_type=f32)
    se = jnp.dot(c_ref[1], xe, preferred_element_type=f32)
    co = jnp.dot(c_ref[2], xo, preferred_element_type=f32)
    so = jnp.dot(c_ref[3], xo, preferred_element_type=f32)
    return ce, se, co, so


def _filter_spectrum_kernel(c_ref, h_ref, asum_ref, k_ref, k0_ref, *, L, C):
    ce, se, co, so = _fwd_products(c_ref, h_ref)
    a = asum_ref[0:1, :]
    scale = 1.0 / ((a[:, :C] + a[:, C:]) * L)
    F = lambda v: v[:, :C]
    G = lambda v: v[:, C:]
    k_ref[0] = (F(ce) + F(co) + G(ce) + G(co)) * scale
    k_ref[1] = (-(F(se) + F(so)) + (G(se) + G(so))) * scale
    k_ref[2] = (F(ce) - F(co) + G(ce) - G(co)) * scale
    k_ref[3] = ((F(se) - F(so)) - (G(se) - G(so))) * scale

    @pl.when(pl.program_id(0) == 0)
    def _():
        k0_ref[...] = jnp.zeros_like(k0_ref)
        k0_ref[0:1, :] = ((F(se) + G(se)) * scale)[0:1, :]
        k0_ref[1:2, :] = ((F(so) - G(so)) * scale)[0:1, :]


def _filter_spectrum(cfwd, hcat, asum, L, C):
    Lh = L // 2
    tm = _tile(256, Lh)
    return pl.pallas_call(
        functools.partial(_filter_spectrum_kernel, L=L, C=C),
        grid=(Lh // tm,),
        in_specs=[pl.BlockSpec((4, tm, Lh), lambda i: (0, i, 0)),
                  pl.BlockSpec((2, Lh, 2 * C), lambda i: (0, 0, 0)),
                  pl.BlockSpec((SUBLANES, 2 * C), lambda i: (0, 0))],
        out_specs=[pl.BlockSpec((4, tm, C), lambda i: (0, i, 0)),
                   pl.BlockSpec((SUBLANES, C), lambda i: (0, 0))],
        out_shape=[jax.ShapeDtypeStruct((4, Lh, C), f32),
                   jax.ShapeDtypeStruct((SUBLANES, C), f32)],
        compiler_params=_cparams(("arbitrary",)),
        name="filter_spectrum",
    )(cfwd, hcat, asum)


def _conv_fwd_kernel(c_ref, z_ref, k_ref, k0_ref, s_ref):
    ce, se, co, so = _fwd_products(c_ref, z_ref)
    zsr, zsi, zdr, zdi = ce + co, -(se + so), ce - co, se - so
    ksr, ksi, kdr, kdi = k_ref[0], k_ref[1], k_ref[2], k_ref[3]
    psr = zsr * ksr - zsi * ksi
    psi = zsr * ksi + zsi * ksr
    pdr = zdr * kdr - zdi * kdi
    pdi = zdr * kdi + zdi * kdr
    gen = (psr + pdr, psi - pdi, psr - pdr, psi + pdi)
    first = pl.program_id(1) == 0

    @pl.when(first)
    def _():
        row0 = lax.broadcasted_iota(jnp.int32, ce.shape, 0) == 0
        p0 = zsr * ksr
        pl_ = zdr * kdr
        ak, bk = k0_ref[0:1, :], k0_ref[1:2, :]
        phr = se * ak - so * bk
        phi = -(se * bk + so * ak)
        spec = (0.5 * (p0 + pl_), -phr, 0.5 * (p0 - pl_), phi)
        for i in range(4):
            s_ref[i] = jnp.where(row0, spec[i], gen[i]).astype(bf16)

    @pl.when(jnp.logical_not(first))
    def _():
        for i in range(4):
            s_ref[i] = gen[i].astype(bf16)


def _conv_fwd(cfwd, z, kspec, k0, C):
    _, Lh, N = z.shape
    tm = _tile(DFT_ROWS, Lh)
    return pl.pallas_call(
        _conv_fwd_kernel,
        grid=(N // C, Lh // tm),
        in_specs=[pl.BlockSpec((4, tm, Lh), lambda n, i: (0, i, 0)),
                  pl.BlockSpec((2, Lh, C), lambda n, i: (0, 0, n)),
                  pl.BlockSpec((4, tm, C), lambda n, i: (0, i, 0)),
                  pl.BlockSpec((SUBLANES, C), lambda n, i: (0, 0))],
        out_specs=pl.BlockSpec((4, tm, C), lambda n, i: (0, i, n)),
        out_shape=jax.ShapeDtypeStruct((4, Lh, N), bf16),
        compiler_params=_cparams(("parallel", "arbitrary")),
        name="conv_fwd",
    )(cfwd, z, kspec, k0)


def _pair_products(c_ref, r_ref):
    ev = (jnp.dot(c_ref[0], r_ref[0], preferred_element_type=f32)
          + jnp.dot(c_ref[1], r_ref[1], preferred_element_type=f32))
    od = (jnp.dot(c_ref[2], r_ref[2], preferred_element_type=f32)
          + jnp.dot(c_ref[3], r_ref[3], preferred_element_type=f32))
    return ev, od


def _conv_inv_kernel(c_ref, s_ref, x0_ref, z_ref, fb_ref, y_ref):
    conv = _pair_products(c_ref, s_ref)
    fb = fb_ref[...]
    for p in range(2):
        zp = z_ref[p].astype(f32)
        y_ref[p] = (x0_ref[p].astype(f32) * (conv[p] + zp * fb)).astype(bf16)


def _conv_inv(cinv, s, x0, z, f_bias, C):
    _, Lh, N = s.shape
    tm = _tile(DFT_ROWS, Lh)
    return pl.pallas_call(
        _conv_inv_kernel,
        grid=(N // C, Lh // tm),
        in_specs=[pl.BlockSpec((4, tm, Lh), lambda n, i: (0, i, 0)),
                  pl.BlockSpec((4, Lh, C), lambda n, i: (0, 0, n)),
                  pl.BlockSpec((2, tm, C), lambda n, i: (0, i, n)),
                  pl.BlockSpec((2, tm, C), lambda n, i: (0, i, n)),
                  pl.BlockSpec((1, C), lambda n, i: (0, 0))],
        out_specs=pl.BlockSpec((2, tm, C), lambda n, i: (0, i, n)),
        out_shape=jax.ShapeDtypeStruct((2, Lh, N), bf16),
        compiler_params=_cparams(("parallel", "arbitrary")),
        name="conv_inv",
    )(cinv, s, x0, z, f_bias.reshape(1, C))


def _fourier_kernel(c_ref, u_ref, y_ref, rhs, stage, tw1, tw2, *, L, scale):
    Lh, n, H = L // 2, L // 4, L // 8
    tn = u_ref.shape[2]

    @pl.when(pl.program_id(0) == 0)
    def _():
        r1 = lax.broadcasted_iota(jnp.int32, (Lh, LANES), 0)
        tw1[0], tw1[1] = _cos_sin_of_turn_fraction(2 * (r1 % n) + r1 // n, L)
        r2 = lax.broadcasted_iota(jnp.int32, (n, LANES), 0)
        tw2[0], tw2[1] = _cos_sin_of_turn_fraction(2 * (r2 % H) + r2 // H, Lh)

    def rotate(a, b, c, s):
        return c * a - s * b, s * a + c * b

    for q in range(tn // LANES):
        cols = slice(q * LANES, (q + 1) * LANES)
        for p in range(2):
            first = slice(p * n, p * n + H)
            second = slice(p * n + H, (p + 1) * n)
            out = slice(p * H, (p + 1) * H)
            out_b = slice(n + p * H, n + (p + 1) * H)
            c2, s2 = tw2[0, out, :], tw2[1, out, :]

            def emit(sub, a0, b0, a1, b1):
                rhs[sub, out, cols] = (a0 + a1).astype(bf16)
                rhs[sub, out_b, cols] = (b0 + b1).astype(bf16)
                da, db = rotate(a0 - a1, b0 - b1, c2, s2)
                rhs[sub + 1, out, cols] = da.astype(bf16)
                rhs[sub + 1, out_b, cols] = db.astype(bf16)

            ld = lambda k, rows: u_ref[k, rows, cols].astype(f32)
            emit(0, ld(0, first), ld(1, first), ld(0, second), ld(1, second))
            odd = [rotate(ld(2, rows), ld(3, rows), tw1[0, rows, :], tw1[1, rows, :]) for rows in (first, second)]
            emit(2, odd[0][0], odd[0][1], odd[1][0], odd[1][1])

    for parity in range(2):
        pair = [jnp.dot(c_ref[...], rhs[2 * parity + h], preferred_element_type=f32) * scale for h in range(2)]
        for q in range(tn // LANES):
            cols = slice(q * LANES, (q + 1) * LANES)
            for h in range(2):
                stage[q, pl.ds(h, n, stride=2), :] = pair[h][:, cols]
            y_ref[parity, :, cols] = stage[q].astype(bf16)


def _fourier_dft(cfou, u4, L, C):
    _, Lh, N = u4.shape
    n = L // 4
    tn = _tile(FOURIER_COLS, C)
    return pl.pallas_call(
        functools.partial(_fourier_kernel, L=L, scale=1.0 / math.sqrt(L * HEAD_DIM)),
        grid=(N // tn,),
        in_specs=[pl.BlockSpec((n, 2 * n), lambda j: (0, 0)),
                  pl.BlockSpec((4, Lh, tn), lambda j: (0, 0, j))],
        out_specs=pl.BlockSpec((2, Lh, tn), lambda j: (0, 0, j)),
        out_shape=jax.ShapeDtypeStruct((2, Lh, N), bf16),
        scratch_shapes=[pltpu.VMEM((4, 2 * n, tn), bf16), pltpu.VMEM((tn // LANES, Lh, LANES), f32),
                        pltpu.VMEM((2, Lh, LANES), f32), pltpu.VMEM((2, n, LANES), f32)],
        compiler_params=_cparams(("arbitrary",)),
        name="fourier_dft",
    )(cfou, u4)


def _rms(x, g):
    return x * lax.rsqrt(jnp.mean(x * x, axis=-1, keepdims=True) + EPS) * g


def _stage_rows(chunks, x):
    for k in range(chunks.shape[0]):
        chunks[k] = x[:, k * LANES:(k + 1) * LANES]


def _parity_rows(chunks, p, n):
    return jnp.concatenate([chunks[k, pl.ds(p, n, stride=2), :] for k in range(chunks.shape[0])], axis=1)


def _inproj_kernel(xa_ref, xb_ref, pa_ref, na_ref, pb_ref, nb_ref, g_ref, w_ref, cw_ref, cb_ref,
                   z_ref, x0_ref, u_ref, stage, *, C, D):
    tj = xa_ref.shape[1] // 2
    jt = pl.program_id(1)
    n_half = pl.num_programs(1)
    g = g_ref[...]
    w = w_ref[...]
    cw0, cw1, cw2 = cw_ref[0:1, :], cw_ref[1:2, :], cw_ref[2:3, :]
    cb = cb_ref[...]
    row = lax.broadcasted_iota(jnp.int32, (tj, 3 * C), 0)
    four = []
    halves = ((xa_ref, pa_ref, na_ref, jt == 0, False),
              (xb_ref, pb_ref, nb_ref, False, jt == n_half - 1))
    for h, (x_ref, p_ref, n_ref, at_start, at_end) in enumerate(halves):
        _stage_rows(stage, x_ref[0])
        lhs = jnp.concatenate([
            _rms(_parity_rows(stage, 0, tj), g).astype(bf16),
            _rms(_parity_rows(stage, 1, tj), g).astype(bf16),
            _rms(p_ref[0], g).astype(bf16),
            _rms(n_ref[0], g).astype(bf16)], axis=0)
        res = jnp.dot(lhs, w, preferred_element_type=f32)
        pe, po = res[:tj, :3 * C], res[tj:2 * tj, :3 * C]
        prev = res[2 * tj + SUBLANES - 1:2 * tj + SUBLANES, :3 * C]
        nxt = res[2 * tj + SUBLANES:2 * tj + SUBLANES + 1, :3 * C]
        if at_start is not False:
            prev = jnp.where(at_start, 0.0, prev)
        if at_end is not False:
            nxt = jnp.where(at_end, 0.0, nxt)
        po_dn = jnp.where(row == 0, prev, pltpu.roll(po, 1, 0))
        pe_up = jnp.where(row == tj - 1, nxt, pltpu.roll(pe, tj - 1, 0))
        uc = (cb + cw0 * po_dn + cw1 * pe + cw2 * po,
              cb + cw0 * pe + cw1 * po + cw2 * pe_up)
        for p in range(2):
            x0_ref[p, h] = uc[p][:, :C].astype(bf16)
            z_ref[p, h] = (uc[p][:, C:2 * C] * uc[p][:, 2 * C:]).astype(bf16)
        four.append((res[:tj, 3 * C:], res[tj:2 * tj, 3 * C:]))
    for p in range(2):
        fa, fb = four[0][p], four[1][p]
        u_ref[0, p] = (fa[:, :C] + fb[:, :C]).astype(bf16)
        u_ref[1, p] = (fa[:, C:] + fb[:, C:]).astype(bf16)
        u_ref[2, p] = (fa[:, :C] - fb[:, :C]).astype(bf16)
        u_ref[3, p] = (fa[:, C:] - fb[:, C:]).astype(bf16)


def _inproj(x, norm_g, wcat, conv_w, conv_b, C):
    B, L, D = x.shape
    Lq = L // 4
    tj = _tile(256, Lq)
    nq = Lq // tj
    N = B * C
    hb = 2 * tj // SUBLANES
    last = L // SUBLANES - 1
    main = lambda off: pl.BlockSpec((1, 2 * tj, D), lambda b, j: (b, j + off, 0))
    prev = lambda off: pl.BlockSpec((1, SUBLANES, D), lambda b, j: (b, jnp.maximum((j + off) * hb - 1, 0), 0))
    nxt = lambda off: pl.BlockSpec((1, SUBLANES, D), lambda b, j: (b, jnp.minimum((j + off + 1) * hb, last), 0))
    full = lambda a: pl.BlockSpec(a.shape, lambda b, j: (0,) * a.ndim)
    g = norm_g.reshape(1, D)
    cb = conv_b.reshape(1, 3 * C)
    return pl.pallas_call(
        functools.partial(_inproj_kernel, C=C, D=D),
        grid=(B, nq),
        in_specs=[main(0), main(nq), prev(0), nxt(0), prev(nq), nxt(nq),
                  full(g), full(wcat), full(conv_w), full(cb)],
        out_specs=[pl.BlockSpec((2, 2, tj, C), lambda b, j: (0, 0, j, b)),
                   pl.BlockSpec((2, 2, tj, C), lambda b, j: (0, 0, j, b)),
                   pl.BlockSpec((4, 2, tj, C), lambda b, j: (0, 0, j, b))],
        out_shape=[jax.ShapeDtypeStruct((2, 2, Lq, N), bf16),
                   jax.ShapeDtypeStruct((2, 2, Lq, N), bf16),
                   jax.ShapeDtypeStruct((4, 2, Lq, N), bf16)],
        scratch_shapes=[pltpu.VMEM((D // LANES, 2 * tj, LANES), f32)],
        compiler_params=_cparams(("parallel", "arbitrary")),
        name="inproj",
    )(x, x, x, x, x, x, g, wcat, conv_w, cb)


def _pack_bf16_pairs(v, is_bf16_exact=False):
    k = v.shape[1] // 2
    rounded = v if is_bf16_exact else v.astype(bf16).astype(f32)
    hi = pltpu.bitcast(rounded[:, :k], jnp.uint32)
    lo = pltpu.bitcast(rounded[:, k:], jnp.uint32)
    return hi | (lo >> 16)


def _unpack_bf16_pairs(u):
    hi = pltpu.bitcast(u & jnp.uint32(0xFFFF0000), f32)
    lo = pltpu.bitcast(u << 16, f32)
    return jnp.concatenate([hi, lo], axis=1)


def _row_tiles_shape(n, D):
    return (n, D // (2 * LANES), LANES)


def _store_row_tiles(ref, packed, row0=0):
    total, per_row, _ = ref.shape
    n = packed.shape[0]
    flat = ref.reshape(total * per_row, LANES)
    for s in range(per_row):
        flat[pl.ds(row0 * per_row + s, n, stride=per_row), :] = packed[:, s * LANES:(s + 1) * LANES]


def _load_row_tiles(ref, row0=0, n=None):
    total, per_row, _ = ref.shape
    n = total if n is None else n
    flat = ref.reshape(total * per_row, LANES)
    return jnp.concatenate([flat[pl.ds(row0 * per_row + s, n, stride=per_row), :] for s in range(per_row)],
                           axis=1)


def _outproj_router_kernel(yh_ref, yf_ref, x_ref, mg_ref, wo_ref, g2_ref, wr_ref, br_ref,
                           h2_ref, tok_ref, pos_ref, vec_ref, tinfo_ref, cnt_ref, carry_ref, stage, *, C):
    n_sub = pos_ref.shape[0]
    tj = pos_ref.shape[2]
    _stage_rows(stage, x_ref[0])

    @pl.when((pl.program_id(0) == 0) & (pl.program_id(1) == 0))
    def _():
        carry_ref[...] = jnp.zeros_like(carry_ref)

    li = lax.broadcasted_iota(jnp.int32, (LANES, LANES), 0)
    lj = lax.broadcasted_iota(jnp.int32, (LANES, LANES), 1)
    hw = _tile(MXU_DIM, C)
    ai = lax.broadcasted_iota(jnp.int32, (hw, hw), 0)
    aj = lax.broadcasted_iota(jnp.int32, (hw, hw), 1)
    avg = jnp.where(ai // HEAD_DIM == aj // HEAD_DIM, 1.0 / HEAD_DIM, 0.0).astype(bf16)
    upper = (li < lj).astype(bf16)
    ri = lax.broadcasted_iota(jnp.int32, (tj, tj), 0)
    ci = lax.broadcasted_iota(jnp.int32, (tj, tj), 1)
    tri = (ci < ri).astype(bf16)
    wr = wr_ref[...]
    wr_hi = wr.astype(bf16)
    wr_lo = (wr - wr_hi.astype(f32)).astype(bf16)
    wr_both = jnp.concatenate([wr_hi, wr_lo], axis=1)
    lane = lax.broadcasted_iota(jnp.int32, (n_sub * tj, LANES), 1).astype(f32)
    sub = lax.broadcasted_iota(jnp.int32, (SUBLANES, LANES), 0)
    neg = jnp.float32(-jnp.inf)
    big = jnp.float32(LANES)

    def first_argmax(vals, mask):
        v = jnp.where(mask, vals, neg)
        m = jnp.max(v, axis=1, keepdims=True)
        idx = jnp.min(jnp.where(mask & (v == m), lane, big), axis=1, keepdims=True)
        return m, idx

    tile_logits = []
    for s in range(n_sub):
        r, p = divmod(s, 2)
        rows = slice(s * tj, (s + 1) * tj)
        parts = []
        for ref in (yh_ref, yf_ref):
            y = ref[p, r * tj:(r + 1) * tj, :].astype(f32)
            ysq = (y * y).astype(bf16)
            ms = jnp.concatenate([jnp.dot(ysq[:, k:k + hw], avg, preferred_element_type=f32)
                                  for k in range(0, C, hw)], axis=1)
            parts.append(y * lax.rsqrt(ms + EPS))
        yn = (jnp.concatenate(parts, axis=1) * mg_ref[...]).astype(bf16)
        h2 = _parity_rows(stage, r * 2 * tj + p, tj) + jnp.dot(yn, wo_ref[...], preferred_element_type=f32)
        h2_ref[rows, :] = h2
        tok = _rms(h2, g2_ref[...])
        tok_ref[rows, :] = tok.astype(bf16)

        t_hi = tok.astype(bf16)
        t_lo = (tok - t_hi.astype(f32)).astype(bf16)
        hi_terms = jnp.dot(t_hi, wr_both, preferred_element_type=f32)
        tile_logits.append(hi_terms[:, :LANES] + hi_terms[:, LANES:]
                           + jnp.dot(t_lo, wr_hi, preferred_element_type=f32))
    logits = jnp.concatenate(tile_logits, axis=0) + br_ref[...]

    gmask = (lane >= N_EXPERTS) & (lane < N_EXPERTS + N_GROUPS)
    gmax, gidx = first_argmax(logits, gmask)
    gsel = gidx - N_EXPERTS
    p_group = 1.0 / jnp.sum(jnp.where(gmask, jnp.exp(logits - gmax), 0.0), axis=1, keepdims=True)
    emask = (lane >= gsel * EXPERTS_PER_GROUP) & (lane < (gsel + 1) * EXPERTS_PER_GROUP)
    l0, e0 = first_argmax(logits, emask)
    l1, e1 = first_argmax(logits, emask & (lane != e0))
    ex = jnp.exp(l1 - l0)
    w0 = p_group / (1.0 + ex)
    w1 = p_group * ex / (1.0 + ex)

    oh0 = (lane == e0).astype(f32)
    oh1 = (lane == e1).astype(f32)
    both = (oh0 + oh1).astype(bf16)
    where, counts = [], []
    for s in range(n_sub):
        tile_both = both[s * tj:(s + 1) * tj, :]
        ranks = jnp.dot(tri, tile_both, preferred_element_type=f32)
        n_e = ranks[tj - 1:tj, :] + tile_both[tj - 1:tj, :].astype(f32)
        n_e8 = jnp.broadcast_to(n_e, (SUBLANES, LANES))
        tile_off = jnp.dot(n_e8.astype(bf16), upper, preferred_element_type=f32)[0:1, :]
        where.append(ranks + tile_off)
        counts.append(n_e8)
    where = jnp.concatenate(where, axis=0)
    pos0 = jnp.sum(oh0 * where, axis=1, keepdims=True)
    pos1 = jnp.sum(oh1 * where, axis=1, keepdims=True)
    cols = jnp.where(lane == 0, w0, jnp.where(lane == 1, w1,
                     jnp.where(lane == 2, pos0, jnp.where(lane == 3, pos1, 0.0))))
    vec_ref[...] = cols
    base = carry_ref[...]
    for s in range(n_sub):
        pos_ref[s] = cols[s * tj:(s + 1) * tj, :].T[:SUBLANES, :].astype(jnp.int32)
        long_seg = (jnp.max(counts[s], axis=1, keepdims=True) >= SEG_SPLIT).astype(f32)
        tinfo_ref[s] = jnp.where(sub == 0, counts[s], jnp.where(sub == 1, base, jnp.where(sub == 2, long_seg, 0.0))
                                 ).astype(jnp.int32)
        base = base + counts[s]
    carry_ref[...] = base
    cnt_ref[...] = base.astype(jnp.int32)


def _outproj_router(yh, yf, x, mix_g, wo, norm2_g, wr, br, C):
    B, L, D = x.shape
    Lh = L // 2
    tj = _tile(256, Lh)
    rt = ROW_TILES_PER_STEP
    ns = 2 * rt
    nj = Lh // (rt * tj)
    T = B * L
    ntiles = T // tj
    lin = lambda b, j: b * nj + j
    full = lambda a: pl.BlockSpec(a.shape, lambda b, j: (0,) * a.ndim)
    mg = mix_g.reshape(1, D)
    g2 = norm2_g.reshape(1, D)
    return pl.pallas_call(
        functools.partial(_outproj_router_kernel, C=C),
        grid=(B, nj),
        in_specs=[pl.BlockSpec((2, rt * tj, C), lambda b, j: (0, j, b)),
                  pl.BlockSpec((2, rt * tj, C), lambda b, j: (0, j, b)),
                  pl.BlockSpec((1, ns * tj, D), lambda b, j: (b, j, 0)),
                  full(mg), full(wo), full(g2), full(wr), full(br)],
        out_specs=[pl.BlockSpec((ns * tj, D), lambda b, j: (lin(b, j), 0)),
                   pl.BlockSpec((ns * tj, D), lambda b, j: (lin(b, j), 0)),
                   pl.BlockSpec((ns, SUBLANES, tj), lambda b, j: (lin(b, j), 0, 0)),
                   pl.BlockSpec((ns * tj, LANES), lambda b, j: (lin(b, j), 0)),
                   pl.BlockSpec((ns, SUBLANES, LANES), lambda b, j: (lin(b, j), 0, 0)),
                   pl.BlockSpec((SUBLANES, LANES), lambda b, j: (0, 0))],
        out_shape=[jax.ShapeDtypeStruct((T, D), f32),
                   jax.ShapeDtypeStruct((T, D), bf16),
                   jax.ShapeDtypeStruct((ntiles, SUBLANES, tj), jnp.int32),
                   jax.ShapeDtypeStruct((T, LANES), f32),
                   jax.ShapeDtypeStruct((ntiles, SUBLANES, LANES), jnp.int32),
                   jax.ShapeDtypeStruct((SUBLANES, LANES), jnp.int32)],
        scratch_shapes=[pltpu.VMEM((SUBLANES, LANES), f32), pltpu.VMEM((D // LANES, ns * tj, LANES), f32)],
        compiler_params=_cparams(("arbitrary", "arbitrary")),
        name="outproj_router",
    )(yh, yf, x, mg, wo, g2, wr, br)


def _padded(c):
    return ((c + MOE_ROWS - 1) // MOE_ROWS) * MOE_ROWS


def _pad_starts(cnt_ref, start_ref):
    def body(e, acc):
        start_ref[e] = acc
        return acc + _padded(cnt_ref[0, e])
    return lax.fori_loop(0, N_EXPERTS, body, jnp.int32(0))


def _segment_copies(n, src, src_off, dst, dst_off, sem, max_rows, wait, src_advances=True, min_rows=1):
    bit = max_rows
    while bit >= min_rows:
        take = n & bit

        @pl.when(take != 0)
        def _(src_off=src_off, dst_off=dst_off, bit=bit):
            cp = pltpu.make_async_copy(src.at[pl.ds(src_off, bit)], dst.at[pl.ds(dst_off, bit)], sem)
            if wait:
                cp.wait()
            else:
                cp.start()
        if src_advances:
            src_off = src_off + take
        dst_off = dst_off + take
        bit //= 2


def _tile_segments(tinfo_ref, t, start_ref, local, local_is_src, remote, sem, tj):
    def copies(long_part):
        off = 0
        for e in range(N_EXPERTS):
            n = tinfo_ref[t, 0, e]
            far = start_ref[e] + tinfo_ref[t, 1, e]
            if long_part:
                args = (n, off, far, tj, SEG_SPLIT)
            else:
                head = n & ~(SEG_SPLIT - 1)
                args = (n, off + head, far + head, SEG_SPLIT // 2, 1)
            n_, near_, far_, hi, lo = args
            if local_is_src:
                _segment_copies(n_, local, near_, remote, far_, sem, hi, wait=False, min_rows=lo)
            else:
                _segment_copies(n_, remote, far_, local, near_, sem, hi, wait=False, min_rows=lo)
            off = off + n

    copies(long_part=False)

    @pl.when(tinfo_ref[t, 2, 0] != 0)
    def _():
        copies(long_part=True)


def _dispatch_kernel(tinfo_ref, cnt_ref, tok_ref, pos_ref, xs_ref, first_ref, nblk_ref,
                     sbufs, zbuf, start_ref, sems, zsem):
    tj = pos_ref.shape[2]
    n_blocks = xs_ref.shape[0] // MOE_ROWS

    @pl.when(pl.program_id(0) == 0)
    def _():
        total = _pad_starts(cnt_ref, start_ref)
        used = total // MOE_ROWS

        def block_ranges(e, c):
            first_ref[e] = start_ref[e] // MOE_ROWS
            nblk_ref[e] = _padded(cnt_ref[0, e]) // MOE_ROWS
            return c
        lax.fori_loop(0, N_EXPERTS, block_ranges, 0)

        zbuf[...] = jnp.zeros_like(zbuf)
        zrows = zbuf.shape[0]
        for wait in (False, True):
            def zero_pad(e, c, wait=wait):
                cnt = cnt_ref[0, e]
                _segment_copies(_padded(cnt) - cnt, zbuf, 0, xs_ref, start_ref[e] + cnt, zsem,
                                zrows, wait, src_advances=False)
                return c
            lax.fori_loop(0, N_EXPERTS, zero_pad, 0)

            def zero_tail(i, c, wait=wait):
                cp = pltpu.make_async_copy(zbuf, xs_ref.at[pl.ds(total + i * zrows, zrows)], zsem)
                if wait:
                    cp.wait()
                else:
                    cp.start()
                return c
            lax.fori_loop(0, (n_blocks - used) * (MOE_ROWS // zrows), zero_tail, 0)

    def wait_tile(k):
        pltpu.make_async_copy(sbufs[k], xs_ref.at[pl.ds(0, 2 * tj)], sems.at[k]).wait()

    slot = lax.broadcasted_iota(jnp.int32, (2 * tj, tj), 0)
    for k in range(TILES_PER_STEP):
        @pl.when(pl.program_id(0) > 0)
        def _(k=k):
            wait_tile(k)
        perm = ((slot == pos_ref[k, 2:3, :]) | (slot == pos_ref[k, 3:4, :])).astype(bf16)
        sorted_rows = jnp.dot(perm, tok_ref[k * tj:(k + 1) * tj, :], preferred_element_type=f32)
        _store_row_tiles(sbufs[k], _pack_bf16_pairs(sorted_rows, is_bf16_exact=True))
        if k > 0:
            _tile_segments(tinfo_ref, k - 1, start_ref, sbufs[k - 1], True, xs_ref, sems.at[k - 1], tj)
    last = TILES_PER_STEP - 1
    _tile_segments(tinfo_ref, last, start_ref, sbufs[last], True, xs_ref, sems.at[last], tj)

    @pl.when(pl.program_id(0) == pl.num_programs(0) - 1)
    def _():
        for k in range(TILES_PER_STEP):
            wait_tile(k)


def _dispatch(tinfo, cnt, tok, pos, n_rows):
    ntiles, _, tj = pos.shape
    T, D = tok.shape
    n_blocks = n_rows // MOE_ROWS
    nt = TILES_PER_STEP
    assert ntiles % nt == 0
    sbuf = pltpu.VMEM(_row_tiles_shape(2 * tj, D), jnp.uint32)
    return pl.pallas_call(
        _dispatch_kernel,
        grid=(ntiles // nt,),
        in_specs=[pl.BlockSpec((nt, SUBLANES, LANES), lambda i: (i, 0, 0), memory_space=pltpu.SMEM),
                  pl.BlockSpec(memory_space=pltpu.SMEM),
                  pl.BlockSpec((nt * tj, D), lambda i: (i, 0)),
                  pl.BlockSpec((nt, SUBLANES, tj), lambda i: (i, 0, 0))],
        out_specs=[pl.BlockSpec(memory_space=pl.ANY),
                   pl.BlockSpec(memory_space=pltpu.SMEM),
                   pl.BlockSpec(memory_space=pltpu.SMEM)],
        out_shape=[jax.ShapeDtypeStruct(_row_tiles_shape(n_rows, D), jnp.uint32),
                   jax.ShapeDtypeStruct((N_EXPERTS,), jnp.int32),
                   jax.ShapeDtypeStruct((N_EXPERTS,), jnp.int32)],
        scratch_shapes=[[sbuf] * nt,
                        pltpu.VMEM(_row_tiles_shape(MOE_ROWS // 2, D), jnp.uint32),
                        pltpu.SMEM((N_EXPERTS,), jnp.int32), pltpu.SemaphoreType.DMA((nt,)),
                        pltpu.SemaphoreType.DMA],
        compiler_params=_cparams(("arbitrary",)),
        name="moe_dispatch",
    )(tinfo, cnt, tok, pos)


def _experts_kernel(first_ref, nblk_ref, xs_ref, wg_ref, wu_ref, wd_ref, ys_ref,
                    wg_s, wu_s, wd_s, xbuf, ybuf, in_sems, out_sems):
    e = pl.program_id(0)
    last_e = pl.num_programs(0) - 1
    rows = MOE_ROWS
    n_total = xs_ref.shape[0] // rows
    used = first_ref[last_e] + nblk_ref[last_e]

    def fetch(g, slot):
        return pltpu.make_async_copy(xs_ref.at[pl.ds(g * rows, rows)], xbuf.at[pl.ds(slot * rows, rows)],
                                     in_sems.at[slot])

    def flush(g, slot):
        return pltpu.make_async_copy(ybuf.at[pl.ds(slot * rows, rows)], ys_ref.at[pl.ds(g * rows, rows)],
                                     out_sems.at[slot])

    @pl.when((e == 0) & (used > 0))
    def _():
        fetch(0, 0).start()

    @pl.when(nblk_ref[e] > 0)
    def _():
        wg_s[...] = wg_ref[...].astype(bf16)
        wu_s[...] = wu_ref[...].astype(bf16)
        wd_s[...] = wd_ref[...].astype(bf16)

    def block(b, carry):
        g = first_ref[e] + b
        slot = g % 2
        fetch(g, slot).wait()

        @pl.when(g + 1 < used)
        def _():
            fetch(g + 1, 1 - slot).start()

        @pl.when(g >= 2)
        def _():
            flush(g - 2, slot).wait()
        half = rows // 2
        starts = [slot * rows + r * half for r in range(2)]
        gated = []
        for row0 in starts:
            xb = _unpack_bf16_pairs(_load_row_tiles(xbuf, row0, half)).astype(bf16)
            gated.append((jnp.dot(xb, wg_s[...], preferred_element_type=f32),
                          jnp.dot(xb, wu_s[...], preferred_element_type=f32)))
        for row0, (a, u) in zip(starts, gated):
            h = (a * jax.nn.sigmoid(a) * u).astype(bf16)
            _store_row_tiles(ybuf, _pack_bf16_pairs(jnp.dot(h, wd_s[...], preferred_element_type=f32)), row0)
        flush(g, slot).start()
        return carry
    lax.fori_loop(0, nblk_ref[e], block, 0)

    @pl.when(e == last_e)
    def _():
        for back in (1, 2):
            @pl.when(used >= back)
            def _(back=back):
                flush(used - back, (used - back) % 2).wait()
        ybuf[pl.ds(0, rows)] = jnp.zeros((rows,) + ybuf.shape[1:], ybuf.dtype)
        for wait in (False, True):
            def zero_tail(g, c, wait=wait):
                cp = flush(g, 0)
                if wait:
                    cp.wait()
                else:
                    cp.start()
                return c
            lax.fori_loop(used, n_total, zero_tail, 0)


def _experts(first_block, n_blocks, xs, w_gate, w_up, w_down):
    _, D, De = w_gate.shape
    widx = lambda e, first, nblk: (e, 0, 0)
    buf = pltpu.VMEM(_row_tiles_shape(2 * MOE_ROWS, D), jnp.uint32)
    return pl.pallas_call(
        _experts_kernel,
        grid_spec=pltpu.PrefetchScalarGridSpec(
            num_scalar_prefetch=2,
            grid=(N_EXPERTS,),
            in_specs=[pl.BlockSpec(memory_space=pl.ANY),
                      pl.BlockSpec((None, D, De), widx),
                      pl.BlockSpec((None, D, De), widx),
                      pl.BlockSpec((None, De, D), widx)],
            out_specs=pl.BlockSpec(memory_space=pl.ANY),
            scratch_shapes=[pltpu.VMEM((D, De), bf16), pltpu.VMEM((D, De), bf16), pltpu.VMEM((De, D), bf16),
                            buf, buf, pltpu.SemaphoreType.DMA((2,)), pltpu.SemaphoreType.DMA((2,))],
        ),
        out_shape=jax.ShapeDtypeStruct(xs.shape, jnp.uint32),
        compiler_params=_cparams(("arbitrary",)),
        name="moe_experts",
    )(first_block, n_blocks, xs, w_gate, w_up, w_down)


def _combine_kernel(tinfo_ref, tnext_ref, cnt_ref, h2_ref, vec_ref, g_ref, ys_ref, o_ref,
                    gbufs, stages, start_ref, sems):
    tj = vec_ref.shape[0] // TILES_PER_STEP
    step = pl.program_id(0)

    def fetch(info_ref, t, buf):
        _tile_segments(info_ref, t, start_ref, gbufs[buf], False, ys_ref, sems.at[buf], tj)

    @pl.when(step == 0)
    def _():
        _pad_starts(cnt_ref, start_ref)
        for t in range(GATHER_AHEAD):
            fetch(tinfo_ref, t, t)

    slot = lax.broadcasted_iota(jnp.int32, (tj, 2 * tj), 1)
    for k in range(TILES_PER_STEP):
        pltpu.make_async_copy(ys_ref.at[pl.ds(0, 2 * tj)], gbufs[k], sems.at[k]).wait()
        ahead = k + GATHER_AHEAD
        if ahead < TILES_PER_STEP:
            fetch(tinfo_ref, ahead, ahead)
        else:
            fetch(tnext_ref, ahead - TILES_PER_STEP, ahead - TILES_PER_STEP)
        rows = _unpack_bf16_pairs(_load_row_tiles(gbufs[k])).astype(bf16)
        v = vec_ref[k * tj:(k + 1) * tj, :]
        y = h2_ref[k * tj:(k + 1) * tj, :]
        for c in range(2):
            pick = (slot == v[:, 2 + c:3 + c].astype(jnp.int32)).astype(bf16)
            y = y + v[:, c:c + 1] * jnp.dot(pick, rows, preferred_element_type=f32)
        out = _rms(y, g_ref[...])
        stage = stages[k // 2]
        for q in range(stage.shape[0]):
            stage[q, pl.ds(k % 2, tj, stride=2), :] = out[:, q * LANES:(q + 1) * LANES]
        if k % 2 == 1:
            r0 = (k // 2) * 2 * tj
            for q in range(stage.shape[0]):
                o_ref[0, r0:r0 + 2 * tj, q * LANES:(q + 1) * LANES] = stage[q]


def _combine(tinfo, cnt, h2, vec, final_g, ys, B, L):
    T, D = h2.shape
    ntiles = tinfo.shape[0]
    tj = T // ntiles
    nt = TILES_PER_STEP
    assert GATHER_AHEAD < nt and nt % 2 == 0 and (L // tj) % nt == 0
    nsteps = ntiles // nt
    per_seq = L // (nt * tj)
    g = final_g.reshape(1, D)
    gbuf = pltpu.VMEM(_row_tiles_shape(2 * tj, D), jnp.uint32)
    return pl.pallas_call(
        _combine_kernel,
        grid=(nsteps,),
        in_specs=[pl.BlockSpec((nt, SUBLANES, LANES), lambda i: (i, 0, 0), memory_space=pltpu.SMEM),
                  pl.BlockSpec((nt, SUBLANES, LANES), lambda i: (i + 1, 0, 0), memory_space=pltpu.SMEM),
                  pl.BlockSpec(memory_space=pltpu.SMEM),
                  pl.BlockSpec((nt * tj, D), lambda i: (i, 0)),
                  pl.BlockSpec((nt * tj, LANES), lambda i: (i, 0)),
                  pl.BlockSpec((1, D), lambda i: (0, 0)),
                  pl.BlockSpec(memory_space=pl.ANY)],
        out_specs=pl.BlockSpec((1, nt * tj, D), lambda i: (i // per_seq, i % per_seq, 0)),
        out_shape=jax.ShapeDtypeStruct((B, L, D), f32),
        scratch_shapes=[[gbuf] * nt,
                        [pltpu.VMEM((D // LANES, 2 * tj, LANES), f32)] * (nt // 2),
                        pltpu.SMEM((N_EXPERTS,), jnp.int32), pltpu.SemaphoreType.DMA((nt,))],
        compiler_params=_cparams(("arbitrary",)),
        name="moe_combine",
    )(tinfo, jnp.pad(tinfo, ((0, nt), (0, 0), (0, 0))), cnt, h2, vec, g, ys)


def kernel(x, norm1_g, w_in, conv_w, conv_b, f_w_in, f_b_in, f_w_mid, f_b_mid, f_freq, f_w_out, f_bias,
           mix_g, w_out, norm2_g, w_group, b_group, w_router, b_router, w_gate, w_up, w_down, final_g):
    B, L, D = x.shape
    depth = norm1_g.shape[0]
    assert depth == 1, "the final RMSNorm is fused into the single layer's MoE combine"
    C = D // 2
    Lh = L // 2
    N = B * C
    T = B * L
    n_rows = T * 2 + N_EXPERTS * MOE_ROWS
    cfwd, cinv = _dft_consts(L)
    cfou = _fourier_consts(L)
    i = 0
    wcat = _prep_win(w_in[i], C)
    hcat, asum = _hyena_filters(L, C, f_w_in[i], f_b_in[i], f_w_mid[i], f_b_mid[i], f_freq[i], f_w_out[i])
    kspec, k0 = _filter_spectrum(cfwd, hcat, asum, L, C)
    z, x0, u4 = _inproj(x, norm1_g[i], wcat, conv_w[i], conv_b[i], C)
    z = z.reshape(2, Lh, N)
    x0 = x0.reshape(2, Lh, N)
    u4 = u4.reshape(4, Lh, N)
    s = _conv_fwd(cfwd, z, kspec, k0, C)
    yh = _conv_inv(cinv, s, x0, z, f_bias[i], C)
    yf = _fourier_dft(cfou, u4, L, C)
    wr = jnp.concatenate([jnp.transpose(w_router[i], (1, 0, 2)).reshape(D, N_EXPERTS), w_group[i]], axis=1)
    wr = jnp.pad(wr, ((0, 0), (0, LANES - wr.shape[1])))
    br = jnp.pad(jnp.concatenate([b_router[i].reshape(-1), b_group[i]]), (0, LANES - N_EXPERTS - N_GROUPS))
    h2, tok, pos, vec, tinfo, cnt = _outproj_router(yh, yf, x, mix_g[i], w_out[i].astype(bf16), norm2_g[i],
                                                     wr, br.reshape(1, LANES), C)
    xs, first_block, n_blocks = _dispatch(tinfo, cnt, tok, pos, n_rows)
    ys = _experts(first_block, n_blocks, xs, w_gate[i], w_up[i], w_down[i])
    return _combine(tinfo, cnt, h2, vec, final_g, ys, B, L)
```

```python
import functools
import math

import jax
import jax.numpy as jnp
from jax import lax
from jax.experimental import pallas as pl
from jax.experimental.pallas import tpu as pltpu

HEAD_DIM = 64
N_GROUPS = 4
EXPERTS_PER_GROUP = 8
N_EXPERTS = N_GROUPS * EXPERTS_PER_GROUP
FILTER_BANDS = 16
DECAY_FAST_PCT = 0.3
DECAY_SLOW_PCT = 1.5
DECAY_TARGET = 1e-2
EPS = 1e-6

LANES = 128
SUBLANES = 8
MXU_DIM = 256
CONV_ROWS = 256
CONV_COLS = 1024
FOURIER_COLS = 256
MOE_ROWS = 512
ROW_TILES_PER_STEP = 2
TILES_PER_STEP = 4
GATHER_AHEAD = 2
SEG_SPLIT = 32
VMEM_LIMIT = 56 * 1024 * 1024

f32 = jnp.float32
bf16 = jnp.bfloat16
HI = lax.Precision.HIGHEST


def _cparams(sem):
    return pltpu.CompilerParams(dimension_semantics=sem, vmem_limit_bytes=VMEM_LIMIT)


def _tile(pref, n):
    return min(pref, n)


_SIN_TERMS = tuple((-1.0) ** k / math.factorial(2 * k + 1) for k in range(7))
_COS_TERMS = tuple((-1.0) ** k / math.factorial(2 * k) for k in range(8))


def _cos_sin_of_turn_fraction(prod, m):
    num = prod & (m - 1)
    quadrant = num // (m // 4)
    x = (num & (m // 4 - 1)).astype(f32) * (2.0 * math.pi / m)
    x2 = x * x
    s = jnp.full_like(x, _SIN_TERMS[-1])
    for c in _SIN_TERMS[-2::-1]:
        s = s * x2 + c
    s = s * x
    c_ = jnp.full_like(x, _COS_TERMS[-1])
    for c in _COS_TERMS[-2::-1]:
        c_ = c_ * x2 + c
    odd = (quadrant & 1) == 1
    cos = jnp.where(odd, s, c_)
    sin = jnp.where(odd, c_, s)
    cos = jnp.where((quadrant == 1) | (quadrant == 2), -cos, cos)
    sin = jnp.where(quadrant >= 2, -sin, sin)
    return cos, sin


def _cos_sin_blocks(a, s, b0, m, ncb, d):
    tc, ts = _cos_sin_of_turn_fraction(a * (s * d), m)
    b0_lane = jnp.zeros(d.shape, jnp.int32)
    for cb in range(ncb):
        b0_lane = jnp.where(d == cb, b0(cb), b0_lane)
    ac, asn = _cos_sin_of_turn_fraction(a * b0_lane, m)
    for cb in range(ncb):
        ca = ac[:, cb:cb + 1]
        sa = asn[:, cb:cb + 1]
        yield cb, ca * tc - sa * ts, sa * tc + ca * ts


def _dft_consts_kernel(fwd_ref, inv_ref, *, L):
    Lh = L // 2
    tr = fwd_ref.shape[1]
    r = lax.broadcasted_iota(jnp.int32, (tr, LANES), 0) + pl.program_id(0) * tr
    d = lax.broadcasted_iota(jnp.int32, (tr, LANES), 1)
    alt_r = (1 - 2 * (r & 1)).astype(f32)
    alt_d = (1 - 2 * (d & 1)).astype(f32)
    ncb = Lh // LANES
    fams = [
        (r, 1, lambda cb: cb * LANES, L),
        (r, 2, lambda cb: 2 * cb * LANES + 1, 2 * L),
        (2 * r + 1, 1, lambda cb: cb * LANES, 2 * L),
    ]
    for fi, (a, s, b0, m) in enumerate(fams):
        for cb, c, sn in _cos_sin_blocks(a, s, b0, m, ncb, d):
            sl = slice(cb * LANES, (cb + 1) * LANES)
            if fi == 0:
                fwd_ref[0, :, sl] = c.astype(bf16)
                inv_ref[0, :, sl] = c.astype(bf16)
                fwd_ref[1, :, sl] = jnp.where(r == 0, alt_d, sn).astype(bf16)
                nst = -sn
                if cb == 0:
                    nst = jnp.where(d == 0, -alt_r, nst)
                inv_ref[1, :, sl] = nst.astype(bf16)
            elif fi == 1:
                fwd_ref[2, :, sl] = c.astype(bf16)
                fwd_ref[3, :, sl] = jnp.where(r == 0, alt_d, sn).astype(bf16)
            else:
                inv_ref[2, :, sl] = c.astype(bf16)
                nst = -sn
                if cb == 0:
                    nst = jnp.where(d == 0, -alt_r, nst)
                inv_ref[3, :, sl] = nst.astype(bf16)


def _dft_consts(L):
    Lh = L // 2
    tr = _tile(256, Lh)
    shp = jax.ShapeDtypeStruct((4, Lh, Lh), bf16)
    spec = pl.BlockSpec((4, tr, Lh), lambda i: (0, i, 0))
    return pl.pallas_call(
        functools.partial(_dft_consts_kernel, L=L),
        grid=(Lh // tr,),
        out_specs=[spec, spec],
        out_shape=[shp, shp],
        compiler_params=_cparams(("parallel",)),
        name="dft_consts",
    )()


def _fourier_consts_kernel(o_ref, *, L):
    n = L // 4
    tr = o_ref.shape[0]
    r = lax.broadcasted_iota(jnp.int32, (tr, LANES), 0) + pl.program_id(0) * tr
    d = lax.broadcasted_iota(jnp.int32, (tr, LANES), 1)

    def col_l(cb):
        parity, j0 = divmod(cb * LANES, n // 2)
        return 2 * j0 + parity

    for cb, c, sn in _cos_sin_blocks(r, 2, col_l, n, n // LANES, d):
        o_ref[:, cb * LANES:(cb + 1) * LANES] = c.astype(bf16)
        o_ref[:, n + cb * LANES:n + (cb + 1) * LANES] = (-sn).astype(bf16)


def _fourier_consts(L):
    n = L // 4
    tr = _tile(256, n)
    return pl.pallas_call(
        functools.partial(_fourier_consts_kernel, L=L),
        grid=(n // tr,),
        out_specs=pl.BlockSpec((tr, 2 * n), lambda i: (i, 0)),
        out_shape=jax.ShapeDtypeStruct((n, 2 * n), bf16),
        compiler_params=_cparams(("parallel",)),
        name="fourier_consts",
    )()


def _prep_win_kernel(w_ref, o_ref, *, C):
    w = w_ref[...]
    i = lax.broadcasted_iota(jnp.int32, (LANES, LANES), 0)
    j = lax.broadcasted_iota(jnp.int32, (LANES, LANES), 1)
    same = (i // HEAD_DIM) == (j // HEAD_DIM)
    cos, sin = _cos_sin_of_turn_fraction((i % HEAD_DIM) * (j % HEAD_DIM), HEAD_DIM)
    bc = jnp.where(same, cos, 0.0)
    bs = jnp.where(same, sin, 0.0)
    o_ref[:, :3 * C] = w[:, :3 * C].astype(bf16)
    for k in range(0, C, LANES):
        wf = w[:, 3 * C + k:3 * C + k + LANES]
        o_ref[:, 3 * C + k:3 * C + k + LANES] = jnp.dot(wf, bc, precision=HI, preferred_element_type=f32).astype(bf16)
        o_ref[:, 4 * C + k:4 * C + k + LANES] = jnp.dot(wf, bs, precision=HI, preferred_element_type=f32).astype(bf16)


def _prep_win(w_in, C):
    D = w_in.shape[0]
    tr = _tile(256, D)
    return pl.pallas_call(
        functools.partial(_prep_win_kernel, C=C),
        grid=(D // tr,),
        in_specs=[pl.BlockSpec((tr, 4 * C), lambda i: (i, 0))],
        out_specs=pl.BlockSpec((tr, 5 * C), lambda i: (i, 0)),
        out_shape=jax.ShapeDtypeStruct((D, 5 * C), bf16),
        compiler_params=_cparams(("parallel",)),
        name="prep_win",
    )(w_in)


def _filter_kernel(win_ref, bin_ref, wmid_ref, bmid_ref, freq_ref, wout_ref, h_ref, asum_ref, *, L, C):
    tr = h_ref.shape[1]
    step = pl.program_id(0)
    lane = lax.broadcasted_iota(jnp.int32, (tr, LANES), 1)
    row = lax.broadcasted_iota(jnp.int32, (tr, LANES), 0) + step * tr
    band_i = jnp.where(lane <= FILTER_BANDS, lane - 1, lane - 1 - FILTER_BANDS)
    band = 1e-4 + band_i.astype(f32) * ((FILTER_BANDS - 1 - 1e-4) / (FILTER_BANDS - 1))
    ch = lax.broadcasted_iota(jnp.int32, (tr, 2 * C), 1) % C
    max_decay = math.log(DECAY_TARGET) / DECAY_FAST_PCT
    min_decay = math.log(DECAY_TARGET) / DECAY_SLOW_PCT
    delta = jnp.abs(min_decay + ch.astype(f32) * ((max_decay - min_decay) / (C - 1)))
    is_bwd = lax.broadcasted_iota(jnp.int32, (tr, 2 * C), 1) >= C
    row_c = lax.broadcasted_iota(jnp.int32, (tr, 2 * C), 0) + step * tr

    @pl.when(step == 0)
    def _():
        asum_ref[...] = jnp.zeros_like(asum_ref)

    total = jnp.zeros((1, 2 * C), f32)
    for p in range(2):
        pos = (2 * row + p).astype(f32)
        t = pos * (1.0 / (L - 1))
        w = pos * (2.0 * math.pi / L)
        fw = band * w
        z = jnp.where(lane == 0, t,
                      jnp.where(lane <= FILTER_BANDS, jnp.cos(fw),
                                jnp.where(lane <= 2 * FILTER_BANDS, -jnp.sin(fw), 0.0)))
        h = jnp.sin(freq_ref[0:1, :] * (jnp.dot(z, win_ref[...], precision=HI, preferred_element_type=f32)
                                       + bin_ref[...]))
        for i in range(wmid_ref.shape[0]):
            h = jnp.sin(freq_ref[i + 1:i + 2, :]
                        * (jnp.dot(h, wmid_ref[i], precision=HI, preferred_element_type=f32)
                           + bmid_ref[i:i + 1, :]))
        out = jnp.dot(h, wout_ref[...], precision=HI, preferred_element_type=f32)
        tc = (2 * row_c + p).astype(f32) * (1.0 / (L - 1))
        out = out * jnp.exp(-tc * delta)
        if p == 0:
            out = jnp.where(is_bwd & (row_c == 0), 0.0, out)
        h_ref[p] = out.astype(bf16)
        total = total + jnp.sum(jnp.abs(out), axis=0, keepdims=True)
    asum_ref[0:1, :] += total


def _hyena_filters(L, C, f_w_in, f_b_in, f_w_mid, f_b_mid, f_freq, f_w_out):
    Lh = L // 2
    tr = _tile(256, Lh)
    order = f_w_in.shape[1]
    win = jnp.pad(f_w_in, ((0, LANES - f_w_in.shape[0]), (0, 0)))
    full = lambda a: pl.BlockSpec(a.shape, lambda i: (0,) * a.ndim)
    args = (win, f_b_in.reshape(1, order), f_w_mid, f_b_mid, f_freq, f_w_out)
    return pl.pallas_call(
        functools.partial(_filter_kernel, L=L, C=C),
        grid=(Lh // tr,),
        in_specs=[full(a) for a in args],
        out_specs=[pl.BlockSpec((2, tr, 2 * C), lambda i: (0, i, 0)),
                   pl.BlockSpec((SUBLANES, 2 * C), lambda i: (0, 0))],
        out_shape=[jax.ShapeDtypeStruct((2, Lh, 2 * C), bf16),
                   jax.ShapeDtypeStruct((SUBLANES, 2 * C), f32)],
        compiler_params=_cparams(("arbitrary",)),
        name="hyena_filters",
    )(*args)


def _fwd_products(c_ref, x_ref):
    xe, xo = x_ref[0], x_ref[1]
    ce = jnp.dot(c_ref[0], xe, preferred_element_type=f32)
    se = jnp.dot(c_ref[1], xe, preferred_element_type=f32)
    co = jnp.dot(c_ref[2], xo, preferred_element_type=f32)
    so = jnp.dot(c_ref[3], xo, preferred_element_type=f32)
    return ce, se, co, so


def _filter_spectrum_kernel(c_ref, h_ref, asum_ref, k_ref, k0_ref, *, L, C):
    ce, se, co, so = _fwd_products(c_ref, h_ref)
    a = asum_ref[0:1, :]
    scale = 1.0 / ((a[:, :C] + a[:, C:]) * L)
    F = lambda v: v[:, :C]
    G = lambda v: v[:, C:]
    k_ref[0] = (F(ce) + F(co) + G(ce) + G(co)) * scale
    k_ref[1] = (-(F(se) + F(so)) + (G(se) + G(so))) * scale
    k_ref[2] = (F(ce) - F(co) + G(ce) - G(co)) * scale
    k_ref[3] = ((F(se) - F(so)) - (G(se) - G(so))) * scale

    @pl.when(pl.program_id(0) == 0)
    def _():
        k0_ref[...] = jnp.zeros_like(k0_ref)
        k0_ref[0:1, :] = ((F(se) + G(se)) * scale)[0:1, :]
        k0_ref[1:2, :] = ((F(so) - G(so)) * scale)[0:1, :]


def _filter_spectrum(cfwd, hcat, asum, L, C):
    Lh = L // 2
    tm = _tile(256, Lh)
    return pl.pallas_call(
        functools.partial(_filter_spectrum_kernel, L=L, C=C),
        grid=(Lh // tm,),
        in_specs=[pl.BlockSpec((4, tm, Lh), lambda i: (0, i, 0)),
                  pl.BlockSpec((2, Lh, 2 * C), lambda i: (0, 0, 0)),
                  pl.BlockSpec((SUBLANES, 2 * C), lambda i: (0, 0))],
        out_specs=[pl.BlockSpec((4, tm, C), lambda i: (0, i, 0)),
                   pl.BlockSpec((SUBLANES, C), lambda i: (0, 0))],
        out_shape=[jax.ShapeDtypeStruct((4, Lh, C), f32),
                   jax.ShapeDtypeStruct((SUBLANES, C), f32)],
        compiler_params=_cparams(("arbitrary",)),
        name="filter_spectrum",
    )(cfwd, hcat, asum)


def _conv_fwd_kernel(c_ref, z_ref, k_ref, k0_ref, s_ref):
    products = _fwd_products(c_ref, z_ref)
    C = k_ref.shape[2]
    first = pl.program_id(1) == 0
    for h in range(z_ref.shape[2] // C):
        cols = slice(h * C, (h + 1) * C)
        ce, se, co, so = (v[:, cols] for v in products)
        zsr, zsi, zdr, zdi = ce + co, -(se + so), ce - co, se - so
        ksr, ksi, kdr, kdi = k_ref[0], k_ref[1], k_ref[2], k_ref[3]
        psr = zsr * ksr - zsi * ksi
        psi = zsr * ksi + zsi * ksr
        pdr = zdr * kdr - zdi * kdi
        pdi = zdr * kdi + zdi * kdr
        gen = (psr + pdr, psi - pdi, psr - pdr, psi + pdi)

        @pl.when(first)
        def _(cols=cols, ce=ce, se=se, so=so, zsr=zsr, zdr=zdr, ksr=ksr, kdr=kdr, gen=gen):
            row0 = lax.broadcasted_iota(jnp.int32, ce.shape, 0) == 0
            p0 = zsr * ksr
            pl_ = zdr * kdr
            ak, bk = k0_ref[0:1, :], k0_ref[1:2, :]
            phr = se * ak - so * bk
            phi = -(se * bk + so * ak)
            spec = (0.5 * (p0 + pl_), -phr, 0.5 * (p0 - pl_), phi)
            for i in range(4):
                s_ref[i, :, cols] = jnp.where(row0, spec[i], gen[i]).astype(bf16)

        @pl.when(jnp.logical_not(first))
        def _(cols=cols, gen=gen):
            for i in range(4):
                s_ref[i, :, cols] = gen[i].astype(bf16)


def _conv_fwd(cfwd, z, kspec, k0, C):
    _, Lh, N = z.shape
    tm = _tile(CONV_ROWS, Lh)
    tn = _tile(CONV_COLS, N)
    return pl.pallas_call(
        _conv_fwd_kernel,
        grid=(N // tn, Lh // tm),
        in_specs=[pl.BlockSpec((4, tm, Lh), lambda n, i: (0, i, 0)),
                  pl.BlockSpec((2, Lh, tn), lambda n, i: (0, 0, n)),
                  pl.BlockSpec((4, tm, C), lambda n, i: (0, i, 0)),
                  pl.BlockSpec((SUBLANES, C), lambda n, i: (0, 0))],
        out_specs=pl.BlockSpec((4, tm, tn), lambda n, i: (0, i, n)),
        out_shape=jax.ShapeDtypeStruct((4, Lh, N), bf16),
        compiler_params=_cparams(("parallel", "arbitrary")),
        name="conv_fwd",
    )(cfwd, z, kspec, k0)


def _pair_products(c_ref, r_ref):
    ev = (jnp.dot(c_ref[0], r_ref[0], preferred_element_type=f32)
          + jnp.dot(c_ref[1], r_ref[1], preferred_element_type=f32))
    od = (jnp.dot(c_ref[2], r_ref[2], preferred_element_type=f32)
          + jnp.dot(c_ref[3], r_ref[3], preferred_element_type=f32))
    return ev, od


def _conv_inv_kernel(c_ref, s_ref, x0_ref, z_ref, fb_ref, y_ref):
    conv = _pair_products(c_ref, s_ref)
    C = fb_ref.shape[1]
    fb = fb_ref[...]
    for p in range(2):
        for h in range(z_ref.shape[2] // C):
            cols = slice(h * C, (h + 1) * C)
            zp = z_ref[p, :, cols].astype(f32)
            y_ref[p, :, cols] = (x0_ref[p, :, cols].astype(f32) * (conv[p][:, cols] + zp * fb)).astype(bf16)


def _conv_inv(cinv, s, x0, z, f_bias, C):
    _, Lh, N = s.shape
    tm = _tile(CONV_ROWS, Lh)
    tn = _tile(CONV_COLS, N)
    return pl.pallas_call(
        _conv_inv_kernel,
        grid=(N // tn, Lh // tm),
        in_specs=[pl.BlockSpec((4, tm, Lh), lambda n, i: (0, i, 0)),
                  pl.BlockSpec((4, Lh, tn), lambda n, i: (0, 0, n)),
                  pl.BlockSpec((2, tm, tn), lambda n, i: (0, i, n)),
                  pl.BlockSpec((2, tm, tn), lambda n, i: (0, i, n)),
                  pl.BlockSpec((1, C), lambda n, i: (0, 0))],
        out_specs=pl.BlockSpec((2, tm, tn), lambda n, i: (0, i, n)),
        out_shape=jax.ShapeDtypeStruct((2, Lh, N), bf16),
        compiler_params=_cparams(("parallel", "arbitrary")),
        name="conv_inv",
    )(cinv, s, x0, z, f_bias.reshape(1, C))


def _fourier_kernel(c_ref, u_ref, y_ref, rhs, stage, tw1, tw2, *, L, scale):
    Lh, n, H = L // 2, L // 4, L // 8
    tn = u_ref.shape[2]

    @pl.when(pl.program_id(0) == 0)
    def _():
        r1 = lax.broadcasted_iota(jnp.int32, (Lh, LANES), 0)
        tw1[0], tw1[1] = _cos_sin_of_turn_fraction(2 * (r1 % n) + r1 // n, L)
        r2 = lax.broadcasted_iota(jnp.int32, (n, LANES), 0)
        tw2[0], tw2[1] = _cos_sin_of_turn_fraction(2 * (r2 % H) + r2 // H, Lh)

    def rotate(a, b, c, s):
        return c * a - s * b, s * a + c * b

    for q in range(tn // LANES):
        cols = slice(q * LANES, (q + 1) * LANES)
        for p in range(2):
            first = slice(p * n, p * n + H)
            second = slice(p * n + H, (p + 1) * n)
            out = slice(p * H, (p + 1) * H)
            out_b = slice(n + p * H, n + (p + 1) * H)
            c2, s2 = tw2[0, out, :], tw2[1, out, :]

            def emit(sub, a0, b0, a1, b1):
                rhs[sub, out, cols] = (a0 + a1).astype(bf16)
                rhs[sub, out_b, cols] = (b0 + b1).astype(bf16)
                da, db = rotate(a0 - a1, b0 - b1, c2, s2)
                rhs[sub + 1, out, cols] = da.astype(bf16)
                rhs[sub + 1, out_b, cols] = db.astype(bf16)

            ld = lambda k, rows: u_ref[k, rows, cols].astype(f32)
            emit(0, ld(0, first), ld(1, first), ld(0, second), ld(1, second))
            odd = [rotate(ld(2, rows), ld(3, rows), tw1[0, rows, :], tw1[1, rows, :]) for rows in (first, second)]
            emit(2, odd[0][0], odd[0][1], odd[1][0], odd[1][1])

    for parity in range(2):
        pair = [jnp.dot(c_ref[...], rhs[2 * parity + h], preferred_element_type=f32) * scale for h in range(2)]
        for q in range(tn // LANES):
            cols = slice(q * LANES, (q + 1) * LANES)
            for h in range(2):
                stage[q, pl.ds(h, n, stride=2), :] = pair[h][:, cols]
            y_ref[parity, :, cols] = stage[q].astype(bf16)


def _fourier_dft(cfou, u4, L, C):
    _, Lh, N = u4.shape
    n = L // 4
    tn = _tile(FOURIER_COLS, C)
    return pl.pallas_call(
        functools.partial(_fourier_kernel, L=L, scale=1.0 / math.sqrt(L * HEAD_DIM)),
        grid=(N // tn,),
        in_specs=[pl.BlockSpec((n, 2 * n), lambda j: (0, 0)),
                  pl.BlockSpec((4, Lh, tn), lambda j: (0, 0, j))],
        out_specs=pl.BlockSpec((2, Lh, tn), lambda j: (0, 0, j)),
        out_shape=jax.ShapeDtypeStruct((2, Lh, N), bf16),
        scratch_shapes=[pltpu.VMEM((4, 2 * n, tn), bf16), pltpu.VMEM((tn // LANES, Lh, LANES), f32),
                        pltpu.VMEM((2, Lh, LANES), f32), pltpu.VMEM((2, n, LANES), f32)],
        compiler_params=_cparams(("arbitrary",)),
        name="fourier_dft",
    )(cfou, u4)


def _rms(x, g):
    return x * lax.rsqrt(jnp.mean(x * x, axis=-1, keepdims=True) + EPS) * g


def _stage_rows(chunks, x):
    for k in range(chunks.shape[0]):
        chunks[k] = x[:, k * LANES:(k + 1) * LANES]


def _parity_rows(chunks, p, n):
    return jnp.concatenate([chunks[k, pl.ds(p, n, stride=2), :] for k in range(chunks.shape[0])], axis=1)


def _inproj_kernel(xa_ref, xb_ref, pa_ref, na_ref, pb_ref, nb_ref, g_ref, w_ref, cw_ref, cb_ref,
                   z_ref, x0_ref, u_ref, stage, *, C, D):
    tj = xa_ref.shape[1] // 2
    jt = pl.program_id(1)
    n_half = pl.num_programs(1)
    g = g_ref[...]
    w = w_ref[...]
    cw0, cw1, cw2 = cw_ref[0:1, :], cw_ref[1:2, :], cw_ref[2:3, :]
    cb = cb_ref[...]
    row = lax.broadcasted_iota(jnp.int32, (tj, 3 * C), 0)
    four = []
    halves = ((xa_ref, pa_ref, na_ref, jt == 0, False),
              (xb_ref, pb_ref, nb_ref, False, jt == n_half - 1))
    for h, (x_ref, p_ref, n_ref, at_start, at_end) in enumerate(halves):
        _stage_rows(stage, x_ref[0])
        lhs = jnp.concatenate([
            _rms(_parity_rows(stage, 0, tj), g).astype(bf16),
            _rms(_parity_rows(stage, 1, tj), g).astype(bf16),
            _rms(p_ref[0], g).astype(bf16),
            _rms(n_ref[0], g).astype(bf16)], axis=0)
        res = jnp.dot(lhs, w, preferred_element_type=f32)
        pe, po = res[:tj, :3 * C], res[tj:2 * tj, :3 * C]
        prev = res[2 * tj + SUBLANES - 1:2 * tj + SUBLANES, :3 * C]
        nxt = res[2 * tj + SUBLANES:2 * tj + SUBLANES + 1, :3 * C]
        if at_start is not False:
            prev = jnp.where(at_start, 0.0, prev)
        if at_end is not False:
            nxt = jnp.where(at_end, 0.0, nxt)
        po_dn = jnp.where(row == 0, prev, pltpu.roll(po, 1, 0))
        pe_up = jnp.where(row == tj - 1, nxt, pltpu.roll(pe, tj - 1, 0))
        uc = (cb + cw0 * po_dn + cw1 * pe + cw2 * po,
              cb + cw0 * pe + cw1 * po + cw2 * pe_up)
        for p in range(2):
            x0_ref[p, h] = uc[p][:, :C].astype(bf16)
            z_ref[p, h] = (uc[p][:, C:2 * C] * uc[p][:, 2 * C:]).astype(bf16)
        four.append((res[:tj, 3 * C:], res[tj:2 * tj, 3 * C:]))
    for p in range(2):
        fa, fb = four[0][p], four[1][p]
        u_ref[0, p] = (fa[:, :C] + fb[:, :C]).astype(bf16)
        u_ref[1, p] = (fa[:, C:] + fb[:, C:]).astype(bf16)
        u_ref[2, p] = (fa[:, :C] - fb[:, :C]).astype(bf16)
        u_ref[3, p] = (fa[:, C:] - fb[:, C:]).astype(bf16)


def _inproj(x, norm_g, wcat, conv_w, conv_b, C):
    B, L, D = x.shape
    Lq = L // 4
    tj = _tile(256, Lq)
    nq = Lq // tj
    N = B * C
    hb = 2 * tj // SUBLANES
    last = L // SUBLANES - 1
    main = lambda off: pl.BlockSpec((1, 2 * tj, D), lambda b, j: (b, j + off, 0))
    prev = lambda off: pl.BlockSpec((1, SUBLANES, D), lambda b, j: (b, jnp.maximum((j + off) * hb - 1, 0), 0))
    nxt = lambda off: pl.BlockSpec((1, SUBLANES, D), lambda b, j: (b, jnp.minimum((j + off + 1) * hb, last), 0))
    full = lambda a: pl.BlockSpec(a.shape, lambda b, j: (0,) * a.ndim)
    g = norm_g.reshape(1, D)
    cb = conv_b.reshape(1, 3 * C)
    return pl.pallas_call(
        functools.partial(_inproj_kernel, C=C, D=D),
        grid=(B, nq),
        in_specs=[main(0), main(nq), prev(0), nxt(0), prev(nq), nxt(nq),
                  full(g), full(wcat), full(conv_w), full(cb)],
        out_specs=[pl.BlockSpec((2, 2, tj, C), lambda b, j: (0, 0, j, b)),
                   pl.BlockSpec((2, 2, tj, C), lambda b, j: (0, 0, j, b)),
                   pl.BlockSpec((4, 2, tj, C), lambda b, j: (0, 0, j, b))],
        out_shape=[jax.ShapeDtypeStruct((2, 2, Lq, N), bf16),
                   jax.ShapeDtypeStruct((2, 2, Lq, N), bf16),
                   jax.ShapeDtypeStruct((4, 2, Lq, N), bf16)],
        scratch_shapes=[pltpu.VMEM((D // LANES, 2 * tj, LANES), f32)],
        compiler_params=_cparams(("parallel", "arbitrary")),
        name="inproj",
    )(x, x, x, x, x, x, g, wcat, conv_w, cb)


def _pack_bf16_pairs(v, is_bf16_exact=False):
    k = v.shape[1] // 2
    rounded = v if is_bf16_exact else v.astype(bf16).astype(f32)
    hi = pltpu.bitcast(rounded[:, :k], jnp.uint32)
    lo = pltpu.bitcast(rounded[:, k:], jnp.uint32)
    return hi | (lo >> 16)


def _unpack_bf16_pairs(u):
    hi = pltpu.bitcast(u & jnp.uint32(0xFFFF0000), f32)
    lo = pltpu.bitcast(u << 16, f32)
    return jnp.concatenate([hi, lo], axis=1)


def _row_tiles_shape(n, D):
    return (n, D // (2 * LANES), LANES)


def _store_row_tiles(ref, packed, row0=0):
    total, per_row, _ = ref.shape
    n = packed.shape[0]
    flat = ref.reshape(total * per_row, LANES)
    for s in range(per_row):
        flat[pl.ds(row0 * per_row + s, n, stride=per_row), :] = packed[:, s * LANES:(s + 1) * LANES]


def _load_row_tiles(ref, row0=0, n=None):
    total, per_row, _ = ref.shape
    n = total if n is None else n
    flat = ref.reshape(total * per_row, LANES)
    return jnp.concatenate([flat[pl.ds(row0 * per_row + s, n, stride=per_row), :] for s in range(per_row)],
                           axis=1)


def _outproj_router_kernel(yh_ref, yf_ref, x_ref, mg_ref, wo_ref, g2_ref, wr_ref, br_ref,
                           h2_ref, tok_ref, pos_ref, vec_ref, tinfo_ref, cnt_ref, carry_ref, stage, *, C):
    n_sub = pos_ref.shape[0]
    tj = pos_ref.shape[2]
    _stage_rows(stage, x_ref[0])

    @pl.when((pl.program_id(0) == 0) & (pl.program_id(1) == 0))
    def _():
        carry_ref[...] = jnp.zeros_like(carry_ref)

    li = lax.broadcasted_iota(jnp.int32, (LANES, LANES), 0)
    lj = lax.broadcasted_iota(jnp.int32, (LANES, LANES), 1)
    hw = _tile(MXU_DIM, C)
    ai = lax.broadcasted_iota(jnp.int32, (hw, hw), 0)
    aj = lax.broadcasted_iota(jnp.int32, (hw, hw), 1)
    avg = jnp.where(ai // HEAD_DIM == aj // HEAD_DIM, 1.0 / HEAD_DIM, 0.0).astype(bf16)
    upper = (li < lj).astype(bf16)
    ri = lax.broadcasted_iota(jnp.int32, (tj, tj), 0)
    ci = lax.broadcasted_iota(jnp.int32, (tj, tj), 1)
    tri = (ci < ri).astype(bf16)
    wr = wr_ref[...]
    wr_hi = wr.astype(bf16)
    wr_lo = (wr - wr_hi.astype(f32)).astype(bf16)
    wr_both = jnp.concatenate([wr_hi, wr_lo], axis=1)
    lane = lax.broadcasted_iota(jnp.int32, (n_sub * tj, LANES), 1).astype(f32)
    sub = lax.broadcasted_iota(jnp.int32, (SUBLANES, LANES), 0)
    neg = jnp.float32(-jnp.inf)
    big = jnp.float32(LANES)

    def first_argmax(vals, mask):
        v = jnp.where(mask, vals, neg)
        m = jnp.max(v, axis=1, keepdims=True)
        idx = jnp.min(jnp.where(mask & (v == m), lane, big), axis=1, keepdims=True)
        return m, idx

    tile_logits = []
    for s in range(n_sub):
        r, p = divmod(s, 2)
        rows = slice(s * tj, (s + 1) * tj)
        parts = []
        for ref in (yh_ref, yf_ref):
            y = ref[p, r * tj:(r + 1) * tj, :].astype(f32)
            ysq = (y * y).astype(bf16)
            ms = jnp.concatenate([jnp.dot(ysq[:, k:k + hw], avg, preferred_element_type=f32)
                                  for k in range(0, C, hw)], axis=1)
            parts.append(y * lax.rsqrt(ms + EPS))
        yn = (jnp.concatenate(parts, axis=1) * mg_ref[...]).astype(bf16)
        h2 = _parity_rows(stage, r * 2 * tj + p, tj) + jnp.dot(yn, wo_ref[...], preferred_element_type=f32)
        h2_ref[rows, :] = h2
        tok = _rms(h2, g2_ref[...])
        tok_ref[rows, :] = tok.astype(bf16)

        t_hi = tok.astype(bf16)
        t_lo = (tok - t_hi.astype(f32)).astype(bf16)
        hi_terms = jnp.dot(t_hi, wr_both, preferred_element_type=f32)
        tile_logits.append(hi_terms[:, :LANES] + hi_terms[:, LANES:]
                           + jnp.dot(t_lo, wr_hi, preferred_element_type=f32))
    logits = jnp.concatenate(tile_logits, axis=0) + br_ref[...]

    gmask = (lane >= N_EXPERTS) & (lane < N_EXPERTS + N_GROUPS)
    gmax, gidx = first_argmax(logits, gmask)
    gsel = gidx - N_EXPERTS
    p_group = 1.0 / jnp.sum(jnp.where(gmask, jnp.exp(logits - gmax), 0.0), axis=1, keepdims=True)
    emask = (lane >= gsel * EXPERTS_PER_GROUP) & (lane < (gsel + 1) * EXPERTS_PER_GROUP)
    l0, e0 = first_argmax(logits, emask)
    l1, e1 = first_argmax(logits, emask & (lane != e0))
    ex = jnp.exp(l1 - l0)
    w0 = p_group / (1.0 + ex)
    w1 = p_group * ex / (1.0 + ex)

    oh0 = (lane == e0).astype(f32)
    oh1 = (lane == e1).astype(f32)
    both = (oh0 + oh1).astype(bf16)
    where, counts = [], []
    for s in range(n_sub):
        tile_both = both[s * tj:(s + 1) * tj, :]
        ranks = jnp.dot(tri, tile_both, preferred_element_type=f32)
        n_e = ranks[tj - 1:tj, :] + tile_both[tj - 1:tj, :].astype(f32)
        n_e8 = jnp.broadcast_to(n_e, (SUBLANES, LANES))
        tile_off = jnp.dot(n_e8.astype(bf16), upper, preferred_element_type=f32)[0:1, :]
        where.append(ranks + tile_off)
        counts.append(n_e8)
    where = jnp.concatenate(where, axis=0)
    pos0 = jnp.sum(oh0 * where, axis=1, keepdims=True)
    pos1 = jnp.sum(oh1 * where, axis=1, keepdims=True)
    cols = jnp.where(lane == 0, w0, jnp.where(lane == 1, w1,
                     jnp.where(lane == 2, pos0, jnp.where(lane == 3, pos1, 0.0))))
    vec_ref[...] = cols
    base = carry_ref[...]
    for s in range(n_sub):
        pos_ref[s] = cols[s * tj:(s + 1) * tj, :].T[:SUBLANES, :].astype(jnp.int32)
        long_seg = (jnp.max(counts[s], axis=1, keepdims=True) >= SEG_SPLIT).astype(f32)
        tinfo_ref[s] = jnp.where(sub == 0, counts[s], jnp.where(sub == 1, base, jnp.where(sub == 2, long_seg, 0.0))
                                 ).astype(jnp.int32)
        base = base + counts[s]
    carry_ref[...] = base
    cnt_ref[...] = base.astype(jnp.int32)


def _outproj_router(yh, yf, x, mix_g, wo, norm2_g, wr, br, C):
    B, L, D = x.shape
    Lh = L // 2
    tj = _tile(256, Lh)
    rt = ROW_TILES_PER_STEP
    ns = 2 * rt
    nj = Lh // (rt * tj)
    T = B * L
    ntiles = T // tj
    lin = lambda b, j: b * nj + j
    full = lambda a: pl.BlockSpec(a.shape, lambda b, j: (0,) * a.ndim)
    mg = mix_g.reshape(1, D)
    g2 = norm2_g.reshape(1, D)
    return pl.pallas_call(
        functools.partial(_outproj_router_kernel, C=C),
        grid=(B, nj),
        in_specs=[pl.BlockSpec((2, rt * tj, C), lambda b, j: (0, j, b)),
                  pl.BlockSpec((2, rt * tj, C), lambda b, j: (0, j, b)),
                  pl.BlockSpec((1, ns * tj, D), lambda b, j: (b, j, 0)),
                  full(mg), full(wo), full(g2), full(wr), full(br)],
        out_specs=[pl.BlockSpec((ns * tj, D), lambda b, j: (lin(b, j), 0)),
                   pl.BlockSpec((ns * tj, D), lambda b, j: (lin(b, j), 0)),
                   pl.BlockSpec((ns, SUBLANES, tj), lambda b, j: (lin(b, j), 0, 0)),
                   pl.BlockSpec((ns * tj, LANES), lambda b, j: (lin(b, j), 0)),
                   pl.BlockSpec((ns, SUBLANES, LANES), lambda b, j: (lin(b, j), 0, 0)),
                   pl.BlockSpec((SUBLANES, LANES), lambda b, j: (0, 0))],
        out_shape=[jax.ShapeDtypeStruct((T, D), f32),
                   jax.ShapeDtypeStruct((T, D), bf16),
                   jax.ShapeDtypeStruct((ntiles, SUBLANES, tj), jnp.int32),
                   jax.ShapeDtypeStruct((T, LANES), f32),
                   jax.ShapeDtypeStruct((ntiles, SUBLANES, LANES), jnp.int32),
                   jax.ShapeDtypeStruct((SUBLANES, LANES), jnp.int32)],
        scratch_shapes=[pltpu.VMEM((SUBLANES, LANES), f32), pltpu.VMEM((D // LANES, ns * tj, LANES), f32)],
        compiler_params=_cparams(("arbitrary", "arbitrary")),
        name="outproj_router",
    )(yh, yf, x, mg, wo, g2, wr, br)


def _padded(c):
    return ((c + MOE_ROWS - 1) // MOE_ROWS) * MOE_ROWS


def _pad_starts(cnt_ref, start_ref):
    def body(e, acc):
        start_ref[e] = acc
        return acc + _padded(cnt_ref[0, e])
    return lax.fori_loop(0, N_EXPERTS, body, jnp.int32(0))


def _segment_copies(n, src, src_off, dst, dst_off, sem, max_rows, wait, src_advances=True, min_rows=1):
    bit = max_rows
    while bit >= min_rows:
        take = n & bit

        @pl.when(take != 0)
        def _(src_off=src_off, dst_off=dst_off, bit=bit):
            cp = pltpu.make_async_copy(src.at[pl.ds(src_off, bit)], dst.at[pl.ds(dst_off, bit)], sem)
            if wait:
                cp.wait()
            else:
                cp.start()
        if src_advances:
            src_off = src_off + take
        dst_off = dst_off + take
        bit //= 2


def _tile_segments(tinfo_ref, t, start_ref, local, local_is_src, remote, sem, tj):
    def copies(long_part):
        off = 0
        for e in range(N_EXPERTS):
            n = tinfo_ref[t, 0, e]
            far = start_ref[e] + tinfo_ref[t, 1, e]
            if long_part:
                args = (n, off, far, tj, SEG_SPLIT)
            else:
                head = n & ~(SEG_SPLIT - 1)
                args = (n, off + head, far + head, SEG_SPLIT // 2, 1)
            n_, near_, far_, hi, lo = args
            if local_is_src:
                _segment_copies(n_, local, near_, remote, far_, sem, hi, wait=False, min_rows=lo)
            else:
                _segment_copies(n_, remote, far_, local, near_, sem, hi, wait=False, min_rows=lo)
            off = off + n

    copies(long_part=False)

    @pl.when(tinfo_ref[t, 2, 0] != 0)
    def _():
        copies(long_part=True)


def _dispatch_kernel(tinfo_ref, cnt_ref, tok_ref, pos_ref, xs_ref, first_ref, nblk_ref,
                     sbufs, zbuf, start_ref, sems, zsem):
    tj = pos_ref.shape[2]
    n_blocks = xs_ref.shape[0] // MOE_ROWS

    @pl.when(pl.program_id(0) == 0)
    def _():
        total = _pad_starts(cnt_ref, start_ref)
        used = total // MOE_ROWS

        def block_ranges(e, c):
            first_ref[e] = start_ref[e] // MOE_ROWS
            nblk_ref[e] = _padded(cnt_ref[0, e]) // MOE_ROWS
            return c
        lax.fori_loop(0, N_EXPERTS, block_ranges, 0)

        zbuf[...] = jnp.zeros_like(zbuf)
        zrows = zbuf.shape[0]
        for wait in (False, True):
            def zero_pad(e, c, wait=wait):
                cnt = cnt_ref[0, e]
                _segment_copies(_padded(cnt) - cnt, zbuf, 0, xs_ref, start_ref[e] + cnt, zsem,
                                zrows, wait, src_advances=False)
                return c
            lax.fori_loop(0, N_EXPERTS, zero_pad, 0)

            def zero_tail(i, c, wait=wait):
                cp = pltpu.make_async_copy(zbuf, xs_ref.at[pl.ds(total + i * zrows, zrows)], zsem)
                if wait:
                    cp.wait()
                else:
                    cp.start()
                return c
            lax.fori_loop(0, (n_blocks - used) * (MOE_ROWS // zrows), zero_tail, 0)

    def wait_tile(k):
        pltpu.make_async_copy(sbufs[k], xs_ref.at[pl.ds(0, 2 * tj)], sems.at[k]).wait()

    slot = lax.broadcasted_iota(jnp.int32, (2 * tj, tj), 0)
    for k in range(TILES_PER_STEP):
        @pl.when(pl.program_id(0) > 0)
        def _(k=k):
            wait_tile(k)
        perm = ((slot == pos_ref[k, 2:3, :]) | (slot == pos_ref[k, 3:4, :])).astype(bf16)
        sorted_rows = jnp.dot(perm, tok_ref[k * tj:(k + 1) * tj, :], preferred_element_type=f32)
        _store_row_tiles(sbufs[k], _pack_bf16_pairs(sorted_rows, is_bf16_exact=True))
        if k > 0:
            _tile_segments(tinfo_ref, k - 1, start_ref, sbufs[k - 1], True, xs_ref, sems.at[k - 1], tj)
    last = TILES_PER_STEP - 1
    _tile_segments(tinfo_ref, last, start_ref, sbufs[last], True, xs_ref, sems.at[last], tj)

    @pl.when(pl.program_id(0) == pl.num_programs(0) - 1)
    def _():
        for k in range(TILES_PER_STEP):
            wait_tile(k)


def _dispatch(tinfo, cnt, tok, pos, n_rows):
    ntiles, _, tj = pos.shape
    T, D = tok.shape
    n_blocks = n_rows // MOE_ROWS
    nt = TILES_PER_STEP
    assert ntiles % nt == 0
    sbuf = pltpu.VMEM(_row_tiles_shape(2 * tj, D), jnp.uint32)
    return pl.pallas_call(
        _dispatch_kernel,
        grid=(ntiles // nt,),
        in_specs=[pl.BlockSpec((nt, SUBLANES, LANES), lambda i: (i, 0, 0), memory_space=pltpu.SMEM),
                  pl.BlockSpec(memory_space=pltpu.SMEM),
                  pl.BlockSpec((nt * tj, D), lambda i: (i, 0)),
                  pl.BlockSpec((nt, SUBLANES, tj), lambda i: (i, 0, 0))],
        out_specs=[pl.BlockSpec(memory_space=pl.ANY),
                   pl.BlockSpec(memory_space=pltpu.SMEM),
                   pl.BlockSpec(memory_space=pltpu.SMEM)],
        out_shape=[jax.ShapeDtypeStruct(_row_tiles_shape(n_rows, D), jnp.uint32),
                   jax.ShapeDtypeStruct((N_EXPERTS,), jnp.int32),
                   jax.ShapeDtypeStruct((N_EXPERTS,), jnp.int32)],
        scratch_shapes=[[sbuf] * nt,
                        pltpu.VMEM(_row_tiles_shape(MOE_ROWS // 2, D), jnp.uint32),
                        pltpu.SMEM((N_EXPERTS,), jnp.int32), pltpu.SemaphoreType.DMA((nt,)),
                        pltpu.SemaphoreType.DMA],
        compiler_params=_cparams(("arbitrary",)),
        name="moe_dispatch",
    )(tinfo, cnt, tok, pos)


def _experts_kernel(first_ref, nblk_ref, xs_ref, wg_ref, wu_ref, wd_ref, ys_ref,
                    wg_s, wu_s, wd_s, xbuf, ybuf, in_sems, out_sems):
    e = pl.program_id(0)
    last_e = pl.num_programs(0) - 1
    rows = MOE_ROWS
    n_total = xs_ref.shape[0] // rows
    used = first_ref[last_e] + nblk_ref[last_e]

    def fetch(g, slot):
        return pltpu.make_async_copy(xs_ref.at[pl.ds(g * rows, rows)], xbuf.at[pl.ds(slot * rows, rows)],
                                     in_sems.at[slot])

    def flush(g, slot):
        return pltpu.make_async_copy(ybuf.at[pl.ds(slot * rows, rows)], ys_ref.at[pl.ds(g * rows, rows)],
                                     out_sems.at[slot])

    @pl.when((e == 0) & (used > 0))
    def _():
        fetch(0, 0).start()

    @pl.when(nblk_ref[e] > 0)
    def _():
        wg_s[...] = wg_ref[...].astype(bf16)
        wu_s[...] = wu_ref[...].astype(bf16)
        wd_s[...] = wd_ref[...].astype(bf16)

    def block(b, carry):
        g = first_ref[e] + b
        slot = g % 2
        fetch(g, slot).wait()

        @pl.when(g + 1 < used)
        def _():
            fetch(g + 1, 1 - slot).start()

        @pl.when(g >= 2)
        def _():
            flush(g - 2, slot).wait()
        half = rows // 2
        starts = [slot * rows + r * half for r in range(2)]
        gated = []
        for row0 in starts:
            xb = _unpack_bf16_pairs(_load_row_tiles(xbuf, row0, half)).astype(bf16)
            gated.append((jnp.dot(xb, wg_s[...], preferred_element_type=f32),
                          jnp.dot(xb, wu_s[...], preferred_element_type=f32)))
        for row0, (a, u) in zip(starts, gated):
            h = (a * jax.nn.sigmoid(a) * u).astype(bf16)
            _store_row_tiles(ybuf, _pack_bf16_pairs(jnp.dot(h, wd_s[...], preferred_element_type=f32)), row0)
        flush(g, slot).start()
        return carry
    lax.fori_loop(0, nblk_ref[e], block, 0)

    @pl.when(e == last_e)
    def _():
        for back in (1, 2):
            @pl.when(used >= back)
            def _(back=back):
                flush(used - back, (used - back) % 2).wait()
        ybuf[pl.ds(0, rows)] = jnp.zeros((rows,) + ybuf.shape[1:], ybuf.dtype)
        for wait in (False, True):
            def zero_tail(g, c, wait=wait):
                cp = flush(g, 0)
                if wait:
                    cp.wait()
                else:
                    cp.start()
                return c
            lax.fori_loop(used, n_total, zero_tail, 0)


def _experts(first_block, n_blocks, xs, w_gate, w_up, w_down):
    _, D, De = w_gate.shape
    widx = lambda e, first, nblk: (e, 0, 0)
    buf = pltpu.VMEM(_row_tiles_shape(2 * MOE_ROWS, D), jnp.uint32)
    return pl.pallas_call(
        _experts_kernel,
        grid_spec=pltpu.PrefetchScalarGridSpec(
            num_scalar_prefetch=2,
            grid=(N_EXPERTS,),
            in_specs=[pl.BlockSpec(memory_space=pl.ANY),
                      pl.BlockSpec((None, D, De), widx),
                      pl.BlockSpec((None, D, De), widx),
                      pl.BlockSpec((None, De, D), widx)],
            out_specs=pl.BlockSpec(memory_space=pl.ANY),
            scratch_shapes=[pltpu.VMEM((D, De), bf16), pltpu.VMEM((D, De), bf16), pltpu.VMEM((De, D), bf16),
                            buf, buf, pltpu.SemaphoreType.DMA((2,)), pltpu.SemaphoreType.DMA((2,))],
        ),
        out_shape=jax.ShapeDtypeStruct(xs.shape, jnp.uint32),
        compiler_params=_cparams(("arbitrary",)),
        name="moe_experts",
    )(first_block, n_blocks, xs, w_gate, w_up, w_down)


def _combine_kernel(tinfo_ref, tnext_ref, cnt_ref, h2_ref, vec_ref, g_ref, ys_ref, o_ref,
                    gbufs, stages, start_ref, sems):
    tj = vec_ref.shape[0] // TILES_PER_STEP
    step = pl.program_id(0)

    def fetch(info_ref, t, buf):
        _tile_segments(info_ref, t, start_ref, gbufs[buf], False, ys_ref, sems.at[buf], tj)

    @pl.when(step == 0)
    def _():
        _pad_starts(cnt_ref, start_ref)
        for t in range(GATHER_AHEAD):
            fetch(tinfo_ref, t, t)

    slot = lax.broadcasted_iota(jnp.int32, (tj, 2 * tj), 1)
    for k in range(TILES_PER_STEP):
        pltpu.make_async_copy(ys_ref.at[pl.ds(0, 2 * tj)], gbufs[k], sems.at[k]).wait()
        ahead = k + GATHER_AHEAD
        if ahead < TILES_PER_STEP:
            fetch(tinfo_ref, ahead, ahead)
        else:
            fetch(tnext_ref, ahead - TILES_PER_STEP, ahead - TILES_PER_STEP)
        rows = _unpack_bf16_pairs(_load_row_tiles(gbufs[k])).astype(bf16)
        v = vec_ref[k * tj:(k + 1) * tj, :]
        y = h2_ref[k * tj:(k + 1) * tj, :]
        for c in range(2):
            pick = (slot == v[:, 2 + c:3 + c].astype(jnp.int32)).astype(bf16)
            y = y + v[:, c:c + 1] * jnp.dot(pick, rows, preferred_element_type=f32)
        out = _rms(y, g_ref[...])
        stage = stages[k // 2]
        for q in range(stage.shape[0]):
            stage[q, pl.ds(k % 2, tj, stride=2), :] = out[:, q * LANES:(q + 1) * LANES]
        if k % 2 == 1:
            r0 = (k // 2) * 2 * tj
            for q in range(stage.shape[0]):
                o_ref[0, r0:r0 + 2 * tj, q * LANES:(q + 1) * LANES] = stage[q]


def _combine(tinfo, cnt, h2, vec, final_g, ys, B, L):
    T, D = h2.shape
    ntiles = tinfo.shape[0]
    tj = T // ntiles
    nt = TILES_PER_STEP
    assert GATHER_AHEAD < nt and nt % 2 == 0 and (L // tj) % nt == 0
    nsteps = ntiles // nt
    per_seq = L // (nt * tj)
    g = final_g.reshape(1, D)
    gbuf = pltpu.VMEM(_row_tiles_shape(2 * tj, D), jnp.uint32)
    return pl.pallas_call(
        _combine_kernel,
        grid=(nsteps,),
        in_specs=[pl.BlockSpec((nt, SUBLANES, LANES), lambda i: (i, 0, 0), memory_space=pltpu.SMEM),
                  pl.BlockSpec((nt, SUBLANES, LANES), lambda i: (i + 1, 0, 0), memory_space=pltpu.SMEM),
                  pl.BlockSpec(memory_space=pltpu.SMEM),
                  pl.BlockSpec((nt * tj, D), lambda i: (i, 0)),
                  pl.BlockSpec((nt * tj, LANES), lambda i: (i, 0)),
                  pl.BlockSpec((1, D), lambda i: (0, 0)),
                  pl.BlockSpec(memory_space=pl.ANY)],
        out_specs=pl.BlockSpec((1, nt * tj, D), lambda i: (i // per_seq, i % per_seq, 0)),
        out_shape=jax.ShapeDtypeStruct((B, L, D), f32),
        scratch_shapes=[[gbuf] * nt,
                        [pltpu.VMEM((D // LANES, 2 * tj, LANES), f32)] * (nt // 2),
                        pltpu.SMEM((N_EXPERTS,), jnp.int32), pltpu.SemaphoreType.DMA((nt,))],
        compiler_params=_cparams(("arbitrary",)),
        name="moe_combine",
    )(tinfo, jnp.pad(tinfo, ((0, nt), (0, 0), (0, 0))), cnt, h2, vec, g, ys)


def kernel(x, norm1_g, w_in, conv_w, conv_b, f_w_in, f_b_in, f_w_mid, f_b_mid, f_freq, f_w_out, f_bias,
           mix_g, w_out, norm2_g, w_group, b_group, w_router, b_router, w_gate, w_up, w_down, final_g):
    B, L, D = x.shape
    depth = norm1_g.shape[0]
    assert depth == 1, "the final RMSNorm is fused into the single layer's MoE combine"
    C = D // 2
    Lh = L // 2
    N = B * C
    T = B * L
    n_rows = T * 2 + N_EXPERTS * MOE_ROWS
    cfwd, cinv = _dft_consts(L)
    cfou = _fourier_consts(L)
    i = 0
    wcat = _prep_win(w_in[i], C)
    hcat, asum = _hyena_filters(L, C, f_w_in[i], f_b_in[i], f_w_mid[i], f_b_mid[i], f_freq[i], f_w_out[i])
    kspec, k0 = _filter_spectrum(cfwd, hcat, asum, L, C)
    z, x0, u4 = _inproj(x, norm1_g[i], wcat, conv_w[i], conv_b[i], C)
    z = z.reshape(2, Lh, N)
    x0 = x0.reshape(2, Lh, N)
    u4 = u4.reshape(4, Lh, N)
    s = _conv_fwd(cfwd, z, kspec, k0, C)
    yh = _conv_inv(cinv, s, x0, z, f_bias[i], C)
    yf = _fourier_dft(cfou, u4, L, C)
    wr = jnp.concatenate([jnp.transpose(w_router[i], (1, 0, 2)).reshape(D, N_EXPERTS), w_group[i]], axis=1)
    wr = jnp.pad(wr, ((0, 0), (0, LANES - wr.shape[1])))
    br = jnp.pad(jnp.concatenate([b_router[i].reshape(-1), b_group[i]]), (0, LANES - N_EXPERTS - N_GROUPS))
    h2, tok, pos, vec, tinfo, cnt = _outproj_router(yh, yf, x, mix_g[i], w_out[i].astype(bf16), norm2_g[i],
                                                     wr, br.reshape(1, LANES), C)
    xs, first_block, n_blocks = _dispatch(tinfo, cnt, tok, pos, n_rows)
    ys = _experts(first_block, n_blocks, xs, w_gate[i], w_up[i], w_down[i])
    return _combine(tinfo, cnt, h2, vec, final_g, ys, B, L)
```

```python
import functools
import math

import jax
import jax.numpy as jnp
from jax import lax
from jax.experimental import pallas as pl
from jax.experimental.pallas import tpu as pltpu

HEAD_DIM = 64
N_GROUPS = 4
EXPERTS_PER_GROUP = 8
N_EXPERTS = N_GROUPS * EXPERTS_PER_GROUP
FILTER_BANDS = 16
DECAY_FAST_PCT = 0.3
DECAY_SLOW_PCT = 1.5
DECAY_TARGET = 1e-2
EPS = 1e-6

LANES = 128
SUBLANES = 8
MXU_DIM = 256
CONV_ROWS = 512
CONV_COLS = 512
FOURIER_COLS = 256
MOE_ROWS = 512
ROW_TILES_PER_STEP = 2
TILES_PER_STEP = 4
GATHER_AHEAD = 2
SEG_SPLIT = 32
VMEM_LIMIT = 56 * 1024 * 1024

f32 = jnp.float32
bf16 = jnp.bfloat16
HI = lax.Precision.HIGHEST


def _cparams(sem):
    return pltpu.CompilerParams(dimension_semantics=sem, vmem_limit_bytes=VMEM_LIMIT)


def _tile(pref, n):
    return min(pref, n)


_SIN_TERMS = tuple((-1.0) ** k / math.factorial(2 * k + 1) for k in range(7))
_COS_TERMS = tuple((-1.0) ** k / math.factorial(2 * k) for k in range(8))


def _cos_sin_of_turn_fraction(prod, m):
    num = prod & (m - 1)
    quadrant = num // (m // 4)
    x = (num & (m // 4 - 1)).astype(f32) * (2.0 * math.pi / m)
    x2 = x * x
    s = jnp.full_like(x, _SIN_TERMS[-1])
    for c in _SIN_TERMS[-2::-1]:
        s = s * x2 + c
    s = s * x
    c_ = jnp.full_like(x, _COS_TERMS[-1])
    for c in _COS_TERMS[-2::-1]:
        c_ = c_ * x2 + c
    odd = (quadrant & 1) == 1
    cos = jnp.where(odd, s, c_)
    sin = jnp.where(odd, c_, s)
    cos = jnp.where((quadrant == 1) | (quadrant == 2), -cos, cos)
    sin = jnp.where(quadrant >= 2, -sin, sin)
    return cos, sin


def _cos_sin_blocks(a, s, b0, m, ncb, d):
    tc, ts = _cos_sin_of_turn_fraction(a * (s * d), m)
    b0_lane = jnp.zeros(d.shape, jnp.int32)
    for cb in range(ncb):
        b0_lane = jnp.where(d == cb, b0(cb), b0_lane)
    ac, asn = _cos_sin_of_turn_fraction(a * b0_lane, m)
    for cb in range(ncb):
        ca = ac[:, cb:cb + 1]
        sa = asn[:, cb:cb + 1]
        yield cb, ca * tc - sa * ts, sa * tc + ca * ts


def _dft_consts_kernel(fwd_ref, inv_ref, *, L):
    Lh = L // 2
    tr = fwd_ref.shape[1]
    r = lax.broadcasted_iota(jnp.int32, (tr, LANES), 0) + pl.program_id(0) * tr
    d = lax.broadcasted_iota(jnp.int32, (tr, LANES), 1)
    alt_r = (1 - 2 * (r & 1)).astype(f32)
    alt_d = (1 - 2 * (d & 1)).astype(f32)
    ncb = Lh // LANES
    fams = [
        (r, 1, lambda cb: cb * LANES, L),
        (r, 2, lambda cb: 2 * cb * LANES + 1, 2 * L),
        (2 * r + 1, 1, lambda cb: cb * LANES, 2 * L),
    ]
    for fi, (a, s, b0, m) in enumerate(fams):
        for cb, c, sn in _cos_sin_blocks(a, s, b0, m, ncb, d):
            sl = slice(cb * LANES, (cb + 1) * LANES)
            if fi == 0:
                fwd_ref[0, :, sl] = c.astype(bf16)
                inv_ref[0, :, sl] = c.astype(bf16)
                fwd_ref[1, :, sl] = jnp.where(r == 0, alt_d, sn).astype(bf16)
                nst = -sn
                if cb == 0:
                    nst = jnp.where(d == 0, -alt_r, nst)
                inv_ref[1, :, sl] = nst.astype(bf16)
            elif fi == 1:
                fwd_ref[2, :, sl] = c.astype(bf16)
                fwd_ref[3, :, sl] = jnp.where(r == 0, alt_d, sn).astype(bf16)
            else:
                inv_ref[2, :, sl] = c.astype(bf16)
                nst = -sn
                if cb == 0:
                    nst = jnp.where(d == 0, -alt_r, nst)
                inv_ref[3, :, sl] = nst.astype(bf16)


def _dft_consts(L):
    Lh = L // 2
    tr = _tile(256, Lh)
    shp = jax.ShapeDtypeStruct((4, Lh, Lh), bf16)
    spec = pl.BlockSpec((4, tr, Lh), lambda i: (0, i, 0))
    return pl.pallas_call(
        functools.partial(_dft_consts_kernel, L=L),
        grid=(Lh // tr,),
        out_specs=[spec, spec],
        out_shape=[shp, shp],
        compiler_params=_cparams(("parallel",)),
        name="dft_consts",
    )()


def _fourier_consts_kernel(o_ref, *, L):
    n = L // 4
    tr = o_ref.shape[0]
    r = lax.broadcasted_iota(jnp.int32, (tr, LANES), 0) + pl.program_id(0) * tr
    d = lax.broadcasted_iota(jnp.int32, (tr, LANES), 1)

    def col_l(cb):
        parity, j0 = divmod(cb * LANES, n // 2)
        return 2 * j0 + parity

    for cb, c, sn in _cos_sin_blocks(r, 2, col_l, n, n // LANES, d):
        o_ref[:, cb * LANES:(cb + 1) * LANES] = c.astype(bf16)
        o_ref[:, n + cb * LANES:n + (cb + 1) * LANES] = (-sn).astype(bf16)


def _fourier_consts(L):
    n = L // 4
    tr = _tile(256, n)
    return pl.pallas_call(
        functools.partial(_fourier_consts_kernel, L=L),
        grid=(n // tr,),
        out_specs=pl.BlockSpec((tr, 2 * n), lambda i: (i, 0)),
        out_shape=jax.ShapeDtypeStruct((n, 2 * n), bf16),
        compiler_params=_cparams(("parallel",)),
        name="fourier_consts",
    )()


def _prep_win_kernel(w_ref, o_ref, *, C):
    w = w_ref[...]
    i = lax.broadcasted_iota(jnp.int32, (LANES, LANES), 0)
    j = lax.broadcasted_iota(jnp.int32, (LANES, LANES), 1)
    same = (i // HEAD_DIM) == (j // HEAD_DIM)
    cos, sin = _cos_sin_of_turn_fraction((i % HEAD_DIM) * (j % HEAD_DIM), HEAD_DIM)
    bc = jnp.where(same, cos, 0.0)
    bs = jnp.where(same, sin, 0.0)
    o_ref[:, :3 * C] = w[:, :3 * C].astype(bf16)
    for k in range(0, C, LANES):
        wf = w[:, 3 * C + k:3 * C + k + LANES]
        o_ref[:, 3 * C + k:3 * C + k + LANES] = jnp.dot(wf, bc, precision=HI, preferred_element_type=f32).astype(bf16)
        o_ref[:, 4 * C + k:4 * C + k + LANES] = jnp.dot(wf, bs, precision=HI, preferred_element_type=f32).astype(bf16)


def _prep_win(w_in, C):
    D = w_in.shape[0]
    tr = _tile(256, D)
    return pl.pallas_call(
        functools.partial(_prep_win_kernel, C=C),
        grid=(D // tr,),
        in_specs=[pl.BlockSpec((tr, 4 * C), lambda i: (i, 0))],
        out_specs=pl.BlockSpec((tr, 5 * C), lambda i: (i, 0)),
        out_shape=jax.ShapeDtypeStruct((D, 5 * C), bf16),
        compiler_params=_cparams(("parallel",)),
        name="prep_win",
    )(w_in)


def _filter_kernel(win_ref, bin_ref, wmid_ref, bmid_ref, freq_ref, wout_ref, h_ref, asum_ref, *, L, C):
    tr = h_ref.shape[1]
    step = pl.program_id(0)
    lane = lax.broadcasted_iota(jnp.int32, (tr, LANES), 1)
    row = lax.broadcasted_iota(jnp.int32, (tr, LANES), 0) + step * tr
    band_i = jnp.where(lane <= FILTER_BANDS, lane - 1, lane - 1 - FILTER_BANDS)
    band = 1e-4 + band_i.astype(f32) * ((FILTER_BANDS - 1 - 1e-4) / (FILTER_BANDS - 1))
    ch = lax.broadcasted_iota(jnp.int32, (tr, 2 * C), 1) % C
    max_decay = math.log(DECAY_TARGET) / DECAY_FAST_PCT
    min_decay = math.log(DECAY_TARGET) / DECAY_SLOW_PCT
    delta = jnp.abs(min_decay + ch.astype(f32) * ((max_decay - min_decay) / (C - 1)))
    is_bwd = lax.broadcasted_iota(jnp.int32, (tr, 2 * C), 1) >= C
    row_c = lax.broadcasted_iota(jnp.int32, (tr, 2 * C), 0) + step * tr

    @pl.when(step == 0)
    def _():
        asum_ref[...] = jnp.zeros_like(asum_ref)

    total = jnp.zeros((1, 2 * C), f32)
    for p in range(2):
        pos = (2 * row + p).astype(f32)
        t = pos * (1.0 / (L - 1))
        w = pos * (2.0 * math.pi / L)
        fw = band * w
        z = jnp.where(lane == 0, t,
                      jnp.where(lane <= FILTER_BANDS, jnp.cos(fw),
                                jnp.where(lane <= 2 * FILTER_BANDS, -jnp.sin(fw), 0.0)))
        h = jnp.sin(freq_ref[0:1, :] * (jnp.dot(z, win_ref[...], precision=HI, preferred_element_type=f32)
                                       + bin_ref[...]))
        for i in range(wmid_ref.shape[0]):
            h = jnp.sin(freq_ref[i + 1:i + 2, :]
                        * (jnp.dot(h, wmid_ref[i], precision=HI, preferred_element_type=f32)
                           + bmid_ref[i:i + 1, :]))
        out = jnp.dot(h, wout_ref[...], precision=HI, preferred_element_type=f32)
        tc = (2 * row_c + p).astype(f32) * (1.0 / (L - 1))
        out = out * jnp.exp(-tc * delta)
        if p == 0:
            out = jnp.where(is_bwd & (row_c == 0), 0.0, out)
        h_ref[p] = out.astype(bf16)
        total = total + jnp.sum(jnp.abs(out), axis=0, keepdims=True)
    asum_ref[0:1, :] += total


def _hyena_filters(L, C, f_w_in, f_b_in, f_w_mid, f_b_mid, f_freq, f_w_out):
    Lh = L // 2
    tr = _tile(256, Lh)
    order = f_w_in.shape[1]
    win = jnp.pad(f_w_in, ((0, LANES - f_w_in.shape[0]), (0, 0)))
    full = lambda a: pl.BlockSpec(a.shape, lambda i: (0,) * a.ndim)
    args = (win, f_b_in.reshape(1, order), f_w_mid, f_b_mid, f_freq, f_w_out)
    return pl.pallas_call(
        functools.partial(_filter_kernel, L=L, C=C),
        grid=(Lh // tr,),
        in_specs=[full(a) for a in args],
        out_specs=[pl.BlockSpec((2, tr, 2 * C), lambda i: (0, i, 0)),
                   pl.BlockSpec((SUBLANES, 2 * C), lambda i: (0, 0))],
        out_shape=[jax.ShapeDtypeStruct((2, Lh, 2 * C), bf16),
                   jax.ShapeDtypeStruct((SUBLANES, 2 * C), f32)],
        compiler_params=_cparams(("arbitrary",)),
        name="hyena_filters",
    )(*args)


def _fwd_products(c_ref, x_ref):
    tm, k = c_ref.shape[1:]
    even = jnp.dot(c_ref[0:2].reshape(2 * tm, k), x_ref[0], preferred_element_type=f32)
    odd = jnp.dot(c_ref[2:4].reshape(2 * tm, k), x_ref[1], preferred_element_type=f32)
    return even[:tm], even[tm:], odd[:tm], odd[tm:]


def _filter_spectrum_kernel(c_ref, h_ref, asum_ref, k_ref, k0_ref, *, L, C):
    ce, se, co, so = _fwd_products(c_ref, h_ref)
    a = asum_ref[0:1, :]
    scale = 1.0 / ((a[:, :C] + a[:, C:]) * L)
    F = lambda v: v[:, :C]
    G = lambda v: v[:, C:]
    k_ref[0] = (F(ce) + F(co) + G(ce) + G(co)) * scale
    k_ref[1] = (-(F(se) + F(so)) + (G(se) + G(so))) * scale
    k_ref[2] = (F(ce) - F(co) + G(ce) - G(co)) * scale
    k_ref[3] = ((F(se) - F(so)) - (G(se) - G(so))) * scale

    @pl.when(pl.program_id(0) == 0)
    def _():
        k0_ref[...] = jnp.zeros_like(k0_ref)
        k0_ref[0:1, :] = ((F(se) + G(se)) * scale)[0:1, :]
        k0_ref[1:2, :] = ((F(so) - G(so)) * scale)[0:1, :]


def _filter_spectrum(cfwd, hcat, asum, L, C):
    Lh = L // 2
    tm = _tile(256, Lh)
    return pl.pallas_call(
        functools.partial(_filter_spectrum_kernel, L=L, C=C),
        grid=(Lh // tm,),
        in_specs=[pl.BlockSpec((4, tm, Lh), lambda i: (0, i, 0)),
                  pl.BlockSpec((2, Lh, 2 * C), lambda i: (0, 0, 0)),
                  pl.BlockSpec((SUBLANES, 2 * C), lambda i: (0, 0))],
        out_specs=[pl.BlockSpec((4, tm, C), lambda i: (0, i, 0)),
                   pl.BlockSpec((SUBLANES, C), lambda i: (0, 0))],
        out_shape=[jax.ShapeDtypeStruct((4, Lh, C), f32),
                   jax.ShapeDtypeStruct((SUBLANES, C), f32)],
        compiler_params=_cparams(("arbitrary",)),
        name="filter_spectrum",
    )(cfwd, hcat, asum)


def _conv_fwd_kernel(c_ref, z_ref, k_ref, k0_ref, s_ref):
    products = _fwd_products(c_ref, z_ref)
    C = k_ref.shape[2]
    first = pl.program_id(1) == 0
    for h in range(z_ref.shape[2] // C):
        cols = slice(h * C, (h + 1) * C)
        ce, se, co, so = (v[:, cols] for v in products)
        zsr, zsi, zdr, zdi = ce + co, -(se + so), ce - co, se - so
        ksr, ksi, kdr, kdi = k_ref[0], k_ref[1], k_ref[2], k_ref[3]
        psr = zsr * ksr - zsi * ksi
        psi = zsr * ksi + zsi * ksr
        pdr = zdr * kdr - zdi * kdi
        pdi = zdr * kdi + zdi * kdr
        gen = (psr + pdr, psi - pdi, psr - pdr, psi + pdi)

        @pl.when(first)
        def _(cols=cols, ce=ce, se=se, so=so, zsr=zsr, zdr=zdr, ksr=ksr, kdr=kdr, gen=gen):
            row0 = lax.broadcasted_iota(jnp.int32, ce.shape, 0) == 0
            p0 = zsr * ksr
            pl_ = zdr * kdr
            ak, bk = k0_ref[0:1, :], k0_ref[1:2, :]
            phr = se * ak - so * bk
            phi = -(se * bk + so * ak)
            spec = (0.5 * (p0 + pl_), -phr, 0.5 * (p0 - pl_), phi)
            for i in range(4):
                s_ref[i, :, cols] = jnp.where(row0, spec[i], gen[i]).astype(bf16)

        @pl.when(jnp.logical_not(first))
        def _(cols=cols, gen=gen):
            for i in range(4):
                s_ref[i, :, cols] = gen[i].astype(bf16)


def _conv_fwd(cfwd, z, kspec, k0, C):
    _, Lh, N = z.shape
    tm = _tile(CONV_ROWS, Lh)
    tn = _tile(CONV_COLS, N)
    return pl.pallas_call(
        _conv_fwd_kernel,
        grid=(N // tn, Lh // tm),
        in_specs=[pl.BlockSpec((4, tm, Lh), lambda n, i: (0, i, 0)),
                  pl.BlockSpec((2, Lh, tn), lambda n, i: (0, 0, n)),
                  pl.BlockSpec((4, tm, C), lambda n, i: (0, i, 0)),
                  pl.BlockSpec((SUBLANES, C), lambda n, i: (0, 0))],
        out_specs=pl.BlockSpec((4, tm, tn), lambda n, i: (0, i, n)),
        out_shape=jax.ShapeDtypeStruct((4, Lh, N), bf16),
        compiler_params=_cparams(("parallel", "arbitrary")),
        name="conv_fwd",
    )(cfwd, z, kspec, k0)


def _pair_products(c_ref, r_ref):
    ev = (jnp.dot(c_ref[0], r_ref[0], preferred_element_type=f32)
          + jnp.dot(c_ref[1], r_ref[1], preferred_element_type=f32))
    od = (jnp.dot(c_ref[2], r_ref[2], preferred_element_type=f32)
          + jnp.dot(c_ref[3], r_ref[3], preferred_element_type=f32))
    return ev, od


def _conv_inv_kernel(c_ref, s_ref, x0_ref, z_ref, fb_ref, y_ref):
    conv = _pair_products(c_ref, s_ref)
    C = fb_ref.shape[1]
    fb = fb_ref[...]
    for p in range(2):
        for h in range(z_ref.shape[2] // C):
            cols = slice(h * C, (h + 1) * C)
            zp = z_ref[p, :, cols].astype(f32)
            y_ref[p, :, cols] = (x0_ref[p, :, cols].astype(f32) * (conv[p][:, cols] + zp * fb)).astype(bf16)


def _conv_inv(cinv, s, x0, z, f_bias, C):
    _, Lh, N = s.shape
    tm = _tile(CONV_ROWS, Lh)
    tn = _tile(CONV_COLS, N)
    return pl.pallas_call(
        _conv_inv_kernel,
        grid=(N // tn, Lh // tm),
        in_specs=[pl.BlockSpec((4, tm, Lh), lambda n, i: (0, i, 0)),
                  pl.BlockSpec((4, Lh, tn), lambda n, i: (0, 0, n)),
                  pl.BlockSpec((2, tm, tn), lambda n, i: (0, i, n)),
                  pl.BlockSpec((2, tm, tn), lambda n, i: (0, i, n)),
                  pl.BlockSpec((1, C), lambda n, i: (0, 0))],
        out_specs=pl.BlockSpec((2, tm, tn), lambda n, i: (0, i, n)),
        out_shape=jax.ShapeDtypeStruct((2, Lh, N), bf16),
        compiler_params=_cparams(("parallel", "arbitrary")),
        name="conv_inv",
    )(cinv, s, x0, z, f_bias.reshape(1, C))


def _fourier_kernel(c_ref, u_ref, y_ref, rhs, stage, tw1, tw2, *, L, scale):
    Lh, n, H = L // 2, L // 4, L // 8
    tn = u_ref.shape[2]

    @pl.when(pl.program_id(0) == 0)
    def _():
        r1 = lax.broadcasted_iota(jnp.int32, (Lh, LANES), 0)
        tw1[0], tw1[1] = _cos_sin_of_turn_fraction(2 * (r1 % n) + r1 // n, L)
        r2 = lax.broadcasted_iota(jnp.int32, (n, LANES), 0)
        tw2[0], tw2[1] = _cos_sin_of_turn_fraction(2 * (r2 % H) + r2 // H, Lh)

    def rotate(a, b, c, s):
        return c * a - s * b, s * a + c * b

    for q in range(tn // LANES):
        cols = slice(q * LANES, (q + 1) * LANES)
        for p in range(2):
            first = slice(p * n, p * n + H)
            second = slice(p * n + H, (p + 1) * n)
            out = slice(p * H, (p + 1) * H)
            out_b = slice(n + p * H, n + (p + 1) * H)
            c2, s2 = tw2[0, out, :], tw2[1, out, :]

            def emit(sub, a0, b0, a1, b1):
                rhs[sub, out, cols] = (a0 + a1).astype(bf16)
                rhs[sub, out_b, cols] = (b0 + b1).astype(bf16)
                da, db = rotate(a0 - a1, b0 - b1, c2, s2)
                rhs[sub + 1, out, cols] = da.astype(bf16)
                rhs[sub + 1, out_b, cols] = db.astype(bf16)

            ld = lambda k, rows: u_ref[k, rows, cols].astype(f32)
            emit(0, ld(0, first), ld(1, first), ld(0, second), ld(1, second))
            odd = [rotate(ld(2, rows), ld(3, rows), tw1[0, rows, :], tw1[1, rows, :]) for rows in (first, second)]
            emit(2, odd[0][0], odd[0][1], odd[1][0], odd[1][1])

    for parity in range(2):
        pair = [jnp.dot(c_ref[...], rhs[2 * parity + h], preferred_element_type=f32) * scale for h in range(2)]
        for q in range(tn // LANES):
            cols = slice(q * LANES, (q + 1) * LANES)
            for h in range(2):
                stage[q, pl.ds(h, n, stride=2), :] = pair[h][:, cols]
            y_ref[parity, :, cols] = stage[q].astype(bf16)


def _fourier_dft(cfou, u4, L, C):
    _, Lh, N = u4.shape
    n = L // 4
    tn = _tile(FOURIER_COLS, C)
    return pl.pallas_call(
        functools.partial(_fourier_kernel, L=L, scale=1.0 / math.sqrt(L * HEAD_DIM)),
        grid=(N // tn,),
        in_specs=[pl.BlockSpec((n, 2 * n), lambda j: (0, 0)),
                  pl.BlockSpec((4, Lh, tn), lambda j: (0, 0, j))],
        out_specs=pl.BlockSpec((2, Lh, tn), lambda j: (0, 0, j)),
        out_shape=jax.ShapeDtypeStruct((2, Lh, N), bf16),
        scratch_shapes=[pltpu.VMEM((4, 2 * n, tn), bf16), pltpu.VMEM((tn // LANES, Lh, LANES), f32),
                        pltpu.VMEM((2, Lh, LANES), f32), pltpu.VMEM((2, n, LANES), f32)],
        compiler_params=_cparams(("arbitrary",)),
        name="fourier_dft",
    )(cfou, u4)


def _rms(x, g):
    return x * lax.rsqrt(jnp.mean(x * x, axis=-1, keepdims=True) + EPS) * g


def _stage_rows(chunks, x):
    for k in range(chunks.shape[0]):
        chunks[k] = x[:, k * LANES:(k + 1) * LANES]


def _parity_rows(chunks, p, n):
    return jnp.concatenate([chunks[k, pl.ds(p, n, stride=2), :] for k in range(chunks.shape[0])], axis=1)


def _inproj_kernel(xa_ref, xb_ref, pa_ref, na_ref, pb_ref, nb_ref, g_ref, w_ref, cw_ref, cb_ref,
                   z_ref, x0_ref, u_ref, stage, *, C, D):
    tj = xa_ref.shape[1] // 2
    jt = pl.program_id(1)
    n_half = pl.num_programs(1)
    g = g_ref[...]
    w = w_ref[...]
    cw0, cw1, cw2 = cw_ref[0:1, :], cw_ref[1:2, :], cw_ref[2:3, :]
    cb = cb_ref[...]
    row = lax.broadcasted_iota(jnp.int32, (tj, 3 * C), 0)
    four = []
    halves = ((xa_ref, pa_ref, na_ref, jt == 0, False),
              (xb_ref, pb_ref, nb_ref, False, jt == n_half - 1))
    lhs = []
    for x_ref, p_ref, n_ref, _, _ in halves:
        _stage_rows(stage, x_ref[0])
        lhs += [_rms(_parity_rows(stage, 0, tj), g).astype(bf16),
                _rms(_parity_rows(stage, 1, tj), g).astype(bf16),
                _rms(p_ref[0], g).astype(bf16),
                _rms(n_ref[0], g).astype(bf16)]
    both = jnp.dot(jnp.concatenate(lhs, axis=0), w, preferred_element_type=f32)
    rows_per_half = 2 * tj + 2 * SUBLANES
    for h, (_, _, _, at_start, at_end) in enumerate(halves):
        res = both[h * rows_per_half:(h + 1) * rows_per_half]
        pe, po = res[:tj, :3 * C], res[tj:2 * tj, :3 * C]
        prev = res[2 * tj + SUBLANES - 1:2 * tj + SUBLANES, :3 * C]
        nxt = res[2 * tj + SUBLANES:2 * tj + SUBLANES + 1, :3 * C]
        if at_start is not False:
            prev = jnp.where(at_start, 0.0, prev)
        if at_end is not False:
            nxt = jnp.where(at_end, 0.0, nxt)
        po_dn = jnp.where(row == 0, prev, pltpu.roll(po, 1, 0))
        pe_up = jnp.where(row == tj - 1, nxt, pltpu.roll(pe, tj - 1, 0))
        uc = (cb + cw0 * po_dn + cw1 * pe + cw2 * po,
              cb + cw0 * pe + cw1 * po + cw2 * pe_up)
        for p in range(2):
            x0_ref[p, h] = uc[p][:, :C].astype(bf16)
            z_ref[p, h] = (uc[p][:, C:2 * C] * uc[p][:, 2 * C:]).astype(bf16)
        four.append((res[:tj, 3 * C:], res[tj:2 * tj, 3 * C:]))
    for p in range(2):
        fa, fb = four[0][p], four[1][p]
        u_ref[0, p] = (fa[:, :C] + fb[:, :C]).astype(bf16)
        u_ref[1, p] = (fa[:, C:] + fb[:, C:]).astype(bf16)
        u_ref[2, p] = (fa[:, :C] - fb[:, :C]).astype(bf16)
        u_ref[3, p] = (fa[:, C:] - fb[:, C:]).astype(bf16)


def _inproj(x, norm_g, wcat, conv_w, conv_b, C):
    B, L, D = x.shape
    Lq = L // 4
    tj = _tile(256, Lq)
    nq = Lq // tj
    N = B * C
    hb = 2 * tj // SUBLANES
    last = L // SUBLANES - 1
    main = lambda off: pl.BlockSpec((1, 2 * tj, D), lambda b, j: (b, j + off, 0))
    prev = lambda off: pl.BlockSpec((1, SUBLANES, D), lambda b, j: (b, jnp.maximum((j + off) * hb - 1, 0), 0))
    nxt = lambda off: pl.BlockSpec((1, SUBLANES, D), lambda b, j: (b, jnp.minimum((j + off + 1) * hb, last), 0))
    full = lambda a: pl.BlockSpec(a.shape, lambda b, j: (0,) * a.ndim)
    g = norm_g.reshape(1, D)
    cb = conv_b.reshape(1, 3 * C)
    return pl.pallas_call(
        functools.partial(_inproj_kernel, C=C, D=D),
        grid=(B, nq),
        in_specs=[main(0), main(nq), prev(0), nxt(0), prev(nq), nxt(nq),
                  full(g), full(wcat), full(conv_w), full(cb)],
        out_specs=[pl.BlockSpec((2, 2, tj, C), lambda b, j: (0, 0, j, b)),
                   pl.BlockSpec((2, 2, tj, C), lambda b, j: (0, 0, j, b)),
                   pl.BlockSpec((4, 2, tj, C), lambda b, j: (0, 0, j, b))],
        out_shape=[jax.ShapeDtypeStruct((2, 2, Lq, N), bf16),
                   jax.ShapeDtypeStruct((2, 2, Lq, N), bf16),
                   jax.ShapeDtypeStruct((4, 2, Lq, N), bf16)],
        scratch_shapes=[pltpu.VMEM((D // LANES, 2 * tj, LANES), f32)],
        compiler_params=_cparams(("parallel", "arbitrary")),
        name="inproj",
    )(x, x, x, x, x, x, g, wcat, conv_w, cb)


def _pack_bf16_pairs(v, is_bf16_exact=False):
    k = v.shape[1] // 2
    rounded = v if is_bf16_exact else v.astype(bf16).astype(f32)
    hi = pltpu.bitcast(rounded[:, :k], jnp.uint32)
    lo = pltpu.bitcast(rounded[:, k:], jnp.uint32)
    return hi | (lo >> 16)


def _unpack_bf16_pairs(u):
    hi = pltpu.bitcast(u & jnp.uint32(0xFFFF0000), f32)
    lo = pltpu.bitcast(u << 16, f32)
    return jnp.concatenate([hi, lo], axis=1)


def _row_tiles_shape(n, D):
    return (n, D // (2 * LANES), LANES)


def _store_row_tiles(ref, packed, row0=0):
    total, per_row, _ = ref.shape
    n = packed.shape[0]
    flat = ref.reshape(total * per_row, LANES)
    for s in range(per_row):
        flat[pl.ds(row0 * per_row + s, n, stride=per_row), :] = packed[:, s * LANES:(s + 1) * LANES]


def _load_row_tiles(ref, row0=0, n=None):
    total, per_row, _ = ref.shape
    n = total if n is None else n
    flat = ref.reshape(total * per_row, LANES)
    return jnp.concatenate([flat[pl.ds(row0 * per_row + s, n, stride=per_row), :] for s in range(per_row)],
                           axis=1)


def _outproj_router_kernel(yh_ref, yf_ref, x_ref, mg_ref, wo_ref, g2_ref, wr_ref, br_ref,
                           h2_ref, tok_ref, pos_ref, vec_ref, tinfo_ref, cnt_ref, carry_ref, stage, *, C):
    n_sub = pos_ref.shape[0]
    tj = pos_ref.shape[2]
    _stage_rows(stage, x_ref[0])

    @pl.when((pl.program_id(0) == 0) & (pl.program_id(1) == 0))
    def _():
        carry_ref[...] = jnp.zeros_like(carry_ref)

    li = lax.broadcasted_iota(jnp.int32, (LANES, LANES), 0)
    lj = lax.broadcasted_iota(jnp.int32, (LANES, LANES), 1)
    hw = _tile(MXU_DIM, C)
    ai = lax.broadcasted_iota(jnp.int32, (hw, hw), 0)
    aj = lax.broadcasted_iota(jnp.int32, (hw, hw), 1)
    avg = jnp.where(ai // HEAD_DIM == aj // HEAD_DIM, 1.0 / HEAD_DIM, 0.0).astype(bf16)
    upper = (li < lj).astype(bf16)
    ri = lax.broadcasted_iota(jnp.int32, (tj, tj), 0)
    ci = lax.broadcasted_iota(jnp.int32, (tj, tj), 1)
    tri = (ci < ri).astype(bf16)
    wr = wr_ref[...]
    wr_hi = wr.astype(bf16)
    wr_lo = (wr - wr_hi.astype(f32)).astype(bf16)
    wr_both = jnp.concatenate([wr_hi, wr_lo], axis=1)
    lane = lax.broadcasted_iota(jnp.int32, (n_sub * tj, LANES), 1).astype(f32)
    sub = lax.broadcasted_iota(jnp.int32, (SUBLANES, LANES), 0)
    neg = jnp.float32(-jnp.inf)
    big = jnp.float32(LANES)

    def first_argmax(vals, mask):
        v = jnp.where(mask, vals, neg)
        m = jnp.max(v, axis=1, keepdims=True)
        idx = jnp.min(jnp.where(mask & (v == m), lane, big), axis=1, keepdims=True)
        return m, idx

    normed = []
    for s in range(n_sub):
        r, p = divmod(s, 2)
        parts = []
        for ref in (yh_ref, yf_ref):
            y = ref[p, r * tj:(r + 1) * tj, :].astype(f32)
            ysq = (y * y).astype(bf16)
            ms = jnp.concatenate([jnp.dot(ysq[:, k:k + hw], avg, preferred_element_type=f32)
                                  for k in range(0, C, hw)], axis=1)
            parts.append(y * lax.rsqrt(ms + EPS))
        normed.append((jnp.concatenate(parts, axis=1) * mg_ref[...]).astype(bf16))
    proj = jnp.dot(jnp.concatenate(normed, axis=0), wo_ref[...], preferred_element_type=f32)
    t_hi, t_lo = [], []
    for s in range(n_sub):
        r, p = divmod(s, 2)
        rows = slice(s * tj, (s + 1) * tj)
        h2 = _parity_rows(stage, r * 2 * tj + p, tj) + proj[rows]
        h2_ref[rows, :] = h2
        tok = _rms(h2, g2_ref[...])
        hi = tok.astype(bf16)
        tok_ref[rows, :] = hi
        t_hi.append(hi)
        t_lo.append((tok - hi.astype(f32)).astype(bf16))
    hi_terms = jnp.dot(jnp.concatenate(t_hi, axis=0), wr_both, preferred_element_type=f32)
    logits = (hi_terms[:, :LANES] + hi_terms[:, LANES:]
              + jnp.dot(jnp.concatenate(t_lo, axis=0), wr_hi, preferred_element_type=f32) + br_ref[...])

    gmask = (lane >= N_EXPERTS) & (lane < N_EXPERTS + N_GROUPS)
    gmax, gidx = first_argmax(logits, gmask)
    gsel = gidx - N_EXPERTS
    p_group = 1.0 / jnp.sum(jnp.where(gmask, jnp.exp(logits - gmax), 0.0), axis=1, keepdims=True)
    emask = (lane >= gsel * EXPERTS_PER_GROUP) & (lane < (gsel + 1) * EXPERTS_PER_GROUP)
    l0, e0 = first_argmax(logits, emask)
    l1, e1 = first_argmax(logits, emask & (lane != e0))
    ex = jnp.exp(l1 - l0)
    w0 = p_group / (1.0 + ex)
    w1 = p_group * ex / (1.0 + ex)

    oh0 = (lane == e0).astype(f32)
    oh1 = (lane == e1).astype(f32)
    both = (oh0 + oh1).astype(bf16)
    where, counts = [], []
    for s in range(n_sub):
        tile_both = both[s * tj:(s + 1) * tj, :]
        ranks = jnp.dot(tri, tile_both, preferred_element_type=f32)
        n_e = ranks[tj - 1:tj, :] + tile_both[tj - 1:tj, :].astype(f32)
        n_e8 = jnp.broadcast_to(n_e, (SUBLANES, LANES))
        tile_off = jnp.dot(n_e8.astype(bf16), upper, preferred_element_type=f32)[0:1, :]
        where.append(ranks + tile_off)
        counts.append(n_e8)
    where = jnp.concatenate(where, axis=0)
    pos0 = jnp.sum(oh0 * where, axis=1, keepdims=True)
    pos1 = jnp.sum(oh1 * where, axis=1, keepdims=True)
    cols = jnp.where(lane == 0, w0, jnp.where(lane == 1, w1,
                     jnp.where(lane == 2, pos0, jnp.where(lane == 3, pos1, 0.0))))
    vec_ref[...] = cols
    base = carry_ref[...]
    for s in range(n_sub):
        pos_ref[s] = cols[s * tj:(s + 1) * tj, :].T[:SUBLANES, :].astype(jnp.int32)
        long_seg = (jnp.max(counts[s], axis=1, keepdims=True) >= SEG_SPLIT).astype(f32)
        tinfo_ref[s] = jnp.where(sub == 0, counts[s], jnp.where(sub == 1, base, jnp.where(sub == 2, long_seg, 0.0))
                                 ).astype(jnp.int32)
        base = base + counts[s]
    carry_ref[...] = base
    cnt_ref[...] = base.astype(jnp.int32)


def _outproj_router(yh, yf, x, mix_g, wo, norm2_g, wr, br, C):
    B, L, D = x.shape
    Lh = L // 2
    tj = _tile(256, Lh)
    rt = ROW_TILES_PER_STEP
    ns = 2 * rt
    nj = Lh // (rt * tj)
    T = B * L
    ntiles = T // tj
    lin = lambda b, j: b * nj + j
    full = lambda a: pl.BlockSpec(a.shape, lambda b, j: (0,) * a.ndim)
    mg = mix_g.reshape(1, D)
    g2 = norm2_g.reshape(1, D)
    return pl.pallas_call(
        functools.partial(_outproj_router_kernel, C=C),
        grid=(B, nj),
        in_specs=[pl.BlockSpec((2, rt * tj, C), lambda b, j: (0, j, b)),
                  pl.BlockSpec((2, rt * tj, C), lambda b, j: (0, j, b)),
                  pl.BlockSpec((1, ns * tj, D), lambda b, j: (b, j, 0)),
                  full(mg), full(wo), full(g2), full(wr), full(br)],
        out_specs=[pl.BlockSpec((ns * tj, D), lambda b, j: (lin(b, j), 0)),
                   pl.BlockSpec((ns * tj, D), lambda b, j: (lin(b, j), 0)),
                   pl.BlockSpec((ns, SUBLANES, tj), lambda b, j: (lin(b, j), 0, 0)),
                   pl.BlockSpec((ns * tj, LANES), lambda b, j: (lin(b, j), 0)),
                   pl.BlockSpec((ns, SUBLANES, LANES), lambda b, j: (lin(b, j), 0, 0)),
                   pl.BlockSpec((SUBLANES, LANES), lambda b, j: (0, 0))],
        out_shape=[jax.ShapeDtypeStruct((T, D), f32),
                   jax.ShapeDtypeStruct((T, D), bf16),
                   jax.ShapeDtypeStruct((ntiles, SUBLANES, tj), jnp.int32),
                   jax.ShapeDtypeStruct((T, LANES), f32),
                   jax.ShapeDtypeStruct((ntiles, SUBLANES, LANES), jnp.int32),
                   jax.ShapeDtypeStruct((SUBLANES, LANES), jnp.int32)],
        scratch_shapes=[pltpu.VMEM((SUBLANES, LANES), f32), pltpu.VMEM((D // LANES, ns * tj, LANES), f32)],
        compiler_params=_cparams(("arbitrary", "arbitrary")),
        name="outproj_router",
    )(yh, yf, x, mg, wo, g2, wr, br)


def _padded(c):
    return ((c + MOE_ROWS - 1) // MOE_ROWS) * MOE_ROWS


def _pad_starts(cnt_ref, start_ref):
    def body(e, acc):
        start_ref[e] = acc
        return acc + _padded(cnt_ref[0, e])
    return lax.fori_loop(0, N_EXPERTS, body, jnp.int32(0))


def _segment_copies(n, src, src_off, dst, dst_off, sem, max_rows, wait, src_advances=True, min_rows=1):
    bit = max_rows
    while bit >= min_rows:
        take = n & bit

        @pl.when(take != 0)
        def _(src_off=src_off, dst_off=dst_off, bit=bit):
            cp = pltpu.make_async_copy(src.at[pl.ds(src_off, bit)], dst.at[pl.ds(dst_off, bit)], sem)
            if wait:
                cp.wait()
            else:
                cp.start()
        if src_advances:
            src_off = src_off + take
        dst_off = dst_off + take
        bit //= 2


def _tile_segments(tinfo_ref, t, start_ref, local, local_is_src, remote, sem, tj):
    def copies(long_part):
        off = 0
        for e in range(N_EXPERTS):
            n = tinfo_ref[t, 0, e]
            far = start_ref[e] + tinfo_ref[t, 1, e]
            if long_part:
                args = (n, off, far, tj, SEG_SPLIT)
            else:
                head = n & ~(SEG_SPLIT - 1)
                args = (n, off + head, far + head, SEG_SPLIT // 2, 1)
            n_, near_, far_, hi, lo = args
            if local_is_src:
                _segment_copies(n_, local, near_, remote, far_, sem, hi, wait=False, min_rows=lo)
            else:
                _segment_copies(n_, remote, far_, local, near_, sem, hi, wait=False, min_rows=lo)
            off = off + n

    copies(long_part=False)

    @pl.when(tinfo_ref[t, 2, 0] != 0)
    def _():
        copies(long_part=True)


def _dispatch_kernel(tinfo_ref, cnt_ref, tok_ref, pos_ref, xs_ref, first_ref, nblk_ref,
                     sbufs, zbuf, start_ref, sems, zsem):
    tj = pos_ref.shape[2]
    n_blocks = xs_ref.shape[0] // MOE_ROWS

    @pl.when(pl.program_id(0) == 0)
    def _():
        total = _pad_starts(cnt_ref, start_ref)
        used = total // MOE_ROWS

        def block_ranges(e, c):
            first_ref[e] = start_ref[e] // MOE_ROWS
            nblk_ref[e] = _padded(cnt_ref[0, e]) // MOE_ROWS
            return c
        lax.fori_loop(0, N_EXPERTS, block_ranges, 0)

        zbuf[...] = jnp.zeros_like(zbuf)
        zrows = zbuf.shape[0]
        for wait in (False, True):
            def zero_pad(e, c, wait=wait):
                cnt = cnt_ref[0, e]
                _segment_copies(_padded(cnt) - cnt, zbuf, 0, xs_ref, start_ref[e] + cnt, zsem,
                                zrows, wait, src_advances=False)
                return c
            lax.fori_loop(0, N_EXPERTS, zero_pad, 0)

            def zero_tail(i, c, wait=wait):
                cp = pltpu.make_async_copy(zbuf, xs_ref.at[pl.ds(total + i * zrows, zrows)], zsem)
                if wait:
                    cp.wait()
                else:
                    cp.start()
                return c
            lax.fori_loop(0, (n_blocks - used) * (MOE_ROWS // zrows), zero_tail, 0)

    def wait_tile(k):
        pltpu.make_async_copy(sbufs[k], xs_ref.at[pl.ds(0, 2 * tj)], sems.at[k]).wait()

    slot = lax.broadcasted_iota(jnp.int32, (2 * tj, tj), 0)
    for k in range(TILES_PER_STEP):
        @pl.when(pl.program_id(0) > 0)
        def _(k=k):
            wait_tile(k)
        perm = ((slot == pos_ref[k, 2:3, :]) | (slot == pos_ref[k, 3:4, :])).astype(bf16)
        sorted_rows = jnp.dot(perm, tok_ref[k * tj:(k + 1) * tj, :], preferred_element_type=f32)
        _store_row_tiles(sbufs[k], _pack_bf16_pairs(sorted_rows, is_bf16_exact=True))
        if k > 0:
            _tile_segments(tinfo_ref, k - 1, start_ref, sbufs[k - 1], True, xs_ref, sems.at[k - 1], tj)
    last = TILES_PER_STEP - 1
    _tile_segments(tinfo_ref, last, start_ref, sbufs[last], True, xs_ref, sems.at[last], tj)

    @pl.when(pl.program_id(0) == pl.num_programs(0) - 1)
    def _():
        for k in range(TILES_PER_STEP):
            wait_tile(k)


def _dispatch(tinfo, cnt, tok, pos, n_rows):
    ntiles, _, tj = pos.shape
    T, D = tok.shape
    n_blocks = n_rows // MOE_ROWS
    nt = TILES_PER_STEP
    assert ntiles % nt == 0
    sbuf = pltpu.VMEM(_row_tiles_shape(2 * tj, D), jnp.uint32)
    return pl.pallas_call(
        _dispatch_kernel,
        grid=(ntiles // nt,),
        in_specs=[pl.BlockSpec((nt, SUBLANES, LANES), lambda i: (i, 0, 0), memory_space=pltpu.SMEM),
                  pl.BlockSpec(memory_space=pltpu.SMEM),
                  pl.BlockSpec((nt * tj, D), lambda i: (i, 0)),
                  pl.BlockSpec((nt, SUBLANES, tj), lambda i: (i, 0, 0))],
        out_specs=[pl.BlockSpec(memory_space=pl.ANY),
                   pl.BlockSpec(memory_space=pltpu.SMEM),
                   pl.BlockSpec(memory_space=pltpu.SMEM)],
        out_shape=[jax.ShapeDtypeStruct(_row_tiles_shape(n_rows, D), jnp.uint32),
                   jax.ShapeDtypeStruct((N_EXPERTS,), jnp.int32),
                   jax.ShapeDtypeStruct((N_EXPERTS,), jnp.int32)],
        scratch_shapes=[[sbuf] * nt,
                        pltpu.VMEM(_row_tiles_shape(MOE_ROWS // 2, D), jnp.uint32),
                        pltpu.SMEM((N_EXPERTS,), jnp.int32), pltpu.SemaphoreType.DMA((nt,)),
                        pltpu.SemaphoreType.DMA],
        compiler_params=_cparams(("arbitrary",)),
        name="moe_dispatch",
    )(tinfo, cnt, tok, pos)


def _experts_kernel(first_ref, nblk_ref, xs_ref, wg_ref, wu_ref, wd_ref, ys_ref,
                    wg_s, wu_s, wd_s, xbuf, ybuf, in_sems, out_sems):
    e = pl.program_id(0)
    last_e = pl.num_programs(0) - 1
    rows = MOE_ROWS
    n_total = xs_ref.shape[0] // rows
    used = first_ref[last_e] + nblk_ref[last_e]

    def fetch(g, slot):
        return pltpu.make_async_copy(xs_ref.at[pl.ds(g * rows, rows)], xbuf.at[pl.ds(slot * rows, rows)],
                                     in_sems.at[slot])

    def flush(g, slot):
        return pltpu.make_async_copy(ybuf.at[pl.ds(slot * rows, rows)], ys_ref.at[pl.ds(g * rows, rows)],
                                     out_sems.at[slot])

    @pl.when((e == 0) & (used > 0))
    def _():
        fetch(0, 0).start()

    @pl.when(nblk_ref[e] > 0)
    def _():
        wg_s[...] = wg_ref[...].astype(bf16)
        wu_s[...] = wu_ref[...].astype(bf16)
        wd_s[...] = wd_ref[...].astype(bf16)

    def block(b, carry):
        g = first_ref[e] + b
        slot = g % 2
        fetch(g, slot).wait()

        @pl.when(g + 1 < used)
        def _():
            fetch(g + 1, 1 - slot).start()

        @pl.when(g >= 2)
        def _():
            flush(g - 2, slot).wait()
        half = rows // 2
        starts = [slot * rows + r * half for r in range(2)]
        gated = []
        for row0 in starts:
            xb = _unpack_bf16_pairs(_load_row_tiles(xbuf, row0, half)).astype(bf16)
            gated.append((jnp.dot(xb, wg_s[...], preferred_element_type=f32),
                          jnp.dot(xb, wu_s[...], preferred_element_type=f32)))
        for row0, (a, u) in zip(starts, gated):
            h = (a * jax.nn.sigmoid(a) * u).astype(bf16)
            _store_row_tiles(ybuf, _pack_bf16_pairs(jnp.dot(h, wd_s[...], preferred_element_type=f32)), row0)
        flush(g, slot).start()
        return carry
    lax.fori_loop(0, nblk_ref[e], block, 0)

    @pl.when(e == last_e)
    def _():
        for back in (1, 2):
            @pl.when(used >= back)
            def _(back=back):
                flush(used - back, (used - back) % 2).wait()
        ybuf[pl.ds(0, rows)] = jnp.zeros((rows,) + ybuf.shape[1:], ybuf.dtype)
        for wait in (False, True):
            def zero_tail(g, c, wait=wait):
                cp = flush(g, 0)
                if wait:
                    cp.wait()
                else:
                    cp.start()
                return c
            lax.fori_loop(used, n_total, zero_tail, 0)


def _experts(first_block, n_blocks, xs, w_gate, w_up, w_down):
    _, D, De = w_gate.shape
    widx = lambda e, first, nblk: (e, 0, 0)
    buf = pltpu.VMEM(_row_tiles_shape(2 * MOE_ROWS, D), jnp.uint32)
    return pl.pallas_call(
        _experts_kernel,
        grid_spec=pltpu.PrefetchScalarGridSpec(
            num_scalar_prefetch=2,
            grid=(N_EXPERTS,),
            in_specs=[pl.BlockSpec(memory_space=pl.ANY),
                      pl.BlockSpec((None, D, De), widx),
                      pl.BlockSpec((None, D, De), widx),
                      pl.BlockSpec((None, De, D), widx)],
            out_specs=pl.BlockSpec(memory_space=pl.ANY),
            scratch_shapes=[pltpu.VMEM((D, De), bf16), pltpu.VMEM((D, De), bf16), pltpu.VMEM((De, D), bf16),
                            buf, buf, pltpu.SemaphoreType.DMA((2,)), pltpu.SemaphoreType.DMA((2,))],
        ),
        out_shape=jax.ShapeDtypeStruct(xs.shape, jnp.uint32),
        compiler_params=_cparams(("arbitrary",)),
        name="moe_experts",
    )(first_block, n_blocks, xs, w_gate, w_up, w_down)


def _combine_kernel(tinfo_ref, tnext_ref, cnt_ref, h2_ref, vec_ref, g_ref, ys_ref, o_ref,
                    gbufs, stages, start_ref, sems):
    tj = vec_ref.shape[0] // TILES_PER_STEP
    step = pl.program_id(0)

    def fetch(info_ref, t, buf):
        _tile_segments(info_ref, t, start_ref, gbufs[buf], False, ys_ref, sems.at[buf], tj)

    @pl.when(step == 0)
    def _():
        _pad_starts(cnt_ref, start_ref)
        for t in range(GATHER_AHEAD):
            fetch(tinfo_ref, t, t)

    slot = lax.broadcasted_iota(jnp.int32, (tj, 2 * tj), 1)
    for k in range(TILES_PER_STEP):
        pltpu.make_async_copy(ys_ref.at[pl.ds(0, 2 * tj)], gbufs[k], sems.at[k]).wait()
        ahead = k + GATHER_AHEAD
        if ahead < TILES_PER_STEP:
            fetch(tinfo_ref, ahead, ahead)
        else:
            fetch(tnext_ref, ahead - TILES_PER_STEP, ahead - TILES_PER_STEP)
        rows = _unpack_bf16_pairs(_load_row_tiles(gbufs[k])).astype(bf16)
        v = vec_ref[k * tj:(k + 1) * tj, :]
        pick = jnp.concatenate([(slot == v[:, 2 + c:3 + c].astype(jnp.int32)).astype(bf16) for c in range(2)],
                               axis=0)
        picked = jnp.dot(pick, rows, preferred_element_type=f32)
        y = h2_ref[k * tj:(k + 1) * tj, :] + v[:, 0:1] * picked[:tj] + v[:, 1:2] * picked[tj:]
        out = _rms(y, g_ref[...])
        stage = stages[k // 2]
        for q in range(stage.shape[0]):
            stage[q, pl.ds(k % 2, tj, stride=2), :] = out[:, q * LANES:(q + 1) * LANES]
        if k % 2 == 1:
            r0 = (k // 2) * 2 * tj
            for q in range(stage.shape[0]):
                o_ref[0, r0:r0 + 2 * tj, q * LANES:(q + 1) * LANES] = stage[q]


def _combine(tinfo, cnt, h2, vec, final_g, ys, B, L):
    T, D = h2.shape
    ntiles = tinfo.shape[0]
    tj = T // ntiles
    nt = TILES_PER_STEP
    assert GATHER_AHEAD < nt and nt % 2 == 0 and (L // tj) % nt == 0
    nsteps = ntiles // nt
    per_seq = L // (nt * tj)
    g = final_g.reshape(1, D)
    gbuf = pltpu.VMEM(_row_tiles_shape(2 * tj, D), jnp.uint32)
    return pl.pallas_call(
        _combine_kernel,
        grid=(nsteps,),
        in_specs=[pl.BlockSpec((nt, SUBLANES, LANES), lambda i: (i, 0, 0), memory_space=pltpu.SMEM),
                  pl.BlockSpec((nt, SUBLANES, LANES), lambda i: (i + 1, 0, 0), memory_space=pltpu.SMEM),
                  pl.BlockSpec(memory_space=pltpu.SMEM),
                  pl.BlockSpec((nt * tj, D), lambda i: (i, 0)),
                  pl.BlockSpec((nt * tj, LANES), lambda i: (i, 0)),
                  pl.BlockSpec((1, D), lambda i: (0, 0)),
                  pl.BlockSpec(memory_space=pl.ANY)],
        out_specs=pl.BlockSpec((1, nt * tj, D), lambda i: (i // per_seq, i % per_seq, 0)),
        out_shape=jax.ShapeDtypeStruct((B, L, D), f32),
        scratch_shapes=[[gbuf] * nt,
                        [pltpu.VMEM((D // LANES, 2 * tj, LANES), f32)] * (nt // 2),
                        pltpu.SMEM((N_EXPERTS,), jnp.int32), pltpu.SemaphoreType.DMA((nt,))],
        compiler_params=_cparams(("arbitrary",)),
        name="moe_combine",
    )(tinfo, jnp.pad(tinfo, ((0, nt), (0, 0), (0, 0))), cnt, h2, vec, g, ys)


def kernel(x, norm1_g, w_in, conv_w, conv_b, f_w_in, f_b_in, f_w_mid, f_b_mid, f_freq, f_w_out, f_bias,
           mix_g, w_out, norm2_g, w_group, b_group, w_router, b_router, w_gate, w_up, w_down, final_g):
    B, L, D = x.shape
    depth = norm1_g.shape[0]
    assert depth == 1, "the final RMSNorm is fused into the single layer's MoE combine"
    C = D // 2
    Lh = L // 2
    N = B * C
    T = B * L
    n_rows = T * 2 + N_EXPERTS * MOE_ROWS
    cfwd, cinv = _dft_consts(L)
    cfou = _fourier_consts(L)
    i = 0
    wcat = _prep_win(w_in[i], C)
    hcat, asum = _hyena_filters(L, C, f_w_in[i], f_b_in[i], f_w_mid[i], f_b_mid[i], f_freq[i], f_w_out[i])
    kspec, k0 = _filter_spectrum(cfwd, hcat, asum, L, C)
    z, x0, u4 = _inproj(x, norm1_g[i], wcat, conv_w[i], conv_b[i], C)
    z = z.reshape(2, Lh, N)
    x0 = x0.reshape(2, Lh, N)
    u4 = u4.reshape(4, Lh, N)
    s = _conv_fwd(cfwd, z, kspec, k0, C)
    yh = _conv_inv(cinv, s, x0, z, f_bias[i], C)
    yf = _fourier_dft(cfou, u4, L, C)
    wr = jnp.concatenate([jnp.transpose(w_router[i], (1, 0, 2)).reshape(D, N_EXPERTS), w_group[i]], axis=1)
    wr = jnp.pad(wr, ((0, 0), (0, LANES - wr.shape[1])))
    br = jnp.pad(jnp.concatenate([b_router[i].reshape(-1), b_group[i]]), (0, LANES - N_EXPERTS - N_GROUPS))
    h2, tok, pos, vec, tinfo, cnt = _outproj_router(yh, yf, x, mix_g[i], w_out[i].astype(bf16), norm2_g[i],
                                                     wr, br.reshape(1, LANES), C)
    xs, first_block, n_blocks = _dispatch(tinfo, cnt, tok, pos, n_rows)
    ys = _experts(first_block, n_blocks, xs, w_gate[i], w_up[i], w_down[i])
    return _combine(tinfo, cnt, h2, vec, final_g, ys, B, L)
```

```python
import functools
import math

import jax
import jax.numpy as jnp
from jax import lax
from jax.experimental import pallas as pl
from jax.experimental.pallas import tpu as pltpu

HEAD_DIM = 64
N_GROUPS = 4
EXPERTS_PER_GROUP = 8
N_EXPERTS = N_GROUPS * EXPERTS_PER_GROUP
FILTER_BANDS = 16
DECAY_FAST_PCT = 0.3
DECAY_SLOW_PCT = 1.5
DECAY_TARGET = 1e-2
EPS = 1e-6

LANES = 128
SUBLANES = 8
MXU_DIM = 256
CONV_ROWS = 512
CONV_COLS = 512
FOURIER_COLS = 256
MOE_ROWS = 512
EXPERT_ROW_GROUPS = 1
ROW_TILES_PER_STEP = 2
TILES_PER_STEP = 4
GATHER_AHEAD = 2
SEG_SPLIT = 32
VMEM_LIMIT = 56 * 1024 * 1024

f32 = jnp.float32
bf16 = jnp.bfloat16
HI = lax.Precision.HIGHEST


def _cparams(sem):
    return pltpu.CompilerParams(dimension_semantics=sem, vmem_limit_bytes=VMEM_LIMIT)


def _tile(pref, n):
    return min(pref, n)


_SIN_TERMS = tuple((-1.0) ** k / math.factorial(2 * k + 1) for k in range(7))
_COS_TERMS = tuple((-1.0) ** k / math.factorial(2 * k) for k in range(8))


def _cos_sin_of_turn_fraction(prod, m):
    num = prod & (m - 1)
    quadrant = num // (m // 4)
    x = (num & (m // 4 - 1)).astype(f32) * (2.0 * math.pi / m)
    x2 = x * x
    s = jnp.full_like(x, _SIN_TERMS[-1])
    for c in _SIN_TERMS[-2::-1]:
        s = s * x2 + c
    s = s * x
    c_ = jnp.full_like(x, _COS_TERMS[-1])
    for c in _COS_TERMS[-2::-1]:
        c_ = c_ * x2 + c
    odd = (quadrant & 1) == 1
    cos = jnp.where(odd, s, c_)
    sin = jnp.where(odd, c_, s)
    cos = jnp.where((quadrant == 1) | (quadrant == 2), -cos, cos)
    sin = jnp.where(quadrant >= 2, -sin, sin)
    return cos, sin


def _cos_sin_blocks(a, s, b0, m, ncb, d):
    tc, ts = _cos_sin_of_turn_fraction(a * (s * d), m)
    b0_lane = jnp.zeros(d.shape, jnp.int32)
    for cb in range(ncb):
        b0_lane = jnp.where(d == cb, b0(cb), b0_lane)
    ac, asn = _cos_sin_of_turn_fraction(a * b0_lane, m)
    for cb in range(ncb):
        ca = ac[:, cb:cb + 1]
        sa = asn[:, cb:cb + 1]
        yield cb, ca * tc - sa * ts, sa * tc + ca * ts


def _dft_consts_kernel(fwd_ref, inv_ref, *, L):
    Lh = L // 2
    tr = fwd_ref.shape[1]
    r = lax.broadcasted_iota(jnp.int32, (tr, LANES), 0) + pl.program_id(0) * tr
    d = lax.broadcasted_iota(jnp.int32, (tr, LANES), 1)
    alt_r = (1 - 2 * (r & 1)).astype(f32)
    alt_d = (1 - 2 * (d & 1)).astype(f32)
    ncb = Lh // LANES
    fams = [
        (r, 1, lambda cb: cb * LANES, L),
        (r, 2, lambda cb: 2 * cb * LANES + 1, 2 * L),
        (2 * r + 1, 1, lambda cb: cb * LANES, 2 * L),
    ]
    for fi, (a, s, b0, m) in enumerate(fams):
        for cb, c, sn in _cos_sin_blocks(a, s, b0, m, ncb, d):
            sl = slice(cb * LANES, (cb + 1) * LANES)
            if fi == 0:
                fwd_ref[0, :, sl] = c.astype(bf16)
                inv_ref[0, :, sl] = c.astype(bf16)
                fwd_ref[1, :, sl] = jnp.where(r == 0, alt_d, sn).astype(bf16)
                nst = -sn
                if cb == 0:
                    nst = jnp.where(d == 0, -alt_r, nst)
                inv_ref[1, :, sl] = nst.astype(bf16)
            elif fi == 1:
                fwd_ref[2, :, sl] = c.astype(bf16)
                fwd_ref[3, :, sl] = jnp.where(r == 0, alt_d, sn).astype(bf16)
            else:
                inv_ref[2, :, sl] = c.astype(bf16)
                nst = -sn
                if cb == 0:
                    nst = jnp.where(d == 0, -alt_r, nst)
                inv_ref[3, :, sl] = nst.astype(bf16)


def _dft_consts(L):
    Lh = L // 2
    tr = _tile(256, Lh)
    shp = jax.ShapeDtypeStruct((4, Lh, Lh), bf16)
    spec = pl.BlockSpec((4, tr, Lh), lambda i: (0, i, 0))
    return pl.pallas_call(
        functools.partial(_dft_consts_kernel, L=L),
        grid=(Lh // tr,),
        out_specs=[spec, spec],
        out_shape=[shp, shp],
        compiler_params=_cparams(("parallel",)),
        name="dft_consts",
    )()


def _fourier_consts_kernel(o_ref, *, L):
    n = L // 4
    tr = o_ref.shape[0]
    r = lax.broadcasted_iota(jnp.int32, (tr, LANES), 0) + pl.program_id(0) * tr
    d = lax.broadcasted_iota(jnp.int32, (tr, LANES), 1)

    def col_l(cb):
        parity, j0 = divmod(cb * LANES, n // 2)
        return 2 * j0 + parity

    for cb, c, sn in _cos_sin_blocks(r, 2, col_l, n, n // LANES, d):
        o_ref[:, cb * LANES:(cb + 1) * LANES] = c.astype(bf16)
        o_ref[:, n + cb * LANES:n + (cb + 1) * LANES] = (-sn).astype(bf16)


def _fourier_consts(L):
    n = L // 4
    tr = _tile(256, n)
    return pl.pallas_call(
        functools.partial(_fourier_consts_kernel, L=L),
        grid=(n // tr,),
        out_specs=pl.BlockSpec((tr, 2 * n), lambda i: (i, 0)),
        out_shape=jax.ShapeDtypeStruct((n, 2 * n), bf16),
        compiler_params=_cparams(("parallel",)),
        name="fourier_consts",
    )()


def _prep_win_kernel(w_ref, o_ref, *, C):
    w = w_ref[...]
    i = lax.broadcasted_iota(jnp.int32, (LANES, LANES), 0)
    j = lax.broadcasted_iota(jnp.int32, (LANES, LANES), 1)
    same = (i // HEAD_DIM) == (j // HEAD_DIM)
    cos, sin = _cos_sin_of_turn_fraction((i % HEAD_DIM) * (j % HEAD_DIM), HEAD_DIM)
    bc = jnp.where(same, cos, 0.0)
    bs = jnp.where(same, sin, 0.0)
    o_ref[:, :3 * C] = w[:, :3 * C].astype(bf16)
    for k in range(0, C, LANES):
        wf = w[:, 3 * C + k:3 * C + k + LANES]
        o_ref[:, 3 * C + k:3 * C + k + LANES] = jnp.dot(wf, bc, precision=HI, preferred_element_type=f32).astype(bf16)
        o_ref[:, 4 * C + k:4 * C + k + LANES] = jnp.dot(wf, bs, precision=HI, preferred_element_type=f32).astype(bf16)


def _prep_win(w_in, C):
    D = w_in.shape[0]
    tr = _tile(256, D)
    return pl.pallas_call(
        functools.partial(_prep_win_kernel, C=C),
        grid=(D // tr,),
        in_specs=[pl.BlockSpec((tr, 4 * C), lambda i: (i, 0))],
        out_specs=pl.BlockSpec((tr, 5 * C), lambda i: (i, 0)),
        out_shape=jax.ShapeDtypeStruct((D, 5 * C), bf16),
        compiler_params=_cparams(("parallel",)),
        name="prep_win",
    )(w_in)


def _filter_kernel(win_ref, bin_ref, wmid_ref, bmid_ref, freq_ref, wout_ref, h_ref, asum_ref, *, L, C):
    tr = h_ref.shape[1]
    step = pl.program_id(0)
    lane = lax.broadcasted_iota(jnp.int32, (tr, LANES), 1)
    row = lax.broadcasted_iota(jnp.int32, (tr, LANES), 0) + step * tr
    band_i = jnp.where(lane <= FILTER_BANDS, lane - 1, lane - 1 - FILTER_BANDS)
    band = 1e-4 + band_i.astype(f32) * ((FILTER_BANDS - 1 - 1e-4) / (FILTER_BANDS - 1))
    ch = lax.broadcasted_iota(jnp.int32, (tr, 2 * C), 1) % C
    max_decay = math.log(DECAY_TARGET) / DECAY_FAST_PCT
    min_decay = math.log(DECAY_TARGET) / DECAY_SLOW_PCT
    delta = jnp.abs(min_decay + ch.astype(f32) * ((max_decay - min_decay) / (C - 1)))
    is_bwd = lax.broadcasted_iota(jnp.int32, (tr, 2 * C), 1) >= C
    row_c = lax.broadcasted_iota(jnp.int32, (tr, 2 * C), 0) + step * tr

    @pl.when(step == 0)
    def _():
        asum_ref[...] = jnp.zeros_like(asum_ref)

    total = jnp.zeros((1, 2 * C), f32)
    for p in range(2):
        pos = (2 * row + p).astype(f32)
        t = pos * (1.0 / (L - 1))
        w = pos * (2.0 * math.pi / L)
        fw = band * w
        z = jnp.where(lane == 0, t,
                      jnp.where(lane <= FILTER_BANDS, jnp.cos(fw),
                                jnp.where(lane <= 2 * FILTER_BANDS, -jnp.sin(fw), 0.0)))
        h = jnp.sin(freq_ref[0:1, :] * (jnp.dot(z, win_ref[...], precision=HI, preferred_element_type=f32)
                                       + bin_ref[...]))
        for i in range(wmid_ref.shape[0]):
            h = jnp.sin(freq_ref[i + 1:i + 2, :]
                        * (jnp.dot(h, wmid_ref[i], precision=HI, preferred_element_type=f32)
                           + bmid_ref[i:i + 1, :]))
        out = jnp.dot(h, wout_ref[...], precision=HI, preferred_element_type=f32)
        tc = (2 * row_c + p).astype(f32) * (1.0 / (L - 1))
        out = out * jnp.exp(-tc * delta)
        if p == 0:
            out = jnp.where(is_bwd & (row_c == 0), 0.0, out)
        h_ref[p] = out.astype(bf16)
        total = total + jnp.sum(jnp.abs(out), axis=0, keepdims=True)
    asum_ref[0:1, :] += total


def _hyena_filters(L, C, f_w_in, f_b_in, f_w_mid, f_b_mid, f_freq, f_w_out):
    Lh = L // 2
    tr = _tile(256, Lh)
    order = f_w_in.shape[1]
    win = jnp.pad(f_w_in, ((0, LANES - f_w_in.shape[0]), (0, 0)))
    full = lambda a: pl.BlockSpec(a.shape, lambda i: (0,) * a.ndim)
    args = (win, f_b_in.reshape(1, order), f_w_mid, f_b_mid, f_freq, f_w_out)
    return pl.pallas_call(
        functools.partial(_filter_kernel, L=L, C=C),
        grid=(Lh // tr,),
        in_specs=[full(a) for a in args],
        out_specs=[pl.BlockSpec((2, tr, 2 * C), lambda i: (0, i, 0)),
                   pl.BlockSpec((SUBLANES, 2 * C), lambda i: (0, 0))],
        out_shape=[jax.ShapeDtypeStruct((2, Lh, 2 * C), bf16),
                   jax.ShapeDtypeStruct((SUBLANES, 2 * C), f32)],
        compiler_params=_cparams(("arbitrary",)),
        name="hyena_filters",
    )(*args)


def _fwd_products(c_ref, x_ref):
    tm, k = c_ref.shape[1:]
    even = jnp.dot(c_ref[0:2].reshape(2 * tm, k), x_ref[0], preferred_element_type=f32)
    odd = jnp.dot(c_ref[2:4].reshape(2 * tm, k), x_ref[1], preferred_element_type=f32)
    return even[:tm], even[tm:], odd[:tm], odd[tm:]


def _filter_spectrum_kernel(c_ref, h_ref, asum_ref, k_ref, k0_ref, *, L, C):
    ce, se, co, so = _fwd_products(c_ref, h_ref)
    a = asum_ref[0:1, :]
    scale = 1.0 / ((a[:, :C] + a[:, C:]) * L)
    F = lambda v: v[:, :C]
    G = lambda v: v[:, C:]
    k_ref[0] = (F(ce) + F(co) + G(ce) + G(co)) * scale
    k_ref[1] = (-(F(se) + F(so)) + (G(se) + G(so))) * scale
    k_ref[2] = (F(ce) - F(co) + G(ce) - G(co)) * scale
    k_ref[3] = ((F(se) - F(so)) - (G(se) - G(so))) * scale

    @pl.when(pl.program_id(0) == 0)
    def _():
        k0_ref[...] = jnp.zeros_like(k0_ref)
        k0_ref[0:1, :] = ((F(se) + G(se)) * scale)[0:1, :]
        k0_ref[1:2, :] = ((F(so) - G(so)) * scale)[0:1, :]


def _filter_spectrum(cfwd, hcat, asum, L, C):
    Lh = L // 2
    tm = _tile(256, Lh)
    return pl.pallas_call(
        functools.partial(_filter_spectrum_kernel, L=L, C=C),
        grid=(Lh // tm,),
        in_specs=[pl.BlockSpec((4, tm, Lh), lambda i: (0, i, 0)),
                  pl.BlockSpec((2, Lh, 2 * C), lambda i: (0, 0, 0)),
                  pl.BlockSpec((SUBLANES, 2 * C), lambda i: (0, 0))],
        out_specs=[pl.BlockSpec((4, tm, C), lambda i: (0, i, 0)),
                   pl.BlockSpec((SUBLANES, C), lambda i: (0, 0))],
        out_shape=[jax.ShapeDtypeStruct((4, Lh, C), f32),
                   jax.ShapeDtypeStruct((SUBLANES, C), f32)],
        compiler_params=_cparams(("arbitrary",)),
        name="filter_spectrum",
    )(cfwd, hcat, asum)


def _conv_fwd_kernel(c_ref, z_ref, k_ref, k0_ref, s_ref):
    products = _fwd_products(c_ref, z_ref)
    C = k_ref.shape[2]
    first = pl.program_id(1) == 0
    for h in range(z_ref.shape[2] // C):
        cols = slice(h * C, (h + 1) * C)
        ce, se, co, so = (v[:, cols] for v in products)
        zsr, zsi, zdr, zdi = ce + co, -(se + so), ce - co, se - so
        ksr, ksi, kdr, kdi = k_ref[0], k_ref[1], k_ref[2], k_ref[3]
        psr = zsr * ksr - zsi * ksi
        psi = zsr * ksi + zsi * ksr
        pdr = zdr * kdr - zdi * kdi
        pdi = zdr * kdi + zdi * kdr
        gen = (psr + pdr, psi - pdi, psr - pdr, psi + pdi)

        @pl.when(first)
        def _(cols=cols, ce=ce, se=se, so=so, zsr=zsr, zdr=zdr, ksr=ksr, kdr=kdr, gen=gen):
            row0 = lax.broadcasted_iota(jnp.int32, ce.shape, 0) == 0
            p0 = zsr * ksr
            pl_ = zdr * kdr
            ak, bk = k0_ref[0:1, :], k0_ref[1:2, :]
            phr = se * ak - so * bk
            phi = -(se * bk + so * ak)
            spec = (0.5 * (p0 + pl_), -phr, 0.5 * (p0 - pl_), phi)
            for i in range(4):
                s_ref[i, :, cols] = jnp.where(row0, spec[i], gen[i]).astype(bf16)

        @pl.when(jnp.logical_not(first))
        def _(cols=cols, gen=gen):
            for i in range(4):
                s_ref[i, :, cols] = gen[i].astype(bf16)


def _conv_fwd(cfwd, z, kspec, k0, C):
    _, Lh, N = z.shape
    tm = _tile(CONV_ROWS, Lh)
    tn = _tile(CONV_COLS, N)
    return pl.pallas_call(
        _conv_fwd_kernel,
        grid=(N // tn, Lh // tm),
        in_specs=[pl.BlockSpec((4, tm, Lh), lambda n, i: (0, i, 0)),
                  pl.BlockSpec((2, Lh, tn), lambda n, i: (0, 0, n)),
                  pl.BlockSpec((4, tm, C), lambda n, i: (0, i, 0)),
                  pl.BlockSpec((SUBLANES, C), lambda n, i: (0, 0))],
        out_specs=pl.BlockSpec((4, tm, tn), lambda n, i: (0, i, n)),
        out_shape=jax.ShapeDtypeStruct((4, Lh, N), bf16),
        compiler_params=_cparams(("parallel", "arbitrary")),
        name="conv_fwd",
    )(cfwd, z, kspec, k0)


def _pair_products(c_ref, r_ref):
    ev = (jnp.dot(c_ref[0], r_ref[0], preferred_element_type=f32)
          + jnp.dot(c_ref[1], r_ref[1], preferred_element_type=f32))
    od = (jnp.dot(c_ref[2], r_ref[2], preferred_element_type=f32)
          + jnp.dot(c_ref[3], r_ref[3], preferred_element_type=f32))
    return ev, od


def _conv_inv_kernel(c_ref, s_ref, x0_ref, z_ref, fb_ref, y_ref):
    conv = _pair_products(c_ref, s_ref)
    C = fb_ref.shape[1]
    fb = fb_ref[...]
    for p in range(2):
        for h in range(z_ref.shape[2] // C):
            cols = slice(h * C, (h + 1) * C)
            zp = z_ref[p, :, cols].astype(f32)
            y_ref[p, :, cols] = (x0_ref[p, :, cols].astype(f32) * (conv[p][:, cols] + zp * fb)).astype(bf16)


def _conv_inv(cinv, s, x0, z, f_bias, C):
    _, Lh, N = s.shape
    tm = _tile(CONV_ROWS, Lh)
    tn = _tile(CONV_COLS, N)
    return pl.pallas_call(
        _conv_inv_kernel,
        grid=(N // tn, Lh // tm),
        in_specs=[pl.BlockSpec((4, tm, Lh), lambda n, i: (0, i, 0)),
                  pl.BlockSpec((4, Lh, tn), lambda n, i: (0, 0, n)),
                  pl.BlockSpec((2, tm, tn), lambda n, i: (0, i, n)),
                  pl.BlockSpec((2, tm, tn), lambda n, i: (0, i, n)),
                  pl.BlockSpec((1, C), lambda n, i: (0, 0))],
        out_specs=pl.BlockSpec((2, tm, tn), lambda n, i: (0, i, n)),
        out_shape=jax.ShapeDtypeStruct((2, Lh, N), bf16),
        compiler_params=_cparams(("parallel", "arbitrary")),
        name="conv_inv",
    )(cinv, s, x0, z, f_bias.reshape(1, C))


def _fourier_kernel(c_ref, u_ref, y_ref, rhs, stage, tw1, tw2, *, L, scale):
    Lh, n, H = L // 2, L // 4, L // 8
    tn = u_ref.shape[2]

    @pl.when(pl.program_id(0) == 0)
    def _():
        r1 = lax.broadcasted_iota(jnp.int32, (Lh, LANES), 0)
        tw1[0], tw1[1] = _cos_sin_of_turn_fraction(2 * (r1 % n) + r1 // n, L)
        r2 = lax.broadcasted_iota(jnp.int32, (n, LANES), 0)
        tw2[0], tw2[1] = _cos_sin_of_turn_fraction(2 * (r2 % H) + r2 // H, Lh)

    def rotate(a, b, c, s):
        return c * a - s * b, s * a + c * b

    for q in range(tn // LANES):
        cols = slice(q * LANES, (q + 1) * LANES)
        for p in range(2):
            first = slice(p * n, p * n + H)
            second = slice(p * n + H, (p + 1) * n)
            out = slice(p * H, (p + 1) * H)
            out_b = slice(n + p * H, n + (p + 1) * H)
            c2, s2 = tw2[0, out, :], tw2[1, out, :]

            def emit(sub, a0, b0, a1, b1):
                rhs[sub, out, cols] = (a0 + a1).astype(bf16)
                rhs[sub, out_b, cols] = (b0 + b1).astype(bf16)
                da, db = rotate(a0 - a1, b0 - b1, c2, s2)
                rhs[sub + 1, out, cols] = da.astype(bf16)
                rhs[sub + 1, out_b, cols] = db.astype(bf16)

            ld = lambda k, rows: u_ref[k, rows, cols].astype(f32)
            emit(0, ld(0, first), ld(1, first), ld(0, second), ld(1, second))
            odd = [rotate(ld(2, rows), ld(3, rows), tw1[0, rows, :], tw1[1, rows, :]) for rows in (first, second)]
            emit(2, odd[0][0], odd[0][1], odd[1][0], odd[1][1])

    for parity in range(2):
        pair = [jnp.dot(c_ref[...], rhs[2 * parity + h], preferred_element_type=f32) * scale for h in range(2)]
        for q in range(tn // LANES):
            cols = slice(q * LANES, (q + 1) * LANES)
            for h in range(2):
                stage[q, pl.ds(h, n, stride=2), :] = pair[h][:, cols]
            y_ref[parity, :, cols] = stage[q].astype(bf16)


def _fourier_dft(cfou, u4, L, C):
    _, Lh, N = u4.shape
    n = L // 4
    tn = _tile(FOURIER_COLS, C)
    return pl.pallas_call(
        functools.partial(_fourier_kernel, L=L, scale=1.0 / math.sqrt(L * HEAD_DIM)),
        grid=(N // tn,),
        in_specs=[pl.BlockSpec((n, 2 * n), lambda j: (0, 0)),
                  pl.BlockSpec((4, Lh, tn), lambda j: (0, 0, j))],
        out_specs=pl.BlockSpec((2, Lh, tn), lambda j: (0, 0, j)),
        out_shape=jax.ShapeDtypeStruct((2, Lh, N), bf16),
        scratch_shapes=[pltpu.VMEM((4, 2 * n, tn), bf16), pltpu.VMEM((tn // LANES, Lh, LANES), f32),
                        pltpu.VMEM((2, Lh, LANES), f32), pltpu.VMEM((2, n, LANES), f32)],
        compiler_params=_cparams(("arbitrary",)),
        name="fourier_dft",
    )(cfou, u4)


def _rms(x, g):
    return x * lax.rsqrt(jnp.mean(x * x, axis=-1, keepdims=True) + EPS) * g


def _stage_rows(chunks, x):
    for k in range(chunks.shape[0]):
        chunks[k] = x[:, k * LANES:(k + 1) * LANES]


def _parity_rows(chunks, p, n):
    return jnp.concatenate([chunks[k, pl.ds(p, n, stride=2), :] for k in range(chunks.shape[0])], axis=1)


def _inproj_kernel(xa_ref, xb_ref, pa_ref, na_ref, pb_ref, nb_ref, g_ref, w_ref, cw_ref, cb_ref,
                   z_ref, x0_ref, u_ref, stage, *, C, D):
    tj = xa_ref.shape[1] // 2
    jt = pl.program_id(1)
    n_half = pl.num_programs(1)
    g = g_ref[...]
    w = w_ref[...]
    cw0, cw1, cw2 = cw_ref[0:1, :], cw_ref[1:2, :], cw_ref[2:3, :]
    cb = cb_ref[...]
    row = lax.broadcasted_iota(jnp.int32, (tj, 3 * C), 0)
    four = []
    halves = ((xa_ref, pa_ref, na_ref, jt == 0, False),
              (xb_ref, pb_ref, nb_ref, False, jt == n_half - 1))
    for h, (x_ref, p_ref, n_ref, at_start, at_end) in enumerate(halves):
        _stage_rows(stage, x_ref[0])
        lhs = jnp.concatenate([
            _rms(_parity_rows(stage, 0, tj), g).astype(bf16),
            _rms(_parity_rows(stage, 1, tj), g).astype(bf16),
            _rms(p_ref[0], g).astype(bf16),
            _rms(n_ref[0], g).astype(bf16)], axis=0)
        res = jnp.dot(lhs, w, preferred_element_type=f32)
        pe, po = res[:tj, :3 * C], res[tj:2 * tj, :3 * C]
        prev = res[2 * tj + SUBLANES - 1:2 * tj + SUBLANES, :3 * C]
        nxt = res[2 * tj + SUBLANES:2 * tj + SUBLANES + 1, :3 * C]
        if at_start is not False:
            prev = jnp.where(at_start, 0.0, prev)
        if at_end is not False:
            nxt = jnp.where(at_end, 0.0, nxt)
        po_dn = jnp.where(row == 0, prev, pltpu.roll(po, 1, 0))
        pe_up = jnp.where(row == tj - 1, nxt, pltpu.roll(pe, tj - 1, 0))
        uc = (cb + cw0 * po_dn + cw1 * pe + cw2 * po,
              cb + cw0 * pe + cw1 * po + cw2 * pe_up)
        for p in range(2):
            x0_ref[p, h] = uc[p][:, :C].astype(bf16)
            z_ref[p, h] = (uc[p][:, C:2 * C] * uc[p][:, 2 * C:]).astype(bf16)
        four.append((res[:tj, 3 * C:], res[tj:2 * tj, 3 * C:]))
    for p in range(2):
        fa, fb = four[0][p], four[1][p]
        u_ref[0, p] = (fa[:, :C] + fb[:, :C]).astype(bf16)
        u_ref[1, p] = (fa[:, C:] + fb[:, C:]).astype(bf16)
        u_ref[2, p] = (fa[:, :C] - fb[:, :C]).astype(bf16)
        u_ref[3, p] = (fa[:, C:] - fb[:, C:]).astype(bf16)


def _inproj(x, norm_g, wcat, conv_w, conv_b, C):
    B, L, D = x.shape
    Lq = L // 4
    tj = _tile(256, Lq)
    nq = Lq // tj
    N = B * C
    hb = 2 * tj // SUBLANES
    last = L // SUBLANES - 1
    main = lambda off: pl.BlockSpec((1, 2 * tj, D), lambda b, j: (b, j + off, 0))
    prev = lambda off: pl.BlockSpec((1, SUBLANES, D), lambda b, j: (b, jnp.maximum((j + off) * hb - 1, 0), 0))
    nxt = lambda off: pl.BlockSpec((1, SUBLANES, D), lambda b, j: (b, jnp.minimum((j + off + 1) * hb, last), 0))
    full = lambda a: pl.BlockSpec(a.shape, lambda b, j: (0,) * a.ndim)
    g = norm_g.reshape(1, D)
    cb = conv_b.reshape(1, 3 * C)
    return pl.pallas_call(
        functools.partial(_inproj_kernel, C=C, D=D),
        grid=(B, nq),
        in_specs=[main(0), main(nq), prev(0), nxt(0), prev(nq), nxt(nq),
                  full(g), full(wcat), full(conv_w), full(cb)],
        out_specs=[pl.BlockSpec((2, 2, tj, C), lambda b, j: (0, 0, j, b)),
                   pl.BlockSpec((2, 2, tj, C), lambda b, j: (0, 0, j, b)),
                   pl.BlockSpec((4, 2, tj, C), lambda b, j: (0, 0, j, b))],
        out_shape=[jax.ShapeDtypeStruct((2, 2, Lq, N), bf16),
                   jax.ShapeDtypeStruct((2, 2, Lq, N), bf16),
                   jax.ShapeDtypeStruct((4, 2, Lq, N), bf16)],
        scratch_shapes=[pltpu.VMEM((D // LANES, 2 * tj, LANES), f32)],
        compiler_params=_cparams(("parallel", "arbitrary")),
        name="inproj",
    )(x, x, x, x, x, x, g, wcat, conv_w, cb)


def _pack_bf16_pairs(v, is_bf16_exact=False):
    k = v.shape[1] // 2
    rounded = v if is_bf16_exact else v.astype(bf16).astype(f32)
    hi = pltpu.bitcast(rounded[:, :k], jnp.uint32)
    lo = pltpu.bitcast(rounded[:, k:], jnp.uint32)
    return hi | (lo >> 16)


def _unpack_bf16_pairs(u):
    hi = pltpu.bitcast(u & jnp.uint32(0xFFFF0000), f32)
    lo = pltpu.bitcast(u << 16, f32)
    return jnp.concatenate([hi, lo], axis=1)


def _row_tiles_shape(n, D):
    return (n, D // (2 * LANES), LANES)


def _store_row_tiles(ref, packed, row0=0):
    total, per_row, _ = ref.shape
    n = packed.shape[0]
    flat = ref.reshape(total * per_row, LANES)
    for s in range(per_row):
        flat[pl.ds(row0 * per_row + s, n, stride=per_row), :] = packed[:, s * LANES:(s + 1) * LANES]


def _load_row_tiles(ref, row0=0, n=None):
    total, per_row, _ = ref.shape
    n = total if n is None else n
    flat = ref.reshape(total * per_row, LANES)
    return jnp.concatenate([flat[pl.ds(row0 * per_row + s, n, stride=per_row), :] for s in range(per_row)],
                           axis=1)


def _outproj_router_kernel(yh_ref, yf_ref, x_ref, mg_ref, wo_ref, g2_ref, wr_ref, br_ref,
                           h2_ref, tok_ref, pos_ref, vec_ref, tinfo_ref, cnt_ref, carry_ref, stage, *, C):
    n_sub = pos_ref.shape[0]
    tj = pos_ref.shape[2]
    _stage_rows(stage, x_ref[0])

    @pl.when((pl.program_id(0) == 0) & (pl.program_id(1) == 0))
    def _():
        carry_ref[...] = jnp.zeros_like(carry_ref)

    li = lax.broadcasted_iota(jnp.int32, (LANES, LANES), 0)
    lj = lax.broadcasted_iota(jnp.int32, (LANES, LANES), 1)
    hw = _tile(MXU_DIM, C)
    ai = lax.broadcasted_iota(jnp.int32, (hw, hw), 0)
    aj = lax.broadcasted_iota(jnp.int32, (hw, hw), 1)
    avg = jnp.where(ai // HEAD_DIM == aj // HEAD_DIM, 1.0 / HEAD_DIM, 0.0).astype(bf16)
    upper = (li < lj).astype(bf16)
    ri = lax.broadcasted_iota(jnp.int32, (tj, tj), 0)
    ci = lax.broadcasted_iota(jnp.int32, (tj, tj), 1)
    tri = (ci < ri).astype(bf16)
    wr = wr_ref[...]
    wr_hi = wr.astype(bf16)
    wr_lo = (wr - wr_hi.astype(f32)).astype(bf16)
    wr_both = jnp.concatenate([wr_hi, wr_lo], axis=1)
    lane = lax.broadcasted_iota(jnp.int32, (n_sub * tj, LANES), 1).astype(f32)
    sub = lax.broadcasted_iota(jnp.int32, (SUBLANES, LANES), 0)
    neg = jnp.float32(-jnp.inf)
    big = jnp.float32(LANES)

    def first_argmax(vals, mask):
        v = jnp.where(mask, vals, neg)
        m = jnp.max(v, axis=1, keepdims=True)
        idx = jnp.min(jnp.where(mask & (v == m), lane, big), axis=1, keepdims=True)
        return m, idx

    normed = []
    for s in range(n_sub):
        r, p = divmod(s, 2)
        parts = []
        for ref in (yh_ref, yf_ref):
            y = ref[p, r * tj:(r + 1) * tj, :].astype(f32)
            ysq = (y * y).astype(bf16)
            ms = jnp.concatenate([jnp.dot(ysq[:, k:k + hw], avg, preferred_element_type=f32)
                                  for k in range(0, C, hw)], axis=1)
            parts.append(y * lax.rsqrt(ms + EPS))
        normed.append((jnp.concatenate(parts, axis=1) * mg_ref[...]).astype(bf16))
    proj = jnp.dot(jnp.concatenate(normed, axis=0), wo_ref[...], preferred_element_type=f32)
    t_hi, t_lo = [], []
    for s in range(n_sub):
        r, p = divmod(s, 2)
        rows = slice(s * tj, (s + 1) * tj)
        h2 = _parity_rows(stage, r * 2 * tj + p, tj) + proj[rows]
        h2_ref[rows, :] = h2
        tok = _rms(h2, g2_ref[...])
        hi = tok.astype(bf16)
        tok_ref[rows, :] = hi
        t_hi.append(hi)
        t_lo.append((tok - hi.astype(f32)).astype(bf16))
    hi_terms = jnp.dot(jnp.concatenate(t_hi, axis=0), wr_both, preferred_element_type=f32)
    logits = (hi_terms[:, :LANES] + hi_terms[:, LANES:]
              + jnp.dot(jnp.concatenate(t_lo, axis=0), wr_hi, preferred_element_type=f32) + br_ref[...])

    gmask = (lane >= N_EXPERTS) & (lane < N_EXPERTS + N_GROUPS)
    gmax, gidx = first_argmax(logits, gmask)
    gsel = gidx - N_EXPERTS
    p_group = 1.0 / jnp.sum(jnp.where(gmask, jnp.exp(logits - gmax), 0.0), axis=1, keepdims=True)
    emask = (lane >= gsel * EXPERTS_PER_GROUP) & (lane < (gsel + 1) * EXPERTS_PER_GROUP)
    l0, e0 = first_argmax(logits, emask)
    l1, e1 = first_argmax(logits, emask & (lane != e0))
    ex = jnp.exp(l1 - l0)
    w0 = p_group / (1.0 + ex)
    w1 = p_group * ex / (1.0 + ex)

    oh0 = (lane == e0).astype(f32)
    oh1 = (lane == e1).astype(f32)
    both = (oh0 + oh1).astype(bf16)
    where, counts = [], []
    for s in range(n_sub):
        tile_both = both[s * tj:(s + 1) * tj, :]
        ranks = jnp.dot(tri, tile_both, preferred_element_type=f32)
        n_e = ranks[tj - 1:tj, :] + tile_both[tj - 1:tj, :].astype(f32)
        n_e8 = jnp.broadcast_to(n_e, (SUBLANES, LANES))
        tile_off = jnp.dot(n_e8.astype(bf16), upper, preferred_element_type=f32)[0:1, :]
        where.append(ranks + tile_off)
        counts.append(n_e8)
    where = jnp.concatenate(where, axis=0)
    pos0 = jnp.sum(oh0 * where, axis=1, keepdims=True)
    pos1 = jnp.sum(oh1 * where, axis=1, keepdims=True)
    cols = jnp.where(lane == 0, w0, jnp.where(lane == 1, w1,
                     jnp.where(lane == 2, pos0, jnp.where(lane == 3, pos1, 0.0))))
    vec_ref[...] = cols
    base = carry_ref[...]
    for s in range(n_sub):
        pos_ref[s] = cols[s * tj:(s + 1) * tj, :].T[:SUBLANES, :].astype(jnp.int32)
        long_seg = (jnp.max(counts[s], axis=1, keepdims=True) >= SEG_SPLIT).astype(f32)
        tinfo_ref[s] = jnp.where(sub == 0, counts[s], jnp.where(sub == 1, base, jnp.where(sub == 2, long_seg, 0.0))
                                 ).astype(jnp.int32)
        base = base + counts[s]
    carry_ref[...] = base
    cnt_ref[...] = base.astype(jnp.int32)


def _outproj_router(yh, yf, x, mix_g, wo, norm2_g, wr, br, C):
    B, L, D = x.shape
    Lh = L // 2
    tj = _tile(256, Lh)
    rt = ROW_TILES_PER_STEP
    ns = 2 * rt
    nj = Lh // (rt * tj)
    T = B * L
    ntiles = T // tj
    lin = lambda b, j: b * nj + j
    full = lambda a: pl.BlockSpec(a.shape, lambda b, j: (0,) * a.ndim)
    mg = mix_g.reshape(1, D)
    g2 = norm2_g.reshape(1, D)
    return pl.pallas_call(
        functools.partial(_outproj_router_kernel, C=C),
        grid=(B, nj),
        in_specs=[pl.BlockSpec((2, rt * tj, C), lambda b, j: (0, j, b)),
                  pl.BlockSpec((2, rt * tj, C), lambda b, j: (0, j, b)),
                  pl.BlockSpec((1, ns * tj, D), lambda b, j: (b, j, 0)),
                  full(mg), full(wo), full(g2), full(wr), full(br)],
        out_specs=[pl.BlockSpec((ns * tj, D), lambda b, j: (lin(b, j), 0)),
                   pl.BlockSpec((ns * tj, D), lambda b, j: (lin(b, j), 0)),
                   pl.BlockSpec((ns, SUBLANES, tj), lambda b, j: (lin(b, j), 0, 0)),
                   pl.BlockSpec((ns * tj, LANES), lambda b, j: (lin(b, j), 0)),
                   pl.BlockSpec((ns, SUBLANES, LANES), lambda b, j: (lin(b, j), 0, 0)),
                   pl.BlockSpec((SUBLANES, LANES), lambda b, j: (0, 0))],
        out_shape=[jax.ShapeDtypeStruct((T, D), f32),
                   jax.ShapeDtypeStruct((T, D), bf16),
                   jax.ShapeDtypeStruct((ntiles, SUBLANES, tj), jnp.int32),
                   jax.ShapeDtypeStruct((T, LANES), f32),
                   jax.ShapeDtypeStruct((ntiles, SUBLANES, LANES), jnp.int32),
                   jax.ShapeDtypeStruct((SUBLANES, LANES), jnp.int32)],
        scratch_shapes=[pltpu.VMEM((SUBLANES, LANES), f32), pltpu.VMEM((D // LANES, ns * tj, LANES), f32)],
        compiler_params=_cparams(("arbitrary", "arbitrary")),
        name="outproj_router",
    )(yh, yf, x, mg, wo, g2, wr, br)


def _padded(c):
    return ((c + MOE_ROWS - 1) // MOE_ROWS) * MOE_ROWS


def _pad_starts(cnt_ref, start_ref):
    def body(e, acc):
        start_ref[e] = acc
        return acc + _padded(cnt_ref[0, e])
    return lax.fori_loop(0, N_EXPERTS, body, jnp.int32(0))


def _segment_copies(n, src, src_off, dst, dst_off, sem, max_rows, wait, src_advances=True, min_rows=1):
    bit = max_rows
    while bit >= min_rows:
        take = n & bit

        @pl.when(take != 0)
        def _(src_off=src_off, dst_off=dst_off, bit=bit):
            cp = pltpu.make_async_copy(src.at[pl.ds(src_off, bit)], dst.at[pl.ds(dst_off, bit)], sem)
            if wait:
                cp.wait()
            else:
                cp.start()
        if src_advances:
            src_off = src_off + take
        dst_off = dst_off + take
        bit //= 2


def _tile_segments(tinfo_ref, t, start_ref, local, local_is_src, remote, sem, tj):
    def copies(long_part):
        off = 0
        for e in range(N_EXPERTS):
            n = tinfo_ref[t, 0, e]
            far = start_ref[e] + tinfo_ref[t, 1, e]
            if long_part:
                args = (n, off, far, tj, SEG_SPLIT)
            else:
                head = n & ~(SEG_SPLIT - 1)
                args = (n, off + head, far + head, SEG_SPLIT // 2, 1)
            n_, near_, far_, hi, lo = args
            if local_is_src:
                _segment_copies(n_, local, near_, remote, far_, sem, hi, wait=False, min_rows=lo)
            else:
                _segment_copies(n_, remote, far_, local, near_, sem, hi, wait=False, min_rows=lo)
            off = off + n

    copies(long_part=False)

    @pl.when(tinfo_ref[t, 2, 0] != 0)
    def _():
        copies(long_part=True)


def _dispatch_kernel(tinfo_ref, cnt_ref, tok_ref, pos_ref, xs_ref, first_ref, nblk_ref,
                     sbufs, zbuf, start_ref, sems, zsem):
    tj = pos_ref.shape[2]
    n_blocks = xs_ref.shape[0] // MOE_ROWS

    @pl.when(pl.program_id(0) == 0)
    def _():
        total = _pad_starts(cnt_ref, start_ref)
        used = total // MOE_ROWS

        def block_ranges(e, c):
            first_ref[e] = start_ref[e] // MOE_ROWS
            nblk_ref[e] = _padded(cnt_ref[0, e]) // MOE_ROWS
            return c
        lax.fori_loop(0, N_EXPERTS, block_ranges, 0)

        zbuf[...] = jnp.zeros_like(zbuf)
        zrows = zbuf.shape[0]
        for wait in (False, True):
            def zero_pad(e, c, wait=wait):
                cnt = cnt_ref[0, e]
                _segment_copies(_padded(cnt) - cnt, zbuf, 0, xs_ref, start_ref[e] + cnt, zsem,
                                zrows, wait, src_advances=False)
                return c
            lax.fori_loop(0, N_EXPERTS, zero_pad, 0)

            def zero_tail(i, c, wait=wait):
                cp = pltpu.make_async_copy(zbuf, xs_ref.at[pl.ds(total + i * zrows, zrows)], zsem)
                if wait:
                    cp.wait()
                else:
                    cp.start()
                return c
            lax.fori_loop(0, (n_blocks - used) * (MOE_ROWS // zrows), zero_tail, 0)

    def wait_tile(k):
        pltpu.make_async_copy(sbufs[k], xs_ref.at[pl.ds(0, 2 * tj)], sems.at[k]).wait()

    slot = lax.broadcasted_iota(jnp.int32, (2 * tj, tj), 0)
    for k in range(TILES_PER_STEP):
        @pl.when(pl.program_id(0) > 0)
        def _(k=k):
            wait_tile(k)
        perm = ((slot == pos_ref[k, 2:3, :]) | (slot == pos_ref[k, 3:4, :])).astype(bf16)
        sorted_rows = jnp.dot(perm, tok_ref[k * tj:(k + 1) * tj, :], preferred_element_type=f32)
        _store_row_tiles(sbufs[k], _pack_bf16_pairs(sorted_rows, is_bf16_exact=True))
        if k > 0:
            _tile_segments(tinfo_ref, k - 1, start_ref, sbufs[k - 1], True, xs_ref, sems.at[k - 1], tj)
    last = TILES_PER_STEP - 1
    _tile_segments(tinfo_ref, last, start_ref, sbufs[last], True, xs_ref, sems.at[last], tj)

    @pl.when(pl.program_id(0) == pl.num_programs(0) - 1)
    def _():
        for k in range(TILES_PER_STEP):
            wait_tile(k)


def _dispatch(tinfo, cnt, tok, pos, n_rows):
    ntiles, _, tj = pos.shape
    T, D = tok.shape
    n_blocks = n_rows // MOE_ROWS
    nt = TILES_PER_STEP
    assert ntiles % nt == 0
    sbuf = pltpu.VMEM(_row_tiles_shape(2 * tj, D), jnp.uint32)
    return pl.pallas_call(
        _dispatch_kernel,
        grid=(ntiles // nt,),
        in_specs=[pl.BlockSpec((nt, SUBLANES, LANES), lambda i: (i, 0, 0), memory_space=pltpu.SMEM),
                  pl.BlockSpec(memory_space=pltpu.SMEM),
                  pl.BlockSpec((nt * tj, D), lambda i: (i, 0)),
                  pl.BlockSpec((nt, SUBLANES, tj), lambda i: (i, 0, 0))],
        out_specs=[pl.BlockSpec(memory_space=pl.ANY),
                   pl.BlockSpec(memory_space=pltpu.SMEM),
                   pl.BlockSpec(memory_space=pltpu.SMEM)],
        out_shape=[jax.ShapeDtypeStruct(_row_tiles_shape(n_rows, D), jnp.uint32),
                   jax.ShapeDtypeStruct((N_EXPERTS,), jnp.int32),
                   jax.ShapeDtypeStruct((N_EXPERTS,), jnp.int32)],
        scratch_shapes=[[sbuf] * nt,
                        pltpu.VMEM(_row_tiles_shape(MOE_ROWS // 2, D), jnp.uint32),
                        pltpu.SMEM((N_EXPERTS,), jnp.int32), pltpu.SemaphoreType.DMA((nt,)),
                        pltpu.SemaphoreType.DMA],
        compiler_params=_cparams(("arbitrary",)),
        name="moe_dispatch",
    )(tinfo, cnt, tok, pos)


def _experts_kernel(first_ref, nblk_ref, xs_ref, wg_ref, wu_ref, wd_ref, ys_ref,
                    wg_s, wu_s, wd_s, xbuf, ybuf, in_sems, out_sems):
    e = pl.program_id(0)
    last_e = pl.num_programs(0) - 1
    rows = MOE_ROWS
    n_total = xs_ref.shape[0] // rows
    used = first_ref[last_e] + nblk_ref[last_e]

    def fetch(g, slot):
        return pltpu.make_async_copy(xs_ref.at[pl.ds(g * rows, rows)], xbuf.at[pl.ds(slot * rows, rows)],
                                     in_sems.at[slot])

    def flush(g, slot):
        return pltpu.make_async_copy(ybuf.at[pl.ds(slot * rows, rows)], ys_ref.at[pl.ds(g * rows, rows)],
                                     out_sems.at[slot])

    @pl.when((e == 0) & (used > 0))
    def _():
        fetch(0, 0).start()

    @pl.when(nblk_ref[e] > 0)
    def _():
        wg_s[...] = wg_ref[...].astype(bf16)
        wu_s[...] = wu_ref[...].astype(bf16)
        wd_s[...] = wd_ref[...].astype(bf16)

    def block(b, carry):
        g = first_ref[e] + b
        slot = g % 2
        fetch(g, slot).wait()

        @pl.when(g + 1 < used)
        def _():
            fetch(g + 1, 1 - slot).start()

        @pl.when(g >= 2)
        def _():
            flush(g - 2, slot).wait()
        half = rows // EXPERT_ROW_GROUPS
        starts = [slot * rows + r * half for r in range(EXPERT_ROW_GROUPS)]
        gated = []
        for row0 in starts:
            xb = _unpack_bf16_pairs(_load_row_tiles(xbuf, row0, half)).astype(bf16)
            gated.append((jnp.dot(xb, wg_s[...], preferred_element_type=f32),
                          jnp.dot(xb, wu_s[...], preferred_element_type=f32)))
        for row0, (a, u) in zip(starts, gated):
            h = (a * jax.nn.sigmoid(a) * u).astype(bf16)
            _store_row_tiles(ybuf, _pack_bf16_pairs(jnp.dot(h, wd_s[...], preferred_element_type=f32)), row0)
        flush(g, slot).start()
        return carry
    lax.fori_loop(0, nblk_ref[e], block, 0)

    @pl.when(e == last_e)
    def _():
        for back in (1, 2):
            @pl.when(used >= back)
            def _(back=back):
                flush(used - back, (used - back) % 2).wait()
        ybuf[pl.ds(0, rows)] = jnp.zeros((rows,) + ybuf.shape[1:], ybuf.dtype)
        for wait in (False, True):
            def zero_tail(g, c, wait=wait):
                cp = flush(g, 0)
                if wait:
                    cp.wait()
                else:
                    cp.start()
                return c
            lax.fori_loop(used, n_total, zero_tail, 0)


def _experts(first_block, n_blocks, xs, w_gate, w_up, w_down):
    _, D, De = w_gate.shape
    widx = lambda e, first, nblk: (e, 0, 0)
    buf = pltpu.VMEM(_row_tiles_shape(2 * MOE_ROWS, D), jnp.uint32)
    return pl.pallas_call(
        _experts_kernel,
        grid_spec=pltpu.PrefetchScalarGridSpec(
            num_scalar_prefetch=2,
            grid=(N_EXPERTS,),
            in_specs=[pl.BlockSpec(memory_space=pl.ANY),
                      pl.BlockSpec((None, D, De), widx),
                      pl.BlockSpec((None, D, De), widx),
                      pl.BlockSpec((None, De, D), widx)],
            out_specs=pl.BlockSpec(memory_space=pl.ANY),
            scratch_shapes=[pltpu.VMEM((D, De), bf16), pltpu.VMEM((D, De), bf16), pltpu.VMEM((De, D), bf16),
                            buf, buf, pltpu.SemaphoreType.DMA((2,)), pltpu.SemaphoreType.DMA((2,))],
        ),
        out_shape=jax.ShapeDtypeStruct(xs.shape, jnp.uint32),
        compiler_params=_cparams(("arbitrary",)),
        name="moe_experts",
    )(first_block, n_blocks, xs, w_gate, w_up, w_down)


def _combine_kernel(tinfo_ref, tnext_ref, cnt_ref, h2_ref, vec_ref, g_ref, ys_ref, o_ref,
                    gbufs, stages, start_ref, sems):
    tj = vec_ref.shape[0] // TILES_PER_STEP
    step = pl.program_id(0)

    def fetch(info_ref, t, buf):
        _tile_segments(info_ref, t, start_ref, gbufs[buf], False, ys_ref, sems.at[buf], tj)

    @pl.when(step == 0)
    def _():
        _pad_starts(cnt_ref, start_ref)
        for t in range(GATHER_AHEAD):
            fetch(tinfo_ref, t, t)

    slot = lax.broadcasted_iota(jnp.int32, (tj, 2 * tj), 1)
    for k in range(TILES_PER_STEP):
        pltpu.make_async_copy(ys_ref.at[pl.ds(0, 2 * tj)], gbufs[k], sems.at[k]).wait()
        ahead = k + GATHER_AHEAD
        if ahead < TILES_PER_STEP:
            fetch(tinfo_ref, ahead, ahead)
        else:
            fetch(tnext_ref, ahead - TILES_PER_STEP, ahead - TILES_PER_STEP)
        rows = _unpack_bf16_pairs(_load_row_tiles(gbufs[k])).astype(bf16)
        v = vec_ref[k * tj:(k + 1) * tj, :]
        pick = jnp.concatenate([(slot == v[:, 2 + c:3 + c].astype(jnp.int32)).astype(bf16) for c in range(2)],
                               axis=0)
        picked = jnp.dot(pick, rows, preferred_element_type=f32)
        y = h2_ref[k * tj:(k + 1) * tj, :] + v[:, 0:1] * picked[:tj] + v[:, 1:2] * picked[tj:]
        out = _rms(y, g_ref[...])
        stage = stages[k // 2]
        for q in range(stage.shape[0]):
            stage[q, pl.ds(k % 2, tj, stride=2), :] = out[:, q * LANES:(q + 1) * LANES]
        if k % 2 == 1:
            r0 = (k // 2) * 2 * tj
            for q in range(stage.shape[0]):
                o_ref[0, r0:r0 + 2 * tj, q * LANES:(q + 1) * LANES] = stage[q]


def _combine(tinfo, cnt, h2, vec, final_g, ys, B, L):
    T, D = h2.shape
    ntiles = tinfo.shape[0]
    tj = T // ntiles
    nt = TILES_PER_STEP
    assert GATHER_AHEAD < nt and nt % 2 == 0 and (L // tj) % nt == 0
    nsteps = ntiles // nt
    per_seq = L // (nt * tj)
    g = final_g.reshape(1, D)
    gbuf = pltpu.VMEM(_row_tiles_shape(2 * tj, D), jnp.uint32)
    return pl.pallas_call(
        _combine_kernel,
        grid=(nsteps,),
        in_specs=[pl.BlockSpec((nt, SUBLANES, LANES), lambda i: (i, 0, 0), memory_space=pltpu.SMEM),
                  pl.BlockSpec((nt, SUBLANES, LANES), lambda i: (i + 1, 0, 0), memory_space=pltpu.SMEM),
                  pl.BlockSpec(memory_space=pltpu.SMEM),
                  pl.BlockSpec((nt * tj, D), lambda i: (i, 0)),
                  pl.BlockSpec((nt * tj, LANES), lambda i: (i, 0)),
                  pl.BlockSpec((1, D), lambda i: (0, 0)),
                  pl.BlockSpec(memory_space=pl.ANY)],
        out_specs=pl.BlockSpec((1, nt * tj, D), lambda i: (i // per_seq, i % per_seq, 0)),
        out_shape=jax.ShapeDtypeStruct((B, L, D), f32),
        scratch_shapes=[[gbuf] * nt,
                        [pltpu.VMEM((D // LANES, 2 * tj, LANES), f32)] * (nt // 2),
                        pltpu.SMEM((N_EXPERTS,), jnp.int32), pltpu.SemaphoreType.DMA((nt,))],
        compiler_params=_cparams(("arbitrary",)),
        name="moe_combine",
    )(tinfo, jnp.pad(tinfo, ((0, nt), (0, 0), (0, 0))), cnt, h2, vec, g, ys)


def kernel(x, norm1_g, w_in, conv_w, conv_b, f_w_in, f_b_in, f_w_mid, f_b_mid, f_freq, f_w_out, f_bias,
           mix_g, w_out, norm2_g, w_group, b_group, w_router, b_router, w_gate, w_up, w_down, final_g):
    B, L, D = x.shape
    depth = norm1_g.shape[0]
    assert depth == 1, "the final RMSNorm is fused into the single layer's MoE combine"
    C = D // 2
    Lh = L // 2
    N = B * C
    T = B * L
    n_rows = T * 2 + N_EXPERTS * MOE_ROWS
    cfwd, cinv = _dft_consts(L)
    cfou = _fourier_consts(L)
    i = 0
    wcat = _prep_win(w_in[i], C)
    hcat, asum = _hyena_filters(L, C, f_w_in[i], f_b_in[i], f_w_mid[i], f_b_mid[i], f_freq[i], f_w_out[i])
    kspec, k0 = _filter_spectrum(cfwd, hcat, asum, L, C)
    z, x0, u4 = _inproj(x, norm1_g[i], wcat, conv_w[i], conv_b[i], C)
    z = z.reshape(2, Lh, N)
    x0 = x0.reshape(2, Lh, N)
    u4 = u4.reshape(4, Lh, N)
    s = _conv_fwd(cfwd, z, kspec, k0, C)
    yh = _conv_inv(cinv, s, x0, z, f_bias[i], C)
    yf = _fourier_dft(cfou, u4, L, C)
    wr = jnp.concatenate([jnp.transpose(w_router[i], (1, 0, 2)).reshape(D, N_EXPERTS), w_group[i]], axis=1)
    wr = jnp.pad(wr, ((0, 0), (0, LANES - wr.shape[1])))
    br = jnp.pad(jnp.concatenate([b_router[i].reshape(-1), b_group[i]]), (0, LANES - N_EXPERTS - N_GROUPS))
    h2, tok, pos, vec, tinfo, cnt = _outproj_router(yh, yf, x, mix_g[i], w_out[i].astype(bf16), norm2_g[i],
                                                     wr, br.reshape(1, LANES), C)
    xs, first_block, n_blocks = _dispatch(tinfo, cnt, tok, pos, n_rows)
    ys = _experts(first_block, n_blocks, xs, w_gate[i], w_up[i], w_down[i])
    return _combine(tinfo, cnt, h2, vec, final_g, ys, B, L)
```

```python
import functools
import math

import jax
import jax.numpy as jnp
from jax import lax
from jax.experimental import pallas as pl
from jax.experimental.pallas import tpu as pltpu

HEAD_DIM = 64
N_GROUPS = 4
EXPERTS_PER_GROUP = 8
N_EXPERTS = N_GROUPS * EXPERTS_PER_GROUP
FILTER_BANDS = 16
DECAY_FAST_PCT = 0.3
DECAY_SLOW_PCT = 1.5
DECAY_TARGET = 1e-2
EPS = 1e-6

LANES = 128
SUBLANES = 8
MXU_DIM = 256
CONV_ROWS = 512
CONV_COLS = 512
FOURIER_COLS = 256
MOE_ROWS = 512
EXPERT_ROW_GROUPS = 2
ROW_DMA_PRIORITY = 1
ROW_TILES_PER_STEP = 2
TILES_PER_STEP = 4
GATHER_AHEAD = 2
SEG_SPLIT = 32
VMEM_LIMIT = 56 * 1024 * 1024

f32 = jnp.float32
bf16 = jnp.bfloat16
HI = lax.Precision.HIGHEST


def _cparams(sem):
    return pltpu.CompilerParams(dimension_semantics=sem, vmem_limit_bytes=VMEM_LIMIT)


def _tile(pref, n):
    return min(pref, n)


_SIN_TERMS = tuple((-1.0) ** k / math.factorial(2 * k + 1) for k in range(7))
_COS_TERMS = tuple((-1.0) ** k / math.factorial(2 * k) for k in range(8))


def _cos_sin_of_turn_fraction(prod, m):
    num = prod & (m - 1)
    quadrant = num // (m // 4)
    x = (num & (m // 4 - 1)).astype(f32) * (2.0 * math.pi / m)
    x2 = x * x
    s = jnp.full_like(x, _SIN_TERMS[-1])
    for c in _SIN_TERMS[-2::-1]:
        s = s * x2 + c
    s = s * x
    c_ = jnp.full_like(x, _COS_TERMS[-1])
    for c in _COS_TERMS[-2::-1]:
        c_ = c_ * x2 + c
    odd = (quadrant & 1) == 1
    cos = jnp.where(odd, s, c_)
    sin = jnp.where(odd, c_, s)
    cos = jnp.where((quadrant == 1) | (quadrant == 2), -cos, cos)
    sin = jnp.where(quadrant >= 2, -sin, sin)
    return cos, sin


def _cos_sin_blocks(a, s, b0, m, ncb, d):
    tc, ts = _cos_sin_of_turn_fraction(a * (s * d), m)
    b0_lane = jnp.zeros(d.shape, jnp.int32)
    for cb in range(ncb):
        b0_lane = jnp.where(d == cb, b0(cb), b0_lane)
    ac, asn = _cos_sin_of_turn_fraction(a * b0_lane, m)
    for cb in range(ncb):
        ca = ac[:, cb:cb + 1]
        sa = asn[:, cb:cb + 1]
        yield cb, ca * tc - sa * ts, sa * tc + ca * ts


def _dft_consts_kernel(fwd_ref, inv_ref, *, L):
    Lh = L // 2
    tr = fwd_ref.shape[1]
    r = lax.broadcasted_iota(jnp.int32, (tr, LANES), 0) + pl.program_id(0) * tr
    d = lax.broadcasted_iota(jnp.int32, (tr, LANES), 1)
    alt_r = (1 - 2 * (r & 1)).astype(f32)
    alt_d = (1 - 2 * (d & 1)).astype(f32)
    ncb = Lh // LANES
    fams = [
        (r, 1, lambda cb: cb * LANES, L),
        (r, 2, lambda cb: 2 * cb * LANES + 1, 2 * L),
        (2 * r + 1, 1, lambda cb: cb * LANES, 2 * L),
    ]
    for fi, (a, s, b0, m) in enumerate(fams):
        for cb, c, sn in _cos_sin_blocks(a, s, b0, m, ncb, d):
            sl = slice(cb * LANES, (cb + 1) * LANES)
            if fi == 0:
                fwd_ref[0, :, sl] = c.astype(bf16)
                inv_ref[0, :, sl] = c.astype(bf16)
                fwd_ref[1, :, sl] = jnp.where(r == 0, alt_d, sn).astype(bf16)
                nst = -sn
                if cb == 0:
                    nst = jnp.where(d == 0, -alt_r, nst)
                inv_ref[1, :, sl] = nst.astype(bf16)
            elif fi == 1:
                fwd_ref[2, :, sl] = c.astype(bf16)
                fwd_ref[3, :, sl] = jnp.where(r == 0, alt_d, sn).astype(bf16)
            else:
                inv_ref[2, :, sl] = c.astype(bf16)
                nst = -sn
                if cb == 0:
                    nst = jnp.where(d == 0, -alt_r, nst)
                inv_ref[3, :, sl] = nst.astype(bf16)


def _dft_consts(L):
    Lh = L // 2
    tr = _tile(256, Lh)
    shp = jax.ShapeDtypeStruct((4, Lh, Lh), bf16)
    spec = pl.BlockSpec((4, tr, Lh), lambda i: (0, i, 0))
    return pl.pallas_call(
        functools.partial(_dft_consts_kernel, L=L),
        grid=(Lh // tr,),
        out_specs=[spec, spec],
        out_shape=[shp, shp],
        compiler_params=_cparams(("parallel",)),
        name="dft_consts",
    )()


def _fourier_consts_kernel(o_ref, *, L):
    n = L // 4
    tr = o_ref.shape[0]
    r = lax.broadcasted_iota(jnp.int32, (tr, LANES), 0) + pl.program_id(0) * tr
    d = lax.broadcasted_iota(jnp.int32, (tr, LANES), 1)

    def col_l(cb):
        parity, j0 = divmod(cb * LANES, n // 2)
        return 2 * j0 + parity

    for cb, c, sn in _cos_sin_blocks(r, 2, col_l, n, n // LANES, d):
        o_ref[:, cb * LANES:(cb + 1) * LANES] = c.astype(bf16)
        o_ref[:, n + cb * LANES:n + (cb + 1) * LANES] = (-sn).astype(bf16)


def _fourier_consts(L):
    n = L // 4
    tr = _tile(256, n)
    return pl.pallas_call(
        functools.partial(_fourier_consts_kernel, L=L),
        grid=(n // tr,),
        out_specs=pl.BlockSpec((tr, 2 * n), lambda i: (i, 0)),
        out_shape=jax.ShapeDtypeStruct((n, 2 * n), bf16),
        compiler_params=_cparams(("parallel",)),
        name="fourier_consts",
    )()


def _prep_win_kernel(w_ref, o_ref, *, C):
    w = w_ref[...]
    i = lax.broadcasted_iota(jnp.int32, (LANES, LANES), 0)
    j = lax.broadcasted_iota(jnp.int32, (LANES, LANES), 1)
    same = (i // HEAD_DIM) == (j // HEAD_DIM)
    cos, sin = _cos_sin_of_turn_fraction((i % HEAD_DIM) * (j % HEAD_DIM), HEAD_DIM)
    bc = jnp.where(same, cos, 0.0)
    bs = jnp.where(same, sin, 0.0)
    o_ref[:, :3 * C] = w[:, :3 * C].astype(bf16)
    for k in range(0, C, LANES):
        wf = w[:, 3 * C + k:3 * C + k + LANES]
        o_ref[:, 3 * C + k:3 * C + k + LANES] = jnp.dot(wf, bc, precision=HI, preferred_element_type=f32).astype(bf16)
        o_ref[:, 4 * C + k:4 * C + k + LANES] = jnp.dot(wf, bs, precision=HI, preferred_element_type=f32).astype(bf16)


def _prep_win(w_in, C):
    D = w_in.shape[0]
    tr = _tile(256, D)
    return pl.pallas_call(
        functools.partial(_prep_win_kernel, C=C),
        grid=(D // tr,),
        in_specs=[pl.BlockSpec((tr, 4 * C), lambda i: (i, 0))],
        out_specs=pl.BlockSpec((tr, 5 * C), lambda i: (i, 0)),
        out_shape=jax.ShapeDtypeStruct((D, 5 * C), bf16),
        compiler_params=_cparams(("parallel",)),
        name="prep_win",
    )(w_in)


def _filter_kernel(win_ref, bin_ref, wmid_ref, bmid_ref, freq_ref, wout_ref, h_ref, asum_ref, *, L, C):
    tr = h_ref.shape[1]
    step = pl.program_id(0)
    lane = lax.broadcasted_iota(jnp.int32, (tr, LANES), 1)
    row = lax.broadcasted_iota(jnp.int32, (tr, LANES), 0) + step * tr
    band_i = jnp.where(lane <= FILTER_BANDS, lane - 1, lane - 1 - FILTER_BANDS)
    band = 1e-4 + band_i.astype(f32) * ((FILTER_BANDS - 1 - 1e-4) / (FILTER_BANDS - 1))
    ch = lax.broadcasted_iota(jnp.int32, (tr, 2 * C), 1) % C
    max_decay = math.log(DECAY_TARGET) / DECAY_FAST_PCT
    min_decay = math.log(DECAY_TARGET) / DECAY_SLOW_PCT
    delta = jnp.abs(min_decay + ch.astype(f32) * ((max_decay - min_decay) / (C - 1)))
    is_bwd = lax.broadcasted_iota(jnp.int32, (tr, 2 * C), 1) >= C
    row_c = lax.broadcasted_iota(jnp.int32, (tr, 2 * C), 0) + step * tr

    @pl.when(step == 0)
    def _():
        asum_ref[...] = jnp.zeros_like(asum_ref)

    total = jnp.zeros((1, 2 * C), f32)
    for p in range(2):
        pos = (2 * row + p).astype(f32)
        t = pos * (1.0 / (L - 1))
        w = pos * (2.0 * math.pi / L)
        fw = band * w
        z = jnp.where(lane == 0, t,
                      jnp.where(lane <= FILTER_BANDS, jnp.cos(fw),
                                jnp.where(lane <= 2 * FILTER_BANDS, -jnp.sin(fw), 0.0)))
        h = jnp.sin(freq_ref[0:1, :] * (jnp.dot(z, win_ref[...], precision=HI, preferred_element_type=f32)
                                       + bin_ref[...]))
        for i in range(wmid_ref.shape[0]):
            h = jnp.sin(freq_ref[i + 1:i + 2, :]
                        * (jnp.dot(h, wmid_ref[i], precision=HI, preferred_element_type=f32)
                           + bmid_ref[i:i + 1, :]))
        out = jnp.dot(h, wout_ref[...], precision=HI, preferred_element_type=f32)
        tc = (2 * row_c + p).astype(f32) * (1.0 / (L - 1))
        out = out * jnp.exp(-tc * delta)
        if p == 0:
            out = jnp.where(is_bwd & (row_c == 0), 0.0, out)
        h_ref[p] = out.astype(bf16)
        total = total + jnp.sum(jnp.abs(out), axis=0, keepdims=True)
    asum_ref[0:1, :] += total


def _hyena_filters(L, C, f_w_in, f_b_in, f_w_mid, f_b_mid, f_freq, f_w_out):
    Lh = L // 2
    tr = _tile(256, Lh)
    order = f_w_in.shape[1]
    win = jnp.pad(f_w_in, ((0, LANES - f_w_in.shape[0]), (0, 0)))
    full = lambda a: pl.BlockSpec(a.shape, lambda i: (0,) * a.ndim)
    args = (win, f_b_in.reshape(1, order), f_w_mid, f_b_mid, f_freq, f_w_out)
    return pl.pallas_call(
        functools.partial(_filter_kernel, L=L, C=C),
        grid=(Lh // tr,),
        in_specs=[full(a) for a in args],
        out_specs=[pl.BlockSpec((2, tr, 2 * C), lambda i: (0, i, 0)),
                   pl.BlockSpec((SUBLANES, 2 * C), lambda i: (0, 0))],
        out_shape=[jax.ShapeDtypeStruct((2, Lh, 2 * C), bf16),
                   jax.ShapeDtypeStruct((SUBLANES, 2 * C), f32)],
        compiler_params=_cparams(("arbitrary",)),
        name="hyena_filters",
    )(*args)


def _fwd_products(c_ref, x_ref):
    tm, k = c_ref.shape[1:]
    even = jnp.dot(c_ref[0:2].reshape(2 * tm, k), x_ref[0], preferred_element_type=f32)
    odd = jnp.dot(c_ref[2:4].reshape(2 * tm, k), x_ref[1], preferred_element_type=f32)
    return even[:tm], even[tm:], odd[:tm], odd[tm:]


def _filter_spectrum_kernel(c_ref, h_ref, asum_ref, k_ref, k0_ref, *, L, C):
    ce, se, co, so = _fwd_products(c_ref, h_ref)
    a = asum_ref[0:1, :]
    scale = 1.0 / ((a[:, :C] + a[:, C:]) * L)
    F = lambda v: v[:, :C]
    G = lambda v: v[:, C:]
    k_ref[0] = (F(ce) + F(co) + G(ce) + G(co)) * scale
    k_ref[1] = (-(F(se) + F(so)) + (G(se) + G(so))) * scale
    k_ref[2] = (F(ce) - F(co) + G(ce) - G(co)) * scale
    k_ref[3] = ((F(se) - F(so)) - (G(se) - G(so))) * scale

    @pl.when(pl.program_id(0) == 0)
    def _():
        k0_ref[...] = jnp.zeros_like(k0_ref)
        k0_ref[0:1, :] = ((F(se) + G(se)) * scale)[0:1, :]
        k0_ref[1:2, :] = ((F(so) - G(so)) * scale)[0:1, :]


def _filter_spectrum(cfwd, hcat, asum, L, C):
    Lh = L // 2
    tm = _tile(256, Lh)
    return pl.pallas_call(
        functools.partial(_filter_spectrum_kernel, L=L, C=C),
        grid=(Lh // tm,),
        in_specs=[pl.BlockSpec((4, tm, Lh), lambda i: (0, i, 0)),
                  pl.BlockSpec((2, Lh, 2 * C), lambda i: (0, 0, 0)),
                  pl.BlockSpec((SUBLANES, 2 * C), lambda i: (0, 0))],
        out_specs=[pl.BlockSpec((4, tm, C), lambda i: (0, i, 0)),
                   pl.BlockSpec((SUBLANES, C), lambda i: (0, 0))],
        out_shape=[jax.ShapeDtypeStruct((4, Lh, C), f32),
                   jax.ShapeDtypeStruct((SUBLANES, C), f32)],
        compiler_params=_cparams(("arbitrary",)),
        name="filter_spectrum",
    )(cfwd, hcat, asum)


def _conv_fwd_kernel(c_ref, z_ref, k_ref, k0_ref, s_ref):
    products = _fwd_products(c_ref, z_ref)
    C = k_ref.shape[2]
    first = pl.program_id(1) == 0
    for h in range(z_ref.shape[2] // C):
        cols = slice(h * C, (h + 1) * C)
        ce, se, co, so = (v[:, cols] for v in products)
        zsr, zsi, zdr, zdi = ce + co, -(se + so), ce - co, se - so
        ksr, ksi, kdr, kdi = k_ref[0], k_ref[1], k_ref[2], k_ref[3]
        psr = zsr * ksr - zsi * ksi
        psi = zsr * ksi + zsi * ksr
        pdr = zdr * kdr - zdi * kdi
        pdi = zdr * kdi + zdi * kdr
        gen = (psr + pdr, psi - pdi, psr - pdr, psi + pdi)

        @pl.when(first)
        def _(cols=cols, ce=ce, se=se, so=so, zsr=zsr, zdr=zdr, ksr=ksr, kdr=kdr, gen=gen):
            row0 = lax.broadcasted_iota(jnp.int32, ce.shape, 0) == 0
            p0 = zsr * ksr
            pl_ = zdr * kdr
            ak, bk = k0_ref[0:1, :], k0_ref[1:2, :]
            phr = se * ak - so * bk
            phi = -(se * bk + so * ak)
            spec = (0.5 * (p0 + pl_), -phr, 0.5 * (p0 - pl_), phi)
            for i in range(4):
                s_ref[i, :, cols] = jnp.where(row0, spec[i], gen[i]).astype(bf16)

        @pl.when(jnp.logical_not(first))
        def _(cols=cols, gen=gen):
            for i in range(4):
                s_ref[i, :, cols] = gen[i].astype(bf16)


def _conv_fwd(cfwd, z, kspec, k0, C):
    _, Lh, N = z.shape
    tm = _tile(CONV_ROWS, Lh)
    tn = _tile(CONV_COLS, N)
    return pl.pallas_call(
        _conv_fwd_kernel,
        grid=(N // tn, Lh // tm),
        in_specs=[pl.BlockSpec((4, tm, Lh), lambda n, i: (0, i, 0)),
                  pl.BlockSpec((2, Lh, tn), lambda n, i: (0, 0, n)),
                  pl.BlockSpec((4, tm, C), lambda n, i: (0, i, 0)),
                  pl.BlockSpec((SUBLANES, C), lambda n, i: (0, 0))],
        out_specs=pl.BlockSpec((4, tm, tn), lambda n, i: (0, i, n)),
        out_shape=jax.ShapeDtypeStruct((4, Lh, N), bf16),
        compiler_params=_cparams(("parallel", "arbitrary")),
        name="conv_fwd",
    )(cfwd, z, kspec, k0)


def _pair_products(c_ref, r_ref):
    ev = (jnp.dot(c_ref[0], r_ref[0], preferred_element_type=f32)
          + jnp.dot(c_ref[1], r_ref[1], preferred_element_type=f32))
    od = (jnp.dot(c_ref[2], r_ref[2], preferred_element_type=f32)
          + jnp.dot(c_ref[3], r_ref[3], preferred_element_type=f32))
    return ev, od


def _conv_inv_kernel(c_ref, s_ref, x0_ref, z_ref, fb_ref, y_ref):
    conv = _pair_products(c_ref, s_ref)
    C = fb_ref.shape[1]
    fb = fb_ref[...]
    for p in range(2):
        for h in range(z_ref.shape[2] // C):
            cols = slice(h * C, (h + 1) * C)
            zp = z_ref[p, :, cols].astype(f32)
            y_ref[p, :, cols] = (x0_ref[p, :, cols].astype(f32) * (conv[p][:, cols] + zp * fb)).astype(bf16)


def _conv_inv(cinv, s, x0, z, f_bias, C):
    _, Lh, N = s.shape
    tm = _tile(CONV_ROWS, Lh)
    tn = _tile(CONV_COLS, N)
    return pl.pallas_call(
        _conv_inv_kernel,
        grid=(N // tn, Lh // tm),
        in_specs=[pl.BlockSpec((4, tm, Lh), lambda n, i: (0, i, 0)),
                  pl.BlockSpec((4, Lh, tn), lambda n, i: (0, 0, n)),
                  pl.BlockSpec((2, tm, tn), lambda n, i: (0, i, n)),
                  pl.BlockSpec((2, tm, tn), lambda n, i: (0, i, n)),
                  pl.BlockSpec((1, C), lambda n, i: (0, 0))],
        out_specs=pl.BlockSpec((2, tm, tn), lambda n, i: (0, i, n)),
        out_shape=jax.ShapeDtypeStruct((2, Lh, N), bf16),
        compiler_params=_cparams(("parallel", "arbitrary")),
        name="conv_inv",
    )(cinv, s, x0, z, f_bias.reshape(1, C))


def _fourier_kernel(c_ref, u_ref, y_ref, rhs, stage, tw1, tw2, *, L, scale):
    Lh, n, H = L // 2, L // 4, L // 8
    tn = u_ref.shape[2]

    @pl.when(pl.program_id(0) == 0)
    def _():
        r1 = lax.broadcasted_iota(jnp.int32, (Lh, LANES), 0)
        tw1[0], tw1[1] = _cos_sin_of_turn_fraction(2 * (r1 % n) + r1 // n, L)
        r2 = lax.broadcasted_iota(jnp.int32, (n, LANES), 0)
        tw2[0], tw2[1] = _cos_sin_of_turn_fraction(2 * (r2 % H) + r2 // H, Lh)

    def rotate(a, b, c, s):
        return c * a - s * b, s * a + c * b

    for q in range(tn // LANES):
        cols = slice(q * LANES, (q + 1) * LANES)
        for p in range(2):
            first = slice(p * n, p * n + H)
            second = slice(p * n + H, (p + 1) * n)
            out = slice(p * H, (p + 1) * H)
            out_b = slice(n + p * H, n + (p + 1) * H)
            c2, s2 = tw2[0, out, :], tw2[1, out, :]

            def emit(sub, a0, b0, a1, b1):
                rhs[sub, out, cols] = (a0 + a1).astype(bf16)
                rhs[sub, out_b, cols] = (b0 + b1).astype(bf16)
                da, db = rotate(a0 - a1, b0 - b1, c2, s2)
                rhs[sub + 1, out, cols] = da.astype(bf16)
                rhs[sub + 1, out_b, cols] = db.astype(bf16)

            ld = lambda k, rows: u_ref[k, rows, cols].astype(f32)
            emit(0, ld(0, first), ld(1, first), ld(0, second), ld(1, second))
            odd = [rotate(ld(2, rows), ld(3, rows), tw1[0, rows, :], tw1[1, rows, :]) for rows in (first, second)]
            emit(2, odd[0][0], odd[0][1], odd[1][0], odd[1][1])

    for parity in range(2):
        pair = [jnp.dot(c_ref[...], rhs[2 * parity + h], preferred_element_type=f32) * scale for h in range(2)]
        for q in range(tn // LANES):
            cols = slice(q * LANES, (q + 1) * LANES)
            for h in range(2):
                stage[q, pl.ds(h, n, stride=2), :] = pair[h][:, cols]
            y_ref[parity, :, cols] = stage[q].astype(bf16)


def _fourier_dft(cfou, u4, L, C):
    _, Lh, N = u4.shape
    n = L // 4
    tn = _tile(FOURIER_COLS, C)
    return pl.pallas_call(
        functools.partial(_fourier_kernel, L=L, scale=1.0 / math.sqrt(L * HEAD_DIM)),
        grid=(N // tn,),
        in_specs=[pl.BlockSpec((n, 2 * n), lambda j: (0, 0)),
                  pl.BlockSpec((4, Lh, tn), lambda j: (0, 0, j))],
        out_specs=pl.BlockSpec((2, Lh, tn), lambda j: (0, 0, j)),
        out_shape=jax.ShapeDtypeStruct((2, Lh, N), bf16),
        scratch_shapes=[pltpu.VMEM((4, 2 * n, tn), bf16), pltpu.VMEM((tn // LANES, Lh, LANES), f32),
                        pltpu.VMEM((2, Lh, LANES), f32), pltpu.VMEM((2, n, LANES), f32)],
        compiler_params=_cparams(("arbitrary",)),
        name="fourier_dft",
    )(cfou, u4)


def _rms(x, g):
    return x * lax.rsqrt(jnp.mean(x * x, axis=-1, keepdims=True) + EPS) * g


def _stage_rows(chunks, x):
    for k in range(chunks.shape[0]):
        chunks[k] = x[:, k * LANES:(k + 1) * LANES]


def _parity_rows(chunks, p, n):
    return jnp.concatenate([chunks[k, pl.ds(p, n, stride=2), :] for k in range(chunks.shape[0])], axis=1)


def _inproj_kernel(xa_ref, xb_ref, pa_ref, na_ref, pb_ref, nb_ref, g_ref, w_ref, cw_ref, cb_ref,
                   z_ref, x0_ref, u_ref, stage, *, C, D):
    tj = xa_ref.shape[1] // 2
    jt = pl.program_id(1)
    n_half = pl.num_programs(1)
    g = g_ref[...]
    w = w_ref[...]
    cw0, cw1, cw2 = cw_ref[0:1, :], cw_ref[1:2, :], cw_ref[2:3, :]
    cb = cb_ref[...]
    row = lax.broadcasted_iota(jnp.int32, (tj, 3 * C), 0)
    four = []
    halves = ((xa_ref, pa_ref, na_ref, jt == 0, False),
              (xb_ref, pb_ref, nb_ref, False, jt == n_half - 1))
    for h, (x_ref, p_ref, n_ref, at_start, at_end) in enumerate(halves):
        _stage_rows(stage, x_ref[0])
        lhs = jnp.concatenate([
            _rms(_parity_rows(stage, 0, tj), g).astype(bf16),
            _rms(_parity_rows(stage, 1, tj), g).astype(bf16),
            _rms(p_ref[0], g).astype(bf16),
            _rms(n_ref[0], g).astype(bf16)], axis=0)
        res = jnp.dot(lhs, w, preferred_element_type=f32)
        pe, po = res[:tj, :3 * C], res[tj:2 * tj, :3 * C]
        prev = res[2 * tj + SUBLANES - 1:2 * tj + SUBLANES, :3 * C]
        nxt = res[2 * tj + SUBLANES:2 * tj + SUBLANES + 1, :3 * C]
        if at_start is not False:
            prev = jnp.where(at_start, 0.0, prev)
        if at_end is not False:
            nxt = jnp.where(at_end, 0.0, nxt)
        po_dn = jnp.where(row == 0, prev, pltpu.roll(po, 1, 0))
        pe_up = jnp.where(row == tj - 1, nxt, pltpu.roll(pe, tj - 1, 0))
        uc = (cb + cw0 * po_dn + cw1 * pe + cw2 * po,
              cb + cw0 * pe + cw1 * po + cw2 * pe_up)
        for p in range(2):
            x0_ref[p, h] = uc[p][:, :C].astype(bf16)
            z_ref[p, h] = (uc[p][:, C:2 * C] * uc[p][:, 2 * C:]).astype(bf16)
        four.append((res[:tj, 3 * C:], res[tj:2 * tj, 3 * C:]))
    for p in range(2):
        fa, fb = four[0][p], four[1][p]
        u_ref[0, p] = (fa[:, :C] + fb[:, :C]).astype(bf16)
        u_ref[1, p] = (fa[:, C:] + fb[:, C:]).astype(bf16)
        u_ref[2, p] = (fa[:, :C] - fb[:, :C]).astype(bf16)
        u_ref[3, p] = (fa[:, C:] - fb[:, C:]).astype(bf16)


def _inproj(x, norm_g, wcat, conv_w, conv_b, C):
    B, L, D = x.shape
    Lq = L // 4
    tj = _tile(256, Lq)
    nq = Lq // tj
    N = B * C
    hb = 2 * tj // SUBLANES
    last = L // SUBLANES - 1
    main = lambda off: pl.BlockSpec((1, 2 * tj, D), lambda b, j: (b, j + off, 0))
    prev = lambda off: pl.BlockSpec((1, SUBLANES, D), lambda b, j: (b, jnp.maximum((j + off) * hb - 1, 0), 0))
    nxt = lambda off: pl.BlockSpec((1, SUBLANES, D), lambda b, j: (b, jnp.minimum((j + off + 1) * hb, last), 0))
    full = lambda a: pl.BlockSpec(a.shape, lambda b, j: (0,) * a.ndim)
    g = norm_g.reshape(1, D)
    cb = conv_b.reshape(1, 3 * C)
    return pl.pallas_call(
        functools.partial(_inproj_kernel, C=C, D=D),
        grid=(B, nq),
        in_specs=[main(0), main(nq), prev(0), nxt(0), prev(nq), nxt(nq),
                  full(g), full(wcat), full(conv_w), full(cb)],
        out_specs=[pl.BlockSpec((2, 2, tj, C), lambda b, j: (0, 0, j, b)),
                   pl.BlockSpec((2, 2, tj, C), lambda b, j: (0, 0, j, b)),
                   pl.BlockSpec((4, 2, tj, C), lambda b, j: (0, 0, j, b))],
        out_shape=[jax.ShapeDtypeStruct((2, 2, Lq, N), bf16),
                   jax.ShapeDtypeStruct((2, 2, Lq, N), bf16),
                   jax.ShapeDtypeStruct((4, 2, Lq, N), bf16)],
        scratch_shapes=[pltpu.VMEM((D // LANES, 2 * tj, LANES), f32)],
        compiler_params=_cparams(("parallel", "arbitrary")),
        name="inproj",
    )(x, x, x, x, x, x, g, wcat, conv_w, cb)


def _pack_bf16_pairs(v, is_bf16_exact=False):
    k = v.shape[1] // 2
    rounded = v if is_bf16_exact else v.astype(bf16).astype(f32)
    hi = pltpu.bitcast(rounded[:, :k], jnp.uint32)
    lo = pltpu.bitcast(rounded[:, k:], jnp.uint32)
    return hi | (lo >> 16)


def _unpack_bf16_pairs(u):
    hi = pltpu.bitcast(u & jnp.uint32(0xFFFF0000), f32)
    lo = pltpu.bitcast(u << 16, f32)
    return jnp.concatenate([hi, lo], axis=1)


def _row_tiles_shape(n, D):
    return (n, D // (2 * LANES), LANES)


def _store_row_tiles(ref, packed, row0=0):
    total, per_row, _ = ref.shape
    n = packed.shape[0]
    flat = ref.reshape(total * per_row, LANES)
    for s in range(per_row):
        flat[pl.ds(row0 * per_row + s, n, stride=per_row), :] = packed[:, s * LANES:(s + 1) * LANES]


def _load_row_tiles(ref, row0=0, n=None):
    total, per_row, _ = ref.shape
    n = total if n is None else n
    flat = ref.reshape(total * per_row, LANES)
    return jnp.concatenate([flat[pl.ds(row0 * per_row + s, n, stride=per_row), :] for s in range(per_row)],
                           axis=1)


def _outproj_router_kernel(yh_ref, yf_ref, x_ref, mg_ref, wo_ref, g2_ref, wr_ref, br_ref,
                           h2_ref, tok_ref, pos_ref, vec_ref, tinfo_ref, cnt_ref, carry_ref, stage, *, C):
    n_sub = pos_ref.shape[0]
    tj = pos_ref.shape[2]
    _stage_rows(stage, x_ref[0])

    @pl.when((pl.program_id(0) == 0) & (pl.program_id(1) == 0))
    def _():
        carry_ref[...] = jnp.zeros_like(carry_ref)

    li = lax.broadcasted_iota(jnp.int32, (LANES, LANES), 0)
    lj = lax.broadcasted_iota(jnp.int32, (LANES, LANES), 1)
    hw = _tile(MXU_DIM, C)
    ai = lax.broadcasted_iota(jnp.int32, (hw, hw), 0)
    aj = lax.broadcasted_iota(jnp.int32, (hw, hw), 1)
    avg = jnp.where(ai // HEAD_DIM == aj // HEAD_DIM, 1.0 / HEAD_DIM, 0.0).astype(bf16)
    upper = (li < lj).astype(bf16)
    ri = lax.broadcasted_iota(jnp.int32, (tj, tj), 0)
    ci = lax.broadcasted_iota(jnp.int32, (tj, tj), 1)
    tri = (ci < ri).astype(bf16)
    wr = wr_ref[...]
    wr_hi = wr.astype(bf16)
    wr_lo = (wr - wr_hi.astype(f32)).astype(bf16)
    wr_both = jnp.concatenate([wr_hi, wr_lo], axis=1)
    lane = lax.broadcasted_iota(jnp.int32, (n_sub * tj, LANES), 1).astype(f32)
    sub = lax.broadcasted_iota(jnp.int32, (SUBLANES, LANES), 0)
    neg = jnp.float32(-jnp.inf)
    big = jnp.float32(LANES)

    def first_argmax(vals, mask):
        v = jnp.where(mask, vals, neg)
        m = jnp.max(v, axis=1, keepdims=True)
        idx = jnp.min(jnp.where(mask & (v == m), lane, big), axis=1, keepdims=True)
        return m, idx

    normed = []
    for s in range(n_sub):
        r, p = divmod(s, 2)
        parts = []
        for ref in (yh_ref, yf_ref):
            y = ref[p, r * tj:(r + 1) * tj, :].astype(f32)
            ysq = (y * y).astype(bf16)
            ms = jnp.concatenate([jnp.dot(ysq[:, k:k + hw], avg, preferred_element_type=f32)
                                  for k in range(0, C, hw)], axis=1)
            parts.append(y * lax.rsqrt(ms + EPS))
        normed.append((jnp.concatenate(parts, axis=1) * mg_ref[...]).astype(bf16))
    proj = jnp.dot(jnp.concatenate(normed, axis=0), wo_ref[...], preferred_element_type=f32)
    t_hi, t_lo = [], []
    for s in range(n_sub):
        r, p = divmod(s, 2)
        rows = slice(s * tj, (s + 1) * tj)
        h2 = _parity_rows(stage, r * 2 * tj + p, tj) + proj[rows]
        h2_ref[rows, :] = h2
        tok = _rms(h2, g2_ref[...])
        hi = tok.astype(bf16)
        tok_ref[rows, :] = hi
        t_hi.append(hi)
        t_lo.append((tok - hi.astype(f32)).astype(bf16))
    hi_terms = jnp.dot(jnp.concatenate(t_hi, axis=0), wr_both, preferred_element_type=f32)
    logits = (hi_terms[:, :LANES] + hi_terms[:, LANES:]
              + jnp.dot(jnp.concatenate(t_lo, axis=0), wr_hi, preferred_element_type=f32) + br_ref[...])

    gmask = (lane >= N_EXPERTS) & (lane < N_EXPERTS + N_GROUPS)
    gmax, gidx = first_argmax(logits, gmask)
    gsel = gidx - N_EXPERTS
    p_group = 1.0 / jnp.sum(jnp.where(gmask, jnp.exp(logits - gmax), 0.0), axis=1, keepdims=True)
    emask = (lane >= gsel * EXPERTS_PER_GROUP) & (lane < (gsel + 1) * EXPERTS_PER_GROUP)
    l0, e0 = first_argmax(logits, emask)
    l1, e1 = first_argmax(logits, emask & (lane != e0))
    ex = jnp.exp(l1 - l0)
    w0 = p_group / (1.0 + ex)
    w1 = p_group * ex / (1.0 + ex)

    oh0 = (lane == e0).astype(f32)
    oh1 = (lane == e1).astype(f32)
    both = (oh0 + oh1).astype(bf16)
    where, counts = [], []
    for s in range(n_sub):
        tile_both = both[s * tj:(s + 1) * tj, :]
        ranks = jnp.dot(tri, tile_both, preferred_element_type=f32)
        n_e = ranks[tj - 1:tj, :] + tile_both[tj - 1:tj, :].astype(f32)
        n_e8 = jnp.broadcast_to(n_e, (SUBLANES, LANES))
        tile_off = jnp.dot(n_e8.astype(bf16), upper, preferred_element_type=f32)[0:1, :]
        where.append(ranks + tile_off)
        counts.append(n_e8)
    where = jnp.concatenate(where, axis=0)
    pos0 = jnp.sum(oh0 * where, axis=1, keepdims=True)
    pos1 = jnp.sum(oh1 * where, axis=1, keepdims=True)
    cols = jnp.where(lane == 0, w0, jnp.where(lane == 1, w1,
                     jnp.where(lane == 2, pos0, jnp.where(lane == 3, pos1, 0.0))))
    vec_ref[...] = cols
    base = carry_ref[...]
    for s in range(n_sub):
        pos_ref[s] = cols[s * tj:(s + 1) * tj, :].T[:SUBLANES, :].astype(jnp.int32)
        long_seg = (jnp.max(counts[s], axis=1, keepdims=True) >= SEG_SPLIT).astype(f32)
        tinfo_ref[s] = jnp.where(sub == 0, counts[s], jnp.where(sub == 1, base, jnp.where(sub == 2, long_seg, 0.0))
                                 ).astype(jnp.int32)
        base = base + counts[s]
    carry_ref[...] = base
    cnt_ref[...] = base.astype(jnp.int32)


def _outproj_router(yh, yf, x, mix_g, wo, norm2_g, wr, br, C):
    B, L, D = x.shape
    Lh = L // 2
    tj = _tile(256, Lh)
    rt = ROW_TILES_PER_STEP
    ns = 2 * rt
    nj = Lh // (rt * tj)
    T = B * L
    ntiles = T // tj
    lin = lambda b, j: b * nj + j
    full = lambda a: pl.BlockSpec(a.shape, lambda b, j: (0,) * a.ndim)
    mg = mix_g.reshape(1, D)
    g2 = norm2_g.reshape(1, D)
    return pl.pallas_call(
        functools.partial(_outproj_router_kernel, C=C),
        grid=(B, nj),
        in_specs=[pl.BlockSpec((2, rt * tj, C), lambda b, j: (0, j, b)),
                  pl.BlockSpec((2, rt * tj, C), lambda b, j: (0, j, b)),
                  pl.BlockSpec((1, ns * tj, D), lambda b, j: (b, j, 0)),
                  full(mg), full(wo), full(g2), full(wr), full(br)],
        out_specs=[pl.BlockSpec((ns * tj, D), lambda b, j: (lin(b, j), 0)),
                   pl.BlockSpec((ns * tj, D), lambda b, j: (lin(b, j), 0)),
                   pl.BlockSpec((ns, SUBLANES, tj), lambda b, j: (lin(b, j), 0, 0)),
                   pl.BlockSpec((ns * tj, LANES), lambda b, j: (lin(b, j), 0)),
                   pl.BlockSpec((ns, SUBLANES, LANES), lambda b, j: (lin(b, j), 0, 0)),
                   pl.BlockSpec((SUBLANES, LANES), lambda b, j: (0, 0))],
        out_shape=[jax.ShapeDtypeStruct((T, D), f32),
                   jax.ShapeDtypeStruct((T, D), bf16),
                   jax.ShapeDtypeStruct((ntiles, SUBLANES, tj), jnp.int32),
                   jax.ShapeDtypeStruct((T, LANES), f32),
                   jax.ShapeDtypeStruct((ntiles, SUBLANES, LANES), jnp.int32),
                   jax.ShapeDtypeStruct((SUBLANES, LANES), jnp.int32)],
        scratch_shapes=[pltpu.VMEM((SUBLANES, LANES), f32), pltpu.VMEM((D // LANES, ns * tj, LANES), f32)],
        compiler_params=_cparams(("arbitrary", "arbitrary")),
        name="outproj_router",
    )(yh, yf, x, mg, wo, g2, wr, br)


def _padded(c):
    return ((c + MOE_ROWS - 1) // MOE_ROWS) * MOE_ROWS


def _pad_starts(cnt_ref, start_ref):
    def body(e, acc):
        start_ref[e] = acc
        return acc + _padded(cnt_ref[0, e])
    return lax.fori_loop(0, N_EXPERTS, body, jnp.int32(0))


def _segment_copies(n, src, src_off, dst, dst_off, sem, max_rows, wait, src_advances=True, min_rows=1):
    bit = max_rows
    while bit >= min_rows:
        take = n & bit

        @pl.when(take != 0)
        def _(src_off=src_off, dst_off=dst_off, bit=bit):
            cp = pltpu.make_async_copy(src.at[pl.ds(src_off, bit)], dst.at[pl.ds(dst_off, bit)], sem)
            if wait:
                cp.wait()
            else:
                cp.start(priority=ROW_DMA_PRIORITY)
        if src_advances:
            src_off = src_off + take
        dst_off = dst_off + take
        bit //= 2


def _tile_segments(tinfo_ref, t, start_ref, local, local_is_src, remote, sem, tj):
    def copies(long_part):
        off = 0
        for e in range(N_EXPERTS):
            n = tinfo_ref[t, 0, e]
            far = start_ref[e] + tinfo_ref[t, 1, e]
            if long_part:
                args = (n, off, far, tj, SEG_SPLIT)
            else:
                head = n & ~(SEG_SPLIT - 1)
                args = (n, off + head, far + head, SEG_SPLIT // 2, 1)
            n_, near_, far_, hi, lo = args
            if local_is_src:
                _segment_copies(n_, local, near_, remote, far_, sem, hi, wait=False, min_rows=lo)
            else:
                _segment_copies(n_, remote, far_, local, near_, sem, hi, wait=False, min_rows=lo)
            off = off + n

    copies(long_part=False)

    @pl.when(tinfo_ref[t, 2, 0] != 0)
    def _():
        copies(long_part=True)


def _dispatch_kernel(tinfo_ref, cnt_ref, tok_ref, pos_ref, xs_ref, first_ref, nblk_ref,
                     sbufs, zbuf, start_ref, sems, zsem):
    tj = pos_ref.shape[2]
    n_blocks = xs_ref.shape[0] // MOE_ROWS

    @pl.when(pl.program_id(0) == 0)
    def _():
        total = _pad_starts(cnt_ref, start_ref)
        used = total // MOE_ROWS

        def block_ranges(e, c):
            first_ref[e] = start_ref[e] // MOE_ROWS
            nblk_ref[e] = _padded(cnt_ref[0, e]) // MOE_ROWS
            return c
        lax.fori_loop(0, N_EXPERTS, block_ranges, 0)

        zbuf[...] = jnp.zeros_like(zbuf)
        zrows = zbuf.shape[0]
        for wait in (False, True):
            def zero_pad(e, c, wait=wait):
                cnt = cnt_ref[0, e]
                _segment_copies(_padded(cnt) - cnt, zbuf, 0, xs_ref, start_ref[e] + cnt, zsem,
                                zrows, wait, src_advances=False)
                return c
            lax.fori_loop(0, N_EXPERTS, zero_pad, 0)

            def zero_tail(i, c, wait=wait):
                cp = pltpu.make_async_copy(zbuf, xs_ref.at[pl.ds(total + i * zrows, zrows)], zsem)
                if wait:
                    cp.wait()
                else:
                    cp.start()
                return c
            lax.fori_loop(0, (n_blocks - used) * (MOE_ROWS // zrows), zero_tail, 0)

    def wait_tile(k):
        pltpu.make_async_copy(sbufs[k], xs_ref.at[pl.ds(0, 2 * tj)], sems.at[k]).wait()

    slot = lax.broadcasted_iota(jnp.int32, (2 * tj, tj), 0)
    for k in range(TILES_PER_STEP):
        @pl.when(pl.program_id(0) > 0)
        def _(k=k):
            wait_tile(k)
        perm = ((slot == pos_ref[k, 2:3, :]) | (slot == pos_ref[k, 3:4, :])).astype(bf16)
        sorted_rows = jnp.dot(perm, tok_ref[k * tj:(k + 1) * tj, :], preferred_element_type=f32)
        _store_row_tiles(sbufs[k], _pack_bf16_pairs(sorted_rows, is_bf16_exact=True))
        if k > 0:
            _tile_segments(tinfo_ref, k - 1, start_ref, sbufs[k - 1], True, xs_ref, sems.at[k - 1], tj)
    last = TILES_PER_STEP - 1
    _tile_segments(tinfo_ref, last, start_ref, sbufs[last], True, xs_ref, sems.at[last], tj)

    @pl.when(pl.program_id(0) == pl.num_programs(0) - 1)
    def _():
        for k in range(TILES_PER_STEP):
            wait_tile(k)


def _dispatch(tinfo, cnt, tok, pos, n_rows):
    ntiles, _, tj = pos.shape
    T, D = tok.shape
    n_blocks = n_rows // MOE_ROWS
    nt = TILES_PER_STEP
    assert ntiles % nt == 0
    sbuf = pltpu.VMEM(_row_tiles_shape(2 * tj, D), jnp.uint32)
    return pl.pallas_call(
        _dispatch_kernel,
        grid=(ntiles // nt,),
        in_specs=[pl.BlockSpec((nt, SUBLANES, LANES), lambda i: (i, 0, 0), memory_space=pltpu.SMEM),
                  pl.BlockSpec(memory_space=pltpu.SMEM),
                  pl.BlockSpec((nt * tj, D), lambda i: (i, 0)),
                  pl.BlockSpec((nt, SUBLANES, tj), lambda i: (i, 0, 0))],
        out_specs=[pl.BlockSpec(memory_space=pl.ANY),
                   pl.BlockSpec(memory_space=pltpu.SMEM),
                   pl.BlockSpec(memory_space=pltpu.SMEM)],
        out_shape=[jax.ShapeDtypeStruct(_row_tiles_shape(n_rows, D), jnp.uint32),
                   jax.ShapeDtypeStruct((N_EXPERTS,), jnp.int32),
                   jax.ShapeDtypeStruct((N_EXPERTS,), jnp.int32)],
        scratch_shapes=[[sbuf] * nt,
                        pltpu.VMEM(_row_tiles_shape(MOE_ROWS // 2, D), jnp.uint32),
                        pltpu.SMEM((N_EXPERTS,), jnp.int32), pltpu.SemaphoreType.DMA((nt,)),
                        pltpu.SemaphoreType.DMA],
        compiler_params=_cparams(("arbitrary",)),
        name="moe_dispatch",
    )(tinfo, cnt, tok, pos)


def _experts_kernel(first_ref, nblk_ref, xs_ref, wg_ref, wu_ref, wd_ref, ys_ref,
                    wg_s, wu_s, wd_s, xbuf, ybuf, in_sems, out_sems):
    e = pl.program_id(0)
    last_e = pl.num_programs(0) - 1
    rows = MOE_ROWS
    n_total = xs_ref.shape[0] // rows
    used = first_ref[last_e] + nblk_ref[last_e]

    def fetch(g, slot):
        return pltpu.make_async_copy(xs_ref.at[pl.ds(g * rows, rows)], xbuf.at[pl.ds(slot * rows, rows)],
                                     in_sems.at[slot])

    def flush(g, slot):
        return pltpu.make_async_copy(ybuf.at[pl.ds(slot * rows, rows)], ys_ref.at[pl.ds(g * rows, rows)],
                                     out_sems.at[slot])

    @pl.when((e == 0) & (used > 0))
    def _():
        fetch(0, 0).start(priority=ROW_DMA_PRIORITY)

    @pl.when(nblk_ref[e] > 0)
    def _():
        wg_s[...] = wg_ref[...].astype(bf16)
        wu_s[...] = wu_ref[...].astype(bf16)
        wd_s[...] = wd_ref[...].astype(bf16)

    def block(b, carry):
        g = first_ref[e] + b
        slot = g % 2
        fetch(g, slot).wait()

        @pl.when(g + 1 < used)
        def _():
            fetch(g + 1, 1 - slot).start(priority=ROW_DMA_PRIORITY)

        @pl.when(g >= 2)
        def _():
            flush(g - 2, slot).wait()
        half = rows // EXPERT_ROW_GROUPS
        starts = [slot * rows + r * half for r in range(EXPERT_ROW_GROUPS)]
        gated = []
        for row0 in starts:
            xb = _unpack_bf16_pairs(_load_row_tiles(xbuf, row0, half)).astype(bf16)
            gated.append((jnp.dot(xb, wg_s[...], preferred_element_type=f32),
                          jnp.dot(xb, wu_s[...], preferred_element_type=f32)))
        for row0, (a, u) in zip(starts, gated):
            h = (a * jax.nn.sigmoid(a) * u).astype(bf16)
            _store_row_tiles(ybuf, _pack_bf16_pairs(jnp.dot(h, wd_s[...], preferred_element_type=f32)), row0)
        flush(g, slot).start(priority=ROW_DMA_PRIORITY)
        return carry
    lax.fori_loop(0, nblk_ref[e], block, 0)

    @pl.when(e == last_e)
    def _():
        for back in (1, 2):
            @pl.when(used >= back)
            def _(back=back):
                flush(used - back, (used - back) % 2).wait()
        ybuf[pl.ds(0, rows)] = jnp.zeros((rows,) + ybuf.shape[1:], ybuf.dtype)
        for wait in (False, True):
            def zero_tail(g, c, wait=wait):
                cp = flush(g, 0)
                if wait:
                    cp.wait()
                else:
                    cp.start()
                return c
            lax.fori_loop(used, n_total, zero_tail, 0)


def _experts(first_block, n_blocks, xs, w_gate, w_up, w_down):
    _, D, De = w_gate.shape
    widx = lambda e, first, nblk: (e, 0, 0)
    buf = pltpu.VMEM(_row_tiles_shape(2 * MOE_ROWS, D), jnp.uint32)
    return pl.pallas_call(
        _experts_kernel,
        grid_spec=pltpu.PrefetchScalarGridSpec(
            num_scalar_prefetch=2,
            grid=(N_EXPERTS,),
            in_specs=[pl.BlockSpec(memory_space=pl.ANY),
                      pl.BlockSpec((None, D, De), widx),
                      pl.BlockSpec((None, D, De), widx),
                      pl.BlockSpec((None, De, D), widx)],
            out_specs=pl.BlockSpec(memory_space=pl.ANY),
            scratch_shapes=[pltpu.VMEM((D, De), bf16), pltpu.VMEM((D, De), bf16), pltpu.VMEM((De, D), bf16),
                            buf, buf, pltpu.SemaphoreType.DMA((2,)), pltpu.SemaphoreType.DMA((2,))],
        ),
        out_shape=jax.ShapeDtypeStruct(xs.shape, jnp.uint32),
        compiler_params=_cparams(("arbitrary",)),
        name="moe_experts",
    )(first_block, n_blocks, xs, w_gate, w_up, w_down)


def _combine_kernel(tinfo_ref, tnext_ref, cnt_ref, h2_ref, vec_ref, g_ref, ys_ref, o_ref,
                    gbufs, stages, start_ref, sems):
    tj = vec_ref.shape[0] // TILES_PER_STEP
    step = pl.program_id(0)

    def fetch(info_ref, t, buf):
        _tile_segments(info_ref, t, start_ref, gbufs[buf], False, ys_ref, sems.at[buf], tj)

    @pl.when(step == 0)
    def _():
        _pad_starts(cnt_ref, start_ref)
        for t in range(GATHER_AHEAD):
            fetch(tinfo_ref, t, t)

    slot = lax.broadcasted_iota(jnp.int32, (tj, 2 * tj), 1)
    for k in range(TILES_PER_STEP):
        pltpu.make_async_copy(ys_ref.at[pl.ds(0, 2 * tj)], gbufs[k], sems.at[k]).wait()
        ahead = k + GATHER_AHEAD
        if ahead < TILES_PER_STEP:
            fetch(tinfo_ref, ahead, ahead)
        else:
            fetch(tnext_ref, ahead - TILES_PER_STEP, ahead - TILES_PER_STEP)
        rows = _unpack_bf16_pairs(_load_row_tiles(gbufs[k])).astype(bf16)
        v = vec_ref[k * tj:(k + 1) * tj, :]
        pick = jnp.concatenate([(slot == v[:, 2 + c:3 + c].astype(jnp.int32)).astype(bf16) for c in range(2)],
                               axis=0)
        picked = jnp.dot(pick, rows, preferred_element_type=f32)
        y = h2_ref[k * tj:(k + 1) * tj, :] + v[:, 0:1] * picked[:tj] + v[:, 1:2] * picked[tj:]
        out = _rms(y, g_ref[...])
        stage = stages[k // 2]
        for q in range(stage.shape[0]):
            stage[q, pl.ds(k % 2, tj, stride=2), :] = out[:, q * LANES:(q + 1) * LANES]
        if k % 2 == 1:
            r0 = (k // 2) * 2 * tj
            for q in range(stage.shape[0]):
                o_ref[0, r0:r0 + 2 * tj, q * LANES:(q + 1) * LANES] = stage[q]


def _combine(tinfo, cnt, h2, vec, final_g, ys, B, L):
    T, D = h2.shape
    ntiles = tinfo.shape[0]
    tj = T // ntiles
    nt = TILES_PER_STEP
    assert GATHER_AHEAD < nt and nt % 2 == 0 and (L // tj) % nt == 0
    nsteps = ntiles // nt
    per_seq = L // (nt * tj)
    g = final_g.reshape(1, D)
    gbuf = pltpu.VMEM(_row_tiles_shape(2 * tj, D), jnp.uint32)
    return pl.pallas_call(
        _combine_kernel,
        grid=(nsteps,),
        in_specs=[pl.BlockSpec((nt, SUBLANES, LANES), lambda i: (i, 0, 0), memory_space=pltpu.SMEM),
                  pl.BlockSpec((nt, SUBLANES, LANES), lambda i: (i + 1, 0, 0), memory_space=pltpu.SMEM),
                  pl.BlockSpec(memory_space=pltpu.SMEM),
                  pl.BlockSpec((nt * tj, D), lambda i: (i, 0)),
                  pl.BlockSpec((nt * tj, LANES), lambda i: (i, 0)),
                  pl.BlockSpec((1, D), lambda i: (0, 0)),
                  pl.BlockSpec(memory_space=pl.ANY)],
        out_specs=pl.BlockSpec((1, nt * tj, D), lambda i: (i // per_seq, i % per_seq, 0)),
        out_shape=jax.ShapeDtypeStruct((B, L, D), f32),
        scratch_shapes=[[gbuf] * nt,
                        [pltpu.VMEM((D // LANES, 2 * tj, LANES), f32)] * (nt // 2),
                        pltpu.SMEM((N_EXPERTS,), jnp.int32), pltpu.SemaphoreType.DMA((nt,))],
        compiler_params=_cparams(("arbitrary",)),
        name="moe_combine",
    )(tinfo, jnp.pad(tinfo, ((0, nt), (0, 0), (0, 0))), cnt, h2, vec, g, ys)


def kernel(x, norm1_g, w_in, conv_w, conv_b, f_w_in, f_b_in, f_w_mid, f_b_mid, f_freq, f_w_out, f_bias,
           mix_g, w_out, norm2_g, w_group, b_group, w_router, b_router, w_gate, w_up, w_down, final_g):
    B, L, D = x.shape
    depth = norm1_g.shape[0]
    assert depth == 1, "the final RMSNorm is fused into the single layer's MoE combine"
    C = D // 2
    Lh = L // 2
    N = B * C
    T = B * L
    n_rows = T * 2 + N_EXPERTS * MOE_ROWS
    cfwd, cinv = _dft_consts(L)
    cfou = _fourier_consts(L)
    i = 0
    wcat = _prep_win(w_in[i], C)
    hcat, asum = _hyena_filters(L, C, f_w_in[i], f_b_in[i], f_w_mid[i], f_b_mid[i], f_freq[i], f_w_out[i])
    kspec, k0 = _filter_spectrum(cfwd, hcat, asum, L, C)
    z, x0, u4 = _inproj(x, norm1_g[i], wcat, conv_w[i], conv_b[i], C)
    z = z.reshape(2, Lh, N)
    x0 = x0.reshape(2, Lh, N)
    u4 = u4.reshape(4, Lh, N)
    s = _conv_fwd(cfwd, z, kspec, k0, C)
    yh = _conv_inv(cinv, s, x0, z, f_bias[i], C)
    yf = _fourier_dft(cfou, u4, L, C)
    wr = jnp.concatenate([jnp.transpose(w_router[i], (1, 0, 2)).reshape(D, N_EXPERTS), w_group[i]], axis=1)
    wr = jnp.pad(wr, ((0, 0), (0, LANES - wr.shape[1])))
    br = jnp.pad(jnp.concatenate([b_router[i].reshape(-1), b_group[i]]), (0, LANES - N_EXPERTS - N_GROUPS))
    h2, tok, pos, vec, tinfo, cnt = _outproj_router(yh, yf, x, mix_g[i], w_out[i].astype(bf16), norm2_g[i],
                                                     wr, br.reshape(1, LANES), C)
    xs, first_block, n_blocks = _dispatch(tinfo, cnt, tok, pos, n_rows)
    ys = _experts(first_block, n_blocks, xs, w_gate[i], w_up[i], w_down[i])
    return _combine(tinfo, cnt, h2, vec, final_g, ys, B, L)
```

```python
import functools
import math

import jax
import jax.numpy as jnp
from jax import lax
from jax.experimental import pallas as pl
from jax.experimental.pallas import tpu as pltpu

HEAD_DIM = 64
N_GROUPS = 4
EXPERTS_PER_GROUP = 8
N_EXPERTS = N_GROUPS * EXPERTS_PER_GROUP
FILTER_BANDS = 16
DECAY_FAST_PCT = 0.3
DECAY_SLOW_PCT = 1.5
DECAY_TARGET = 1e-2
EPS = 1e-6

LANES = 128
SUBLANES = 8
MXU_DIM = 256
CONV_ROWS = 512
CONV_COLS = 512
FOURIER_COLS = 256
MOE_ROWS = 512
EXPERT_ROW_GROUPS = 2
ROW_DMA_PRIORITY = 1
ROW_TILES_PER_STEP = 2
TILES_PER_STEP = 4
GATHER_AHEAD = 3
SEG_SPLIT = 32
VMEM_LIMIT = 56 * 1024 * 1024

f32 = jnp.float32
bf16 = jnp.bfloat16
HI = lax.Precision.HIGHEST


def _cparams(sem):
    return pltpu.CompilerParams(dimension_semantics=sem, vmem_limit_bytes=VMEM_LIMIT)


def _tile(pref, n):
    return min(pref, n)


_SIN_TERMS = tuple((-1.0) ** k / math.factorial(2 * k + 1) for k in range(7))
_COS_TERMS = tuple((-1.0) ** k / math.factorial(2 * k) for k in range(8))


def _cos_sin_of_turn_fraction(prod, m):
    num = prod & (m - 1)
    quadrant = num // (m // 4)
    x = (num & (m // 4 - 1)).astype(f32) * (2.0 * math.pi / m)
    x2 = x * x
    s = jnp.full_like(x, _SIN_TERMS[-1])
    for c in _SIN_TERMS[-2::-1]:
        s = s * x2 + c
    s = s * x
    c_ = jnp.full_like(x, _COS_TERMS[-1])
    for c in _COS_TERMS[-2::-1]:
        c_ = c_ * x2 + c
    odd = (quadrant & 1) == 1
    cos = jnp.where(odd, s, c_)
    sin = jnp.where(odd, c_, s)
    cos = jnp.where((quadrant == 1) | (quadrant == 2), -cos, cos)
    sin = jnp.where(quadrant >= 2, -sin, sin)
    return cos, sin


def _cos_sin_blocks(a, s, b0, m, ncb, d):
    tc, ts = _cos_sin_of_turn_fraction(a * (s * d), m)
    b0_lane = jnp.zeros(d.shape, jnp.int32)
    for cb in range(ncb):
        b0_lane = jnp.where(d == cb, b0(cb), b0_lane)
    ac, asn = _cos_sin_of_turn_fraction(a * b0_lane, m)
    for cb in range(ncb):
        ca = ac[:, cb:cb + 1]
        sa = asn[:, cb:cb + 1]
        yield cb, ca * tc - sa * ts, sa * tc + ca * ts


def _dft_consts_kernel(fwd_ref, inv_ref, *, L):
    Lh = L // 2
    tr = fwd_ref.shape[1]
    r = lax.broadcasted_iota(jnp.int32, (tr, LANES), 0) + pl.program_id(0) * tr
    d = lax.broadcasted_iota(jnp.int32, (tr, LANES), 1)
    alt_r = (1 - 2 * (r & 1)).astype(f32)
    alt_d = (1 - 2 * (d & 1)).astype(f32)
    ncb = Lh // LANES
    fams = [
        (r, 1, lambda cb: cb * LANES, L),
        (r, 2, lambda cb: 2 * cb * LANES + 1, 2 * L),
        (2 * r + 1, 1, lambda cb: cb * LANES, 2 * L),
    ]
    for fi, (a, s, b0, m) in enumerate(fams):
        for cb, c, sn in _cos_sin_blocks(a, s, b0, m, ncb, d):
            sl = slice(cb * LANES, (cb + 1) * LANES)
            if fi == 0:
                fwd_ref[0, :, sl] = c.astype(bf16)
                inv_ref[0, :, sl] = c.astype(bf16)
                fwd_ref[1, :, sl] = jnp.where(r == 0, alt_d, sn).astype(bf16)
                nst = -sn
                if cb == 0:
                    nst = jnp.where(d == 0, -alt_r, nst)
                inv_ref[1, :, sl] = nst.astype(bf16)
            elif fi == 1:
                fwd_ref[2, :, sl] = c.astype(bf16)
                fwd_ref[3, :, sl] = jnp.where(r == 0, alt_d, sn).astype(bf16)
            else:
                inv_ref[2, :, sl] = c.astype(bf16)
                nst = -sn
                if cb == 0:
                    nst = jnp.where(d == 0, -alt_r, nst)
                inv_ref[3, :, sl] = nst.astype(bf16)


def _dft_consts(L):
    Lh = L // 2
    tr = _tile(256, Lh)
    shp = jax.ShapeDtypeStruct((4, Lh, Lh), bf16)
    spec = pl.BlockSpec((4, tr, Lh), lambda i: (0, i, 0))
    return pl.pallas_call(
        functools.partial(_dft_consts_kernel, L=L),
        grid=(Lh // tr,),
        out_specs=[spec, spec],
        out_shape=[shp, shp],
        compiler_params=_cparams(("parallel",)),
        name="dft_consts",
    )()


def _fourier_consts_kernel(o_ref, *, L):
    n = L // 4
    tr = o_ref.shape[0]
    r = lax.broadcasted_iota(jnp.int32, (tr, LANES), 0) + pl.program_id(0) * tr
    d = lax.broadcasted_iota(jnp.int32, (tr, LANES), 1)

    def col_l(cb):
        parity, j0 = divmod(cb * LANES, n // 2)
        return 2 * j0 + parity

    for cb, c, sn in _cos_sin_blocks(r, 2, col_l, n, n // LANES, d):
        o_ref[:, cb * LANES:(cb + 1) * LANES] = c.astype(bf16)
        o_ref[:, n + cb * LANES:n + (cb + 1) * LANES] = (-sn).astype(bf16)


def _fourier_consts(L):
    n = L // 4
    tr = _tile(256, n)
    return pl.pallas_call(
        functools.partial(_fourier_consts_kernel, L=L),
        grid=(n // tr,),
        out_specs=pl.BlockSpec((tr, 2 * n), lambda i: (i, 0)),
        out_shape=jax.ShapeDtypeStruct((n, 2 * n), bf16),
        compiler_params=_cparams(("parallel",)),
        name="fourier_consts",
    )()


def _prep_win_kernel(w_ref, o_ref, *, C):
    w = w_ref[...]
    i = lax.broadcasted_iota(jnp.int32, (LANES, LANES), 0)
    j = lax.broadcasted_iota(jnp.int32, (LANES, LANES), 1)
    same = (i // HEAD_DIM) == (j // HEAD_DIM)
    cos, sin = _cos_sin_of_turn_fraction((i % HEAD_DIM) * (j % HEAD_DIM), HEAD_DIM)
    bc = jnp.where(same, cos, 0.0)
    bs = jnp.where(same, sin, 0.0)
    o_ref[:, :3 * C] = w[:, :3 * C].astype(bf16)
    for k in range(0, C, LANES):
        wf = w[:, 3 * C + k:3 * C + k + LANES]
        o_ref[:, 3 * C + k:3 * C + k + LANES] = jnp.dot(wf, bc, precision=HI, preferred_element_type=f32).astype(bf16)
        o_ref[:, 4 * C + k:4 * C + k + LANES] = jnp.dot(wf, bs, precision=HI, preferred_element_type=f32).astype(bf16)


def _prep_win(w_in, C):
    D = w_in.shape[0]
    tr = _tile(256, D)
    return pl.pallas_call(
        functools.partial(_prep_win_kernel, C=C),
        grid=(D // tr,),
        in_specs=[pl.BlockSpec((tr, 4 * C), lambda i: (i, 0))],
        out_specs=pl.BlockSpec((tr, 5 * C), lambda i: (i, 0)),
        out_shape=jax.ShapeDtypeStruct((D, 5 * C), bf16),
        compiler_params=_cparams(("parallel",)),
        name="prep_win",
    )(w_in)


def _filter_kernel(win_ref, bin_ref, wmid_ref, bmid_ref, freq_ref, wout_ref, h_ref, asum_ref, *, L, C):
    tr = h_ref.shape[1]
    step = pl.program_id(0)
    lane = lax.broadcasted_iota(jnp.int32, (tr, LANES), 1)
    row = lax.broadcasted_iota(jnp.int32, (tr, LANES), 0) + step * tr
    band_i = jnp.where(lane <= FILTER_BANDS, lane - 1, lane - 1 - FILTER_BANDS)
    band = 1e-4 + band_i.astype(f32) * ((FILTER_BANDS - 1 - 1e-4) / (FILTER_BANDS - 1))
    ch = lax.broadcasted_iota(jnp.int32, (tr, 2 * C), 1) % C
    max_decay = math.log(DECAY_TARGET) / DECAY_FAST_PCT
    min_decay = math.log(DECAY_TARGET) / DECAY_SLOW_PCT
    delta = jnp.abs(min_decay + ch.astype(f32) * ((max_decay - min_decay) / (C - 1)))
    is_bwd = lax.broadcasted_iota(jnp.int32, (tr, 2 * C), 1) >= C
    row_c = lax.broadcasted_iota(jnp.int32, (tr, 2 * C), 0) + step * tr

    @pl.when(step == 0)
    def _():
        asum_ref[...] = jnp.zeros_like(asum_ref)

    total = jnp.zeros((1, 2 * C), f32)
    for p in range(2):
        pos = (2 * row + p).astype(f32)
        t = pos * (1.0 / (L - 1))
        w = pos * (2.0 * math.pi / L)
        fw = band * w
        z = jnp.where(lane == 0, t,
                      jnp.where(lane <= FILTER_BANDS, jnp.cos(fw),
                                jnp.where(lane <= 2 * FILTER_BANDS, -jnp.sin(fw), 0.0)))
        h = jnp.sin(freq_ref[0:1, :] * (jnp.dot(z, win_ref[...], precision=HI, preferred_element_type=f32)
                                       + bin_ref[...]))
        for i in range(wmid_ref.shape[0]):
            h = jnp.sin(freq_ref[i + 1:i + 2, :]
                        * (jnp.dot(h, wmid_ref[i], precision=HI, preferred_element_type=f32)
                           + bmid_ref[i:i + 1, :]))
        out = jnp.dot(h, wout_ref[...], precision=HI, preferred_element_type=f32)
        tc = (2 * row_c + p).astype(f32) * (1.0 / (L - 1))
        out = out * jnp.exp(-tc * delta)
        if p == 0:
            out = jnp.where(is_bwd & (row_c == 0), 0.0, out)
        h_ref[p] = out.astype(bf16)
        total = total + jnp.sum(jnp.abs(out), axis=0, keepdims=True)
    asum_ref[0:1, :] += total


def _hyena_filters(L, C, f_w_in, f_b_in, f_w_mid, f_b_mid, f_freq, f_w_out):
    Lh = L // 2
    tr = _tile(256, Lh)
    order = f_w_in.shape[1]
    win = jnp.pad(f_w_in, ((0, LANES - f_w_in.shape[0]), (0, 0)))
    full = lambda a: pl.BlockSpec(a.shape, lambda i: (0,) * a.ndim)
    args = (win, f_b_in.reshape(1, order), f_w_mid, f_b_mid, f_freq, f_w_out)
    return pl.pallas_call(
        functools.partial(_filter_kernel, L=L, C=C),
        grid=(Lh // tr,),
        in_specs=[full(a) for a in args],
        out_specs=[pl.BlockSpec((2, tr, 2 * C), lambda i: (0, i, 0)),
                   pl.BlockSpec((SUBLANES, 2 * C), lambda i: (0, 0))],
        out_shape=[jax.ShapeDtypeStruct((2, Lh, 2 * C), bf16),
                   jax.ShapeDtypeStruct((SUBLANES, 2 * C), f32)],
        compiler_params=_cparams(("arbitrary",)),
        name="hyena_filters",
    )(*args)


def _fwd_products(c_ref, x_ref):
    tm, k = c_ref.shape[1:]
    even = jnp.dot(c_ref[0:2].reshape(2 * tm, k), x_ref[0], preferred_element_type=f32)
    odd = jnp.dot(c_ref[2:4].reshape(2 * tm, k), x_ref[1], preferred_element_type=f32)
    return even[:tm], even[tm:], odd[:tm], odd[tm:]


def _filter_spectrum_kernel(c_ref, h_ref, asum_ref, k_ref, k0_ref, *, L, C):
    ce, se, co, so = _fwd_products(c_ref, h_ref)
    a = asum_ref[0:1, :]
    scale = 1.0 / ((a[:, :C] + a[:, C:]) * L)
    F = lambda v: v[:, :C]
    G = lambda v: v[:, C:]
    k_ref[0] = (F(ce) + F(co) + G(ce) + G(co)) * scale
    k_ref[1] = (-(F(se) + F(so)) + (G(se) + G(so))) * scale
    k_ref[2] = (F(ce) - F(co) + G(ce) - G(co)) * scale
    k_ref[3] = ((F(se) - F(so)) - (G(se) - G(so))) * scale

    @pl.when(pl.program_id(0) == 0)
    def _():
        k0_ref[...] = jnp.zeros_like(k0_ref)
        k0_ref[0:1, :] = ((F(se) + G(se)) * scale)[0:1, :]
        k0_ref[1:2, :] = ((F(so) - G(so)) * scale)[0:1, :]


def _filter_spectrum(cfwd, hcat, asum, L, C):
    Lh = L // 2
    tm = _tile(256, Lh)
    return pl.pallas_call(
        functools.partial(_filter_spectrum_kernel, L=L, C=C),
        grid=(Lh // tm,),
        in_specs=[pl.BlockSpec((4, tm, Lh), lambda i: (0, i, 0)),
                  pl.BlockSpec((2, Lh, 2 * C), lambda i: (0, 0, 0)),
                  pl.BlockSpec((SUBLANES, 2 * C), lambda i: (0, 0))],
        out_specs=[pl.BlockSpec((4, tm, C), lambda i: (0, i, 0)),
                   pl.BlockSpec((SUBLANES, C), lambda i: (0, 0))],
        out_shape=[jax.ShapeDtypeStruct((4, Lh, C), f32),
                   jax.ShapeDtypeStruct((SUBLANES, C), f32)],
        compiler_params=_cparams(("arbitrary",)),
        name="filter_spectrum",
    )(cfwd, hcat, asum)


def _conv_fwd_kernel(c_ref, z_ref, k_ref, k0_ref, s_ref):
    products = _fwd_products(c_ref, z_ref)
    C = k_ref.shape[2]
    first = pl.program_id(1) == 0
    for h in range(z_ref.shape[2] // C):
        cols = slice(h * C, (h + 1) * C)
        ce, se, co, so = (v[:, cols] for v in products)
        zsr, zsi, zdr, zdi = ce + co, -(se + so), ce - co, se - so
        ksr, ksi, kdr, kdi = k_ref[0], k_ref[1], k_ref[2], k_ref[3]
        psr = zsr * ksr - zsi * ksi
        psi = zsr * ksi + zsi * ksr
        pdr = zdr * kdr - zdi * kdi
        pdi = zdr * kdi + zdi * kdr
        gen = (psr + pdr, psi - pdi, psr - pdr, psi + pdi)

        @pl.when(first)
        def _(cols=cols, ce=ce, se=se, so=so, zsr=zsr, zdr=zdr, ksr=ksr, kdr=kdr, gen=gen):
            row0 = lax.broadcasted_iota(jnp.int32, ce.shape, 0) == 0
            p0 = zsr * ksr
            pl_ = zdr * kdr
            ak, bk = k0_ref[0:1, :], k0_ref[1:2, :]
            phr = se * ak - so * bk
            phi = -(se * bk + so * ak)
            spec = (0.5 * (p0 + pl_), -phr, 0.5 * (p0 - pl_), phi)
            for i in range(4):
                s_ref[i, :, cols] = jnp.where(row0, spec[i], gen[i]).astype(bf16)

        @pl.when(jnp.logical_not(first))
        def _(cols=cols, gen=gen):
            for i in range(4):
                s_ref[i, :, cols] = gen[i].astype(bf16)


def _conv_fwd(cfwd, z, kspec, k0, C):
    _, Lh, N = z.shape
    tm = _tile(CONV_ROWS, Lh)
    tn = _tile(CONV_COLS, N)
    return pl.pallas_call(
        _conv_fwd_kernel,
        grid=(N // tn, Lh // tm),
        in_specs=[pl.BlockSpec((4, tm, Lh), lambda n, i: (0, i, 0)),
                  pl.BlockSpec((2, Lh, tn), lambda n, i: (0, 0, n)),
                  pl.BlockSpec((4, tm, C), lambda n, i: (0, i, 0)),
                  pl.BlockSpec((SUBLANES, C), lambda n, i: (0, 0))],
        out_specs=pl.BlockSpec((4, tm, tn), lambda n, i: (0, i, n)),
        out_shape=jax.ShapeDtypeStruct((4, Lh, N), bf16),
        compiler_params=_cparams(("parallel", "arbitrary")),
        name="conv_fwd",
    )(cfwd, z, kspec, k0)


def _pair_products(c_ref, r_ref):
    ev = (jnp.dot(c_ref[0], r_ref[0], preferred_element_type=f32)
          + jnp.dot(c_ref[1], r_ref[1], preferred_element_type=f32))
    od = (jnp.dot(c_ref[2], r_ref[2], preferred_element_type=f32)
          + jnp.dot(c_ref[3], r_ref[3], preferred_element_type=f32))
    return ev, od


def _conv_inv_kernel(c_ref, s_ref, x0_ref, z_ref, fb_ref, y_ref):
    conv = _pair_products(c_ref, s_ref)
    C = fb_ref.shape[1]
    fb = fb_ref[...]
    for p in range(2):
        for h in range(z_ref.shape[2] // C):
            cols = slice(h * C, (h + 1) * C)
            zp = z_ref[p, :, cols].astype(f32)
            y_ref[p, :, cols] = (x0_ref[p, :, cols].astype(f32) * (conv[p][:, cols] + zp * fb)).astype(bf16)


def _conv_inv(cinv, s, x0, z, f_bias, C):
    _, Lh, N = s.shape
    tm = _tile(CONV_ROWS, Lh)
    tn = _tile(CONV_COLS, N)
    return pl.pallas_call(
        _conv_inv_kernel,
        grid=(N // tn, Lh // tm),
        in_specs=[pl.BlockSpec((4, tm, Lh), lambda n, i: (0, i, 0)),
                  pl.BlockSpec((4, Lh, tn), lambda n, i: (0, 0, n)),
                  pl.BlockSpec((2, tm, tn), lambda n, i: (0, i, n)),
                  pl.BlockSpec((2, tm, tn), lambda n, i: (0, i, n)),
                  pl.BlockSpec((1, C), lambda n, i: (0, 0))],
        out_specs=pl.BlockSpec((2, tm, tn), lambda n, i: (0, i, n)),
        out_shape=jax.ShapeDtypeStruct((2, Lh, N), bf16),
        compiler_params=_cparams(("parallel", "arbitrary")),
        name="conv_inv",
    )(cinv, s, x0, z, f_bias.reshape(1, C))


def _fourier_kernel(c_ref, u_ref, y_ref, rhs, stage, tw1, tw2, *, L, scale):
    Lh, n, H = L // 2, L // 4, L // 8
    tn = u_ref.shape[2]

    @pl.when(pl.program_id(0) == 0)
    def _():
        r1 = lax.broadcasted_iota(jnp.int32, (Lh, LANES), 0)
        tw1[0], tw1[1] = _cos_sin_of_turn_fraction(2 * (r1 % n) + r1 // n, L)
        r2 = lax.broadcasted_iota(jnp.int32, (n, LANES), 0)
        tw2[0], tw2[1] = _cos_sin_of_turn_fraction(2 * (r2 % H) + r2 // H, Lh)

    def rotate(a, b, c, s):
        return c * a - s * b, s * a + c * b

    for q in range(tn // LANES):
        cols = slice(q * LANES, (q + 1) * LANES)
        for p in range(2):
            first = slice(p * n, p * n + H)
            second = slice(p * n + H, (p + 1) * n)
            out = slice(p * H, (p + 1) * H)
            out_b = slice(n + p * H, n + (p + 1) * H)
            c2, s2 = tw2[0, out, :], tw2[1, out, :]

            def emit(sub, a0, b0, a1, b1):
                rhs[sub, out, cols] = (a0 + a1).astype(bf16)
                rhs[sub, out_b, cols] = (b0 + b1).astype(bf16)
                da, db = rotate(a0 - a1, b0 - b1, c2, s2)
                rhs[sub + 1, out, cols] = da.astype(bf16)
                rhs[sub + 1, out_b, cols] = db.astype(bf16)

            ld = lambda k, rows: u_ref[k, rows, cols].astype(f32)
            emit(0, ld(0, first), ld(1, first), ld(0, second), ld(1, second))
            odd = [rotate(ld(2, rows), ld(3, rows), tw1[0, rows, :], tw1[1, rows, :]) for rows in (first, second)]
            emit(2, odd[0][0], odd[0][1], odd[1][0], odd[1][1])

    for parity in range(2):
        pair = [jnp.dot(c_ref[...], rhs[2 * parity + h], preferred_element_type=f32) * scale for h in range(2)]
        for q in range(tn // LANES):
            cols = slice(q * LANES, (q + 1) * LANES)
            for h in range(2):
                stage[q, pl.ds(h, n, stride=2), :] = pair[h][:, cols]
            y_ref[parity, :, cols] = stage[q].astype(bf16)


def _fourier_dft(cfou, u4, L, C):
    _, Lh, N = u4.shape
    n = L // 4
    tn = _tile(FOURIER_COLS, C)
    return pl.pallas_call(
        functools.partial(_fourier_kernel, L=L, scale=1.0 / math.sqrt(L * HEAD_DIM)),
        grid=(N // tn,),
        in_specs=[pl.BlockSpec((n, 2 * n), lambda j: (0, 0)),
                  pl.BlockSpec((4, Lh, tn), lambda j: (0, 0, j))],
        out_specs=pl.BlockSpec((2, Lh, tn), lambda j: (0, 0, j)),
        out_shape=jax.ShapeDtypeStruct((2, Lh, N), bf16),
        scratch_shapes=[pltpu.VMEM((4, 2 * n, tn), bf16), pltpu.VMEM((tn // LANES, Lh, LANES), f32),
                        pltpu.VMEM((2, Lh, LANES), f32), pltpu.VMEM((2, n, LANES), f32)],
        compiler_params=_cparams(("arbitrary",)),
        name="fourier_dft",
    )(cfou, u4)


def _rms(x, g):
    return x * lax.rsqrt(jnp.mean(x * x, axis=-1, keepdims=True) + EPS) * g


def _stage_rows(chunks, x):
    for k in range(chunks.shape[0]):
        chunks[k] = x[:, k * LANES:(k + 1) * LANES]


def _parity_rows(chunks, p, n):
    return jnp.concatenate([chunks[k, pl.ds(p, n, stride=2), :] for k in range(chunks.shape[0])], axis=1)


def _inproj_kernel(xa_ref, xb_ref, pa_ref, na_ref, pb_ref, nb_ref, g_ref, w_ref, cw_ref, cb_ref,
                   z_ref, x0_ref, u_ref, stage, *, C, D):
    tj = xa_ref.shape[1] // 2
    jt = pl.program_id(1)
    n_half = pl.num_programs(1)
    g = g_ref[...]
    w = w_ref[...]
    cw0, cw1, cw2 = cw_ref[0:1, :], cw_ref[1:2, :], cw_ref[2:3, :]
    cb = cb_ref[...]
    row = lax.broadcasted_iota(jnp.int32, (tj, 3 * C), 0)
    four = []
    halves = ((xa_ref, pa_ref, na_ref, jt == 0, False),
              (xb_ref, pb_ref, nb_ref, False, jt == n_half - 1))
    for h, (x_ref, p_ref, n_ref, at_start, at_end) in enumerate(halves):
        _stage_rows(stage, x_ref[0])
        lhs = jnp.concatenate([
            _rms(_parity_rows(stage, 0, tj), g).astype(bf16),
            _rms(_parity_rows(stage, 1, tj), g).astype(bf16),
            _rms(p_ref[0], g).astype(bf16),
            _rms(n_ref[0], g).astype(bf16)], axis=0)
        res = jnp.dot(lhs, w, preferred_element_type=f32)
        pe, po = res[:tj, :3 * C], res[tj:2 * tj, :3 * C]
        prev = res[2 * tj + SUBLANES - 1:2 * tj + SUBLANES, :3 * C]
        nxt = res[2 * tj + SUBLANES:2 * tj + SUBLANES + 1, :3 * C]
        if at_start is not False:
            prev = jnp.where(at_start, 0.0, prev)
        if at_end is not False:
            nxt = jnp.where(at_end, 0.0, nxt)
        po_dn = jnp.where(row == 0, prev, pltpu.roll(po, 1, 0))
        pe_up = jnp.where(row == tj - 1, nxt, pltpu.roll(pe, tj - 1, 0))
        uc = (cb + cw0 * po_dn + cw1 * pe + cw2 * po,
              cb + cw0 * pe + cw1 * po + cw2 * pe_up)
        for p in range(2):
            x0_ref[p, h] = uc[p][:, :C].astype(bf16)
            z_ref[p, h] = (uc[p][:, C:2 * C] * uc[p][:, 2 * C:]).astype(bf16)
        four.append((res[:tj, 3 * C:], res[tj:2 * tj, 3 * C:]))
    for p in range(2):
        fa, fb = four[0][p], four[1][p]
        u_ref[0, p] = (fa[:, :C] + fb[:, :C]).astype(bf16)
        u_ref[1, p] = (fa[:, C:] + fb[:, C:]).astype(bf16)
        u_ref[2, p] = (fa[:, :C] - fb[:, :C]).astype(bf16)
        u_ref[3, p] = (fa[:, C:] - fb[:, C:]).astype(bf16)


def _inproj(x, norm_g, wcat, conv_w, conv_b, C):
    B, L, D = x.shape
    Lq = L // 4
    tj = _tile(256, Lq)
    nq = Lq // tj
    N = B * C
    hb = 2 * tj // SUBLANES
    last = L // SUBLANES - 1
    main = lambda off: pl.BlockSpec((1, 2 * tj, D), lambda b, j: (b, j + off, 0))
    prev = lambda off: pl.BlockSpec((1, SUBLANES, D), lambda b, j: (b, jnp.maximum((j + off) * hb - 1, 0), 0))
    nxt = lambda off: pl.BlockSpec((1, SUBLANES, D), lambda b, j: (b, jnp.minimum((j + off + 1) * hb, last), 0))
    full = lambda a: pl.BlockSpec(a.shape, lambda b, j: (0,) * a.ndim)
    g = norm_g.reshape(1, D)
    cb = conv_b.reshape(1, 3 * C)
    return pl.pallas_call(
        functools.partial(_inproj_kernel, C=C, D=D),
        grid=(B, nq),
        in_specs=[main(0), main(nq), prev(0), nxt(0), prev(nq), nxt(nq),
                  full(g), full(wcat), full(conv_w), full(cb)],
        out_specs=[pl.BlockSpec((2, 2, tj, C), lambda b, j: (0, 0, j, b)),
                   pl.BlockSpec((2, 2, tj, C), lambda b, j: (0, 0, j, b)),
                   pl.BlockSpec((4, 2, tj, C), lambda b, j: (0, 0, j, b))],
        out_shape=[jax.ShapeDtypeStruct((2, 2, Lq, N), bf16),
                   jax.ShapeDtypeStruct((2, 2, Lq, N), bf16),
                   jax.ShapeDtypeStruct((4, 2, Lq, N), bf16)],
        scratch_shapes=[pltpu.VMEM((D // LANES, 2 * tj, LANES), f32)],
        compiler_params=_cparams(("parallel", "arbitrary")),
        name="inproj",
    )(x, x, x, x, x, x, g, wcat, conv_w, cb)


def _pack_bf16_pairs(v, is_bf16_exact=False):
    k = v.shape[1] // 2
    rounded = v if is_bf16_exact else v.astype(bf16).astype(f32)
    hi = pltpu.bitcast(rounded[:, :k], jnp.uint32)
    lo = pltpu.bitcast(rounded[:, k:], jnp.uint32)
    return hi | (lo >> 16)


def _unpack_bf16_pairs(u):
    hi = pltpu.bitcast(u & jnp.uint32(0xFFFF0000), f32)
    lo = pltpu.bitcast(u << 16, f32)
    return jnp.concatenate([hi, lo], axis=1)


def _row_tiles_shape(n, D):
    return (n, D // (2 * LANES), LANES)


def _store_row_tiles(ref, packed, row0=0):
    total, per_row, _ = ref.shape
    n = packed.shape[0]
    flat = ref.reshape(total * per_row, LANES)
    for s in range(per_row):
        flat[pl.ds(row0 * per_row + s, n, stride=per_row), :] = packed[:, s * LANES:(s + 1) * LANES]


def _load_row_tiles(ref, row0=0, n=None):
    total, per_row, _ = ref.shape
    n = total if n is None else n
    flat = ref.reshape(total * per_row, LANES)
    return jnp.concatenate([flat[pl.ds(row0 * per_row + s, n, stride=per_row), :] for s in range(per_row)],
                           axis=1)


def _outproj_router_kernel(yh_ref, yf_ref, x_ref, mg_ref, wo_ref, g2_ref, wr_ref, br_ref,
                           h2_ref, tok_ref, pos_ref, vec_ref, tinfo_ref, cnt_ref, carry_ref, stage, *, C):
    n_sub = pos_ref.shape[0]
    tj = pos_ref.shape[2]
    _stage_rows(stage, x_ref[0])

    @pl.when((pl.program_id(0) == 0) & (pl.program_id(1) == 0))
    def _():
        carry_ref[...] = jnp.zeros_like(carry_ref)

    li = lax.broadcasted_iota(jnp.int32, (LANES, LANES), 0)
    lj = lax.broadcasted_iota(jnp.int32, (LANES, LANES), 1)
    hw = _tile(MXU_DIM, C)
    ai = lax.broadcasted_iota(jnp.int32, (hw, hw), 0)
    aj = lax.broadcasted_iota(jnp.int32, (hw, hw), 1)
    avg = jnp.where(ai // HEAD_DIM == aj // HEAD_DIM, 1.0 / HEAD_DIM, 0.0).astype(bf16)
    upper = (li < lj).astype(bf16)
    ri = lax.broadcasted_iota(jnp.int32, (tj, tj), 0)
    ci = lax.broadcasted_iota(jnp.int32, (tj, tj), 1)
    tri = (ci < ri).astype(bf16)
    wr = wr_ref[...]
    wr_hi = wr.astype(bf16)
    wr_lo = (wr - wr_hi.astype(f32)).astype(bf16)
    wr_both = jnp.concatenate([wr_hi, wr_lo], axis=1)
    lane = lax.broadcasted_iota(jnp.int32, (n_sub * tj, LANES), 1).astype(f32)
    sub = lax.broadcasted_iota(jnp.int32, (SUBLANES, LANES), 0)
    neg = jnp.float32(-jnp.inf)
    big = jnp.float32(LANES)

    def first_argmax(vals, mask):
        v = jnp.where(mask, vals, neg)
        m = jnp.max(v, axis=1, keepdims=True)
        idx = jnp.min(jnp.where(mask & (v == m), lane, big), axis=1, keepdims=True)
        return m, idx

    normed = []
    for s in range(n_sub):
        r, p = divmod(s, 2)
        parts = []
        for ref in (yh_ref, yf_ref):
            y = ref[p, r * tj:(r + 1) * tj, :].astype(f32)
            ysq = (y * y).astype(bf16)
            ms = jnp.concatenate([jnp.dot(ysq[:, k:k + hw], avg, preferred_element_type=f32)
                                  for k in range(0, C, hw)], axis=1)
            parts.append(y * lax.rsqrt(ms + EPS))
        normed.append((jnp.concatenate(parts, axis=1) * mg_ref[...]).astype(bf16))
    proj = jnp.dot(jnp.concatenate(normed, axis=0), wo_ref[...], preferred_element_type=f32)
    t_hi, t_lo = [], []
    for s in range(n_sub):
        r, p = divmod(s, 2)
        rows = slice(s * tj, (s + 1) * tj)
        h2 = _parity_rows(stage, r * 2 * tj + p, tj) + proj[rows]
        h2_ref[rows, :] = h2
        tok = _rms(h2, g2_ref[...])
        hi = tok.astype(bf16)
        tok_ref[rows, :] = hi
        t_hi.append(hi)
        t_lo.append((tok - hi.astype(f32)).astype(bf16))
    hi_terms = jnp.dot(jnp.concatenate(t_hi, axis=0), wr_both, preferred_element_type=f32)
    logits = (hi_terms[:, :LANES] + hi_terms[:, LANES:]
              + jnp.dot(jnp.concatenate(t_lo, axis=0), wr_hi, preferred_element_type=f32) + br_ref[...])

    gmask = (lane >= N_EXPERTS) & (lane < N_EXPERTS + N_GROUPS)
    gmax, gidx = first_argmax(logits, gmask)
    gsel = gidx - N_EXPERTS
    p_group = 1.0 / jnp.sum(jnp.where(gmask, jnp.exp(logits - gmax), 0.0), axis=1, keepdims=True)
    emask = (lane >= gsel * EXPERTS_PER_GROUP) & (lane < (gsel + 1) * EXPERTS_PER_GROUP)
    l0, e0 = first_argmax(logits, emask)
    l1, e1 = first_argmax(logits, emask & (lane != e0))
    ex = jnp.exp(l1 - l0)
    w0 = p_group / (1.0 + ex)
    w1 = p_group * ex / (1.0 + ex)

    oh0 = (lane == e0).astype(f32)
    oh1 = (lane == e1).astype(f32)
    both = (oh0 + oh1).astype(bf16)
    where, counts = [], []
    for s in range(n_sub):
        tile_both = both[s * tj:(s + 1) * tj, :]
        ranks = jnp.dot(tri, tile_both, preferred_element_type=f32)
        n_e = ranks[tj - 1:tj, :] + tile_both[tj - 1:tj, :].astype(f32)
        n_e8 = jnp.broadcast_to(n_e, (SUBLANES, LANES))
        tile_off = jnp.dot(n_e8.astype(bf16), upper, preferred_element_type=f32)[0:1, :]
        where.append(ranks + tile_off)
        counts.append(n_e8)
    where = jnp.concatenate(where, axis=0)
    pos0 = jnp.sum(oh0 * where, axis=1, keepdims=True)
    pos1 = jnp.sum(oh1 * where, axis=1, keepdims=True)
    cols = jnp.where(lane == 0, w0, jnp.where(lane == 1, w1,
                     jnp.where(lane == 2, pos0, jnp.where(lane == 3, pos1, 0.0))))
    vec_ref[...] = cols
    base = carry_ref[...]
    for s in range(n_sub):
        pos_ref[s] = cols[s * tj:(s + 1) * tj, :].T[:SUBLANES, :].astype(jnp.int32)
        long_seg = (jnp.max(counts[s], axis=1, keepdims=True) >= SEG_SPLIT).astype(f32)
        tinfo_ref[s] = jnp.where(sub == 0, counts[s], jnp.where(sub == 1, base, jnp.where(sub == 2, long_seg, 0.0))
                                 ).astype(jnp.int32)
        base = base + counts[s]
    carry_ref[...] = base
    cnt_ref[...] = base.astype(jnp.int32)


def _outproj_router(yh, yf, x, mix_g, wo, norm2_g, wr, br, C):
    B, L, D = x.shape
    Lh = L // 2
    tj = _tile(256, Lh)
    rt = ROW_TILES_PER_STEP
    ns = 2 * rt
    nj = Lh // (rt * tj)
    T = B * L
    ntiles = T // tj
    lin = lambda b, j: b * nj + j
    full = lambda a: pl.BlockSpec(a.shape, lambda b, j: (0,) * a.ndim)
    mg = mix_g.reshape(1, D)
    g2 = norm2_g.reshape(1, D)
    return pl.pallas_call(
        functools.partial(_outproj_router_kernel, C=C),
        grid=(B, nj),
        in_specs=[pl.BlockSpec((2, rt * tj, C), lambda b, j: (0, j, b)),
                  pl.BlockSpec((2, rt * tj, C), lambda b, j: (0, j, b)),
                  pl.BlockSpec((1, ns * tj, D), lambda b, j: (b, j, 0)),
                  full(mg), full(wo), full(g2), full(wr), full(br)],
        out_specs=[pl.BlockSpec((ns * tj, D), lambda b, j: (lin(b, j), 0)),
                   pl.BlockSpec((ns * tj, D), lambda b, j: (lin(b, j), 0)),
                   pl.BlockSpec((ns, SUBLANES, tj), lambda b, j: (lin(b, j), 0, 0)),
                   pl.BlockSpec((ns * tj, LANES), lambda b, j: (lin(b, j), 0)),
                   pl.BlockSpec((ns, SUBLANES, LANES), lambda b, j: (lin(b, j), 0, 0)),
                   pl.BlockSpec((SUBLANES, LANES), lambda b, j: (0, 0))],
        out_shape=[jax.ShapeDtypeStruct((T, D), f32),
                   jax.ShapeDtypeStruct((T, D), bf16),
                   jax.ShapeDtypeStruct((ntiles, SUBLANES, tj), jnp.int32),
                   jax.ShapeDtypeStruct((T, LANES), f32),
                   jax.ShapeDtypeStruct((ntiles, SUBLANES, LANES), jnp.int32),
                   jax.ShapeDtypeStruct((SUBLANES, LANES), jnp.int32)],
        scratch_shapes=[pltpu.VMEM((SUBLANES, LANES), f32), pltpu.VMEM((D // LANES, ns * tj, LANES), f32)],
        compiler_params=_cparams(("arbitrary", "arbitrary")),
        name="outproj_router",
    )(yh, yf, x, mg, wo, g2, wr, br)


def _padded(c):
    return ((c + MOE_ROWS - 1) // MOE_ROWS) * MOE_ROWS


def _pad_starts(cnt_ref, start_ref):
    def body(e, acc):
        start_ref[e] = acc
        return acc + _padded(cnt_ref[0, e])
    return lax.fori_loop(0, N_EXPERTS, body, jnp.int32(0))


def _segment_copies(n, src, src_off, dst, dst_off, sem, max_rows, wait, src_advances=True, min_rows=1):
    bit = max_rows
    while bit >= min_rows:
        take = n & bit

        @pl.when(take != 0)
        def _(src_off=src_off, dst_off=dst_off, bit=bit):
            cp = pltpu.make_async_copy(src.at[pl.ds(src_off, bit)], dst.at[pl.ds(dst_off, bit)], sem)
            if wait:
                cp.wait()
            else:
                cp.start(priority=ROW_DMA_PRIORITY)
        if src_advances:
            src_off = src_off + take
        dst_off = dst_off + take
        bit //= 2


def _tile_segments(tinfo_ref, t, start_ref, local, local_is_src, remote, sem, tj):
    def copies(long_part):
        off = 0
        for e in range(N_EXPERTS):
            n = tinfo_ref[t, 0, e]
            far = start_ref[e] + tinfo_ref[t, 1, e]
            if long_part:
                args = (n, off, far, tj, SEG_SPLIT)
            else:
                head = n & ~(SEG_SPLIT - 1)
                args = (n, off + head, far + head, SEG_SPLIT // 2, 1)
            n_, near_, far_, hi, lo = args
            if local_is_src:
                _segment_copies(n_, local, near_, remote, far_, sem, hi, wait=False, min_rows=lo)
            else:
                _segment_copies(n_, remote, far_, local, near_, sem, hi, wait=False, min_rows=lo)
            off = off + n

    copies(long_part=False)

    @pl.when(tinfo_ref[t, 2, 0] != 0)
    def _():
        copies(long_part=True)


def _dispatch_kernel(tinfo_ref, cnt_ref, tok_ref, pos_ref, xs_ref, first_ref, nblk_ref,
                     sbufs, zbuf, start_ref, sems, zsem):
    tj = pos_ref.shape[2]
    n_blocks = xs_ref.shape[0] // MOE_ROWS

    @pl.when(pl.program_id(0) == 0)
    def _():
        total = _pad_starts(cnt_ref, start_ref)
        used = total // MOE_ROWS

        def block_ranges(e, c):
            first_ref[e] = start_ref[e] // MOE_ROWS
            nblk_ref[e] = _padded(cnt_ref[0, e]) // MOE_ROWS
            return c
        lax.fori_loop(0, N_EXPERTS, block_ranges, 0)

        zbuf[...] = jnp.zeros_like(zbuf)
        zrows = zbuf.shape[0]
        for wait in (False, True):
            def zero_pad(e, c, wait=wait):
                cnt = cnt_ref[0, e]
                _segment_copies(_padded(cnt) - cnt, zbuf, 0, xs_ref, start_ref[e] + cnt, zsem,
                                zrows, wait, src_advances=False)
                return c
            lax.fori_loop(0, N_EXPERTS, zero_pad, 0)

            def zero_tail(i, c, wait=wait):
                cp = pltpu.make_async_copy(zbuf, xs_ref.at[pl.ds(total + i * zrows, zrows)], zsem)
                if wait:
                    cp.wait()
                else:
                    cp.start()
                return c
            lax.fori_loop(0, (n_blocks - used) * (MOE_ROWS // zrows), zero_tail, 0)

    def wait_tile(k):
        pltpu.make_async_copy(sbufs[k], xs_ref.at[pl.ds(0, 2 * tj)], sems.at[k]).wait()

    slot = lax.broadcasted_iota(jnp.int32, (2 * tj, tj), 0)
    for k in range(TILES_PER_STEP):
        @pl.when(pl.program_id(0) > 0)
        def _(k=k):
            wait_tile(k)
        perm = ((slot == pos_ref[k, 2:3, :]) | (slot == pos_ref[k, 3:4, :])).astype(bf16)
        sorted_rows = jnp.dot(perm, tok_ref[k * tj:(k + 1) * tj, :], preferred_element_type=f32)
        _store_row_tiles(sbufs[k], _pack_bf16_pairs(sorted_rows, is_bf16_exact=True))
        if k > 0:
            _tile_segments(tinfo_ref, k - 1, start_ref, sbufs[k - 1], True, xs_ref, sems.at[k - 1], tj)
    last = TILES_PER_STEP - 1
    _tile_segments(tinfo_ref, last, start_ref, sbufs[last], True, xs_ref, sems.at[last], tj)

    @pl.when(pl.program_id(0) == pl.num_programs(0) - 1)
    def _():
        for k in range(TILES_PER_STEP):
            wait_tile(k)


def _dispatch(tinfo, cnt, tok, pos, n_rows):
    ntiles, _, tj = pos.shape
    T, D = tok.shape
    n_blocks = n_rows // MOE_ROWS
    nt = TILES_PER_STEP
    assert ntiles % nt == 0
    sbuf = pltpu.VMEM(_row_tiles_shape(2 * tj, D), jnp.uint32)
    return pl.pallas_call(
        _dispatch_kernel,
        grid=(ntiles // nt,),
        in_specs=[pl.BlockSpec((nt, SUBLANES, LANES), lambda i: (i, 0, 0), memory_space=pltpu.SMEM),
                  pl.BlockSpec(memory_space=pltpu.SMEM),
                  pl.BlockSpec((nt * tj, D), lambda i: (i, 0)),
                  pl.BlockSpec((nt, SUBLANES, tj), lambda i: (i, 0, 0))],
        out_specs=[pl.BlockSpec(memory_space=pl.ANY),
                   pl.BlockSpec(memory_space=pltpu.SMEM),
                   pl.BlockSpec(memory_space=pltpu.SMEM)],
        out_shape=[jax.ShapeDtypeStruct(_row_tiles_shape(n_rows, D), jnp.uint32),
                   jax.ShapeDtypeStruct((N_EXPERTS,), jnp.int32),
                   jax.ShapeDtypeStruct((N_EXPERTS,), jnp.int32)],
        scratch_shapes=[[sbuf] * nt,
                        pltpu.VMEM(_row_tiles_shape(MOE_ROWS // 2, D), jnp.uint32),
                        pltpu.SMEM((N_EXPERTS,), jnp.int32), pltpu.SemaphoreType.DMA((nt,)),
                        pltpu.SemaphoreType.DMA],
        compiler_params=_cparams(("arbitrary",)),
        name="moe_dispatch",
    )(tinfo, cnt, tok, pos)


def _experts_kernel(first_ref, nblk_ref, xs_ref, wg_ref, wu_ref, wd_ref, ys_ref,
                    wg_s, wu_s, wd_s, xbuf, ybuf, in_sems, out_sems):
    e = pl.program_id(0)
    last_e = pl.num_programs(0) - 1
    rows = MOE_ROWS
    n_total = xs_ref.shape[0] // rows
    used = first_ref[last_e] + nblk_ref[last_e]

    def fetch(g, slot):
        return pltpu.make_async_copy(xs_ref.at[pl.ds(g * rows, rows)], xbuf.at[pl.ds(slot * rows, rows)],
                                     in_sems.at[slot])

    def flush(g, slot):
        return pltpu.make_async_copy(ybuf.at[pl.ds(slot * rows, rows)], ys_ref.at[pl.ds(g * rows, rows)],
                                     out_sems.at[slot])

    @pl.when((e == 0) & (used > 0))
    def _():
        fetch(0, 0).start(priority=ROW_DMA_PRIORITY)

    @pl.when(nblk_ref[e] > 0)
    def _():
        wg_s[...] = wg_ref[...].astype(bf16)
        wu_s[...] = wu_ref[...].astype(bf16)
        wd_s[...] = wd_ref[...].astype(bf16)

    def block(b, carry):
        g = first_ref[e] + b
        slot = g % 2
        fetch(g, slot).wait()

        @pl.when(g + 1 < used)
        def _():
            fetch(g + 1, 1 - slot).start(priority=ROW_DMA_PRIORITY)

        @pl.when(g >= 2)
        def _():
            flush(g - 2, slot).wait()
        half = rows // EXPERT_ROW_GROUPS
        starts = [slot * rows + r * half for r in range(EXPERT_ROW_GROUPS)]
        gated = []
        for row0 in starts:
            xb = _unpack_bf16_pairs(_load_row_tiles(xbuf, row0, half)).astype(bf16)
            gated.append((jnp.dot(xb, wg_s[...], preferred_element_type=f32),
                          jnp.dot(xb, wu_s[...], preferred_element_type=f32)))
        for row0, (a, u) in zip(starts, gated):
            h = (a * jax.nn.sigmoid(a) * u).astype(bf16)
            _store_row_tiles(ybuf, _pack_bf16_pairs(jnp.dot(h, wd_s[...], preferred_element_type=f32)), row0)
        flush(g, slot).start(priority=ROW_DMA_PRIORITY)
        return carry
    lax.fori_loop(0, nblk_ref[e], block, 0)

    @pl.when(e == last_e)
    def _():
        for back in (1, 2):
            @pl.when(used >= back)
            def _(back=back):
                flush(used - back, (used - back) % 2).wait()
        ybuf[pl.ds(0, rows)] = jnp.zeros((rows,) + ybuf.shape[1:], ybuf.dtype)
        for wait in (False, True):
            def zero_tail(g, c, wait=wait):
                cp = flush(g, 0)
                if wait:
                    cp.wait()
                else:
                    cp.start()
                return c
            lax.fori_loop(used, n_total, zero_tail, 0)


def _experts(first_block, n_blocks, xs, w_gate, w_up, w_down):
    _, D, De = w_gate.shape
    widx = lambda e, first, nblk: (e, 0, 0)
    buf = pltpu.VMEM(_row_tiles_shape(2 * MOE_ROWS, D), jnp.uint32)
    return pl.pallas_call(
        _experts_kernel,
        grid_spec=pltpu.PrefetchScalarGridSpec(
            num_scalar_prefetch=2,
            grid=(N_EXPERTS,),
            in_specs=[pl.BlockSpec(memory_space=pl.ANY),
                      pl.BlockSpec((None, D, De), widx),
                      pl.BlockSpec((None, D, De), widx),
                      pl.BlockSpec((None, De, D), widx)],
            out_specs=pl.BlockSpec(memory_space=pl.ANY),
            scratch_shapes=[pltpu.VMEM((D, De), bf16), pltpu.VMEM((D, De), bf16), pltpu.VMEM((De, D), bf16),
                            buf, buf, pltpu.SemaphoreType.DMA((2,)), pltpu.SemaphoreType.DMA((2,))],
        ),
        out_shape=jax.ShapeDtypeStruct(xs.shape, jnp.uint32),
        compiler_params=_cparams(("arbitrary",)),
        name="moe_experts",
    )(first_block, n_blocks, xs, w_gate, w_up, w_down)


def _combine_kernel(tinfo_ref, tnext_ref, cnt_ref, h2_ref, vec_ref, g_ref, ys_ref, o_ref,
                    gbufs, stages, start_ref, sems):
    tj = vec_ref.shape[0] // TILES_PER_STEP
    step = pl.program_id(0)

    def fetch(info_ref, t, buf):
        _tile_segments(info_ref, t, start_ref, gbufs[buf], False, ys_ref, sems.at[buf], tj)

    @pl.when(step == 0)
    def _():
        _pad_starts(cnt_ref, start_ref)
        for t in range(GATHER_AHEAD):
            fetch(tinfo_ref, t, t)

    slot = lax.broadcasted_iota(jnp.int32, (tj, 2 * tj), 1)

    def finish(k, picked):
        v = vec_ref[k * tj:(k + 1) * tj, :]
        y = h2_ref[k * tj:(k + 1) * tj, :] + v[:, 0:1] * picked[:tj] + v[:, 1:2] * picked[tj:]
        out = _rms(y, g_ref[...])
        stage = stages[k // 2]
        for q in range(stage.shape[0]):
            stage[q, pl.ds(k % 2, tj, stride=2), :] = out[:, q * LANES:(q + 1) * LANES]
        if k % 2 == 1:
            r0 = (k // 2) * 2 * tj
            for q in range(stage.shape[0]):
                o_ref[0, r0:r0 + 2 * tj, q * LANES:(q + 1) * LANES] = stage[q]

    picked = None
    for k in range(TILES_PER_STEP):
        pltpu.make_async_copy(ys_ref.at[pl.ds(0, 2 * tj)], gbufs[k], sems.at[k]).wait()
        ahead = k + GATHER_AHEAD
        if ahead < TILES_PER_STEP:
            fetch(tinfo_ref, ahead, ahead)
        else:
            fetch(tnext_ref, ahead - TILES_PER_STEP, ahead - TILES_PER_STEP)
        rows = _unpack_bf16_pairs(_load_row_tiles(gbufs[k])).astype(bf16)
        v = vec_ref[k * tj:(k + 1) * tj, :]
        pick = jnp.concatenate([(slot == v[:, 2 + c:3 + c].astype(jnp.int32)).astype(bf16) for c in range(2)],
                               axis=0)
        now = jnp.dot(pick, rows, preferred_element_type=f32)
        if k > 0:
            finish(k - 1, picked)
        picked = now
    finish(TILES_PER_STEP - 1, picked)


def _combine(tinfo, cnt, h2, vec, final_g, ys, B, L):
    T, D = h2.shape
    ntiles = tinfo.shape[0]
    tj = T // ntiles
    nt = TILES_PER_STEP
    assert GATHER_AHEAD < nt and nt % 2 == 0 and (L // tj) % nt == 0
    nsteps = ntiles // nt
    per_seq = L // (nt * tj)
    g = final_g.reshape(1, D)
    gbuf = pltpu.VMEM(_row_tiles_shape(2 * tj, D), jnp.uint32)
    return pl.pallas_call(
        _combine_kernel,
        grid=(nsteps,),
        in_specs=[pl.BlockSpec((nt, SUBLANES, LANES), lambda i: (i, 0, 0), memory_space=pltpu.SMEM),
                  pl.BlockSpec((nt, SUBLANES, LANES), lambda i: (i + 1, 0, 0), memory_space=pltpu.SMEM),
                  pl.BlockSpec(memory_space=pltpu.SMEM),
                  pl.BlockSpec((nt * tj, D), lambda i: (i, 0)),
                  pl.BlockSpec((nt * tj, LANES), lambda i: (i, 0)),
                  pl.BlockSpec((1, D), lambda i: (0, 0)),
                  pl.BlockSpec(memory_space=pl.ANY)],
        out_specs=pl.BlockSpec((1, nt * tj, D), lambda i: (i // per_seq, i % per_seq, 0)),
        out_shape=jax.ShapeDtypeStruct((B, L, D), f32),
        scratch_shapes=[[gbuf] * nt,
                        [pltpu.VMEM((D // LANES, 2 * tj, LANES), f32)] * (nt // 2),
                        pltpu.SMEM((N_EXPERTS,), jnp.int32), pltpu.SemaphoreType.DMA((nt,))],
        compiler_params=_cparams(("arbitrary",)),
        name="moe_combine",
    )(tinfo, jnp.pad(tinfo, ((0, nt), (0, 0), (0, 0))), cnt, h2, vec, g, ys)


def kernel(x, norm1_g, w_in, conv_w, conv_b, f_w_in, f_b_in, f_w_mid, f_b_mid, f_freq, f_w_out, f_bias,
           mix_g, w_out, norm2_g, w_group, b_group, w_router, b_router, w_gate, w_up, w_down, final_g):
    B, L, D = x.shape
    depth = norm1_g.shape[0]
    assert depth == 1, "the final RMSNorm is fused into the single layer's MoE combine"
    C = D // 2
    Lh = L // 2
    N = B * C
    T = B * L
    n_rows = T * 2 + N_EXPERTS * MOE_ROWS
    cfwd, cinv = _dft_consts(L)
    cfou = _fourier_consts(L)
    i = 0
    wcat = _prep_win(w_in[i], C)
    hcat, asum = _hyena_filters(L, C, f_w_in[i], f_b_in[i], f_w_mid[i], f_b_mid[i], f_freq[i], f_w_out[i])
    kspec, k0 = _filter_spectrum(cfwd, hcat, asum, L, C)
    z, x0, u4 = _inproj(x, norm1_g[i], wcat, conv_w[i], conv_b[i], C)
    z = z.reshape(2, Lh, N)
    x0 = x0.reshape(2, Lh, N)
    u4 = u4.reshape(4, Lh, N)
    s = _conv_fwd(cfwd, z, kspec, k0, C)
    yh = _conv_inv(cinv, s, x0, z, f_bias[i], C)
    yf = _fourier_dft(cfou, u4, L, C)
    wr = jnp.concatenate([jnp.transpose(w_router[i], (1, 0, 2)).reshape(D, N_EXPERTS), w_group[i]], axis=1)
    wr = jnp.pad(wr, ((0, 0), (0, LANES - wr.shape[1])))
    br = jnp.pad(jnp.concatenate([b_router[i].reshape(-1), b_group[i]]), (0, LANES - N_EXPERTS - N_GROUPS))
    h2, tok, pos, vec, tinfo, cnt = _outproj_router(yh, yf, x, mix_g[i], w_out[i].astype(bf16), norm2_g[i],
                                                     wr, br.reshape(1, LANES), C)
    xs, first_block, n_blocks = _dispatch(tinfo, cnt, tok, pos, n_rows)
    ys = _experts(first_block, n_blocks, xs, w_gate[i], w_up[i], w_down[i])
    return _combine(tinfo, cnt, h2, vec, final_g, ys, B, L)
```

```python
import functools
import math

import jax
import jax.numpy as jnp
from jax import lax
from jax.experimental import pallas as pl
from jax.experimental.pallas import tpu as pltpu

HEAD_DIM = 64
N_GROUPS = 4
EXPERTS_PER_GROUP = 8
N_EXPERTS = N_GROUPS * EXPERTS_PER_GROUP
FILTER_BANDS = 16
DECAY_FAST_PCT = 0.3
DECAY_SLOW_PCT = 1.5
DECAY_TARGET = 1e-2
EPS = 1e-6

LANES = 128
SUBLANES = 8
MXU_DIM = 256
CONV_ROWS = 512
CONV_COLS = 512
FOURIER_COLS = 256
MOE_ROWS = 512
EXPERT_ROW_GROUPS = 2
ROW_DMA_PRIORITY = 1
ROW_TILES_PER_STEP = 2
TILES_PER_STEP = 4
GATHER_AHEAD = 3
SEG_SPLIT = 32
VMEM_LIMIT = 56 * 1024 * 1024

f32 = jnp.float32
bf16 = jnp.bfloat16
HI = lax.Precision.HIGHEST


def _cparams(sem):
    return pltpu.CompilerParams(dimension_semantics=sem, vmem_limit_bytes=VMEM_LIMIT)


def _tile(pref, n):
    return min(pref, n)


_SIN_TERMS = tuple((-1.0) ** k / math.factorial(2 * k + 1) for k in range(7))
_COS_TERMS = tuple((-1.0) ** k / math.factorial(2 * k) for k in range(8))


def _cos_sin_of_turn_fraction(prod, m):
    num = prod & (m - 1)
    quadrant = num // (m // 4)
    x = (num & (m // 4 - 1)).astype(f32) * (2.0 * math.pi / m)
    x2 = x * x
    s = jnp.full_like(x, _SIN_TERMS[-1])
    for c in _SIN_TERMS[-2::-1]:
        s = s * x2 + c
    s = s * x
    c_ = jnp.full_like(x, _COS_TERMS[-1])
    for c in _COS_TERMS[-2::-1]:
        c_ = c_ * x2 + c
    odd = (quadrant & 1) == 1
    cos = jnp.where(odd, s, c_)
    sin = jnp.where(odd, c_, s)
    cos = jnp.where((quadrant == 1) | (quadrant == 2), -cos, cos)
    sin = jnp.where(quadrant >= 2, -sin, sin)
    return cos, sin


def _cos_sin_blocks(a, s, b0, m, ncb, d):
    tc, ts = _cos_sin_of_turn_fraction(a * (s * d), m)
    b0_lane = jnp.zeros(d.shape, jnp.int32)
    for cb in range(ncb):
        b0_lane = jnp.where(d == cb, b0(cb), b0_lane)
    ac, asn = _cos_sin_of_turn_fraction(a * b0_lane, m)
    for cb in range(ncb):
        ca = ac[:, cb:cb + 1]
        sa = asn[:, cb:cb + 1]
        yield cb, ca * tc - sa * ts, sa * tc + ca * ts


def _write_dft_rows(fwd_ref, inv_ref, row_block, L):
    Lh = L // 2
    tr = fwd_ref.shape[1]
    r = lax.broadcasted_iota(jnp.int32, (tr, LANES), 0) + row_block * tr
    d = lax.broadcasted_iota(jnp.int32, (tr, LANES), 1)
    alt_r = (1 - 2 * (r & 1)).astype(f32)
    alt_d = (1 - 2 * (d & 1)).astype(f32)
    ncb = Lh // LANES
    fams = [
        (r, 1, lambda cb: cb * LANES, L),
        (r, 2, lambda cb: 2 * cb * LANES + 1, 2 * L),
        (2 * r + 1, 1, lambda cb: cb * LANES, 2 * L),
    ]
    for fi, (a, s, b0, m) in enumerate(fams):
        for cb, c, sn in _cos_sin_blocks(a, s, b0, m, ncb, d):
            sl = slice(cb * LANES, (cb + 1) * LANES)
            if fi == 0:
                fwd_ref[0, :, sl] = c.astype(bf16)
                inv_ref[0, :, sl] = c.astype(bf16)
                fwd_ref[1, :, sl] = jnp.where(r == 0, alt_d, sn).astype(bf16)
                nst = -sn
                if cb == 0:
                    nst = jnp.where(d == 0, -alt_r, nst)
                inv_ref[1, :, sl] = nst.astype(bf16)
            elif fi == 1:
                fwd_ref[2, :, sl] = c.astype(bf16)
                fwd_ref[3, :, sl] = jnp.where(r == 0, alt_d, sn).astype(bf16)
            else:
                inv_ref[2, :, sl] = c.astype(bf16)
                nst = -sn
                if cb == 0:
                    nst = jnp.where(d == 0, -alt_r, nst)
                inv_ref[3, :, sl] = nst.astype(bf16)


def _fourier_consts_kernel(o_ref, *, L):
    n = L // 4
    tr = o_ref.shape[0]
    r = lax.broadcasted_iota(jnp.int32, (tr, LANES), 0) + pl.program_id(0) * tr
    d = lax.broadcasted_iota(jnp.int32, (tr, LANES), 1)

    def col_l(cb):
        parity, j0 = divmod(cb * LANES, n // 2)
        return 2 * j0 + parity

    for cb, c, sn in _cos_sin_blocks(r, 2, col_l, n, n // LANES, d):
        o_ref[:, cb * LANES:(cb + 1) * LANES] = c.astype(bf16)
        o_ref[:, n + cb * LANES:n + (cb + 1) * LANES] = (-sn).astype(bf16)


def _fourier_consts(L):
    n = L // 4
    tr = _tile(256, n)
    return pl.pallas_call(
        functools.partial(_fourier_consts_kernel, L=L),
        grid=(n // tr,),
        out_specs=pl.BlockSpec((tr, 2 * n), lambda i: (i, 0)),
        out_shape=jax.ShapeDtypeStruct((n, 2 * n), bf16),
        compiler_params=_cparams(("parallel",)),
        name="fourier_consts",
    )()


def _prep_win_kernel(w_ref, o_ref, *, C):
    w = w_ref[...]
    i = lax.broadcasted_iota(jnp.int32, (LANES, LANES), 0)
    j = lax.broadcasted_iota(jnp.int32, (LANES, LANES), 1)
    same = (i // HEAD_DIM) == (j // HEAD_DIM)
    cos, sin = _cos_sin_of_turn_fraction((i % HEAD_DIM) * (j % HEAD_DIM), HEAD_DIM)
    bc = jnp.where(same, cos, 0.0)
    bs = jnp.where(same, sin, 0.0)
    o_ref[:, :3 * C] = w[:, :3 * C].astype(bf16)
    for k in range(0, C, LANES):
        wf = w[:, 3 * C + k:3 * C + k + LANES]
        o_ref[:, 3 * C + k:3 * C + k + LANES] = jnp.dot(wf, bc, precision=HI, preferred_element_type=f32).astype(bf16)
        o_ref[:, 4 * C + k:4 * C + k + LANES] = jnp.dot(wf, bs, precision=HI, preferred_element_type=f32).astype(bf16)


def _prep_win(w_in, C):
    D = w_in.shape[0]
    tr = _tile(256, D)
    return pl.pallas_call(
        functools.partial(_prep_win_kernel, C=C),
        grid=(D // tr,),
        in_specs=[pl.BlockSpec((tr, 4 * C), lambda i: (i, 0))],
        out_specs=pl.BlockSpec((tr, 5 * C), lambda i: (i, 0)),
        out_shape=jax.ShapeDtypeStruct((D, 5 * C), bf16),
        compiler_params=_cparams(("parallel",)),
        name="prep_win",
    )(w_in)


def _filter_kernel(win_ref, bin_ref, wmid_ref, bmid_ref, freq_ref, wout_ref, h_ref, asum_ref, *, L, C):
    tr = h_ref.shape[1]
    step = pl.program_id(0)
    lane = lax.broadcasted_iota(jnp.int32, (tr, LANES), 1)
    row = lax.broadcasted_iota(jnp.int32, (tr, LANES), 0) + step * tr
    band_i = jnp.where(lane <= FILTER_BANDS, lane - 1, lane - 1 - FILTER_BANDS)
    band = 1e-4 + band_i.astype(f32) * ((FILTER_BANDS - 1 - 1e-4) / (FILTER_BANDS - 1))
    ch = lax.broadcasted_iota(jnp.int32, (tr, 2 * C), 1) % C
    max_decay = math.log(DECAY_TARGET) / DECAY_FAST_PCT
    min_decay = math.log(DECAY_TARGET) / DECAY_SLOW_PCT
    delta = jnp.abs(min_decay + ch.astype(f32) * ((max_decay - min_decay) / (C - 1)))
    is_bwd = lax.broadcasted_iota(jnp.int32, (tr, 2 * C), 1) >= C
    row_c = lax.broadcasted_iota(jnp.int32, (tr, 2 * C), 0) + step * tr

    @pl.when(step == 0)
    def _():
        asum_ref[...] = jnp.zeros_like(asum_ref)

    total = jnp.zeros((1, 2 * C), f32)
    for p in range(2):
        pos = (2 * row + p).astype(f32)
        t = pos * (1.0 / (L - 1))
        w = pos * (2.0 * math.pi / L)
        fw = band * w
        z = jnp.where(lane == 0, t,
                      jnp.where(lane <= FILTER_BANDS, jnp.cos(fw),
                                jnp.where(lane <= 2 * FILTER_BANDS, -jnp.sin(fw), 0.0)))
        h = jnp.sin(freq_ref[0:1, :] * (jnp.dot(z, win_ref[...], precision=HI, preferred_element_type=f32)
                                       + bin_ref[...]))
        for i in range(wmid_ref.shape[0]):
            h = jnp.sin(freq_ref[i + 1:i + 2, :]
                        * (jnp.dot(h, wmid_ref[i], precision=HI, preferred_element_type=f32)
                           + bmid_ref[i:i + 1, :]))
        out = jnp.dot(h, wout_ref[...], precision=HI, preferred_element_type=f32)
        tc = (2 * row_c + p).astype(f32) * (1.0 / (L - 1))
        out = out * jnp.exp(-tc * delta)
        if p == 0:
            out = jnp.where(is_bwd & (row_c == 0), 0.0, out)
        h_ref[p] = out.astype(bf16)
        total = total + jnp.sum(jnp.abs(out), axis=0, keepdims=True)
    asum_ref[0:1, :] += total


def _hyena_filters(L, C, f_w_in, f_b_in, f_w_mid, f_b_mid, f_freq, f_w_out):
    Lh = L // 2
    tr = _tile(256, Lh)
    order = f_w_in.shape[1]
    win = jnp.pad(f_w_in, ((0, LANES - f_w_in.shape[0]), (0, 0)))
    full = lambda a: pl.BlockSpec(a.shape, lambda i: (0,) * a.ndim)
    args = (win, f_b_in.reshape(1, order), f_w_mid, f_b_mid, f_freq, f_w_out)
    return pl.pallas_call(
        functools.partial(_filter_kernel, L=L, C=C),
        grid=(Lh // tr,),
        in_specs=[full(a) for a in args],
        out_specs=[pl.BlockSpec((2, tr, 2 * C), lambda i: (0, i, 0)),
                   pl.BlockSpec((SUBLANES, 2 * C), lambda i: (0, 0))],
        out_shape=[jax.ShapeDtypeStruct((2, Lh, 2 * C), bf16),
                   jax.ShapeDtypeStruct((SUBLANES, 2 * C), f32)],
        compiler_params=_cparams(("arbitrary",)),
        name="hyena_filters",
    )(*args)


def _fwd_products(c_ref, x_ref):
    tm, k = c_ref.shape[1:]
    even = jnp.dot(c_ref[0:2].reshape(2 * tm, k), x_ref[0], preferred_element_type=f32)
    odd = jnp.dot(c_ref[2:4].reshape(2 * tm, k), x_ref[1], preferred_element_type=f32)
    return even[:tm], even[tm:], odd[:tm], odd[tm:]


def _filter_spectrum_kernel(c_ref, h_ref, asum_ref, k_ref, k0_ref, *, L, C):
    ce, se, co, so = _fwd_products(c_ref, h_ref)
    a = asum_ref[0:1, :]
    scale = 1.0 / ((a[:, :C] + a[:, C:]) * L)
    F = lambda v: v[:, :C]
    G = lambda v: v[:, C:]
    k_ref[0] = (F(ce) + F(co) + G(ce) + G(co)) * scale
    k_ref[1] = (-(F(se) + F(so)) + (G(se) + G(so))) * scale
    k_ref[2] = (F(ce) - F(co) + G(ce) - G(co)) * scale
    k_ref[3] = ((F(se) - F(so)) - (G(se) - G(so))) * scale

    @pl.when(pl.program_id(0) == 0)
    def _():
        k0_ref[...] = jnp.zeros_like(k0_ref)
        k0_ref[0:1, :] = ((F(se) + G(se)) * scale)[0:1, :]
        k0_ref[1:2, :] = ((F(so) - G(so)) * scale)[0:1, :]


def _filter_spectrum(cfwd, hcat, asum, L, C):
    Lh = L // 2
    tm = _tile(256, Lh)
    return pl.pallas_call(
        functools.partial(_filter_spectrum_kernel, L=L, C=C),
        grid=(Lh // tm,),
        in_specs=[pl.BlockSpec((4, tm, Lh), lambda i: (0, i, 0)),
                  pl.BlockSpec((2, Lh, 2 * C), lambda i: (0, 0, 0)),
                  pl.BlockSpec((SUBLANES, 2 * C), lambda i: (0, 0))],
        out_specs=[pl.BlockSpec((4, tm, C), lambda i: (0, i, 0)),
                   pl.BlockSpec((SUBLANES, C), lambda i: (0, 0))],
        out_shape=[jax.ShapeDtypeStruct((4, Lh, C), f32),
                   jax.ShapeDtypeStruct((SUBLANES, C), f32)],
        compiler_params=_cparams(("arbitrary",)),
        name="filter_spectrum",
    )(cfwd, hcat, asum)


def _conv_fwd_kernel(c_ref, z_ref, k_ref, k0_ref, s_ref):
    products = _fwd_products(c_ref, z_ref)
    C = k_ref.shape[2]
    first = pl.program_id(1) == 0
    for h in range(z_ref.shape[2] // C):
        cols = slice(h * C, (h + 1) * C)
        ce, se, co, so = (v[:, cols] for v in products)
        zsr, zsi, zdr, zdi = ce + co, -(se + so), ce - co, se - so
        ksr, ksi, kdr, kdi = k_ref[0], k_ref[1], k_ref[2], k_ref[3]
        psr = zsr * ksr - zsi * ksi
        psi = zsr * ksi + zsi * ksr
        pdr = zdr * kdr - zdi * kdi
        pdi = zdr * kdi + zdi * kdr
        gen = (psr + pdr, psi - pdi, psr - pdr, psi + pdi)

        @pl.when(first)
        def _(cols=cols, ce=ce, se=se, so=so, zsr=zsr, zdr=zdr, ksr=ksr, kdr=kdr, gen=gen):
            row0 = lax.broadcasted_iota(jnp.int32, ce.shape, 0) == 0
            p0 = zsr * ksr
            pl_ = zdr * kdr
            ak, bk = k0_ref[0:1, :], k0_ref[1:2, :]
            phr = se * ak - so * bk
            phi = -(se * bk + so * ak)
            spec = (0.5 * (p0 + pl_), -phr, 0.5 * (p0 - pl_), phi)
            for i in range(4):
                s_ref[i, :, cols] = jnp.where(row0, spec[i], gen[i]).astype(bf16)

        @pl.when(jnp.logical_not(first))
        def _(cols=cols, gen=gen):
            for i in range(4):
                s_ref[i, :, cols] = gen[i].astype(bf16)


def _conv_fwd(cfwd, z, kspec, k0, C):
    _, Lh, N = z.shape
    tm = _tile(CONV_ROWS, Lh)
    tn = _tile(CONV_COLS, N)
    return pl.pallas_call(
        _conv_fwd_kernel,
        grid=(N // tn, Lh // tm),
        in_specs=[pl.BlockSpec((4, tm, Lh), lambda n, i: (0, i, 0)),
                  pl.BlockSpec((2, Lh, tn), lambda n, i: (0, 0, n)),
                  pl.BlockSpec((4, tm, C), lambda n, i: (0, i, 0)),
                  pl.BlockSpec((SUBLANES, C), lambda n, i: (0, 0))],
        out_specs=pl.BlockSpec((4, tm, tn), lambda n, i: (0, i, n)),
        out_shape=jax.ShapeDtypeStruct((4, Lh, N), bf16),
        compiler_params=_cparams(("parallel", "arbitrary")),
        name="conv_fwd",
    )(cfwd, z, kspec, k0)


def _pair_products(c_ref, r_ref):
    ev = (jnp.dot(c_ref[0], r_ref[0], preferred_element_type=f32)
          + jnp.dot(c_ref[1], r_ref[1], preferred_element_type=f32))
    od = (jnp.dot(c_ref[2], r_ref[2], preferred_element_type=f32)
          + jnp.dot(c_ref[3], r_ref[3], preferred_element_type=f32))
    return ev, od


def _conv_inv_kernel(c_ref, s_ref, x0_ref, z_ref, fb_ref, y_ref):
    conv = _pair_products(c_ref, s_ref)
    C = fb_ref.shape[1]
    fb = fb_ref[...]
    for p in range(2):
        for h in range(z_ref.shape[2] // C):
            cols = slice(h * C, (h + 1) * C)
            zp = z_ref[p, :, cols].astype(f32)
            y_ref[p, :, cols] = (x0_ref[p, :, cols].astype(f32) * (conv[p][:, cols] + zp * fb)).astype(bf16)


def _conv_inv(cinv, s, x0, z, f_bias, C):
    _, Lh, N = s.shape
    tm = _tile(CONV_ROWS, Lh)
    tn = _tile(CONV_COLS, N)
    return pl.pallas_call(
        _conv_inv_kernel,
        grid=(N // tn, Lh // tm),
        in_specs=[pl.BlockSpec((4, tm, Lh), lambda n, i: (0, i, 0)),
                  pl.BlockSpec((4, Lh, tn), lambda n, i: (0, 0, n)),
                  pl.BlockSpec((2, tm, tn), lambda n, i: (0, i, n)),
                  pl.BlockSpec((2, tm, tn), lambda n, i: (0, i, n)),
                  pl.BlockSpec((1, C), lambda n, i: (0, 0))],
        out_specs=pl.BlockSpec((2, tm, tn), lambda n, i: (0, i, n)),
        out_shape=jax.ShapeDtypeStruct((2, Lh, N), bf16),
        compiler_params=_cparams(("parallel", "arbitrary")),
        name="conv_inv",
    )(cinv, s, x0, z, f_bias.reshape(1, C))


def _fourier_kernel(c_ref, u_ref, y_ref, rhs, stage, tw1, tw2, *, L, scale):
    Lh, n, H = L // 2, L // 4, L // 8
    tn = u_ref.shape[2]

    @pl.when(pl.program_id(0) == 0)
    def _():
        r1 = lax.broadcasted_iota(jnp.int32, (Lh, LANES), 0)
        tw1[0], tw1[1] = _cos_sin_of_turn_fraction(2 * (r1 % n) + r1 // n, L)
        r2 = lax.broadcasted_iota(jnp.int32, (n, LANES), 0)
        tw2[0], tw2[1] = _cos_sin_of_turn_fraction(2 * (r2 % H) + r2 // H, Lh)

    def rotate(a, b, c, s):
        return c * a - s * b, s * a + c * b

    for q in range(tn // LANES):
        cols = slice(q * LANES, (q + 1) * LANES)
        for p in range(2):
            first = slice(p * n, p * n + H)
            second = slice(p * n + H, (p + 1) * n)
            out = slice(p * H, (p + 1) * H)
            out_b = slice(n + p * H, n + (p + 1) * H)
            c2, s2 = tw2[0, out, :], tw2[1, out, :]

            def emit(sub, a0, b0, a1, b1):
                rhs[sub, out, cols] = (a0 + a1).astype(bf16)
                rhs[sub, out_b, cols] = (b0 + b1).astype(bf16)
                da, db = rotate(a0 - a1, b0 - b1, c2, s2)
                rhs[sub + 1, out, cols] = da.astype(bf16)
                rhs[sub + 1, out_b, cols] = db.astype(bf16)

            ld = lambda k, rows: u_ref[k, rows, cols].astype(f32)
            emit(0, ld(0, first), ld(1, first), ld(0, second), ld(1, second))
            odd = [rotate(ld(2, rows), ld(3, rows), tw1[0, rows, :], tw1[1, rows, :]) for rows in (first, second)]
            emit(2, odd[0][0], odd[0][1], odd[1][0], odd[1][1])

    for parity in range(2):
        pair = [jnp.dot(c_ref[...], rhs[2 * parity + h], preferred_element_type=f32) * scale for h in range(2)]
        for q in range(tn // LANES):
            cols = slice(q * LANES, (q + 1) * LANES)
            for h in range(2):
                stage[q, pl.ds(h, n, stride=2), :] = pair[h][:, cols]
            y_ref[parity, :, cols] = stage[q].astype(bf16)


def _fourier_dft(cfou, u4, L, C):
    _, Lh, N = u4.shape
    n = L // 4
    tn = _tile(FOURIER_COLS, C)
    return pl.pallas_call(
        functools.partial(_fourier_kernel, L=L, scale=1.0 / math.sqrt(L * HEAD_DIM)),
        grid=(N // tn,),
        in_specs=[pl.BlockSpec((n, 2 * n), lambda j: (0, 0)),
                  pl.BlockSpec((4, Lh, tn), lambda j: (0, 0, j))],
        out_specs=pl.BlockSpec((2, Lh, tn), lambda j: (0, 0, j)),
        out_shape=jax.ShapeDtypeStruct((2, Lh, N), bf16),
        scratch_shapes=[pltpu.VMEM((4, 2 * n, tn), bf16), pltpu.VMEM((tn // LANES, Lh, LANES), f32),
                        pltpu.VMEM((2, Lh, LANES), f32), pltpu.VMEM((2, n, LANES), f32)],
        compiler_params=_cparams(("arbitrary",)),
        name="fourier_dft",
    )(cfou, u4)


def _rms(x, g):
    return x * lax.rsqrt(jnp.mean(x * x, axis=-1, keepdims=True) + EPS) * g


def _stage_rows(chunks, x):
    for k in range(chunks.shape[0]):
        chunks[k] = x[:, k * LANES:(k + 1) * LANES]


def _parity_rows(chunks, p, n):
    return jnp.concatenate([chunks[k, pl.ds(p, n, stride=2), :] for k in range(chunks.shape[0])], axis=1)


def _inproj_kernel(xa_ref, xb_ref, pa_ref, na_ref, pb_ref, nb_ref, g_ref, w_ref, cw_ref, cb_ref,
                   z_ref, x0_ref, u_ref, cfwd_ref, cinv_ref, stage, *, C, D, L):
    tj = xa_ref.shape[1] // 2
    jt = pl.program_id(1)
    n_half = pl.num_programs(1)
    _write_dft_rows(cfwd_ref, cinv_ref, pl.program_id(0) * n_half + jt, L)
    g = g_ref[...]
    w = w_ref[...]
    cw0, cw1, cw2 = cw_ref[0:1, :], cw_ref[1:2, :], cw_ref[2:3, :]
    cb = cb_ref[...]
    row = lax.broadcasted_iota(jnp.int32, (tj, 3 * C), 0)
    four = []
    halves = ((xa_ref, pa_ref, na_ref, jt == 0, False),
              (xb_ref, pb_ref, nb_ref, False, jt == n_half - 1))
    for h, (x_ref, p_ref, n_ref, at_start, at_end) in enumerate(halves):
        _stage_rows(stage, x_ref[0])
        lhs = jnp.concatenate([
            _rms(_parity_rows(stage, 0, tj), g).astype(bf16),
            _rms(_parity_rows(stage, 1, tj), g).astype(bf16),
            _rms(p_ref[0], g).astype(bf16),
            _rms(n_ref[0], g).astype(bf16)], axis=0)
        res = jnp.dot(lhs, w, preferred_element_type=f32)
        pe, po = res[:tj, :3 * C], res[tj:2 * tj, :3 * C]
        prev = res[2 * tj + SUBLANES - 1:2 * tj + SUBLANES, :3 * C]
        nxt = res[2 * tj + SUBLANES:2 * tj + SUBLANES + 1, :3 * C]
        if at_start is not False:
            prev = jnp.where(at_start, 0.0, prev)
        if at_end is not False:
            nxt = jnp.where(at_end, 0.0, nxt)
        po_dn = jnp.where(row == 0, prev, pltpu.roll(po, 1, 0))
        pe_up = jnp.where(row == tj - 1, nxt, pltpu.roll(pe, tj - 1, 0))
        uc = (cb + cw0 * po_dn + cw1 * pe + cw2 * po,
              cb + cw0 * pe + cw1 * po + cw2 * pe_up)
        for p in range(2):
            x0_ref[p, h] = uc[p][:, :C].astype(bf16)
            z_ref[p, h] = (uc[p][:, C:2 * C] * uc[p][:, 2 * C:]).astype(bf16)
        four.append((res[:tj, 3 * C:], res[tj:2 * tj, 3 * C:]))
    for p in range(2):
        fa, fb = four[0][p], four[1][p]
        u_ref[0, p] = (fa[:, :C] + fb[:, :C]).astype(bf16)
        u_ref[1, p] = (fa[:, C:] + fb[:, C:]).astype(bf16)
        u_ref[2, p] = (fa[:, :C] - fb[:, :C]).astype(bf16)
        u_ref[3, p] = (fa[:, C:] - fb[:, C:]).astype(bf16)


def _inproj(x, norm_g, wcat, conv_w, conv_b, C):
    B, L, D = x.shape
    Lq = L // 4
    tj = _tile(256, Lq)
    nq = Lq // tj
    N = B * C
    hb = 2 * tj // SUBLANES
    last = L // SUBLANES - 1
    main = lambda off: pl.BlockSpec((1, 2 * tj, D), lambda b, j: (b, j + off, 0))
    prev = lambda off: pl.BlockSpec((1, SUBLANES, D), lambda b, j: (b, jnp.maximum((j + off) * hb - 1, 0), 0))
    nxt = lambda off: pl.BlockSpec((1, SUBLANES, D), lambda b, j: (b, jnp.minimum((j + off + 1) * hb, last), 0))
    full = lambda a: pl.BlockSpec(a.shape, lambda b, j: (0,) * a.ndim)
    g = norm_g.reshape(1, D)
    cb = conv_b.reshape(1, 3 * C)
    Lh = L // 2
    dft_rows = Lh // (B * nq)
    assert dft_rows * B * nq == Lh and dft_rows % (2 * SUBLANES) == 0
    dft_spec = pl.BlockSpec((4, dft_rows, Lh), lambda b, j: (0, b * nq + j, 0))
    dft_shape = jax.ShapeDtypeStruct((4, Lh, Lh), bf16)
    return pl.pallas_call(
        functools.partial(_inproj_kernel, C=C, D=D, L=L),
        grid=(B, nq),
        in_specs=[main(0), main(nq), prev(0), nxt(0), prev(nq), nxt(nq),
                  full(g), full(wcat), full(conv_w), full(cb)],
        out_specs=[pl.BlockSpec((2, 2, tj, C), lambda b, j: (0, 0, j, b)),
                   pl.BlockSpec((2, 2, tj, C), lambda b, j: (0, 0, j, b)),
                   pl.BlockSpec((4, 2, tj, C), lambda b, j: (0, 0, j, b)),
                   dft_spec, dft_spec],
        out_shape=[jax.ShapeDtypeStruct((2, 2, Lq, N), bf16),
                   jax.ShapeDtypeStruct((2, 2, Lq, N), bf16),
                   jax.ShapeDtypeStruct((4, 2, Lq, N), bf16),
                   dft_shape, dft_shape],
        scratch_shapes=[pltpu.VMEM((D // LANES, 2 * tj, LANES), f32)],
        compiler_params=_cparams(("parallel", "arbitrary")),
        name="inproj",
    )(x, x, x, x, x, x, g, wcat, conv_w, cb)


def _pack_bf16_pairs(v, is_bf16_exact=False):
    k = v.shape[1] // 2
    rounded = v if is_bf16_exact else v.astype(bf16).astype(f32)
    hi = pltpu.bitcast(rounded[:, :k], jnp.uint32)
    lo = pltpu.bitcast(rounded[:, k:], jnp.uint32)
    return hi | (lo >> 16)


def _unpack_bf16_pairs(u):
    hi = pltpu.bitcast(u & jnp.uint32(0xFFFF0000), f32)
    lo = pltpu.bitcast(u << 16, f32)
    return jnp.concatenate([hi, lo], axis=1)


def _row_tiles_shape(n, D):
    return (n, D // (2 * LANES), LANES)


def _store_row_tiles(ref, packed, row0=0):
    total, per_row, _ = ref.shape
    n = packed.shape[0]
    flat = ref.reshape(total * per_row, LANES)
    for s in range(per_row):
        flat[pl.ds(row0 * per_row + s, n, stride=per_row), :] = packed[:, s * LANES:(s + 1) * LANES]


def _load_row_tiles(ref, row0=0, n=None):
    total, per_row, _ = ref.shape
    n = total if n is None else n
    flat = ref.reshape(total * per_row, LANES)
    return jnp.concatenate([flat[pl.ds(row0 * per_row + s, n, stride=per_row), :] for s in range(per_row)],
                           axis=1)


def _outproj_router_kernel(yh_ref, yf_ref, x_ref, mg_ref, wo_ref, g2_ref, wr_ref, br_ref,
                           h2_ref, tok_ref, pos_ref, vec_ref, tinfo_ref, cnt_ref, carry_ref, stage, *, C):
    n_sub = pos_ref.shape[0]
    tj = pos_ref.shape[2]
    _stage_rows(stage, x_ref[0])

    @pl.when((pl.program_id(0) == 0) & (pl.program_id(1) == 0))
    def _():
        carry_ref[...] = jnp.zeros_like(carry_ref)

    li = lax.broadcasted_iota(jnp.int32, (LANES, LANES), 0)
    lj = lax.broadcasted_iota(jnp.int32, (LANES, LANES), 1)
    hw = _tile(MXU_DIM, C)
    ai = lax.broadcasted_iota(jnp.int32, (hw, hw), 0)
    aj = lax.broadcasted_iota(jnp.int32, (hw, hw), 1)
    avg = jnp.where(ai // HEAD_DIM == aj // HEAD_DIM, 1.0 / HEAD_DIM, 0.0).astype(bf16)
    upper = (li < lj).astype(bf16)
    ri = lax.broadcasted_iota(jnp.int32, (tj, tj), 0)
    ci = lax.broadcasted_iota(jnp.int32, (tj, tj), 1)
    tri = (ci < ri).astype(bf16)
    wr = wr_ref[...]
    wr_hi = wr.astype(bf16)
    wr_lo = (wr - wr_hi.astype(f32)).astype(bf16)
    wr_both = jnp.concatenate([wr_hi, wr_lo], axis=1)
    lane = lax.broadcasted_iota(jnp.int32, (n_sub * tj, LANES), 1).astype(f32)
    sub = lax.broadcasted_iota(jnp.int32, (SUBLANES, LANES), 0)
    neg = jnp.float32(-jnp.inf)
    big = jnp.float32(LANES)

    def first_argmax(vals, mask):
        v = jnp.where(mask, vals, neg)
        m = jnp.max(v, axis=1, keepdims=True)
        idx = jnp.min(jnp.where(mask & (v == m), lane, big), axis=1, keepdims=True)
        return m, idx

    normed = []
    for s in range(n_sub):
        r, p = divmod(s, 2)
        parts = []
        for ref in (yh_ref, yf_ref):
            y = ref[p, r * tj:(r + 1) * tj, :].astype(f32)
            ysq = (y * y).astype(bf16)
            ms = jnp.concatenate([jnp.dot(ysq[:, k:k + hw], avg, preferred_element_type=f32)
                                  for k in range(0, C, hw)], axis=1)
            parts.append(y * lax.rsqrt(ms + EPS))
        normed.append((jnp.concatenate(parts, axis=1) * mg_ref[...]).astype(bf16))
    proj = jnp.dot(jnp.concatenate(normed, axis=0), wo_ref[...], preferred_element_type=f32)
    t_hi, t_lo = [], []
    for s in range(n_sub):
        r, p = divmod(s, 2)
        rows = slice(s * tj, (s + 1) * tj)
        h2 = _parity_rows(stage, r * 2 * tj + p, tj) + proj[rows]
        h2_ref[rows, :] = h2
        tok = _rms(h2, g2_ref[...])
        hi = tok.astype(bf16)
        tok_ref[rows, :] = hi
        t_hi.append(hi)
        t_lo.append((tok - hi.astype(f32)).astype(bf16))
    hi_terms = jnp.dot(jnp.concatenate(t_hi, axis=0), wr_both, preferred_element_type=f32)
    logits = (hi_terms[:, :LANES] + hi_terms[:, LANES:]
              + jnp.dot(jnp.concatenate(t_lo, axis=0), wr_hi, preferred_element_type=f32) + br_ref[...])

    gmask = (lane >= N_EXPERTS) & (lane < N_EXPERTS + N_GROUPS)
    gmax, gidx = first_argmax(logits, gmask)
    gsel = gidx - N_EXPERTS
    p_group = 1.0 / jnp.sum(jnp.where(gmask, jnp.exp(logits - gmax), 0.0), axis=1, keepdims=True)
    emask = (lane >= gsel * EXPERTS_PER_GROUP) & (lane < (gsel + 1) * EXPERTS_PER_GROUP)
    l0, e0 = first_argmax(logits, emask)
    l1, e1 = first_argmax(logits, emask & (lane != e0))
    ex = jnp.exp(l1 - l0)
    w0 = p_group / (1.0 + ex)
    w1 = p_group * ex / (1.0 + ex)

    oh0 = (lane == e0).astype(f32)
    oh1 = (lane == e1).astype(f32)
    both = (oh0 + oh1).astype(bf16)
    where, counts = [], []
    for s in range(n_sub):
        tile_both = both[s * tj:(s + 1) * tj, :]
        ranks = jnp.dot(tri, tile_both, preferred_element_type=f32)
        n_e = ranks[tj - 1:tj, :] + tile_both[tj - 1:tj, :].astype(f32)
        n_e8 = jnp.broadcast_to(n_e, (SUBLANES, LANES))
        tile_off = jnp.dot(n_e8.astype(bf16), upper, preferred_element_type=f32)[0:1, :]
        where.append(ranks + tile_off)
        counts.append(n_e8)
    where = jnp.concatenate(where, axis=0)
    pos0 = jnp.sum(oh0 * where, axis=1, keepdims=True)
    pos1 = jnp.sum(oh1 * where, axis=1, keepdims=True)
    cols = jnp.where(lane == 0, w0, jnp.where(lane == 1, w1,
                     jnp.where(lane == 2, pos0, jnp.where(lane == 3, pos1, 0.0))))
    vec_ref[...] = cols
    base = carry_ref[...]
    for s in range(n_sub):
        pos_ref[s] = cols[s * tj:(s + 1) * tj, :].T[:SUBLANES, :].astype(jnp.int32)
        long_seg = (jnp.max(counts[s], axis=1, keepdims=True) >= SEG_SPLIT).astype(f32)
        tinfo_ref[s] = jnp.where(sub == 0, counts[s], jnp.where(sub == 1, base, jnp.where(sub == 2, long_seg, 0.0))
                                 ).astype(jnp.int32)
        base = base + counts[s]
    carry_ref[...] = base
    cnt_ref[...] = base.astype(jnp.int32)


def _outproj_router(yh, yf, x, mix_g, wo, norm2_g, wr, br, C):
    B, L, D = x.shape
    Lh = L // 2
    tj = _tile(256, Lh)
    rt = ROW_TILES_PER_STEP
    ns = 2 * rt
    nj = Lh // (rt * tj)
    T = B * L
    ntiles = T // tj
    lin = lambda b, j: b * nj + j
    full = lambda a: pl.BlockSpec(a.shape, lambda b, j: (0,) * a.ndim)
    mg = mix_g.reshape(1, D)
    g2 = norm2_g.reshape(1, D)
    return pl.pallas_call(
        functools.partial(_outproj_router_kernel, C=C),
        grid=(B, nj),
        in_specs=[pl.BlockSpec((2, rt * tj, C), lambda b, j: (0, j, b)),
                  pl.BlockSpec((2, rt * tj, C), lambda b, j: (0, j, b)),
                  pl.BlockSpec((1, ns * tj, D), lambda b, j: (b, j, 0)),
                  full(mg), full(wo), full(g2), full(wr), full(br)],
        out_specs=[pl.BlockSpec((ns * tj, D), lambda b, j: (lin(b, j), 0)),
                   pl.BlockSpec((ns * tj, D), lambda b, j: (lin(b, j), 0)),
                   pl.BlockSpec((ns, SUBLANES, tj), lambda b, j: (lin(b, j), 0, 0)),
                   pl.BlockSpec((ns * tj, LANES), lambda b, j: (lin(b, j), 0)),
                   pl.BlockSpec((ns, SUBLANES, LANES), lambda b, j: (lin(b, j), 0, 0)),
                   pl.BlockSpec((SUBLANES, LANES), lambda b, j: (0, 0))],
        out_shape=[jax.ShapeDtypeStruct((T, D), f32),
                   jax.ShapeDtypeStruct((T, D), bf16),
                   jax.ShapeDtypeStruct((ntiles, SUBLANES, tj), jnp.int32),
                   jax.ShapeDtypeStruct((T, LANES), f32),
                   jax.ShapeDtypeStruct((ntiles, SUBLANES, LANES), jnp.int32),
                   jax.ShapeDtypeStruct((SUBLANES, LANES), jnp.int32)],
        scratch_shapes=[pltpu.VMEM((SUBLANES, LANES), f32), pltpu.VMEM((D // LANES, ns * tj, LANES), f32)],
        compiler_params=_cparams(("arbitrary", "arbitrary")),
        name="outproj_router",
    )(yh, yf, x, mg, wo, g2, wr, br)


def _padded(c):
    return ((c + MOE_ROWS - 1) // MOE_ROWS) * MOE_ROWS


def _pad_starts(cnt_ref, start_ref):
    def body(e, acc):
        start_ref[e] = acc
        return acc + _padded(cnt_ref[0, e])
    return lax.fori_loop(0, N_EXPERTS, body, jnp.int32(0))


def _segment_copies(n, src, src_off, dst, dst_off, sem, max_rows, wait, src_advances=True, min_rows=1):
    bit = max_rows
    while bit >= min_rows:
        take = n & bit

        @pl.when(take != 0)
        def _(src_off=src_off, dst_off=dst_off, bit=bit):
            cp = pltpu.make_async_copy(src.at[pl.ds(src_off, bit)], dst.at[pl.ds(dst_off, bit)], sem)
            if wait:
                cp.wait()
            else:
                cp.start(priority=ROW_DMA_PRIORITY)
        if src_advances:
            src_off = src_off + take
        dst_off = dst_off + take
        bit //= 2


def _tile_segments(tinfo_ref, t, start_ref, local, local_is_src, remote, sem, tj):
    def copies(long_part):
        off = 0
        for e in range(N_EXPERTS):
            n = tinfo_ref[t, 0, e]
            far = start_ref[e] + tinfo_ref[t, 1, e]
            if long_part:
                args = (n, off, far, tj, SEG_SPLIT)
            else:
                head = n & ~(SEG_SPLIT - 1)
                args = (n, off + head, far + head, SEG_SPLIT // 2, 1)
            n_, near_, far_, hi, lo = args
            if local_is_src:
                _segment_copies(n_, local, near_, remote, far_, sem, hi, wait=False, min_rows=lo)
            else:
                _segment_copies(n_, remote, far_, local, near_, sem, hi, wait=False, min_rows=lo)
            off = off + n

    copies(long_part=False)

    @pl.when(tinfo_ref[t, 2, 0] != 0)
    def _():
        copies(long_part=True)


def _dispatch_kernel(tinfo_ref, cnt_ref, tok_ref, pos_ref, xs_ref, first_ref, nblk_ref,
                     sbufs, zbuf, start_ref, sems, zsem):
    tj = pos_ref.shape[2]
    n_blocks = xs_ref.shape[0] // MOE_ROWS

    @pl.when(pl.program_id(0) == 0)
    def _():
        total = _pad_starts(cnt_ref, start_ref)
        used = total // MOE_ROWS

        def block_ranges(e, c):
            first_ref[e] = start_ref[e] // MOE_ROWS
            nblk_ref[e] = _padded(cnt_ref[0, e]) // MOE_ROWS
            return c
        lax.fori_loop(0, N_EXPERTS, block_ranges, 0)

        zbuf[...] = jnp.zeros_like(zbuf)
        zrows = zbuf.shape[0]
        for wait in (False, True):
            def zero_pad(e, c, wait=wait):
                cnt = cnt_ref[0, e]
                _segment_copies(_padded(cnt) - cnt, zbuf, 0, xs_ref, start_ref[e] + cnt, zsem,
                                zrows, wait, src_advances=False)
                return c
            lax.fori_loop(0, N_EXPERTS, zero_pad, 0)

            def zero_tail(i, c, wait=wait):
                cp = pltpu.make_async_copy(zbuf, xs_ref.at[pl.ds(total + i * zrows, zrows)], zsem)
                if wait:
                    cp.wait()
                else:
                    cp.start()
                return c
            lax.fori_loop(0, (n_blocks - used) * (MOE_ROWS // zrows), zero_tail, 0)

    def wait_tile(k):
        pltpu.make_async_copy(sbufs[k], xs_ref.at[pl.ds(0, 2 * tj)], sems.at[k]).wait()

    slot = lax.broadcasted_iota(jnp.int32, (2 * tj, tj), 0)
    for k in range(TILES_PER_STEP):
        @pl.when(pl.program_id(0) > 0)
        def _(k=k):
            wait_tile(k)
        perm = ((slot == pos_ref[k, 2:3, :]) | (slot == pos_ref[k, 3:4, :])).astype(bf16)
        sorted_rows = jnp.dot(perm, tok_ref[k * tj:(k + 1) * tj, :], preferred_element_type=f32)
        _store_row_tiles(sbufs[k], _pack_bf16_pairs(sorted_rows, is_bf16_exact=True))
        if k > 0:
            _tile_segments(tinfo_ref, k - 1, start_ref, sbufs[k - 1], True, xs_ref, sems.at[k - 1], tj)
    last = TILES_PER_STEP - 1
    _tile_segments(tinfo_ref, last, start_ref, sbufs[last], True, xs_ref, sems.at[last], tj)

    @pl.when(pl.program_id(0) == pl.num_programs(0) - 1)
    def _():
        for k in range(TILES_PER_STEP):
            wait_tile(k)


def _dispatch(tinfo, cnt, tok, pos, n_rows):
    ntiles, _, tj = pos.shape
    T, D = tok.shape
    n_blocks = n_rows // MOE_ROWS
    nt = TILES_PER_STEP
    assert ntiles % nt == 0
    sbuf = pltpu.VMEM(_row_tiles_shape(2 * tj, D), jnp.uint32)
    return pl.pallas_call(
        _dispatch_kernel,
        grid=(ntiles // nt,),
        in_specs=[pl.BlockSpec((nt, SUBLANES, LANES), lambda i: (i, 0, 0), memory_space=pltpu.SMEM),
                  pl.BlockSpec(memory_space=pltpu.SMEM),
                  pl.BlockSpec((nt * tj, D), lambda i: (i, 0)),
                  pl.BlockSpec((nt, SUBLANES, tj), lambda i: (i, 0, 0))],
        out_specs=[pl.BlockSpec(memory_space=pl.ANY),
                   pl.BlockSpec(memory_space=pltpu.SMEM),
                   pl.BlockSpec(memory_space=pltpu.SMEM)],
        out_shape=[jax.ShapeDtypeStruct(_row_tiles_shape(n_rows, D), jnp.uint32),
                   jax.ShapeDtypeStruct((N_EXPERTS,), jnp.int32),
                   jax.ShapeDtypeStruct((N_EXPERTS,), jnp.int32)],
        scratch_shapes=[[sbuf] * nt,
                        pltpu.VMEM(_row_tiles_shape(MOE_ROWS // 2, D), jnp.uint32),
                        pltpu.SMEM((N_EXPERTS,), jnp.int32), pltpu.SemaphoreType.DMA((nt,)),
                        pltpu.SemaphoreType.DMA],
        compiler_params=_cparams(("arbitrary",)),
        name="moe_dispatch",
    )(tinfo, cnt, tok, pos)


def _experts_kernel(first_ref, nblk_ref, xs_ref, wg_ref, wu_ref, wd_ref, ys_ref,
                    wg_s, wu_s, wd_s, xbuf, ybuf, in_sems, out_sems):
    e = pl.program_id(0)
    last_e = pl.num_programs(0) - 1
    rows = MOE_ROWS
    n_total = xs_ref.shape[0] // rows
    used = first_ref[last_e] + nblk_ref[last_e]

    def fetch(g, slot):
        return pltpu.make_async_copy(xs_ref.at[pl.ds(g * rows, rows)], xbuf.at[pl.ds(slot * rows, rows)],
                                     in_sems.at[slot])

    def flush(g, slot):
        return pltpu.make_async_copy(ybuf.at[pl.ds(slot * rows, rows)], ys_ref.at[pl.ds(g * rows, rows)],
                                     out_sems.at[slot])

    @pl.when((e == 0) & (used > 0))
    def _():
        fetch(0, 0).start(priority=ROW_DMA_PRIORITY)

    @pl.when(nblk_ref[e] > 0)
    def _():
        wg_s[...] = wg_ref[...].astype(bf16)
        wu_s[...] = wu_ref[...].astype(bf16)
        wd_s[...] = wd_ref[...].astype(bf16)

    def block(b, carry):
        g = first_ref[e] + b
        slot = g % 2
        fetch(g, slot).wait()

        @pl.when(g + 1 < used)
        def _():
            fetch(g + 1, 1 - slot).start(priority=ROW_DMA_PRIORITY)

        @pl.when(g >= 2)
        def _():
            flush(g - 2, slot).wait()
        half = rows // EXPERT_ROW_GROUPS
        starts = [slot * rows + r * half for r in range(EXPERT_ROW_GROUPS)]
        gated = []
        for row0 in starts:
            xb = _unpack_bf16_pairs(_load_row_tiles(xbuf, row0, half)).astype(bf16)
            gated.append((jnp.dot(xb, wg_s[...], preferred_element_type=f32),
                          jnp.dot(xb, wu_s[...], preferred_element_type=f32)))
        for row0, (a, u) in zip(starts, gated):
            h = (a * jax.nn.sigmoid(a) * u).astype(bf16)
            _store_row_tiles(ybuf, _pack_bf16_pairs(jnp.dot(h, wd_s[...], preferred_element_type=f32)), row0)
        flush(g, slot).start(priority=ROW_DMA_PRIORITY)
        return carry
    lax.fori_loop(0, nblk_ref[e], block, 0)

    @pl.when(e == last_e)
    def _():
        for back in (1, 2):
            @pl.when(used >= back)
            def _(back=back):
                flush(used - back, (used - back) % 2).wait()
        ybuf[pl.ds(0, rows)] = jnp.zeros((rows,) + ybuf.shape[1:], ybuf.dtype)
        for wait in (False, True):
            def zero_tail(g, c, wait=wait):
                cp = flush(g, 0)
                if wait:
                    cp.wait()
                else:
                    cp.start()
                return c
            lax.fori_loop(used, n_total, zero_tail, 0)


def _experts(first_block, n_blocks, xs, w_gate, w_up, w_down):
    _, D, De = w_gate.shape
    widx = lambda e, first, nblk: (e, 0, 0)
    buf = pltpu.VMEM(_row_tiles_shape(2 * MOE_ROWS, D), jnp.uint32)
    return pl.pallas_call(
        _experts_kernel,
        grid_spec=pltpu.PrefetchScalarGridSpec(
            num_scalar_prefetch=2,
            grid=(N_EXPERTS,),
            in_specs=[pl.BlockSpec(memory_space=pl.ANY),
                      pl.BlockSpec((None, D, De), widx),
                      pl.BlockSpec((None, D, De), widx),
                      pl.BlockSpec((None, De, D), widx)],
            out_specs=pl.BlockSpec(memory_space=pl.ANY),
            scratch_shapes=[pltpu.VMEM((D, De), bf16), pltpu.VMEM((D, De), bf16), pltpu.VMEM((De, D), bf16),
                            buf, buf, pltpu.SemaphoreType.DMA((2,)), pltpu.SemaphoreType.DMA((2,))],
        ),
        out_shape=jax.ShapeDtypeStruct(xs.shape, jnp.uint32),
        compiler_params=_cparams(("arbitrary",)),
        name="moe_experts",
    )(first_block, n_blocks, xs, w_gate, w_up, w_down)


def _combine_kernel(tinfo_ref, tnext_ref, cnt_ref, h2_ref, vec_ref, g_ref, ys_ref, o_ref,
                    gbufs, stages, start_ref, sems):
    tj = vec_ref.shape[0] // TILES_PER_STEP
    step = pl.program_id(0)

    def fetch(info_ref, t, buf):
        _tile_segments(info_ref, t, start_ref, gbufs[buf], False, ys_ref, sems.at[buf], tj)

    @pl.when(step == 0)
    def _():
        _pad_starts(cnt_ref, start_ref)
        for t in range(GATHER_AHEAD):
            fetch(tinfo_ref, t, t)

    slot = lax.broadcasted_iota(jnp.int32, (tj, 2 * tj), 1)

    def finish(k, picked):
        v = vec_ref[k * tj:(k + 1) * tj, :]
        y = h2_ref[k * tj:(k + 1) * tj, :] + v[:, 0:1] * picked[:tj] + v[:, 1:2] * picked[tj:]
        out = _rms(y, g_ref[...])
        stage = stages[k // 2]
        for q in range(stage.shape[0]):
            stage[q, pl.ds(k % 2, tj, stride=2), :] = out[:, q * LANES:(q + 1) * LANES]
        if k % 2 == 1:
            r0 = (k // 2) * 2 * tj
            for q in range(stage.shape[0]):
                o_ref[0, r0:r0 + 2 * tj, q * LANES:(q + 1) * LANES] = stage[q]

    picked = None
    for k in range(TILES_PER_STEP):
        pltpu.make_async_copy(ys_ref.at[pl.ds(0, 2 * tj)], gbufs[k], sems.at[k]).wait()
        ahead = k + GATHER_AHEAD
        if ahead < TILES_PER_STEP:
            fetch(tinfo_ref, ahead, ahead)
        else:
            fetch(tnext_ref, ahead - TILES_PER_STEP, ahead - TILES_PER_STEP)
        rows = _unpack_bf16_pairs(_load_row_tiles(gbufs[k])).astype(bf16)
        v = vec_ref[k * tj:(k + 1) * tj, :]
        pick = jnp.concatenate([(slot == v[:, 2 + c:3 + c].astype(jnp.int32)).astype(bf16) for c in range(2)],
                               axis=0)
        now = jnp.dot(pick, rows, preferred_element_type=f32)
        if k > 0:
            finish(k - 1, picked)
        picked = now
    finish(TILES_PER_STEP - 1, picked)


def _combine(tinfo, cnt, h2, vec, final_g, ys, B, L):
    T, D = h2.shape
    ntiles = tinfo.shape[0]
    tj = T // ntiles
    nt = TILES_PER_STEP
    assert GATHER_AHEAD < nt and nt % 2 == 0 and (L // tj) % nt == 0
    nsteps = ntiles // nt
    per_seq = L // (nt * tj)
    g = final_g.reshape(1, D)
    gbuf = pltpu.VMEM(_row_tiles_shape(2 * tj, D), jnp.uint32)
    return pl.pallas_call(
        _combine_kernel,
        grid=(nsteps,),
        in_specs=[pl.BlockSpec((nt, SUBLANES, LANES), lambda i: (i, 0, 0), memory_space=pltpu.SMEM),
                  pl.BlockSpec((nt, SUBLANES, LANES), lambda i: (i + 1, 0, 0), memory_space=pltpu.SMEM),
                  pl.BlockSpec(memory_space=pltpu.SMEM),
                  pl.BlockSpec((nt * tj, D), lambda i: (i, 0)),
                  pl.BlockSpec((nt * tj, LANES), lambda i: (i, 0)),
                  pl.BlockSpec((1, D), lambda i: (0, 0)),
                  pl.BlockSpec(memory_space=pl.ANY)],
        out_specs=pl.BlockSpec((1, nt * tj, D), lambda i: (i // per_seq, i % per_seq, 0)),
        out_shape=jax.ShapeDtypeStruct((B, L, D), f32),
        scratch_shapes=[[gbuf] * nt,
                        [pltpu.VMEM((D // LANES, 2 * tj, LANES), f32)] * (nt // 2),
                        pltpu.SMEM((N_EXPERTS,), jnp.int32), pltpu.SemaphoreType.DMA((nt,))],
        compiler_params=_cparams(("arbitrary",)),
        name="moe_combine",
    )(tinfo, jnp.pad(tinfo, ((0, nt), (0, 0), (0, 0))), cnt, h2, vec, g, ys)


def kernel(x, norm1_g, w_in, conv_w, conv_b, f_w_in, f_b_in, f_w_mid, f_b_mid, f_freq, f_w_out, f_bias,
           mix_g, w_out, norm2_g, w_group, b_group, w_router, b_router, w_gate, w_up, w_down, final_g):
    B, L, D = x.shape
    depth = norm1_g.shape[0]
    assert depth == 1, "the final RMSNorm is fused into the single layer's MoE combine"
    C = D // 2
    Lh = L // 2
    N = B * C
    T = B * L
    n_rows = T * 2 + N_EXPERTS * MOE_ROWS
    cfou = _fourier_consts(L)
    i = 0
    wcat = _prep_win(w_in[i], C)
    hcat, asum = _hyena_filters(L, C, f_w_in[i], f_b_in[i], f_w_mid[i], f_b_mid[i], f_freq[i], f_w_out[i])
    z, x0, u4, cfwd, cinv = _inproj(x, norm1_g[i], wcat, conv_w[i], conv_b[i], C)
    kspec, k0 = _filter_spectrum(cfwd, hcat, asum, L, C)
    z = z.reshape(2, Lh, N)
    x0 = x0.reshape(2, Lh, N)
    u4 = u4.reshape(4, Lh, N)
    s = _conv_fwd(cfwd, z, kspec, k0, C)
    yh = _conv_inv(cinv, s, x0, z, f_bias[i], C)
    yf = _fourier_dft(cfou, u4, L, C)
    wr = jnp.concatenate([jnp.transpose(w_router[i], (1, 0, 2)).reshape(D, N_EXPERTS), w_group[i]], axis=1)
    wr = jnp.pad(wr, ((0, 0), (0, LANES - wr.shape[1])))
    br = jnp.pad(jnp.concatenate([b_router[i].reshape(-1), b_group[i]]), (0, LANES - N_EXPERTS - N_GROUPS))
    h2, tok, pos, vec, tinfo, cnt = _outproj_router(yh, yf, x, mix_g[i], w_out[i].astype(bf16), norm2_g[i],
                                                     wr, br.reshape(1, LANES), C)
    xs, first_block, n_blocks = _dispatch(tinfo, cnt, tok, pos, n_rows)
    ys = _experts(first_block, n_blocks, xs, w_gate[i], w_up[i], w_down[i])
    return _combine(tinfo, cnt, h2, vec, final_g, ys, B, L)
```

```python
import functools
import math

import jax
import jax.numpy as jnp
from jax import lax
from jax.experimental import pallas as pl
from jax.experimental.pallas import tpu as pltpu

HEAD_DIM = 64
N_GROUPS = 4
EXPERTS_PER_GROUP = 8
N_EXPERTS = N_GROUPS * EXPERTS_PER_GROUP
FILTER_BANDS = 16
DECAY_FAST_PCT = 0.3
DECAY_SLOW_PCT = 1.5
DECAY_TARGET = 1e-2
EPS = 1e-6

LANES = 128
SUBLANES = 8
MXU_DIM = 256
CONV_ROWS = 512
CONV_COLS = 512
FOURIER_COLS = 256
MOE_ROWS = 512
EXPERT_ROW_GROUPS = 2
FETCH_AHEAD = 2
ROW_DMA_PRIORITY = 1
ROW_TILES_PER_STEP = 2
TILES_PER_STEP = 4
GATHER_AHEAD = 3
SEG_SPLIT = 32
VMEM_LIMIT = 56 * 1024 * 1024

f32 = jnp.float32
bf16 = jnp.bfloat16
HI = lax.Precision.HIGHEST


def _cparams(sem):
    return pltpu.CompilerParams(dimension_semantics=sem, vmem_limit_bytes=VMEM_LIMIT)


def _tile(pref, n):
    return min(pref, n)


_SIN_TERMS = tuple((-1.0) ** k / math.factorial(2 * k + 1) for k in range(7))
_COS_TERMS = tuple((-1.0) ** k / math.factorial(2 * k) for k in range(8))


def _cos_sin_of_turn_fraction(prod, m):
    num = prod & (m - 1)
    quadrant = num // (m // 4)
    x = (num & (m // 4 - 1)).astype(f32) * (2.0 * math.pi / m)
    x2 = x * x
    s = jnp.full_like(x, _SIN_TERMS[-1])
    for c in _SIN_TERMS[-2::-1]:
        s = s * x2 + c
    s = s * x
    c_ = jnp.full_like(x, _COS_TERMS[-1])
    for c in _COS_TERMS[-2::-1]:
        c_ = c_ * x2 + c
    odd = (quadrant & 1) == 1
    cos = jnp.where(odd, s, c_)
    sin = jnp.where(odd, c_, s)
    cos = jnp.where((quadrant == 1) | (quadrant == 2), -cos, cos)
    sin = jnp.where(quadrant >= 2, -sin, sin)
    return cos, sin


def _cos_sin_blocks(a, s, b0, m, ncb, d):
    tc, ts = _cos_sin_of_turn_fraction(a * (s * d), m)
    b0_lane = jnp.zeros(d.shape, jnp.int32)
    for cb in range(ncb):
        b0_lane = jnp.where(d == cb, b0(cb), b0_lane)
    ac, asn = _cos_sin_of_turn_fraction(a * b0_lane, m)
    for cb in range(ncb):
        ca = ac[:, cb:cb + 1]
        sa = asn[:, cb:cb + 1]
        yield cb, ca * tc - sa * ts, sa * tc + ca * ts


def _write_dft_rows(fwd_ref, inv_ref, row_block, L):
    Lh = L // 2
    tr = fwd_ref.shape[1]
    r = lax.broadcasted_iota(jnp.int32, (tr, LANES), 0) + row_block * tr
    d = lax.broadcasted_iota(jnp.int32, (tr, LANES), 1)
    alt_r = (1 - 2 * (r & 1)).astype(f32)
    alt_d = (1 - 2 * (d & 1)).astype(f32)
    ncb = Lh // LANES
    fams = [
        (r, 1, lambda cb: cb * LANES, L),
        (r, 2, lambda cb: 2 * cb * LANES + 1, 2 * L),
        (2 * r + 1, 1, lambda cb: cb * LANES, 2 * L),
    ]
    for fi, (a, s, b0, m) in enumerate(fams):
        for cb, c, sn in _cos_sin_blocks(a, s, b0, m, ncb, d):
            sl = slice(cb * LANES, (cb + 1) * LANES)
            if fi == 0:
                fwd_ref[0, :, sl] = c.astype(bf16)
                inv_ref[0, :, sl] = c.astype(bf16)
                fwd_ref[1, :, sl] = jnp.where(r == 0, alt_d, sn).astype(bf16)
                nst = -sn
                if cb == 0:
                    nst = jnp.where(d == 0, -alt_r, nst)
                inv_ref[1, :, sl] = nst.astype(bf16)
            elif fi == 1:
                fwd_ref[2, :, sl] = c.astype(bf16)
                fwd_ref[3, :, sl] = jnp.where(r == 0, alt_d, sn).astype(bf16)
            else:
                inv_ref[2, :, sl] = c.astype(bf16)
                nst = -sn
                if cb == 0:
                    nst = jnp.where(d == 0, -alt_r, nst)
                inv_ref[3, :, sl] = nst.astype(bf16)


def _fourier_consts_kernel(o_ref, *, L):
    n = L // 4
    tr = o_ref.shape[0]
    r = lax.broadcasted_iota(jnp.int32, (tr, LANES), 0) + pl.program_id(0) * tr
    d = lax.broadcasted_iota(jnp.int32, (tr, LANES), 1)

    def col_l(cb):
        parity, j0 = divmod(cb * LANES, n // 2)
        return 2 * j0 + parity

    for cb, c, sn in _cos_sin_blocks(r, 2, col_l, n, n // LANES, d):
        o_ref[:, cb * LANES:(cb + 1) * LANES] = c.astype(bf16)
        o_ref[:, n + cb * LANES:n + (cb + 1) * LANES] = (-sn).astype(bf16)


def _fourier_consts(L):
    n = L // 4
    tr = _tile(256, n)
    return pl.pallas_call(
        functools.partial(_fourier_consts_kernel, L=L),
        grid=(n // tr,),
        out_specs=pl.BlockSpec((tr, 2 * n), lambda i: (i, 0)),
        out_shape=jax.ShapeDtypeStruct((n, 2 * n), bf16),
        compiler_params=_cparams(("parallel",)),
        name="fourier_consts",
    )()


def _prep_win_kernel(w_ref, o_ref, *, C):
    w = w_ref[...]
    i = lax.broadcasted_iota(jnp.int32, (LANES, LANES), 0)
    j = lax.broadcasted_iota(jnp.int32, (LANES, LANES), 1)
    same = (i // HEAD_DIM) == (j // HEAD_DIM)
    cos, sin = _cos_sin_of_turn_fraction((i % HEAD_DIM) * (j % HEAD_DIM), HEAD_DIM)
    bc = jnp.where(same, cos, 0.0)
    bs = jnp.where(same, sin, 0.0)
    o_ref[:, :3 * C] = w[:, :3 * C].astype(bf16)
    for k in range(0, C, LANES):
        wf = w[:, 3 * C + k:3 * C + k + LANES]
        o_ref[:, 3 * C + k:3 * C + k + LANES] = jnp.dot(wf, bc, precision=HI, preferred_element_type=f32).astype(bf16)
        o_ref[:, 4 * C + k:4 * C + k + LANES] = jnp.dot(wf, bs, precision=HI, preferred_element_type=f32).astype(bf16)


def _prep_win(w_in, C):
    D = w_in.shape[0]
    tr = _tile(256, D)
    return pl.pallas_call(
        functools.partial(_prep_win_kernel, C=C),
        grid=(D // tr,),
        in_specs=[pl.BlockSpec((tr, 4 * C), lambda i: (i, 0))],
        out_specs=pl.BlockSpec((tr, 5 * C), lambda i: (i, 0)),
        out_shape=jax.ShapeDtypeStruct((D, 5 * C), bf16),
        compiler_params=_cparams(("parallel",)),
        name="prep_win",
    )(w_in)


def _write_filter_rows(win_ref, bin_ref, wmid_ref, bmid_ref, freq_ref, wout_ref, h_ref, asum_ref, step, L, C):
    tr = h_ref.shape[1]
    lane = lax.broadcasted_iota(jnp.int32, (tr, LANES), 1)
    row = lax.broadcasted_iota(jnp.int32, (tr, LANES), 0) + step * tr
    band_i = jnp.where(lane <= FILTER_BANDS, lane - 1, lane - 1 - FILTER_BANDS)
    band = 1e-4 + band_i.astype(f32) * ((FILTER_BANDS - 1 - 1e-4) / (FILTER_BANDS - 1))
    ch = lax.broadcasted_iota(jnp.int32, (tr, 2 * C), 1) % C
    max_decay = math.log(DECAY_TARGET) / DECAY_FAST_PCT
    min_decay = math.log(DECAY_TARGET) / DECAY_SLOW_PCT
    delta = jnp.abs(min_decay + ch.astype(f32) * ((max_decay - min_decay) / (C - 1)))
    is_bwd = lax.broadcasted_iota(jnp.int32, (tr, 2 * C), 1) >= C
    row_c = lax.broadcasted_iota(jnp.int32, (tr, 2 * C), 0) + step * tr

    @pl.when(step == 0)
    def _():
        asum_ref[...] = jnp.zeros_like(asum_ref)

    total = jnp.zeros((1, 2 * C), f32)
    for p in range(2):
        pos = (2 * row + p).astype(f32)
        t = pos * (1.0 / (L - 1))
        w = pos * (2.0 * math.pi / L)
        fw = band * w
        z = jnp.where(lane == 0, t,
                      jnp.where(lane <= FILTER_BANDS, jnp.cos(fw),
                                jnp.where(lane <= 2 * FILTER_BANDS, -jnp.sin(fw), 0.0)))
        h = jnp.sin(freq_ref[0:1, :] * (jnp.dot(z, win_ref[...], precision=HI, preferred_element_type=f32)
                                       + bin_ref[...]))
        for i in range(wmid_ref.shape[0]):
            h = jnp.sin(freq_ref[i + 1:i + 2, :]
                        * (jnp.dot(h, wmid_ref[i], precision=HI, preferred_element_type=f32)
                           + bmid_ref[i:i + 1, :]))
        out = jnp.dot(h, wout_ref[...], precision=HI, preferred_element_type=f32)
        tc = (2 * row_c + p).astype(f32) * (1.0 / (L - 1))
        out = out * jnp.exp(-tc * delta)
        if p == 0:
            out = jnp.where(is_bwd & (row_c == 0), 0.0, out)
        h_ref[p] = out.astype(bf16)
        total = total + jnp.sum(jnp.abs(out), axis=0, keepdims=True)
    asum_ref[0:1, :] += total


def _filter_kernel(win_ref, bin_ref, wmid_ref, bmid_ref, freq_ref, wout_ref, h_ref, asum_ref, *, L, C):
    _write_filter_rows(win_ref, bin_ref, wmid_ref, bmid_ref, freq_ref, wout_ref, h_ref, asum_ref,
                       pl.program_id(0), L, C)


def _hyena_filters(L, C, f_w_in, f_b_in, f_w_mid, f_b_mid, f_freq, f_w_out):
    Lh = L // 2
    tr = _tile(256, Lh)
    order = f_w_in.shape[1]
    win = jnp.pad(f_w_in, ((0, LANES - f_w_in.shape[0]), (0, 0)))
    full = lambda a: pl.BlockSpec(a.shape, lambda i: (0,) * a.ndim)
    args = (win, f_b_in.reshape(1, order), f_w_mid, f_b_mid, f_freq, f_w_out)
    return pl.pallas_call(
        functools.partial(_filter_kernel, L=L, C=C),
        grid=(Lh // tr,),
        in_specs=[full(a) for a in args],
        out_specs=[pl.BlockSpec((2, tr, 2 * C), lambda i: (0, i, 0)),
                   pl.BlockSpec((SUBLANES, 2 * C), lambda i: (0, 0))],
        out_shape=[jax.ShapeDtypeStruct((2, Lh, 2 * C), bf16),
                   jax.ShapeDtypeStruct((SUBLANES, 2 * C), f32)],
        compiler_params=_cparams(("arbitrary",)),
        name="hyena_filters",
    )(*args)


def _fwd_products(c_ref, x_ref):
    tm, k = c_ref.shape[1:]
    even = jnp.dot(c_ref[0:2].reshape(2 * tm, k), x_ref[0], preferred_element_type=f32)
    odd = jnp.dot(c_ref[2:4].reshape(2 * tm, k), x_ref[1], preferred_element_type=f32)
    return even[:tm], even[tm:], odd[:tm], odd[tm:]


def _filter_spectrum_kernel(c_ref, h_ref, asum_ref, k_ref, k0_ref, *, L, C):
    ce, se, co, so = _fwd_products(c_ref, h_ref)
    a = asum_ref[0:1, :]
    scale = 1.0 / ((a[:, :C] + a[:, C:]) * L)
    F = lambda v: v[:, :C]
    G = lambda v: v[:, C:]
    k_ref[0] = (F(ce) + F(co) + G(ce) + G(co)) * scale
    k_ref[1] = (-(F(se) + F(so)) + (G(se) + G(so))) * scale
    k_ref[2] = (F(ce) - F(co) + G(ce) - G(co)) * scale
    k_ref[3] = ((F(se) - F(so)) - (G(se) - G(so))) * scale

    @pl.when(pl.program_id(0) == 0)
    def _():
        k0_ref[...] = jnp.zeros_like(k0_ref)
        k0_ref[0:1, :] = ((F(se) + G(se)) * scale)[0:1, :]
        k0_ref[1:2, :] = ((F(so) - G(so)) * scale)[0:1, :]


def _filter_spectrum(cfwd, hcat, asum, L, C):
    Lh = L // 2
    tm = _tile(256, Lh)
    return pl.pallas_call(
        functools.partial(_filter_spectrum_kernel, L=L, C=C),
        grid=(Lh // tm,),
        in_specs=[pl.BlockSpec((4, tm, Lh), lambda i: (0, i, 0)),
                  pl.BlockSpec((2, Lh, 2 * C), lambda i: (0, 0, 0)),
                  pl.BlockSpec((SUBLANES, 2 * C), lambda i: (0, 0))],
        out_specs=[pl.BlockSpec((4, tm, C), lambda i: (0, i, 0)),
                   pl.BlockSpec((SUBLANES, C), lambda i: (0, 0))],
        out_shape=[jax.ShapeDtypeStruct((4, Lh, C), f32),
                   jax.ShapeDtypeStruct((SUBLANES, C), f32)],
        compiler_params=_cparams(("arbitrary",)),
        name="filter_spectrum",
    )(cfwd, hcat, asum)


def _conv_fwd_kernel(c_ref, z_ref, k_ref, k0_ref, s_ref):
    products = _fwd_products(c_ref, z_ref)
    C = k_ref.shape[2]
    first = pl.program_id(1) == 0
    for h in range(z_ref.shape[2] // C):
        cols = slice(h * C, (h + 1) * C)
        ce, se, co, so = (v[:, cols] for v in products)
        zsr, zsi, zdr, zdi = ce + co, -(se + so), ce - co, se - so
        ksr, ksi, kdr, kdi = k_ref[0], k_ref[1], k_ref[2], k_ref[3]
        psr = zsr * ksr - zsi * ksi
        psi = zsr * ksi + zsi * ksr
        pdr = zdr * kdr - zdi * kdi
        pdi = zdr * kdi + zdi * kdr
        gen = (psr + pdr, psi - pdi, psr - pdr, psi + pdi)

        @pl.when(first)
        def _(cols=cols, ce=ce, se=se, so=so, zsr=zsr, zdr=zdr, ksr=ksr, kdr=kdr, gen=gen):
            row0 = lax.broadcasted_iota(jnp.int32, ce.shape, 0) == 0
            p0 = zsr * ksr
            pl_ = zdr * kdr
            ak, bk = k0_ref[0:1, :], k0_ref[1:2, :]
            phr = se * ak - so * bk
            phi = -(se * bk + so * ak)
            spec = (0.5 * (p0 + pl_), -phr, 0.5 * (p0 - pl_), phi)
            for i in range(4):
                s_ref[i, :, cols] = jnp.where(row0, spec[i], gen[i]).astype(bf16)

        @pl.when(jnp.logical_not(first))
        def _(cols=cols, gen=gen):
            for i in range(4):
                s_ref[i, :, cols] = gen[i].astype(bf16)


def _conv_fwd(cfwd, z, kspec, k0, C):
    _, Lh, N = z.shape
    tm = _tile(CONV_ROWS, Lh)
    tn = _tile(CONV_COLS, N)
    return pl.pallas_call(
        _conv_fwd_kernel,
        grid=(N // tn, Lh // tm),
        in_specs=[pl.BlockSpec((4, tm, Lh), lambda n, i: (0, i, 0)),
                  pl.BlockSpec((2, Lh, tn), lambda n, i: (0, 0, n)),
                  pl.BlockSpec((4, tm, C), lambda n, i: (0, i, 0)),
                  pl.BlockSpec((SUBLANES, C), lambda n, i: (0, 0))],
        out_specs=pl.BlockSpec((4, tm, tn), lambda n, i: (0, i, n)),
        out_shape=jax.ShapeDtypeStruct((4, Lh, N), bf16),
        compiler_params=_cparams(("parallel", "arbitrary")),
        name="conv_fwd",
    )(cfwd, z, kspec, k0)


def _pair_products(c_ref, r_ref):
    ev = (jnp.dot(c_ref[0], r_ref[0], preferred_element_type=f32)
          + jnp.dot(c_ref[1], r_ref[1], preferred_element_type=f32))
    od = (jnp.dot(c_ref[2], r_ref[2], preferred_element_type=f32)
          + jnp.dot(c_ref[3], r_ref[3], preferred_element_type=f32))
    return ev, od


def _conv_inv_kernel(c_ref, s_ref, x0_ref, z_ref, fb_ref, y_ref):
    conv = _pair_products(c_ref, s_ref)
    C = fb_ref.shape[1]
    fb = fb_ref[...]
    for p in range(2):
        for h in range(z_ref.shape[2] // C):
            cols = slice(h * C, (h + 1) * C)
            zp = z_ref[p, :, cols].astype(f32)
            y_ref[p, :, cols] = (x0_ref[p, :, cols].astype(f32) * (conv[p][:, cols] + zp * fb)).astype(bf16)


def _conv_inv(cinv, s, x0, z, f_bias, C):
    _, Lh, N = s.shape
    tm = _tile(CONV_ROWS, Lh)
    tn = _tile(CONV_COLS, N)
    return pl.pallas_call(
        _conv_inv_kernel,
        grid=(N // tn, Lh // tm),
        in_specs=[pl.BlockSpec((4, tm, Lh), lambda n, i: (0, i, 0)),
                  pl.BlockSpec((4, Lh, tn), lambda n, i: (0, 0, n)),
                  pl.BlockSpec((2, tm, tn), lambda n, i: (0, i, n)),
                  pl.BlockSpec((2, tm, tn), lambda n, i: (0, i, n)),
                  pl.BlockSpec((1, C), lambda n, i: (0, 0))],
        out_specs=pl.BlockSpec((2, tm, tn), lambda n, i: (0, i, n)),
        out_shape=jax.ShapeDtypeStruct((2, Lh, N), bf16),
        compiler_params=_cparams(("parallel", "arbitrary")),
        name="conv_inv",
    )(cinv, s, x0, z, f_bias.reshape(1, C))


def _fourier_kernel(c_ref, u_ref, y_ref, rhs, stage, tw1, tw2, *, L, scale):
    Lh, n, H = L // 2, L // 4, L // 8
    tn = u_ref.shape[2]

    @pl.when(pl.program_id(0) == 0)
    def _():
        r1 = lax.broadcasted_iota(jnp.int32, (Lh, LANES), 0)
        tw1[0], tw1[1] = _cos_sin_of_turn_fraction(2 * (r1 % n) + r1 // n, L)
        r2 = lax.broadcasted_iota(jnp.int32, (n, LANES), 0)
        tw2[0], tw2[1] = _cos_sin_of_turn_fraction(2 * (r2 % H) + r2 // H, Lh)

    def rotate(a, b, c, s):
        return c * a - s * b, s * a + c * b

    for q in range(tn // LANES):
        cols = slice(q * LANES, (q + 1) * LANES)
        for p in range(2):
            first = slice(p * n, p * n + H)
            second = slice(p * n + H, (p + 1) * n)
            out = slice(p * H, (p + 1) * H)
            out_b = slice(n + p * H, n + (p + 1) * H)
            c2, s2 = tw2[0, out, :], tw2[1, out, :]

            def emit(sub, a0, b0, a1, b1):
                rhs[sub, out, cols] = (a0 + a1).astype(bf16)
                rhs[sub, out_b, cols] = (b0 + b1).astype(bf16)
                da, db = rotate(a0 - a1, b0 - b1, c2, s2)
                rhs[sub + 1, out, cols] = da.astype(bf16)
                rhs[sub + 1, out_b, cols] = db.astype(bf16)

            ld = lambda k, rows: u_ref[k, rows, cols].astype(f32)
            emit(0, ld(0, first), ld(1, first), ld(0, second), ld(1, second))
            odd = [rotate(ld(2, rows), ld(3, rows), tw1[0, rows, :], tw1[1, rows, :]) for rows in (first, second)]
            emit(2, odd[0][0], odd[0][1], odd[1][0], odd[1][1])

    for parity in range(2):
        pair = [jnp.dot(c_ref[...], rhs[2 * parity + h], preferred_element_type=f32) * scale for h in range(2)]
        for q in range(tn // LANES):
            cols = slice(q * LANES, (q + 1) * LANES)
            for h in range(2):
                stage[q, pl.ds(h, n, stride=2), :] = pair[h][:, cols]
            y_ref[parity, :, cols] = stage[q].astype(bf16)


def _fourier_dft(cfou, u4, L, C):
    _, Lh, N = u4.shape
    n = L // 4
    tn = _tile(FOURIER_COLS, C)
    return pl.pallas_call(
        functools.partial(_fourier_kernel, L=L, scale=1.0 / math.sqrt(L * HEAD_DIM)),
        grid=(N // tn,),
        in_specs=[pl.BlockSpec((n, 2 * n), lambda j: (0, 0)),
                  pl.BlockSpec((4, Lh, tn), lambda j: (0, 0, j))],
        out_specs=pl.BlockSpec((2, Lh, tn), lambda j: (0, 0, j)),
        out_shape=jax.ShapeDtypeStruct((2, Lh, N), bf16),
        scratch_shapes=[pltpu.VMEM((4, 2 * n, tn), bf16), pltpu.VMEM((tn // LANES, Lh, LANES), f32),
                        pltpu.VMEM((2, Lh, LANES), f32), pltpu.VMEM((2, n, LANES), f32)],
        compiler_params=_cparams(("arbitrary",)),
        name="fourier_dft",
    )(cfou, u4)


def _rms(x, g):
    return x * lax.rsqrt(jnp.mean(x * x, axis=-1, keepdims=True) + EPS) * g


def _stage_rows(chunks, x):
    for k in range(chunks.shape[0]):
        chunks[k] = x[:, k * LANES:(k + 1) * LANES]


def _parity_rows(chunks, p, n):
    return jnp.concatenate([chunks[k, pl.ds(p, n, stride=2), :] for k in range(chunks.shape[0])], axis=1)


def _inproj_kernel(xa_ref, xb_ref, pa_ref, na_ref, pb_ref, nb_ref, g_ref, w_ref, cw_ref, cb_ref,
                   z_ref, x0_ref, u_ref, cfwd_ref, cinv_ref, stage, *, C, D, L):
    tj = xa_ref.shape[1] // 2
    jt = pl.program_id(1)
    n_half = pl.num_programs(1)
    _write_dft_rows(cfwd_ref, cinv_ref, pl.program_id(0) * n_half + jt, L)
    g = g_ref[...]
    w = w_ref[...]
    cw0, cw1, cw2 = cw_ref[0:1, :], cw_ref[1:2, :], cw_ref[2:3, :]
    cb = cb_ref[...]
    row = lax.broadcasted_iota(jnp.int32, (tj, 3 * C), 0)
    four = []
    halves = ((xa_ref, pa_ref, na_ref, jt == 0, False),
              (xb_ref, pb_ref, nb_ref, False, jt == n_half - 1))
    for h, (x_ref, p_ref, n_ref, at_start, at_end) in enumerate(halves):
        _stage_rows(stage, x_ref[0])
        lhs = jnp.concatenate([
            _rms(_parity_rows(stage, 0, tj), g).astype(bf16),
            _rms(_parity_rows(stage, 1, tj), g).astype(bf16),
            _rms(p_ref[0], g).astype(bf16),
            _rms(n_ref[0], g).astype(bf16)], axis=0)
        res = jnp.dot(lhs, w, preferred_element_type=f32)
        pe, po = res[:tj, :3 * C], res[tj:2 * tj, :3 * C]
        prev = res[2 * tj + SUBLANES - 1:2 * tj + SUBLANES, :3 * C]
        nxt = res[2 * tj + SUBLANES:2 * tj + SUBLANES + 1, :3 * C]
        if at_start is not False:
            prev = jnp.where(at_start, 0.0, prev)
        if at_end is not False:
            nxt = jnp.where(at_end, 0.0, nxt)
        po_dn = jnp.where(row == 0, prev, pltpu.roll(po, 1, 0))
        pe_up = jnp.where(row == tj - 1, nxt, pltpu.roll(pe, tj - 1, 0))
        uc = (cb + cw0 * po_dn + cw1 * pe + cw2 * po,
              cb + cw0 * pe + cw1 * po + cw2 * pe_up)
        for p in range(2):
            x0_ref[p, h] = uc[p][:, :C].astype(bf16)
            z_ref[p, h] = (uc[p][:, C:2 * C] * uc[p][:, 2 * C:]).astype(bf16)
        four.append((res[:tj, 3 * C:], res[tj:2 * tj, 3 * C:]))
    for p in range(2):
        fa, fb = four[0][p], four[1][p]
        u_ref[0, p] = (fa[:, :C] + fb[:, :C]).astype(bf16)
        u_ref[1, p] = (fa[:, C:] + fb[:, C:]).astype(bf16)
        u_ref[2, p] = (fa[:, :C] - fb[:, :C]).astype(bf16)
        u_ref[3, p] = (fa[:, C:] - fb[:, C:]).astype(bf16)


def _inproj(x, norm_g, wcat, conv_w, conv_b, C):
    B, L, D = x.shape
    Lq = L // 4
    tj = _tile(256, Lq)
    nq = Lq // tj
    N = B * C
    hb = 2 * tj // SUBLANES
    last = L // SUBLANES - 1
    main = lambda off: pl.BlockSpec((1, 2 * tj, D), lambda b, j: (b, j + off, 0))
    prev = lambda off: pl.BlockSpec((1, SUBLANES, D), lambda b, j: (b, jnp.maximum((j + off) * hb - 1, 0), 0))
    nxt = lambda off: pl.BlockSpec((1, SUBLANES, D), lambda b, j: (b, jnp.minimum((j + off + 1) * hb, last), 0))
    full = lambda a: pl.BlockSpec(a.shape, lambda b, j: (0,) * a.ndim)
    g = norm_g.reshape(1, D)
    cb = conv_b.reshape(1, 3 * C)
    Lh = L // 2
    dft_rows = Lh // (B * nq)
    assert dft_rows * B * nq == Lh and dft_rows % (2 * SUBLANES) == 0
    dft_spec = pl.BlockSpec((4, dft_rows, Lh), lambda b, j: (0, b * nq + j, 0))
    dft_shape = jax.ShapeDtypeStruct((4, Lh, Lh), bf16)
    return pl.pallas_call(
        functools.partial(_inproj_kernel, C=C, D=D, L=L),
        grid=(B, nq),
        in_specs=[main(0), main(nq), prev(0), nxt(0), prev(nq), nxt(nq),
                  full(g), full(wcat), full(conv_w), full(cb)],
        out_specs=[pl.BlockSpec((2, 2, tj, C), lambda b, j: (0, 0, j, b)),
                   pl.BlockSpec((2, 2, tj, C), lambda b, j: (0, 0, j, b)),
                   pl.BlockSpec((4, 2, tj, C), lambda b, j: (0, 0, j, b)),
                   dft_spec, dft_spec],
        out_shape=[jax.ShapeDtypeStruct((2, 2, Lq, N), bf16),
                   jax.ShapeDtypeStruct((2, 2, Lq, N), bf16),
                   jax.ShapeDtypeStruct((4, 2, Lq, N), bf16),
                   dft_shape, dft_shape],
        scratch_shapes=[pltpu.VMEM((D // LANES, 2 * tj, LANES), f32)],
        compiler_params=_cparams(("parallel", "arbitrary")),
        name="inproj",
    )(x, x, x, x, x, x, g, wcat, conv_w, cb)


def _pack_bf16_pairs(v, is_bf16_exact=False):
    k = v.shape[1] // 2
    rounded = v if is_bf16_exact else v.astype(bf16).astype(f32)
    hi = pltpu.bitcast(rounded[:, :k], jnp.uint32)
    lo = pltpu.bitcast(rounded[:, k:], jnp.uint32)
    return hi | (lo >> 16)


def _unpack_bf16_pairs(u):
    hi = pltpu.bitcast(u & jnp.uint32(0xFFFF0000), f32)
    lo = pltpu.bitcast(u << 16, f32)
    return jnp.concatenate([hi, lo], axis=1)


def _row_tiles_shape(n, D):
    return (n, D // (2 * LANES), LANES)


def _store_row_tiles(ref, packed, row0=0):
    total, per_row, _ = ref.shape
    n = packed.shape[0]
    flat = ref.reshape(total * per_row, LANES)
    for s in range(per_row):
        flat[pl.ds(row0 * per_row + s, n, stride=per_row), :] = packed[:, s * LANES:(s + 1) * LANES]


def _load_row_tiles(ref, row0=0, n=None):
    total, per_row, _ = ref.shape
    n = total if n is None else n
    flat = ref.reshape(total * per_row, LANES)
    return jnp.concatenate([flat[pl.ds(row0 * per_row + s, n, stride=per_row), :] for s in range(per_row)],
                           axis=1)


def _outproj_router_kernel(yh_ref, yf_ref, x_ref, mg_ref, wo_ref, g2_ref, wr_ref, br_ref,
                           h2_ref, tok_ref, pos_ref, vec_ref, tinfo_ref, cnt_ref, carry_ref, stage, *, C):
    n_sub = pos_ref.shape[0]
    tj = pos_ref.shape[2]
    _stage_rows(stage, x_ref[0])

    @pl.when((pl.program_id(0) == 0) & (pl.program_id(1) == 0))
    def _():
        carry_ref[...] = jnp.zeros_like(carry_ref)

    li = lax.broadcasted_iota(jnp.int32, (LANES, LANES), 0)
    lj = lax.broadcasted_iota(jnp.int32, (LANES, LANES), 1)
    hw = _tile(MXU_DIM, C)
    ai = lax.broadcasted_iota(jnp.int32, (hw, hw), 0)
    aj = lax.broadcasted_iota(jnp.int32, (hw, hw), 1)
    avg = jnp.where(ai // HEAD_DIM == aj // HEAD_DIM, 1.0 / HEAD_DIM, 0.0).astype(bf16)
    upper = (li < lj).astype(bf16)
    ri = lax.broadcasted_iota(jnp.int32, (tj, tj), 0)
    ci = lax.broadcasted_iota(jnp.int32, (tj, tj), 1)
    tri = (ci < ri).astype(bf16)
    wr = wr_ref[...]
    wr_hi = wr.astype(bf16)
    wr_lo = (wr - wr_hi.astype(f32)).astype(bf16)
    wr_both = jnp.concatenate([wr_hi, wr_lo], axis=1)
    lane = lax.broadcasted_iota(jnp.int32, (n_sub * tj, LANES), 1).astype(f32)
    sub = lax.broadcasted_iota(jnp.int32, (SUBLANES, LANES), 0)
    neg = jnp.float32(-jnp.inf)
    big = jnp.float32(LANES)

    def first_argmax(vals, mask):
        v = jnp.where(mask, vals, neg)
        m = jnp.max(v, axis=1, keepdims=True)
        idx = jnp.min(jnp.where(mask & (v == m), lane, big), axis=1, keepdims=True)
        return m, idx

    normed = []
    for s in range(n_sub):
        r, p = divmod(s, 2)
        parts = []
        for i, ref in enumerate((yh_ref, yf_ref)):
            y = ref[p, r * tj:(r + 1) * tj, :]
            ysq = y * y
            ms = jnp.concatenate([jnp.dot(ysq[:, k:k + hw], avg, preferred_element_type=f32)
                                  for k in range(0, C, hw)], axis=1)
            scale = lax.rsqrt(ms + EPS) * mg_ref[:, i * C:(i + 1) * C]
            parts.append(y * scale.astype(bf16))
        normed.append(jnp.concatenate(parts, axis=1))
    proj = jnp.dot(jnp.concatenate(normed, axis=0), wo_ref[...], preferred_element_type=f32)
    t_hi, t_lo = [], []
    for s in range(n_sub):
        r, p = divmod(s, 2)
        rows = slice(s * tj, (s + 1) * tj)
        h2 = _parity_rows(stage, r * 2 * tj + p, tj) + proj[rows]
        h2_ref[rows, :] = h2
        tok = _rms(h2, g2_ref[...])
        hi = tok.astype(bf16)
        tok_ref[rows, :] = hi
        t_hi.append(hi)
        t_lo.append((tok - hi.astype(f32)).astype(bf16))
    hi_terms = jnp.dot(jnp.concatenate(t_hi, axis=0), wr_both, preferred_element_type=f32)
    logits = (hi_terms[:, :LANES] + hi_terms[:, LANES:]
              + jnp.dot(jnp.concatenate(t_lo, axis=0), wr_hi, preferred_element_type=f32) + br_ref[...])

    gmask = (lane >= N_EXPERTS) & (lane < N_EXPERTS + N_GROUPS)
    gmax, gidx = first_argmax(logits, gmask)
    gsel = gidx - N_EXPERTS
    p_group = 1.0 / jnp.sum(jnp.where(gmask, jnp.exp(logits - gmax), 0.0), axis=1, keepdims=True)
    emask = (lane >= gsel * EXPERTS_PER_GROUP) & (lane < (gsel + 1) * EXPERTS_PER_GROUP)
    l0, e0 = first_argmax(logits, emask)
    l1, e1 = first_argmax(logits, emask & (lane != e0))
    ex = jnp.exp(l1 - l0)
    w0 = p_group / (1.0 + ex)
    w1 = p_group * ex / (1.0 + ex)

    oh0 = (lane == e0).astype(f32)
    oh1 = (lane == e1).astype(f32)
    both = (oh0 + oh1).astype(bf16)
    where, counts = [], []
    for s in range(n_sub):
        tile_both = both[s * tj:(s + 1) * tj, :]
        ranks = jnp.dot(tri, tile_both, preferred_element_type=f32)
        n_e = ranks[tj - 1:tj, :] + tile_both[tj - 1:tj, :].astype(f32)
        n_e8 = jnp.broadcast_to(n_e, (SUBLANES, LANES))
        tile_off = jnp.dot(n_e8.astype(bf16), upper, preferred_element_type=f32)[0:1, :]
        where.append(ranks + tile_off)
        counts.append(n_e8)
    where = jnp.concatenate(where, axis=0)
    pos0 = jnp.sum(oh0 * where, axis=1, keepdims=True)
    pos1 = jnp.sum(oh1 * where, axis=1, keepdims=True)
    cols = jnp.where(lane == 0, w0, jnp.where(lane == 1, w1,
                     jnp.where(lane == 2, pos0, jnp.where(lane == 3, pos1, 0.0))))
    vec_ref[...] = cols
    base = carry_ref[...]
    for s in range(n_sub):
        pos_ref[s] = cols[s * tj:(s + 1) * tj, :].T[:SUBLANES, :].astype(jnp.int32)
        long_seg = (jnp.max(counts[s], axis=1, keepdims=True) >= SEG_SPLIT).astype(f32)
        tinfo_ref[s] = jnp.where(sub == 0, counts[s], jnp.where(sub == 1, base, jnp.where(sub == 2, long_seg, 0.0))
                                 ).astype(jnp.int32)
        base = base + counts[s]
    carry_ref[...] = base
    cnt_ref[...] = base.astype(jnp.int32)


def _outproj_router(yh, yf, x, mix_g, wo, norm2_g, wr, br, C):
    B, L, D = x.shape
    Lh = L // 2
    tj = _tile(256, Lh)
    rt = ROW_TILES_PER_STEP
    ns = 2 * rt
    nj = Lh // (rt * tj)
    T = B * L
    ntiles = T // tj
    lin = lambda b, j: b * nj + j
    full = lambda a: pl.BlockSpec(a.shape, lambda b, j: (0,) * a.ndim)
    mg = mix_g.reshape(1, D)
    g2 = norm2_g.reshape(1, D)
    return pl.pallas_call(
        functools.partial(_outproj_router_kernel, C=C),
        grid=(B, nj),
        in_specs=[pl.BlockSpec((2, rt * tj, C), lambda b, j: (0, j, b)),
                  pl.BlockSpec((2, rt * tj, C), lambda b, j: (0, j, b)),
                  pl.BlockSpec((1, ns * tj, D), lambda b, j: (b, j, 0)),
                  full(mg), full(wo), full(g2), full(wr), full(br)],
        out_specs=[pl.BlockSpec((ns * tj, D), lambda b, j: (lin(b, j), 0)),
                   pl.BlockSpec((ns * tj, D), lambda b, j: (lin(b, j), 0)),
                   pl.BlockSpec((ns, SUBLANES, tj), lambda b, j: (lin(b, j), 0, 0)),
                   pl.BlockSpec((ns * tj, LANES), lambda b, j: (lin(b, j), 0)),
                   pl.BlockSpec((ns, SUBLANES, LANES), lambda b, j: (lin(b, j), 0, 0)),
                   pl.BlockSpec((SUBLANES, LANES), lambda b, j: (0, 0))],
        out_shape=[jax.ShapeDtypeStruct((T, D), f32),
                   jax.ShapeDtypeStruct((T, D), bf16),
                   jax.ShapeDtypeStruct((ntiles, SUBLANES, tj), jnp.int32),
                   jax.ShapeDtypeStruct((T, LANES), f32),
                   jax.ShapeDtypeStruct((ntiles, SUBLANES, LANES), jnp.int32),
                   jax.ShapeDtypeStruct((SUBLANES, LANES), jnp.int32)],
        scratch_shapes=[pltpu.VMEM((SUBLANES, LANES), f32), pltpu.VMEM((D // LANES, ns * tj, LANES), f32)],
        compiler_params=_cparams(("arbitrary", "arbitrary")),
        name="outproj_router",
    )(yh, yf, x, mg, wo, g2, wr, br)


def _padded(c):
    return ((c + MOE_ROWS - 1) // MOE_ROWS) * MOE_ROWS


def _pad_starts(cnt_ref, start_ref):
    def body(e, acc):
        start_ref[e] = acc
        return acc + _padded(cnt_ref[0, e])
    return lax.fori_loop(0, N_EXPERTS, body, jnp.int32(0))


def _segment_copies(n, src, src_off, dst, dst_off, sem, max_rows, wait, src_advances=True, min_rows=1):
    bit = max_rows
    while bit >= min_rows:
        take = n & bit

        @pl.when(take != 0)
        def _(src_off=src_off, dst_off=dst_off, bit=bit):
            cp = pltpu.make_async_copy(src.at[pl.ds(src_off, bit)], dst.at[pl.ds(dst_off, bit)], sem)
            if wait:
                cp.wait()
            else:
                cp.start(priority=ROW_DMA_PRIORITY)
        if src_advances:
            src_off = src_off + take
        dst_off = dst_off + take
        bit //= 2


def _tile_segments(tinfo_ref, t, start_ref, local, local_is_src, remote, sem, tj):
    def copies(long_part):
        off = 0
        for e in range(N_EXPERTS):
            n = tinfo_ref[t, 0, e]
            far = start_ref[e] + tinfo_ref[t, 1, e]
            if long_part:
                args = (n, off, far, tj, SEG_SPLIT)
            else:
                head = n & ~(SEG_SPLIT - 1)
                args = (n, off + head, far + head, SEG_SPLIT // 2, 1)
            n_, near_, far_, hi, lo = args
            if local_is_src:
                _segment_copies(n_, local, near_, remote, far_, sem, hi, wait=False, min_rows=lo)
            else:
                _segment_copies(n_, remote, far_, local, near_, sem, hi, wait=False, min_rows=lo)
            off = off + n

    copies(long_part=False)

    @pl.when(tinfo_ref[t, 2, 0] != 0)
    def _():
        copies(long_part=True)


def _dispatch_kernel(tinfo_ref, cnt_ref, tok_ref, pos_ref, xs_ref, first_ref, nblk_ref,
                     sbufs, zbuf, start_ref, sems, zsem):
    tj = pos_ref.shape[2]
    n_blocks = xs_ref.shape[0] // MOE_ROWS

    @pl.when(pl.program_id(0) == 0)
    def _():
        total = _pad_starts(cnt_ref, start_ref)
        used = total // MOE_ROWS

        def block_ranges(e, c):
            first_ref[e] = start_ref[e] // MOE_ROWS
            nblk_ref[e] = _padded(cnt_ref[0, e]) // MOE_ROWS
            return c
        lax.fori_loop(0, N_EXPERTS, block_ranges, 0)

        zbuf[...] = jnp.zeros_like(zbuf)
        zrows = zbuf.shape[0]
        for wait in (False, True):
            def zero_pad(e, c, wait=wait):
                cnt = cnt_ref[0, e]
                _segment_copies(_padded(cnt) - cnt, zbuf, 0, xs_ref, start_ref[e] + cnt, zsem,
                                zrows, wait, src_advances=False)
                return c
            lax.fori_loop(0, N_EXPERTS, zero_pad, 0)

            def zero_tail(i, c, wait=wait):
                cp = pltpu.make_async_copy(zbuf, xs_ref.at[pl.ds(total + i * zrows, zrows)], zsem)
                if wait:
                    cp.wait()
                else:
                    cp.start()
                return c
            lax.fori_loop(0, (n_blocks - used) * (MOE_ROWS // zrows), zero_tail, 0)

    def wait_tile(k):
        pltpu.make_async_copy(sbufs[k], xs_ref.at[pl.ds(0, 2 * tj)], sems.at[k]).wait()

    slot = lax.broadcasted_iota(jnp.int32, (2 * tj, tj), 0)
    for k in range(TILES_PER_STEP):
        @pl.when(pl.program_id(0) > 0)
        def _(k=k):
            wait_tile(k)
        perm = ((slot == pos_ref[k, 2:3, :]) | (slot == pos_ref[k, 3:4, :])).astype(bf16)
        sorted_rows = jnp.dot(perm, tok_ref[k * tj:(k + 1) * tj, :], preferred_element_type=f32)
        _store_row_tiles(sbufs[k], _pack_bf16_pairs(sorted_rows, is_bf16_exact=True))
        if k > 0:
            _tile_segments(tinfo_ref, k - 1, start_ref, sbufs[k - 1], True, xs_ref, sems.at[k - 1], tj)
    last = TILES_PER_STEP - 1
    _tile_segments(tinfo_ref, last, start_ref, sbufs[last], True, xs_ref, sems.at[last], tj)

    @pl.when(pl.program_id(0) == pl.num_programs(0) - 1)
    def _():
        for k in range(TILES_PER_STEP):
            wait_tile(k)


def _dispatch(tinfo, cnt, tok, pos, n_rows):
    ntiles, _, tj = pos.shape
    T, D = tok.shape
    n_blocks = n_rows // MOE_ROWS
    nt = TILES_PER_STEP
    assert ntiles % nt == 0
    sbuf = pltpu.VMEM(_row_tiles_shape(2 * tj, D), jnp.uint32)
    return pl.pallas_call(
        _dispatch_kernel,
        grid=(ntiles // nt,),
        in_specs=[pl.BlockSpec((nt, SUBLANES, LANES), lambda i: (i, 0, 0), memory_space=pltpu.SMEM),
                  pl.BlockSpec(memory_space=pltpu.SMEM),
                  pl.BlockSpec((nt * tj, D), lambda i: (i, 0)),
                  pl.BlockSpec((nt, SUBLANES, tj), lambda i: (i, 0, 0))],
        out_specs=[pl.BlockSpec(memory_space=pl.ANY),
                   pl.BlockSpec(memory_space=pltpu.SMEM),
                   pl.BlockSpec(memory_space=pltpu.SMEM)],
        out_shape=[jax.ShapeDtypeStruct(_row_tiles_shape(n_rows, D), jnp.uint32),
                   jax.ShapeDtypeStruct((N_EXPERTS,), jnp.int32),
                   jax.ShapeDtypeStruct((N_EXPERTS,), jnp.int32)],
        scratch_shapes=[[sbuf] * nt,
                        pltpu.VMEM(_row_tiles_shape(MOE_ROWS // 2, D), jnp.uint32),
                        pltpu.SMEM((N_EXPERTS,), jnp.int32), pltpu.SemaphoreType.DMA((nt,)),
                        pltpu.SemaphoreType.DMA],
        compiler_params=_cparams(("arbitrary",)),
        name="moe_dispatch",
    )(tinfo, cnt, tok, pos)


def _experts_kernel(first_ref, nblk_ref, xs_ref, wg_ref, wu_ref, wd_ref, ys_ref,
                    wg_s, wu_s, wd_s, xbuf, ybuf, in_sems, out_sems):
    e = pl.program_id(0)
    last_e = pl.num_programs(0) - 1
    rows = MOE_ROWS
    n_total = xs_ref.shape[0] // rows
    used = first_ref[last_e] + nblk_ref[last_e]

    def fetch(g, slot):
        return pltpu.make_async_copy(xs_ref.at[pl.ds(g * rows, rows)], xbuf.at[pl.ds(slot * rows, rows)],
                                     in_sems.at[slot])

    def flush(g, slot):
        return pltpu.make_async_copy(ybuf.at[pl.ds(slot * rows, rows)], ys_ref.at[pl.ds(g * rows, rows)],
                                     out_sems.at[slot])

    for ahead in range(FETCH_AHEAD):
        @pl.when((e == 0) & (used > ahead))
        def _(ahead=ahead):
            fetch(ahead, ahead).start(priority=ROW_DMA_PRIORITY)

    @pl.when(nblk_ref[e] > 0)
    def _():
        wg_s[...] = wg_ref[...].astype(bf16)
        wu_s[...] = wu_ref[...].astype(bf16)
        wd_s[...] = wd_ref[...].astype(bf16)

    def block(b, carry):
        g = first_ref[e] + b
        slot = g % 2
        in_slot = g % (FETCH_AHEAD + 1)
        fetch(g, in_slot).wait()

        @pl.when(g + FETCH_AHEAD < used)
        def _():
            fetch(g + FETCH_AHEAD, (g + FETCH_AHEAD) % (FETCH_AHEAD + 1)).start(priority=ROW_DMA_PRIORITY)

        @pl.when(g >= 2)
        def _():
            flush(g - 2, slot).wait()
        half = rows // EXPERT_ROW_GROUPS
        starts = [slot * rows + r * half for r in range(EXPERT_ROW_GROUPS)]
        gated = []
        for r in range(EXPERT_ROW_GROUPS):
            xb = _unpack_bf16_pairs(_load_row_tiles(xbuf, in_slot * rows + r * half, half)).astype(bf16)
            gated.append((jnp.dot(xb, wg_s[...], preferred_element_type=f32),
                          jnp.dot(xb, wu_s[...], preferred_element_type=f32)))
        for row0, (a, u) in zip(starts, gated):
            h = (a * jax.nn.sigmoid(a) * u).astype(bf16)
            _store_row_tiles(ybuf, _pack_bf16_pairs(jnp.dot(h, wd_s[...], preferred_element_type=f32)), row0)
        flush(g, slot).start(priority=ROW_DMA_PRIORITY)
        return carry
    lax.fori_loop(0, nblk_ref[e], block, 0)

    @pl.when(e == last_e)
    def _():
        for back in (1, 2):
            @pl.when(used >= back)
            def _(back=back):
                flush(used - back, (used - back) % 2).wait()
        ybuf[pl.ds(0, rows)] = jnp.zeros((rows,) + ybuf.shape[1:], ybuf.dtype)
        for wait in (False, True):
            def zero_tail(g, c, wait=wait):
                cp = flush(g, 0)
                if wait:
                    cp.wait()
                else:
                    cp.start()
                return c
            lax.fori_loop(used, n_total, zero_tail, 0)


def _experts(first_block, n_blocks, xs, w_gate, w_up, w_down):
    _, D, De = w_gate.shape
    widx = lambda e, first, nblk: (e, 0, 0)
    in_buf = pltpu.VMEM(_row_tiles_shape((FETCH_AHEAD + 1) * MOE_ROWS, D), jnp.uint32)
    out_buf = pltpu.VMEM(_row_tiles_shape(2 * MOE_ROWS, D), jnp.uint32)
    return pl.pallas_call(
        _experts_kernel,
        grid_spec=pltpu.PrefetchScalarGridSpec(
            num_scalar_prefetch=2,
            grid=(N_EXPERTS,),
            in_specs=[pl.BlockSpec(memory_space=pl.ANY),
                      pl.BlockSpec((None, D, De), widx),
                      pl.BlockSpec((None, D, De), widx),
                      pl.BlockSpec((None, De, D), widx)],
            out_specs=pl.BlockSpec(memory_space=pl.ANY),
            scratch_shapes=[pltpu.VMEM((D, De), bf16), pltpu.VMEM((D, De), bf16), pltpu.VMEM((De, D), bf16),
                            in_buf, out_buf, pltpu.SemaphoreType.DMA((FETCH_AHEAD + 1,)),
                            pltpu.SemaphoreType.DMA((2,))],
        ),
        out_shape=jax.ShapeDtypeStruct(xs.shape, jnp.uint32),
        compiler_params=_cparams(("arbitrary",)),
        name="moe_experts",
    )(first_block, n_blocks, xs, w_gate, w_up, w_down)


def _combine_kernel(tinfo_ref, tnext_ref, cnt_ref, h2_ref, vec_ref, g_ref, ys_ref, o_ref,
                    gbufs, stages, start_ref, sems):
    tj = vec_ref.shape[0] // TILES_PER_STEP
    step = pl.program_id(0)

    def fetch(info_ref, t, buf):
        _tile_segments(info_ref, t, start_ref, gbufs[buf], False, ys_ref, sems.at[buf], tj)

    @pl.when(step == 0)
    def _():
        _pad_starts(cnt_ref, start_ref)
        for t in range(GATHER_AHEAD):
            fetch(tinfo_ref, t, t)

    slot = lax.broadcasted_iota(jnp.int32, (tj, 2 * tj), 1)

    def finish(k, picked):
        v = vec_ref[k * tj:(k + 1) * tj, :]
        y = h2_ref[k * tj:(k + 1) * tj, :] + v[:, 0:1] * picked[:tj] + v[:, 1:2] * picked[tj:]
        out = _rms(y, g_ref[...])
        stage = stages[k // 2]
        for q in range(stage.shape[0]):
            stage[q, pl.ds(k % 2, tj, stride=2), :] = out[:, q * LANES:(q + 1) * LANES]
        if k % 2 == 1:
            r0 = (k // 2) * 2 * tj
            for q in range(stage.shape[0]):
                o_ref[0, r0:r0 + 2 * tj, q * LANES:(q + 1) * LANES] = stage[q]

    picked = None
    for k in range(TILES_PER_STEP):
        pltpu.make_async_copy(ys_ref.at[pl.ds(0, 2 * tj)], gbufs[k], sems.at[k]).wait()
        ahead = k + GATHER_AHEAD
        if ahead < TILES_PER_STEP:
            fetch(tinfo_ref, ahead, ahead)
        else:
            fetch(tnext_ref, ahead - TILES_PER_STEP, ahead - TILES_PER_STEP)
        rows = _unpack_bf16_pairs(_load_row_tiles(gbufs[k])).astype(bf16)
        v = vec_ref[k * tj:(k + 1) * tj, :]
        pick = jnp.concatenate([(slot == v[:, 2 + c:3 + c].astype(jnp.int32)).astype(bf16) for c in range(2)],
                               axis=0)
        now = jnp.dot(pick, rows, preferred_element_type=f32)
        if k > 0:
            finish(k - 1, picked)
        picked = now
    finish(TILES_PER_STEP - 1, picked)


def _combine(tinfo, cnt, h2, vec, final_g, ys, B, L):
    T, D = h2.shape
    ntiles = tinfo.shape[0]
    tj = T // ntiles
    nt = TILES_PER_STEP
    assert GATHER_AHEAD < nt and nt % 2 == 0 and (L // tj) % nt == 0
    nsteps = ntiles // nt
    per_seq = L // (nt * tj)
    g = final_g.reshape(1, D)
    gbuf = pltpu.VMEM(_row_tiles_shape(2 * tj, D), jnp.uint32)
    return pl.pallas_call(
        _combine_kernel,
        grid=(nsteps,),
        in_specs=[pl.BlockSpec((nt, SUBLANES, LANES), lambda i: (i, 0, 0), memory_space=pltpu.SMEM),
                  pl.BlockSpec((nt, SUBLANES, LANES), lambda i: (i + 1, 0, 0), memory_space=pltpu.SMEM),
                  pl.BlockSpec(memory_space=pltpu.SMEM),
                  pl.BlockSpec((nt * tj, D), lambda i: (i, 0)),
                  pl.BlockSpec((nt * tj, LANES), lambda i: (i, 0)),
                  pl.BlockSpec((1, D), lambda i: (0, 0)),
                  pl.BlockSpec(memory_space=pl.ANY)],
        out_specs=pl.BlockSpec((1, nt * tj, D), lambda i: (i // per_seq, i % per_seq, 0)),
        out_shape=jax.ShapeDtypeStruct((B, L, D), f32),
        scratch_shapes=[[gbuf] * nt,
                        [pltpu.VMEM((D // LANES, 2 * tj, LANES), f32)] * (nt // 2),
                        pltpu.SMEM((N_EXPERTS,), jnp.int32), pltpu.SemaphoreType.DMA((nt,))],
        compiler_params=_cparams(("arbitrary",)),
        name="moe_combine",
    )(tinfo, jnp.pad(tinfo, ((0, nt), (0, 0), (0, 0))), cnt, h2, vec, g, ys)


def kernel(x, norm1_g, w_in, conv_w, conv_b, f_w_in, f_b_in, f_w_mid, f_b_mid, f_freq, f_w_out, f_bias,
           mix_g, w_out, norm2_g, w_group, b_group, w_router, b_router, w_gate, w_up, w_down, final_g):
    B, L, D = x.shape
    depth = norm1_g.shape[0]
    assert depth == 1, "the final RMSNorm is fused into the single layer's MoE combine"
    C = D // 2
    Lh = L // 2
    N = B * C
    T = B * L
    n_rows = T * 2 + N_EXPERTS * MOE_ROWS
    cfou = _fourier_consts(L)
    i = 0
    wcat = _prep_win(w_in[i], C)
    hcat, asum = _hyena_filters(L, C, f_w_in[i], f_b_in[i], f_w_mid[i], f_b_mid[i], f_freq[i], f_w_out[i])
    z, x0, u4, cfwd, cinv = _inproj(x, norm1_g[i], wcat, conv_w[i], conv_b[i], C)
    kspec, k0 = _filter_spectrum(cfwd, hcat, asum, L, C)
    z = z.reshape(2, Lh, N)
    x0 = x0.reshape(2, Lh, N)
    u4 = u4.reshape(4, Lh, N)
    s = _conv_fwd(cfwd, z, kspec, k0, C)
    yh = _conv_inv(cinv, s, x0, z, f_bias[i], C)
    yf = _fourier_dft(cfou, u4, L, C)
    wr = jnp.concatenate([jnp.transpose(w_router[i], (1, 0, 2)).reshape(D, N_EXPERTS), w_group[i]], axis=1)
    wr = jnp.pad(wr, ((0, 0), (0, LANES - wr.shape[1])))
    br = jnp.pad(jnp.concatenate([b_router[i].reshape(-1), b_group[i]]), (0, LANES - N_EXPERTS - N_GROUPS))
    h2, tok, pos, vec, tinfo, cnt = _outproj_router(yh, yf, x, mix_g[i], w_out[i].astype(bf16), norm2_g[i],
                                                     wr, br.reshape(1, LANES), C)
    xs, first_block, n_blocks = _dispatch(tinfo, cnt, tok, pos, n_rows)
    ys = _experts(first_block, n_blocks, xs, w_gate[i], w_up[i], w_down[i])
    return _combine(tinfo, cnt, h2, vec, final_g, ys, B, L)
```

```python
import functools
import math

import jax
import jax.numpy as jnp
from jax import lax
from jax.experimental import pallas as pl
from jax.experimental.pallas import tpu as pltpu

HEAD_DIM = 64
N_GROUPS = 4
EXPERTS_PER_GROUP = 8
N_EXPERTS = N_GROUPS * EXPERTS_PER_GROUP
FILTER_BANDS = 16
DECAY_FAST_PCT = 0.3
DECAY_SLOW_PCT = 1.5
DECAY_TARGET = 1e-2
EPS = 1e-6

LANES = 128
SUBLANES = 8
MXU_DIM = 256
CONV_ROWS = 512
CONV_ROW_GROUPS = 2
CONV_COLS = 512
FOURIER_COLS = 256
MOE_ROWS = 512
EXPERT_ROW_GROUPS = 2
FETCH_AHEAD = 3
ROW_DMA_PRIORITY = 1
ROW_TILES_PER_STEP = 2
TILES_PER_STEP = 4
GATHER_AHEAD = 3
SEG_SPLIT = 32
VMEM_LIMIT = 56 * 1024 * 1024

f32 = jnp.float32
bf16 = jnp.bfloat16
HI = lax.Precision.HIGHEST


def _cparams(sem):
    return pltpu.CompilerParams(dimension_semantics=sem, vmem_limit_bytes=VMEM_LIMIT)


def _tile(pref, n):
    return min(pref, n)


_SIN_TERMS = tuple((-1.0) ** k / math.factorial(2 * k + 1) for k in range(7))
_COS_TERMS = tuple((-1.0) ** k / math.factorial(2 * k) for k in range(8))


def _cos_sin_of_turn_fraction(prod, m):
    num = prod & (m - 1)
    quadrant = num // (m // 4)
    x = (num & (m // 4 - 1)).astype(f32) * (2.0 * math.pi / m)
    x2 = x * x
    s = jnp.full_like(x, _SIN_TERMS[-1])
    for c in _SIN_TERMS[-2::-1]:
        s = s * x2 + c
    s = s * x
    c_ = jnp.full_like(x, _COS_TERMS[-1])
    for c in _COS_TERMS[-2::-1]:
        c_ = c_ * x2 + c
    odd = (quadrant & 1) == 1
    cos = jnp.where(odd, s, c_)
    sin = jnp.where(odd, c_, s)
    cos = jnp.where((quadrant == 1) | (quadrant == 2), -cos, cos)
    sin = jnp.where(quadrant >= 2, -sin, sin)
    return cos, sin


def _cos_sin_blocks(a, s, b0, m, ncb, d):
    tc, ts = _cos_sin_of_turn_fraction(a * (s * d), m)
    b0_lane = jnp.zeros(d.shape, jnp.int32)
    for cb in range(ncb):
        b0_lane = jnp.where(d == cb, b0(cb), b0_lane)
    ac, asn = _cos_sin_of_turn_fraction(a * b0_lane, m)
    for cb in range(ncb):
        ca = ac[:, cb:cb + 1]
        sa = asn[:, cb:cb + 1]
        yield cb, ca * tc - sa * ts, sa * tc + ca * ts


def _write_dft_rows(fwd_ref, inv_ref, row_block, L):
    Lh = L // 2
    tr = fwd_ref.shape[1]
    r = lax.broadcasted_iota(jnp.int32, (tr, LANES), 0) + row_block * tr
    d = lax.broadcasted_iota(jnp.int32, (tr, LANES), 1)
    alt_r = (1 - 2 * (r & 1)).astype(f32)
    alt_d = (1 - 2 * (d & 1)).astype(f32)
    ncb = Lh // LANES
    fams = [
        (r, 1, lambda cb: cb * LANES, L),
        (r, 2, lambda cb: 2 * cb * LANES + 1, 2 * L),
        (2 * r + 1, 1, lambda cb: cb * LANES, 2 * L),
    ]
    for fi, (a, s, b0, m) in enumerate(fams):
        for cb, c, sn in _cos_sin_blocks(a, s, b0, m, ncb, d):
            sl = slice(cb * LANES, (cb + 1) * LANES)
            if fi == 0:
                fwd_ref[0, :, sl] = c.astype(bf16)
                inv_ref[0, :, sl] = c.astype(bf16)
                fwd_ref[1, :, sl] = jnp.where(r == 0, alt_d, sn).astype(bf16)
                nst = -sn
                if cb == 0:
                    nst = jnp.where(d == 0, -alt_r, nst)
                inv_ref[1, :, sl] = nst.astype(bf16)
            elif fi == 1:
                fwd_ref[2, :, sl] = c.astype(bf16)
                fwd_ref[3, :, sl] = jnp.where(r == 0, alt_d, sn).astype(bf16)
            else:
                inv_ref[2, :, sl] = c.astype(bf16)
                nst = -sn
                if cb == 0:
                    nst = jnp.where(d == 0, -alt_r, nst)
                inv_ref[3, :, sl] = nst.astype(bf16)


def _fourier_consts_kernel(o_ref, *, L):
    n = L // 4
    tr = o_ref.shape[0]
    r = lax.broadcasted_iota(jnp.int32, (tr, LANES), 0) + pl.program_id(0) * tr
    d = lax.broadcasted_iota(jnp.int32, (tr, LANES), 1)

    def col_l(cb):
        parity, j0 = divmod(cb * LANES, n // 2)
        return 2 * j0 + parity

    for cb, c, sn in _cos_sin_blocks(r, 2, col_l, n, n // LANES, d):
        o_ref[:, cb * LANES:(cb + 1) * LANES] = c.astype(bf16)
        o_ref[:, n + cb * LANES:n + (cb + 1) * LANES] = (-sn).astype(bf16)


def _fourier_consts(L):
    n = L // 4
    tr = _tile(256, n)
    return pl.pallas_call(
        functools.partial(_fourier_consts_kernel, L=L),
        grid=(n // tr,),
        out_specs=pl.BlockSpec((tr, 2 * n), lambda i: (i, 0)),
        out_shape=jax.ShapeDtypeStruct((n, 2 * n), bf16),
        compiler_params=_cparams(("parallel",)),
        name="fourier_consts",
    )()


def _prep_win_kernel(w_ref, o_ref, *, C):
    w = w_ref[...]
    i = lax.broadcasted_iota(jnp.int32, (LANES, LANES), 0)
    j = lax.broadcasted_iota(jnp.int32, (LANES, LANES), 1)
    same = (i // HEAD_DIM) == (j // HEAD_DIM)
    cos, sin = _cos_sin_of_turn_fraction((i % HEAD_DIM) * (j % HEAD_DIM), HEAD_DIM)
    bc = jnp.where(same, cos, 0.0)
    bs = jnp.where(same, sin, 0.0)
    o_ref[:, :3 * C] = w[:, :3 * C].astype(bf16)
    for k in range(0, C, LANES):
        wf = w[:, 3 * C + k:3 * C + k + LANES]
        o_ref[:, 3 * C + k:3 * C + k + LANES] = jnp.dot(wf, bc, precision=HI, preferred_element_type=f32).astype(bf16)
        o_ref[:, 4 * C + k:4 * C + k + LANES] = jnp.dot(wf, bs, precision=HI, preferred_element_type=f32).astype(bf16)


def _prep_win(w_in, C):
    D = w_in.shape[0]
    tr = _tile(256, D)
    return pl.pallas_call(
        functools.partial(_prep_win_kernel, C=C),
        grid=(D // tr,),
        in_specs=[pl.BlockSpec((tr, 4 * C), lambda i: (i, 0))],
        out_specs=pl.BlockSpec((tr, 5 * C), lambda i: (i, 0)),
        out_shape=jax.ShapeDtypeStruct((D, 5 * C), bf16),
        compiler_params=_cparams(("parallel",)),
        name="prep_win",
    )(w_in)


def _dot_split(a, b):
    a_hi = a.astype(bf16)
    b_hi = b.astype(bf16)
    a_lo = (a - a_hi.astype(f32)).astype(bf16)
    b_lo = (b - b_hi.astype(f32)).astype(bf16)
    return (jnp.dot(a_hi, b_hi, preferred_element_type=f32) + jnp.dot(a_lo, b_hi, preferred_element_type=f32)
            + jnp.dot(a_hi, b_lo, preferred_element_type=f32))


def _write_filter_rows(win_ref, bin_ref, wmid_ref, bmid_ref, freq_ref, wout_ref, h_ref, asum_ref, step, L, C):
    tr = h_ref.shape[1]
    lane = lax.broadcasted_iota(jnp.int32, (tr, LANES), 1)
    row = lax.broadcasted_iota(jnp.int32, (tr, LANES), 0) + step * tr
    band_i = jnp.where(lane <= FILTER_BANDS, lane - 1, lane - 1 - FILTER_BANDS)
    band = 1e-4 + band_i.astype(f32) * ((FILTER_BANDS - 1 - 1e-4) / (FILTER_BANDS - 1))
    ch = lax.broadcasted_iota(jnp.int32, (tr, 2 * C), 1) % C
    max_decay = math.log(DECAY_TARGET) / DECAY_FAST_PCT
    min_decay = math.log(DECAY_TARGET) / DECAY_SLOW_PCT
    delta = jnp.abs(min_decay + ch.astype(f32) * ((max_decay - min_decay) / (C - 1)))
    is_bwd = lax.broadcasted_iota(jnp.int32, (tr, 2 * C), 1) >= C
    row_c = lax.broadcasted_iota(jnp.int32, (tr, 2 * C), 0) + step * tr

    @pl.when(step == 0)
    def _():
        asum_ref[...] = jnp.zeros_like(asum_ref)

    total = jnp.zeros((1, 2 * C), f32)
    for p in range(2):
        pos = (2 * row + p).astype(f32)
        t = pos * (1.0 / (L - 1))
        w = pos * (2.0 * math.pi / L)
        fw = band * w
        z = jnp.where(lane == 0, t,
                      jnp.where(lane <= FILTER_BANDS, jnp.cos(fw),
                                jnp.where(lane <= 2 * FILTER_BANDS, -jnp.sin(fw), 0.0)))
        h = jnp.sin(freq_ref[0:1, :] * (_dot_split(z, win_ref[...]) + bin_ref[...]))
        for i in range(wmid_ref.shape[0]):
            h = jnp.sin(freq_ref[i + 1:i + 2, :] * (_dot_split(h, wmid_ref[i]) + bmid_ref[i:i + 1, :]))
        out = _dot_split(h, wout_ref[...])
        tc = (2 * row_c + p).astype(f32) * (1.0 / (L - 1))
        out = out * jnp.exp(-tc * delta)
        if p == 0:
            out = jnp.where(is_bwd & (row_c == 0), 0.0, out)
        h_ref[p] = out.astype(bf16)
        total = total + jnp.sum(jnp.abs(out), axis=0, keepdims=True)
    asum_ref[0:1, :] += total


def _filter_kernel(win_ref, bin_ref, wmid_ref, bmid_ref, freq_ref, wout_ref, h_ref, asum_ref, *, L, C):
    _write_filter_rows(win_ref, bin_ref, wmid_ref, bmid_ref, freq_ref, wout_ref, h_ref, asum_ref,
                       pl.program_id(0), L, C)


def _hyena_filters(L, C, f_w_in, f_b_in, f_w_mid, f_b_mid, f_freq, f_w_out):
    Lh = L // 2
    tr = _tile(256, Lh)
    order = f_w_in.shape[1]
    win = jnp.pad(f_w_in, ((0, LANES - f_w_in.shape[0]), (0, 0)))
    full = lambda a: pl.BlockSpec(a.shape, lambda i: (0,) * a.ndim)
    args = (win, f_b_in.reshape(1, order), f_w_mid, f_b_mid, f_freq, f_w_out)
    return pl.pallas_call(
        functools.partial(_filter_kernel, L=L, C=C),
        grid=(Lh // tr,),
        in_specs=[full(a) for a in args],
        out_specs=[pl.BlockSpec((2, tr, 2 * C), lambda i: (0, i, 0)),
                   pl.BlockSpec((SUBLANES, 2 * C), lambda i: (0, 0))],
        out_shape=[jax.ShapeDtypeStruct((2, Lh, 2 * C), bf16),
                   jax.ShapeDtypeStruct((SUBLANES, 2 * C), f32)],
        compiler_params=_cparams(("arbitrary",)),
        name="hyena_filters",
    )(*args)


def _fwd_products(c_ref, x_ref, rows=slice(None)):
    k = c_ref.shape[2]
    even = c_ref[0:2, rows, :]
    odd = c_ref[2:4, rows, :]
    tm = even.shape[1]
    even = jnp.dot(even.reshape(2 * tm, k), x_ref[0], preferred_element_type=f32)
    odd = jnp.dot(odd.reshape(2 * tm, k), x_ref[1], preferred_element_type=f32)
    return even[:tm], even[tm:], odd[:tm], odd[tm:]


def _filter_spectrum_kernel(c_ref, h_ref, asum_ref, k_ref, k0_ref, *, L, C):
    ce, se, co, so = _fwd_products(c_ref, h_ref)
    a = asum_ref[0:1, :]
    scale = 1.0 / ((a[:, :C] + a[:, C:]) * L)
    F = lambda v: v[:, :C]
    G = lambda v: v[:, C:]
    k_ref[0] = (F(ce) + F(co) + G(ce) + G(co)) * scale
    k_ref[1] = (-(F(se) + F(so)) + (G(se) + G(so))) * scale
    k_ref[2] = (F(ce) - F(co) + G(ce) - G(co)) * scale
    k_ref[3] = ((F(se) - F(so)) - (G(se) - G(so))) * scale

    @pl.when(pl.program_id(0) == 0)
    def _():
        k0_ref[...] = jnp.zeros_like(k0_ref)
        k0_ref[0:1, :] = ((F(se) + G(se)) * scale)[0:1, :]
        k0_ref[1:2, :] = ((F(so) - G(so)) * scale)[0:1, :]


def _filter_spectrum(cfwd, hcat, asum, L, C):
    Lh = L // 2
    tm = _tile(256, Lh)
    return pl.pallas_call(
        functools.partial(_filter_spectrum_kernel, L=L, C=C),
        grid=(Lh // tm,),
        in_specs=[pl.BlockSpec((4, tm, Lh), lambda i: (0, i, 0)),
                  pl.BlockSpec((2, Lh, 2 * C), lambda i: (0, 0, 0)),
                  pl.BlockSpec((SUBLANES, 2 * C), lambda i: (0, 0))],
        out_specs=[pl.BlockSpec((4, tm, C), lambda i: (0, i, 0)),
                   pl.BlockSpec((SUBLANES, C), lambda i: (0, 0))],
        out_shape=[jax.ShapeDtypeStruct((4, Lh, C), f32),
                   jax.ShapeDtypeStruct((SUBLANES, C), f32)],
        compiler_params=_cparams(("arbitrary",)),
        name="filter_spectrum",
    )(cfwd, hcat, asum)


def _conv_fwd_kernel(c_ref, z_ref, k_ref, k0_ref, s_ref):
    C = k_ref.shape[2]
    tm = c_ref.shape[1]
    tg = tm // CONV_ROW_GROUPS
    groups = [slice(r * tg, (r + 1) * tg) for r in range(CONV_ROW_GROUPS)]
    products = [_fwd_products(c_ref, z_ref, rows) for rows in groups]
    is_row0 = (lax.broadcasted_iota(jnp.int32, (SUBLANES, C), 0) == 0) & (pl.program_id(1) == 0)
    for rows, prod in zip(groups, products):
        for h in range(z_ref.shape[2] // C):
            cols = slice(h * C, (h + 1) * C)
            ce, se, co, so = (v[:, cols] for v in prod)
            zsr, zsi, zdr, zdi = ce + co, -(se + so), ce - co, se - so
            ksr, ksi, kdr, kdi = (k_ref[i, rows, :] for i in range(4))
            psr = zsr * ksr - zsi * ksi
            psi = zsr * ksi + zsi * ksr
            pdr = zdr * kdr - zdi * kdi
            pdi = zdr * kdi + zdi * kdr
            gen = [psr + pdr, psi - pdi, psr - pdr, psi + pdi]
            if rows.start == 0:
                top = slice(0, SUBLANES)
                p0 = zsr[top] * ksr[top]
                pl_ = zdr[top] * kdr[top]
                ak, bk = k0_ref[0:1, :], k0_ref[1:2, :]
                phr = se[top] * ak - so[top] * bk
                phi = -(se[top] * bk + so[top] * ak)
                spec = (0.5 * (p0 + pl_), -phr, 0.5 * (p0 - pl_), phi)
                gen = [jnp.concatenate([jnp.where(is_row0, spec[i], gen[i][top]), gen[i][SUBLANES:]], axis=0)
                       for i in range(4)]
            for i in range(4):
                s_ref[i, rows, cols] = gen[i].astype(bf16)


def _conv_fwd(cfwd, z, kspec, k0, C):
    _, Lh, N = z.shape
    tm = _tile(CONV_ROWS, Lh)
    tn = _tile(CONV_COLS, N)
    return pl.pallas_call(
        _conv_fwd_kernel,
        grid=(N // tn, Lh // tm),
        in_specs=[pl.BlockSpec((4, tm, Lh), lambda n, i: (0, i, 0)),
                  pl.BlockSpec((2, Lh, tn), lambda n, i: (0, 0, n)),
                  pl.BlockSpec((4, tm, C), lambda n, i: (0, i, 0)),
                  pl.BlockSpec((SUBLANES, C), lambda n, i: (0, 0))],
        out_specs=pl.BlockSpec((4, tm, tn), lambda n, i: (0, i, n)),
        out_shape=jax.ShapeDtypeStruct((4, Lh, N), bf16),
        compiler_params=_cparams(("parallel", "arbitrary")),
        name="conv_fwd",
    )(cfwd, z, kspec, k0)


def _pair_products(c_ref, r_ref):
    ev = (jnp.dot(c_ref[0], r_ref[0], preferred_element_type=f32)
          + jnp.dot(c_ref[1], r_ref[1], preferred_element_type=f32))
    od = (jnp.dot(c_ref[2], r_ref[2], preferred_element_type=f32)
          + jnp.dot(c_ref[3], r_ref[3], preferred_element_type=f32))
    return ev, od


def _conv_inv_kernel(c_ref, s_ref, x0_ref, z_ref, fb_ref, y_ref):
    conv = _pair_products(c_ref, s_ref)
    C = fb_ref.shape[1]
    fb = fb_ref[...]
    for p in range(2):
        for h in range(z_ref.shape[2] // C):
            cols = slice(h * C, (h + 1) * C)
            zp = z_ref[p, :, cols].astype(f32)
            y_ref[p, :, cols] = (x0_ref[p, :, cols].astype(f32) * (conv[p][:, cols] + zp * fb)).astype(bf16)


def _conv_inv(cinv, s, x0, z, f_bias, C):
    _, Lh, N = s.shape
    tm = _tile(CONV_ROWS, Lh)
    tn = _tile(CONV_COLS, N)
    return pl.pallas_call(
        _conv_inv_kernel,
        grid=(N // tn, Lh // tm),
        in_specs=[pl.BlockSpec((4, tm, Lh), lambda n, i: (0, i, 0)),
                  pl.BlockSpec((4, Lh, tn), lambda n, i: (0, 0, n)),
                  pl.BlockSpec((2, tm, tn), lambda n, i: (0, i, n)),
                  pl.BlockSpec((2, tm, tn), lambda n, i: (0, i, n)),
                  pl.BlockSpec((1, C), lambda n, i: (0, 0))],
        out_specs=pl.BlockSpec((2, tm, tn), lambda n, i: (0, i, n)),
        out_shape=jax.ShapeDtypeStruct((2, Lh, N), bf16),
        compiler_params=_cparams(("parallel", "arbitrary")),
        name="conv_inv",
    )(cinv, s, x0, z, f_bias.reshape(1, C))


def _fourier_kernel(c_ref, u_ref, y_ref, rhs, stage, tw1, tw2, *, L, scale):
    Lh, n, H = L // 2, L // 4, L // 8
    tn = u_ref.shape[2]

    @pl.when(pl.program_id(0) == 0)
    def _():
        r1 = lax.broadcasted_iota(jnp.int32, (Lh, LANES), 0)
        tw1[0], tw1[1] = _cos_sin_of_turn_fraction(2 * (r1 % n) + r1 // n, L)
        r2 = lax.broadcasted_iota(jnp.int32, (n, LANES), 0)
        tw2[0], tw2[1] = _cos_sin_of_turn_fraction(2 * (r2 % H) + r2 // H, Lh)

    def rotate(a, b, c, s):
        return c * a - s * b, s * a + c * b

    for q in range(tn // LANES):
        cols = slice(q * LANES, (q + 1) * LANES)
        for p in range(2):
            first = slice(p * n, p * n + H)
            second = slice(p * n + H, (p + 1) * n)
            out = slice(p * H, (p + 1) * H)
            out_b = slice(n + p * H, n + (p + 1) * H)
            c2, s2 = tw2[0, out, :], tw2[1, out, :]

            def emit(sub, a0, b0, a1, b1):
                rhs[sub, out, cols] = (a0 + a1).astype(bf16)
                rhs[sub, out_b, cols] = (b0 + b1).astype(bf16)
                da, db = rotate(a0 - a1, b0 - b1, c2, s2)
                rhs[sub + 1, out, cols] = da.astype(bf16)
                rhs[sub + 1, out_b, cols] = db.astype(bf16)

            ld = lambda k, rows: u_ref[k, rows, cols].astype(f32)
            emit(0, ld(0, first), ld(1, first), ld(0, second), ld(1, second))
            odd = [rotate(ld(2, rows), ld(3, rows), tw1[0, rows, :], tw1[1, rows, :]) for rows in (first, second)]
            emit(2, odd[0][0], odd[0][1], odd[1][0], odd[1][1])

    for parity in range(2):
        pair = [jnp.dot(c_ref[...], rhs[2 * parity + h], preferred_element_type=f32) * scale for h in range(2)]
        for q in range(tn // LANES):
            cols = slice(q * LANES, (q + 1) * LANES)
            for h in range(2):
                stage[q, pl.ds(h, n, stride=2), :] = pair[h][:, cols]
            y_ref[parity, :, cols] = stage[q].astype(bf16)


def _fourier_dft(cfou, u4, L, C):
    _, Lh, N = u4.shape
    n = L // 4
    tn = _tile(FOURIER_COLS, C)
    return pl.pallas_call(
        functools.partial(_fourier_kernel, L=L, scale=1.0 / math.sqrt(L * HEAD_DIM)),
        grid=(N // tn,),
        in_specs=[pl.BlockSpec((n, 2 * n), lambda j: (0, 0)),
                  pl.BlockSpec((4, Lh, tn), lambda j: (0, 0, j))],
        out_specs=pl.BlockSpec((2, Lh, tn), lambda j: (0, 0, j)),
        out_shape=jax.ShapeDtypeStruct((2, Lh, N), bf16),
        scratch_shapes=[pltpu.VMEM((4, 2 * n, tn), bf16), pltpu.VMEM((tn // LANES, Lh, LANES), f32),
                        pltpu.VMEM((2, Lh, LANES), f32), pltpu.VMEM((2, n, LANES), f32)],
        compiler_params=_cparams(("arbitrary",)),
        name="fourier_dft",
    )(cfou, u4)


def _rms(x, g):
    return x * lax.rsqrt(jnp.mean(x * x, axis=-1, keepdims=True) + EPS) * g


def _stage_rows(chunks, x):
    for k in range(chunks.shape[0]):
        chunks[k] = x[:, k * LANES:(k + 1) * LANES]


def _parity_rows(chunks, p, n):
    return jnp.concatenate([chunks[k, pl.ds(p, n, stride=2), :] for k in range(chunks.shape[0])], axis=1)


def _inproj_kernel(xa_ref, xb_ref, pa_ref, na_ref, pb_ref, nb_ref, g_ref, w_ref, cw_ref, cb_ref,
                   z_ref, x0_ref, u_ref, cfwd_ref, cinv_ref, stage, *, C, D, L):
    tj = xa_ref.shape[1] // 2
    jt = pl.program_id(1)
    n_half = pl.num_programs(1)
    _write_dft_rows(cfwd_ref, cinv_ref, pl.program_id(0) * n_half + jt, L)
    g = g_ref[...]
    w = w_ref[...]
    cw0, cw1, cw2 = cw_ref[0:1, :], cw_ref[1:2, :], cw_ref[2:3, :]
    cb = cb_ref[...]
    row = lax.broadcasted_iota(jnp.int32, (tj, 3 * C), 0)
    four = []
    halves = ((xa_ref, pa_ref, na_ref, jt == 0, False),
              (xb_ref, pb_ref, nb_ref, False, jt == n_half - 1))
    for h, (x_ref, p_ref, n_ref, at_start, at_end) in enumerate(halves):
        _stage_rows(stage, x_ref[0])
        lhs = jnp.concatenate([
            _rms(_parity_rows(stage, 0, tj), g).astype(bf16),
            _rms(_parity_rows(stage, 1, tj), g).astype(bf16),
            _rms(p_ref[0], g).astype(bf16),
            _rms(n_ref[0], g).astype(bf16)], axis=0)
        res = jnp.dot(lhs, w, preferred_element_type=f32)
        pe, po = res[:tj, :3 * C], res[tj:2 * tj, :3 * C]
        prev = res[2 * tj + SUBLANES - 1:2 * tj + SUBLANES, :3 * C]
        nxt = res[2 * tj + SUBLANES:2 * tj + SUBLANES + 1, :3 * C]
        if at_start is not False:
            prev = jnp.where(at_start, 0.0, prev)
        if at_end is not False:
            nxt = jnp.where(at_end, 0.0, nxt)
        po_dn = jnp.where(row == 0, prev, pltpu.roll(po, 1, 0))
        pe_up = jnp.where(row == tj - 1, nxt, pltpu.roll(pe, tj - 1, 0))
        uc = (cb + cw0 * po_dn + cw1 * pe + cw2 * po,
              cb + cw0 * pe + cw1 * po + cw2 * pe_up)
        for p in range(2):
            x0_ref[p, h] = uc[p][:, :C].astype(bf16)
            z_ref[p, h] = (uc[p][:, C:2 * C] * uc[p][:, 2 * C:]).astype(bf16)
        four.append((res[:tj, 3 * C:], res[tj:2 * tj, 3 * C:]))
    for p in range(2):
        fa, fb = four[0][p], four[1][p]
        u_ref[0, p] = (fa[:, :C] + fb[:, :C]).astype(bf16)
        u_ref[1, p] = (fa[:, C:] + fb[:, C:]).astype(bf16)
        u_ref[2, p] = (fa[:, :C] - fb[:, :C]).astype(bf16)
        u_ref[3, p] = (fa[:, C:] - fb[:, C:]).astype(bf16)


def _inproj(x, norm_g, wcat, conv_w, conv_b, C):
    B, L, D = x.shape
    Lq = L // 4
    tj = _tile(256, Lq)
    nq = Lq // tj
    N = B * C
    hb = 2 * tj // SUBLANES
    last = L // SUBLANES - 1
    main = lambda off: pl.BlockSpec((1, 2 * tj, D), lambda b, j: (b, j + off, 0))
    prev = lambda off: pl.BlockSpec((1, SUBLANES, D), lambda b, j: (b, jnp.maximum((j + off) * hb - 1, 0), 0))
    nxt = lambda off: pl.BlockSpec((1, SUBLANES, D), lambda b, j: (b, jnp.minimum((j + off + 1) * hb, last), 0))
    full = lambda a: pl.BlockSpec(a.shape, lambda b, j: (0,) * a.ndim)
    g = norm_g.reshape(1, D)
    cb = conv_b.reshape(1, 3 * C)
    Lh = L // 2
    dft_rows = Lh // (B * nq)
    assert dft_rows * B * nq == Lh and dft_rows % (2 * SUBLANES) == 0
    dft_spec = pl.BlockSpec((4, dft_rows, Lh), lambda b, j: (0, b * nq + j, 0))
    dft_shape = jax.ShapeDtypeStruct((4, Lh, Lh), bf16)
    return pl.pallas_call(
        functools.partial(_inproj_kernel, C=C, D=D, L=L),
        grid=(B, nq),
        in_specs=[main(0), main(nq), prev(0), nxt(0), prev(nq), nxt(nq),
                  full(g), full(wcat), full(conv_w), full(cb)],
        out_specs=[pl.BlockSpec((2, 2, tj, C), lambda b, j: (0, 0, j, b)),
                   pl.BlockSpec((2, 2, tj, C), lambda b, j: (0, 0, j, b)),
                   pl.BlockSpec((4, 2, tj, C), lambda b, j: (0, 0, j, b)),
                   dft_spec, dft_spec],
        out_shape=[jax.ShapeDtypeStruct((2, 2, Lq, N), bf16),
                   jax.ShapeDtypeStruct((2, 2, Lq, N), bf16),
                   jax.ShapeDtypeStruct((4, 2, Lq, N), bf16),
                   dft_shape, dft_shape],
        scratch_shapes=[pltpu.VMEM((D // LANES, 2 * tj, LANES), f32)],
        compiler_params=_cparams(("parallel", "arbitrary")),
        name="inproj",
    )(x, x, x, x, x, x, g, wcat, conv_w, cb)


def _pack_bf16_pairs(v, is_bf16_exact=False):
    k = v.shape[1] // 2
    rounded = v if is_bf16_exact else v.astype(bf16).astype(f32)
    hi = pltpu.bitcast(rounded[:, :k], jnp.uint32)
    lo = pltpu.bitcast(rounded[:, k:], jnp.uint32)
    return hi | (lo >> 16)


def _unpack_bf16_pairs(u):
    hi = pltpu.bitcast(u & jnp.uint32(0xFFFF0000), f32)
    lo = pltpu.bitcast(u << 16, f32)
    return jnp.concatenate([hi, lo], axis=1)


def _row_tiles_shape(n, D):
    return (n, D // (2 * LANES), LANES)


def _store_row_tiles(ref, packed, row0=0):
    total, per_row, _ = ref.shape
    n = packed.shape[0]
    flat = ref.reshape(total * per_row, LANES)
    for s in range(per_row):
        flat[pl.ds(row0 * per_row + s, n, stride=per_row), :] = packed[:, s * LANES:(s + 1) * LANES]


def _load_row_tiles(ref, row0=0, n=None):
    total, per_row, _ = ref.shape
    n = total if n is None else n
    flat = ref.reshape(total * per_row, LANES)
    return jnp.concatenate([flat[pl.ds(row0 * per_row + s, n, stride=per_row), :] for s in range(per_row)],
                           axis=1)


def _outproj_router_kernel(yh_ref, yf_ref, x_ref, mg_ref, wo_ref, g2_ref, wr_ref, br_ref,
                           h2_ref, tok_ref, pos_ref, vec_ref, tinfo_ref, cnt_ref, carry_ref, stage, *, C):
    n_sub = pos_ref.shape[0]
    tj = pos_ref.shape[2]
    _stage_rows(stage, x_ref[0])

    @pl.when((pl.program_id(0) == 0) & (pl.program_id(1) == 0))
    def _():
        carry_ref[...] = jnp.zeros_like(carry_ref)

    li = lax.broadcasted_iota(jnp.int32, (LANES, LANES), 0)
    lj = lax.broadcasted_iota(jnp.int32, (LANES, LANES), 1)
    hw = _tile(MXU_DIM, C)
    ai = lax.broadcasted_iota(jnp.int32, (hw, hw), 0)
    aj = lax.broadcasted_iota(jnp.int32, (hw, hw), 1)
    avg = jnp.where(ai // HEAD_DIM == aj // HEAD_DIM, 1.0 / HEAD_DIM, 0.0).astype(bf16)
    upper = (li < lj).astype(bf16)
    ri = lax.broadcasted_iota(jnp.int32, (tj, tj), 0)
    ci = lax.broadcasted_iota(jnp.int32, (tj, tj), 1)
    tri = (ci < ri).astype(bf16)
    wr = wr_ref[...]
    wr_hi = wr.astype(bf16)
    wr_lo = (wr - wr_hi.astype(f32)).astype(bf16)
    wr_both = jnp.concatenate([wr_hi, wr_lo], axis=1)
    lane = lax.broadcasted_iota(jnp.int32, (n_sub * tj, LANES), 1).astype(f32)
    sub = lax.broadcasted_iota(jnp.int32, (SUBLANES, LANES), 0)
    neg = jnp.float32(-jnp.inf)
    big = jnp.float32(LANES)

    def first_argmax(vals, mask):
        v = jnp.where(mask, vals, neg)
        m = jnp.max(v, axis=1, keepdims=True)
        idx = jnp.min(jnp.where(mask & (v == m), lane, big), axis=1, keepdims=True)
        return m, idx

    normed = []
    for s in range(n_sub):
        r, p = divmod(s, 2)
        parts = []
        for i, ref in enumerate((yh_ref, yf_ref)):
            y = ref[p, r * tj:(r + 1) * tj, :]
            ysq = y * y
            ms = jnp.concatenate([jnp.dot(ysq[:, k:k + hw], avg, preferred_element_type=f32)
                                  for k in range(0, C, hw)], axis=1)
            scale = lax.rsqrt(ms + EPS) * mg_ref[:, i * C:(i + 1) * C]
            parts.append(y * scale.astype(bf16))
        normed.append(jnp.concatenate(parts, axis=1))
    proj = jnp.dot(jnp.concatenate(normed, axis=0), wo_ref[...], preferred_element_type=f32)
    t_hi, t_lo = [], []
    for s in range(n_sub):
        r, p = divmod(s, 2)
        rows = slice(s * tj, (s + 1) * tj)
        h2 = _parity_rows(stage, r * 2 * tj + p, tj) + proj[rows]
        h2_ref[rows, :] = h2
        tok = _rms(h2, g2_ref[...])
        hi = tok.astype(bf16)
        tok_ref[rows, :] = hi
        t_hi.append(hi)
        t_lo.append((tok - hi.astype(f32)).astype(bf16))
    hi_terms = jnp.dot(jnp.concatenate(t_hi, axis=0), wr_both, preferred_element_type=f32)
    logits = (hi_terms[:, :LANES] + hi_terms[:, LANES:]
              + jnp.dot(jnp.concatenate(t_lo, axis=0), wr_hi, preferred_element_type=f32) + br_ref[...])

    gmask = (lane >= N_EXPERTS) & (lane < N_EXPERTS + N_GROUPS)
    gmax, gidx = first_argmax(logits, gmask)
    gsel = gidx - N_EXPERTS
    p_group = 1.0 / jnp.sum(jnp.where(gmask, jnp.exp(logits - gmax), 0.0), axis=1, keepdims=True)
    emask = (lane >= gsel * EXPERTS_PER_GROUP) & (lane < (gsel + 1) * EXPERTS_PER_GROUP)
    l0, e0 = first_argmax(logits, emask)
    l1, e1 = first_argmax(logits, emask & (lane != e0))
    ex = jnp.exp(l1 - l0)
    w0 = p_group / (1.0 + ex)
    w1 = p_group * ex / (1.0 + ex)

    oh0 = (lane == e0).astype(f32)
    oh1 = (lane == e1).astype(f32)
    both = (oh0 + oh1).astype(bf16)
    where, counts = [], []
    for s in range(n_sub):
        tile_both = both[s * tj:(s + 1) * tj, :]
        ranks = jnp.dot(tri, tile_both, preferred_element_type=f32)
        n_e = ranks[tj - 1:tj, :] + tile_both[tj - 1:tj, :].astype(f32)
        n_e8 = jnp.broadcast_to(n_e, (SUBLANES, LANES))
        tile_off = jnp.dot(n_e8.astype(bf16), upper, preferred_element_type=f32)[0:1, :]
        where.append(ranks + tile_off)
        counts.append(n_e8)
    where = jnp.concatenate(where, axis=0)
    pos0 = jnp.sum(oh0 * where, axis=1, keepdims=True)
    pos1 = jnp.sum(oh1 * where, axis=1, keepdims=True)
    cols = jnp.where(lane == 0, w0, jnp.where(lane == 1, w1,
                     jnp.where(lane == 2, pos0, jnp.where(lane == 3, pos1, 0.0))))
    vec_ref[...] = cols
    base = carry_ref[...]
    for s in range(n_sub):
        pos_ref[s] = cols[s * tj:(s + 1) * tj, :].T[:SUBLANES, :].astype(jnp.int32)
        long_seg = (jnp.max(counts[s], axis=1, keepdims=True) >= SEG_SPLIT).astype(f32)
        tinfo_ref[s] = jnp.where(sub == 0, counts[s], jnp.where(sub == 1, base, jnp.where(sub == 2, long_seg, 0.0))
                                 ).astype(jnp.int32)
        base = base + counts[s]
    carry_ref[...] = base
    cnt_ref[...] = base.astype(jnp.int32)


def _outproj_router(yh, yf, x, mix_g, wo, norm2_g, wr, br, C):
    B, L, D = x.shape
    Lh = L // 2
    tj = _tile(256, Lh)
    rt = ROW_TILES_PER_STEP
    ns = 2 * rt
    nj = Lh // (rt * tj)
    T = B * L
    ntiles = T // tj
    lin = lambda b, j: b * nj + j
    full = lambda a: pl.BlockSpec(a.shape, lambda b, j: (0,) * a.ndim)
    mg = mix_g.reshape(1, D)
    g2 = norm2_g.reshape(1, D)
    return pl.pallas_call(
        functools.partial(_outproj_router_kernel, C=C),
        grid=(B, nj),
        in_specs=[pl.BlockSpec((2, rt * tj, C), lambda b, j: (0, j, b)),
                  pl.BlockSpec((2, rt * tj, C), lambda b, j: (0, j, b)),
                  pl.BlockSpec((1, ns * tj, D), lambda b, j: (b, j, 0)),
                  full(mg), full(wo), full(g2), full(wr), full(br)],
        out_specs=[pl.BlockSpec((ns * tj, D), lambda b, j: (lin(b, j), 0)),
                   pl.BlockSpec((ns * tj, D), lambda b, j: (lin(b, j), 0)),
                   pl.BlockSpec((ns, SUBLANES, tj), lambda b, j: (lin(b, j), 0, 0)),
                   pl.BlockSpec((ns * tj, LANES), lambda b, j: (lin(b, j), 0)),
                   pl.BlockSpec((ns, SUBLANES, LANES), lambda b, j: (lin(b, j), 0, 0)),
                   pl.BlockSpec((SUBLANES, LANES), lambda b, j: (0, 0))],
        out_shape=[jax.ShapeDtypeStruct((T, D), f32),
                   jax.ShapeDtypeStruct((T, D), bf16),
                   jax.ShapeDtypeStruct((ntiles, SUBLANES, tj), jnp.int32),
                   jax.ShapeDtypeStruct((T, LANES), f32),
                   jax.ShapeDtypeStruct((ntiles, SUBLANES, LANES), jnp.int32),
                   jax.ShapeDtypeStruct((SUBLANES, LANES), jnp.int32)],
        scratch_shapes=[pltpu.VMEM((SUBLANES, LANES), f32), pltpu.VMEM((D // LANES, ns * tj, LANES), f32)],
        compiler_params=_cparams(("arbitrary", "arbitrary")),
        name="outproj_router",
    )(yh, yf, x, mg, wo, g2, wr, br)


def _padded(c):
    return ((c + MOE_ROWS - 1) // MOE_ROWS) * MOE_ROWS


def _pad_starts(cnt_ref, start_ref):
    def body(e, acc):
        start_ref[e] = acc
        return acc + _padded(cnt_ref[0, e])
    return lax.fori_loop(0, N_EXPERTS, body, jnp.int32(0))


def _segment_copies(n, src, src_off, dst, dst_off, sem, max_rows, wait, src_advances=True, min_rows=1):
    bit = max_rows
    while bit >= min_rows:
        take = n & bit

        @pl.when(take != 0)
        def _(src_off=src_off, dst_off=dst_off, bit=bit):
            cp = pltpu.make_async_copy(src.at[pl.ds(src_off, bit)], dst.at[pl.ds(dst_off, bit)], sem)
            if wait:
                cp.wait()
            else:
                cp.start(priority=ROW_DMA_PRIORITY)
        if src_advances:
            src_off = src_off + take
        dst_off = dst_off + take
        bit //= 2


def _tile_segments(tinfo_ref, t, start_ref, local, local_is_src, remote, sem, tj):
    def copies(long_part):
        off = 0
        for e in range(N_EXPERTS):
            n = tinfo_ref[t, 0, e]
            far = start_ref[e] + tinfo_ref[t, 1, e]
            if long_part:
                args = (n, off, far, tj, SEG_SPLIT)
            else:
                head = n & ~(SEG_SPLIT - 1)
                args = (n, off + head, far + head, SEG_SPLIT // 2, 1)
            n_, near_, far_, hi, lo = args
            if local_is_src:
                _segment_copies(n_, local, near_, remote, far_, sem, hi, wait=False, min_rows=lo)
            else:
                _segment_copies(n_, remote, far_, local, near_, sem, hi, wait=False, min_rows=lo)
            off = off + n

    copies(long_part=False)

    @pl.when(tinfo_ref[t, 2, 0] != 0)
    def _():
        copies(long_part=True)


def _dispatch_kernel(tinfo_ref, cnt_ref, tok_ref, pos_ref, xs_ref, first_ref, nblk_ref,
                     sbufs, zbuf, start_ref, sems, zsem):
    tj = pos_ref.shape[2]
    n_blocks = xs_ref.shape[0] // MOE_ROWS

    @pl.when(pl.program_id(0) == 0)
    def _():
        total = _pad_starts(cnt_ref, start_ref)
        used = total // MOE_ROWS

        def block_ranges(e, c):
            first_ref[e] = start_ref[e] // MOE_ROWS
            nblk_ref[e] = _padded(cnt_ref[0, e]) // MOE_ROWS
            return c
        lax.fori_loop(0, N_EXPERTS, block_ranges, 0)

        zbuf[...] = jnp.zeros_like(zbuf)
        zrows = zbuf.shape[0]
        for wait in (False, True):
            def zero_pad(e, c, wait=wait):
                cnt = cnt_ref[0, e]
                _segment_copies(_padded(cnt) - cnt, zbuf, 0, xs_ref, start_ref[e] + cnt, zsem,
                                zrows, wait, src_advances=False)
                return c
            lax.fori_loop(0, N_EXPERTS, zero_pad, 0)

            def zero_tail(i, c, wait=wait):
                cp = pltpu.make_async_copy(zbuf, xs_ref.at[pl.ds(total + i * zrows, zrows)], zsem)
                if wait:
                    cp.wait()
                else:
                    cp.start()
                return c
            lax.fori_loop(0, (n_blocks - used) * (MOE_ROWS // zrows), zero_tail, 0)

    def wait_tile(k):
        pltpu.make_async_copy(sbufs[k], xs_ref.at[pl.ds(0, 2 * tj)], sems.at[k]).wait()

    slot = lax.broadcasted_iota(jnp.int32, (2 * tj, tj), 0)
    for k in range(TILES_PER_STEP):
        @pl.when(pl.program_id(0) > 0)
        def _(k=k):
            wait_tile(k)
        perm = ((slot == pos_ref[k, 2:3, :]) | (slot == pos_ref[k, 3:4, :])).astype(bf16)
        sorted_rows = jnp.dot(perm, tok_ref[k * tj:(k + 1) * tj, :], preferred_element_type=f32)
        _store_row_tiles(sbufs[k], _pack_bf16_pairs(sorted_rows, is_bf16_exact=True))
        if k > 0:
            _tile_segments(tinfo_ref, k - 1, start_ref, sbufs[k - 1], True, xs_ref, sems.at[k - 1], tj)
    last = TILES_PER_STEP - 1
    _tile_segments(tinfo_ref, last, start_ref, sbufs[last], True, xs_ref, sems.at[last], tj)

    @pl.when(pl.program_id(0) == pl.num_programs(0) - 1)
    def _():
        for k in range(TILES_PER_STEP):
            wait_tile(k)


def _dispatch(tinfo, cnt, tok, pos, n_rows):
    ntiles, _, tj = pos.shape
    T, D = tok.shape
    n_blocks = n_rows // MOE_ROWS
    nt = TILES_PER_STEP
    assert ntiles % nt == 0
    sbuf = pltpu.VMEM(_row_tiles_shape(2 * tj, D), jnp.uint32)
    return pl.pallas_call(
        _dispatch_kernel,
        grid=(ntiles // nt,),
        in_specs=[pl.BlockSpec((nt, SUBLANES, LANES), lambda i: (i, 0, 0), memory_space=pltpu.SMEM),
                  pl.BlockSpec(memory_space=pltpu.SMEM),
                  pl.BlockSpec((nt * tj, D), lambda i: (i, 0)),
                  pl.BlockSpec((nt, SUBLANES, tj), lambda i: (i, 0, 0))],
        out_specs=[pl.BlockSpec(memory_space=pl.ANY),
                   pl.BlockSpec(memory_space=pltpu.SMEM),
                   pl.BlockSpec(memory_space=pltpu.SMEM)],
        out_shape=[jax.ShapeDtypeStruct(_row_tiles_shape(n_rows, D), jnp.uint32),
                   jax.ShapeDtypeStruct((N_EXPERTS,), jnp.int32),
                   jax.ShapeDtypeStruct((N_EXPERTS,), jnp.int32)],
        scratch_shapes=[[sbuf] * nt,
                        pltpu.VMEM(_row_tiles_shape(MOE_ROWS // 2, D), jnp.uint32),
                        pltpu.SMEM((N_EXPERTS,), jnp.int32), pltpu.SemaphoreType.DMA((nt,)),
                        pltpu.SemaphoreType.DMA],
        compiler_params=_cparams(("arbitrary",)),
        name="moe_dispatch",
    )(tinfo, cnt, tok, pos)


def _experts_kernel(first_ref, nblk_ref, xs_ref, wg_ref, wu_ref, wd_ref, ys_ref,
                    wg_s, wu_s, wd_s, xbuf, ybuf, in_sems, out_sems):
    e = pl.program_id(0)
    last_e = pl.num_programs(0) - 1
    rows = MOE_ROWS
    n_total = xs_ref.shape[0] // rows
    used = first_ref[last_e] + nblk_ref[last_e]

    def fetch(g, slot):
        return pltpu.make_async_copy(xs_ref.at[pl.ds(g * rows, rows)], xbuf.at[pl.ds(slot * rows, rows)],
                                     in_sems.at[slot])

    def flush(g, slot):
        return pltpu.make_async_copy(ybuf.at[pl.ds(slot * rows, rows)], ys_ref.at[pl.ds(g * rows, rows)],
                                     out_sems.at[slot])

    for ahead in range(FETCH_AHEAD):
        @pl.when((e == 0) & (used > ahead))
        def _(ahead=ahead):
            fetch(ahead, ahead).start(priority=ROW_DMA_PRIORITY)

    @pl.when(nblk_ref[e] > 0)
    def _():
        wg_s[...] = wg_ref[...].astype(bf16)
        wu_s[...] = wu_ref[...].astype(bf16)
        wd_s[...] = wd_ref[...].astype(bf16)

    def block(b, carry):
        g = first_ref[e] + b
        slot = g % 2
        in_slot = g % (FETCH_AHEAD + 1)
        fetch(g, in_slot).wait()

        @pl.when(g + FETCH_AHEAD < used)
        def _():
            fetch(g + FETCH_AHEAD, (g + FETCH_AHEAD) % (FETCH_AHEAD + 1)).start(priority=ROW_DMA_PRIORITY)

        @pl.when(g >= 2)
        def _():
            flush(g - 2, slot).wait()
        half = rows // EXPERT_ROW_GROUPS
        starts = [slot * rows + r * half for r in range(EXPERT_ROW_GROUPS)]
        gated = []
        for r in range(EXPERT_ROW_GROUPS):
            xb = _unpack_bf16_pairs(_load_row_tiles(xbuf, in_slot * rows + r * half, half)).astype(bf16)
            gated.append((jnp.dot(xb, wg_s[...], preferred_element_type=f32),
                          jnp.dot(xb, wu_s[...], preferred_element_type=f32)))
        for row0, (a, u) in zip(starts, gated):
            h = (a * jax.nn.sigmoid(a) * u).astype(bf16)
            _store_row_tiles(ybuf, _pack_bf16_pairs(jnp.dot(h, wd_s[...], preferred_element_type=f32)), row0)
        flush(g, slot).start(priority=ROW_DMA_PRIORITY)
        return carry
    lax.fori_loop(0, nblk_ref[e], block, 0)

    @pl.when(e == last_e)
    def _():
        for back in (1, 2):
            @pl.when(used >= back)
            def _(back=back):
                flush(used - back, (used - back) % 2).wait()
        ybuf[pl.ds(0, rows)] = jnp.zeros((rows,) + ybuf.shape[1:], ybuf.dtype)
        for wait in (False, True):
            def zero_tail(g, c, wait=wait):
                cp = flush(g, 0)
                if wait:
                    cp.wait()
                else:
                    cp.start()
                return c
            lax.fori_loop(used, n_total, zero_tail, 0)


def _experts(first_block, n_blocks, xs, w_gate, w_up, w_down):
    _, D, De = w_gate.shape
    widx = lambda e, first, nblk: (e, 0, 0)
    in_buf = pltpu.VMEM(_row_tiles_shape((FETCH_AHEAD + 1) * MOE_ROWS, D), jnp.uint32)
    out_buf = pltpu.VMEM(_row_tiles_shape(2 * MOE_ROWS, D), jnp.uint32)
    return pl.pallas_call(
        _experts_kernel,
        grid_spec=pltpu.PrefetchScalarGridSpec(
            num_scalar_prefetch=2,
            grid=(N_EXPERTS,),
            in_specs=[pl.BlockSpec(memory_space=pl.ANY),
                      pl.BlockSpec((None, D, De), widx),
                      pl.BlockSpec((None, D, De), widx),
                      pl.BlockSpec((None, De, D), widx)],
            out_specs=pl.BlockSpec(memory_space=pl.ANY),
            scratch_shapes=[pltpu.VMEM((D, De), bf16), pltpu.VMEM((D, De), bf16), pltpu.VMEM((De, D), bf16),
                            in_buf, out_buf, pltpu.SemaphoreType.DMA((FETCH_AHEAD + 1,)),
                            pltpu.SemaphoreType.DMA((2,))],
        ),
        out_shape=jax.ShapeDtypeStruct(xs.shape, jnp.uint32),
        compiler_params=_cparams(("arbitrary",)),
        name="moe_experts",
    )(first_block, n_blocks, xs, w_gate, w_up, w_down)


def _combine_kernel(tinfo_ref, tnext_ref, cnt_ref, h2_ref, vec_ref, g_ref, ys_ref, o_ref,
                    gbufs, stages, start_ref, sems):
    tj = vec_ref.shape[0] // TILES_PER_STEP
    step = pl.program_id(0)

    def fetch(info_ref, t, buf):
        _tile_segments(info_ref, t, start_ref, gbufs[buf], False, ys_ref, sems.at[buf], tj)

    @pl.when(step == 0)
    def _():
        _pad_starts(cnt_ref, start_ref)
        for t in range(GATHER_AHEAD):
            fetch(tinfo_ref, t, t)

    slot = lax.broadcasted_iota(jnp.int32, (tj, 2 * tj), 1)

    def finish(k, picked):
        v = vec_ref[k * tj:(k + 1) * tj, :]
        y = h2_ref[k * tj:(k + 1) * tj, :] + v[:, 0:1] * picked[:tj] + v[:, 1:2] * picked[tj:]
        out = _rms(y, g_ref[...])
        stage = stages[k // 2]
        for q in range(stage.shape[0]):
            stage[q, pl.ds(k % 2, tj, stride=2), :] = out[:, q * LANES:(q + 1) * LANES]
        if k % 2 == 1:
            r0 = (k // 2) * 2 * tj
            for q in range(stage.shape[0]):
                o_ref[0, r0:r0 + 2 * tj, q * LANES:(q + 1) * LANES] = stage[q]

    picked = None
    for k in range(TILES_PER_STEP):
        pltpu.make_async_copy(ys_ref.at[pl.ds(0, 2 * tj)], gbufs[k], sems.at[k]).wait()
        ahead = k + GATHER_AHEAD
        if ahead < TILES_PER_STEP:
            fetch(tinfo_ref, ahead, ahead)
        else:
            fetch(tnext_ref, ahead - TILES_PER_STEP, ahead - TILES_PER_STEP)
        rows = _unpack_bf16_pairs(_load_row_tiles(gbufs[k])).astype(bf16)
        v = vec_ref[k * tj:(k + 1) * tj, :]
        pick = jnp.concatenate([(slot == v[:, 2 + c:3 + c].astype(jnp.int32)).astype(bf16) for c in range(2)],
                               axis=0)
        now = jnp.dot(pick, rows, preferred_element_type=f32)
        if k > 0:
            finish(k - 1, picked)
        picked = now
    finish(TILES_PER_STEP - 1, picked)


def _combine(tinfo, cnt, h2, vec, final_g, ys, B, L):
    T, D = h2.shape
    ntiles = tinfo.shape[0]
    tj = T // ntiles
    nt = TILES_PER_STEP
    assert GATHER_AHEAD < nt and nt % 2 == 0 and (L // tj) % nt == 0
    nsteps = ntiles // nt
    per_seq = L // (nt * tj)
    g = final_g.reshape(1, D)
    gbuf = pltpu.VMEM(_row_tiles_shape(2 * tj, D), jnp.uint32)
    return pl.pallas_call(
        _combine_kernel,
        grid=(nsteps,),
        in_specs=[pl.BlockSpec((nt, SUBLANES, LANES), lambda i: (i, 0, 0), memory_space=pltpu.SMEM),
                  pl.BlockSpec((nt, SUBLANES, LANES), lambda i: (i + 1, 0, 0), memory_space=pltpu.SMEM),
                  pl.BlockSpec(memory_space=pltpu.SMEM),
                  pl.BlockSpec((nt * tj, D), lambda i: (i, 0)),
                  pl.BlockSpec((nt * tj, LANES), lambda i: (i, 0)),
                  pl.BlockSpec((1, D), lambda i: (0, 0)),
                  pl.BlockSpec(memory_space=pl.ANY)],
        out_specs=pl.BlockSpec((1, nt * tj, D), lambda i: (i // per_seq, i % per_seq, 0)),
        out_shape=jax.ShapeDtypeStruct((B, L, D), f32),
        scratch_shapes=[[gbuf] * nt,
                        [pltpu.VMEM((D // LANES, 2 * tj, LANES), f32)] * (nt // 2),
                        pltpu.SMEM((N_EXPERTS,), jnp.int32), pltpu.SemaphoreType.DMA((nt,))],
        compiler_params=_cparams(("arbitrary",)),
        name="moe_combine",
    )(tinfo, jnp.pad(tinfo, ((0, nt), (0, 0), (0, 0))), cnt, h2, vec, g, ys)


def kernel(x, norm1_g, w_in, conv_w, conv_b, f_w_in, f_b_in, f_w_mid, f_b_mid, f_freq, f_w_out, f_bias,
           mix_g, w_out, norm2_g, w_group, b_group, w_router, b_router, w_gate, w_up, w_down, final_g):
    B, L, D = x.shape
    depth = norm1_g.shape[0]
    assert depth == 1, "the final RMSNorm is fused into the single layer's MoE combine"
    C = D // 2
    Lh = L // 2
    N = B * C
    T = B * L
    n_rows = T * 2 + N_EXPERTS * MOE_ROWS
    cfou = _fourier_consts(L)
    i = 0
    wcat = _prep_win(w_in[i], C)
    hcat, asum = _hyena_filters(L, C, f_w_in[i], f_b_in[i], f_w_mid[i], f_b_mid[i], f_freq[i], f_w_out[i])
    z, x0, u4, cfwd, cinv = _inproj(x, norm1_g[i], wcat, conv_w[i], conv_b[i], C)
    kspec, k0 = _filter_spectrum(cfwd, hcat, asum, L, C)
    z = z.reshape(2, Lh, N)
    x0 = x0.reshape(2, Lh, N)
    u4 = u4.reshape(4, Lh, N)
    s = _conv_fwd(cfwd, z, kspec, k0, C)
    yh = _conv_inv(cinv, s, x0, z, f_bias[i], C)
    yf = _fourier_dft(cfou, u4, L, C)
    wr = jnp.concatenate([jnp.transpose(w_router[i], (1, 0, 2)).reshape(D, N_EXPERTS), w_group[i]], axis=1)
    wr = jnp.pad(wr, ((0, 0), (0, LANES - wr.shape[1])))
    br = jnp.pad(jnp.concatenate([b_router[i].reshape(-1), b_group[i]]), (0, LANES - N_EXPERTS - N_GROUPS))
    h2, tok, pos, vec, tinfo, cnt = _outproj_router(yh, yf, x, mix_g[i], w_out[i].astype(bf16), norm2_g[i],
                                                     wr, br.reshape(1, LANES), C)
    xs, first_block, n_blocks = _dispatch(tinfo, cnt, tok, pos, n_rows)
    ys = _experts(first_block, n_blocks, xs, w_gate[i], w_up[i], w_down[i])
    return _combine(tinfo, cnt, h2, vec, final_g, ys, B, L)
```

```python
import functools
import math

import jax
import jax.numpy as jnp
from jax import lax
from jax.experimental import pallas as pl
from jax.experimental.pallas import tpu as pltpu

HEAD_DIM = 64
N_GROUPS = 4
EXPERTS_PER_GROUP = 8
N_EXPERTS = N_GROUPS * EXPERTS_PER_GROUP
FILTER_BANDS = 16
DECAY_FAST_PCT = 0.3
DECAY_SLOW_PCT = 1.5
DECAY_TARGET = 1e-2
EPS = 1e-6

LANES = 128
SUBLANES = 8
MXU_DIM = 256
CONV_ROWS = 512
CONV_ROW_GROUPS = 2
CONV_COLS = 512
FOURIER_COLS = 256
MOE_ROWS = 512
EXPERT_ROW_GROUPS = 2
FETCH_AHEAD = 3
ROW_DMA_PRIORITY = 1
ROW_TILES_PER_STEP = 2
TILES_PER_STEP = 4
GATHER_AHEAD = 3
SEG_SPLIT = 32
VMEM_LIMIT = 56 * 1024 * 1024

f32 = jnp.float32
bf16 = jnp.bfloat16
HI = lax.Precision.HIGHEST


def _cparams(sem):
    return pltpu.CompilerParams(dimension_semantics=sem, vmem_limit_bytes=VMEM_LIMIT)


def _tile(pref, n):
    return min(pref, n)


_SIN_TERMS = tuple((-1.0) ** k / math.factorial(2 * k + 1) for k in range(7))
_COS_TERMS = tuple((-1.0) ** k / math.factorial(2 * k) for k in range(8))


def _cos_sin_of_turn_fraction(prod, m):
    num = prod & (m - 1)
    quadrant = num // (m // 4)
    x = (num & (m // 4 - 1)).astype(f32) * (2.0 * math.pi / m)
    x2 = x * x
    s = jnp.full_like(x, _SIN_TERMS[-1])
    for c in _SIN_TERMS[-2::-1]:
        s = s * x2 + c
    s = s * x
    c_ = jnp.full_like(x, _COS_TERMS[-1])
    for c in _COS_TERMS[-2::-1]:
        c_ = c_ * x2 + c
    odd = (quadrant & 1) == 1
    cos = jnp.where(odd, s, c_)
    sin = jnp.where(odd, c_, s)
    cos = jnp.where((quadrant == 1) | (quadrant == 2), -cos, cos)
    sin = jnp.where(quadrant >= 2, -sin, sin)
    return cos, sin


def _cos_sin_blocks(a, s, b0, m, ncb, d):
    tc, ts = _cos_sin_of_turn_fraction(a * (s * d), m)
    b0_lane = jnp.zeros(d.shape, jnp.int32)
    for cb in range(ncb):
        b0_lane = jnp.where(d == cb, b0(cb), b0_lane)
    ac, asn = _cos_sin_of_turn_fraction(a * b0_lane, m)
    for cb in range(ncb):
        ca = ac[:, cb:cb + 1]
        sa = asn[:, cb:cb + 1]
        yield cb, ca * tc - sa * ts, sa * tc + ca * ts


def _write_dft_rows(fwd_ref, inv_ref, row_block, L):
    Lh = L // 2
    tr = fwd_ref.shape[1]
    r = lax.broadcasted_iota(jnp.int32, (tr, LANES), 0) + row_block * tr
    d = lax.broadcasted_iota(jnp.int32, (tr, LANES), 1)
    alt_r = (1 - 2 * (r & 1)).astype(f32)
    alt_d = (1 - 2 * (d & 1)).astype(f32)
    ncb = Lh // LANES
    fams = [
        (r, 1, lambda cb: cb * LANES, L),
        (r, 2, lambda cb: 2 * cb * LANES + 1, 2 * L),
        (2 * r + 1, 1, lambda cb: cb * LANES, 2 * L),
    ]
    for fi, (a, s, b0, m) in enumerate(fams):
        for cb, c, sn in _cos_sin_blocks(a, s, b0, m, ncb, d):
            sl = slice(cb * LANES, (cb + 1) * LANES)
            if fi == 0:
                fwd_ref[0, :, sl] = c.astype(bf16)
                inv_ref[0, :, sl] = c.astype(bf16)
                fwd_ref[1, :, sl] = jnp.where(r == 0, alt_d, sn).astype(bf16)
                nst = -sn
                if cb == 0:
                    nst = jnp.where(d == 0, -alt_r, nst)
                inv_ref[1, :, sl] = nst.astype(bf16)
            elif fi == 1:
                fwd_ref[2, :, sl] = c.astype(bf16)
                fwd_ref[3, :, sl] = jnp.where(r == 0, alt_d, sn).astype(bf16)
            else:
                inv_ref[2, :, sl] = c.astype(bf16)
                nst = -sn
                if cb == 0:
                    nst = jnp.where(d == 0, -alt_r, nst)
                inv_ref[3, :, sl] = nst.astype(bf16)


def _fourier_consts_kernel(o_ref, *, L):
    n = L // 4
    tr = o_ref.shape[0]
    r = lax.broadcasted_iota(jnp.int32, (tr, LANES), 0) + pl.program_id(0) * tr
    d = lax.broadcasted_iota(jnp.int32, (tr, LANES), 1)

    def col_l(cb):
        parity, j0 = divmod(cb * LANES, n // 2)
        return 2 * j0 + parity

    for cb, c, sn in _cos_sin_blocks(r, 2, col_l, n, n // LANES, d):
        o_ref[:, cb * LANES:(cb + 1) * LANES] = c.astype(bf16)
        o_ref[:, n + cb * LANES:n + (cb + 1) * LANES] = (-sn).astype(bf16)


def _fourier_consts(L):
    n = L // 4
    tr = _tile(256, n)
    return pl.pallas_call(
        functools.partial(_fourier_consts_kernel, L=L),
        grid=(n // tr,),
        out_specs=pl.BlockSpec((tr, 2 * n), lambda i: (i, 0)),
        out_shape=jax.ShapeDtypeStruct((n, 2 * n), bf16),
        compiler_params=_cparams(("parallel",)),
        name="fourier_consts",
    )()


def _prep_win_kernel(w_ref, o_ref, *, C):
    w = w_ref[...]
    i = lax.broadcasted_iota(jnp.int32, (LANES, LANES), 0)
    j = lax.broadcasted_iota(jnp.int32, (LANES, LANES), 1)
    same = (i // HEAD_DIM) == (j // HEAD_DIM)
    cos, sin = _cos_sin_of_turn_fraction((i % HEAD_DIM) * (j % HEAD_DIM), HEAD_DIM)
    bc = jnp.where(same, cos, 0.0)
    bs = jnp.where(same, sin, 0.0)
    o_ref[:, :3 * C] = w[:, :3 * C].astype(bf16)
    for k in range(0, C, LANES):
        wf = w[:, 3 * C + k:3 * C + k + LANES]
        o_ref[:, 3 * C + k:3 * C + k + LANES] = jnp.dot(wf, bc, precision=HI, preferred_element_type=f32).astype(bf16)
        o_ref[:, 4 * C + k:4 * C + k + LANES] = jnp.dot(wf, bs, precision=HI, preferred_element_type=f32).astype(bf16)


def _prep_win(w_in, C):
    D = w_in.shape[0]
    tr = _tile(256, D)
    return pl.pallas_call(
        functools.partial(_prep_win_kernel, C=C),
        grid=(D // tr,),
        in_specs=[pl.BlockSpec((tr, 4 * C), lambda i: (i, 0))],
        out_specs=pl.BlockSpec((tr, 5 * C), lambda i: (i, 0)),
        out_shape=jax.ShapeDtypeStruct((D, 5 * C), bf16),
        compiler_params=_cparams(("parallel",)),
        name="prep_win",
    )(w_in)


def _dot_split(a, b):
    a_hi = a.astype(bf16)
    b_hi = b.astype(bf16)
    a_lo = (a - a_hi.astype(f32)).astype(bf16)
    b_lo = (b - b_hi.astype(f32)).astype(bf16)
    return (jnp.dot(a_hi, b_hi, preferred_element_type=f32) + jnp.dot(a_lo, b_hi, preferred_element_type=f32)
            + jnp.dot(a_hi, b_lo, preferred_element_type=f32))


def _write_filter_rows(win_ref, bin_ref, wmid_ref, bmid_ref, freq_ref, wout_ref, h_ref, asum_ref, step, L, C):
    tr = h_ref.shape[1]
    lane = lax.broadcasted_iota(jnp.int32, (tr, LANES), 1)
    row = lax.broadcasted_iota(jnp.int32, (tr, LANES), 0) + step * tr
    band_i = jnp.where(lane <= FILTER_BANDS, lane - 1, lane - 1 - FILTER_BANDS)
    band = 1e-4 + band_i.astype(f32) * ((FILTER_BANDS - 1 - 1e-4) / (FILTER_BANDS - 1))
    ch = lax.broadcasted_iota(jnp.int32, (tr, 2 * C), 1) % C
    max_decay = math.log(DECAY_TARGET) / DECAY_FAST_PCT
    min_decay = math.log(DECAY_TARGET) / DECAY_SLOW_PCT
    delta = jnp.abs(min_decay + ch.astype(f32) * ((max_decay - min_decay) / (C - 1)))
    is_bwd = lax.broadcasted_iota(jnp.int32, (tr, 2 * C), 1) >= C
    row_c = lax.broadcasted_iota(jnp.int32, (tr, 2 * C), 0) + step * tr

    @pl.when(step == 0)
    def _():
        asum_ref[...] = jnp.zeros_like(asum_ref)

    total = jnp.zeros((1, 2 * C), f32)
    for p in range(2):
        pos = (2 * row + p).astype(f32)
        t = pos * (1.0 / (L - 1))
        w = pos * (2.0 * math.pi / L)
        fw = band * w
        z = jnp.where(lane == 0, t,
                      jnp.where(lane <= FILTER_BANDS, jnp.cos(fw),
                                jnp.where(lane <= 2 * FILTER_BANDS, -jnp.sin(fw), 0.0)))
        h = jnp.sin(freq_ref[0:1, :] * (_dot_split(z, win_ref[...]) + bin_ref[...]))
        for i in range(wmid_ref.shape[0]):
            h = jnp.sin(freq_ref[i + 1:i + 2, :] * (_dot_split(h, wmid_ref[i]) + bmid_ref[i:i + 1, :]))
        out = _dot_split(h, wout_ref[...])
        tc = (2 * row_c + p).astype(f32) * (1.0 / (L - 1))
        out = out * jnp.exp(-tc * delta)
        if p == 0:
            out = jnp.where(is_bwd & (row_c == 0), 0.0, out)
        h_ref[p] = out.astype(bf16)
        total = total + jnp.sum(jnp.abs(out), axis=0, keepdims=True)
    asum_ref[0:1, :] += total


def _filter_kernel(win_ref, bin_ref, wmid_ref, bmid_ref, freq_ref, wout_ref, h_ref, asum_ref, *, L, C):
    _write_filter_rows(win_ref, bin_ref, wmid_ref, bmid_ref, freq_ref, wout_ref, h_ref, asum_ref,
                       pl.program_id(0), L, C)


def _hyena_filters(L, C, f_w_in, f_b_in, f_w_mid, f_b_mid, f_freq, f_w_out):
    Lh = L // 2
    tr = _tile(256, Lh)
    order = f_w_in.shape[1]
    win = jnp.pad(f_w_in, ((0, LANES - f_w_in.shape[0]), (0, 0)))
    full = lambda a: pl.BlockSpec(a.shape, lambda i: (0,) * a.ndim)
    args = (win, f_b_in.reshape(1, order), f_w_mid, f_b_mid, f_freq, f_w_out)
    return pl.pallas_call(
        functools.partial(_filter_kernel, L=L, C=C),
        grid=(Lh // tr,),
        in_specs=[full(a) for a in args],
        out_specs=[pl.BlockSpec((2, tr, 2 * C), lambda i: (0, i, 0)),
                   pl.BlockSpec((SUBLANES, 2 * C), lambda i: (0, 0))],
        out_shape=[jax.ShapeDtypeStruct((2, Lh, 2 * C), bf16),
                   jax.ShapeDtypeStruct((SUBLANES, 2 * C), f32)],
        compiler_params=_cparams(("arbitrary",)),
        name="hyena_filters",
    )(*args)


def _fwd_products(c_ref, x_ref, rows=slice(None)):
    k = c_ref.shape[2]
    even = c_ref[0:2, rows, :]
    odd = c_ref[2:4, rows, :]
    tm = even.shape[1]
    even = jnp.dot(even.reshape(2 * tm, k), x_ref[0], preferred_element_type=f32)
    odd = jnp.dot(odd.reshape(2 * tm, k), x_ref[1], preferred_element_type=f32)
    return even[:tm], even[tm:], odd[:tm], odd[tm:]


def _filter_spectrum_kernel(c_ref, h_ref, asum_ref, k_ref, k0_ref, *, L, C):
    ce, se, co, so = _fwd_products(c_ref, h_ref)
    a = asum_ref[0:1, :]
    scale = 1.0 / ((a[:, :C] + a[:, C:]) * L)
    F = lambda v: v[:, :C]
    G = lambda v: v[:, C:]
    k_ref[0] = (F(ce) + F(co) + G(ce) + G(co)) * scale
    k_ref[1] = (-(F(se) + F(so)) + (G(se) + G(so))) * scale
    k_ref[2] = (F(ce) - F(co) + G(ce) - G(co)) * scale
    k_ref[3] = ((F(se) - F(so)) - (G(se) - G(so))) * scale

    @pl.when(pl.program_id(0) == 0)
    def _():
        k0_ref[...] = jnp.zeros_like(k0_ref)
        k0_ref[0:1, :] = ((F(se) + G(se)) * scale)[0:1, :]
        k0_ref[1:2, :] = ((F(so) - G(so)) * scale)[0:1, :]


def _filter_spectrum(cfwd, hcat, asum, L, C):
    Lh = L // 2
    tm = _tile(256, Lh)
    return pl.pallas_call(
        functools.partial(_filter_spectrum_kernel, L=L, C=C),
        grid=(Lh // tm,),
        in_specs=[pl.BlockSpec((4, tm, Lh), lambda i: (0, i, 0)),
                  pl.BlockSpec((2, Lh, 2 * C), lambda i: (0, 0, 0)),
                  pl.BlockSpec((SUBLANES, 2 * C), lambda i: (0, 0))],
        out_specs=[pl.BlockSpec((4, tm, C), lambda i: (0, i, 0)),
                   pl.BlockSpec((SUBLANES, C), lambda i: (0, 0))],
        out_shape=[jax.ShapeDtypeStruct((4, Lh, C), f32),
                   jax.ShapeDtypeStruct((SUBLANES, C), f32)],
        compiler_params=_cparams(("arbitrary",)),
        name="filter_spectrum",
    )(cfwd, hcat, asum)


def _conv_fwd_kernel(c_ref, z_ref, k_ref, k0_ref, s_ref):
    C = k_ref.shape[2]
    tm = c_ref.shape[1]
    tg = tm // CONV_ROW_GROUPS
    groups = [slice(r * tg, (r + 1) * tg) for r in range(CONV_ROW_GROUPS)]
    products = [_fwd_products(c_ref, z_ref, rows) for rows in groups]
    is_row0 = (lax.broadcasted_iota(jnp.int32, (SUBLANES, C), 0) == 0) & (pl.program_id(1) == 0)
    for rows, prod in zip(groups, products):
        for h in range(z_ref.shape[2] // C):
            cols = slice(h * C, (h + 1) * C)
            ce, se, co, so = (v[:, cols] for v in prod)
            zsr, zsi, zdr, zdi = ce + co, -(se + so), ce - co, se - so
            ksr, ksi, kdr, kdi = (k_ref[i, rows, :] for i in range(4))
            psr = zsr * ksr - zsi * ksi
            psi = zsr * ksi + zsi * ksr
            pdr = zdr * kdr - zdi * kdi
            pdi = zdr * kdi + zdi * kdr
            gen = [psr + pdr, psi - pdi, psr - pdr, psi + pdi]
            if rows.start == 0:
                top = slice(0, SUBLANES)
                p0 = zsr[top] * ksr[top]
                pl_ = zdr[top] * kdr[top]
                ak, bk = k0_ref[0:1, :], k0_ref[1:2, :]
                phr = se[top] * ak - so[top] * bk
                phi = -(se[top] * bk + so[top] * ak)
                spec = (0.5 * (p0 + pl_), -phr, 0.5 * (p0 - pl_), phi)
                gen = [jnp.concatenate([jnp.where(is_row0, spec[i], gen[i][top]), gen[i][SUBLANES:]], axis=0)
                       for i in range(4)]
            for i in range(4):
                s_ref[i, rows, cols] = gen[i].astype(bf16)


def _conv_fwd(cfwd, z, kspec, k0, C):
    _, Lh, N = z.shape
    tm = _tile(CONV_ROWS, Lh)
    tn = _tile(CONV_COLS, N)
    return pl.pallas_call(
        _conv_fwd_kernel,
        grid=(N // tn, Lh // tm),
        in_specs=[pl.BlockSpec((4, tm, Lh), lambda n, i: (0, i, 0)),
                  pl.BlockSpec((2, Lh, tn), lambda n, i: (0, 0, n)),
                  pl.BlockSpec((4, tm, C), lambda n, i: (0, i, 0)),
                  pl.BlockSpec((SUBLANES, C), lambda n, i: (0, 0))],
        out_specs=pl.BlockSpec((4, tm, tn), lambda n, i: (0, i, n)),
        out_shape=jax.ShapeDtypeStruct((4, Lh, N), bf16),
        compiler_params=_cparams(("parallel", "arbitrary")),
        name="conv_fwd",
    )(cfwd, z, kspec, k0)


def _pair_products(c_ref, r_ref):
    ev = (jnp.dot(c_ref[0], r_ref[0], preferred_element_type=f32)
          + jnp.dot(c_ref[1], r_ref[1], preferred_element_type=f32))
    od = (jnp.dot(c_ref[2], r_ref[2], preferred_element_type=f32)
          + jnp.dot(c_ref[3], r_ref[3], preferred_element_type=f32))
    return ev, od


def _conv_inv_kernel(c_ref, s_ref, x0_ref, z_ref, fb_ref, y_ref):
    conv = _pair_products(c_ref, s_ref)
    C = fb_ref.shape[1]
    fb = fb_ref[...]
    for p in range(2):
        for h in range(z_ref.shape[2] // C):
            cols = slice(h * C, (h + 1) * C)
            zp = z_ref[p, :, cols].astype(f32)
            y_ref[p, :, cols] = (x0_ref[p, :, cols].astype(f32) * (conv[p][:, cols] + zp * fb)).astype(bf16)


def _conv_inv(cinv, s, x0, z, f_bias, C):
    _, Lh, N = s.shape
    tm = _tile(CONV_ROWS, Lh)
    tn = _tile(CONV_COLS, N)
    return pl.pallas_call(
        _conv_inv_kernel,
        grid=(N // tn, Lh // tm),
        in_specs=[pl.BlockSpec((4, tm, Lh), lambda n, i: (0, i, 0)),
                  pl.BlockSpec((4, Lh, tn), lambda n, i: (0, 0, n)),
                  pl.BlockSpec((2, tm, tn), lambda n, i: (0, i, n)),
                  pl.BlockSpec((2, tm, tn), lambda n, i: (0, i, n)),
                  pl.BlockSpec((1, C), lambda n, i: (0, 0))],
        out_specs=pl.BlockSpec((2, tm, tn), lambda n, i: (0, i, n)),
        out_shape=jax.ShapeDtypeStruct((2, Lh, N), bf16),
        compiler_params=_cparams(("parallel", "arbitrary")),
        name="conv_inv",
    )(cinv, s, x0, z, f_bias.reshape(1, C))


def _fourier_kernel(c_ref, u_ref, y_ref, rhs, stage, tw1, tw2, *, L, scale):
    Lh, n, H = L // 2, L // 4, L // 8
    tn = u_ref.shape[2]

    @pl.when(pl.program_id(0) == 0)
    def _():
        r1 = lax.broadcasted_iota(jnp.int32, (Lh, LANES), 0)
        tw1[0], tw1[1] = _cos_sin_of_turn_fraction(2 * (r1 % n) + r1 // n, L)
        r2 = lax.broadcasted_iota(jnp.int32, (n, LANES), 0)
        tw2[0], tw2[1] = _cos_sin_of_turn_fraction(2 * (r2 % H) + r2 // H, Lh)

    def rotate(a, b, c, s):
        return c * a - s * b, s * a + c * b

    for q in range(tn // LANES):
        cols = slice(q * LANES, (q + 1) * LANES)
        for p in range(2):
            first = slice(p * n, p * n + H)
            second = slice(p * n + H, (p + 1) * n)
            out = slice(p * H, (p + 1) * H)
            out_b = slice(n + p * H, n + (p + 1) * H)
            c2, s2 = tw2[0, out, :], tw2[1, out, :]

            def emit(sub, a0, b0, a1, b1):
                rhs[sub, out, cols] = (a0 + a1).astype(bf16)
                rhs[sub, out_b, cols] = (b0 + b1).astype(bf16)
                da, db = rotate(a0 - a1, b0 - b1, c2, s2)
                rhs[sub + 1, out, cols] = da.astype(bf16)
                rhs[sub + 1, out_b, cols] = db.astype(bf16)

            ld = lambda k, rows: u_ref[k, rows, cols].astype(f32)
            emit(0, ld(0, first), ld(1, first), ld(0, second), ld(1, second))
            odd = [rotate(ld(2, rows), ld(3, rows), tw1[0, rows, :], tw1[1, rows, :]) for rows in (first, second)]
            emit(2, odd[0][0], odd[0][1], odd[1][0], odd[1][1])

    for parity in range(2):
        pair = [jnp.dot(c_ref[...], rhs[2 * parity + h], preferred_element_type=f32) * scale for h in range(2)]
        for q in range(tn // LANES):
            cols = slice(q * LANES, (q + 1) * LANES)
            for h in range(2):
                stage[q, pl.ds(h, n, stride=2), :] = pair[h][:, cols]
            y_ref[parity, :, cols] = stage[q].astype(bf16)


def _fourier_dft(cfou, u4, L, C):
    _, Lh, N = u4.shape
    n = L // 4
    tn = _tile(FOURIER_COLS, C)
    return pl.pallas_call(
        functools.partial(_fourier_kernel, L=L, scale=1.0 / math.sqrt(L * HEAD_DIM)),
        grid=(N // tn,),
        in_specs=[pl.BlockSpec((n, 2 * n), lambda j: (0, 0)),
                  pl.BlockSpec((4, Lh, tn), lambda j: (0, 0, j))],
        out_specs=pl.BlockSpec((2, Lh, tn), lambda j: (0, 0, j)),
        out_shape=jax.ShapeDtypeStruct((2, Lh, N), bf16),
        scratch_shapes=[pltpu.VMEM((4, 2 * n, tn), bf16), pltpu.VMEM((tn // LANES, Lh, LANES), f32),
                        pltpu.VMEM((2, Lh, LANES), f32), pltpu.VMEM((2, n, LANES), f32)],
        compiler_params=_cparams(("arbitrary",)),
        name="fourier_dft",
    )(cfou, u4)


def _rms(x, g):
    return x * lax.rsqrt(jnp.mean(x * x, axis=-1, keepdims=True) + EPS) * g


def _stage_rows(chunks, x):
    for k in range(chunks.shape[0]):
        chunks[k] = x[:, k * LANES:(k + 1) * LANES]


def _parity_rows(chunks, p, n):
    return jnp.concatenate([chunks[k, pl.ds(p, n, stride=2), :] for k in range(chunks.shape[0])], axis=1)


def _inproj_kernel(xa_ref, xb_ref, pa_ref, na_ref, pb_ref, nb_ref, g_ref, w_ref, cw_ref, cb_ref,
                   z_ref, x0_ref, u_ref, cfwd_ref, cinv_ref, stage, *, C, D, L):
    tj = xa_ref.shape[1] // 2
    jt = pl.program_id(1)
    n_half = pl.num_programs(1)
    _write_dft_rows(cfwd_ref, cinv_ref, pl.program_id(0) * n_half + jt, L)
    g = g_ref[...]
    w = w_ref[...]
    cw0, cw1, cw2 = cw_ref[0:1, :], cw_ref[1:2, :], cw_ref[2:3, :]
    cb = cb_ref[...]
    row = lax.broadcasted_iota(jnp.int32, (tj, 3 * C), 0)
    four = []
    halves = ((xa_ref, pa_ref, na_ref, jt == 0, False),
              (xb_ref, pb_ref, nb_ref, False, jt == n_half - 1))
    for h, (x_ref, p_ref, n_ref, at_start, at_end) in enumerate(halves):
        _stage_rows(stage, x_ref[0])
        lhs = jnp.concatenate([
            _rms(_parity_rows(stage, 0, tj), g).astype(bf16),
            _rms(_parity_rows(stage, 1, tj), g).astype(bf16),
            _rms(p_ref[0], g).astype(bf16),
            _rms(n_ref[0], g).astype(bf16)], axis=0)
        res = jnp.dot(lhs, w, preferred_element_type=f32)
        pe, po = res[:tj, :3 * C], res[tj:2 * tj, :3 * C]
        prev = res[2 * tj + SUBLANES - 1:2 * tj + SUBLANES, :3 * C]
        nxt = res[2 * tj + SUBLANES:2 * tj + SUBLANES + 1, :3 * C]
        if at_start is not False:
            prev = jnp.where(at_start, 0.0, prev)
        if at_end is not False:
            nxt = jnp.where(at_end, 0.0, nxt)
        po_dn = jnp.where(row == 0, prev, pltpu.roll(po, 1, 0))
        pe_up = jnp.where(row == tj - 1, nxt, pltpu.roll(pe, tj - 1, 0))
        uc = (cb + cw0 * po_dn + cw1 * pe + cw2 * po,
              cb + cw0 * pe + cw1 * po + cw2 * pe_up)
        for p in range(2):
            x0_ref[p, h] = uc[p][:, :C].astype(bf16)
            z_ref[p, h] = (uc[p][:, C:2 * C] * uc[p][:, 2 * C:]).astype(bf16)
        four.append((res[:tj, 3 * C:], res[tj:2 * tj, 3 * C:]))
    for p in range(2):
        fa, fb = four[0][p], four[1][p]
        u_ref[0, p] = (fa[:, :C] + fb[:, :C]).astype(bf16)
        u_ref[1, p] = (fa[:, C:] + fb[:, C:]).astype(bf16)
        u_ref[2, p] = (fa[:, :C] - fb[:, :C]).astype(bf16)
        u_ref[3, p] = (fa[:, C:] - fb[:, C:]).astype(bf16)


def _inproj(x, norm_g, wcat, conv_w, conv_b, C):
    B, L, D = x.shape
    Lq = L // 4
    tj = _tile(256, Lq)
    nq = Lq // tj
    N = B * C
    hb = 2 * tj // SUBLANES
    last = L // SUBLANES - 1
    main = lambda off: pl.BlockSpec((1, 2 * tj, D), lambda b, j: (b, j + off, 0))
    prev = lambda off: pl.BlockSpec((1, SUBLANES, D), lambda b, j: (b, jnp.maximum((j + off) * hb - 1, 0), 0))
    nxt = lambda off: pl.BlockSpec((1, SUBLANES, D), lambda b, j: (b, jnp.minimum((j + off + 1) * hb, last), 0))
    full = lambda a: pl.BlockSpec(a.shape, lambda b, j: (0,) * a.ndim)
    g = norm_g.reshape(1, D)
    cb = conv_b.reshape(1, 3 * C)
    Lh = L // 2
    dft_rows = Lh // (B * nq)
    assert dft_rows * B * nq == Lh and dft_rows % (2 * SUBLANES) == 0
    dft_spec = pl.BlockSpec((4, dft_rows, Lh), lambda b, j: (0, b * nq + j, 0))
    dft_shape = jax.ShapeDtypeStruct((4, Lh, Lh), bf16)
    return pl.pallas_call(
        functools.partial(_inproj_kernel, C=C, D=D, L=L),
        grid=(B, nq),
        in_specs=[main(0), main(nq), prev(0), nxt(0), prev(nq), nxt(nq),
                  full(g), full(wcat), full(conv_w), full(cb)],
        out_specs=[pl.BlockSpec((2, 2, tj, C), lambda b, j: (0, 0, j, b)),
                   pl.BlockSpec((2, 2, tj, C), lambda b, j: (0, 0, j, b)),
                   pl.BlockSpec((4, 2, tj, C), lambda b, j: (0, 0, j, b)),
                   dft_spec, dft_spec],
        out_shape=[jax.ShapeDtypeStruct((2, 2, Lq, N), bf16),
                   jax.ShapeDtypeStruct((2, 2, Lq, N), bf16),
                   jax.ShapeDtypeStruct((4, 2, Lq, N), bf16),
                   dft_shape, dft_shape],
        scratch_shapes=[pltpu.VMEM((D // LANES, 2 * tj, LANES), f32)],
        compiler_params=_cparams(("parallel", "arbitrary")),
        name="inproj",
    )(x, x, x, x, x, x, g, wcat, conv_w, cb)


def _pack_bf16_pairs(v, is_bf16_exact=False):
    k = v.shape[1] // 2
    rounded = v if is_bf16_exact else v.astype(bf16).astype(f32)
    hi = pltpu.bitcast(rounded[:, :k], jnp.uint32)
    lo = pltpu.bitcast(rounded[:, k:], jnp.uint32)
    return hi | (lo >> 16)


def _unpack_bf16_pairs(u):
    hi = pltpu.bitcast(u & jnp.uint32(0xFFFF0000), f32)
    lo = pltpu.bitcast(u << 16, f32)
    return jnp.concatenate([hi, lo], axis=1)


def _row_tiles_shape(n, D):
    return (n, D // (2 * LANES), LANES)


def _store_row_tiles(ref, packed, row0=0):
    total, per_row, _ = ref.shape
    n = packed.shape[0]
    flat = ref.reshape(total * per_row, LANES)
    for s in range(per_row):
        flat[pl.ds(row0 * per_row + s, n, stride=per_row), :] = packed[:, s * LANES:(s + 1) * LANES]


def _load_row_tiles(ref, row0=0, n=None):
    total, per_row, _ = ref.shape
    n = total if n is None else n
    flat = ref.reshape(total * per_row, LANES)
    return jnp.concatenate([flat[pl.ds(row0 * per_row + s, n, stride=per_row), :] for s in range(per_row)],
                           axis=1)


def _outproj_router_kernel(yh_ref, yf_ref, x_ref, mg_ref, wo_ref, g2_ref, wr_ref, br_ref,
                           h2_ref, tok_ref, pos_ref, vec_ref, tinfo_ref, cnt_ref, carry_ref, stage, *, C):
    n_sub = pos_ref.shape[0]
    tj = pos_ref.shape[2]
    _stage_rows(stage, x_ref[0])

    @pl.when((pl.program_id(0) == 0) & (pl.program_id(1) == 0))
    def _():
        carry_ref[...] = jnp.zeros_like(carry_ref)

    li = lax.broadcasted_iota(jnp.int32, (LANES, LANES), 0)
    lj = lax.broadcasted_iota(jnp.int32, (LANES, LANES), 1)
    hw = _tile(MXU_DIM, C)
    ai = lax.broadcasted_iota(jnp.int32, (hw, hw), 0)
    aj = lax.broadcasted_iota(jnp.int32, (hw, hw), 1)
    avg = jnp.where(ai // HEAD_DIM == aj // HEAD_DIM, 1.0 / HEAD_DIM, 0.0).astype(bf16)
    upper = (li < lj).astype(bf16)
    ri = lax.broadcasted_iota(jnp.int32, (tj, tj), 0)
    ci = lax.broadcasted_iota(jnp.int32, (tj, tj), 1)
    tri = (ci < ri).astype(bf16)
    wr = wr_ref[...]
    wr_hi = wr.astype(bf16)
    wr_lo = (wr - wr_hi.astype(f32)).astype(bf16)
    wr_both = jnp.concatenate([wr_hi, wr_lo], axis=1)
    lane = lax.broadcasted_iota(jnp.int32, (n_sub * tj, LANES), 1).astype(f32)
    sub = lax.broadcasted_iota(jnp.int32, (SUBLANES, LANES), 0)
    neg = jnp.float32(-jnp.inf)
    big = jnp.float32(LANES)

    def first_argmax(vals, mask):
        v = jnp.where(mask, vals, neg)
        m = jnp.max(v, axis=1, keepdims=True)
        idx = jnp.min(jnp.where(mask & (v == m), lane, big), axis=1, keepdims=True)
        return m, idx

    normed = []
    for s in range(n_sub):
        r, p = divmod(s, 2)
        parts = []
        for i, ref in enumerate((yh_ref, yf_ref)):
            y = ref[p, r * tj:(r + 1) * tj, :]
            ysq = y * y
            ms = jnp.concatenate([jnp.dot(ysq[:, k:k + hw], avg, preferred_element_type=f32)
                                  for k in range(0, C, hw)], axis=1)
            scale = lax.rsqrt(ms + EPS) * mg_ref[:, i * C:(i + 1) * C]
            parts.append(y * scale.astype(bf16))
        normed.append(jnp.concatenate(parts, axis=1))
    proj = jnp.dot(jnp.concatenate(normed, axis=0), wo_ref[...], preferred_element_type=f32)
    t_hi, t_lo = [], []
    for s in range(n_sub):
        r, p = divmod(s, 2)
        rows = slice(s * tj, (s + 1) * tj)
        h2 = _parity_rows(stage, r * 2 * tj + p, tj) + proj[rows]
        h2_ref[rows, :] = h2
        tok = _rms(h2, g2_ref[...])
        hi = tok.astype(bf16)
        tok_ref[rows, :] = hi
        t_hi.append(hi)
        t_lo.append((tok - hi.astype(f32)).astype(bf16))
    hi_terms = jnp.dot(jnp.concatenate(t_hi, axis=0), wr_both, preferred_element_type=f32)
    logits = (hi_terms[:, :LANES] + hi_terms[:, LANES:]
              + jnp.dot(jnp.concatenate(t_lo, axis=0), wr_hi, preferred_element_type=f32) + br_ref[...])

    gmask = (lane >= N_EXPERTS) & (lane < N_EXPERTS + N_GROUPS)
    gmax, gidx = first_argmax(logits, gmask)
    gsel = gidx - N_EXPERTS
    p_group = 1.0 / jnp.sum(jnp.where(gmask, jnp.exp(logits - gmax), 0.0), axis=1, keepdims=True)
    emask = (lane >= gsel * EXPERTS_PER_GROUP) & (lane < (gsel + 1) * EXPERTS_PER_GROUP)
    l0, e0 = first_argmax(logits, emask)
    l1, e1 = first_argmax(logits, emask & (lane != e0))
    ex = jnp.exp(l1 - l0)
    w0 = p_group / (1.0 + ex)
    w1 = p_group * ex / (1.0 + ex)

    oh0 = (lane == e0).astype(f32)
    oh1 = (lane == e1).astype(f32)
    both = (oh0 + oh1).astype(bf16)
    where, counts = [], []
    for s in range(n_sub):
        tile_both = both[s * tj:(s + 1) * tj, :]
        ranks = jnp.dot(tri, tile_both, preferred_element_type=f32)
        n_e = ranks[tj - 1:tj, :] + tile_both[tj - 1:tj, :].astype(f32)
        n_e8 = jnp.broadcast_to(n_e, (SUBLANES, LANES))
        tile_off = jnp.dot(n_e8.astype(bf16), upper, preferred_element_type=f32)[0:1, :]
        where.append(ranks + tile_off)
        counts.append(n_e8)
    where = jnp.concatenate(where, axis=0)
    pos0 = jnp.sum(oh0 * where, axis=1, keepdims=True)
    pos1 = jnp.sum(oh1 * where, axis=1, keepdims=True)
    cols = jnp.where(lane == 0, w0, jnp.where(lane == 1, w1,
                     jnp.where(lane == 2, pos0, jnp.where(lane == 3, pos1, 0.0))))
    vec_ref[...] = cols
    base = carry_ref[...]
    for s in range(n_sub):
        pos_ref[s] = cols[s * tj:(s + 1) * tj, :].T[:SUBLANES, :].astype(jnp.int32)
        long_seg = (jnp.max(counts[s], axis=1, keepdims=True) >= SEG_SPLIT).astype(f32)
        tinfo_ref[s] = jnp.where(sub == 0, counts[s], jnp.where(sub == 1, base, jnp.where(sub == 2, long_seg, 0.0))
                                 ).astype(jnp.int32)
        base = base + counts[s]
    carry_ref[...] = base
    cnt_ref[...] = base.astype(jnp.int32)


def _outproj_router(yh, yf, x, mix_g, wo, norm2_g, wr, br, C):
    B, L, D = x.shape
    Lh = L // 2
    tj = _tile(256, Lh)
    rt = ROW_TILES_PER_STEP
    ns = 2 * rt
    nj = Lh // (rt * tj)
    T = B * L
    ntiles = T // tj
    lin = lambda b, j: b * nj + j
    full = lambda a: pl.BlockSpec(a.shape, lambda b, j: (0,) * a.ndim)
    mg = mix_g.reshape(1, D)
    g2 = norm2_g.reshape(1, D)
    return pl.pallas_call(
        functools.partial(_outproj_router_kernel, C=C),
        grid=(B, nj),
        in_specs=[pl.BlockSpec((2, rt * tj, C), lambda b, j: (0, j, b)),
                  pl.BlockSpec((2, rt * tj, C), lambda b, j: (0, j, b)),
                  pl.BlockSpec((1, ns * tj, D), lambda b, j: (b, j, 0)),
                  full(mg), full(wo), full(g2), full(wr), full(br)],
        out_specs=[pl.BlockSpec((ns * tj, D), lambda b, j: (lin(b, j), 0)),
                   pl.BlockSpec((ns * tj, D), lambda b, j: (lin(b, j), 0)),
                   pl.BlockSpec((ns, SUBLANES, tj), lambda b, j: (lin(b, j), 0, 0)),
                   pl.BlockSpec((ns * tj, LANES), lambda b, j: (lin(b, j), 0)),
                   pl.BlockSpec((ns, SUBLANES, LANES), lambda b, j: (lin(b, j), 0, 0)),
                   pl.BlockSpec((SUBLANES, LANES), lambda b, j: (0, 0))],
        out_shape=[jax.ShapeDtypeStruct((T, D), f32),
                   jax.ShapeDtypeStruct((T, D), bf16),
                   jax.ShapeDtypeStruct((ntiles, SUBLANES, tj), jnp.int32),
                   jax.ShapeDtypeStruct((T, LANES), f32),
                   jax.ShapeDtypeStruct((ntiles, SUBLANES, LANES), jnp.int32),
                   jax.ShapeDtypeStruct((SUBLANES, LANES), jnp.int32)],
        scratch_shapes=[pltpu.VMEM((SUBLANES, LANES), f32), pltpu.VMEM((D // LANES, ns * tj, LANES), f32)],
        compiler_params=_cparams(("arbitrary", "arbitrary")),
        name="outproj_router",
    )(yh, yf, x, mg, wo, g2, wr, br)


def _padded(c):
    return ((c + MOE_ROWS - 1) // MOE_ROWS) * MOE_ROWS


def _pad_starts(cnt_ref, start_ref):
    def body(e, acc):
        start_ref[e] = acc
        return acc + _padded(cnt_ref[0, e])
    return lax.fori_loop(0, N_EXPERTS, body, jnp.int32(0))


def _segment_copies(n, src, src_off, dst, dst_off, sem, max_rows, wait, src_advances=True, min_rows=1):
    bit = max_rows
    while bit >= min_rows:
        take = n & bit

        @pl.when(take != 0)
        def _(src_off=src_off, dst_off=dst_off, bit=bit):
            cp = pltpu.make_async_copy(src.at[pl.ds(src_off, bit)], dst.at[pl.ds(dst_off, bit)], sem)
            if wait:
                cp.wait()
            else:
                cp.start(priority=ROW_DMA_PRIORITY)
        if src_advances:
            src_off = src_off + take
        dst_off = dst_off + take
        bit //= 2


def _tile_segments(tinfo_ref, t, start_ref, local, local_is_src, remote, sem, tj, part="both"):
    def copies(long_part):
        off = 0
        for e in range(N_EXPERTS):
            n = tinfo_ref[t, 0, e]
            far = start_ref[e] + tinfo_ref[t, 1, e]
            if long_part:
                args = (n, off, far, tj, SEG_SPLIT)
            else:
                head = n & ~(SEG_SPLIT - 1)
                args = (n, off + head, far + head, SEG_SPLIT // 2, 1)
            n_, near_, far_, hi, lo = args
            if local_is_src:
                _segment_copies(n_, local, near_, remote, far_, sem, hi, wait=False, min_rows=lo)
            else:
                _segment_copies(n_, remote, far_, local, near_, sem, hi, wait=False, min_rows=lo)
            off = off + n

    if part in ("both", "short"):
        copies(long_part=False)
    if part in ("both", "long"):
        @pl.when(tinfo_ref[t, 2, 0] != 0)
        def _():
            copies(long_part=True)


def _dispatch_kernel(tinfo_ref, cnt_ref, tok_ref, pos_ref, xs_ref, first_ref, nblk_ref,
                     sbufs, zbuf, start_ref, sems, zsem):
    tj = pos_ref.shape[2]
    n_blocks = xs_ref.shape[0] // MOE_ROWS

    @pl.when(pl.program_id(0) == 0)
    def _():
        total = _pad_starts(cnt_ref, start_ref)
        used = total // MOE_ROWS

        def block_ranges(e, c):
            first_ref[e] = start_ref[e] // MOE_ROWS
            nblk_ref[e] = _padded(cnt_ref[0, e]) // MOE_ROWS
            return c
        lax.fori_loop(0, N_EXPERTS, block_ranges, 0)

        zbuf[...] = jnp.zeros_like(zbuf)
        zrows = zbuf.shape[0]
        for wait in (False, True):
            def zero_pad(e, c, wait=wait):
                cnt = cnt_ref[0, e]
                _segment_copies(_padded(cnt) - cnt, zbuf, 0, xs_ref, start_ref[e] + cnt, zsem,
                                zrows, wait, src_advances=False)
                return c
            lax.fori_loop(0, N_EXPERTS, zero_pad, 0)

            def zero_tail(i, c, wait=wait):
                cp = pltpu.make_async_copy(zbuf, xs_ref.at[pl.ds(total + i * zrows, zrows)], zsem)
                if wait:
                    cp.wait()
                else:
                    cp.start()
                return c
            lax.fori_loop(0, (n_blocks - used) * (MOE_ROWS // zrows), zero_tail, 0)

    def wait_tile(k):
        pltpu.make_async_copy(sbufs[k], xs_ref.at[pl.ds(0, 2 * tj)], sems.at[k]).wait()

    slot = lax.broadcasted_iota(jnp.int32, (2 * tj, tj), 0)
    for k in range(TILES_PER_STEP):
        @pl.when(pl.program_id(0) > 0)
        def _(k=k):
            wait_tile(k)
        perm = ((slot == pos_ref[k, 2:3, :]) | (slot == pos_ref[k, 3:4, :])).astype(bf16)
        sorted_rows = jnp.dot(perm, tok_ref[k * tj:(k + 1) * tj, :], preferred_element_type=f32)
        _store_row_tiles(sbufs[k], _pack_bf16_pairs(sorted_rows, is_bf16_exact=True))
        if k > 0:
            _tile_segments(tinfo_ref, k - 1, start_ref, sbufs[k - 1], True, xs_ref, sems.at[k - 1], tj)
    last = TILES_PER_STEP - 1
    _tile_segments(tinfo_ref, last, start_ref, sbufs[last], True, xs_ref, sems.at[last], tj)

    @pl.when(pl.program_id(0) == pl.num_programs(0) - 1)
    def _():
        for k in range(TILES_PER_STEP):
            wait_tile(k)


def _dispatch(tinfo, cnt, tok, pos, n_rows):
    ntiles, _, tj = pos.shape
    T, D = tok.shape
    n_blocks = n_rows // MOE_ROWS
    nt = TILES_PER_STEP
    assert ntiles % nt == 0
    sbuf = pltpu.VMEM(_row_tiles_shape(2 * tj, D), jnp.uint32)
    return pl.pallas_call(
        _dispatch_kernel,
        grid=(ntiles // nt,),
        in_specs=[pl.BlockSpec((nt, SUBLANES, LANES), lambda i: (i, 0, 0), memory_space=pltpu.SMEM),
                  pl.BlockSpec(memory_space=pltpu.SMEM),
                  pl.BlockSpec((nt * tj, D), lambda i: (i, 0)),
                  pl.BlockSpec((nt, SUBLANES, tj), lambda i: (i, 0, 0))],
        out_specs=[pl.BlockSpec(memory_space=pl.ANY),
                   pl.BlockSpec(memory_space=pltpu.SMEM),
                   pl.BlockSpec(memory_space=pltpu.SMEM)],
        out_shape=[jax.ShapeDtypeStruct(_row_tiles_shape(n_rows, D), jnp.uint32),
                   jax.ShapeDtypeStruct((N_EXPERTS,), jnp.int32),
                   jax.ShapeDtypeStruct((N_EXPERTS,), jnp.int32)],
        scratch_shapes=[[sbuf] * nt,
                        pltpu.VMEM(_row_tiles_shape(MOE_ROWS // 2, D), jnp.uint32),
                        pltpu.SMEM((N_EXPERTS,), jnp.int32), pltpu.SemaphoreType.DMA((nt,)),
                        pltpu.SemaphoreType.DMA],
        compiler_params=_cparams(("arbitrary",)),
        name="moe_dispatch",
    )(tinfo, cnt, tok, pos)


def _experts_kernel(first_ref, nblk_ref, xs_ref, wg_ref, wu_ref, wd_ref, ys_ref,
                    wg_s, wu_s, wd_s, xbuf, ybuf, in_sems, out_sems):
    e = pl.program_id(0)
    last_e = pl.num_programs(0) - 1
    rows = MOE_ROWS
    n_total = xs_ref.shape[0] // rows
    used = first_ref[last_e] + nblk_ref[last_e]

    def fetch(g, slot):
        return pltpu.make_async_copy(xs_ref.at[pl.ds(g * rows, rows)], xbuf.at[pl.ds(slot * rows, rows)],
                                     in_sems.at[slot])

    def flush(g, slot):
        return pltpu.make_async_copy(ybuf.at[pl.ds(slot * rows, rows)], ys_ref.at[pl.ds(g * rows, rows)],
                                     out_sems.at[slot])

    for ahead in range(FETCH_AHEAD):
        @pl.when((e == 0) & (used > ahead))
        def _(ahead=ahead):
            fetch(ahead, ahead).start(priority=ROW_DMA_PRIORITY)

    @pl.when(nblk_ref[e] > 0)
    def _():
        wg_s[...] = wg_ref[...].astype(bf16)
        wu_s[...] = wu_ref[...].astype(bf16)
        wd_s[...] = wd_ref[...].astype(bf16)

    def block(b, carry):
        g = first_ref[e] + b
        slot = g % 2
        in_slot = g % (FETCH_AHEAD + 1)
        fetch(g, in_slot).wait()

        @pl.when(g + FETCH_AHEAD < used)
        def _():
            fetch(g + FETCH_AHEAD, (g + FETCH_AHEAD) % (FETCH_AHEAD + 1)).start(priority=ROW_DMA_PRIORITY)

        @pl.when(g >= 2)
        def _():
            flush(g - 2, slot).wait()
        half = rows // EXPERT_ROW_GROUPS
        starts = [slot * rows + r * half for r in range(EXPERT_ROW_GROUPS)]
        gated = []
        for r in range(EXPERT_ROW_GROUPS):
            xb = _unpack_bf16_pairs(_load_row_tiles(xbuf, in_slot * rows + r * half, half)).astype(bf16)
            gated.append((jnp.dot(xb, wg_s[...], preferred_element_type=f32),
                          jnp.dot(xb, wu_s[...], preferred_element_type=f32)))
        for row0, (a, u) in zip(starts, gated):
            h = (a * jax.nn.sigmoid(a) * u).astype(bf16)
            _store_row_tiles(ybuf, _pack_bf16_pairs(jnp.dot(h, wd_s[...], preferred_element_type=f32)), row0)
        flush(g, slot).start(priority=ROW_DMA_PRIORITY)
        return carry
    lax.fori_loop(0, nblk_ref[e], block, 0)

    @pl.when(e == last_e)
    def _():
        for back in (1, 2):
            @pl.when(used >= back)
            def _(back=back):
                flush(used - back, (used - back) % 2).wait()
        ybuf[pl.ds(0, rows)] = jnp.zeros((rows,) + ybuf.shape[1:], ybuf.dtype)
        for wait in (False, True):
            def zero_tail(g, c, wait=wait):
                cp = flush(g, 0)
                if wait:
                    cp.wait()
                else:
                    cp.start()
                return c
            lax.fori_loop(used, n_total, zero_tail, 0)


def _experts(first_block, n_blocks, xs, w_gate, w_up, w_down):
    _, D, De = w_gate.shape
    widx = lambda e, first, nblk: (e, 0, 0)
    in_buf = pltpu.VMEM(_row_tiles_shape((FETCH_AHEAD + 1) * MOE_ROWS, D), jnp.uint32)
    out_buf = pltpu.VMEM(_row_tiles_shape(2 * MOE_ROWS, D), jnp.uint32)
    return pl.pallas_call(
        _experts_kernel,
        grid_spec=pltpu.PrefetchScalarGridSpec(
            num_scalar_prefetch=2,
            grid=(N_EXPERTS,),
            in_specs=[pl.BlockSpec(memory_space=pl.ANY),
                      pl.BlockSpec((None, D, De), widx),
                      pl.BlockSpec((None, D, De), widx),
                      pl.BlockSpec((None, De, D), widx)],
            out_specs=pl.BlockSpec(memory_space=pl.ANY),
            scratch_shapes=[pltpu.VMEM((D, De), bf16), pltpu.VMEM((D, De), bf16), pltpu.VMEM((De, D), bf16),
                            in_buf, out_buf, pltpu.SemaphoreType.DMA((FETCH_AHEAD + 1,)),
                            pltpu.SemaphoreType.DMA((2,))],
        ),
        out_shape=jax.ShapeDtypeStruct(xs.shape, jnp.uint32),
        compiler_params=_cparams(("arbitrary",)),
        name="moe_experts",
    )(first_block, n_blocks, xs, w_gate, w_up, w_down)


def _combine_kernel(tinfo_ref, tnext_ref, cnt_ref, h2_ref, vec_ref, g_ref, ys_ref, o_ref,
                    gbufs, stages, start_ref, sems):
    tj = vec_ref.shape[0] // TILES_PER_STEP
    step = pl.program_id(0)

    def fetch(info_ref, t, buf, part="both"):
        _tile_segments(info_ref, t, start_ref, gbufs[buf], False, ys_ref, sems.at[buf], tj, part)

    @pl.when(step == 0)
    def _():
        _pad_starts(cnt_ref, start_ref)
        for t in range(GATHER_AHEAD):
            fetch(tinfo_ref, t, t)

    slot = lax.broadcasted_iota(jnp.int32, (tj, 2 * tj), 1)

    def finish(k, picked):
        v = vec_ref[k * tj:(k + 1) * tj, :]
        y = h2_ref[k * tj:(k + 1) * tj, :] + v[:, 0:1] * picked[:tj] + v[:, 1:2] * picked[tj:]
        out = _rms(y, g_ref[...])
        stage = stages[k // 2]
        for q in range(stage.shape[0]):
            stage[q, pl.ds(k % 2, tj, stride=2), :] = out[:, q * LANES:(q + 1) * LANES]
        if k % 2 == 1:
            r0 = (k // 2) * 2 * tj
            for q in range(stage.shape[0]):
                o_ref[0, r0:r0 + 2 * tj, q * LANES:(q + 1) * LANES] = stage[q]

    picked = None
    for k in range(TILES_PER_STEP):
        pltpu.make_async_copy(ys_ref.at[pl.ds(0, 2 * tj)], gbufs[k], sems.at[k]).wait()
        ahead = k + GATHER_AHEAD
        if ahead < TILES_PER_STEP:
            later = (tinfo_ref, ahead, ahead)
        else:
            later = (tnext_ref, ahead - TILES_PER_STEP, ahead - TILES_PER_STEP)
        fetch(*later, part="short")
        rows = _unpack_bf16_pairs(_load_row_tiles(gbufs[k])).astype(bf16)
        v = vec_ref[k * tj:(k + 1) * tj, :]
        pick = jnp.concatenate([(slot == v[:, 2 + c:3 + c].astype(jnp.int32)).astype(bf16) for c in range(2)],
                               axis=0)
        now = jnp.dot(pick, rows, preferred_element_type=f32)
        if k > 0:
            finish(k - 1, picked)
        picked = now
        fetch(*later, part="long")
    finish(TILES_PER_STEP - 1, picked)


def _combine(tinfo, cnt, h2, vec, final_g, ys, B, L):
    T, D = h2.shape
    ntiles = tinfo.shape[0]
    tj = T // ntiles
    nt = TILES_PER_STEP
    assert GATHER_AHEAD < nt and nt % 2 == 0 and (L // tj) % nt == 0
    nsteps = ntiles // nt
    per_seq = L // (nt * tj)
    g = final_g.reshape(1, D)
    gbuf = pltpu.VMEM(_row_tiles_shape(2 * tj, D), jnp.uint32)
    return pl.pallas_call(
        _combine_kernel,
        grid=(nsteps,),
        in_specs=[pl.BlockSpec((nt, SUBLANES, LANES), lambda i: (i, 0, 0), memory_space=pltpu.SMEM),
                  pl.BlockSpec((nt, SUBLANES, LANES), lambda i: (i + 1, 0, 0), memory_space=pltpu.SMEM),
                  pl.BlockSpec(memory_space=pltpu.SMEM),
                  pl.BlockSpec((nt * tj, D), lambda i: (i, 0)),
                  pl.BlockSpec((nt * tj, LANES), lambda i: (i, 0)),
                  pl.BlockSpec((1, D), lambda i: (0, 0)),
                  pl.BlockSpec(memory_space=pl.ANY)],
        out_specs=pl.BlockSpec((1, nt * tj, D), lambda i: (i // per_seq, i % per_seq, 0)),
        out_shape=jax.ShapeDtypeStruct((B, L, D), f32),
        scratch_shapes=[[gbuf] * nt,
                        [pltpu.VMEM((D // LANES, 2 * tj, LANES), f32)] * (nt // 2),
                        pltpu.SMEM((N_EXPERTS,), jnp.int32), pltpu.SemaphoreType.DMA((nt,))],
        compiler_params=_cparams(("arbitrary",)),
        name="moe_combine",
    )(tinfo, jnp.pad(tinfo, ((0, nt), (0, 0), (0, 0))), cnt, h2, vec, g, ys)


def kernel(x, norm1_g, w_in, conv_w, conv_b, f_w_in, f_b_in, f_w_mid, f_b_mid, f_freq, f_w_out, f_bias,
           mix_g, w_out, norm2_g, w_group, b_group, w_router, b_router, w_gate, w_up, w_down, final_g):
    B, L, D = x.shape
    depth = norm1_g.shape[0]
    assert depth == 1, "the final RMSNorm is fused into the single layer's MoE combine"
    C = D // 2
    Lh = L // 2
    N = B * C
    T = B * L
    n_rows = T * 2 + N_EXPERTS * MOE_ROWS
    cfou = _fourier_consts(L)
    i = 0
    wcat = _prep_win(w_in[i], C)
    hcat, asum = _hyena_filters(L, C, f_w_in[i], f_b_in[i], f_w_mid[i], f_b_mid[i], f_freq[i], f_w_out[i])
    z, x0, u4, cfwd, cinv = _inproj(x, norm1_g[i], wcat, conv_w[i], conv_b[i], C)
    kspec, k0 = _filter_spectrum(cfwd, hcat, asum, L, C)
    z = z.reshape(2, Lh, N)
    x0 = x0.reshape(2, Lh, N)
    u4 = u4.reshape(4, Lh, N)
    s = _conv_fwd(cfwd, z, kspec, k0, C)
    yh = _conv_inv(cinv, s, x0, z, f_bias[i], C)
    yf = _fourier_dft(cfou, u4, L, C)
    wr = jnp.concatenate([jnp.transpose(w_router[i], (1, 0, 2)).reshape(D, N_EXPERTS), w_group[i]], axis=1)
    wr = jnp.pad(wr, ((0, 0), (0, LANES - wr.shape[1])))
    br = jnp.pad(jnp.concatenate([b_router[i].reshape(-1), b_group[i]]), (0, LANES - N_EXPERTS - N_GROUPS))
    h2, tok, pos, vec, tinfo, cnt = _outproj_router(yh, yf, x, mix_g[i], w_out[i].astype(bf16), norm2_g[i],
                                                     wr, br.reshape(1, LANES), C)
    xs, first_block, n_blocks = _dispatch(tinfo, cnt, tok, pos, n_rows)
    ys = _experts(first_block, n_blocks, xs, w_gate[i], w_up[i], w_down[i])
    return _combine(tinfo, cnt, h2, vec, final_g, ys, B, L)
```

```python
import functools
import math

import jax
import jax.numpy as jnp
from jax import lax
from jax.experimental import pallas as pl
from jax.experimental.pallas import tpu as pltpu

HEAD_DIM = 64
N_GROUPS = 4
EXPERTS_PER_GROUP = 8
N_EXPERTS = N_GROUPS * EXPERTS_PER_GROUP
FILTER_BANDS = 16
DECAY_FAST_PCT = 0.3
DECAY_SLOW_PCT = 1.5
DECAY_TARGET = 1e-2
EPS = 1e-6

LANES = 128
SUBLANES = 8
MXU_DIM = 256
GEN_ROWS = 256
TOKEN_TILE = 256
CONV_ROWS = 512
CONV_ROW_GROUPS = 2
CONV_COLS = 512
FOURIER_COLS = 256
MOE_ROWS = 512
EXPERT_ROW_GROUPS = 2
FETCH_AHEAD = 3
ROW_DMA_PRIORITY = 1
ROW_TILES_PER_STEP = 2
TILES_PER_STEP = 4
GATHER_AHEAD = 3
SEG_SPLIT = 32
VMEM_LIMIT = 56 * 1024 * 1024

f32 = jnp.float32
bf16 = jnp.bfloat16
HI = lax.Precision.HIGHEST


def _cparams(sem):
    return pltpu.CompilerParams(dimension_semantics=sem, vmem_limit_bytes=VMEM_LIMIT)


def _tile(pref, n):
    return min(pref, n)


_SIN_TERMS = tuple((-1.0) ** k / math.factorial(2 * k + 1) for k in range(7))
_COS_TERMS = tuple((-1.0) ** k / math.factorial(2 * k) for k in range(8))


def _cos_sin_of_turn_fraction(prod, m):
    num = prod & (m - 1)
    quadrant = num // (m // 4)
    x = (num & (m // 4 - 1)).astype(f32) * (2.0 * math.pi / m)
    x2 = x * x
    s = jnp.full_like(x, _SIN_TERMS[-1])
    for c in _SIN_TERMS[-2::-1]:
        s = s * x2 + c
    s = s * x
    c_ = jnp.full_like(x, _COS_TERMS[-1])
    for c in _COS_TERMS[-2::-1]:
        c_ = c_ * x2 + c
    odd = (quadrant & 1) == 1
    cos = jnp.where(odd, s, c_)
    sin = jnp.where(odd, c_, s)
    cos = jnp.where((quadrant == 1) | (quadrant == 2), -cos, cos)
    sin = jnp.where(quadrant >= 2, -sin, sin)
    return cos, sin


def _cos_sin_blocks(a, s, b0, m, ncb, d):
    tc, ts = _cos_sin_of_turn_fraction(a * (s * d), m)
    b0_lane = jnp.zeros(d.shape, jnp.int32)
    for cb in range(ncb):
        b0_lane = jnp.where(d == cb, b0(cb), b0_lane)
    ac, asn = _cos_sin_of_turn_fraction(a * b0_lane, m)
    for cb in range(ncb):
        ca = ac[:, cb:cb + 1]
        sa = asn[:, cb:cb + 1]
        yield cb, ca * tc - sa * ts, sa * tc + ca * ts


def _write_dft_rows(fwd_ref, inv_ref, row_block, L):
    Lh = L // 2
    tr = fwd_ref.shape[1]
    r = lax.broadcasted_iota(jnp.int32, (tr, LANES), 0) + row_block * tr
    d = lax.broadcasted_iota(jnp.int32, (tr, LANES), 1)
    alt_r = (1 - 2 * (r & 1)).astype(f32)
    alt_d = (1 - 2 * (d & 1)).astype(f32)
    ncb = Lh // LANES
    fams = [
        (r, 1, lambda cb: cb * LANES, L),
        (r, 2, lambda cb: 2 * cb * LANES + 1, 2 * L),
        (2 * r + 1, 1, lambda cb: cb * LANES, 2 * L),
    ]
    for fi, (a, s, b0, m) in enumerate(fams):
        for cb, c, sn in _cos_sin_blocks(a, s, b0, m, ncb, d):
            sl = slice(cb * LANES, (cb + 1) * LANES)
            sl_sin = slice(Lh + cb * LANES, Lh + (cb + 1) * LANES)
            if fi == 0:
                fwd_ref[0, :, sl] = c.astype(bf16)
                inv_ref[0, :, sl] = c.astype(bf16)
                fwd_ref[1, :, sl] = jnp.where(r == 0, alt_d, sn).astype(bf16)
                nst = -sn
                if cb == 0:
                    nst = jnp.where(d == 0, -alt_r, nst)
                inv_ref[0, :, sl_sin] = nst.astype(bf16)
            elif fi == 1:
                fwd_ref[2, :, sl] = c.astype(bf16)
                fwd_ref[3, :, sl] = jnp.where(r == 0, alt_d, sn).astype(bf16)
            else:
                inv_ref[1, :, sl] = c.astype(bf16)
                nst = -sn
                if cb == 0:
                    nst = jnp.where(d == 0, -alt_r, nst)
                inv_ref[1, :, sl_sin] = nst.astype(bf16)


def _fourier_consts_kernel(o_ref, *, L):
    n = L // 4
    tr = o_ref.shape[0]
    r = lax.broadcasted_iota(jnp.int32, (tr, LANES), 0) + pl.program_id(0) * tr
    d = lax.broadcasted_iota(jnp.int32, (tr, LANES), 1)

    def col_l(cb):
        parity, j0 = divmod(cb * LANES, n // 2)
        return 2 * j0 + parity

    for cb, c, sn in _cos_sin_blocks(r, 2, col_l, n, n // LANES, d):
        o_ref[:, cb * LANES:(cb + 1) * LANES] = c.astype(bf16)
        o_ref[:, n + cb * LANES:n + (cb + 1) * LANES] = (-sn).astype(bf16)


def _fourier_consts(L):
    n = L // 4
    tr = _tile(GEN_ROWS, n)
    return pl.pallas_call(
        functools.partial(_fourier_consts_kernel, L=L),
        grid=(n // tr,),
        out_specs=pl.BlockSpec((tr, 2 * n), lambda i: (i, 0)),
        out_shape=jax.ShapeDtypeStruct((n, 2 * n), bf16),
        compiler_params=_cparams(("parallel",)),
        name="fourier_consts",
    )()


def _prep_win_kernel(w_ref, o_ref, *, C):
    w = w_ref[...]
    i = lax.broadcasted_iota(jnp.int32, (LANES, LANES), 0)
    j = lax.broadcasted_iota(jnp.int32, (LANES, LANES), 1)
    same = (i // HEAD_DIM) == (j // HEAD_DIM)
    cos, sin = _cos_sin_of_turn_fraction((i % HEAD_DIM) * (j % HEAD_DIM), HEAD_DIM)
    bc = jnp.where(same, cos, 0.0)
    bs = jnp.where(same, sin, 0.0)
    o_ref[:, :3 * C] = w[:, :3 * C].astype(bf16)
    for k in range(0, C, LANES):
        wf = w[:, 3 * C + k:3 * C + k + LANES]
        o_ref[:, 3 * C + k:3 * C + k + LANES] = jnp.dot(wf, bc, precision=HI, preferred_element_type=f32).astype(bf16)
        o_ref[:, 4 * C + k:4 * C + k + LANES] = jnp.dot(wf, bs, precision=HI, preferred_element_type=f32).astype(bf16)


def _prep_win(w_in, C):
    D = w_in.shape[0]
    tr = _tile(GEN_ROWS, D)
    return pl.pallas_call(
        functools.partial(_prep_win_kernel, C=C),
        grid=(D // tr,),
        in_specs=[pl.BlockSpec((tr, 4 * C), lambda i: (i, 0))],
        out_specs=pl.BlockSpec((tr, 5 * C), lambda i: (i, 0)),
        out_shape=jax.ShapeDtypeStruct((D, 5 * C), bf16),
        compiler_params=_cparams(("parallel",)),
        name="prep_win",
    )(w_in)


def _dot_split(a, b):
    a_hi = a.astype(bf16)
    b_hi = b.astype(bf16)
    a_lo = (a - a_hi.astype(f32)).astype(bf16)
    b_lo = (b - b_hi.astype(f32)).astype(bf16)
    return (jnp.dot(a_hi, b_hi, preferred_element_type=f32) + jnp.dot(a_lo, b_hi, preferred_element_type=f32)
            + jnp.dot(a_hi, b_lo, preferred_element_type=f32))


def _write_filter_rows(win_ref, bin_ref, wmid_ref, bmid_ref, freq_ref, wout_ref, h_ref, asum_ref, step, L, C):
    tr = h_ref.shape[1]
    lane = lax.broadcasted_iota(jnp.int32, (tr, LANES), 1)
    row = lax.broadcasted_iota(jnp.int32, (tr, LANES), 0) + step * tr
    band_i = jnp.where(lane <= FILTER_BANDS, lane - 1, lane - 1 - FILTER_BANDS)
    band = 1e-4 + band_i.astype(f32) * ((FILTER_BANDS - 1 - 1e-4) / (FILTER_BANDS - 1))
    ch = lax.broadcasted_iota(jnp.int32, (tr, 2 * C), 1) % C
    max_decay = math.log(DECAY_TARGET) / DECAY_FAST_PCT
    min_decay = math.log(DECAY_TARGET) / DECAY_SLOW_PCT
    delta = jnp.abs(min_decay + ch.astype(f32) * ((max_decay - min_decay) / (C - 1)))
    is_bwd = lax.broadcasted_iota(jnp.int32, (tr, 2 * C), 1) >= C
    row_c = lax.broadcasted_iota(jnp.int32, (tr, 2 * C), 0) + step * tr

    @pl.when(step == 0)
    def _():
        asum_ref[...] = jnp.zeros_like(asum_ref)

    total = jnp.zeros((1, 2 * C), f32)
    for p in range(2):
        pos = (2 * row + p).astype(f32)
        t = pos * (1.0 / (L - 1))
        w = pos * (2.0 * math.pi / L)
        fw = band * w
        z = jnp.where(lane == 0, t,
                      jnp.where(lane <= FILTER_BANDS, jnp.cos(fw),
                                jnp.where(lane <= 2 * FILTER_BANDS, -jnp.sin(fw), 0.0)))
        h = jnp.sin(freq_ref[0:1, :] * (_dot_split(z, win_ref[...]) + bin_ref[...]))
        for i in range(wmid_ref.shape[0]):
            h = jnp.sin(freq_ref[i + 1:i + 2, :] * (_dot_split(h, wmid_ref[i]) + bmid_ref[i:i + 1, :]))
        out = _dot_split(h, wout_ref[...])
        tc = (2 * row_c + p).astype(f32) * (1.0 / (L - 1))
        out = out * jnp.exp(-tc * delta)
        if p == 0:
            out = jnp.where(is_bwd & (row_c == 0), 0.0, out)
        h_ref[p] = out.astype(bf16)
        total = total + jnp.sum(jnp.abs(out), axis=0, keepdims=True)
    asum_ref[0:1, :] += total


def _filter_kernel(win_ref, bin_ref, wmid_ref, bmid_ref, freq_ref, wout_ref, h_ref, asum_ref, *, L, C):
    _write_filter_rows(win_ref, bin_ref, wmid_ref, bmid_ref, freq_ref, wout_ref, h_ref, asum_ref,
                       pl.program_id(0), L, C)


def _hyena_filters(L, C, f_w_in, f_b_in, f_w_mid, f_b_mid, f_freq, f_w_out):
    Lh = L // 2
    tr = _tile(GEN_ROWS, Lh)
    order = f_w_in.shape[1]
    win = jnp.pad(f_w_in, ((0, LANES - f_w_in.shape[0]), (0, 0)))
    full = lambda a: pl.BlockSpec(a.shape, lambda i: (0,) * a.ndim)
    args = (win, f_b_in.reshape(1, order), f_w_mid, f_b_mid, f_freq, f_w_out)
    return pl.pallas_call(
        functools.partial(_filter_kernel, L=L, C=C),
        grid=(Lh // tr,),
        in_specs=[full(a) for a in args],
        out_specs=[pl.BlockSpec((2, tr, 2 * C), lambda i: (0, i, 0)),
                   pl.BlockSpec((SUBLANES, 2 * C), lambda i: (0, 0))],
        out_shape=[jax.ShapeDtypeStruct((2, Lh, 2 * C), bf16),
                   jax.ShapeDtypeStruct((SUBLANES, 2 * C), f32)],
        compiler_params=_cparams(("arbitrary",)),
        name="hyena_filters",
    )(*args)


def _fwd_products(c_ref, x_ref, rows=slice(None)):
    k = c_ref.shape[2]
    even = c_ref[0:2, rows, :]
    odd = c_ref[2:4, rows, :]
    tm = even.shape[1]
    even = jnp.dot(even.reshape(2 * tm, k), x_ref[0], preferred_element_type=f32)
    odd = jnp.dot(odd.reshape(2 * tm, k), x_ref[1], preferred_element_type=f32)
    return even[:tm], even[tm:], odd[:tm], odd[tm:]


def _filter_spectrum_kernel(c_ref, h_ref, asum_ref, k_ref, k0_ref, *, L, C):
    ce, se, co, so = _fwd_products(c_ref, h_ref)
    a = asum_ref[0:1, :]
    scale = 1.0 / ((a[:, :C] + a[:, C:]) * L)
    F = lambda v: v[:, :C]
    G = lambda v: v[:, C:]
    k_ref[0] = (F(ce) + F(co) + G(ce) + G(co)) * scale
    k_ref[1] = (-(F(se) + F(so)) + (G(se) + G(so))) * scale
    k_ref[2] = (F(ce) - F(co) + G(ce) - G(co)) * scale
    k_ref[3] = ((F(se) - F(so)) - (G(se) - G(so))) * scale

    @pl.when(pl.program_id(0) == 0)
    def _():
        k0_ref[...] = jnp.zeros_like(k0_ref)
        k0_ref[0:1, :] = ((F(se) + G(se)) * scale)[0:1, :]
        k0_ref[1:2, :] = ((F(so) - G(so)) * scale)[0:1, :]


def _filter_spectrum(cfwd, hcat, asum, L, C):
    Lh = L // 2
    tm = _tile(CONV_ROWS, Lh)
    return pl.pallas_call(
        functools.partial(_filter_spectrum_kernel, L=L, C=C),
        grid=(Lh // tm,),
        in_specs=[pl.BlockSpec((4, tm, Lh), lambda i: (0, i, 0)),
                  pl.BlockSpec((2, Lh, 2 * C), lambda i: (0, 0, 0)),
                  pl.BlockSpec((SUBLANES, 2 * C), lambda i: (0, 0))],
        out_specs=[pl.BlockSpec((4, tm, C), lambda i: (0, i, 0)),
                   pl.BlockSpec((SUBLANES, C), lambda i: (0, 0))],
        out_shape=[jax.ShapeDtypeStruct((4, Lh, C), f32),
                   jax.ShapeDtypeStruct((SUBLANES, C), f32)],
        compiler_params=_cparams(("arbitrary",)),
        name="filter_spectrum",
    )(cfwd, hcat, asum)


def _conv_fwd_kernel(c_ref, z_ref, k_ref, k0_ref, s_ref):
    C = k_ref.shape[2]
    tm = c_ref.shape[1]
    tg = tm // CONV_ROW_GROUPS
    groups = [slice(r * tg, (r + 1) * tg) for r in range(CONV_ROW_GROUPS)]
    products = [_fwd_products(c_ref, z_ref, rows) for rows in groups]
    is_row0 = (lax.broadcasted_iota(jnp.int32, (SUBLANES, C), 0) == 0) & (pl.program_id(1) == 0)
    for rows, prod in zip(groups, products):
        for h in range(z_ref.shape[2] // C):
            cols = slice(h * C, (h + 1) * C)
            ce, se, co, so = (v[:, cols] for v in prod)
            zsr, zsi, zdr, zdi = ce + co, -(se + so), ce - co, se - so
            ksr, ksi, kdr, kdi = (k_ref[i, rows, :] for i in range(4))
            psr = zsr * ksr - zsi * ksi
            psi = zsr * ksi + zsi * ksr
            pdr = zdr * kdr - zdi * kdi
            pdi = zdr * kdi + zdi * kdr
            gen = [psr + pdr, psi - pdi, psr - pdr, psi + pdi]
            if rows.start == 0:
                top = slice(0, SUBLANES)
                p0 = zsr[top] * ksr[top]
                pl_ = zdr[top] * kdr[top]
                ak, bk = k0_ref[0:1, :], k0_ref[1:2, :]
                phr = se[top] * ak - so[top] * bk
                phi = -(se[top] * bk + so[top] * ak)
                spec = (0.5 * (p0 + pl_), -phr, 0.5 * (p0 - pl_), phi)
                gen = [jnp.concatenate([jnp.where(is_row0, spec[i], gen[i][top]), gen[i][SUBLANES:]], axis=0)
                       for i in range(4)]
            for i in range(4):
                s_ref[i, rows, cols] = gen[i].astype(bf16)


def _conv_fwd(cfwd, z, kspec, k0, C):
    _, Lh, N = z.shape
    tm = _tile(CONV_ROWS, Lh)
    tn = _tile(CONV_COLS, N)
    return pl.pallas_call(
        _conv_fwd_kernel,
        grid=(N // tn, Lh // tm),
        in_specs=[pl.BlockSpec((4, tm, Lh), lambda n, i: (0, i, 0)),
                  pl.BlockSpec((2, Lh, tn), lambda n, i: (0, 0, n)),
                  pl.BlockSpec((4, tm, C), lambda n, i: (0, i, 0)),
                  pl.BlockSpec((SUBLANES, C), lambda n, i: (0, 0))],
        out_specs=pl.BlockSpec((4, tm, tn), lambda n, i: (0, i, n)),
        out_shape=jax.ShapeDtypeStruct((4, Lh, N), bf16),
        compiler_params=_cparams(("parallel", "arbitrary")),
        name="conv_fwd",
    )(cfwd, z, kspec, k0)


def _pair_products(c_ref, r_ref):
    k, tn = r_ref.shape[1:]
    ev = jnp.dot(c_ref[0], r_ref[0:2].reshape(2 * k, tn), preferred_element_type=f32)
    od = jnp.dot(c_ref[1], r_ref[2:4].reshape(2 * k, tn), preferred_element_type=f32)
    return ev, od


def _conv_inv_kernel(c_ref, s_ref, x0_ref, z_ref, fb_ref, y_ref):
    conv = _pair_products(c_ref, s_ref)
    C = fb_ref.shape[1]
    fb = fb_ref[...]
    for p in range(2):
        for h in range(z_ref.shape[2] // C):
            cols = slice(h * C, (h + 1) * C)
            zp = z_ref[p, :, cols].astype(f32)
            y_ref[p, :, cols] = (x0_ref[p, :, cols].astype(f32) * (conv[p][:, cols] + zp * fb)).astype(bf16)


def _conv_inv(cinv, s, x0, z, f_bias, C):
    _, Lh, N = s.shape
    tm = _tile(CONV_ROWS, Lh)
    tn = _tile(CONV_COLS, N)
    return pl.pallas_call(
        _conv_inv_kernel,
        grid=(N // tn, Lh // tm),
        in_specs=[pl.BlockSpec((2, tm, 2 * Lh), lambda n, i: (0, i, 0)),
                  pl.BlockSpec((4, Lh, tn), lambda n, i: (0, 0, n)),
                  pl.BlockSpec((2, tm, tn), lambda n, i: (0, i, n)),
                  pl.BlockSpec((2, tm, tn), lambda n, i: (0, i, n)),
                  pl.BlockSpec((1, C), lambda n, i: (0, 0))],
        out_specs=pl.BlockSpec((2, tm, tn), lambda n, i: (0, i, n)),
        out_shape=jax.ShapeDtypeStruct((2, Lh, N), bf16),
        compiler_params=_cparams(("parallel", "arbitrary")),
        name="conv_inv",
    )(cinv, s, x0, z, f_bias.reshape(1, C))


def _fourier_kernel(c_ref, u_ref, y_ref, rhs, stage, tw1, tw2, *, L, scale):
    Lh, n, H = L // 2, L // 4, L // 8
    tn = u_ref.shape[2]

    @pl.when(pl.program_id(0) == 0)
    def _():
        r1 = lax.broadcasted_iota(jnp.int32, (Lh, LANES), 0)
        tw1[0], tw1[1] = _cos_sin_of_turn_fraction(2 * (r1 % n) + r1 // n, L)
        r2 = lax.broadcasted_iota(jnp.int32, (n, LANES), 0)
        tw2[0], tw2[1] = _cos_sin_of_turn_fraction(2 * (r2 % H) + r2 // H, Lh)

    def rotate(a, b, c, s):
        return c * a - s * b, s * a + c * b

    for q in range(tn // LANES):
        cols = slice(q * LANES, (q + 1) * LANES)
        for p in range(2):
            first = slice(p * n, p * n + H)
            second = slice(p * n + H, (p + 1) * n)
            out = slice(p * H, (p + 1) * H)
            out_b = slice(n + p * H, n + (p + 1) * H)
            c2, s2 = tw2[0, out, :], tw2[1, out, :]

            def emit(sub, a0, b0, a1, b1):
                rhs[sub, out, cols] = (a0 + a1).astype(bf16)
                rhs[sub, out_b, cols] = (b0 + b1).astype(bf16)
                da, db = rotate(a0 - a1, b0 - b1, c2, s2)
                rhs[sub + 1, out, cols] = da.astype(bf16)
                rhs[sub + 1, out_b, cols] = db.astype(bf16)

            ld = lambda k, rows: u_ref[k, rows, cols].astype(f32)
            emit(0, ld(0, first), ld(1, first), ld(0, second), ld(1, second))
            odd = [rotate(ld(2, rows), ld(3, rows), tw1[0, rows, :], tw1[1, rows, :]) for rows in (first, second)]
            emit(2, odd[0][0], odd[0][1], odd[1][0], odd[1][1])

    for parity in range(2):
        pair = [jnp.dot(c_ref[...], rhs[2 * parity + h], preferred_element_type=f32) * scale for h in range(2)]
        for q in range(tn // LANES):
            cols = slice(q * LANES, (q + 1) * LANES)
            for h in range(2):
                stage[q, pl.ds(h, n, stride=2), :] = pair[h][:, cols]
            y_ref[parity, :, cols] = stage[q].astype(bf16)


def _fourier_dft(cfou, u4, L, C):
    _, Lh, N = u4.shape
    n = L // 4
    tn = _tile(FOURIER_COLS, C)
    return pl.pallas_call(
        functools.partial(_fourier_kernel, L=L, scale=1.0 / math.sqrt(L * HEAD_DIM)),
        grid=(N // tn,),
        in_specs=[pl.BlockSpec((n, 2 * n), lambda j: (0, 0)),
                  pl.BlockSpec((4, Lh, tn), lambda j: (0, 0, j))],
        out_specs=pl.BlockSpec((2, Lh, tn), lambda j: (0, 0, j)),
        out_shape=jax.ShapeDtypeStruct((2, Lh, N), bf16),
        scratch_shapes=[pltpu.VMEM((4, 2 * n, tn), bf16), pltpu.VMEM((tn // LANES, Lh, LANES), f32),
                        pltpu.VMEM((2, Lh, LANES), f32), pltpu.VMEM((2, n, LANES), f32)],
        compiler_params=_cparams(("arbitrary",)),
        name="fourier_dft",
    )(cfou, u4)


def _rms(x, g):
    return x * lax.rsqrt(jnp.mean(x * x, axis=-1, keepdims=True) + EPS) * g


def _stage_rows(chunks, x):
    for k in range(chunks.shape[0]):
        chunks[k] = x[:, k * LANES:(k + 1) * LANES]


def _parity_rows(chunks, p, n):
    return jnp.concatenate([chunks[k, pl.ds(p, n, stride=2), :] for k in range(chunks.shape[0])], axis=1)


def _inproj_kernel(xa_ref, xb_ref, pa_ref, na_ref, pb_ref, nb_ref, g_ref, w_ref, cw_ref, cb_ref,
                   z_ref, x0_ref, u_ref, cfwd_ref, cinv_ref, stage, *, C, D, L):
    tj = xa_ref.shape[1] // 2
    jt = pl.program_id(1)
    n_half = pl.num_programs(1)
    _write_dft_rows(cfwd_ref, cinv_ref, pl.program_id(0) * n_half + jt, L)
    g = g_ref[...]
    w = w_ref[...]
    cw0, cw1, cw2 = cw_ref[0:1, :], cw_ref[1:2, :], cw_ref[2:3, :]
    cb = cb_ref[...]
    row = lax.broadcasted_iota(jnp.int32, (tj, 3 * C), 0)
    four = []
    halves = ((xa_ref, pa_ref, na_ref, jt == 0, False),
              (xb_ref, pb_ref, nb_ref, False, jt == n_half - 1))
    for h, (x_ref, p_ref, n_ref, at_start, at_end) in enumerate(halves):
        _stage_rows(stage, x_ref[0])
        lhs = jnp.concatenate([
            _rms(_parity_rows(stage, 0, tj), g).astype(bf16),
            _rms(_parity_rows(stage, 1, tj), g).astype(bf16),
            _rms(p_ref[0], g).astype(bf16),
            _rms(n_ref[0], g).astype(bf16)], axis=0)
        res = jnp.dot(lhs, w, preferred_element_type=f32)
        pe, po = res[:tj, :3 * C], res[tj:2 * tj, :3 * C]
        prev = res[2 * tj + SUBLANES - 1:2 * tj + SUBLANES, :3 * C]
        nxt = res[2 * tj + SUBLANES:2 * tj + SUBLANES + 1, :3 * C]
        if at_start is not False:
            prev = jnp.where(at_start, 0.0, prev)
        if at_end is not False:
            nxt = jnp.where(at_end, 0.0, nxt)
        po_dn = jnp.where(row == 0, prev, pltpu.roll(po, 1, 0))
        pe_up = jnp.where(row == tj - 1, nxt, pltpu.roll(pe, tj - 1, 0))
        uc = (cb + cw0 * po_dn + cw1 * pe + cw2 * po,
              cb + cw0 * pe + cw1 * po + cw2 * pe_up)
        for p in range(2):
            x0_ref[p, h] = uc[p][:, :C].astype(bf16)
            z_ref[p, h] = (uc[p][:, C:2 * C] * uc[p][:, 2 * C:]).astype(bf16)
        four.append((res[:tj, 3 * C:], res[tj:2 * tj, 3 * C:]))
    for p in range(2):
        fa, fb = four[0][p], four[1][p]
        u_ref[0, p] = (fa[:, :C] + fb[:, :C]).astype(bf16)
        u_ref[1, p] = (fa[:, C:] + fb[:, C:]).astype(bf16)
        u_ref[2, p] = (fa[:, :C] - fb[:, :C]).astype(bf16)
        u_ref[3, p] = (fa[:, C:] - fb[:, C:]).astype(bf16)


def _inproj(x, norm_g, wcat, conv_w, conv_b, C):
    B, L, D = x.shape
    Lq = L // 4
    tj = _tile(TOKEN_TILE, Lq)
    nq = Lq // tj
    N = B * C
    hb = 2 * tj // SUBLANES
    last = L // SUBLANES - 1
    main = lambda off: pl.BlockSpec((1, 2 * tj, D), lambda b, j: (b, j + off, 0))
    prev = lambda off: pl.BlockSpec((1, SUBLANES, D), lambda b, j: (b, jnp.maximum((j + off) * hb - 1, 0), 0))
    nxt = lambda off: pl.BlockSpec((1, SUBLANES, D), lambda b, j: (b, jnp.minimum((j + off + 1) * hb, last), 0))
    full = lambda a: pl.BlockSpec(a.shape, lambda b, j: (0,) * a.ndim)
    g = norm_g.reshape(1, D)
    cb = conv_b.reshape(1, 3 * C)
    Lh = L // 2
    dft_rows = Lh // (B * nq)
    assert dft_rows * B * nq == Lh and dft_rows % (2 * SUBLANES) == 0
    dft_spec = pl.BlockSpec((4, dft_rows, Lh), lambda b, j: (0, b * nq + j, 0))
    dft_shape = jax.ShapeDtypeStruct((4, Lh, Lh), bf16)
    inv_spec = pl.BlockSpec((2, dft_rows, 2 * Lh), lambda b, j: (0, b * nq + j, 0))
    inv_shape = jax.ShapeDtypeStruct((2, Lh, 2 * Lh), bf16)
    return pl.pallas_call(
        functools.partial(_inproj_kernel, C=C, D=D, L=L),
        grid=(B, nq),
        in_specs=[main(0), main(nq), prev(0), nxt(0), prev(nq), nxt(nq),
                  full(g), full(wcat), full(conv_w), full(cb)],
        out_specs=[pl.BlockSpec((2, 2, tj, C), lambda b, j: (0, 0, j, b)),
                   pl.BlockSpec((2, 2, tj, C), lambda b, j: (0, 0, j, b)),
                   pl.BlockSpec((4, 2, tj, C), lambda b, j: (0, 0, j, b)),
                   dft_spec, inv_spec],
        out_shape=[jax.ShapeDtypeStruct((2, 2, Lq, N), bf16),
                   jax.ShapeDtypeStruct((2, 2, Lq, N), bf16),
                   jax.ShapeDtypeStruct((4, 2, Lq, N), bf16),
                   dft_shape, inv_shape],
        scratch_shapes=[pltpu.VMEM((D // LANES, 2 * tj, LANES), f32)],
        compiler_params=_cparams(("parallel", "arbitrary")),
        name="inproj",
    )(x, x, x, x, x, x, g, wcat, conv_w, cb)


def _pack_bf16_pairs(v, is_bf16_exact=False):
    k = v.shape[1] // 2
    rounded = v if is_bf16_exact else v.astype(bf16).astype(f32)
    hi = pltpu.bitcast(rounded[:, :k], jnp.uint32)
    lo = pltpu.bitcast(rounded[:, k:], jnp.uint32)
    return hi | (lo >> 16)


def _unpack_bf16_pairs(u):
    hi = pltpu.bitcast(u & jnp.uint32(0xFFFF0000), f32)
    lo = pltpu.bitcast(u << 16, f32)
    return jnp.concatenate([hi, lo], axis=1)


def _row_tiles_shape(n, D):
    return (n, D // (2 * LANES), LANES)


def _store_row_tiles(ref, packed, row0=0):
    total, per_row, _ = ref.shape
    n = packed.shape[0]
    flat = ref.reshape(total * per_row, LANES)
    for s in range(per_row):
        flat[pl.ds(row0 * per_row + s, n, stride=per_row), :] = packed[:, s * LANES:(s + 1) * LANES]


def _load_row_tiles(ref, row0=0, n=None):
    total, per_row, _ = ref.shape
    n = total if n is None else n
    flat = ref.reshape(total * per_row, LANES)
    return jnp.concatenate([flat[pl.ds(row0 * per_row + s, n, stride=per_row), :] for s in range(per_row)],
                           axis=1)


def _outproj_router_kernel(yh_ref, yf_ref, x_ref, mg_ref, wo_ref, g2_ref, wr_ref, br_ref,
                           h2_ref, tok_ref, pos_ref, vec_ref, tinfo_ref, cnt_ref, carry_ref, stage, *, C):
    n_sub = pos_ref.shape[0]
    tj = pos_ref.shape[2]
    _stage_rows(stage, x_ref[0])

    @pl.when((pl.program_id(0) == 0) & (pl.program_id(1) == 0))
    def _():
        carry_ref[...] = jnp.zeros_like(carry_ref)

    li = lax.broadcasted_iota(jnp.int32, (LANES, LANES), 0)
    lj = lax.broadcasted_iota(jnp.int32, (LANES, LANES), 1)
    hw = _tile(MXU_DIM, C)
    ai = lax.broadcasted_iota(jnp.int32, (hw, hw), 0)
    aj = lax.broadcasted_iota(jnp.int32, (hw, hw), 1)
    avg = jnp.where(ai // HEAD_DIM == aj // HEAD_DIM, 1.0 / HEAD_DIM, 0.0).astype(bf16)
    upper = (li < lj).astype(bf16)
    ri = lax.broadcasted_iota(jnp.int32, (tj, tj), 0)
    ci = lax.broadcasted_iota(jnp.int32, (tj, tj), 1)
    tri = (ci < ri).astype(bf16)
    wr = wr_ref[...]
    wr_hi = wr.astype(bf16)
    wr_lo = (wr - wr_hi.astype(f32)).astype(bf16)
    wr_both = jnp.concatenate([wr_hi, wr_lo], axis=1)
    lane = lax.broadcasted_iota(jnp.int32, (n_sub * tj, LANES), 1).astype(f32)
    sub = lax.broadcasted_iota(jnp.int32, (SUBLANES, LANES), 0)
    neg = jnp.float32(-jnp.inf)
    big = jnp.float32(LANES)

    def first_argmax(vals, mask):
        v = jnp.where(mask, vals, neg)
        m = jnp.max(v, axis=1, keepdims=True)
        idx = jnp.min(jnp.where(mask & (v == m), lane, big), axis=1, keepdims=True)
        return m, idx

    normed = []
    for s in range(n_sub):
        r, p = divmod(s, 2)
        parts = []
        for i, ref in enumerate((yh_ref, yf_ref)):
            y = ref[p, r * tj:(r + 1) * tj, :]
            ysq = y * y
            ms = jnp.concatenate([jnp.dot(ysq[:, k:k + hw], avg, preferred_element_type=f32)
                                  for k in range(0, C, hw)], axis=1)
            scale = lax.rsqrt(ms + EPS) * mg_ref[:, i * C:(i + 1) * C]
            parts.append(y * scale.astype(bf16))
        normed.append(jnp.concatenate(parts, axis=1))
    proj = jnp.dot(jnp.concatenate(normed, axis=0), wo_ref[...], preferred_element_type=f32)
    t_hi, t_lo = [], []
    for s in range(n_sub):
        r, p = divmod(s, 2)
        rows = slice(s * tj, (s + 1) * tj)
        h2 = _parity_rows(stage, r * 2 * tj + p, tj) + proj[rows]
        h2_ref[rows, :] = h2
        tok = _rms(h2, g2_ref[...])
        hi = tok.astype(bf16)
        tok_ref[rows, :] = hi
        t_hi.append(hi)
        t_lo.append((tok - hi.astype(f32)).astype(bf16))
    hi_terms = jnp.dot(jnp.concatenate(t_hi, axis=0), wr_both, preferred_element_type=f32)
    logits = (hi_terms[:, :LANES] + hi_terms[:, LANES:]
              + jnp.dot(jnp.concatenate(t_lo, axis=0), wr_hi, preferred_element_type=f32) + br_ref[...])

    gmask = (lane >= N_EXPERTS) & (lane < N_EXPERTS + N_GROUPS)
    gmax, gidx = first_argmax(logits, gmask)
    gsel = gidx - N_EXPERTS
    p_group = 1.0 / jnp.sum(jnp.where(gmask, jnp.exp(logits - gmax), 0.0), axis=1, keepdims=True)
    emask = (lane >= gsel * EXPERTS_PER_GROUP) & (lane < (gsel + 1) * EXPERTS_PER_GROUP)
    l0, e0 = first_argmax(logits, emask)
    l1, e1 = first_argmax(logits, emask & (lane != e0))
    ex = jnp.exp(l1 - l0)
    w0 = p_group / (1.0 + ex)
    w1 = p_group * ex / (1.0 + ex)

    oh0 = (lane == e0).astype(f32)
    oh1 = (lane == e1).astype(f32)
    both = (oh0 + oh1).astype(bf16)
    where, counts = [], []
    for s in range(n_sub):
        tile_both = both[s * tj:(s + 1) * tj, :]
        ranks = jnp.dot(tri, tile_both, preferred_element_type=f32)
        n_e = ranks[tj - 1:tj, :] + tile_both[tj - 1:tj, :].astype(f32)
        n_e8 = jnp.broadcast_to(n_e, (SUBLANES, LANES))
        tile_off = jnp.dot(n_e8.astype(bf16), upper, preferred_element_type=f32)[0:1, :]
        where.append(ranks + tile_off)
        counts.append(n_e8)
    where = jnp.concatenate(where, axis=0)
    pos0 = jnp.sum(oh0 * where, axis=1, keepdims=True)
    pos1 = jnp.sum(oh1 * where, axis=1, keepdims=True)
    cols = jnp.where(lane == 0, w0, jnp.where(lane == 1, w1,
                     jnp.where(lane == 2, pos0, jnp.where(lane == 3, pos1, 0.0))))
    vec_ref[...] = cols
    base = carry_ref[...]
    for s in range(n_sub):
        pos_ref[s] = cols[s * tj:(s + 1) * tj, :].T[:SUBLANES, :].astype(jnp.int32)
        long_seg = (jnp.max(counts[s], axis=1, keepdims=True) >= SEG_SPLIT).astype(f32)
        tinfo_ref[s] = jnp.where(sub == 0, counts[s], jnp.where(sub == 1, base, jnp.where(sub == 2, long_seg, 0.0))
                                 ).astype(jnp.int32)
        base = base + counts[s]
    carry_ref[...] = base
    cnt_ref[...] = base.astype(jnp.int32)


def _outproj_router(yh, yf, x, mix_g, wo, norm2_g, wr, br, C):
    B, L, D = x.shape
    Lh = L // 2
    tj = _tile(TOKEN_TILE, Lh)
    rt = ROW_TILES_PER_STEP
    ns = 2 * rt
    nj = Lh // (rt * tj)
    T = B * L
    ntiles = T // tj
    lin = lambda b, j: b * nj + j
    full = lambda a: pl.BlockSpec(a.shape, lambda b, j: (0,) * a.ndim)
    mg = mix_g.reshape(1, D)
    g2 = norm2_g.reshape(1, D)
    return pl.pallas_call(
        functools.partial(_outproj_router_kernel, C=C),
        grid=(B, nj),
        in_specs=[pl.BlockSpec((2, rt * tj, C), lambda b, j: (0, j, b)),
                  pl.BlockSpec((2, rt * tj, C), lambda b, j: (0, j, b)),
                  pl.BlockSpec((1, ns * tj, D), lambda b, j: (b, j, 0)),
                  full(mg), full(wo), full(g2), full(wr), full(br)],
        out_specs=[pl.BlockSpec((ns * tj, D), lambda b, j: (lin(b, j), 0)),
                   pl.BlockSpec((ns * tj, D), lambda b, j: (lin(b, j), 0)),
                   pl.BlockSpec((ns, SUBLANES, tj), lambda b, j: (lin(b, j), 0, 0)),
                   pl.BlockSpec((ns * tj, LANES), lambda b, j: (lin(b, j), 0)),
                   pl.BlockSpec((ns, SUBLANES, LANES), lambda b, j: (lin(b, j), 0, 0)),
                   pl.BlockSpec((SUBLANES, LANES), lambda b, j: (0, 0))],
        out_shape=[jax.ShapeDtypeStruct((T, D), f32),
                   jax.ShapeDtypeStruct((T, D), bf16),
                   jax.ShapeDtypeStruct((ntiles, SUBLANES, tj), jnp.int32),
                   jax.ShapeDtypeStruct((T, LANES), f32),
                   jax.ShapeDtypeStruct((ntiles, SUBLANES, LANES), jnp.int32),
                   jax.ShapeDtypeStruct((SUBLANES, LANES), jnp.int32)],
        scratch_shapes=[pltpu.VMEM((SUBLANES, LANES), f32), pltpu.VMEM((D // LANES, ns * tj, LANES), f32)],
        compiler_params=_cparams(("arbitrary", "arbitrary")),
        name="outproj_router",
    )(yh, yf, x, mg, wo, g2, wr, br)


def _padded(c):
    return ((c + MOE_ROWS - 1) // MOE_ROWS) * MOE_ROWS


def _pad_starts(cnt_ref, start_ref):
    def body(e, acc):
        start_ref[e] = acc
        return acc + _padded(cnt_ref[0, e])
    return lax.fori_loop(0, N_EXPERTS, body, jnp.int32(0))


def _segment_copies(n, src, src_off, dst, dst_off, sem, max_rows, wait, src_advances=True, min_rows=1):
    bit = max_rows
    while bit >= min_rows:
        take = n & bit

        @pl.when(take != 0)
        def _(src_off=src_off, dst_off=dst_off, bit=bit):
            cp = pltpu.make_async_copy(src.at[pl.ds(src_off, bit)], dst.at[pl.ds(dst_off, bit)], sem)
            if wait:
                cp.wait()
            else:
                cp.start(priority=ROW_DMA_PRIORITY)
        if src_advances:
            src_off = src_off + take
        dst_off = dst_off + take
        bit //= 2


def _tile_segments(tinfo_ref, t, start_ref, local, local_is_src, remote, sem, tj, part="both"):
    def copies(long_part):
        off = 0
        for e in range(N_EXPERTS):
            n = tinfo_ref[t, 0, e]
            far = start_ref[e] + tinfo_ref[t, 1, e]
            if long_part:
                args = (n, off, far, tj, SEG_SPLIT)
            else:
                head = n & ~(SEG_SPLIT - 1)
                args = (n, off + head, far + head, SEG_SPLIT // 2, 1)
            n_, near_, far_, hi, lo = args
            if local_is_src:
                _segment_copies(n_, local, near_, remote, far_, sem, hi, wait=False, min_rows=lo)
            else:
                _segment_copies(n_, remote, far_, local, near_, sem, hi, wait=False, min_rows=lo)
            off = off + n

    if part in ("both", "short"):
        copies(long_part=False)
    if part in ("both", "long"):
        @pl.when(tinfo_ref[t, 2, 0] != 0)
        def _():
            copies(long_part=True)


def _dispatch_kernel(tinfo_ref, cnt_ref, tok_ref, pos_ref, xs_ref, first_ref, nblk_ref,
                     sbufs, zbuf, start_ref, sems, zsem):
    tj = pos_ref.shape[2]
    n_blocks = xs_ref.shape[0] // MOE_ROWS

    @pl.when(pl.program_id(0) == 0)
    def _():
        total = _pad_starts(cnt_ref, start_ref)
        used = total // MOE_ROWS

        def block_ranges(e, c):
            first_ref[e] = start_ref[e] // MOE_ROWS
            nblk_ref[e] = _padded(cnt_ref[0, e]) // MOE_ROWS
            return c
        lax.fori_loop(0, N_EXPERTS, block_ranges, 0)

        zbuf[...] = jnp.zeros_like(zbuf)
        zrows = zbuf.shape[0]
        for wait in (False, True):
            def zero_pad(e, c, wait=wait):
                cnt = cnt_ref[0, e]
                _segment_copies(_padded(cnt) - cnt, zbuf, 0, xs_ref, start_ref[e] + cnt, zsem,
                                zrows, wait, src_advances=False)
                return c
            lax.fori_loop(0, N_EXPERTS, zero_pad, 0)

            def zero_tail(i, c, wait=wait):
                cp = pltpu.make_async_copy(zbuf, xs_ref.at[pl.ds(total + i * zrows, zrows)], zsem)
                if wait:
                    cp.wait()
                else:
                    cp.start()
                return c
            lax.fori_loop(0, (n_blocks - used) * (MOE_ROWS // zrows), zero_tail, 0)

    def wait_tile(k):
        pltpu.make_async_copy(sbufs[k], xs_ref.at[pl.ds(0, 2 * tj)], sems.at[k]).wait()

    def segments(k, part):
        _tile_segments(tinfo_ref, k, start_ref, sbufs[k], True, xs_ref, sems.at[k], tj, part)

    slot = lax.broadcasted_iota(jnp.int32, (2 * tj, tj), 0)
    last = TILES_PER_STEP - 1

    @pl.when(pl.program_id(0) > 0)
    def _():
        for k in range(last):
            wait_tile(k)
    for k in range(TILES_PER_STEP):
        if k == last:
            @pl.when(pl.program_id(0) > 0)
            def _():
                wait_tile(last)
        perm = ((slot == pos_ref[k, 2:3, :]) | (slot == pos_ref[k, 3:4, :])).astype(bf16)
        sorted_rows = jnp.dot(perm, tok_ref[k * tj:(k + 1) * tj, :], preferred_element_type=f32)
        _store_row_tiles(sbufs[k], _pack_bf16_pairs(sorted_rows, is_bf16_exact=True))
        if k > 0:
            segments(k - 1, "short")
    segments(last, "short")
    for k in range(TILES_PER_STEP):
        segments(k, "long")

    @pl.when(pl.program_id(0) == pl.num_programs(0) - 1)
    def _():
        for k in range(TILES_PER_STEP):
            wait_tile(k)


def _dispatch(tinfo, cnt, tok, pos, n_rows):
    ntiles, _, tj = pos.shape
    T, D = tok.shape
    n_blocks = n_rows // MOE_ROWS
    nt = TILES_PER_STEP
    assert ntiles % nt == 0
    sbuf = pltpu.VMEM(_row_tiles_shape(2 * tj, D), jnp.uint32)
    return pl.pallas_call(
        _dispatch_kernel,
        grid=(ntiles // nt,),
        in_specs=[pl.BlockSpec((nt, SUBLANES, LANES), lambda i: (i, 0, 0), memory_space=pltpu.SMEM),
                  pl.BlockSpec(memory_space=pltpu.SMEM),
                  pl.BlockSpec((nt * tj, D), lambda i: (i, 0)),
                  pl.BlockSpec((nt, SUBLANES, tj), lambda i: (i, 0, 0))],
        out_specs=[pl.BlockSpec(memory_space=pl.ANY),
                   pl.BlockSpec(memory_space=pltpu.SMEM),
                   pl.BlockSpec(memory_space=pltpu.SMEM)],
        out_shape=[jax.ShapeDtypeStruct(_row_tiles_shape(n_rows, D), jnp.uint32),
                   jax.ShapeDtypeStruct((N_EXPERTS,), jnp.int32),
                   jax.ShapeDtypeStruct((N_EXPERTS,), jnp.int32)],
        scratch_shapes=[[sbuf] * nt,
                        pltpu.VMEM(_row_tiles_shape(MOE_ROWS // 2, D), jnp.uint32),
                        pltpu.SMEM((N_EXPERTS,), jnp.int32), pltpu.SemaphoreType.DMA((nt,)),
                        pltpu.SemaphoreType.DMA],
        compiler_params=_cparams(("arbitrary",)),
        name="moe_dispatch",
    )(tinfo, cnt, tok, pos)


def _experts_kernel(first_ref, nblk_ref, xs_ref, wg_ref, wu_ref, wd_ref, ys_ref,
                    wg_s, wu_s, wd_s, xbuf, ybuf, in_sems, out_sems):
    e = pl.program_id(0)
    last_e = pl.num_programs(0) - 1
    rows = MOE_ROWS
    n_total = xs_ref.shape[0] // rows
    used = first_ref[last_e] + nblk_ref[last_e]

    def fetch(g, slot):
        return pltpu.make_async_copy(xs_ref.at[pl.ds(g * rows, rows)], xbuf.at[pl.ds(slot * rows, rows)],
                                     in_sems.at[slot])

    def flush(g, slot):
        return pltpu.make_async_copy(ybuf.at[pl.ds(slot * rows, rows)], ys_ref.at[pl.ds(g * rows, rows)],
                                     out_sems.at[slot])

    for ahead in range(FETCH_AHEAD):
        @pl.when((e == 0) & (used > ahead))
        def _(ahead=ahead):
            fetch(ahead, ahead).start(priority=ROW_DMA_PRIORITY)

    @pl.when(nblk_ref[e] > 0)
    def _():
        wg_s[...] = wg_ref[...].astype(bf16)
        wu_s[...] = wu_ref[...].astype(bf16)
        wd_s[...] = wd_ref[...].astype(bf16)

    def block(b, carry):
        g = first_ref[e] + b
        slot = g % 2
        in_slot = g % (FETCH_AHEAD + 1)
        fetch(g, in_slot).wait()

        @pl.when(g + FETCH_AHEAD < used)
        def _():
            fetch(g + FETCH_AHEAD, (g + FETCH_AHEAD) % (FETCH_AHEAD + 1)).start(priority=ROW_DMA_PRIORITY)

        @pl.when(g >= 2)
        def _():
            flush(g - 2, slot).wait()
        half = rows // EXPERT_ROW_GROUPS
        starts = [slot * rows + r * half for r in range(EXPERT_ROW_GROUPS)]
        gated = []
        for r in range(EXPERT_ROW_GROUPS):
            xb = _unpack_bf16_pairs(_load_row_tiles(xbuf, in_slot * rows + r * half, half)).astype(bf16)
            gated.append((jnp.dot(xb, wg_s[...], preferred_element_type=f32),
                          jnp.dot(xb, wu_s[...], preferred_element_type=f32)))
        for row0, (a, u) in zip(starts, gated):
            h = (a * jax.nn.sigmoid(a) * u).astype(bf16)
            _store_row_tiles(ybuf, _pack_bf16_pairs(jnp.dot(h, wd_s[...], preferred_element_type=f32)), row0)
        flush(g, slot).start(priority=ROW_DMA_PRIORITY)
        return carry
    lax.fori_loop(0, nblk_ref[e], block, 0)

    @pl.when(e == last_e)
    def _():
        for back in (1, 2):
            @pl.when(used >= back)
            def _(back=back):
                flush(used - back, (used - back) % 2).wait()
        ybuf[pl.ds(0, rows)] = jnp.zeros((rows,) + ybuf.shape[1:], ybuf.dtype)
        for wait in (False, True):
            def zero_tail(g, c, wait=wait):
                cp = flush(g, 0)
                if wait:
                    cp.wait()
                else:
                    cp.start()
                return c
            lax.fori_loop(used, n_total, zero_tail, 0)


def _experts(first_block, n_blocks, xs, w_gate, w_up, w_down):
    _, D, De = w_gate.shape
    widx = lambda e, first, nblk: (e, 0, 0)
    in_buf = pltpu.VMEM(_row_tiles_shape((FETCH_AHEAD + 1) * MOE_ROWS, D), jnp.uint32)
    out_buf = pltpu.VMEM(_row_tiles_shape(2 * MOE_ROWS, D), jnp.uint32)
    return pl.pallas_call(
        _experts_kernel,
        grid_spec=pltpu.PrefetchScalarGridSpec(
            num_scalar_prefetch=2,
            grid=(N_EXPERTS,),
            in_specs=[pl.BlockSpec(memory_space=pl.ANY),
                      pl.BlockSpec((None, D, De), widx),
                      pl.BlockSpec((None, D, De), widx),
                      pl.BlockSpec((None, De, D), widx)],
            out_specs=pl.BlockSpec(memory_space=pl.ANY),
            scratch_shapes=[pltpu.VMEM((D, De), bf16), pltpu.VMEM((D, De), bf16), pltpu.VMEM((De, D), bf16),
                            in_buf, out_buf, pltpu.SemaphoreType.DMA((FETCH_AHEAD + 1,)),
                            pltpu.SemaphoreType.DMA((2,))],
        ),
        out_shape=jax.ShapeDtypeStruct(xs.shape, jnp.uint32),
        compiler_params=_cparams(("arbitrary",)),
        name="moe_experts",
    )(first_block, n_blocks, xs, w_gate, w_up, w_down)


def _combine_kernel(tinfo_ref, tnext_ref, cnt_ref, h2_ref, vec_ref, g_ref, ys_ref, o_ref,
                    gbufs, stages, start_ref, sems):
    tj = vec_ref.shape[0] // TILES_PER_STEP
    step = pl.program_id(0)

    def fetch(info_ref, t, buf, part="both"):
        _tile_segments(info_ref, t, start_ref, gbufs[buf], False, ys_ref, sems.at[buf], tj, part)

    @pl.when(step == 0)
    def _():
        _pad_starts(cnt_ref, start_ref)
        for t in range(GATHER_AHEAD):
            fetch(tinfo_ref, t, t)

    slot = lax.broadcasted_iota(jnp.int32, (tj, 2 * tj), 1)

    def finish(k, picked):
        v = vec_ref[k * tj:(k + 1) * tj, :]
        y = h2_ref[k * tj:(k + 1) * tj, :] + v[:, 0:1] * picked[:tj] + v[:, 1:2] * picked[tj:]
        out = _rms(y, g_ref[...])
        stage = stages[k // 2]
        for q in range(stage.shape[0]):
            stage[q, pl.ds(k % 2, tj, stride=2), :] = out[:, q * LANES:(q + 1) * LANES]
        if k % 2 == 1:
            r0 = (k // 2) * 2 * tj
            for q in range(stage.shape[0]):
                o_ref[0, r0:r0 + 2 * tj, q * LANES:(q + 1) * LANES] = stage[q]

    picked = None
    for k in range(TILES_PER_STEP):
        pltpu.make_async_copy(ys_ref.at[pl.ds(0, 2 * tj)], gbufs[k], sems.at[k]).wait()
        ahead = k + GATHER_AHEAD
        if ahead < TILES_PER_STEP:
            later = (tinfo_ref, ahead, ahead)
        else:
            later = (tnext_ref, ahead - TILES_PER_STEP, ahead - TILES_PER_STEP)
        fetch(*later, part="short")
        rows = _unpack_bf16_pairs(_load_row_tiles(gbufs[k])).astype(bf16)
        v = vec_ref[k * tj:(k + 1) * tj, :]
        pick = jnp.concatenate([(slot == v[:, 2 + c:3 + c].astype(jnp.int32)).astype(bf16) for c in range(2)],
                               axis=0)
        now = jnp.dot(pick, rows, preferred_element_type=f32)
        if k > 0:
            finish(k - 1, picked)
        picked = now
        fetch(*later, part="long")
    finish(TILES_PER_STEP - 1, picked)


def _combine(tinfo, cnt, h2, vec, final_g, ys, B, L):
    T, D = h2.shape
    ntiles = tinfo.shape[0]
    tj = T // ntiles
    nt = TILES_PER_STEP
    assert GATHER_AHEAD < nt and nt % 2 == 0 and (L // tj) % nt == 0
    nsteps = ntiles // nt
    per_seq = L // (nt * tj)
    g = final_g.reshape(1, D)
    gbuf = pltpu.VMEM(_row_tiles_shape(2 * tj, D), jnp.uint32)
    return pl.pallas_call(
        _combine_kernel,
        grid=(nsteps,),
        in_specs=[pl.BlockSpec((nt, SUBLANES, LANES), lambda i: (i, 0, 0), memory_space=pltpu.SMEM),
                  pl.BlockSpec((nt, SUBLANES, LANES), lambda i: (i + 1, 0, 0), memory_space=pltpu.SMEM),
                  pl.BlockSpec(memory_space=pltpu.SMEM),
                  pl.BlockSpec((nt * tj, D), lambda i: (i, 0)),
                  pl.BlockSpec((nt * tj, LANES), lambda i: (i, 0)),
                  pl.BlockSpec((1, D), lambda i: (0, 0)),
                  pl.BlockSpec(memory_space=pl.ANY)],
        out_specs=pl.BlockSpec((1, nt * tj, D), lambda i: (i // per_seq, i % per_seq, 0)),
        out_shape=jax.ShapeDtypeStruct((B, L, D), f32),
        scratch_shapes=[[gbuf] * nt,
                        [pltpu.VMEM((D // LANES, 2 * tj, LANES), f32)] * (nt // 2),
                        pltpu.SMEM((N_EXPERTS,), jnp.int32), pltpu.SemaphoreType.DMA((nt,))],
        compiler_params=_cparams(("arbitrary",)),
        name="moe_combine",
    )(tinfo, jnp.pad(tinfo, ((0, nt), (0, 0), (0, 0))), cnt, h2, vec, g, ys)


def kernel(x, norm1_g, w_in, conv_w, conv_b, f_w_in, f_b_in, f_w_mid, f_b_mid, f_freq, f_w_out, f_bias,
           mix_g, w_out, norm2_g, w_group, b_group, w_router, b_router, w_gate, w_up, w_down, final_g):
    B, L, D = x.shape
    depth = norm1_g.shape[0]
    assert depth == 1, "the final RMSNorm is fused into the single layer's MoE combine"
    C = D // 2
    Lh = L // 2
    N = B * C
    T = B * L
    n_rows = T * 2 + N_EXPERTS * MOE_ROWS
    cfou = _fourier_consts(L)
    i = 0
    wcat = _prep_win(w_in[i], C)
    hcat, asum = _hyena_filters(L, C, f_w_in[i], f_b_in[i], f_w_mid[i], f_b_mid[i], f_freq[i], f_w_out[i])
    z, x0, u4, cfwd, cinv = _inproj(x, norm1_g[i], wcat, conv_w[i], conv_b[i], C)
    kspec, k0 = _filter_spectrum(cfwd, hcat, asum, L, C)
    z = z.reshape(2, Lh, N)
    x0 = x0.reshape(2, Lh, N)
    u4 = u4.reshape(4, Lh, N)
    s = _conv_fwd(cfwd, z, kspec, k0, C)
    yh = _conv_inv(cinv, s, x0, z, f_bias[i], C)
    yf = _fourier_dft(cfou, u4, L, C)
    wr = jnp.concatenate([jnp.transpose(w_router[i], (1, 0, 2)).reshape(D, N_EXPERTS), w_group[i]], axis=1)
    wr = jnp.pad(wr, ((0, 0), (0, LANES - wr.shape[1])))
    br = jnp.pad(jnp.concatenate([b_router[i].reshape(-1), b_group[i]]), (0, LANES - N_EXPERTS - N_GROUPS))
    h2, tok, pos, vec, tinfo, cnt = _outproj_router(yh, yf, x, mix_g[i], w_out[i].astype(bf16), norm2_g[i],
                                                     wr, br.reshape(1, LANES), C)
    xs, first_block, n_blocks = _dispatch(tinfo, cnt, tok, pos, n_rows)
    ys = _experts(first_block, n_blocks, xs, w_gate[i], w_up[i], w_down[i])
    return _combine(tinfo, cnt, h2, vec, final_g, ys, B, L)
```

```python
import functools
import math

import jax
import jax.numpy as jnp
from jax import lax
from jax.experimental import pallas as pl
from jax.experimental.pallas import tpu as pltpu

HEAD_DIM = 64
N_GROUPS = 4
EXPERTS_PER_GROUP = 8
N_EXPERTS = N_GROUPS * EXPERTS_PER_GROUP
FILTER_BANDS = 16
DECAY_FAST_PCT = 0.3
DECAY_SLOW_PCT = 1.5
DECAY_TARGET = 1e-2
EPS = 1e-6

LANES = 128
SUBLANES = 8
MXU_DIM = 256
GEN_ROWS = 256
TOKEN_TILE = 256
CONV_ROWS = 512
CONV_ROW_GROUPS = 2
CONV_COLS = 512
FOURIER_COLS = 256
MOE_ROWS = 512
EXPERT_ROW_GROUPS = 2
FETCH_AHEAD = 3
ROW_DMA_PRIORITY = 1
ROW_TILES_PER_STEP = 2
TILES_PER_STEP = 4
GATHER_AHEAD = 3
SEG_SPLIT = 32
VMEM_LIMIT = 56 * 1024 * 1024

f32 = jnp.float32
bf16 = jnp.bfloat16
HI = lax.Precision.HIGHEST


def _cparams(sem):
    return pltpu.CompilerParams(dimension_semantics=sem, vmem_limit_bytes=VMEM_LIMIT)


def _tile(pref, n):
    return min(pref, n)


_SIN_TERMS = tuple((-1.0) ** k / math.factorial(2 * k + 1) for k in range(7))
_COS_TERMS = tuple((-1.0) ** k / math.factorial(2 * k) for k in range(8))


def _cos_sin_of_turn_fraction(prod, m):
    num = prod & (m - 1)
    quadrant = num // (m // 4)
    x = (num & (m // 4 - 1)).astype(f32) * (2.0 * math.pi / m)
    x2 = x * x
    s = jnp.full_like(x, _SIN_TERMS[-1])
    for c in _SIN_TERMS[-2::-1]:
        s = s * x2 + c
    s = s * x
    c_ = jnp.full_like(x, _COS_TERMS[-1])
    for c in _COS_TERMS[-2::-1]:
        c_ = c_ * x2 + c
    odd = (quadrant & 1) == 1
    cos = jnp.where(odd, s, c_)
    sin = jnp.where(odd, c_, s)
    cos = jnp.where((quadrant == 1) | (quadrant == 2), -cos, cos)
    sin = jnp.where(quadrant >= 2, -sin, sin)
    return cos, sin


def _cos_sin_blocks(a, s, b0, m, ncb, d):
    tc, ts = _cos_sin_of_turn_fraction(a * (s * d), m)
    b0_lane = jnp.zeros(d.shape, jnp.int32)
    for cb in range(ncb):
        b0_lane = jnp.where(d == cb, b0(cb), b0_lane)
    ac, asn = _cos_sin_of_turn_fraction(a * b0_lane, m)
    for cb in range(ncb):
        ca = ac[:, cb:cb + 1]
        sa = asn[:, cb:cb + 1]
        yield cb, ca * tc - sa * ts, sa * tc + ca * ts


def _write_dft_rows(fwd_ref, inv_ref, row_block, L):
    Lh = L // 2
    tr = fwd_ref.shape[1]
    r = lax.broadcasted_iota(jnp.int32, (tr, LANES), 0) + row_block * tr
    d = lax.broadcasted_iota(jnp.int32, (tr, LANES), 1)
    alt_r = (1 - 2 * (r & 1)).astype(f32)
    alt_d = (1 - 2 * (d & 1)).astype(f32)
    ncb = Lh // LANES
    fams = [
        (r, 1, lambda cb: cb * LANES, L),
        (r, 2, lambda cb: 2 * cb * LANES + 1, 2 * L),
        (2 * r + 1, 1, lambda cb: cb * LANES, 2 * L),
    ]
    for fi, (a, s, b0, m) in enumerate(fams):
        for cb, c, sn in _cos_sin_blocks(a, s, b0, m, ncb, d):
            sl = slice(cb * LANES, (cb + 1) * LANES)
            sl_sin = slice(Lh + cb * LANES, Lh + (cb + 1) * LANES)
            if fi == 0:
                fwd_ref[0, :, sl] = c.astype(bf16)
                inv_ref[0, :, sl] = c.astype(bf16)
                fwd_ref[1, :, sl] = jnp.where(r == 0, alt_d, sn).astype(bf16)
                nst = -sn
                if cb == 0:
                    nst = jnp.where(d == 0, -alt_r, nst)
                inv_ref[0, :, sl_sin] = nst.astype(bf16)
            elif fi == 1:
                fwd_ref[2, :, sl] = c.astype(bf16)
                fwd_ref[3, :, sl] = jnp.where(r == 0, alt_d, sn).astype(bf16)
            else:
                inv_ref[1, :, sl] = c.astype(bf16)
                nst = -sn
                if cb == 0:
                    nst = jnp.where(d == 0, -alt_r, nst)
                inv_ref[1, :, sl_sin] = nst.astype(bf16)


def _fourier_consts_kernel(o_ref, *, L):
    n = L // 4
    tr = o_ref.shape[0]
    r = lax.broadcasted_iota(jnp.int32, (tr, LANES), 0) + pl.program_id(0) * tr
    d = lax.broadcasted_iota(jnp.int32, (tr, LANES), 1)

    def col_l(cb):
        parity, j0 = divmod(cb * LANES, n // 2)
        return 2 * j0 + parity

    for cb, c, sn in _cos_sin_blocks(r, 2, col_l, n, n // LANES, d):
        o_ref[:, cb * LANES:(cb + 1) * LANES] = c.astype(bf16)
        o_ref[:, n + cb * LANES:n + (cb + 1) * LANES] = (-sn).astype(bf16)


def _fourier_consts(L):
    n = L // 4
    tr = _tile(GEN_ROWS, n)
    return pl.pallas_call(
        functools.partial(_fourier_consts_kernel, L=L),
        grid=(n // tr,),
        out_specs=pl.BlockSpec((tr, 2 * n), lambda i: (i, 0)),
        out_shape=jax.ShapeDtypeStruct((n, 2 * n), bf16),
        compiler_params=_cparams(("parallel",)),
        name="fourier_consts",
    )()


def _prep_win_kernel(w_ref, o_ref, *, C):
    w = w_ref[...]
    i = lax.broadcasted_iota(jnp.int32, (LANES, LANES), 0)
    j = lax.broadcasted_iota(jnp.int32, (LANES, LANES), 1)
    same = (i // HEAD_DIM) == (j // HEAD_DIM)
    cos, sin = _cos_sin_of_turn_fraction((i % HEAD_DIM) * (j % HEAD_DIM), HEAD_DIM)
    bc = jnp.where(same, cos, 0.0)
    bs = jnp.where(same, sin, 0.0)
    o_ref[:, :3 * C] = w[:, :3 * C].astype(bf16)
    for k in range(0, C, LANES):
        wf = w[:, 3 * C + k:3 * C + k + LANES]
        o_ref[:, 3 * C + k:3 * C + k + LANES] = jnp.dot(wf, bc, precision=HI, preferred_element_type=f32).astype(bf16)
        o_ref[:, 4 * C + k:4 * C + k + LANES] = jnp.dot(wf, bs, precision=HI, preferred_element_type=f32).astype(bf16)


def _prep_win(w_in, C):
    D = w_in.shape[0]
    tr = _tile(GEN_ROWS, D)
    return pl.pallas_call(
        functools.partial(_prep_win_kernel, C=C),
        grid=(D // tr,),
        in_specs=[pl.BlockSpec((tr, 4 * C), lambda i: (i, 0))],
        out_specs=pl.BlockSpec((tr, 5 * C), lambda i: (i, 0)),
        out_shape=jax.ShapeDtypeStruct((D, 5 * C), bf16),
        compiler_params=_cparams(("parallel",)),
        name="prep_win",
    )(w_in)


def _dot_split(a, b):
    a_hi = a.astype(bf16)
    b_hi = b.astype(bf16)
    a_lo = (a - a_hi.astype(f32)).astype(bf16)
    b_lo = (b - b_hi.astype(f32)).astype(bf16)
    return (jnp.dot(a_hi, b_hi, preferred_element_type=f32) + jnp.dot(a_lo, b_hi, preferred_element_type=f32)
            + jnp.dot(a_hi, b_lo, preferred_element_type=f32))


def _write_filter_rows(win_ref, bin_ref, wmid_ref, bmid_ref, freq_ref, wout_ref, h_ref, asum_ref, step, L, C):
    tr = h_ref.shape[1]
    lane = lax.broadcasted_iota(jnp.int32, (tr, LANES), 1)
    row = lax.broadcasted_iota(jnp.int32, (tr, LANES), 0) + step * tr
    band_i = jnp.where(lane <= FILTER_BANDS, lane - 1, lane - 1 - FILTER_BANDS)
    band = 1e-4 + band_i.astype(f32) * ((FILTER_BANDS - 1 - 1e-4) / (FILTER_BANDS - 1))
    ch = lax.broadcasted_iota(jnp.int32, (tr, 2 * C), 1) % C
    max_decay = math.log(DECAY_TARGET) / DECAY_FAST_PCT
    min_decay = math.log(DECAY_TARGET) / DECAY_SLOW_PCT
    delta = jnp.abs(min_decay + ch.astype(f32) * ((max_decay - min_decay) / (C - 1)))
    is_bwd = lax.broadcasted_iota(jnp.int32, (tr, 2 * C), 1) >= C
    row_c = lax.broadcasted_iota(jnp.int32, (tr, 2 * C), 0) + step * tr

    @pl.when(step == 0)
    def _():
        asum_ref[...] = jnp.zeros_like(asum_ref)

    phase = jnp.where(lane <= FILTER_BANDS, 0.5 * math.pi, math.pi)
    pre = []
    for p in range(2):
        pos = (2 * row + p).astype(f32)
        t = pos * (1.0 / (L - 1))
        w = pos * (2.0 * math.pi / L)
        z = jnp.where(lane == 0, t, jnp.where(lane <= 2 * FILTER_BANDS, jnp.sin(band * w + phase), 0.0))
        pre.append(_dot_split(z, win_ref[...]))
    twice = lambda v: jnp.concatenate([v, v], axis=1)
    both = lambda m: jnp.concatenate([jnp.concatenate([m, jnp.zeros_like(m)], axis=1),
                                      jnp.concatenate([jnp.zeros_like(m), m], axis=1)], axis=0)
    h = jnp.sin(twice(freq_ref[0:1, :]) * (jnp.concatenate(pre, axis=1) + twice(bin_ref[...])))
    for i in range(wmid_ref.shape[0]):
        h = jnp.sin(twice(freq_ref[i + 1:i + 2, :]) * (_dot_split(h, both(wmid_ref[i])) + twice(bmid_ref[i:i + 1, :])))
    outs = _dot_split(h, both(wout_ref[...]))
    total = jnp.zeros((1, 2 * C), f32)
    for p in range(2):
        out = outs[:, p * 2 * C:(p + 1) * 2 * C]
        tc = (2 * row_c + p).astype(f32) * (1.0 / (L - 1))
        out = out * jnp.exp(-tc * delta)
        if p == 0:
            out = jnp.where(is_bwd & (row_c == 0), 0.0, out)
        h_ref[p] = out.astype(bf16)
        total = total + jnp.sum(jnp.abs(out), axis=0, keepdims=True)
    asum_ref[0:1, :] += total


def _filter_kernel(win_ref, bin_ref, wmid_ref, bmid_ref, freq_ref, wout_ref, h_ref, asum_ref, *, L, C):
    _write_filter_rows(win_ref, bin_ref, wmid_ref, bmid_ref, freq_ref, wout_ref, h_ref, asum_ref,
                       pl.program_id(0), L, C)


def _hyena_filters(L, C, f_w_in, f_b_in, f_w_mid, f_b_mid, f_freq, f_w_out):
    Lh = L // 2
    tr = _tile(GEN_ROWS, Lh)
    order = f_w_in.shape[1]
    win = jnp.pad(f_w_in, ((0, LANES - f_w_in.shape[0]), (0, 0)))
    full = lambda a: pl.BlockSpec(a.shape, lambda i: (0,) * a.ndim)
    args = (win, f_b_in.reshape(1, order), f_w_mid, f_b_mid, f_freq, f_w_out)
    return pl.pallas_call(
        functools.partial(_filter_kernel, L=L, C=C),
        grid=(Lh // tr,),
        in_specs=[full(a) for a in args],
        out_specs=[pl.BlockSpec((2, tr, 2 * C), lambda i: (0, i, 0)),
                   pl.BlockSpec((SUBLANES, 2 * C), lambda i: (0, 0))],
        out_shape=[jax.ShapeDtypeStruct((2, Lh, 2 * C), bf16),
                   jax.ShapeDtypeStruct((SUBLANES, 2 * C), f32)],
        compiler_params=_cparams(("arbitrary",)),
        name="hyena_filters",
    )(*args)


def _fwd_products(c_ref, x_ref, rows=slice(None)):
    k = c_ref.shape[2]
    even = c_ref[0:2, rows, :]
    odd = c_ref[2:4, rows, :]
    tm = even.shape[1]
    even = jnp.dot(even.reshape(2 * tm, k), x_ref[0], preferred_element_type=f32)
    odd = jnp.dot(odd.reshape(2 * tm, k), x_ref[1], preferred_element_type=f32)
    return even[:tm], even[tm:], odd[:tm], odd[tm:]


def _filter_spectrum_kernel(c_ref, h_ref, asum_ref, k_ref, k0_ref, *, L, C):
    ce, se, co, so = _fwd_products(c_ref, h_ref)
    a = asum_ref[0:1, :]
    scale = 1.0 / ((a[:, :C] + a[:, C:]) * L)
    F = lambda v: v[:, :C]
    G = lambda v: v[:, C:]
    k_ref[0] = (F(ce) + F(co) + G(ce) + G(co)) * scale
    k_ref[1] = (-(F(se) + F(so)) + (G(se) + G(so))) * scale
    k_ref[2] = (F(ce) - F(co) + G(ce) - G(co)) * scale
    k_ref[3] = ((F(se) - F(so)) - (G(se) - G(so))) * scale

    @pl.when(pl.program_id(0) == 0)
    def _():
        k0_ref[...] = jnp.zeros_like(k0_ref)
        k0_ref[0:1, :] = ((F(se) + G(se)) * scale)[0:1, :]
        k0_ref[1:2, :] = ((F(so) - G(so)) * scale)[0:1, :]


def _filter_spectrum(cfwd, hcat, asum, L, C):
    Lh = L // 2
    tm = _tile(CONV_ROWS, Lh)
    return pl.pallas_call(
        functools.partial(_filter_spectrum_kernel, L=L, C=C),
        grid=(Lh // tm,),
        in_specs=[pl.BlockSpec((4, tm, Lh), lambda i: (0, i, 0)),
                  pl.BlockSpec((2, Lh, 2 * C), lambda i: (0, 0, 0)),
                  pl.BlockSpec((SUBLANES, 2 * C), lambda i: (0, 0))],
        out_specs=[pl.BlockSpec((4, tm, C), lambda i: (0, i, 0)),
                   pl.BlockSpec((SUBLANES, C), lambda i: (0, 0))],
        out_shape=[jax.ShapeDtypeStruct((4, Lh, C), f32),
                   jax.ShapeDtypeStruct((SUBLANES, C), f32)],
        compiler_params=_cparams(("arbitrary",)),
        name="filter_spectrum",
    )(cfwd, hcat, asum)


def _conv_fwd_kernel(c_ref, z_ref, k_ref, k0_ref, s_ref):
    C = k_ref.shape[2]
    tm = c_ref.shape[1]
    tg = tm // CONV_ROW_GROUPS
    groups = [slice(r * tg, (r + 1) * tg) for r in range(CONV_ROW_GROUPS)]
    products = [_fwd_products(c_ref, z_ref, rows) for rows in groups]
    is_row0 = (lax.broadcasted_iota(jnp.int32, (SUBLANES, C), 0) == 0) & (pl.program_id(1) == 0)
    for rows, prod in zip(groups, products):
        for h in range(z_ref.shape[2] // C):
            cols = slice(h * C, (h + 1) * C)
            ce, se, co, so = (v[:, cols] for v in prod)
            zsr, zsi, zdr, zdi = ce + co, -(se + so), ce - co, se - so
            ksr, ksi, kdr, kdi = (k_ref[i, rows, :] for i in range(4))
            psr = zsr * ksr - zsi * ksi
            psi = zsr * ksi + zsi * ksr
            pdr = zdr * kdr - zdi * kdi
            pdi = zdr * kdi + zdi * kdr
            gen = [psr + pdr, psi - pdi, psr - pdr, psi + pdi]
            if rows.start == 0:
                top = slice(0, SUBLANES)
                p0 = zsr[top] * ksr[top]
                pl_ = zdr[top] * kdr[top]
                ak, bk = k0_ref[0:1, :], k0_ref[1:2, :]
                phr = se[top] * ak - so[top] * bk
                phi = -(se[top] * bk + so[top] * ak)
                spec = (0.5 * (p0 + pl_), -phr, 0.5 * (p0 - pl_), phi)
                gen = [jnp.concatenate([jnp.where(is_row0, spec[i], gen[i][top]), gen[i][SUBLANES:]], axis=0)
                       for i in range(4)]
            for i in range(4):
                s_ref[i, rows, cols] = gen[i].astype(bf16)


def _conv_fwd(cfwd, z, kspec, k0, C):
    _, Lh, N = z.shape
    tm = _tile(CONV_ROWS, Lh)
    tn = _tile(CONV_COLS, N)
    return pl.pallas_call(
        _conv_fwd_kernel,
        grid=(N // tn, Lh // tm),
        in_specs=[pl.BlockSpec((4, tm, Lh), lambda n, i: (0, i, 0)),
                  pl.BlockSpec((2, Lh, tn), lambda n, i: (0, 0, n)),
                  pl.BlockSpec((4, tm, C), lambda n, i: (0, i, 0)),
                  pl.BlockSpec((SUBLANES, C), lambda n, i: (0, 0))],
        out_specs=pl.BlockSpec((4, tm, tn), lambda n, i: (0, i, n)),
        out_shape=jax.ShapeDtypeStruct((4, Lh, N), bf16),
        compiler_params=_cparams(("parallel", "arbitrary")),
        name="conv_fwd",
    )(cfwd, z, kspec, k0)


def _pair_products(c_ref, r_ref):
    k, tn = r_ref.shape[1:]
    ev = jnp.dot(c_ref[0], r_ref[0:2].reshape(2 * k, tn), preferred_element_type=f32)
    od = jnp.dot(c_ref[1], r_ref[2:4].reshape(2 * k, tn), preferred_element_type=f32)
    return ev, od


def _conv_inv_kernel(c_ref, s_ref, x0_ref, z_ref, fb_ref, y_ref):
    conv = _pair_products(c_ref, s_ref)
    C = fb_ref.shape[1]
    fb = fb_ref[...]
    for p in range(2):
        for h in range(z_ref.shape[2] // C):
            cols = slice(h * C, (h + 1) * C)
            zp = z_ref[p, :, cols].astype(f32)
            y_ref[p, :, cols] = (x0_ref[p, :, cols].astype(f32) * (conv[p][:, cols] + zp * fb)).astype(bf16)


def _conv_inv(cinv, s, x0, z, f_bias, C):
    _, Lh, N = s.shape
    tm = _tile(CONV_ROWS, Lh)
    tn = _tile(CONV_COLS, N)
    return pl.pallas_call(
        _conv_inv_kernel,
        grid=(N // tn, Lh // tm),
        in_specs=[pl.BlockSpec((2, tm, 2 * Lh), lambda n, i: (0, i, 0)),
                  pl.BlockSpec((4, Lh, tn), lambda n, i: (0, 0, n)),
                  pl.BlockSpec((2, tm, tn), lambda n, i: (0, i, n)),
                  pl.BlockSpec((2, tm, tn), lambda n, i: (0, i, n)),
                  pl.BlockSpec((1, C), lambda n, i: (0, 0))],
        out_specs=pl.BlockSpec((2, tm, tn), lambda n, i: (0, i, n)),
        out_shape=jax.ShapeDtypeStruct((2, Lh, N), bf16),
        compiler_params=_cparams(("parallel", "arbitrary")),
        name="conv_inv",
    )(cinv, s, x0, z, f_bias.reshape(1, C))


def _fourier_kernel(c_ref, u_ref, y_ref, rhs, stage, tw1, tw2, *, L, scale):
    Lh, n, H = L // 2, L // 4, L // 8
    tn = u_ref.shape[2]

    @pl.when(pl.program_id(0) == 0)
    def _():
        r1 = lax.broadcasted_iota(jnp.int32, (Lh, LANES), 0)
        tw1[0], tw1[1] = _cos_sin_of_turn_fraction(2 * (r1 % n) + r1 // n, L)
        r2 = lax.broadcasted_iota(jnp.int32, (n, LANES), 0)
        tw2[0], tw2[1] = _cos_sin_of_turn_fraction(2 * (r2 % H) + r2 // H, Lh)

    def rotate(a, b, c, s):
        return c * a - s * b, s * a + c * b

    for q in range(tn // LANES):
        cols = slice(q * LANES, (q + 1) * LANES)
        for p in range(2):
            first = slice(p * n, p * n + H)
            second = slice(p * n + H, (p + 1) * n)
            out = slice(p * H, (p + 1) * H)
            out_b = slice(n + p * H, n + (p + 1) * H)
            c2, s2 = tw2[0, out, :], tw2[1, out, :]

            def emit(sub, a0, b0, a1, b1):
                rhs[sub, out, cols] = (a0 + a1).astype(bf16)
                rhs[sub, out_b, cols] = (b0 + b1).astype(bf16)
                da, db = rotate(a0 - a1, b0 - b1, c2, s2)
                rhs[sub + 1, out, cols] = da.astype(bf16)
                rhs[sub + 1, out_b, cols] = db.astype(bf16)

            ld = lambda k, rows: u_ref[k, rows, cols].astype(f32)
            emit(0, ld(0, first), ld(1, first), ld(0, second), ld(1, second))
            odd = [rotate(ld(2, rows), ld(3, rows), tw1[0, rows, :], tw1[1, rows, :]) for rows in (first, second)]
            emit(2, odd[0][0], odd[0][1], odd[1][0], odd[1][1])

    for parity in range(2):
        pair = [jnp.dot(c_ref[...], rhs[2 * parity + h], preferred_element_type=f32) * scale for h in range(2)]
        for q in range(tn // LANES):
            cols = slice(q * LANES, (q + 1) * LANES)
            for h in range(2):
                stage[q, pl.ds(h, n, stride=2), :] = pair[h][:, cols]
            y_ref[parity, :, cols] = stage[q].astype(bf16)


def _fourier_dft(cfou, u4, L, C):
    _, Lh, N = u4.shape
    n = L // 4
    tn = _tile(FOURIER_COLS, C)
    return pl.pallas_call(
        functools.partial(_fourier_kernel, L=L, scale=1.0 / math.sqrt(L * HEAD_DIM)),
        grid=(N // tn,),
        in_specs=[pl.BlockSpec((n, 2 * n), lambda j: (0, 0)),
                  pl.BlockSpec((4, Lh, tn), lambda j: (0, 0, j))],
        out_specs=pl.BlockSpec((2, Lh, tn), lambda j: (0, 0, j)),
        out_shape=jax.ShapeDtypeStruct((2, Lh, N), bf16),
        scratch_shapes=[pltpu.VMEM((4, 2 * n, tn), bf16), pltpu.VMEM((tn // LANES, Lh, LANES), f32),
                        pltpu.VMEM((2, Lh, LANES), f32), pltpu.VMEM((2, n, LANES), f32)],
        compiler_params=_cparams(("arbitrary",)),
        name="fourier_dft",
    )(cfou, u4)


def _rms(x, g):
    return x * lax.rsqrt(jnp.mean(x * x, axis=-1, keepdims=True) + EPS) * g


def _stage_rows(chunks, x):
    for k in range(chunks.shape[0]):
        chunks[k] = x[:, k * LANES:(k + 1) * LANES]


def _parity_rows(chunks, p, n):
    return jnp.concatenate([chunks[k, pl.ds(p, n, stride=2), :] for k in range(chunks.shape[0])], axis=1)


def _inproj_kernel(xa_ref, xb_ref, pa_ref, na_ref, pb_ref, nb_ref, g_ref, w_ref, cw_ref, cb_ref,
                   z_ref, x0_ref, u_ref, cfwd_ref, cinv_ref, stage, *, C, D, L):
    tj = xa_ref.shape[1] // 2
    jt = pl.program_id(1)
    n_half = pl.num_programs(1)
    _write_dft_rows(cfwd_ref, cinv_ref, pl.program_id(0) * n_half + jt, L)
    g = g_ref[...]
    w = w_ref[...]
    cw0, cw1, cw2 = cw_ref[0:1, :], cw_ref[1:2, :], cw_ref[2:3, :]
    cb = cb_ref[...]
    row = lax.broadcasted_iota(jnp.int32, (tj, 3 * C), 0)
    four = []
    halves = ((xa_ref, pa_ref, na_ref, jt == 0, False),
              (xb_ref, pb_ref, nb_ref, False, jt == n_half - 1))
    for h, (x_ref, p_ref, n_ref, at_start, at_end) in enumerate(halves):
        _stage_rows(stage, x_ref[0])
        lhs = jnp.concatenate([
            _rms(_parity_rows(stage, 0, tj), g).astype(bf16),
            _rms(_parity_rows(stage, 1, tj), g).astype(bf16),
            _rms(p_ref[0], g).astype(bf16),
            _rms(n_ref[0], g).astype(bf16)], axis=0)
        res = jnp.dot(lhs, w, preferred_element_type=f32)
        pe, po = res[:tj, :3 * C], res[tj:2 * tj, :3 * C]
        prev = res[2 * tj + SUBLANES - 1:2 * tj + SUBLANES, :3 * C]
        nxt = res[2 * tj + SUBLANES:2 * tj + SUBLANES + 1, :3 * C]
        if at_start is not False:
            prev = jnp.where(at_start, 0.0, prev)
        if at_end is not False:
            nxt = jnp.where(at_end, 0.0, nxt)
        po_dn = jnp.where(row == 0, prev, pltpu.roll(po, 1, 0))
        pe_up = jnp.where(row == tj - 1, nxt, pltpu.roll(pe, tj - 1, 0))
        uc = (cb + cw0 * po_dn + cw1 * pe + cw2 * po,
              cb + cw0 * pe + cw1 * po + cw2 * pe_up)
        for p in range(2):
            x0_ref[p, h] = uc[p][:, :C].astype(bf16)
            z_ref[p, h] = (uc[p][:, C:2 * C] * uc[p][:, 2 * C:]).astype(bf16)
        four.append((res[:tj, 3 * C:], res[tj:2 * tj, 3 * C:]))
    for p in range(2):
        fa, fb = four[0][p], four[1][p]
        u_ref[0, p] = (fa[:, :C] + fb[:, :C]).astype(bf16)
        u_ref[1, p] = (fa[:, C:] + fb[:, C:]).astype(bf16)
        u_ref[2, p] = (fa[:, :C] - fb[:, :C]).astype(bf16)
        u_ref[3, p] = (fa[:, C:] - fb[:, C:]).astype(bf16)


def _inproj(x, norm_g, wcat, conv_w, conv_b, C):
    B, L, D = x.shape
    Lq = L // 4
    tj = _tile(TOKEN_TILE, Lq)
    nq = Lq // tj
    N = B * C
    hb = 2 * tj // SUBLANES
    last = L // SUBLANES - 1
    main = lambda off: pl.BlockSpec((1, 2 * tj, D), lambda b, j: (b, j + off, 0))
    prev = lambda off: pl.BlockSpec((1, SUBLANES, D), lambda b, j: (b, jnp.maximum((j + off) * hb - 1, 0), 0))
    nxt = lambda off: pl.BlockSpec((1, SUBLANES, D), lambda b, j: (b, jnp.minimum((j + off + 1) * hb, last), 0))
    full = lambda a: pl.BlockSpec(a.shape, lambda b, j: (0,) * a.ndim)
    g = norm_g.reshape(1, D)
    cb = conv_b.reshape(1, 3 * C)
    Lh = L // 2
    dft_rows = Lh // (B * nq)
    assert dft_rows * B * nq == Lh and dft_rows % (2 * SUBLANES) == 0
    dft_spec = pl.BlockSpec((4, dft_rows, Lh), lambda b, j: (0, b * nq + j, 0))
    dft_shape = jax.ShapeDtypeStruct((4, Lh, Lh), bf16)
    inv_spec = pl.BlockSpec((2, dft_rows, 2 * Lh), lambda b, j: (0, b * nq + j, 0))
    inv_shape = jax.ShapeDtypeStruct((2, Lh, 2 * Lh), bf16)
    return pl.pallas_call(
        functools.partial(_inproj_kernel, C=C, D=D, L=L),
        grid=(B, nq),
        in_specs=[main(0), main(nq), prev(0), nxt(0), prev(nq), nxt(nq),
                  full(g), full(wcat), full(conv_w), full(cb)],
        out_specs=[pl.BlockSpec((2, 2, tj, C), lambda b, j: (0, 0, j, b)),
                   pl.BlockSpec((2, 2, tj, C), lambda b, j: (0, 0, j, b)),
                   pl.BlockSpec((4, 2, tj, C), lambda b, j: (0, 0, j, b)),
                   dft_spec, inv_spec],
        out_shape=[jax.ShapeDtypeStruct((2, 2, Lq, N), bf16),
                   jax.ShapeDtypeStruct((2, 2, Lq, N), bf16),
                   jax.ShapeDtypeStruct((4, 2, Lq, N), bf16),
                   dft_shape, inv_shape],
        scratch_shapes=[pltpu.VMEM((D // LANES, 2 * tj, LANES), f32)],
        compiler_params=_cparams(("parallel", "arbitrary")),
        name="inproj",
    )(x, x, x, x, x, x, g, wcat, conv_w, cb)


def _pack_bf16_pairs(v, is_bf16_exact=False):
    k = v.shape[1] // 2
    rounded = v if is_bf16_exact else v.astype(bf16).astype(f32)
    hi = pltpu.bitcast(rounded[:, :k], jnp.uint32)
    lo = pltpu.bitcast(rounded[:, k:], jnp.uint32)
    return hi | (lo >> 16)


def _unpack_bf16_pairs(u):
    hi = pltpu.bitcast(u & jnp.uint32(0xFFFF0000), f32)
    lo = pltpu.bitcast(u << 16, f32)
    return jnp.concatenate([hi, lo], axis=1)


def _row_tiles_shape(n, D):
    return (n, D // (2 * LANES), LANES)


def _store_row_tiles(ref, packed, row0=0):
    total, per_row, _ = ref.shape
    n = packed.shape[0]
    flat = ref.reshape(total * per_row, LANES)
    for s in range(per_row):
        flat[pl.ds(row0 * per_row + s, n, stride=per_row), :] = packed[:, s * LANES:(s + 1) * LANES]


def _load_row_tiles(ref, row0=0, n=None):
    total, per_row, _ = ref.shape
    n = total if n is None else n
    flat = ref.reshape(total * per_row, LANES)
    return jnp.concatenate([flat[pl.ds(row0 * per_row + s, n, stride=per_row), :] for s in range(per_row)],
                           axis=1)


def _outproj_router_kernel(yh_ref, yf_ref, x_ref, mg_ref, wo_ref, g2_ref, wr_ref, br_ref,
                           h2_ref, tok_ref, pos_ref, vec_ref, tinfo_ref, cnt_ref, carry_ref, stage, *, C):
    n_sub = pos_ref.shape[0]
    tj = pos_ref.shape[2]
    _stage_rows(stage, x_ref[0])

    @pl.when((pl.program_id(0) == 0) & (pl.program_id(1) == 0))
    def _():
        carry_ref[...] = jnp.zeros_like(carry_ref)

    li = lax.broadcasted_iota(jnp.int32, (LANES, LANES), 0)
    lj = lax.broadcasted_iota(jnp.int32, (LANES, LANES), 1)
    hw = _tile(MXU_DIM, C)
    ai = lax.broadcasted_iota(jnp.int32, (hw, hw), 0)
    aj = lax.broadcasted_iota(jnp.int32, (hw, hw), 1)
    avg = jnp.where(ai // HEAD_DIM == aj // HEAD_DIM, 1.0 / HEAD_DIM, 0.0).astype(bf16)
    upper = (li < lj).astype(bf16)
    ri = lax.broadcasted_iota(jnp.int32, (tj, tj), 0)
    ci = lax.broadcasted_iota(jnp.int32, (tj, tj), 1)
    tri = (ci < ri).astype(bf16)
    wr = wr_ref[...]
    wr_hi = wr.astype(bf16)
    wr_lo = (wr - wr_hi.astype(f32)).astype(bf16)
    wr_both = jnp.concatenate([wr_hi, wr_lo], axis=1)
    lane = lax.broadcasted_iota(jnp.int32, (n_sub * tj, LANES), 1).astype(f32)
    sub = lax.broadcasted_iota(jnp.int32, (SUBLANES, LANES), 0)
    neg = jnp.float32(-jnp.inf)
    big = jnp.float32(LANES)

    def first_argmax(vals, mask):
        v = jnp.where(mask, vals, neg)
        m = jnp.max(v, axis=1, keepdims=True)
        idx = jnp.min(jnp.where(mask & (v == m), lane, big), axis=1, keepdims=True)
        return m, idx

    normed = []
    for s in range(n_sub):
        r, p = divmod(s, 2)
        parts = []
        for i, ref in enumerate((yh_ref, yf_ref)):
            y = ref[p, r * tj:(r + 1) * tj, :]
            ysq = y * y
            ms = jnp.concatenate([jnp.dot(ysq[:, k:k + hw], avg, preferred_element_type=f32)
                                  for k in range(0, C, hw)], axis=1)
            scale = lax.rsqrt(ms + EPS) * mg_ref[:, i * C:(i + 1) * C]
            parts.append(y * scale.astype(bf16))
        normed.append(jnp.concatenate(parts, axis=1))
    proj = jnp.dot(jnp.concatenate(normed, axis=0), wo_ref[...], preferred_element_type=f32)
    t_hi, t_lo = [], []
    for s in range(n_sub):
        r, p = divmod(s, 2)
        rows = slice(s * tj, (s + 1) * tj)
        h2 = _parity_rows(stage, r * 2 * tj + p, tj) + proj[rows]
        h2_ref[rows, :] = h2
        tok = _rms(h2, g2_ref[...])
        hi = tok.astype(bf16)
        tok_ref[rows, :] = hi
        t_hi.append(hi)
        t_lo.append((tok - hi.astype(f32)).astype(bf16))
    hi_terms = jnp.dot(jnp.concatenate(t_hi, axis=0), wr_both, preferred_element_type=f32)
    logits = (hi_terms[:, :LANES] + hi_terms[:, LANES:]
              + jnp.dot(jnp.concatenate(t_lo, axis=0), wr_hi, preferred_element_type=f32) + br_ref[...])

    gmask = (lane >= N_EXPERTS) & (lane < N_EXPERTS + N_GROUPS)
    gmax, gidx = first_argmax(logits, gmask)
    gsel = gidx - N_EXPERTS
    p_group = 1.0 / jnp.sum(jnp.where(gmask, jnp.exp(logits - gmax), 0.0), axis=1, keepdims=True)
    emask = (lane >= gsel * EXPERTS_PER_GROUP) & (lane < (gsel + 1) * EXPERTS_PER_GROUP)
    l0, e0 = first_argmax(logits, emask)
    l1, e1 = first_argmax(logits, emask & (lane != e0))
    ex = jnp.exp(l1 - l0)
    w0 = p_group / (1.0 + ex)
    w1 = p_group * ex / (1.0 + ex)

    oh0 = (lane == e0).astype(f32)
    oh1 = (lane == e1).astype(f32)
    both = (oh0 + oh1).astype(bf16)
    where, counts = [], []
    for s in range(n_sub):
        tile_both = both[s * tj:(s + 1) * tj, :]
        ranks = jnp.dot(tri, tile_both, preferred_element_type=f32)
        n_e = ranks[tj - 1:tj, :] + tile_both[tj - 1:tj, :].astype(f32)
        n_e8 = jnp.broadcast_to(n_e, (SUBLANES, LANES))
        tile_off = jnp.dot(n_e8.astype(bf16), upper, preferred_element_type=f32)[0:1, :]
        where.append(ranks + tile_off)
        counts.append(n_e8)
    where = jnp.concatenate(where, axis=0)
    pos0 = jnp.sum(oh0 * where, axis=1, keepdims=True)
    pos1 = jnp.sum(oh1 * where, axis=1, keepdims=True)
    cols = jnp.where(lane == 0, w0, jnp.where(lane == 1, w1,
                     jnp.where(lane == 2, pos0, jnp.where(lane == 3, pos1, 0.0))))
    vec_ref[...] = cols
    base = carry_ref[...]
    for s in range(n_sub):
        pos_ref[s] = cols[s * tj:(s + 1) * tj, :].T[:SUBLANES, :].astype(jnp.int32)
        long_seg = (jnp.max(counts[s], axis=1, keepdims=True) >= SEG_SPLIT).astype(f32)
        tinfo_ref[s] = jnp.where(sub == 0, counts[s], jnp.where(sub == 1, base, jnp.where(sub == 2, long_seg, 0.0))
                                 ).astype(jnp.int32)
        base = base + counts[s]
    carry_ref[...] = base
    cnt_ref[...] = base.astype(jnp.int32)


def _outproj_router(yh, yf, x, mix_g, wo, norm2_g, wr, br, C):
    B, L, D = x.shape
    Lh = L // 2
    tj = _tile(TOKEN_TILE, Lh)
    rt = ROW_TILES_PER_STEP
    ns = 2 * rt
    nj = Lh // (rt * tj)
    T = B * L
    ntiles = T // tj
    lin = lambda b, j: b * nj + j
    full = lambda a: pl.BlockSpec(a.shape, lambda b, j: (0,) * a.ndim)
    mg = mix_g.reshape(1, D)
    g2 = norm2_g.reshape(1, D)
    return pl.pallas_call(
        functools.partial(_outproj_router_kernel, C=C),
        grid=(B, nj),
        in_specs=[pl.BlockSpec((2, rt * tj, C), lambda b, j: (0, j, b)),
                  pl.BlockSpec((2, rt * tj, C), lambda b, j: (0, j, b)),
                  pl.BlockSpec((1, ns * tj, D), lambda b, j: (b, j, 0)),
                  full(mg), full(wo), full(g2), full(wr), full(br)],
        out_specs=[pl.BlockSpec((ns * tj, D), lambda b, j: (lin(b, j), 0)),
                   pl.BlockSpec((ns * tj, D), lambda b, j: (lin(b, j), 0)),
                   pl.BlockSpec((ns, SUBLANES, tj), lambda b, j: (lin(b, j), 0, 0)),
                   pl.BlockSpec((ns * tj, LANES), lambda b, j: (lin(b, j), 0)),
                   pl.BlockSpec((ns, SUBLANES, LANES), lambda b, j: (lin(b, j), 0, 0)),
                   pl.BlockSpec((SUBLANES, LANES), lambda b, j: (0, 0))],
        out_shape=[jax.ShapeDtypeStruct((T, D), f32),
                   jax.ShapeDtypeStruct((T, D), bf16),
                   jax.ShapeDtypeStruct((ntiles, SUBLANES, tj), jnp.int32),
                   jax.ShapeDtypeStruct((T, LANES), f32),
                   jax.ShapeDtypeStruct((ntiles, SUBLANES, LANES), jnp.int32),
                   jax.ShapeDtypeStruct((SUBLANES, LANES), jnp.int32)],
        scratch_shapes=[pltpu.VMEM((SUBLANES, LANES), f32), pltpu.VMEM((D // LANES, ns * tj, LANES), f32)],
        compiler_params=_cparams(("arbitrary", "arbitrary")),
        name="outproj_router",
    )(yh, yf, x, mg, wo, g2, wr, br)


def _padded(c):
    return ((c + MOE_ROWS - 1) // MOE_ROWS) * MOE_ROWS


def _pad_starts(cnt_ref, start_ref):
    def body(e, acc):
        start_ref[e] = acc
        return acc + _padded(cnt_ref[0, e])
    return lax.fori_loop(0, N_EXPERTS, body, jnp.int32(0))


def _segment_copies(n, src, src_off, dst, dst_off, sem, max_rows, wait, src_advances=True, min_rows=1):
    bit = max_rows
    while bit >= min_rows:
        take = n & bit

        @pl.when(take != 0)
        def _(src_off=src_off, dst_off=dst_off, bit=bit):
            cp = pltpu.make_async_copy(src.at[pl.ds(src_off, bit)], dst.at[pl.ds(dst_off, bit)], sem)
            if wait:
                cp.wait()
            else:
                cp.start(priority=ROW_DMA_PRIORITY)
        if src_advances:
            src_off = src_off + take
        dst_off = dst_off + take
        bit //= 2


def _tile_segments(tinfo_ref, t, start_ref, local, local_is_src, remote, sem, tj, part="both"):
    def copies(long_part):
        off = 0
        for e in range(N_EXPERTS):
            n = tinfo_ref[t, 0, e]
            far = start_ref[e] + tinfo_ref[t, 1, e]
            if long_part:
                args = (n, off, far, tj, SEG_SPLIT)
            else:
                head = n & ~(SEG_SPLIT - 1)
                args = (n, off + head, far + head, SEG_SPLIT // 2, 1)
            n_, near_, far_, hi, lo = args
            if local_is_src:
                _segment_copies(n_, local, near_, remote, far_, sem, hi, wait=False, min_rows=lo)
            else:
                _segment_copies(n_, remote, far_, local, near_, sem, hi, wait=False, min_rows=lo)
            off = off + n

    if part in ("both", "short"):
        copies(long_part=False)
    if part in ("both", "long"):
        @pl.when(tinfo_ref[t, 2, 0] != 0)
        def _():
            copies(long_part=True)


def _dispatch_kernel(tinfo_ref, cnt_ref, tok_ref, pos_ref, xs_ref, first_ref, nblk_ref,
                     sbufs, zbuf, start_ref, sems, zsem):
    tj = pos_ref.shape[2]
    n_blocks = xs_ref.shape[0] // MOE_ROWS

    @pl.when(pl.program_id(0) == 0)
    def _():
        total = _pad_starts(cnt_ref, start_ref)
        used = total // MOE_ROWS

        def block_ranges(e, c):
            first_ref[e] = start_ref[e] // MOE_ROWS
            nblk_ref[e] = _padded(cnt_ref[0, e]) // MOE_ROWS
            return c
        lax.fori_loop(0, N_EXPERTS, block_ranges, 0)

        zbuf[...] = jnp.zeros_like(zbuf)
        zrows = zbuf.shape[0]
        for wait in (False, True):
            def zero_pad(e, c, wait=wait):
                cnt = cnt_ref[0, e]
                _segment_copies(_padded(cnt) - cnt, zbuf, 0, xs_ref, start_ref[e] + cnt, zsem,
                                zrows, wait, src_advances=False)
                return c
            lax.fori_loop(0, N_EXPERTS, zero_pad, 0)

            def zero_tail(i, c, wait=wait):
                cp = pltpu.make_async_copy(zbuf, xs_ref.at[pl.ds(total + i * zrows, zrows)], zsem)
                if wait:
                    cp.wait()
                else:
                    cp.start()
                return c
            lax.fori_loop(0, (n_blocks - used) * (MOE_ROWS // zrows), zero_tail, 0)

    def wait_tile(k):
        pltpu.make_async_copy(sbufs[k], xs_ref.at[pl.ds(0, 2 * tj)], sems.at[k]).wait()

    def segments(k, part):
        _tile_segments(tinfo_ref, k, start_ref, sbufs[k], True, xs_ref, sems.at[k], tj, part)

    slot = lax.broadcasted_iota(jnp.int32, (2 * tj, tj), 0)
    last = TILES_PER_STEP - 1

    @pl.when(pl.program_id(0) > 0)
    def _():
        for k in range(last):
            wait_tile(k)
    for k in range(TILES_PER_STEP):
        if k == last:
            @pl.when(pl.program_id(0) > 0)
            def _():
                wait_tile(last)
        perm = ((slot == pos_ref[k, 2:3, :]) | (slot == pos_ref[k, 3:4, :])).astype(bf16)
        sorted_rows = jnp.dot(perm, tok_ref[k * tj:(k + 1) * tj, :], preferred_element_type=f32)
        _store_row_tiles(sbufs[k], _pack_bf16_pairs(sorted_rows, is_bf16_exact=True))
        if k > 0:
            segments(k - 1, "short")
    segments(last, "short")
    for k in range(TILES_PER_STEP):
        segments(k, "long")

    @pl.when(pl.program_id(0) == pl.num_programs(0) - 1)
    def _():
        for k in range(TILES_PER_STEP):
            wait_tile(k)


def _dispatch(tinfo, cnt, tok, pos, n_rows):
    ntiles, _, tj = pos.shape
    T, D = tok.shape
    n_blocks = n_rows // MOE_ROWS
    nt = TILES_PER_STEP
    assert ntiles % nt == 0
    sbuf = pltpu.VMEM(_row_tiles_shape(2 * tj, D), jnp.uint32)
    return pl.pallas_call(
        _dispatch_kernel,
        grid=(ntiles // nt,),
        in_specs=[pl.BlockSpec((nt, SUBLANES, LANES), lambda i: (i, 0, 0), memory_space=pltpu.SMEM),
                  pl.BlockSpec(memory_space=pltpu.SMEM),
                  pl.BlockSpec((nt * tj, D), lambda i: (i, 0)),
                  pl.BlockSpec((nt, SUBLANES, tj), lambda i: (i, 0, 0))],
        out_specs=[pl.BlockSpec(memory_space=pl.ANY),
                   pl.BlockSpec(memory_space=pltpu.SMEM),
                   pl.BlockSpec(memory_space=pltpu.SMEM)],
        out_shape=[jax.ShapeDtypeStruct(_row_tiles_shape(n_rows, D), jnp.uint32),
                   jax.ShapeDtypeStruct((N_EXPERTS,), jnp.int32),
                   jax.ShapeDtypeStruct((N_EXPERTS,), jnp.int32)],
        scratch_shapes=[[sbuf] * nt,
                        pltpu.VMEM(_row_tiles_shape(MOE_ROWS // 2, D), jnp.uint32),
                        pltpu.SMEM((N_EXPERTS,), jnp.int32), pltpu.SemaphoreType.DMA((nt,)),
                        pltpu.SemaphoreType.DMA],
        compiler_params=_cparams(("arbitrary",)),
        name="moe_dispatch",
    )(tinfo, cnt, tok, pos)


def _experts_kernel(first_ref, nblk_ref, xs_ref, wg_ref, wu_ref, wd_ref, ys_ref,
                    wg_s, wu_s, wd_s, xbuf, ybuf, in_sems, out_sems):
    e = pl.program_id(0)
    last_e = pl.num_programs(0) - 1
    rows = MOE_ROWS
    n_total = xs_ref.shape[0] // rows
    used = first_ref[last_e] + nblk_ref[last_e]

    def fetch(g, slot):
        return pltpu.make_async_copy(xs_ref.at[pl.ds(g * rows, rows)], xbuf.at[pl.ds(slot * rows, rows)],
                                     in_sems.at[slot])

    def flush(g, slot):
        return pltpu.make_async_copy(ybuf.at[pl.ds(slot * rows, rows)], ys_ref.at[pl.ds(g * rows, rows)],
                                     out_sems.at[slot])

    for ahead in range(FETCH_AHEAD):
        @pl.when((e == 0) & (used > ahead))
        def _(ahead=ahead):
            fetch(ahead, ahead).start(priority=ROW_DMA_PRIORITY)

    @pl.when(nblk_ref[e] > 0)
    def _():
        wg_s[...] = wg_ref[...].astype(bf16)
        wu_s[...] = wu_ref[...].astype(bf16)
        wd_s[...] = wd_ref[...].astype(bf16)

    def block(b, carry):
        g = first_ref[e] + b
        slot = g % 2
        in_slot = g % (FETCH_AHEAD + 1)
        fetch(g, in_slot).wait()

        @pl.when(g + FETCH_AHEAD < used)
        def _():
            fetch(g + FETCH_AHEAD, (g + FETCH_AHEAD) % (FETCH_AHEAD + 1)).start(priority=ROW_DMA_PRIORITY)

        @pl.when(g >= 2)
        def _():
            flush(g - 2, slot).wait()
        half = rows // EXPERT_ROW_GROUPS
        starts = [slot * rows + r * half for r in range(EXPERT_ROW_GROUPS)]
        gated = []
        for r in range(EXPERT_ROW_GROUPS):
            xb = _unpack_bf16_pairs(_load_row_tiles(xbuf, in_slot * rows + r * half, half)).astype(bf16)
            gated.append((jnp.dot(xb, wg_s[...], preferred_element_type=f32),
                          jnp.dot(xb, wu_s[...], preferred_element_type=f32)))
        for row0, (a, u) in zip(starts, gated):
            h = (a * jax.nn.sigmoid(a) * u).astype(bf16)
            _store_row_tiles(ybuf, _pack_bf16_pairs(jnp.dot(h, wd_s[...], preferred_element_type=f32)), row0)
        flush(g, slot).start(priority=ROW_DMA_PRIORITY)
        return carry
    lax.fori_loop(0, nblk_ref[e], block, 0)

    @pl.when(e == last_e)
    def _():
        for back in (1, 2):
            @pl.when(used >= back)
            def _(back=back):
                flush(used - back, (used - back) % 2).wait()
        ybuf[pl.ds(0, rows)] = jnp.zeros((rows,) + ybuf.shape[1:], ybuf.dtype)
        for wait in (False, True):
            def zero_tail(g, c, wait=wait):
                cp = flush(g, 0)
                if wait:
                    cp.wait()
                else:
                    cp.start()
                return c
            lax.fori_loop(used, n_total, zero_tail, 0)


def _experts(first_block, n_blocks, xs, w_gate, w_up, w_down):
    _, D, De = w_gate.shape
    widx = lambda e, first, nblk: (e, 0, 0)
    in_buf = pltpu.VMEM(_row_tiles_shape((FETCH_AHEAD + 1) * MOE_ROWS, D), jnp.uint32)
    out_buf = pltpu.VMEM(_row_tiles_shape(2 * MOE_ROWS, D), jnp.uint32)
    return pl.pallas_call(
        _experts_kernel,
        grid_spec=pltpu.PrefetchScalarGridSpec(
            num_scalar_prefetch=2,
            grid=(N_EXPERTS,),
            in_specs=[pl.BlockSpec(memory_space=pl.ANY),
                      pl.BlockSpec((None, D, De), widx),
                      pl.BlockSpec((None, D, De), widx),
                      pl.BlockSpec((None, De, D), widx)],
            out_specs=pl.BlockSpec(memory_space=pl.ANY),
            scratch_shapes=[pltpu.VMEM((D, De), bf16), pltpu.VMEM((D, De), bf16), pltpu.VMEM((De, D), bf16),
                            in_buf, out_buf, pltpu.SemaphoreType.DMA((FETCH_AHEAD + 1,)),
                            pltpu.SemaphoreType.DMA((2,))],
        ),
        out_shape=jax.ShapeDtypeStruct(xs.shape, jnp.uint32),
        compiler_params=_cparams(("arbitrary",)),
        name="moe_experts",
    )(first_block, n_blocks, xs, w_gate, w_up, w_down)


def _combine_kernel(tinfo_ref, tnext_ref, cnt_ref, h2_ref, vec_ref, g_ref, ys_ref, o_ref,
                    gbufs, stages, start_ref, sems):
    tj = vec_ref.shape[0] // TILES_PER_STEP
    step = pl.program_id(0)

    def fetch(info_ref, t, buf, part="both"):
        _tile_segments(info_ref, t, start_ref, gbufs[buf], False, ys_ref, sems.at[buf], tj, part)

    @pl.when(step == 0)
    def _():
        _pad_starts(cnt_ref, start_ref)
        for t in range(GATHER_AHEAD):
            fetch(tinfo_ref, t, t)

    slot = lax.broadcasted_iota(jnp.int32, (tj, 2 * tj), 1)

    def finish(k, picked):
        v = vec_ref[k * tj:(k + 1) * tj, :]
        y = h2_ref[k * tj:(k + 1) * tj, :] + v[:, 0:1] * picked[:tj] + v[:, 1:2] * picked[tj:]
        out = _rms(y, g_ref[...])
        stage = stages[k // 2]
        for q in range(stage.shape[0]):
            stage[q, pl.ds(k % 2, tj, stride=2), :] = out[:, q * LANES:(q + 1) * LANES]
        if k % 2 == 1:
            r0 = (k // 2) * 2 * tj
            for q in range(stage.shape[0]):
                o_ref[0, r0:r0 + 2 * tj, q * LANES:(q + 1) * LANES] = stage[q]

    picked = None
    for k in range(TILES_PER_STEP):
        pltpu.make_async_copy(ys_ref.at[pl.ds(0, 2 * tj)], gbufs[k], sems.at[k]).wait()
        ahead = k + GATHER_AHEAD
        if ahead < TILES_PER_STEP:
            later = (tinfo_ref, ahead, ahead)
        else:
            later = (tnext_ref, ahead - TILES_PER_STEP, ahead - TILES_PER_STEP)
        fetch(*later, part="short")
        rows = _unpack_bf16_pairs(_load_row_tiles(gbufs[k])).astype(bf16)
        v = vec_ref[k * tj:(k + 1) * tj, :]
        pick = jnp.concatenate([(slot == v[:, 2 + c:3 + c].astype(jnp.int32)).astype(bf16) for c in range(2)],
                               axis=0)
        now = jnp.dot(pick, rows, preferred_element_type=f32)
        if k > 0:
            finish(k - 1, picked)
        picked = now
        fetch(*later, part="long")
    finish(TILES_PER_STEP - 1, picked)


def _combine(tinfo, cnt, h2, vec, final_g, ys, B, L):
    T, D = h2.shape
    ntiles = tinfo.shape[0]
    tj = T // ntiles
    nt = TILES_PER_STEP
    assert GATHER_AHEAD < nt and nt % 2 == 0 and (L // tj) % nt == 0
    nsteps = ntiles // nt
    per_seq = L // (nt * tj)
    g = final_g.reshape(1, D)
    gbuf = pltpu.VMEM(_row_tiles_shape(2 * tj, D), jnp.uint32)
    return pl.pallas_call(
        _combine_kernel,
        grid=(nsteps,),
        in_specs=[pl.BlockSpec((nt, SUBLANES, LANES), lambda i: (i, 0, 0), memory_space=pltpu.SMEM),
                  pl.BlockSpec((nt, SUBLANES, LANES), lambda i: (i + 1, 0, 0), memory_space=pltpu.SMEM),
                  pl.BlockSpec(memory_space=pltpu.SMEM),
                  pl.BlockSpec((nt * tj, D), lambda i: (i, 0)),
                  pl.BlockSpec((nt * tj, LANES), lambda i: (i, 0)),
                  pl.BlockSpec((1, D), lambda i: (0, 0)),
                  pl.BlockSpec(memory_space=pl.ANY)],
        out_specs=pl.BlockSpec((1, nt * tj, D), lambda i: (i // per_seq, i % per_seq, 0)),
        out_shape=jax.ShapeDtypeStruct((B, L, D), f32),
        scratch_shapes=[[gbuf] * nt,
                        [pltpu.VMEM((D // LANES, 2 * tj, LANES), f32)] * (nt // 2),
                        pltpu.SMEM((N_EXPERTS,), jnp.int32), pltpu.SemaphoreType.DMA((nt,))],
        compiler_params=_cparams(("arbitrary",)),
        name="moe_combine",
    )(tinfo, jnp.pad(tinfo, ((0, nt), (0, 0), (0, 0))), cnt, h2, vec, g, ys)


def kernel(x, norm1_g, w_in, conv_w, conv_b, f_w_in, f_b_in, f_w_mid, f_b_mid, f_freq, f_w_out, f_bias,
           mix_g, w_out, norm2_g, w_group, b_group, w_router, b_router, w_gate, w_up, w_down, final_g):
    B, L, D = x.shape
    depth = norm1_g.shape[0]
    assert depth == 1, "the final RMSNorm is fused into the single layer's MoE combine"
    C = D // 2
    Lh = L // 2
    N = B * C
    T = B * L
    n_rows = T * 2 + N_EXPERTS * MOE_ROWS
    cfou = _fourier_consts(L)
    i = 0
    wcat = _prep_win(w_in[i], C)
    hcat, asum = _hyena_filters(L, C, f_w_in[i], f_b_in[i], f_w_mid[i], f_b_mid[i], f_freq[i], f_w_out[i])
    z, x0, u4, cfwd, cinv = _inproj(x, norm1_g[i], wcat, conv_w[i], conv_b[i], C)
    kspec, k0 = _filter_spectrum(cfwd, hcat, asum, L, C)
    z = z.reshape(2, Lh, N)
    x0 = x0.reshape(2, Lh, N)
    u4 = u4.reshape(4, Lh, N)
    s = _conv_fwd(cfwd, z, kspec, k0, C)
    yh = _conv_inv(cinv, s, x0, z, f_bias[i], C)
    yf = _fourier_dft(cfou, u4, L, C)
    wr = jnp.concatenate([jnp.transpose(w_router[i], (1, 0, 2)).reshape(D, N_EXPERTS), w_group[i]], axis=1)
    wr = jnp.pad(wr, ((0, 0), (0, LANES - wr.shape[1])))
    br = jnp.pad(jnp.concatenate([b_router[i].reshape(-1), b_group[i]]), (0, LANES - N_EXPERTS - N_GROUPS))
    h2, tok, pos, vec, tinfo, cnt = _outproj_router(yh, yf, x, mix_g[i], w_out[i].astype(bf16), norm2_g[i],
                                                     wr, br.reshape(1, LANES), C)
    xs, first_block, n_blocks = _dispatch(tinfo, cnt, tok, pos, n_rows)
    ys = _experts(first_block, n_blocks, xs, w_gate[i], w_up[i], w_down[i])
    return _combine(tinfo, cnt, h2, vec, final_g, ys, B, L)
```

```python
import functools
import math

import jax
import jax.numpy as jnp
from jax import lax
from jax.experimental import pallas as pl
from jax.experimental.pallas import tpu as pltpu

HEAD_DIM = 64
N_GROUPS = 4
EXPERTS_PER_GROUP = 8
N_EXPERTS = N_GROUPS * EXPERTS_PER_GROUP
FILTER_BANDS = 16
DECAY_FAST_PCT = 0.3
DECAY_SLOW_PCT = 1.5
DECAY_TARGET = 1e-2
EPS = 1e-6

LANES = 128
SUBLANES = 8
MXU_DIM = 256
GEN_ROWS = 256
TOKEN_TILE = 256
CONV_ROWS = 512
CONV_ROW_GROUPS = 2
CONV_COLS = 512
FOURIER_COLS = 256
MOE_ROWS = 512
EXPERT_ROW_GROUPS = 2
FETCH_AHEAD = 3
ROW_DMA_PRIORITY = 1
ROW_TILES_PER_STEP = 2
TILES_PER_STEP = 4
GATHER_AHEAD = 3
SEG_SPLIT = 32
VMEM_LIMIT = 56 * 1024 * 1024

f32 = jnp.float32
bf16 = jnp.bfloat16
HI = lax.Precision.HIGHEST


def _cparams(sem):
    return pltpu.CompilerParams(dimension_semantics=sem, vmem_limit_bytes=VMEM_LIMIT)


def _tile(pref, n):
    return min(pref, n)


_SIN_TERMS = tuple((-1.0) ** k / math.factorial(2 * k + 1) for k in range(7))
_COS_TERMS = tuple((-1.0) ** k / math.factorial(2 * k) for k in range(8))


def _cos_sin_of_turn_fraction(prod, m):
    num = prod & (m - 1)
    quadrant = num // (m // 4)
    x = (num & (m // 4 - 1)).astype(f32) * (2.0 * math.pi / m)
    x2 = x * x
    s = jnp.full_like(x, _SIN_TERMS[-1])
    for c in _SIN_TERMS[-2::-1]:
        s = s * x2 + c
    s = s * x
    c_ = jnp.full_like(x, _COS_TERMS[-1])
    for c in _COS_TERMS[-2::-1]:
        c_ = c_ * x2 + c
    odd = (quadrant & 1) == 1
    cos = jnp.where(odd, s, c_)
    sin = jnp.where(odd, c_, s)
    cos = jnp.where((quadrant == 1) | (quadrant == 2), -cos, cos)
    sin = jnp.where(quadrant >= 2, -sin, sin)
    return cos, sin


def _cos_sin_blocks(a, s, b0, m, ncb, d):
    tc, ts = _cos_sin_of_turn_fraction(a * (s * d), m)
    b0_lane = jnp.zeros(d.shape, jnp.int32)
    for cb in range(ncb):
        b0_lane = jnp.where(d == cb, b0(cb), b0_lane)
    ac, asn = _cos_sin_of_turn_fraction(a * b0_lane, m)
    for cb in range(ncb):
        ca = ac[:, cb:cb + 1]
        sa = asn[:, cb:cb + 1]
        yield cb, ca * tc - sa * ts, sa * tc + ca * ts


def _write_dft_rows(fwd_ref, inv_ref, row_block, L):
    Lh = L // 2
    tr = fwd_ref.shape[1]
    r = lax.broadcasted_iota(jnp.int32, (tr, LANES), 0) + row_block * tr
    d = lax.broadcasted_iota(jnp.int32, (tr, LANES), 1)
    alt_r = (1 - 2 * (r & 1)).astype(f32)
    alt_d = (1 - 2 * (d & 1)).astype(f32)
    ncb = Lh // LANES
    fams = [
        (r, 1, lambda cb: cb * LANES, L),
        (r, 2, lambda cb: 2 * cb * LANES + 1, 2 * L),
        (2 * r + 1, 1, lambda cb: cb * LANES, 2 * L),
    ]
    for fi, (a, s, b0, m) in enumerate(fams):
        for cb, c, sn in _cos_sin_blocks(a, s, b0, m, ncb, d):
            sl = slice(cb * LANES, (cb + 1) * LANES)
            sl_sin = slice(Lh + cb * LANES, Lh + (cb + 1) * LANES)
            if fi == 0:
                fwd_ref[0, :, sl] = c.astype(bf16)
                inv_ref[0, :, sl] = c.astype(bf16)
                fwd_ref[1, :, sl] = jnp.where(r == 0, alt_d, sn).astype(bf16)
                nst = -sn
                if cb == 0:
                    nst = jnp.where(d == 0, -alt_r, nst)
                inv_ref[0, :, sl_sin] = nst.astype(bf16)
            elif fi == 1:
                fwd_ref[2, :, sl] = c.astype(bf16)
                fwd_ref[3, :, sl] = jnp.where(r == 0, alt_d, sn).astype(bf16)
            else:
                inv_ref[1, :, sl] = c.astype(bf16)
                nst = -sn
                if cb == 0:
                    nst = jnp.where(d == 0, -alt_r, nst)
                inv_ref[1, :, sl_sin] = nst.astype(bf16)


def _fourier_consts_kernel(o_ref, *, L):
    n = L // 4
    tr = o_ref.shape[0]
    r = lax.broadcasted_iota(jnp.int32, (tr, LANES), 0) + pl.program_id(0) * tr
    d = lax.broadcasted_iota(jnp.int32, (tr, LANES), 1)

    def col_l(cb):
        parity, j0 = divmod(cb * LANES, n // 2)
        return 2 * j0 + parity

    for cb, c, sn in _cos_sin_blocks(r, 2, col_l, n, n // LANES, d):
        o_ref[:, cb * LANES:(cb + 1) * LANES] = c.astype(bf16)
        o_ref[:, n + cb * LANES:n + (cb + 1) * LANES] = (-sn).astype(bf16)


def _fourier_consts(L):
    n = L // 4
    tr = _tile(GEN_ROWS, n)
    return pl.pallas_call(
        functools.partial(_fourier_consts_kernel, L=L),
        grid=(n // tr,),
        out_specs=pl.BlockSpec((tr, 2 * n), lambda i: (i, 0)),
        out_shape=jax.ShapeDtypeStruct((n, 2 * n), bf16),
        compiler_params=_cparams(("parallel",)),
        name="fourier_consts",
    )()


def _prep_win_kernel(w_ref, o_ref, *, C):
    w = w_ref[...]
    i = lax.broadcasted_iota(jnp.int32, (LANES, LANES), 0)
    j = lax.broadcasted_iota(jnp.int32, (LANES, LANES), 1)
    same = (i // HEAD_DIM) == (j // HEAD_DIM)
    cos, sin = _cos_sin_of_turn_fraction((i % HEAD_DIM) * (j % HEAD_DIM), HEAD_DIM)
    bc = jnp.where(same, cos, 0.0)
    bs = jnp.where(same, sin, 0.0)
    o_ref[:, :3 * C] = w[:, :3 * C].astype(bf16)
    for k in range(0, C, LANES):
        wf = w[:, 3 * C + k:3 * C + k + LANES]
        o_ref[:, 3 * C + k:3 * C + k + LANES] = jnp.dot(wf, bc, precision=HI, preferred_element_type=f32).astype(bf16)
        o_ref[:, 4 * C + k:4 * C + k + LANES] = jnp.dot(wf, bs, precision=HI, preferred_element_type=f32).astype(bf16)


def _prep_win(w_in, C):
    D = w_in.shape[0]
    tr = _tile(GEN_ROWS, D)
    return pl.pallas_call(
        functools.partial(_prep_win_kernel, C=C),
        grid=(D // tr,),
        in_specs=[pl.BlockSpec((tr, 4 * C), lambda i: (i, 0))],
        out_specs=pl.BlockSpec((tr, 5 * C), lambda i: (i, 0)),
        out_shape=jax.ShapeDtypeStruct((D, 5 * C), bf16),
        compiler_params=_cparams(("parallel",)),
        name="prep_win",
    )(w_in)


def _dot_split(a, b):
    a_hi = a.astype(bf16)
    b_hi = b.astype(bf16)
    a_lo = (a - a_hi.astype(f32)).astype(bf16)
    b_lo = (b - b_hi.astype(f32)).astype(bf16)
    return (jnp.dot(a_hi, b_hi, preferred_element_type=f32) + jnp.dot(a_lo, b_hi, preferred_element_type=f32)
            + jnp.dot(a_hi, b_lo, preferred_element_type=f32))


def _write_filter_rows(win_ref, bin_ref, wmid_ref, bmid_ref, freq_ref, wout_ref, h_ref, asum_ref, step, L, C):
    tr = h_ref.shape[1]
    lane = lax.broadcasted_iota(jnp.int32, (tr, LANES), 1)
    row = lax.broadcasted_iota(jnp.int32, (tr, LANES), 0) + step * tr
    band_i = jnp.where(lane <= FILTER_BANDS, lane - 1, lane - 1 - FILTER_BANDS)
    band = 1e-4 + band_i.astype(f32) * ((FILTER_BANDS - 1 - 1e-4) / (FILTER_BANDS - 1))
    ch = lax.broadcasted_iota(jnp.int32, (tr, 2 * C), 1) % C
    max_decay = math.log(DECAY_TARGET) / DECAY_FAST_PCT
    min_decay = math.log(DECAY_TARGET) / DECAY_SLOW_PCT
    delta = jnp.abs(min_decay + ch.astype(f32) * ((max_decay - min_decay) / (C - 1)))
    is_bwd = lax.broadcasted_iota(jnp.int32, (tr, 2 * C), 1) >= C
    row_c = lax.broadcasted_iota(jnp.int32, (tr, 2 * C), 0) + step * tr

    @pl.when(step == 0)
    def _():
        asum_ref[...] = jnp.zeros_like(asum_ref)

    phase = jnp.where(lane <= FILTER_BANDS, 0.5 * math.pi, math.pi)
    pre = []
    for p in range(2):
        pos = (2 * row + p).astype(f32)
        t = pos * (1.0 / (L - 1))
        w = pos * (2.0 * math.pi / L)
        z = jnp.where(lane == 0, t, jnp.where(lane <= 2 * FILTER_BANDS, jnp.sin(band * w + phase), 0.0))
        pre.append(_dot_split(z, win_ref[...]))
    twice = lambda v: jnp.concatenate([v, v], axis=1)
    both = lambda m: jnp.concatenate([jnp.concatenate([m, jnp.zeros_like(m)], axis=1),
                                      jnp.concatenate([jnp.zeros_like(m), m], axis=1)], axis=0)
    h = jnp.sin(twice(freq_ref[0:1, :]) * (jnp.concatenate(pre, axis=1) + twice(bin_ref[...])))
    for i in range(wmid_ref.shape[0]):
        h = jnp.sin(twice(freq_ref[i + 1:i + 2, :]) * (_dot_split(h, both(wmid_ref[i])) + twice(bmid_ref[i:i + 1, :])))
    outs = _dot_split(h, both(wout_ref[...]))
    total = jnp.zeros((1, 2 * C), f32)
    for p in range(2):
        out = outs[:, p * 2 * C:(p + 1) * 2 * C]
        tc = (2 * row_c + p).astype(f32) * (1.0 / (L - 1))
        out = out * jnp.exp(-tc * delta)
        if p == 0:
            out = jnp.where(is_bwd & (row_c == 0), 0.0, out)
        fwd, bwd = out[:, :C], out[:, C:]
        h_ref[p] = jnp.concatenate([fwd + bwd, fwd - bwd], axis=1).astype(bf16)
        total = total + jnp.sum(jnp.abs(out), axis=0, keepdims=True)
    asum_ref[0:1, :] += total


def _filter_kernel(win_ref, bin_ref, wmid_ref, bmid_ref, freq_ref, wout_ref, h_ref, asum_ref, *, L, C):
    _write_filter_rows(win_ref, bin_ref, wmid_ref, bmid_ref, freq_ref, wout_ref, h_ref, asum_ref,
                       pl.program_id(0), L, C)


def _hyena_filters(L, C, f_w_in, f_b_in, f_w_mid, f_b_mid, f_freq, f_w_out):
    Lh = L // 2
    tr = _tile(GEN_ROWS, Lh)
    order = f_w_in.shape[1]
    win = jnp.pad(f_w_in, ((0, LANES - f_w_in.shape[0]), (0, 0)))
    full = lambda a: pl.BlockSpec(a.shape, lambda i: (0,) * a.ndim)
    args = (win, f_b_in.reshape(1, order), f_w_mid, f_b_mid, f_freq, f_w_out)
    return pl.pallas_call(
        functools.partial(_filter_kernel, L=L, C=C),
        grid=(Lh // tr,),
        in_specs=[full(a) for a in args],
        out_specs=[pl.BlockSpec((2, tr, 2 * C), lambda i: (0, i, 0)),
                   pl.BlockSpec((SUBLANES, 2 * C), lambda i: (0, 0))],
        out_shape=[jax.ShapeDtypeStruct((2, Lh, 2 * C), bf16),
                   jax.ShapeDtypeStruct((SUBLANES, 2 * C), f32)],
        compiler_params=_cparams(("arbitrary",)),
        name="hyena_filters",
    )(*args)


def _fwd_products(c_ref, x_ref, rows=slice(None)):
    k = c_ref.shape[2]
    even = c_ref[0:2, rows, :]
    odd = c_ref[2:4, rows, :]
    tm = even.shape[1]
    even = jnp.dot(even.reshape(2 * tm, k), x_ref[0], preferred_element_type=f32)
    odd = jnp.dot(odd.reshape(2 * tm, k), x_ref[1], preferred_element_type=f32)
    return even[:tm], even[tm:], odd[:tm], odd[tm:]


def _filter_spectrum_kernel(c_ref, h_ref, asum_ref, k_ref, k0_ref, *, L, C):
    ce = jnp.dot(c_ref[0], h_ref[0, :, :C], preferred_element_type=f32)
    se = jnp.dot(c_ref[1], h_ref[0, :, C:], preferred_element_type=f32)
    co = jnp.dot(c_ref[2], h_ref[1, :, :C], preferred_element_type=f32)
    so = jnp.dot(c_ref[3], h_ref[1, :, C:], preferred_element_type=f32)
    a = asum_ref[0:1, :]
    scale = 1.0 / ((a[:, :C] + a[:, C:]) * L)
    k_ref[0] = (ce + co) * scale
    k_ref[1] = -(se + so) * scale
    k_ref[2] = (ce - co) * scale
    k_ref[3] = (se - so) * scale

    @pl.when(pl.program_id(0) == 0)
    def _():
        rows = slice(0, 2 * SUBLANES)
        k0_ref[...] = jnp.zeros_like(k0_ref)
        k0_ref[0:1, :] = (jnp.dot(c_ref[1, rows, :], h_ref[0, :, :C], preferred_element_type=f32) * scale)[0:1, :]
        k0_ref[1:2, :] = (so * scale)[0:1, :]


def _filter_spectrum(cfwd, hcat, asum, L, C):
    Lh = L // 2
    tm = _tile(CONV_ROWS, Lh)
    return pl.pallas_call(
        functools.partial(_filter_spectrum_kernel, L=L, C=C),
        grid=(Lh // tm,),
        in_specs=[pl.BlockSpec((4, tm, Lh), lambda i: (0, i, 0)),
                  pl.BlockSpec((2, Lh, 2 * C), lambda i: (0, 0, 0)),
                  pl.BlockSpec((SUBLANES, 2 * C), lambda i: (0, 0))],
        out_specs=[pl.BlockSpec((4, tm, C), lambda i: (0, i, 0)),
                   pl.BlockSpec((SUBLANES, C), lambda i: (0, 0))],
        out_shape=[jax.ShapeDtypeStruct((4, Lh, C), f32),
                   jax.ShapeDtypeStruct((SUBLANES, C), f32)],
        compiler_params=_cparams(("arbitrary",)),
        name="filter_spectrum",
    )(cfwd, hcat, asum)


def _conv_fwd_kernel(c_ref, z_ref, k_ref, k0_ref, s_ref):
    C = k_ref.shape[2]
    tm = c_ref.shape[1]
    tg = tm // CONV_ROW_GROUPS
    groups = [slice(r * tg, (r + 1) * tg) for r in range(CONV_ROW_GROUPS)]
    products = [_fwd_products(c_ref, z_ref, rows) for rows in groups]
    is_row0 = (lax.broadcasted_iota(jnp.int32, (SUBLANES, C), 0) == 0) & (pl.program_id(1) == 0)
    for rows, prod in zip(groups, products):
        for h in range(z_ref.shape[2] // C):
            cols = slice(h * C, (h + 1) * C)
            ce, se, co, so = (v[:, cols] for v in prod)
            zsr, zsi, zdr, zdi = ce + co, -(se + so), ce - co, se - so
            ksr, ksi, kdr, kdi = (k_ref[i, rows, :] for i in range(4))
            psr = zsr * ksr - zsi * ksi
            psi = zsr * ksi + zsi * ksr
            pdr = zdr * kdr - zdi * kdi
            pdi = zdr * kdi + zdi * kdr
            gen = [psr + pdr, psi - pdi, psr - pdr, psi + pdi]
            if rows.start == 0:
                top = slice(0, SUBLANES)
                p0 = zsr[top] * ksr[top]
                pl_ = zdr[top] * kdr[top]
                ak, bk = k0_ref[0:1, :], k0_ref[1:2, :]
                phr = se[top] * ak - so[top] * bk
                phi = -(se[top] * bk + so[top] * ak)
                spec = (0.5 * (p0 + pl_), -phr, 0.5 * (p0 - pl_), phi)
                gen = [jnp.concatenate([jnp.where(is_row0, spec[i], gen[i][top]), gen[i][SUBLANES:]], axis=0)
                       for i in range(4)]
            for i in range(4):
                s_ref[i, rows, cols] = gen[i].astype(bf16)


def _conv_fwd(cfwd, z, kspec, k0, C):
    _, Lh, N = z.shape
    tm = _tile(CONV_ROWS, Lh)
    tn = _tile(CONV_COLS, N)
    return pl.pallas_call(
        _conv_fwd_kernel,
        grid=(N // tn, Lh // tm),
        in_specs=[pl.BlockSpec((4, tm, Lh), lambda n, i: (0, i, 0)),
                  pl.BlockSpec((2, Lh, tn), lambda n, i: (0, 0, n)),
                  pl.BlockSpec((4, tm, C), lambda n, i: (0, i, 0)),
                  pl.BlockSpec((SUBLANES, C), lambda n, i: (0, 0))],
        out_specs=pl.BlockSpec((4, tm, tn), lambda n, i: (0, i, n)),
        out_shape=jax.ShapeDtypeStruct((4, Lh, N), bf16),
        compiler_params=_cparams(("parallel", "arbitrary")),
        name="conv_fwd",
    )(cfwd, z, kspec, k0)


def _pair_products(c_ref, r_ref):
    k, tn = r_ref.shape[1:]
    ev = jnp.dot(c_ref[0], r_ref[0:2].reshape(2 * k, tn), preferred_element_type=f32)
    od = jnp.dot(c_ref[1], r_ref[2:4].reshape(2 * k, tn), preferred_element_type=f32)
    return ev, od


def _conv_inv_kernel(c_ref, s_ref, x0_ref, z_ref, fb_ref, y_ref):
    conv = _pair_products(c_ref, s_ref)
    C = fb_ref.shape[1]
    fb = fb_ref[...]
    for p in range(2):
        for h in range(z_ref.shape[2] // C):
            cols = slice(h * C, (h + 1) * C)
            zp = z_ref[p, :, cols].astype(f32)
            y_ref[p, :, cols] = (x0_ref[p, :, cols].astype(f32) * (conv[p][:, cols] + zp * fb)).astype(bf16)


def _conv_inv(cinv, s, x0, z, f_bias, C):
    _, Lh, N = s.shape
    tm = _tile(CONV_ROWS, Lh)
    tn = _tile(CONV_COLS, N)
    return pl.pallas_call(
        _conv_inv_kernel,
        grid=(N // tn, Lh // tm),
        in_specs=[pl.BlockSpec((2, tm, 2 * Lh), lambda n, i: (0, i, 0)),
                  pl.BlockSpec((4, Lh, tn), lambda n, i: (0, 0, n)),
                  pl.BlockSpec((2, tm, tn), lambda n, i: (0, i, n)),
                  pl.BlockSpec((2, tm, tn), lambda n, i: (0, i, n)),
                  pl.BlockSpec((1, C), lambda n, i: (0, 0))],
        out_specs=pl.BlockSpec((2, tm, tn), lambda n, i: (0, i, n)),
        out_shape=jax.ShapeDtypeStruct((2, Lh, N), bf16),
        compiler_params=_cparams(("parallel", "arbitrary")),
        name="conv_inv",
    )(cinv, s, x0, z, f_bias.reshape(1, C))


def _fourier_kernel(c_ref, u_ref, y_ref, rhs, stage, tw1, tw2, *, L, scale):
    Lh, n, H = L // 2, L // 4, L // 8
    tn = u_ref.shape[2]

    @pl.when(pl.program_id(0) == 0)
    def _():
        r1 = lax.broadcasted_iota(jnp.int32, (Lh, LANES), 0)
        tw1[0], tw1[1] = _cos_sin_of_turn_fraction(2 * (r1 % n) + r1 // n, L)
        r2 = lax.broadcasted_iota(jnp.int32, (n, LANES), 0)
        tw2[0], tw2[1] = _cos_sin_of_turn_fraction(2 * (r2 % H) + r2 // H, Lh)

    def rotate(a, b, c, s):
        return c * a - s * b, s * a + c * b

    for q in range(tn // LANES):
        cols = slice(q * LANES, (q + 1) * LANES)
        for p in range(2):
            first = slice(p * n, p * n + H)
            second = slice(p * n + H, (p + 1) * n)
            out = slice(p * H, (p + 1) * H)
            out_b = slice(n + p * H, n + (p + 1) * H)
            c2, s2 = tw2[0, out, :], tw2[1, out, :]

            def emit(sub, a0, b0, a1, b1):
                rhs[sub, out, cols] = (a0 + a1).astype(bf16)
                rhs[sub, out_b, cols] = (b0 + b1).astype(bf16)
                da, db = rotate(a0 - a1, b0 - b1, c2, s2)
                rhs[sub + 1, out, cols] = da.astype(bf16)
                rhs[sub + 1, out_b, cols] = db.astype(bf16)

            ld = lambda k, rows: u_ref[k, rows, cols].astype(f32)
            emit(0, ld(0, first), ld(1, first), ld(0, second), ld(1, second))
            odd = [rotate(ld(2, rows), ld(3, rows), tw1[0, rows, :], tw1[1, rows, :]) for rows in (first, second)]
            emit(2, odd[0][0], odd[0][1], odd[1][0], odd[1][1])

    for parity in range(2):
        pair = [jnp.dot(c_ref[...], rhs[2 * parity + h], preferred_element_type=f32) * scale for h in range(2)]
        for q in range(tn // LANES):
            cols = slice(q * LANES, (q + 1) * LANES)
            for h in range(2):
                stage[q, pl.ds(h, n, stride=2), :] = pair[h][:, cols]
            y_ref[parity, :, cols] = stage[q].astype(bf16)


def _fourier_dft(cfou, u4, L, C):
    _, Lh, N = u4.shape
    n = L // 4
    tn = _tile(FOURIER_COLS, C)
    return pl.pallas_call(
        functools.partial(_fourier_kernel, L=L, scale=1.0 / math.sqrt(L * HEAD_DIM)),
        grid=(N // tn,),
        in_specs=[pl.BlockSpec((n, 2 * n), lambda j: (0, 0)),
                  pl.BlockSpec((4, Lh, tn), lambda j: (0, 0, j))],
        out_specs=pl.BlockSpec((2, Lh, tn), lambda j: (0, 0, j)),
        out_shape=jax.ShapeDtypeStruct((2, Lh, N), bf16),
        scratch_shapes=[pltpu.VMEM((4, 2 * n, tn), bf16), pltpu.VMEM((tn // LANES, Lh, LANES), f32),
                        pltpu.VMEM((2, Lh, LANES), f32), pltpu.VMEM((2, n, LANES), f32)],
        compiler_params=_cparams(("arbitrary",)),
        name="fourier_dft",
    )(cfou, u4)


def _rms(x, g):
    return x * lax.rsqrt(jnp.mean(x * x, axis=-1, keepdims=True) + EPS) * g


def _stage_rows(chunks, x):
    for k in range(chunks.shape[0]):
        chunks[k] = x[:, k * LANES:(k + 1) * LANES]


def _parity_rows(chunks, p, n):
    return jnp.concatenate([chunks[k, pl.ds(p, n, stride=2), :] for k in range(chunks.shape[0])], axis=1)


def _inproj_kernel(xa_ref, xb_ref, pa_ref, na_ref, pb_ref, nb_ref, g_ref, w_ref, cw_ref, cb_ref,
                   z_ref, x0_ref, u_ref, cfwd_ref, cinv_ref, stage, *, C, D, L):
    tj = xa_ref.shape[1] // 2
    jt = pl.program_id(1)
    n_half = pl.num_programs(1)
    _write_dft_rows(cfwd_ref, cinv_ref, pl.program_id(0) * n_half + jt, L)
    g = g_ref[...]
    w = w_ref[...]
    cw0, cw1, cw2 = cw_ref[0:1, :], cw_ref[1:2, :], cw_ref[2:3, :]
    cb = cb_ref[...]
    row = lax.broadcasted_iota(jnp.int32, (tj, 3 * C), 0)
    four = []
    halves = ((xa_ref, pa_ref, na_ref, jt == 0, False),
              (xb_ref, pb_ref, nb_ref, False, jt == n_half - 1))
    for h, (x_ref, p_ref, n_ref, at_start, at_end) in enumerate(halves):
        _stage_rows(stage, x_ref[0])
        lhs = jnp.concatenate([
            _rms(_parity_rows(stage, 0, tj), g).astype(bf16),
            _rms(_parity_rows(stage, 1, tj), g).astype(bf16),
            _rms(p_ref[0], g).astype(bf16),
            _rms(n_ref[0], g).astype(bf16)], axis=0)
        res = jnp.dot(lhs, w, preferred_element_type=f32)
        pe, po = res[:tj, :3 * C], res[tj:2 * tj, :3 * C]
        prev = res[2 * tj + SUBLANES - 1:2 * tj + SUBLANES, :3 * C]
        nxt = res[2 * tj + SUBLANES:2 * tj + SUBLANES + 1, :3 * C]
        if at_start is not False:
            prev = jnp.where(at_start, 0.0, prev)
        if at_end is not False:
            nxt = jnp.where(at_end, 0.0, nxt)
        po_dn = jnp.where(row == 0, prev, pltpu.roll(po, 1, 0))
        pe_up = jnp.where(row == tj - 1, nxt, pltpu.roll(pe, tj - 1, 0))
        uc = (cb + cw0 * po_dn + cw1 * pe + cw2 * po,
              cb + cw0 * pe + cw1 * po + cw2 * pe_up)
        for p in range(2):
            x0_ref[p, h] = uc[p][:, :C].astype(bf16)
            z_ref[p, h] = (uc[p][:, C:2 * C] * uc[p][:, 2 * C:]).astype(bf16)
        four.append((res[:tj, 3 * C:], res[tj:2 * tj, 3 * C:]))
    for p in range(2):
        fa, fb = four[0][p], four[1][p]
        u_ref[0, p] = (fa[:, :C] + fb[:, :C]).astype(bf16)
        u_ref[1, p] = (fa[:, C:] + fb[:, C:]).astype(bf16)
        u_ref[2, p] = (fa[:, :C] - fb[:, :C]).astype(bf16)
        u_ref[3, p] = (fa[:, C:] - fb[:, C:]).astype(bf16)


def _inproj(x, norm_g, wcat, conv_w, conv_b, C):
    B, L, D = x.shape
    Lq = L // 4
    tj = _tile(TOKEN_TILE, Lq)
    nq = Lq // tj
    N = B * C
    hb = 2 * tj // SUBLANES
    last = L // SUBLANES - 1
    main = lambda off: pl.BlockSpec((1, 2 * tj, D), lambda b, j: (b, j + off, 0))
    prev = lambda off: pl.BlockSpec((1, SUBLANES, D), lambda b, j: (b, jnp.maximum((j + off) * hb - 1, 0), 0))
    nxt = lambda off: pl.BlockSpec((1, SUBLANES, D), lambda b, j: (b, jnp.minimum((j + off + 1) * hb, last), 0))
    full = lambda a: pl.BlockSpec(a.shape, lambda b, j: (0,) * a.ndim)
    g = norm_g.reshape(1, D)
    cb = conv_b.reshape(1, 3 * C)
    Lh = L // 2
    dft_rows = Lh // (B * nq)
    assert dft_rows * B * nq == Lh and dft_rows % (2 * SUBLANES) == 0
    dft_spec = pl.BlockSpec((4, dft_rows, Lh), lambda b, j: (0, b * nq + j, 0))
    dft_shape = jax.ShapeDtypeStruct((4, Lh, Lh), bf16)
    inv_spec = pl.BlockSpec((2, dft_rows, 2 * Lh), lambda b, j: (0, b * nq + j, 0))
    inv_shape = jax.ShapeDtypeStruct((2, Lh, 2 * Lh), bf16)
    return pl.pallas_call(
        functools.partial(_inproj_kernel, C=C, D=D, L=L),
        grid=(B, nq),
        in_specs=[main(0), main(nq), prev(0), nxt(0), prev(nq), nxt(nq),
                  full(g), full(wcat), full(conv_w), full(cb)],
        out_specs=[pl.BlockSpec((2, 2, tj, C), lambda b, j: (0, 0, j, b)),
                   pl.BlockSpec((2, 2, tj, C), lambda b, j: (0, 0, j, b)),
                   pl.BlockSpec((4, 2, tj, C), lambda b, j: (0, 0, j, b)),
                   dft_spec, inv_spec],
        out_shape=[jax.ShapeDtypeStruct((2, 2, Lq, N), bf16),
                   jax.ShapeDtypeStruct((2, 2, Lq, N), bf16),
                   jax.ShapeDtypeStruct((4, 2, Lq, N), bf16),
                   dft_shape, inv_shape],
        scratch_shapes=[pltpu.VMEM((D // LANES, 2 * tj, LANES), f32)],
        compiler_params=_cparams(("parallel", "arbitrary")),
        name="inproj",
    )(x, x, x, x, x, x, g, wcat, conv_w, cb)


def _pack_bf16_pairs(v, is_bf16_exact=False):
    k = v.shape[1] // 2
    rounded = v if is_bf16_exact else v.astype(bf16).astype(f32)
    hi = pltpu.bitcast(rounded[:, :k], jnp.uint32)
    lo = pltpu.bitcast(rounded[:, k:], jnp.uint32)
    return hi | (lo >> 16)


def _unpack_bf16_pairs(u):
    hi = pltpu.bitcast(u & jnp.uint32(0xFFFF0000), f32)
    lo = pltpu.bitcast(u << 16, f32)
    return jnp.concatenate([hi, lo], axis=1)


def _row_tiles_shape(n, D):
    return (n, D // (2 * LANES), LANES)


def _store_row_tiles(ref, packed, row0=0):
    total, per_row, _ = ref.shape
    n = packed.shape[0]
    flat = ref.reshape(total * per_row, LANES)
    for s in range(per_row):
        flat[pl.ds(row0 * per_row + s, n, stride=per_row), :] = packed[:, s * LANES:(s + 1) * LANES]


def _load_row_tiles(ref, row0=0, n=None):
    total, per_row, _ = ref.shape
    n = total if n is None else n
    flat = ref.reshape(total * per_row, LANES)
    return jnp.concatenate([flat[pl.ds(row0 * per_row + s, n, stride=per_row), :] for s in range(per_row)],
                           axis=1)


def _outproj_router_kernel(yh_ref, yf_ref, x_ref, mg_ref, wo_ref, g2_ref, wr_ref, br_ref,
                           h2_ref, tok_ref, pos_ref, vec_ref, tinfo_ref, cnt_ref, carry_ref, stage, *, C):
    n_sub = pos_ref.shape[0]
    tj = pos_ref.shape[2]
    _stage_rows(stage, x_ref[0])

    @pl.when((pl.program_id(0) == 0) & (pl.program_id(1) == 0))
    def _():
        carry_ref[...] = jnp.zeros_like(carry_ref)

    li = lax.broadcasted_iota(jnp.int32, (LANES, LANES), 0)
    lj = lax.broadcasted_iota(jnp.int32, (LANES, LANES), 1)
    hw = _tile(MXU_DIM, C)
    ai = lax.broadcasted_iota(jnp.int32, (hw, hw), 0)
    aj = lax.broadcasted_iota(jnp.int32, (hw, hw), 1)
    avg = jnp.where(ai // HEAD_DIM == aj // HEAD_DIM, 1.0 / HEAD_DIM, 0.0).astype(bf16)
    upper = (li < lj).astype(bf16)
    ri = lax.broadcasted_iota(jnp.int32, (tj, tj), 0)
    ci = lax.broadcasted_iota(jnp.int32, (tj, tj), 1)
    tri = (ci < ri).astype(bf16)
    wr = wr_ref[...]
    wr_hi = wr.astype(bf16)
    wr_lo = (wr - wr_hi.astype(f32)).astype(bf16)
    wr_both = jnp.concatenate([wr_hi, wr_lo], axis=1)
    lane = lax.broadcasted_iota(jnp.int32, (n_sub * tj, LANES), 1).astype(f32)
    sub = lax.broadcasted_iota(jnp.int32, (SUBLANES, LANES), 0)
    neg = jnp.float32(-jnp.inf)
    big = jnp.float32(LANES)

    def first_argmax(vals, mask):
        v = jnp.where(mask, vals, neg)
        m = jnp.max(v, axis=1, keepdims=True)
        idx = jnp.min(jnp.where(mask & (v == m), lane, big), axis=1, keepdims=True)
        return m, idx

    normed = []
    for s in range(n_sub):
        r, p = divmod(s, 2)
        parts = []
        for i, ref in enumerate((yh_ref, yf_ref)):
            y = ref[p, r * tj:(r + 1) * tj, :]
            ysq = y * y
            ms = jnp.concatenate([jnp.dot(ysq[:, k:k + hw], avg, preferred_element_type=f32)
                                  for k in range(0, C, hw)], axis=1)
            scale = lax.rsqrt(ms + EPS) * mg_ref[:, i * C:(i + 1) * C]
            parts.append(y * scale.astype(bf16))
        normed.append(jnp.concatenate(parts, axis=1))
    proj = jnp.dot(jnp.concatenate(normed, axis=0), wo_ref[...], preferred_element_type=f32)
    t_hi, t_lo = [], []
    for s in range(n_sub):
        r, p = divmod(s, 2)
        rows = slice(s * tj, (s + 1) * tj)
        h2 = _parity_rows(stage, r * 2 * tj + p, tj) + proj[rows]
        h2_ref[rows, :] = h2
        tok = _rms(h2, g2_ref[...])
        hi = tok.astype(bf16)
        tok_ref[rows, :] = hi
        t_hi.append(hi)
        t_lo.append((tok - hi.astype(f32)).astype(bf16))
    hi_terms = jnp.dot(jnp.concatenate(t_hi, axis=0), wr_both, preferred_element_type=f32)
    logits = (hi_terms[:, :LANES] + hi_terms[:, LANES:]
              + jnp.dot(jnp.concatenate(t_lo, axis=0), wr_hi, preferred_element_type=f32) + br_ref[...])

    gmask = (lane >= N_EXPERTS) & (lane < N_EXPERTS + N_GROUPS)
    gmax, gidx = first_argmax(logits, gmask)
    gsel = gidx - N_EXPERTS
    p_group = 1.0 / jnp.sum(jnp.where(gmask, jnp.exp(logits - gmax), 0.0), axis=1, keepdims=True)
    emask = (lane >= gsel * EXPERTS_PER_GROUP) & (lane < (gsel + 1) * EXPERTS_PER_GROUP)
    l0, e0 = first_argmax(logits, emask)
    l1, e1 = first_argmax(logits, emask & (lane != e0))
    ex = jnp.exp(l1 - l0)
    w0 = p_group / (1.0 + ex)
    w1 = p_group * ex / (1.0 + ex)

    oh0 = (lane == e0).astype(f32)
    oh1 = (lane == e1).astype(f32)
    both = (oh0 + oh1).astype(bf16)
    where, counts = [], []
    for s in range(n_sub):
        tile_both = both[s * tj:(s + 1) * tj, :]
        ranks = jnp.dot(tri, tile_both, preferred_element_type=f32)
        n_e = ranks[tj - 1:tj, :] + tile_both[tj - 1:tj, :].astype(f32)
        n_e8 = jnp.broadcast_to(n_e, (SUBLANES, LANES))
        tile_off = jnp.dot(n_e8.astype(bf16), upper, preferred_element_type=f32)[0:1, :]
        where.append(ranks + tile_off)
        counts.append(n_e8)
    where = jnp.concatenate(where, axis=0)
    pos0 = jnp.sum(oh0 * where, axis=1, keepdims=True)
    pos1 = jnp.sum(oh1 * where, axis=1, keepdims=True)
    cols = jnp.where(lane == 0, w0, jnp.where(lane == 1, w1,
                     jnp.where(lane == 2, pos0, jnp.where(lane == 3, pos1, 0.0))))
    vec_ref[...] = cols
    base = carry_ref[...]
    for s in range(n_sub):
        pos_ref[s] = cols[s * tj:(s + 1) * tj, :].T[:SUBLANES, :].astype(jnp.int32)
        long_seg = (jnp.max(counts[s], axis=1, keepdims=True) >= SEG_SPLIT).astype(f32)
        tinfo_ref[s] = jnp.where(sub == 0, counts[s], jnp.where(sub == 1, base, jnp.where(sub == 2, long_seg, 0.0))
                                 ).astype(jnp.int32)
        base = base + counts[s]
    carry_ref[...] = base
    cnt_ref[...] = base.astype(jnp.int32)


def _outproj_router(yh, yf, x, mix_g, wo, norm2_g, wr, br, C):
    B, L, D = x.shape
    Lh = L // 2
    tj = _tile(TOKEN_TILE, Lh)
    rt = ROW_TILES_PER_STEP
    ns = 2 * rt
    nj = Lh // (rt * tj)
    T = B * L
    ntiles = T // tj
    lin = lambda b, j: b * nj + j
    full = lambda a: pl.BlockSpec(a.shape, lambda b, j: (0,) * a.ndim)
    mg = mix_g.reshape(1, D)
    g2 = norm2_g.reshape(1, D)
    return pl.pallas_call(
        functools.partial(_outproj_router_kernel, C=C),
        grid=(B, nj),
        in_specs=[pl.BlockSpec((2, rt * tj, C), lambda b, j: (0, j, b)),
                  pl.BlockSpec((2, rt * tj, C), lambda b, j: (0, j, b)),
                  pl.BlockSpec((1, ns * tj, D), lambda b, j: (b, j, 0)),
                  full(mg), full(wo), full(g2), full(wr), full(br)],
        out_specs=[pl.BlockSpec((ns * tj, D), lambda b, j: (lin(b, j), 0)),
                   pl.BlockSpec((ns * tj, D), lambda b, j: (lin(b, j), 0)),
                   pl.BlockSpec((ns, SUBLANES, tj), lambda b, j: (lin(b, j), 0, 0)),
                   pl.BlockSpec((ns * tj, LANES), lambda b, j: (lin(b, j), 0)),
                   pl.BlockSpec((ns, SUBLANES, LANES), lambda b, j: (lin(b, j), 0, 0)),
                   pl.BlockSpec((SUBLANES, LANES), lambda b, j: (0, 0))],
        out_shape=[jax.ShapeDtypeStruct((T, D), f32),
                   jax.ShapeDtypeStruct((T, D), bf16),
                   jax.ShapeDtypeStruct((ntiles, SUBLANES, tj), jnp.int32),
                   jax.ShapeDtypeStruct((T, LANES), f32),
                   jax.ShapeDtypeStruct((ntiles, SUBLANES, LANES), jnp.int32),
                   jax.ShapeDtypeStruct((SUBLANES, LANES), jnp.int32)],
        scratch_shapes=[pltpu.VMEM((SUBLANES, LANES), f32), pltpu.VMEM((D // LANES, ns * tj, LANES), f32)],
        compiler_params=_cparams(("arbitrary", "arbitrary")),
        name="outproj_router",
    )(yh, yf, x, mg, wo, g2, wr, br)


def _padded(c):
    return ((c + MOE_ROWS - 1) // MOE_ROWS) * MOE_ROWS


def _pad_starts(cnt_ref, start_ref):
    def body(e, acc):
        start_ref[e] = acc
        return acc + _padded(cnt_ref[0, e])
    return lax.fori_loop(0, N_EXPERTS, body, jnp.int32(0))


def _segment_copies(n, src, src_off, dst, dst_off, sem, max_rows, wait, src_advances=True, min_rows=1):
    bit = max_rows
    while bit >= min_rows:
        take = n & bit

        @pl.when(take != 0)
        def _(src_off=src_off, dst_off=dst_off, bit=bit):
            cp = pltpu.make_async_copy(src.at[pl.ds(src_off, bit)], dst.at[pl.ds(dst_off, bit)], sem)
            if wait:
                cp.wait()
            else:
                cp.start(priority=ROW_DMA_PRIORITY)
        if src_advances:
            src_off = src_off + take
        dst_off = dst_off + take
        bit //= 2


def _tile_segments(tinfo_ref, t, start_ref, local, local_is_src, remote, sem, tj, part="both"):
    def copies(long_part):
        off = 0
        for e in range(N_EXPERTS):
            n = tinfo_ref[t, 0, e]
            far = start_ref[e] + tinfo_ref[t, 1, e]
            if long_part:
                args = (n, off, far, tj, SEG_SPLIT)
            else:
                head = n & ~(SEG_SPLIT - 1)
                args = (n, off + head, far + head, SEG_SPLIT // 2, 1)
            n_, near_, far_, hi, lo = args
            if local_is_src:
                _segment_copies(n_, local, near_, remote, far_, sem, hi, wait=False, min_rows=lo)
            else:
                _segment_copies(n_, remote, far_, local, near_, sem, hi, wait=False, min_rows=lo)
            off = off + n

    if part in ("both", "short"):
        copies(long_part=False)
    if part in ("both", "long"):
        @pl.when(tinfo_ref[t, 2, 0] != 0)
        def _():
            copies(long_part=True)


def _dispatch_kernel(tinfo_ref, cnt_ref, tok_ref, pos_ref, xs_ref, first_ref, nblk_ref,
                     sbufs, zbuf, start_ref, sems, zsem):
    tj = pos_ref.shape[2]
    n_blocks = xs_ref.shape[0] // MOE_ROWS

    @pl.when(pl.program_id(0) == 0)
    def _():
        total = _pad_starts(cnt_ref, start_ref)
        used = total // MOE_ROWS

        def block_ranges(e, c):
            first_ref[e] = start_ref[e] // MOE_ROWS
            nblk_ref[e] = _padded(cnt_ref[0, e]) // MOE_ROWS
            return c
        lax.fori_loop(0, N_EXPERTS, block_ranges, 0)

        zbuf[...] = jnp.zeros_like(zbuf)
        zrows = zbuf.shape[0]
        for wait in (False, True):
            def zero_pad(e, c, wait=wait):
                cnt = cnt_ref[0, e]
                _segment_copies(_padded(cnt) - cnt, zbuf, 0, xs_ref, start_ref[e] + cnt, zsem,
                                zrows, wait, src_advances=False)
                return c
            lax.fori_loop(0, N_EXPERTS, zero_pad, 0)

            def zero_tail(i, c, wait=wait):
                cp = pltpu.make_async_copy(zbuf, xs_ref.at[pl.ds(total + i * zrows, zrows)], zsem)
                if wait:
                    cp.wait()
                else:
                    cp.start()
                return c
            lax.fori_loop(0, (n_blocks - used) * (MOE_ROWS // zrows), zero_tail, 0)

    def wait_tile(k):
        pltpu.make_async_copy(sbufs[k], xs_ref.at[pl.ds(0, 2 * tj)], sems.at[k]).wait()

    def segments(k, part):
        _tile_segments(tinfo_ref, k, start_ref, sbufs[k], True, xs_ref, sems.at[k], tj, part)

    slot = lax.broadcasted_iota(jnp.int32, (2 * tj, tj), 0)
    last = TILES_PER_STEP - 1

    @pl.when(pl.program_id(0) > 0)
    def _():
        for k in range(last):
            wait_tile(k)
    for k in range(TILES_PER_STEP):
        if k == last:
            @pl.when(pl.program_id(0) > 0)
            def _():
                wait_tile(last)
        perm = ((slot == pos_ref[k, 2:3, :]) | (slot == pos_ref[k, 3:4, :])).astype(bf16)
        sorted_rows = jnp.dot(perm, tok_ref[k * tj:(k + 1) * tj, :], preferred_element_type=f32)
        _store_row_tiles(sbufs[k], _pack_bf16_pairs(sorted_rows, is_bf16_exact=True))
        if k > 0:
            segments(k - 1, "short")
    segments(last, "short")
    for k in range(TILES_PER_STEP):
        segments(k, "long")

    @pl.when(pl.program_id(0) == pl.num_programs(0) - 1)
    def _():
        for k in range(TILES_PER_STEP):
            wait_tile(k)


def _dispatch(tinfo, cnt, tok, pos, n_rows):
    ntiles, _, tj = pos.shape
    T, D = tok.shape
    n_blocks = n_rows // MOE_ROWS
    nt = TILES_PER_STEP
    assert ntiles % nt == 0
    sbuf = pltpu.VMEM(_row_tiles_shape(2 * tj, D), jnp.uint32)
    return pl.pallas_call(
        _dispatch_kernel,
        grid=(ntiles // nt,),
        in_specs=[pl.BlockSpec((nt, SUBLANES, LANES), lambda i: (i, 0, 0), memory_space=pltpu.SMEM),
                  pl.BlockSpec(memory_space=pltpu.SMEM),
                  pl.BlockSpec((nt * tj, D), lambda i: (i, 0)),
                  pl.BlockSpec((nt, SUBLANES, tj), lambda i: (i, 0, 0))],
        out_specs=[pl.BlockSpec(memory_space=pl.ANY),
                   pl.BlockSpec(memory_space=pltpu.SMEM),
                   pl.BlockSpec(memory_space=pltpu.SMEM)],
        out_shape=[jax.ShapeDtypeStruct(_row_tiles_shape(n_rows, D), jnp.uint32),
                   jax.ShapeDtypeStruct((N_EXPERTS,), jnp.int32),
                   jax.ShapeDtypeStruct((N_EXPERTS,), jnp.int32)],
        scratch_shapes=[[sbuf] * nt,
                        pltpu.VMEM(_row_tiles_shape(MOE_ROWS // 2, D), jnp.uint32),
                        pltpu.SMEM((N_EXPERTS,), jnp.int32), pltpu.SemaphoreType.DMA((nt,)),
                        pltpu.SemaphoreType.DMA],
        compiler_params=_cparams(("arbitrary",)),
        name="moe_dispatch",
    )(tinfo, cnt, tok, pos)


def _experts_kernel(first_ref, nblk_ref, xs_ref, wg_ref, wu_ref, wd_ref, ys_ref,
                    wg_s, wu_s, wd_s, xbuf, ybuf, in_sems, out_sems):
    e = pl.program_id(0)
    last_e = pl.num_programs(0) - 1
    rows = MOE_ROWS
    n_total = xs_ref.shape[0] // rows
    used = first_ref[last_e] + nblk_ref[last_e]

    def fetch(g, slot):
        return pltpu.make_async_copy(xs_ref.at[pl.ds(g * rows, rows)], xbuf.at[pl.ds(slot * rows, rows)],
                                     in_sems.at[slot])

    def flush(g, slot):
        return pltpu.make_async_copy(ybuf.at[pl.ds(slot * rows, rows)], ys_ref.at[pl.ds(g * rows, rows)],
                                     out_sems.at[slot])

    for ahead in range(FETCH_AHEAD):
        @pl.when((e == 0) & (used > ahead))
        def _(ahead=ahead):
            fetch(ahead, ahead).start(priority=ROW_DMA_PRIORITY)

    @pl.when(nblk_ref[e] > 0)
    def _():
        wg_s[...] = wg_ref[...].astype(bf16)
        wu_s[...] = wu_ref[...].astype(bf16)
        wd_s[...] = wd_ref[...].astype(bf16)

    def block(b, carry):
        g = first_ref[e] + b
        slot = g % 2
        in_slot = g % (FETCH_AHEAD + 1)
        fetch(g, in_slot).wait()

        @pl.when(g + FETCH_AHEAD < used)
        def _():
            fetch(g + FETCH_AHEAD, (g + FETCH_AHEAD) % (FETCH_AHEAD + 1)).start(priority=ROW_DMA_PRIORITY)

        @pl.when(g >= 2)
        def _():
            flush(g - 2, slot).wait()
        half = rows // EXPERT_ROW_GROUPS
        starts = [slot * rows + r * half for r in range(EXPERT_ROW_GROUPS)]
        gated = []
        for r in range(EXPERT_ROW_GROUPS):
            xb = _unpack_bf16_pairs(_load_row_tiles(xbuf, in_slot * rows + r * half, half)).astype(bf16)
            gated.append((jnp.dot(xb, wg_s[...], preferred_element_type=f32),
                          jnp.dot(xb, wu_s[...], preferred_element_type=f32)))
        for row0, (a, u) in zip(starts, gated):
            h = (a * jax.nn.sigmoid(a) * u).astype(bf16)
            _store_row_tiles(ybuf, _pack_bf16_pairs(jnp.dot(h, wd_s[...], preferred_element_type=f32)), row0)
        flush(g, slot).start(priority=ROW_DMA_PRIORITY)
        return carry
    lax.fori_loop(0, nblk_ref[e], block, 0)

    @pl.when(e == last_e)
    def _():
        for back in (1, 2):
            @pl.when(used >= back)
            def _(back=back):
                flush(used - back, (used - back) % 2).wait()
        ybuf[pl.ds(0, rows)] = jnp.zeros((rows,) + ybuf.shape[1:], ybuf.dtype)
        for wait in (False, True):
            def zero_tail(g, c, wait=wait):
                cp = flush(g, 0)
                if wait:
                    cp.wait()
                else:
                    cp.start()
                return c
            lax.fori_loop(used, n_total, zero_tail, 0)


def _experts(first_block, n_blocks, xs, w_gate, w_up, w_down):
    _, D, De = w_gate.shape
    widx = lambda e, first, nblk: (e, 0, 0)
    in_buf = pltpu.VMEM(_row_tiles_shape((FETCH_AHEAD + 1) * MOE_ROWS, D), jnp.uint32)
    out_buf = pltpu.VMEM(_row_tiles_shape(2 * MOE_ROWS, D), jnp.uint32)
    return pl.pallas_call(
        _experts_kernel,
        grid_spec=pltpu.PrefetchScalarGridSpec(
            num_scalar_prefetch=2,
            grid=(N_EXPERTS,),
            in_specs=[pl.BlockSpec(memory_space=pl.ANY),
                      pl.BlockSpec((None, D, De), widx),
                      pl.BlockSpec((None, D, De), widx),
                      pl.BlockSpec((None, De, D), widx)],
            out_specs=pl.BlockSpec(memory_space=pl.ANY),
            scratch_shapes=[pltpu.VMEM((D, De), bf16), pltpu.VMEM((D, De), bf16), pltpu.VMEM((De, D), bf16),
                            in_buf, out_buf, pltpu.SemaphoreType.DMA((FETCH_AHEAD + 1,)),
                            pltpu.SemaphoreType.DMA((2,))],
        ),
        out_shape=jax.ShapeDtypeStruct(xs.shape, jnp.uint32),
        compiler_params=_cparams(("arbitrary",)),
        name="moe_experts",
    )(first_block, n_blocks, xs, w_gate, w_up, w_down)


def _combine_kernel(tinfo_ref, tnext_ref, cnt_ref, h2_ref, vec_ref, g_ref, ys_ref, o_ref,
                    gbufs, stages, start_ref, sems):
    tj = vec_ref.shape[0] // TILES_PER_STEP
    step = pl.program_id(0)

    def fetch(info_ref, t, buf, part="both"):
        _tile_segments(info_ref, t, start_ref, gbufs[buf], False, ys_ref, sems.at[buf], tj, part)

    @pl.when(step == 0)
    def _():
        _pad_starts(cnt_ref, start_ref)
        for t in range(GATHER_AHEAD):
            fetch(tinfo_ref, t, t)

    slot = lax.broadcasted_iota(jnp.int32, (tj, 2 * tj), 1)

    def finish(k, picked):
        v = vec_ref[k * tj:(k + 1) * tj, :]
        y = h2_ref[k * tj:(k + 1) * tj, :] + v[:, 0:1] * picked[:tj] + v[:, 1:2] * picked[tj:]
        out = _rms(y, g_ref[...])
        stage = stages[k // 2]
        for q in range(stage.shape[0]):
            stage[q, pl.ds(k % 2, tj, stride=2), :] = out[:, q * LANES:(q + 1) * LANES]
        if k % 2 == 1:
            r0 = (k // 2) * 2 * tj
            for q in range(stage.shape[0]):
                o_ref[0, r0:r0 + 2 * tj, q * LANES:(q + 1) * LANES] = stage[q]

    picked = None
    for k in range(TILES_PER_STEP):
        pltpu.make_async_copy(ys_ref.at[pl.ds(0, 2 * tj)], gbufs[k], sems.at[k]).wait()
        ahead = k + GATHER_AHEAD
        if ahead < TILES_PER_STEP:
            later = (tinfo_ref, ahead, ahead)
        else:
            later = (tnext_ref, ahead - TILES_PER_STEP, ahead - TILES_PER_STEP)
        fetch(*later, part="short")
        rows = _unpack_bf16_pairs(_load_row_tiles(gbufs[k])).astype(bf16)
        v = vec_ref[k * tj:(k + 1) * tj, :]
        pick = jnp.concatenate([(slot == v[:, 2 + c:3 + c].astype(jnp.int32)).astype(bf16) for c in range(2)],
                               axis=0)
        now = jnp.dot(pick, rows, preferred_element_type=f32)
        if k > 0:
            finish(k - 1, picked)
        picked = now
        fetch(*later, part="long")
    finish(TILES_PER_STEP - 1, picked)


def _combine(tinfo, cnt, h2, vec, final_g, ys, B, L):
    T, D = h2.shape
    ntiles = tinfo.shape[0]
    tj = T // ntiles
    nt = TILES_PER_STEP
    assert GATHER_AHEAD < nt and nt % 2 == 0 and (L // tj) % nt == 0
    nsteps = ntiles // nt
    per_seq = L // (nt * tj)
    g = final_g.reshape(1, D)
    gbuf = pltpu.VMEM(_row_tiles_shape(2 * tj, D), jnp.uint32)
    return pl.pallas_call(
        _combine_kernel,
        grid=(nsteps,),
        in_specs=[pl.BlockSpec((nt, SUBLANES, LANES), lambda i: (i, 0, 0), memory_space=pltpu.SMEM),
                  pl.BlockSpec((nt, SUBLANES, LANES), lambda i: (i + 1, 0, 0), memory_space=pltpu.SMEM),
                  pl.BlockSpec(memory_space=pltpu.SMEM),
                  pl.BlockSpec((nt * tj, D), lambda i: (i, 0)),
                  pl.BlockSpec((nt * tj, LANES), lambda i: (i, 0)),
                  pl.BlockSpec((1, D), lambda i: (0, 0)),
                  pl.BlockSpec(memory_space=pl.ANY)],
        out_specs=pl.BlockSpec((1, nt * tj, D), lambda i: (i // per_seq, i % per_seq, 0)),
        out_shape=jax.ShapeDtypeStruct((B, L, D), f32),
        scratch_shapes=[[gbuf] * nt,
                        [pltpu.VMEM((D // LANES, 2 * tj, LANES), f32)] * (nt // 2),
                        pltpu.SMEM((N_EXPERTS,), jnp.int32), pltpu.SemaphoreType.DMA((nt,))],
        compiler_params=_cparams(("arbitrary",)),
        name="moe_combine",
    )(tinfo, jnp.pad(tinfo, ((0, nt), (0, 0), (0, 0))), cnt, h2, vec, g, ys)


def kernel(x, norm1_g, w_in, conv_w, conv_b, f_w_in, f_b_in, f_w_mid, f_b_mid, f_freq, f_w_out, f_bias,
           mix_g, w_out, norm2_g, w_group, b_group, w_router, b_router, w_gate, w_up, w_down, final_g):
    B, L, D = x.shape
    depth = norm1_g.shape[0]
    assert depth == 1, "the final RMSNorm is fused into the single layer's MoE combine"
    C = D // 2
    Lh = L // 2
    N = B * C
    T = B * L
    n_rows = T * 2 + N_EXPERTS * MOE_ROWS
    cfou = _fourier_consts(L)
    i = 0
    wcat = _prep_win(w_in[i], C)
    hcat, asum = _hyena_filters(L, C, f_w_in[i], f_b_in[i], f_w_mid[i], f_b_mid[i], f_freq[i], f_w_out[i])
    z, x0, u4, cfwd, cinv = _inproj(x, norm1_g[i], wcat, conv_w[i], conv_b[i], C)
    kspec, k0 = _filter_spectrum(cfwd, hcat, asum, L, C)
    z = z.reshape(2, Lh, N)
    x0 = x0.reshape(2, Lh, N)
    u4 = u4.reshape(4, Lh, N)
    s = _conv_fwd(cfwd, z, kspec, k0, C)
    yh = _conv_inv(cinv, s, x0, z, f_bias[i], C)
    yf = _fourier_dft(cfou, u4, L, C)
    wr = jnp.concatenate([jnp.transpose(w_router[i], (1, 0, 2)).reshape(D, N_EXPERTS), w_group[i]], axis=1)
    wr = jnp.pad(wr, ((0, 0), (0, LANES - wr.shape[1])))
    br = jnp.pad(jnp.concatenate([b_router[i].reshape(-1), b_group[i]]), (0, LANES - N_EXPERTS - N_GROUPS))
    h2, tok, pos, vec, tinfo, cnt = _outproj_router(yh, yf, x, mix_g[i], w_out[i].astype(bf16), norm2_g[i],
                                                     wr, br.reshape(1, LANES), C)
    xs, first_block, n_blocks = _dispatch(tinfo, cnt, tok, pos, n_rows)
    ys = _experts(first_block, n_blocks, xs, w_gate[i], w_up[i], w_down[i])
    return _combine(tinfo, cnt, h2, vec, final_g, ys, B, L)
```

```python
import functools
import math

import jax
import jax.numpy as jnp
from jax import lax
from jax.experimental import pallas as pl
from jax.experimental.pallas import tpu as pltpu

HEAD_DIM = 64
N_GROUPS = 4
EXPERTS_PER_GROUP = 8
N_EXPERTS = N_GROUPS * EXPERTS_PER_GROUP
FILTER_BANDS = 16
DECAY_FAST_PCT = 0.3
DECAY_SLOW_PCT = 1.5
DECAY_TARGET = 1e-2
EPS = 1e-6

LANES = 128
SUBLANES = 8
MXU_DIM = 256
GEN_ROWS = 256
TOKEN_TILE = 256
CONV_ROWS = 512
CONV_ROW_GROUPS = 2
CONV_COLS = 512
FOURIER_COLS = 256
MOE_ROWS = 512
EXPERT_ROW_GROUPS = 2
FETCH_AHEAD = 3
ROW_DMA_PRIORITY = 1
ROW_TILES_PER_STEP = 2
TILES_PER_STEP = 4
GATHER_AHEAD = 3
SEG_SPLIT = 32
VMEM_LIMIT = 56 * 1024 * 1024

f32 = jnp.float32
bf16 = jnp.bfloat16


def _cparams(sem):
    return pltpu.CompilerParams(dimension_semantics=sem, vmem_limit_bytes=VMEM_LIMIT)


def _tile(pref, n):
    return min(pref, n)


_SIN_TERMS = tuple((-1.0) ** k / math.factorial(2 * k + 1) for k in range(7))
_COS_TERMS = tuple((-1.0) ** k / math.factorial(2 * k) for k in range(8))


def _cos_sin_of_turn_fraction(prod, m):
    num = prod & (m - 1)
    quadrant = num // (m // 4)
    x = (num & (m // 4 - 1)).astype(f32) * (2.0 * math.pi / m)
    x2 = x * x
    s = jnp.full_like(x, _SIN_TERMS[-1])
    for c in _SIN_TERMS[-2::-1]:
        s = s * x2 + c
    s = s * x
    c_ = jnp.full_like(x, _COS_TERMS[-1])
    for c in _COS_TERMS[-2::-1]:
        c_ = c_ * x2 + c
    odd = (quadrant & 1) == 1
    cos = jnp.where(odd, s, c_)
    sin = jnp.where(odd, c_, s)
    cos = jnp.where((quadrant == 1) | (quadrant == 2), -cos, cos)
    sin = jnp.where(quadrant >= 2, -sin, sin)
    return cos, sin


def _cos_sin_blocks(a, s, b0, m, ncb, d):
    tc, ts = _cos_sin_of_turn_fraction(a * (s * d), m)
    b0_lane = jnp.zeros(d.shape, jnp.int32)
    for cb in range(ncb):
        b0_lane = jnp.where(d == cb, b0(cb), b0_lane)
    ac, asn = _cos_sin_of_turn_fraction(a * b0_lane, m)
    for cb in range(ncb):
        ca = ac[:, cb:cb + 1]
        sa = asn[:, cb:cb + 1]
        yield cb, ca * tc - sa * ts, sa * tc + ca * ts


def _write_dft_rows(fwd_ref, inv_ref, row_block, L):
    Lh = L // 2
    tr = fwd_ref.shape[1]
    r = lax.broadcasted_iota(jnp.int32, (tr, LANES), 0) + row_block * tr
    d = lax.broadcasted_iota(jnp.int32, (tr, LANES), 1)
    alt_r = (1 - 2 * (r & 1)).astype(f32)
    alt_d = (1 - 2 * (d & 1)).astype(f32)
    ncb = Lh // LANES
    fams = [
        (r, 1, lambda cb: cb * LANES, L),
        (r, 2, lambda cb: 2 * cb * LANES + 1, 2 * L),
        (2 * r + 1, 1, lambda cb: cb * LANES, 2 * L),
    ]
    for fi, (a, s, b0, m) in enumerate(fams):
        for cb, c, sn in _cos_sin_blocks(a, s, b0, m, ncb, d):
            sl = slice(cb * LANES, (cb + 1) * LANES)
            sl_sin = slice(Lh + cb * LANES, Lh + (cb + 1) * LANES)
            if fi == 0:
                fwd_ref[0, :, sl] = c.astype(bf16)
                inv_ref[0, :, sl] = c.astype(bf16)
                fwd_ref[1, :, sl] = jnp.where(r == 0, alt_d, sn).astype(bf16)
                nst = -sn
                if cb == 0:
                    nst = jnp.where(d == 0, -alt_r, nst)
                inv_ref[0, :, sl_sin] = nst.astype(bf16)
            elif fi == 1:
                fwd_ref[2, :, sl] = c.astype(bf16)
                fwd_ref[3, :, sl] = jnp.where(r == 0, alt_d, sn).astype(bf16)
            else:
                inv_ref[1, :, sl] = c.astype(bf16)
                nst = -sn
                if cb == 0:
                    nst = jnp.where(d == 0, -alt_r, nst)
                inv_ref[1, :, sl_sin] = nst.astype(bf16)


def _fourier_consts_kernel(o_ref, *, L):
    n = L // 4
    tr = o_ref.shape[0]
    r = lax.broadcasted_iota(jnp.int32, (tr, LANES), 0) + pl.program_id(0) * tr
    d = lax.broadcasted_iota(jnp.int32, (tr, LANES), 1)

    def col_l(cb):
        parity, j0 = divmod(cb * LANES, n // 2)
        return 2 * j0 + parity

    for cb, c, sn in _cos_sin_blocks(r, 2, col_l, n, n // LANES, d):
        o_ref[:, cb * LANES:(cb + 1) * LANES] = c.astype(bf16)
        o_ref[:, n + cb * LANES:n + (cb + 1) * LANES] = (-sn).astype(bf16)


def _fourier_consts(L):
    n = L // 4
    tr = _tile(GEN_ROWS, n)
    return pl.pallas_call(
        functools.partial(_fourier_consts_kernel, L=L),
        grid=(n // tr,),
        out_specs=pl.BlockSpec((tr, 2 * n), lambda i: (i, 0)),
        out_shape=jax.ShapeDtypeStruct((n, 2 * n), bf16),
        compiler_params=_cparams(("parallel",)),
        name="fourier_consts",
    )()


def _prep_win_kernel(w_ref, o_ref, *, C):
    w = w_ref[...]
    i = lax.broadcasted_iota(jnp.int32, (LANES, LANES), 0)
    j = lax.broadcasted_iota(jnp.int32, (LANES, LANES), 1)
    same = (i // HEAD_DIM) == (j // HEAD_DIM)
    cos, sin = _cos_sin_of_turn_fraction((i % HEAD_DIM) * (j % HEAD_DIM), HEAD_DIM)
    bc = jnp.where(same, cos, 0.0)
    bs = jnp.where(same, sin, 0.0)
    o_ref[:, :3 * C] = w[:, :3 * C].astype(bf16)
    for k in range(0, C, LANES):
        wf = w[:, 3 * C + k:3 * C + k + LANES]
        o_ref[:, 3 * C + k:3 * C + k + LANES] = _dot_split(wf, bc).astype(bf16)
        o_ref[:, 4 * C + k:4 * C + k + LANES] = _dot_split(wf, bs).astype(bf16)


def _prep_win(w_in, C):
    D = w_in.shape[0]
    tr = _tile(GEN_ROWS, D)
    return pl.pallas_call(
        functools.partial(_prep_win_kernel, C=C),
        grid=(D // tr,),
        in_specs=[pl.BlockSpec((tr, 4 * C), lambda i: (i, 0))],
        out_specs=pl.BlockSpec((tr, 5 * C), lambda i: (i, 0)),
        out_shape=jax.ShapeDtypeStruct((D, 5 * C), bf16),
        compiler_params=_cparams(("parallel",)),
        name="prep_win",
    )(w_in)


def _dot_split(a, b):
    a_hi = a.astype(bf16)
    b_hi = b.astype(bf16)
    a_lo = (a - a_hi.astype(f32)).astype(bf16)
    b_lo = (b - b_hi.astype(f32)).astype(bf16)
    return (jnp.dot(a_hi, b_hi, preferred_element_type=f32) + jnp.dot(a_lo, b_hi, preferred_element_type=f32)
            + jnp.dot(a_hi, b_lo, preferred_element_type=f32))


def _write_filter_rows(win_ref, bin_ref, wmid_ref, bmid_ref, freq_ref, wout_ref, h_ref, asum_ref, step, L, C):
    tr = h_ref.shape[1]
    lane = lax.broadcasted_iota(jnp.int32, (tr, LANES), 1)
    row = lax.broadcasted_iota(jnp.int32, (tr, LANES), 0) + step * tr
    band_i = jnp.where(lane <= FILTER_BANDS, lane - 1, lane - 1 - FILTER_BANDS)
    band = 1e-4 + band_i.astype(f32) * ((FILTER_BANDS - 1 - 1e-4) / (FILTER_BANDS - 1))
    ch = lax.broadcasted_iota(jnp.int32, (tr, 2 * C), 1) % C
    max_decay = math.log(DECAY_TARGET) / DECAY_FAST_PCT
    min_decay = math.log(DECAY_TARGET) / DECAY_SLOW_PCT
    delta = jnp.abs(min_decay + ch.astype(f32) * ((max_decay - min_decay) / (C - 1)))
    is_bwd = lax.broadcasted_iota(jnp.int32, (tr, 2 * C), 1) >= C
    row_c = lax.broadcasted_iota(jnp.int32, (tr, 2 * C), 0) + step * tr

    @pl.when(step == 0)
    def _():
        asum_ref[...] = jnp.zeros_like(asum_ref)

    phase = jnp.where(lane <= FILTER_BANDS, 0.5 * math.pi, math.pi)
    pre = []
    for p in range(2):
        pos = (2 * row + p).astype(f32)
        t = pos * (1.0 / (L - 1))
        w = pos * (2.0 * math.pi / L)
        z = jnp.where(lane == 0, t, jnp.where(lane <= 2 * FILTER_BANDS, jnp.sin(band * w + phase), 0.0))
        pre.append(_dot_split(z, win_ref[...]))
    twice = lambda v: jnp.concatenate([v, v], axis=1)
    both = lambda m: jnp.concatenate([jnp.concatenate([m, jnp.zeros_like(m)], axis=1),
                                      jnp.concatenate([jnp.zeros_like(m), m], axis=1)], axis=0)
    h = jnp.sin(twice(freq_ref[0:1, :]) * (jnp.concatenate(pre, axis=1) + twice(bin_ref[...])))
    for i in range(wmid_ref.shape[0]):
        h = jnp.sin(twice(freq_ref[i + 1:i + 2, :]) * (_dot_split(h, both(wmid_ref[i])) + twice(bmid_ref[i:i + 1, :])))
    outs = _dot_split(h, both(wout_ref[...]))
    total = jnp.zeros((1, 2 * C), f32)
    for p in range(2):
        out = outs[:, p * 2 * C:(p + 1) * 2 * C]
        tc = (2 * row_c + p).astype(f32) * (1.0 / (L - 1))
        out = out * jnp.exp(-tc * delta)
        if p == 0:
            out = jnp.where(is_bwd & (row_c == 0), 0.0, out)
        fwd, bwd = out[:, :C], out[:, C:]
        h_ref[p] = jnp.concatenate([fwd + bwd, fwd - bwd], axis=1).astype(bf16)
        total = total + jnp.sum(jnp.abs(out), axis=0, keepdims=True)
    asum_ref[0:1, :] += total


def _filter_kernel(win_ref, bin_ref, wmid_ref, bmid_ref, freq_ref, wout_ref, h_ref, asum_ref, *, L, C):
    _write_filter_rows(win_ref, bin_ref, wmid_ref, bmid_ref, freq_ref, wout_ref, h_ref, asum_ref,
                       pl.program_id(0), L, C)


def _hyena_filters(L, C, f_w_in, f_b_in, f_w_mid, f_b_mid, f_freq, f_w_out):
    Lh = L // 2
    tr = _tile(GEN_ROWS, Lh)
    order = f_w_in.shape[1]
    win = jnp.pad(f_w_in, ((0, LANES - f_w_in.shape[0]), (0, 0)))
    full = lambda a: pl.BlockSpec(a.shape, lambda i: (0,) * a.ndim)
    args = (win, f_b_in.reshape(1, order), f_w_mid, f_b_mid, f_freq, f_w_out)
    return pl.pallas_call(
        functools.partial(_filter_kernel, L=L, C=C),
        grid=(Lh // tr,),
        in_specs=[full(a) for a in args],
        out_specs=[pl.BlockSpec((2, tr, 2 * C), lambda i: (0, i, 0)),
                   pl.BlockSpec((SUBLANES, 2 * C), lambda i: (0, 0))],
        out_shape=[jax.ShapeDtypeStruct((2, Lh, 2 * C), bf16),
                   jax.ShapeDtypeStruct((SUBLANES, 2 * C), f32)],
        compiler_params=_cparams(("arbitrary",)),
        name="hyena_filters",
    )(*args)


def _fwd_products(c_ref, x_ref, rows=slice(None)):
    k = c_ref.shape[2]
    even = c_ref[0:2, rows, :]
    odd = c_ref[2:4, rows, :]
    tm = even.shape[1]
    even = jnp.dot(even.reshape(2 * tm, k), x_ref[0], preferred_element_type=f32)
    odd = jnp.dot(odd.reshape(2 * tm, k), x_ref[1], preferred_element_type=f32)
    return even[:tm], even[tm:], odd[:tm], odd[tm:]


def _filter_spectrum_kernel(c_ref, h_ref, asum_ref, k_ref, k0_ref, *, L, C):
    ce = jnp.dot(c_ref[0], h_ref[0, :, :C], preferred_element_type=f32)
    se = jnp.dot(c_ref[1], h_ref[0, :, C:], preferred_element_type=f32)
    co = jnp.dot(c_ref[2], h_ref[1, :, :C], preferred_element_type=f32)
    so = jnp.dot(c_ref[3], h_ref[1, :, C:], preferred_element_type=f32)
    a = asum_ref[0:1, :]
    scale = 1.0 / ((a[:, :C] + a[:, C:]) * L)
    k_ref[0] = (ce + co) * scale
    k_ref[1] = -(se + so) * scale
    k_ref[2] = (ce - co) * scale
    k_ref[3] = (se - so) * scale

    @pl.when(pl.program_id(0) == 0)
    def _():
        rows = slice(0, 2 * SUBLANES)
        k0_ref[...] = jnp.zeros_like(k0_ref)
        k0_ref[0:1, :] = (jnp.dot(c_ref[1, rows, :], h_ref[0, :, :C], preferred_element_type=f32) * scale)[0:1, :]
        k0_ref[1:2, :] = (so * scale)[0:1, :]


def _filter_spectrum(cfwd, hcat, asum, L, C):
    Lh = L // 2
    tm = _tile(CONV_ROWS, Lh)
    return pl.pallas_call(
        functools.partial(_filter_spectrum_kernel, L=L, C=C),
        grid=(Lh // tm,),
        in_specs=[pl.BlockSpec((4, tm, Lh), lambda i: (0, i, 0)),
                  pl.BlockSpec((2, Lh, 2 * C), lambda i: (0, 0, 0)),
                  pl.BlockSpec((SUBLANES, 2 * C), lambda i: (0, 0))],
        out_specs=[pl.BlockSpec((4, tm, C), lambda i: (0, i, 0)),
                   pl.BlockSpec((SUBLANES, C), lambda i: (0, 0))],
        out_shape=[jax.ShapeDtypeStruct((4, Lh, C), f32),
                   jax.ShapeDtypeStruct((SUBLANES, C), f32)],
        compiler_params=_cparams(("arbitrary",)),
        name="filter_spectrum",
    )(cfwd, hcat, asum)


def _conv_fwd_kernel(c_ref, z_ref, k_ref, k0_ref, s_ref):
    C = k_ref.shape[2]
    tm = c_ref.shape[1]
    tg = tm // CONV_ROW_GROUPS
    groups = [slice(r * tg, (r + 1) * tg) for r in range(CONV_ROW_GROUPS)]
    products = [_fwd_products(c_ref, z_ref, rows) for rows in groups]
    is_row0 = (lax.broadcasted_iota(jnp.int32, (SUBLANES, C), 0) == 0) & (pl.program_id(1) == 0)
    for rows, prod in zip(groups, products):
        for h in range(z_ref.shape[2] // C):
            cols = slice(h * C, (h + 1) * C)
            ce, se, co, so = (v[:, cols] for v in prod)
            zsr, zsi, zdr, zdi = ce + co, -(se + so), ce - co, se - so
            ksr, ksi, kdr, kdi = (k_ref[i, rows, :] for i in range(4))
            psr = zsr * ksr - zsi * ksi
            psi = zsr * ksi + zsi * ksr
            pdr = zdr * kdr - zdi * kdi
            pdi = zdr * kdi + zdi * kdr
            gen = [psr + pdr, psi - pdi, psr - pdr, psi + pdi]
            if rows.start == 0:
                top = slice(0, SUBLANES)
                p0 = zsr[top] * ksr[top]
                pl_ = zdr[top] * kdr[top]
                ak, bk = k0_ref[0:1, :], k0_ref[1:2, :]
                phr = se[top] * ak - so[top] * bk
                phi = -(se[top] * bk + so[top] * ak)
                spec = (0.5 * (p0 + pl_), -phr, 0.5 * (p0 - pl_), phi)
                gen = [jnp.concatenate([jnp.where(is_row0, spec[i], gen[i][top]), gen[i][SUBLANES:]], axis=0)
                       for i in range(4)]
            for i in range(4):
                s_ref[i, rows, cols] = gen[i].astype(bf16)


def _conv_fwd(cfwd, z, kspec, k0, C):
    _, Lh, N = z.shape
    tm = _tile(CONV_ROWS, Lh)
    tn = _tile(CONV_COLS, N)
    return pl.pallas_call(
        _conv_fwd_kernel,
        grid=(N // tn, Lh // tm),
        in_specs=[pl.BlockSpec((4, tm, Lh), lambda n, i: (0, i, 0)),
                  pl.BlockSpec((2, Lh, tn), lambda n, i: (0, 0, n)),
                  pl.BlockSpec((4, tm, C), lambda n, i: (0, i, 0)),
                  pl.BlockSpec((SUBLANES, C), lambda n, i: (0, 0))],
        out_specs=pl.BlockSpec((4, tm, tn), lambda n, i: (0, i, n)),
        out_shape=jax.ShapeDtypeStruct((4, Lh, N), bf16),
        compiler_params=_cparams(("parallel", "arbitrary")),
        name="conv_fwd",
    )(cfwd, z, kspec, k0)


def _pair_products(c_ref, r_ref):
    k, tn = r_ref.shape[1:]
    ev = jnp.dot(c_ref[0], r_ref[0:2].reshape(2 * k, tn), preferred_element_type=f32)
    od = jnp.dot(c_ref[1], r_ref[2:4].reshape(2 * k, tn), preferred_element_type=f32)
    return ev, od


def _conv_inv_kernel(c_ref, s_ref, x0_ref, z_ref, fb_ref, y_ref):
    conv = _pair_products(c_ref, s_ref)
    C = fb_ref.shape[1]
    fb = fb_ref[...]
    for p in range(2):
        for h in range(z_ref.shape[2] // C):
            cols = slice(h * C, (h + 1) * C)
            zp = z_ref[p, :, cols].astype(f32)
            y_ref[p, :, cols] = (x0_ref[p, :, cols].astype(f32) * (conv[p][:, cols] + zp * fb)).astype(bf16)


def _conv_inv(cinv, s, x0, z, f_bias, C):
    _, Lh, N = s.shape
    tm = _tile(CONV_ROWS, Lh)
    tn = _tile(CONV_COLS, N)
    return pl.pallas_call(
        _conv_inv_kernel,
        grid=(N // tn, Lh // tm),
        in_specs=[pl.BlockSpec((2, tm, 2 * Lh), lambda n, i: (0, i, 0)),
                  pl.BlockSpec((4, Lh, tn), lambda n, i: (0, 0, n)),
                  pl.BlockSpec((2, tm, tn), lambda n, i: (0, i, n)),
                  pl.BlockSpec((2, tm, tn), lambda n, i: (0, i, n)),
                  pl.BlockSpec((1, C), lambda n, i: (0, 0))],
        out_specs=pl.BlockSpec((2, tm, tn), lambda n, i: (0, i, n)),
        out_shape=jax.ShapeDtypeStruct((2, Lh, N), bf16),
        compiler_params=_cparams(("parallel", "arbitrary")),
        name="conv_inv",
    )(cinv, s, x0, z, f_bias.reshape(1, C))


def _fourier_kernel(c_ref, u_ref, y_ref, rhs, stage, tw1, tw2, *, L, scale):
    Lh, n, H = L // 2, L // 4, L // 8
    tn = u_ref.shape[2]

    @pl.when(pl.program_id(0) == 0)
    def _():
        r1 = lax.broadcasted_iota(jnp.int32, (Lh, LANES), 0)
        tw1[0], tw1[1] = _cos_sin_of_turn_fraction(2 * (r1 % n) + r1 // n, L)
        r2 = lax.broadcasted_iota(jnp.int32, (n, LANES), 0)
        tw2[0], tw2[1] = _cos_sin_of_turn_fraction(2 * (r2 % H) + r2 // H, Lh)

    def rotate(a, b, c, s):
        return c * a - s * b, s * a + c * b

    for q in range(tn // LANES):
        cols = slice(q * LANES, (q + 1) * LANES)
        for p in range(2):
            first = slice(p * n, p * n + H)
            second = slice(p * n + H, (p + 1) * n)
            out = slice(p * H, (p + 1) * H)
            out_b = slice(n + p * H, n + (p + 1) * H)
            c2, s2 = tw2[0, out, :], tw2[1, out, :]

            def emit(sub, a0, b0, a1, b1):
                rhs[sub, out, cols] = (a0 + a1).astype(bf16)
                rhs[sub, out_b, cols] = (b0 + b1).astype(bf16)
                da, db = rotate(a0 - a1, b0 - b1, c2, s2)
                rhs[sub + 1, out, cols] = da.astype(bf16)
                rhs[sub + 1, out_b, cols] = db.astype(bf16)

            ld = lambda k, rows: u_ref[k, rows, cols].astype(f32)
            emit(0, ld(0, first), ld(1, first), ld(0, second), ld(1, second))
            odd = [rotate(ld(2, rows), ld(3, rows), tw1[0, rows, :], tw1[1, rows, :]) for rows in (first, second)]
            emit(2, odd[0][0], odd[0][1], odd[1][0], odd[1][1])

    for parity in range(2):
        pair = [jnp.dot(c_ref[...], rhs[2 * parity + h], preferred_element_type=f32) * scale for h in range(2)]
        for q in range(tn // LANES):
            cols = slice(q * LANES, (q + 1) * LANES)
            for h in range(2):
                stage[q, pl.ds(h, n, stride=2), :] = pair[h][:, cols]
            y_ref[parity, :, cols] = stage[q].astype(bf16)


def _fourier_dft(cfou, u4, L, C):
    _, Lh, N = u4.shape
    n = L // 4
    tn = _tile(FOURIER_COLS, C)
    return pl.pallas_call(
        functools.partial(_fourier_kernel, L=L, scale=1.0 / math.sqrt(L * HEAD_DIM)),
        grid=(N // tn,),
        in_specs=[pl.BlockSpec((n, 2 * n), lambda j: (0, 0)),
                  pl.BlockSpec((4, Lh, tn), lambda j: (0, 0, j))],
        out_specs=pl.BlockSpec((2, Lh, tn), lambda j: (0, 0, j)),
        out_shape=jax.ShapeDtypeStruct((2, Lh, N), bf16),
        scratch_shapes=[pltpu.VMEM((4, 2 * n, tn), bf16), pltpu.VMEM((tn // LANES, Lh, LANES), f32),
                        pltpu.VMEM((2, Lh, LANES), f32), pltpu.VMEM((2, n, LANES), f32)],
        compiler_params=_cparams(("arbitrary",)),
        name="fourier_dft",
    )(cfou, u4)


def _rms(x, g):
    return x * lax.rsqrt(jnp.mean(x * x, axis=-1, keepdims=True) + EPS) * g


def _stage_rows(chunks, x):
    for k in range(chunks.shape[0]):
        chunks[k] = x[:, k * LANES:(k + 1) * LANES]


def _parity_rows(chunks, p, n):
    return jnp.concatenate([chunks[k, pl.ds(p, n, stride=2), :] for k in range(chunks.shape[0])], axis=1)


def _inproj_kernel(xa_ref, xb_ref, pa_ref, na_ref, pb_ref, nb_ref, g_ref, w_ref, cw_ref, cb_ref,
                   z_ref, x0_ref, u_ref, cfwd_ref, cinv_ref, stage, *, C, D, L):
    tj = xa_ref.shape[1] // 2
    jt = pl.program_id(1)
    n_half = pl.num_programs(1)
    _write_dft_rows(cfwd_ref, cinv_ref, pl.program_id(0) * n_half + jt, L)
    g = g_ref[...]
    w = w_ref[...]
    cw0, cw1, cw2 = cw_ref[0:1, :], cw_ref[1:2, :], cw_ref[2:3, :]
    cb = cb_ref[...]
    row = lax.broadcasted_iota(jnp.int32, (tj, 3 * C), 0)
    four = []
    halves = ((xa_ref, pa_ref, na_ref, jt == 0, False),
              (xb_ref, pb_ref, nb_ref, False, jt == n_half - 1))
    for h, (x_ref, p_ref, n_ref, at_start, at_end) in enumerate(halves):
        _stage_rows(stage, x_ref[0])
        lhs = jnp.concatenate([
            _rms(_parity_rows(stage, 0, tj), g).astype(bf16),
            _rms(_parity_rows(stage, 1, tj), g).astype(bf16),
            _rms(p_ref[0], g).astype(bf16),
            _rms(n_ref[0], g).astype(bf16)], axis=0)
        res = jnp.dot(lhs, w, preferred_element_type=f32)
        pe, po = res[:tj, :3 * C], res[tj:2 * tj, :3 * C]
        prev = res[2 * tj + SUBLANES - 1:2 * tj + SUBLANES, :3 * C]
        nxt = res[2 * tj + SUBLANES:2 * tj + SUBLANES + 1, :3 * C]
        if at_start is not False:
            prev = jnp.where(at_start, 0.0, prev)
        if at_end is not False:
            nxt = jnp.where(at_end, 0.0, nxt)
        po_dn = jnp.where(row == 0, prev, pltpu.roll(po, 1, 0))
        pe_up = jnp.where(row == tj - 1, nxt, pltpu.roll(pe, tj - 1, 0))
        uc = (cb + cw0 * po_dn + cw1 * pe + cw2 * po,
              cb + cw0 * pe + cw1 * po + cw2 * pe_up)
        for p in range(2):
            x0_ref[p, h] = uc[p][:, :C].astype(bf16)
            z_ref[p, h] = (uc[p][:, C:2 * C] * uc[p][:, 2 * C:]).astype(bf16)
        four.append((res[:tj, 3 * C:], res[tj:2 * tj, 3 * C:]))
    for p in range(2):
        fa, fb = four[0][p], four[1][p]
        u_ref[0, p] = (fa[:, :C] + fb[:, :C]).astype(bf16)
        u_ref[1, p] = (fa[:, C:] + fb[:, C:]).astype(bf16)
        u_ref[2, p] = (fa[:, :C] - fb[:, :C]).astype(bf16)
        u_ref[3, p] = (fa[:, C:] - fb[:, C:]).astype(bf16)


def _inproj(x, norm_g, wcat, conv_w, conv_b, C):
    B, L, D = x.shape
    Lq = L // 4
    tj = _tile(TOKEN_TILE, Lq)
    nq = Lq // tj
    N = B * C
    hb = 2 * tj // SUBLANES
    last = L // SUBLANES - 1
    main = lambda off: pl.BlockSpec((1, 2 * tj, D), lambda b, j: (b, j + off, 0))
    prev = lambda off: pl.BlockSpec((1, SUBLANES, D), lambda b, j: (b, jnp.maximum((j + off) * hb - 1, 0), 0))
    nxt = lambda off: pl.BlockSpec((1, SUBLANES, D), lambda b, j: (b, jnp.minimum((j + off + 1) * hb, last), 0))
    full = lambda a: pl.BlockSpec(a.shape, lambda b, j: (0,) * a.ndim)
    g = norm_g.reshape(1, D)
    cb = conv_b.reshape(1, 3 * C)
    Lh = L // 2
    dft_rows = Lh // (B * nq)
    assert dft_rows * B * nq == Lh and dft_rows % (2 * SUBLANES) == 0
    dft_spec = pl.BlockSpec((4, dft_rows, Lh), lambda b, j: (0, b * nq + j, 0))
    dft_shape = jax.ShapeDtypeStruct((4, Lh, Lh), bf16)
    inv_spec = pl.BlockSpec((2, dft_rows, 2 * Lh), lambda b, j: (0, b * nq + j, 0))
    inv_shape = jax.ShapeDtypeStruct((2, Lh, 2 * Lh), bf16)
    return pl.pallas_call(
        functools.partial(_inproj_kernel, C=C, D=D, L=L),
        grid=(B, nq),
        in_specs=[main(0), main(nq), prev(0), nxt(0), prev(nq), nxt(nq),
                  full(g), full(wcat), full(conv_w), full(cb)],
        out_specs=[pl.BlockSpec((2, 2, tj, C), lambda b, j: (0, 0, j, b)),
                   pl.BlockSpec((2, 2, tj, C), lambda b, j: (0, 0, j, b)),
                   pl.BlockSpec((4, 2, tj, C), lambda b, j: (0, 0, j, b)),
                   dft_spec, inv_spec],
        out_shape=[jax.ShapeDtypeStruct((2, 2, Lq, N), bf16),
                   jax.ShapeDtypeStruct((2, 2, Lq, N), bf16),
                   jax.ShapeDtypeStruct((4, 2, Lq, N), bf16),
                   dft_shape, inv_shape],
        scratch_shapes=[pltpu.VMEM((D // LANES, 2 * tj, LANES), f32)],
        compiler_params=_cparams(("parallel", "arbitrary")),
        name="inproj",
    )(x, x, x, x, x, x, g, wcat, conv_w, cb)


def _pack_bf16_pairs(v, is_bf16_exact=False):
    k = v.shape[1] // 2
    rounded = v if is_bf16_exact else v.astype(bf16).astype(f32)
    hi = pltpu.bitcast(rounded[:, :k], jnp.uint32)
    lo = pltpu.bitcast(rounded[:, k:], jnp.uint32)
    return hi | (lo >> 16)


def _unpack_bf16_pairs(u):
    hi = pltpu.bitcast(u & jnp.uint32(0xFFFF0000), f32)
    lo = pltpu.bitcast(u << 16, f32)
    return jnp.concatenate([hi, lo], axis=1)


def _row_tiles_shape(n, D):
    return (n, D // (2 * LANES), LANES)


def _store_row_tiles(ref, packed, row0=0):
    total, per_row, _ = ref.shape
    n = packed.shape[0]
    flat = ref.reshape(total * per_row, LANES)
    for s in range(per_row):
        flat[pl.ds(row0 * per_row + s, n, stride=per_row), :] = packed[:, s * LANES:(s + 1) * LANES]


def _load_row_tiles(ref, row0=0, n=None):
    total, per_row, _ = ref.shape
    n = total if n is None else n
    flat = ref.reshape(total * per_row, LANES)
    return jnp.concatenate([flat[pl.ds(row0 * per_row + s, n, stride=per_row), :] for s in range(per_row)],
                           axis=1)


def _outproj_router_kernel(yh_ref, yf_ref, x_ref, mg_ref, wo_ref, g2_ref, wr_ref, br_ref,
                           h2_ref, tok_ref, pos_ref, vec_ref, tinfo_ref, cnt_ref, carry_ref, stage, *, C):
    n_sub = pos_ref.shape[0]
    tj = pos_ref.shape[2]
    _stage_rows(stage, x_ref[0])

    @pl.when((pl.program_id(0) == 0) & (pl.program_id(1) == 0))
    def _():
        carry_ref[...] = jnp.zeros_like(carry_ref)

    li = lax.broadcasted_iota(jnp.int32, (LANES, LANES), 0)
    lj = lax.broadcasted_iota(jnp.int32, (LANES, LANES), 1)
    hw = _tile(MXU_DIM, C)
    ai = lax.broadcasted_iota(jnp.int32, (hw, hw), 0)
    aj = lax.broadcasted_iota(jnp.int32, (hw, hw), 1)
    avg = jnp.where(ai // HEAD_DIM == aj // HEAD_DIM, 1.0 / HEAD_DIM, 0.0).astype(bf16)
    upper = (li < lj).astype(bf16)
    ri = lax.broadcasted_iota(jnp.int32, (tj, tj), 0)
    ci = lax.broadcasted_iota(jnp.int32, (tj, tj), 1)
    tri = (ci < ri).astype(bf16)
    wr = wr_ref[...]
    wr_hi = wr.astype(bf16)
    wr_lo = (wr - wr_hi.astype(f32)).astype(bf16)
    wr_both = jnp.concatenate([wr_hi, wr_lo], axis=1)
    lane = lax.broadcasted_iota(jnp.int32, (n_sub * tj, LANES), 1).astype(f32)
    sub = lax.broadcasted_iota(jnp.int32, (SUBLANES, LANES), 0)
    neg = jnp.float32(-jnp.inf)
    big = jnp.float32(LANES)

    def first_argmax(vals, mask):
        v = jnp.where(mask, vals, neg)
        m = jnp.max(v, axis=1, keepdims=True)
        idx = jnp.min(jnp.where(mask & (v == m), lane, big), axis=1, keepdims=True)
        return m, idx

    normed = []
    for s in range(n_sub):
        r, p = divmod(s, 2)
        parts = []
        for i, ref in enumerate((yh_ref, yf_ref)):
            y = ref[p, r * tj:(r + 1) * tj, :]
            ysq = y * y
            ms = jnp.concatenate([jnp.dot(ysq[:, k:k + hw], avg, preferred_element_type=f32)
                                  for k in range(0, C, hw)], axis=1)
            scale = lax.rsqrt(ms + EPS) * mg_ref[:, i * C:(i + 1) * C]
            parts.append(y * scale.astype(bf16))
        normed.append(jnp.concatenate(parts, axis=1))
    proj = jnp.dot(jnp.concatenate(normed, axis=0), wo_ref[...], preferred_element_type=f32)
    t_hi, t_lo = [], []
    for s in range(n_sub):
        r, p = divmod(s, 2)
        rows = slice(s * tj, (s + 1) * tj)
        h2 = _parity_rows(stage, r * 2 * tj + p, tj) + proj[rows]
        h2_ref[rows, :] = h2
        tok = _rms(h2, g2_ref[...])
        hi = tok.astype(bf16)
        tok_ref[rows, :] = hi
        t_hi.append(hi)
        t_lo.append((tok - hi.astype(f32)).astype(bf16))
    hi_terms = jnp.dot(jnp.concatenate(t_hi, axis=0), wr_both, preferred_element_type=f32)
    logits = (hi_terms[:, :LANES] + hi_terms[:, LANES:]
              + jnp.dot(jnp.concatenate(t_lo, axis=0), wr_hi, preferred_element_type=f32) + br_ref[...])

    gmask = (lane >= N_EXPERTS) & (lane < N_EXPERTS + N_GROUPS)
    gmax, gidx = first_argmax(logits, gmask)
    gsel = gidx - N_EXPERTS
    p_group = 1.0 / jnp.sum(jnp.where(gmask, jnp.exp(logits - gmax), 0.0), axis=1, keepdims=True)
    emask = (lane >= gsel * EXPERTS_PER_GROUP) & (lane < (gsel + 1) * EXPERTS_PER_GROUP)
    l0, e0 = first_argmax(logits, emask)
    l1, e1 = first_argmax(logits, emask & (lane != e0))
    ex = jnp.exp(l1 - l0)
    w0 = p_group / (1.0 + ex)
    w1 = p_group * ex / (1.0 + ex)

    oh0 = (lane == e0).astype(f32)
    oh1 = (lane == e1).astype(f32)
    both = (oh0 + oh1).astype(bf16)
    where, counts = [], []
    for s in range(n_sub):
        tile_both = both[s * tj:(s + 1) * tj, :]
        ranks = jnp.dot(tri, tile_both, preferred_element_type=f32)
        n_e = ranks[tj - 1:tj, :] + tile_both[tj - 1:tj, :].astype(f32)
        n_e8 = jnp.broadcast_to(n_e, (SUBLANES, LANES))
        tile_off = jnp.dot(n_e8.astype(bf16), upper, preferred_element_type=f32)[0:1, :]
        where.append(ranks + tile_off)
        counts.append(n_e8)
    where = jnp.concatenate(where, axis=0)
    pos0 = jnp.sum(oh0 * where, axis=1, keepdims=True)
    pos1 = jnp.sum(oh1 * where, axis=1, keepdims=True)
    cols = jnp.where(lane == 0, w0, jnp.where(lane == 1, w1,
                     jnp.where(lane == 2, pos0, jnp.where(lane == 3, pos1, 0.0))))
    vec_ref[...] = cols
    base = carry_ref[...]
    for s in range(n_sub):
        pos_ref[s] = cols[s * tj:(s + 1) * tj, :].T[:SUBLANES, :].astype(jnp.int32)
        long_seg = (jnp.max(counts[s], axis=1, keepdims=True) >= SEG_SPLIT).astype(f32)
        tinfo_ref[s] = jnp.where(sub == 0, counts[s], jnp.where(sub == 1, base, jnp.where(sub == 2, long_seg, 0.0))
                                 ).astype(jnp.int32)
        base = base + counts[s]
    carry_ref[...] = base
    cnt_ref[...] = base.astype(jnp.int32)


def _outproj_router(yh, yf, x, mix_g, wo, norm2_g, wr, br, C):
    B, L, D = x.shape
    Lh = L // 2
    tj = _tile(TOKEN_TILE, Lh)
    rt = ROW_TILES_PER_STEP
    ns = 2 * rt
    nj = Lh // (rt * tj)
    T = B * L
    ntiles = T // tj
    lin = lambda b, j: b * nj + j
    full = lambda a: pl.BlockSpec(a.shape, lambda b, j: (0,) * a.ndim)
    mg = mix_g.reshape(1, D)
    g2 = norm2_g.reshape(1, D)
    return pl.pallas_call(
        functools.partial(_outproj_router_kernel, C=C),
        grid=(B, nj),
        in_specs=[pl.BlockSpec((2, rt * tj, C), lambda b, j: (0, j, b)),
                  pl.BlockSpec((2, rt * tj, C), lambda b, j: (0, j, b)),
                  pl.BlockSpec((1, ns * tj, D), lambda b, j: (b, j, 0)),
                  full(mg), full(wo), full(g2), full(wr), full(br)],
        out_specs=[pl.BlockSpec((ns * tj, D), lambda b, j: (lin(b, j), 0)),
                   pl.BlockSpec((ns * tj, D), lambda b, j: (lin(b, j), 0)),
                   pl.BlockSpec((ns, SUBLANES, tj), lambda b, j: (lin(b, j), 0, 0)),
                   pl.BlockSpec((ns * tj, LANES), lambda b, j: (lin(b, j), 0)),
                   pl.BlockSpec((ns, SUBLANES, LANES), lambda b, j: (lin(b, j), 0, 0)),
                   pl.BlockSpec((SUBLANES, LANES), lambda b, j: (0, 0))],
        out_shape=[jax.ShapeDtypeStruct((T, D), f32),
                   jax.ShapeDtypeStruct((T, D), bf16),
                   jax.ShapeDtypeStruct((ntiles, SUBLANES, tj), jnp.int32),
                   jax.ShapeDtypeStruct((T, LANES), f32),
                   jax.ShapeDtypeStruct((ntiles, SUBLANES, LANES), jnp.int32),
                   jax.ShapeDtypeStruct((SUBLANES, LANES), jnp.int32)],
        scratch_shapes=[pltpu.VMEM((SUBLANES, LANES), f32), pltpu.VMEM((D // LANES, ns * tj, LANES), f32)],
        compiler_params=_cparams(("arbitrary", "arbitrary")),
        name="outproj_router",
    )(yh, yf, x, mg, wo, g2, wr, br)


def _padded(c):
    return ((c + MOE_ROWS - 1) // MOE_ROWS) * MOE_ROWS


def _pad_starts(cnt_ref, start_ref):
    def body(e, acc):
        start_ref[e] = acc
        return acc + _padded(cnt_ref[0, e])
    return lax.fori_loop(0, N_EXPERTS, body, jnp.int32(0))


def _segment_copies(n, src, src_off, dst, dst_off, sem, max_rows, wait, src_advances=True, min_rows=1):
    bit = max_rows
    while bit >= min_rows:
        take = n & bit

        @pl.when(take != 0)
        def _(src_off=src_off, dst_off=dst_off, bit=bit):
            cp = pltpu.make_async_copy(src.at[pl.ds(src_off, bit)], dst.at[pl.ds(dst_off, bit)], sem)
            if wait:
                cp.wait()
            else:
                cp.start(priority=ROW_DMA_PRIORITY)
        if src_advances:
            src_off = src_off + take
        dst_off = dst_off + take
        bit //= 2


def _tile_segments(tinfo_ref, t, start_ref, local, local_is_src, remote, sem, tj, part="both"):
    def copies(long_part):
        off = 0
        for e in range(N_EXPERTS):
            n = tinfo_ref[t, 0, e]
            far = start_ref[e] + tinfo_ref[t, 1, e]
            if long_part:
                args = (n, off, far, tj, SEG_SPLIT)
            else:
                head = n & ~(SEG_SPLIT - 1)
                args = (n, off + head, far + head, SEG_SPLIT // 2, 1)
            n_, near_, far_, hi, lo = args
            if local_is_src:
                _segment_copies(n_, local, near_, remote, far_, sem, hi, wait=False, min_rows=lo)
            else:
                _segment_copies(n_, remote, far_, local, near_, sem, hi, wait=False, min_rows=lo)
            off = off + n

    if part in ("both", "short"):
        copies(long_part=False)
    if part in ("both", "long"):
        @pl.when(tinfo_ref[t, 2, 0] != 0)
        def _():
            copies(long_part=True)


def _dispatch_kernel(tinfo_ref, cnt_ref, tok_ref, pos_ref, xs_ref, first_ref, nblk_ref,
                     sbufs, zbuf, start_ref, sems, zsem):
    tj = pos_ref.shape[2]
    n_blocks = xs_ref.shape[0] // MOE_ROWS

    @pl.when(pl.program_id(0) == 0)
    def _():
        total = _pad_starts(cnt_ref, start_ref)
        used = total // MOE_ROWS

        def block_ranges(e, c):
            first_ref[e] = start_ref[e] // MOE_ROWS
            nblk_ref[e] = _padded(cnt_ref[0, e]) // MOE_ROWS
            return c
        lax.fori_loop(0, N_EXPERTS, block_ranges, 0)

        zbuf[...] = jnp.zeros_like(zbuf)
        zrows = zbuf.shape[0]
        for wait in (False, True):
            def zero_pad(e, c, wait=wait):
                cnt = cnt_ref[0, e]
                _segment_copies(_padded(cnt) - cnt, zbuf, 0, xs_ref, start_ref[e] + cnt, zsem,
                                zrows, wait, src_advances=False)
                return c
            lax.fori_loop(0, N_EXPERTS, zero_pad, 0)

            def zero_tail(i, c, wait=wait):
                cp = pltpu.make_async_copy(zbuf, xs_ref.at[pl.ds(total + i * zrows, zrows)], zsem)
                if wait:
                    cp.wait()
                else:
                    cp.start()
                return c
            lax.fori_loop(0, (n_blocks - used) * (MOE_ROWS // zrows), zero_tail, 0)

    def wait_tile(k):
        pltpu.make_async_copy(sbufs[k], xs_ref.at[pl.ds(0, 2 * tj)], sems.at[k]).wait()

    def segments(k, part):
        _tile_segments(tinfo_ref, k, start_ref, sbufs[k], True, xs_ref, sems.at[k], tj, part)

    slot = lax.broadcasted_iota(jnp.int32, (2 * tj, tj), 0)
    last = TILES_PER_STEP - 1

    @pl.when(pl.program_id(0) > 0)
    def _():
        for k in range(last):
            wait_tile(k)
    for k in range(TILES_PER_STEP):
        if k == last:
            @pl.when(pl.program_id(0) > 0)
            def _():
                wait_tile(last)
        perm = ((slot == pos_ref[k, 2:3, :]) | (slot == pos_ref[k, 3:4, :])).astype(bf16)
        sorted_rows = jnp.dot(perm, tok_ref[k * tj:(k + 1) * tj, :], preferred_element_type=f32)
        _store_row_tiles(sbufs[k], _pack_bf16_pairs(sorted_rows, is_bf16_exact=True))
        if k > 0:
            segments(k - 1, "short")
    segments(last, "short")
    for k in range(TILES_PER_STEP):
        segments(k, "long")

    @pl.when(pl.program_id(0) == pl.num_programs(0) - 1)
    def _():
        for k in range(TILES_PER_STEP):
            wait_tile(k)


def _dispatch(tinfo, cnt, tok, pos, n_rows):
    ntiles, _, tj = pos.shape
    T, D = tok.shape
    n_blocks = n_rows // MOE_ROWS
    nt = TILES_PER_STEP
    assert ntiles % nt == 0
    sbuf = pltpu.VMEM(_row_tiles_shape(2 * tj, D), jnp.uint32)
    return pl.pallas_call(
        _dispatch_kernel,
        grid=(ntiles // nt,),
        in_specs=[pl.BlockSpec((nt, SUBLANES, LANES), lambda i: (i, 0, 0), memory_space=pltpu.SMEM),
                  pl.BlockSpec(memory_space=pltpu.SMEM),
                  pl.BlockSpec((nt * tj, D), lambda i: (i, 0)),
                  pl.BlockSpec((nt, SUBLANES, tj), lambda i: (i, 0, 0))],
        out_specs=[pl.BlockSpec(memory_space=pl.ANY),
                   pl.BlockSpec(memory_space=pltpu.SMEM),
                   pl.BlockSpec(memory_space=pltpu.SMEM)],
        out_shape=[jax.ShapeDtypeStruct(_row_tiles_shape(n_rows, D), jnp.uint32),
                   jax.ShapeDtypeStruct((N_EXPERTS,), jnp.int32),
                   jax.ShapeDtypeStruct((N_EXPERTS,), jnp.int32)],
        scratch_shapes=[[sbuf] * nt,
                        pltpu.VMEM(_row_tiles_shape(MOE_ROWS // 2, D), jnp.uint32),
                        pltpu.SMEM((N_EXPERTS,), jnp.int32), pltpu.SemaphoreType.DMA((nt,)),
                        pltpu.SemaphoreType.DMA],
        compiler_params=_cparams(("arbitrary",)),
        name="moe_dispatch",
    )(tinfo, cnt, tok, pos)


def _experts_kernel(first_ref, nblk_ref, xs_ref, wg_ref, wu_ref, wd_ref, ys_ref,
                    wg_s, wu_s, wd_s, xbuf, ybuf, in_sems, out_sems):
    e = pl.program_id(0)
    last_e = pl.num_programs(0) - 1
    rows = MOE_ROWS
    n_total = xs_ref.shape[0] // rows
    used = first_ref[last_e] + nblk_ref[last_e]

    def fetch(g, slot):
        return pltpu.make_async_copy(xs_ref.at[pl.ds(g * rows, rows)], xbuf.at[pl.ds(slot * rows, rows)],
                                     in_sems.at[slot])

    def flush(g, slot):
        return pltpu.make_async_copy(ybuf.at[pl.ds(slot * rows, rows)], ys_ref.at[pl.ds(g * rows, rows)],
                                     out_sems.at[slot])

    for ahead in range(FETCH_AHEAD):
        @pl.when((e == 0) & (used > ahead))
        def _(ahead=ahead):
            fetch(ahead, ahead).start(priority=ROW_DMA_PRIORITY)

    @pl.when(nblk_ref[e] > 0)
    def _():
        wg_s[...] = wg_ref[...].astype(bf16)
        wu_s[...] = wu_ref[...].astype(bf16)
        wd_s[...] = wd_ref[...].astype(bf16)

    def block(b, carry):
        g = first_ref[e] + b
        slot = g % 2
        in_slot = g % (FETCH_AHEAD + 1)
        fetch(g, in_slot).wait()

        @pl.when(g + FETCH_AHEAD < used)
        def _():
            fetch(g + FETCH_AHEAD, (g + FETCH_AHEAD) % (FETCH_AHEAD + 1)).start(priority=ROW_DMA_PRIORITY)

        @pl.when(g >= 2)
        def _():
            flush(g - 2, slot).wait()
        half = rows // EXPERT_ROW_GROUPS
        starts = [slot * rows + r * half for r in range(EXPERT_ROW_GROUPS)]
        gated = []
        for r in range(EXPERT_ROW_GROUPS):
            xb = _unpack_bf16_pairs(_load_row_tiles(xbuf, in_slot * rows + r * half, half)).astype(bf16)
            gated.append((jnp.dot(xb, wg_s[...], preferred_element_type=f32),
                          jnp.dot(xb, wu_s[...], preferred_element_type=f32)))
        for row0, (a, u) in zip(starts, gated):
            h = (a * jax.nn.sigmoid(a) * u).astype(bf16)
            _store_row_tiles(ybuf, _pack_bf16_pairs(jnp.dot(h, wd_s[...], preferred_element_type=f32)), row0)
        flush(g, slot).start(priority=ROW_DMA_PRIORITY)
        return carry
    lax.fori_loop(0, nblk_ref[e], block, 0)

    @pl.when(e == last_e)
    def _():
        for back in (1, 2):
            @pl.when(used >= back)
            def _(back=back):
                flush(used - back, (used - back) % 2).wait()
        ybuf[pl.ds(0, rows)] = jnp.zeros((rows,) + ybuf.shape[1:], ybuf.dtype)
        for wait in (False, True):
            def zero_tail(g, c, wait=wait):
                cp = flush(g, 0)
                if wait:
                    cp.wait()
                else:
                    cp.start()
                return c
            lax.fori_loop(used, n_total, zero_tail, 0)


def _experts(first_block, n_blocks, xs, w_gate, w_up, w_down):
    _, D, De = w_gate.shape
    widx = lambda e, first, nblk: (e, 0, 0)
    in_buf = pltpu.VMEM(_row_tiles_shape((FETCH_AHEAD + 1) * MOE_ROWS, D), jnp.uint32)
    out_buf = pltpu.VMEM(_row_tiles_shape(2 * MOE_ROWS, D), jnp.uint32)
    return pl.pallas_call(
        _experts_kernel,
        grid_spec=pltpu.PrefetchScalarGridSpec(
            num_scalar_prefetch=2,
            grid=(N_EXPERTS,),
            in_specs=[pl.BlockSpec(memory_space=pl.ANY),
                      pl.BlockSpec((None, D, De), widx),
                      pl.BlockSpec((None, D, De), widx),
                      pl.BlockSpec((None, De, D), widx)],
            out_specs=pl.BlockSpec(memory_space=pl.ANY),
            scratch_shapes=[pltpu.VMEM((D, De), bf16), pltpu.VMEM((D, De), bf16), pltpu.VMEM((De, D), bf16),
                            in_buf, out_buf, pltpu.SemaphoreType.DMA((FETCH_AHEAD + 1,)),
                            pltpu.SemaphoreType.DMA((2,))],
        ),
        out_shape=jax.ShapeDtypeStruct(xs.shape, jnp.uint32),
        compiler_params=_cparams(("arbitrary",)),
        name="moe_experts",
    )(first_block, n_blocks, xs, w_gate, w_up, w_down)


def _combine_kernel(tinfo_ref, tnext_ref, cnt_ref, h2_ref, vec_ref, g_ref, ys_ref, o_ref,
                    gbufs, stages, start_ref, sems):
    tj = vec_ref.shape[0] // TILES_PER_STEP
    step = pl.program_id(0)

    def fetch(info_ref, t, buf, part="both"):
        _tile_segments(info_ref, t, start_ref, gbufs[buf], False, ys_ref, sems.at[buf], tj, part)

    @pl.when(step == 0)
    def _():
        _pad_starts(cnt_ref, start_ref)
        for t in range(GATHER_AHEAD):
            fetch(tinfo_ref, t, t)

    slot = lax.broadcasted_iota(jnp.int32, (tj, 2 * tj), 1)

    def finish(k, picked):
        v = vec_ref[k * tj:(k + 1) * tj, :]
        y = h2_ref[k * tj:(k + 1) * tj, :] + v[:, 0:1] * picked[:tj] + v[:, 1:2] * picked[tj:]
        out = _rms(y, g_ref[...])
        stage = stages[k // 2]
        for q in range(stage.shape[0]):
            stage[q, pl.ds(k % 2, tj, stride=2), :] = out[:, q * LANES:(q + 1) * LANES]
        if k % 2 == 1:
            r0 = (k // 2) * 2 * tj
            for q in range(stage.shape[0]):
                o_ref[0, r0:r0 + 2 * tj, q * LANES:(q + 1) * LANES] = stage[q]

    picked = None
    for k in range(TILES_PER_STEP):
        pltpu.make_async_copy(ys_ref.at[pl.ds(0, 2 * tj)], gbufs[k], sems.at[k]).wait()
        ahead = k + GATHER_AHEAD
        if ahead < TILES_PER_STEP:
            later = (tinfo_ref, ahead, ahead)
        else:
            later = (tnext_ref, ahead - TILES_PER_STEP, ahead - TILES_PER_STEP)
        fetch(*later, part="short")
        rows = _unpack_bf16_pairs(_load_row_tiles(gbufs[k])).astype(bf16)
        v = vec_ref[k * tj:(k + 1) * tj, :]
        pick = jnp.concatenate([(slot == v[:, 2 + c:3 + c].astype(jnp.int32)).astype(bf16) for c in range(2)],
                               axis=0)
        now = jnp.dot(pick, rows, preferred_element_type=f32)
        if k > 0:
            finish(k - 1, picked)
        picked = now
        fetch(*later, part="long")
    finish(TILES_PER_STEP - 1, picked)


def _combine(tinfo, cnt, h2, vec, final_g, ys, B, L):
    T, D = h2.shape
    ntiles = tinfo.shape[0]
    tj = T // ntiles
    nt = TILES_PER_STEP
    assert GATHER_AHEAD < nt and nt % 2 == 0 and (L // tj) % nt == 0
    nsteps = ntiles // nt
    per_seq = L // (nt * tj)
    g = final_g.reshape(1, D)
    gbuf = pltpu.VMEM(_row_tiles_shape(2 * tj, D), jnp.uint32)
    return pl.pallas_call(
        _combine_kernel,
        grid=(nsteps,),
        in_specs=[pl.BlockSpec((nt, SUBLANES, LANES), lambda i: (i, 0, 0), memory_space=pltpu.SMEM),
                  pl.BlockSpec((nt, SUBLANES, LANES), lambda i: (i + 1, 0, 0), memory_space=pltpu.SMEM),
                  pl.BlockSpec(memory_space=pltpu.SMEM),
                  pl.BlockSpec((nt * tj, D), lambda i: (i, 0)),
                  pl.BlockSpec((nt * tj, LANES), lambda i: (i, 0)),
                  pl.BlockSpec((1, D), lambda i: (0, 0)),
                  pl.BlockSpec(memory_space=pl.ANY)],
        out_specs=pl.BlockSpec((1, nt * tj, D), lambda i: (i // per_seq, i % per_seq, 0)),
        out_shape=jax.ShapeDtypeStruct((B, L, D), f32),
        scratch_shapes=[[gbuf] * nt,
                        [pltpu.VMEM((D // LANES, 2 * tj, LANES), f32)] * (nt // 2),
                        pltpu.SMEM((N_EXPERTS,), jnp.int32), pltpu.SemaphoreType.DMA((nt,))],
        compiler_params=_cparams(("arbitrary",)),
        name="moe_combine",
    )(tinfo, jnp.pad(tinfo, ((0, nt), (0, 0), (0, 0))), cnt, h2, vec, g, ys)


def kernel(x, norm1_g, w_in, conv_w, conv_b, f_w_in, f_b_in, f_w_mid, f_b_mid, f_freq, f_w_out, f_bias,
           mix_g, w_out, norm2_g, w_group, b_group, w_router, b_router, w_gate, w_up, w_down, final_g):
    B, L, D = x.shape
    depth = norm1_g.shape[0]
    assert depth == 1, "the final RMSNorm is fused into the single layer's MoE combine"
    C = D // 2
    Lh = L // 2
    N = B * C
    T = B * L
    n_rows = T * 2 + N_EXPERTS * MOE_ROWS
    cfou = _fourier_consts(L)
    i = 0
    wcat = _prep_win(w_in[i], C)
    hcat, asum = _hyena_filters(L, C, f_w_in[i], f_b_in[i], f_w_mid[i], f_b_mid[i], f_freq[i], f_w_out[i])
    z, x0, u4, cfwd, cinv = _inproj(x, norm1_g[i], wcat, conv_w[i], conv_b[i], C)
    kspec, k0 = _filter_spectrum(cfwd, hcat, asum, L, C)
    z = z.reshape(2, Lh, N)
    x0 = x0.reshape(2, Lh, N)
    u4 = u4.reshape(4, Lh, N)
    s = _conv_fwd(cfwd, z, kspec, k0, C)
    yh = _conv_inv(cinv, s, x0, z, f_bias[i], C)
    yf = _fourier_dft(cfou, u4, L, C)
    wr = jnp.concatenate([jnp.transpose(w_router[i], (1, 0, 2)).reshape(D, N_EXPERTS), w_group[i]], axis=1)
    wr = jnp.pad(wr, ((0, 0), (0, LANES - wr.shape[1])))
    br = jnp.pad(jnp.concatenate([b_router[i].reshape(-1), b_group[i]]), (0, LANES - N_EXPERTS - N_GROUPS))
    h2, tok, pos, vec, tinfo, cnt = _outproj_router(yh, yf, x, mix_g[i], w_out[i].astype(bf16), norm2_g[i],
                                                     wr, br.reshape(1, LANES), C)
    xs, first_block, n_blocks = _dispatch(tinfo, cnt, tok, pos, n_rows)
    ys = _experts(first_block, n_blocks, xs, w_gate[i], w_up[i], w_down[i])
    return _combine(tinfo, cnt, h2, vec, final_g, ys, B, L)
```
